```python
import math
import jax, jax.numpy as jnp
from jax import lax
import numpy as np

D_MODEL = 2048
BATCH = 8
SEQ = 8192
DEPTH = 1

SSM_EXPAND = 2
D_SSM = SSM_EXPAND * D_MODEL
SSM_HEAD_DIM = 64
N_SSM_HEADS = D_SSM // SSM_HEAD_DIM
N_GROUPS = 8
HEADS_PER_GROUP = N_SSM_HEADS // N_GROUPS
D_STATE = 128
D_XBC = D_SSM + 2 * N_GROUPS * D_STATE
SSM_CONV = 5
CHUNK = 128
D_CONV = D_MODEL
CONV_WIDTH = 31
D_FF = -(-8 * D_MODEL // 768) * 256
N_MOD = 6
EPS = 1e-6
IN_SPLITS = (D_SSM,
             D_SSM + D_XBC,
             D_SSM + D_XBC + 2 * N_SSM_HEADS,
             D_SSM + D_XBC + 2 * N_SSM_HEADS + 2 * D_CONV)
IN_COLS = IN_SPLITS[-1] + 2 * D_MODEL

kernel_name = 'hybrid_bidir_ssd_conformer_gated_block'


def _rmsnorm(x, w):
    xf = x.astype(jnp.float32)
    y = xf * lax.rsqrt(jnp.mean(xf * xf, axis=-1, keepdims=True) + EPS)
    return (y * w.astype(jnp.float32)).astype(x.dtype)


def _layernorm(x, g, b):
    xf = x.astype(jnp.float32)
    xc = xf - jnp.mean(xf, axis=-1, keepdims=True)
    y = xc * lax.rsqrt(jnp.mean(xc * xc, axis=-1, keepdims=True) + EPS)
    return (y * g.astype(jnp.float32) + b.astype(jnp.float32)).astype(x.dtype)


def _dwconv(u, w, bias):
    k = w.shape[0]
    pad = (k - 1) // 2
    out = lax.conv_general_dilated(
        u, w[:, None, :].astype(u.dtype), window_strides=(1,), padding=[(pad, pad)],
        dimension_numbers=('NWC', 'WIO', 'NWC'), feature_group_count=u.shape[-1])
    return out + bias


def _flip(t):
    return jnp.flip(t, axis=1)


def _segsum(a):
    t = a.shape[-1]
    cs = jnp.cumsum(a, axis=-1)
    diff = cs[..., :, None] - cs[..., None, :]
    return jnp.where(jnp.tril(jnp.ones((t, t), dtype=bool)), diff, -jnp.inf)


def _ssd(xdt, adt, bm, cm):
    bsz, s, g, j, p = xdt.shape
    n = bm.shape[-1]
    nc = s // CHUNK
    x = xdt.reshape(bsz, nc, CHUNK, g, j, p)
    b = bm.reshape(bsz, nc, CHUNK, g, n)
    c = cm.reshape(bsz, nc, CHUNK, g, n)
    a = jnp.moveaxis(adt.reshape(bsz, nc, CHUNK, g, j), (1, 2), (3, 4))
    a_cs = jnp.cumsum(a, axis=-1)
    cb = jnp.einsum('bzlgn,bzsgn->bgzls', c, b)
    m = cb[:, :, None] * jnp.exp(_segsum(a))
    y_diag = jnp.einsum('bgjzls,bzsgjp->bzlgjp', m, x)
    decay_to_end = jnp.moveaxis(jnp.exp(a_cs[..., -1:] - a_cs), (3, 4), (1, 2))
    states = jnp.einsum('bzlgn,bzlgjp->bzgjpn', b, x * decay_to_end[..., None])
    chunk_decay = jnp.exp(a_cs[..., -1])

    def step(h, inp):
        s_z, d_z = inp
        return h * d_z[..., None, None] + s_z, h

    h0 = jnp.zeros((bsz, g, j, p, n), x.dtype)
    _, h_in = lax.scan(step, h0, (jnp.moveaxis(states, 1, 0), jnp.moveaxis(chunk_decay, -1, 0)))
    decay_in = jnp.moveaxis(jnp.exp(a_cs), (3, 4), (1, 2))
    y_off = jnp.einsum('bzlgn,zbgjpn->bzlgjp', c, h_in) * decay_in[..., None]
    return (y_diag + y_off).reshape(bsz, s, g, j, p)


def _ssd_branch(z, xbc, dt_raw, w_conv_ssm, b_conv_ssm, dt_bias_fwd, dt_bias_bwd,
                a_log_fwd, a_log_bwd, d_skip, g_ssm_norm, w_ssm_out):
    f32 = jnp.float32
    bsz, s, _ = xbc.shape
    xbc = jax.nn.silu(_dwconv(xbc, w_conv_ssm, b_conv_ssm))
    xs, bm, cm = jnp.split(xbc.astype(f32), (D_SSM, D_SSM + N_GROUPS * D_STATE), axis=-1)
    xh = xs.reshape(bsz, s, N_GROUPS, HEADS_PER_GROUP, SSM_HEAD_DIM)
    bm = bm.reshape(bsz, s, N_GROUPS, D_STATE)
    cm = cm.reshape(bsz, s, N_GROUPS, D_STATE)
    dtf_raw, dtb_raw = jnp.split(dt_raw.astype(f32), 2, axis=-1)
    hshape = (bsz, s, N_GROUPS, HEADS_PER_GROUP)
    dt_f = jax.nn.softplus(dtf_raw + dt_bias_fwd.astype(f32)).reshape(hshape)
    dt_b = jax.nn.softplus(dtb_raw + dt_bias_bwd.astype(f32)).reshape(hshape)
    a_f = -jnp.exp(a_log_fwd.astype(f32)).reshape(N_GROUPS, HEADS_PER_GROUP)
    a_b = -jnp.exp(a_log_bwd.astype(f32)).reshape(N_GROUPS, HEADS_PER_GROUP)
    y_fwd = _ssd(xh * dt_f[..., None], dt_f * a_f, bm, cm)
    y_bwd = _flip(_ssd(_flip(xh * dt_b[..., None]), _flip(dt_b * a_b), _flip(bm), _flip(cm)))
    y = y_fwd + y_bwd + d_skip.astype(f32).reshape(N_GROUPS, HEADS_PER_GROUP, 1) * xh
    gshape = (bsz, s, N_GROUPS, D_SSM // N_GROUPS)
    y = y.reshape(gshape) * jax.nn.silu(z.astype(f32)).reshape(gshape)
    y = y * lax.rsqrt(jnp.mean(y * y, axis=-1, keepdims=True) + EPS)
    y = y * g_ssm_norm.astype(f32).reshape(N_GROUPS, D_SSM // N_GROUPS)
    return y.reshape(bsz, s, D_SSM).astype(z.dtype) @ w_ssm_out


def _conformer_branch(glu_in, b_glu, w_dw, b_dw, ln_g, ln_b, w_conv_out, b_conv_out):
    u = jax.nn.glu(glu_in + b_glu, axis=-1)
    u = _dwconv(u, w_dw, b_dw)
    u = jax.nn.silu(_layernorm(u, ln_g, ln_b))
    return u @ w_conv_out + b_conv_out


def _mixer(h, w_in, w_conv_ssm, b_conv_ssm, dt_bias_fwd, dt_bias_bwd, a_log_fwd, a_log_bwd,
           d_skip, g_ssm_norm, w_ssm_out, b_glu, w_dw, b_dw, ln_g, ln_b, w_conv_out,
           b_conv_out, b_gate, w_mix_out):
    proj = h @ w_in
    z, xbc, dt_raw, glu_in, gate_logits = jnp.split(proj, IN_SPLITS, axis=-1)
    y_a = _ssd_branch(z, xbc, dt_raw, w_conv_ssm, b_conv_ssm, dt_bias_fwd, dt_bias_bwd,
                      a_log_fwd, a_log_bwd, d_skip, g_ssm_norm, w_ssm_out)
    y_b = _conformer_branch(glu_in, b_glu, w_dw, b_dw, ln_g, ln_b, w_conv_out, b_conv_out)
    g_a, g_b = jnp.split(jax.nn.sigmoid(gate_logits + b_gate), 2, axis=-1)
    return (g_a * y_a + g_b * y_b) @ w_mix_out


def _ffn(h, w_gate_up, w_down):
    gt, up = jnp.split(h @ w_gate_up, 2, axis=-1)
    return (jax.nn.silu(gt) * up) @ w_down


def _fwd_setup_inputs(seed: int = 0) -> dict:
    key = jax.random.key(seed)
    ks = jax.random.split(key, 32)
    L = DEPTH
    f32 = jnp.float32

    def nrm(i, shape, scale):
        return scale * jax.random.normal(ks[i], shape, f32)

    def gain(i, shape):
        return 1.0 + 0.1 * jax.random.normal(ks[i], shape, f32)

    dt0 = jnp.exp(jax.random.uniform(ks[9], (2, L, N_SSM_HEADS), f32, math.log(1e-3), math.log(1e-1)))
    dt_bias = dt0 + jnp.log(-jnp.expm1(-dt0))
    a_log = jnp.log(jax.random.uniform(ks[10], (2, L, N_SSM_HEADS), f32, 1.0, 16.0))
    return {
        'x': nrm(0, (BATCH, SEQ, D_MODEL), 1.0),
        'c': nrm(1, (BATCH, D_MODEL), 1.0),
        'w_ada': nrm(2, (L, D_MODEL, N_MOD * D_MODEL), 0.5 * D_MODEL ** -0.5),
        'b_ada': nrm(3, (L, N_MOD * D_MODEL), 0.02),
        'g_pre_mix': gain(4, (L, D_MODEL)),
        'g_post_mix': gain(5, (L, D_MODEL)),
        'w_in': nrm(6, (L, D_MODEL, IN_COLS), D_MODEL ** -0.5),
        'w_conv_ssm': nrm(7, (L, SSM_CONV, D_XBC), SSM_CONV ** -0.5),
        'b_conv_ssm': nrm(8, (L, D_XBC), 0.02),
        'dt_bias_fwd': dt_bias[0],
        'dt_bias_bwd': dt_bias[1],
        'a_log_fwd': a_log[0],
        'a_log_bwd': a_log[1],
        'd_skip': gain(11, (L, N_SSM_HEADS)),
        'g_ssm_norm': gain(12, (L, D_SSM)),
        'w_ssm_out': nrm(13, (L, D_SSM, D_MODEL), D_SSM ** -0.5),
        'b_glu': nrm(14, (L, 2 * D_CONV), 0.02),
        'w_dw': nrm(15, (L, CONV_WIDTH, D_CONV), CONV_WIDTH ** -0.5),
        'b_dw': nrm(16, (L, D_CONV), 0.02),
        'ln_g': gain(17, (L, D_CONV)),
        'ln_b': nrm(18, (L, D_CONV), 0.02),
        'w_conv_out': nrm(19, (L, D_CONV, D_MODEL), D_CONV ** -0.5),
        'b_conv_out': nrm(20, (L, D_MODEL), 0.02),
        'b_gate': nrm(21, (L, 2 * D_MODEL), 0.02),
        'w_mix_out': nrm(22, (L, D_MODEL, D_MODEL), D_MODEL ** -0.5),
        'g_pre_ffn': gain(23, (L, D_MODEL)),
        'g_post_ffn': gain(24, (L, D_MODEL)),
        'w_gate_up': nrm(25, (L, D_MODEL, 2 * D_FF), D_MODEL ** -0.5),
        'w_down': nrm(26, (L, D_FF, D_MODEL), D_FF ** -0.5),
    }


def _fwd_reference(x, c, w_ada, b_ada, g_pre_mix, g_post_mix, w_in, w_conv_ssm, b_conv_ssm,
              dt_bias_fwd, dt_bias_bwd, a_log_fwd, a_log_bwd, d_skip, g_ssm_norm, w_ssm_out,
              b_glu, w_dw, b_dw, ln_g, ln_b, w_conv_out, b_conv_out, b_gate, w_mix_out,
              g_pre_ffn, g_post_ffn, w_gate_up, w_down):
    c_act = jax.nn.silu(c)
    for l in range(DEPTH):
        mod = (c_act @ w_ada[l] + b_ada[l])[:, None, :]
        sh1, sc1, g1, sh2, sc2, g2 = jnp.split(mod, N_MOD, axis=-1)
        h = _rmsnorm(x, g_pre_mix[l]) * (1 + sc1) + sh1
        mix = _mixer(h, w_in[l], w_conv_ssm[l], b_conv_ssm[l], dt_bias_fwd[l], dt_bias_bwd[l],
                     a_log_fwd[l], a_log_bwd[l], d_skip[l], g_ssm_norm[l], w_ssm_out[l],
                     b_glu[l], w_dw[l], b_dw[l], ln_g[l], ln_b[l], w_conv_out[l],
                     b_conv_out[l], b_gate[l], w_mix_out[l])
        x = x + g1 * _rmsnorm(mix, g_post_mix[l])
        h = _rmsnorm(x, g_pre_ffn[l]) * (1 + sc2) + sh2
        x = x + g2 * _rmsnorm(_ffn(h, w_gate_up[l], w_down[l]), g_post_ffn[l])
    return x


import jax as _jax
import jax.numpy as _jnp

TWIN_FORMAT = 'train_step'
FWD_PARAMS = ['x', 'c', 'w_ada', 'b_ada', 'g_pre_mix', 'g_post_mix', 'w_in', 'w_conv_ssm', 'b_conv_ssm', 'dt_bias_fwd', 'dt_bias_bwd', 'a_log_fwd', 'a_log_bwd', 'd_skip', 'g_ssm_norm', 'w_ssm_out', 'b_glu', 'w_dw', 'b_dw', 'ln_g', 'ln_b', 'w_conv_out', 'b_conv_out', 'b_gate', 'w_mix_out', 'g_pre_ffn', 'g_post_ffn', 'w_gate_up', 'w_down']
TWIN_WEIGHTS = ['w_ada', 'b_ada', 'g_pre_mix', 'g_post_mix', 'w_in', 'w_conv_ssm', 'b_conv_ssm', 'dt_bias_fwd', 'dt_bias_bwd', 'a_log_fwd', 'a_log_bwd', 'd_skip', 'g_ssm_norm', 'w_ssm_out', 'b_glu', 'w_dw', 'b_dw', 'ln_g', 'ln_b', 'w_conv_out', 'b_conv_out', 'b_gate', 'w_mix_out', 'g_pre_ffn', 'g_post_ffn', 'w_gate_up', 'w_down']
TWIN_DIFF_INPUT = 'x'
TWIN_INPUTS = ['x', 'c', 'w_ada', 'b_ada', 'g_pre_mix', 'g_post_mix', 'w_in', 'w_conv_ssm', 'b_conv_ssm', 'dt_bias_fwd', 'dt_bias_bwd', 'a_log_fwd', 'a_log_bwd', 'd_skip', 'g_ssm_norm', 'w_ssm_out', 'b_glu', 'w_dw', 'b_dw', 'ln_g', 'ln_b', 'w_conv_out', 'b_conv_out', 'b_gate', 'w_mix_out', 'g_pre_ffn', 'g_post_ffn', 'w_gate_up', 'w_down', 'loss_target', 'm_w_ada', 'm_b_ada', 'm_g_pre_mix', 'm_g_post_mix', 'm_w_in', 'm_w_conv_ssm', 'm_b_conv_ssm', 'm_dt_bias_fwd', 'm_dt_bias_bwd', 'm_a_log_fwd', 'm_a_log_bwd', 'm_d_skip', 'm_g_ssm_norm', 'm_w_ssm_out', 'm_b_glu', 'm_w_dw', 'm_b_dw', 'm_ln_g', 'm_ln_b', 'm_w_conv_out', 'm_b_conv_out', 'm_b_gate', 'm_w_mix_out', 'm_g_pre_ffn', 'm_g_post_ffn', 'm_w_gate_up', 'm_w_down', 'v_w_ada', 'v_b_ada', 'v_g_pre_mix', 'v_g_post_mix', 'v_w_in', 'v_w_conv_ssm', 'v_b_conv_ssm', 'v_dt_bias_fwd', 'v_dt_bias_bwd', 'v_a_log_fwd', 'v_a_log_bwd', 'v_d_skip', 'v_g_ssm_norm', 'v_w_ssm_out', 'v_b_glu', 'v_w_dw', 'v_b_dw', 'v_ln_g', 'v_ln_b', 'v_w_conv_out', 'v_b_conv_out', 'v_b_gate', 'v_w_mix_out', 'v_g_pre_ffn', 'v_g_post_ffn', 'v_w_gate_up', 'v_w_down']
TWIN_OUTPUTS = ['loss', 'grad_x', 'grad_w_ada', 'grad_b_ada', 'grad_g_pre_mix', 'grad_g_post_mix', 'grad_w_in', 'grad_w_conv_ssm', 'grad_b_conv_ssm', 'grad_dt_bias_fwd', 'grad_dt_bias_bwd', 'grad_a_log_fwd', 'grad_a_log_bwd', 'grad_d_skip', 'grad_g_ssm_norm', 'grad_w_ssm_out', 'grad_b_glu', 'grad_w_dw', 'grad_b_dw', 'grad_ln_g', 'grad_ln_b', 'grad_w_conv_out', 'grad_b_conv_out', 'grad_b_gate', 'grad_w_mix_out', 'grad_g_pre_ffn', 'grad_g_post_ffn', 'grad_w_gate_up', 'grad_w_down', 'delta_w_ada', 'delta_b_ada', 'delta_g_pre_mix', 'delta_g_post_mix', 'delta_w_in', 'delta_w_conv_ssm', 'delta_b_conv_ssm', 'delta_dt_bias_fwd', 'delta_dt_bias_bwd', 'delta_a_log_fwd', 'delta_a_log_bwd', 'delta_d_skip', 'delta_g_ssm_norm', 'delta_w_ssm_out', 'delta_b_glu', 'delta_w_dw', 'delta_b_dw', 'delta_ln_g', 'delta_ln_b', 'delta_w_conv_out', 'delta_b_conv_out', 'delta_b_gate', 'delta_w_mix_out', 'delta_g_pre_ffn', 'delta_g_post_ffn', 'delta_w_gate_up', 'delta_w_down', 'new_m_w_ada', 'new_m_b_ada', 'new_m_g_pre_mix', 'new_m_g_post_mix', 'new_m_w_in', 'new_m_w_conv_ssm', 'new_m_b_conv_ssm', 'new_m_dt_bias_fwd', 'new_m_dt_bias_bwd', 'new_m_a_log_fwd', 'new_m_a_log_bwd', 'new_m_d_skip', 'new_m_g_ssm_norm', 'new_m_w_ssm_out', 'new_m_b_glu', 'new_m_w_dw', 'new_m_b_dw', 'new_m_ln_g', 'new_m_ln_b', 'new_m_w_conv_out', 'new_m_b_conv_out', 'new_m_b_gate', 'new_m_w_mix_out', 'new_m_g_pre_ffn', 'new_m_g_post_ffn', 'new_m_w_gate_up', 'new_m_w_down', 'new_v_w_ada', 'new_v_b_ada', 'new_v_g_pre_mix', 'new_v_g_post_mix', 'new_v_w_in', 'new_v_w_conv_ssm', 'new_v_b_conv_ssm', 'new_v_dt_bias_fwd', 'new_v_dt_bias_bwd', 'new_v_a_log_fwd', 'new_v_a_log_bwd', 'new_v_d_skip', 'new_v_g_ssm_norm', 'new_v_w_ssm_out', 'new_v_b_glu', 'new_v_w_dw', 'new_v_b_dw', 'new_v_ln_g', 'new_v_ln_b', 'new_v_w_conv_out', 'new_v_b_conv_out', 'new_v_b_gate', 'new_v_w_mix_out', 'new_v_g_pre_ffn', 'new_v_g_post_ffn', 'new_v_w_gate_up', 'new_v_w_down']
TWIN_LEAF_KINDS = {'loss': 'loss', 'grad_x': 'grad_x', 'grad_w_ada': 'grad_w', 'grad_b_ada': 'grad_w', 'grad_g_pre_mix': 'grad_w', 'grad_g_post_mix': 'grad_w', 'grad_w_in': 'grad_w', 'grad_w_conv_ssm': 'grad_w', 'grad_b_conv_ssm': 'grad_w', 'grad_dt_bias_fwd': 'grad_w', 'grad_dt_bias_bwd': 'grad_w', 'grad_a_log_fwd': 'grad_w', 'grad_a_log_bwd': 'grad_w', 'grad_d_skip': 'grad_w', 'grad_g_ssm_norm': 'grad_w', 'grad_w_ssm_out': 'grad_w', 'grad_b_glu': 'grad_w', 'grad_w_dw': 'grad_w', 'grad_b_dw': 'grad_w', 'grad_ln_g': 'grad_w', 'grad_ln_b': 'grad_w', 'grad_w_conv_out': 'grad_w', 'grad_b_conv_out': 'grad_w', 'grad_b_gate': 'grad_w', 'grad_w_mix_out': 'grad_w', 'grad_g_pre_ffn': 'grad_w', 'grad_g_post_ffn': 'grad_w', 'grad_w_gate_up': 'grad_w', 'grad_w_down': 'grad_w', 'delta_w_ada': 'delta_w', 'delta_b_ada': 'delta_w', 'delta_g_pre_mix': 'delta_w', 'delta_g_post_mix': 'delta_w', 'delta_w_in': 'delta_w', 'delta_w_conv_ssm': 'delta_w', 'delta_b_conv_ssm': 'delta_w', 'delta_dt_bias_fwd': 'delta_w', 'delta_dt_bias_bwd': 'delta_w', 'delta_a_log_fwd': 'delta_w', 'delta_a_log_bwd': 'delta_w', 'delta_d_skip': 'delta_w', 'delta_g_ssm_norm': 'delta_w', 'delta_w_ssm_out': 'delta_w', 'delta_b_glu': 'delta_w', 'delta_w_dw': 'delta_w', 'delta_b_dw': 'delta_w', 'delta_ln_g': 'delta_w', 'delta_ln_b': 'delta_w', 'delta_w_conv_out': 'delta_w', 'delta_b_conv_out': 'delta_w', 'delta_b_gate': 'delta_w', 'delta_w_mix_out': 'delta_w', 'delta_g_pre_ffn': 'delta_w', 'delta_g_post_ffn': 'delta_w', 'delta_w_gate_up': 'delta_w', 'delta_w_down': 'delta_w', 'new_m_w_ada': 'new_m', 'new_m_b_ada': 'new_m', 'new_m_g_pre_mix': 'new_m', 'new_m_g_post_mix': 'new_m', 'new_m_w_in': 'new_m', 'new_m_w_conv_ssm': 'new_m', 'new_m_b_conv_ssm': 'new_m', 'new_m_dt_bias_fwd': 'new_m', 'new_m_dt_bias_bwd': 'new_m', 'new_m_a_log_fwd': 'new_m', 'new_m_a_log_bwd': 'new_m', 'new_m_d_skip': 'new_m', 'new_m_g_ssm_norm': 'new_m', 'new_m_w_ssm_out': 'new_m', 'new_m_b_glu': 'new_m', 'new_m_w_dw': 'new_m', 'new_m_b_dw': 'new_m', 'new_m_ln_g': 'new_m', 'new_m_ln_b': 'new_m', 'new_m_w_conv_out': 'new_m', 'new_m_b_conv_out': 'new_m', 'new_m_b_gate': 'new_m', 'new_m_w_mix_out': 'new_m', 'new_m_g_pre_ffn': 'new_m', 'new_m_g_post_ffn': 'new_m', 'new_m_w_gate_up': 'new_m', 'new_m_w_down': 'new_m', 'new_v_w_ada': 'new_v', 'new_v_b_ada': 'new_v', 'new_v_g_pre_mix': 'new_v', 'new_v_g_post_mix': 'new_v', 'new_v_w_in': 'new_v', 'new_v_w_conv_ssm': 'new_v', 'new_v_b_conv_ssm': 'new_v', 'new_v_dt_bias_fwd': 'new_v', 'new_v_dt_bias_bwd': 'new_v', 'new_v_a_log_fwd': 'new_v', 'new_v_a_log_bwd': 'new_v', 'new_v_d_skip': 'new_v', 'new_v_g_ssm_norm': 'new_v', 'new_v_w_ssm_out': 'new_v', 'new_v_b_glu': 'new_v', 'new_v_w_dw': 'new_v', 'new_v_b_dw': 'new_v', 'new_v_ln_g': 'new_v', 'new_v_ln_b': 'new_v', 'new_v_w_conv_out': 'new_v', 'new_v_b_conv_out': 'new_v', 'new_v_b_gate': 'new_v', 'new_v_w_mix_out': 'new_v', 'new_v_g_pre_ffn': 'new_v', 'new_v_g_post_ffn': 'new_v', 'new_v_w_gate_up': 'new_v', 'new_v_w_down': 'new_v'}


def _forward(args):
    return _fwd_reference(*[args[k] for k in FWD_PARAMS])


def _output_shape():
    def fwd():
        inp = _fwd_setup_inputs(0)
        return _fwd_reference(*[inp[k] for k in FWD_PARAMS])
    out = _jax.eval_shape(fwd)
    return out.shape, out.dtype

N_MICROBATCH = 1
ADAM_LR = 0.001
ADAM_B1 = 0.9
ADAM_B2 = 0.999
ADAM_EPS = 1e-08
ADAM_WD = 0.01
ADAM_STEP = 10
PER_EXAMPLE_BATCH_AXIS = {'x': 0, 'c': 0, 'loss_target': 0}
SHARED_INPUTS = []
_WEIGHT_DTYPES = {'w_ada': _jnp.float32, 'b_ada': _jnp.float32, 'g_pre_mix': _jnp.float32, 'g_post_mix': _jnp.float32, 'w_in': _jnp.float32, 'w_conv_ssm': _jnp.float32, 'b_conv_ssm': _jnp.float32, 'dt_bias_fwd': _jnp.float32, 'dt_bias_bwd': _jnp.float32, 'a_log_fwd': _jnp.float32, 'a_log_bwd': _jnp.float32, 'd_skip': _jnp.float32, 'g_ssm_norm': _jnp.float32, 'w_ssm_out': _jnp.float32, 'b_glu': _jnp.float32, 'w_dw': _jnp.float32, 'b_dw': _jnp.float32, 'ln_g': _jnp.float32, 'ln_b': _jnp.float32, 'w_conv_out': _jnp.float32, 'b_conv_out': _jnp.float32, 'b_gate': _jnp.float32, 'w_mix_out': _jnp.float32, 'g_pre_ffn': _jnp.float32, 'g_post_ffn': _jnp.float32, 'w_gate_up': _jnp.float32, 'w_down': _jnp.float32}
MOMENT_SCALE = {'w_ada': 1.227675e+00, 'b_ada': 2.715291e+00, 'g_pre_mix': 8.749603e-02, 'g_post_mix': 3.304944e+00, 'w_in': 3.635632e-02, 'w_conv_ssm': 4.629472e-02, 'b_conv_ssm': 1.090104e-01, 'dt_bias_fwd': 7.496947e-02, 'dt_bias_bwd': 7.024635e-02, 'a_log_fwd': 1.916760e-01, 'a_log_bwd': 2.819530e-01, 'd_skip': 1.819334e-01, 'g_ssm_norm': 7.303451e-02, 'w_ssm_out': 9.875365e-02, 'b_glu': 1.152084e-01, 'w_dw': 4.539661e-02, 'b_dw': 2.704303e-01, 'ln_g': 1.122014e-01, 'ln_b': 1.649481e-01, 'w_conv_out': 7.131053e-02, 'b_conv_out': 3.333650e-01, 'b_gate': 3.378513e-02, 'w_mix_out': 1.212989e-01, 'g_pre_ffn': 8.690674e-02, 'g_post_ffn': 3.283165e+00, 'w_gate_up': 3.969085e-02, 'w_down': 6.985972e-02}


def _to_microbatches(a, axis):
    t = _jnp.moveaxis(a, axis, 0)
    t = t.reshape((N_MICROBATCH, t.shape[0] // N_MICROBATCH) + t.shape[1:])
    return _jnp.moveaxis(t, 1, axis + 1)


def setup_inputs(seed: int = 0) -> dict:
    inp = _fwd_setup_inputs(seed)
    key = _jax.random.fold_in(_jax.random.key(seed), 7919)
    shape, _ = _output_shape()
    out = dict(inp)
    out["loss_target"] = _jax.random.normal(_jax.random.fold_in(key, 0), shape, _jnp.float32)
    for i, name in enumerate(TWIN_WEIGHTS):
        w = inp[name].astype(_jnp.float32)
        if MOMENT_SCALE is None:
            s = _jnp.sqrt(_jnp.mean(_jnp.square(w)) + 1e-30)
        else:
            s = MOMENT_SCALE[name]
        km, kv = _jax.random.split(_jax.random.fold_in(key, i + 1))
        out[name] = w
        out["m_" + name] = s * _jax.random.normal(km, w.shape, _jnp.float32)
        out["v_" + name] = (s * s) * _jax.random.uniform(kv, w.shape, _jnp.float32, 0.5, 1.5)
    if N_MICROBATCH > 1:
        for name, axis in PER_EXAMPLE_BATCH_AXIS.items():
            out[name] = _to_microbatches(out[name], axis)
    return {'x': out['x'], 'c': out['c'], 'w_ada': out['w_ada'], 'b_ada': out['b_ada'], 'g_pre_mix': out['g_pre_mix'], 'g_post_mix': out['g_post_mix'], 'w_in': out['w_in'], 'w_conv_ssm': out['w_conv_ssm'], 'b_conv_ssm': out['b_conv_ssm'], 'dt_bias_fwd': out['dt_bias_fwd'], 'dt_bias_bwd': out['dt_bias_bwd'], 'a_log_fwd': out['a_log_fwd'], 'a_log_bwd': out['a_log_bwd'], 'd_skip': out['d_skip'], 'g_ssm_norm': out['g_ssm_norm'], 'w_ssm_out': out['w_ssm_out'], 'b_glu': out['b_glu'], 'w_dw': out['w_dw'], 'b_dw': out['b_dw'], 'ln_g': out['ln_g'], 'ln_b': out['ln_b'], 'w_conv_out': out['w_conv_out'], 'b_conv_out': out['b_conv_out'], 'b_gate': out['b_gate'], 'w_mix_out': out['w_mix_out'], 'g_pre_ffn': out['g_pre_ffn'], 'g_post_ffn': out['g_post_ffn'], 'w_gate_up': out['w_gate_up'], 'w_down': out['w_down'], 'loss_target': out['loss_target'], 'm_w_ada': out['m_w_ada'], 'm_b_ada': out['m_b_ada'], 'm_g_pre_mix': out['m_g_pre_mix'], 'm_g_post_mix': out['m_g_post_mix'], 'm_w_in': out['m_w_in'], 'm_w_conv_ssm': out['m_w_conv_ssm'], 'm_b_conv_ssm': out['m_b_conv_ssm'], 'm_dt_bias_fwd': out['m_dt_bias_fwd'], 'm_dt_bias_bwd': out['m_dt_bias_bwd'], 'm_a_log_fwd': out['m_a_log_fwd'], 'm_a_log_bwd': out['m_a_log_bwd'], 'm_d_skip': out['m_d_skip'], 'm_g_ssm_norm': out['m_g_ssm_norm'], 'm_w_ssm_out': out['m_w_ssm_out'], 'm_b_glu': out['m_b_glu'], 'm_w_dw': out['m_w_dw'], 'm_b_dw': out['m_b_dw'], 'm_ln_g': out['m_ln_g'], 'm_ln_b': out['m_ln_b'], 'm_w_conv_out': out['m_w_conv_out'], 'm_b_conv_out': out['m_b_conv_out'], 'm_b_gate': out['m_b_gate'], 'm_w_mix_out': out['m_w_mix_out'], 'm_g_pre_ffn': out['m_g_pre_ffn'], 'm_g_post_ffn': out['m_g_post_ffn'], 'm_w_gate_up': out['m_w_gate_up'], 'm_w_down': out['m_w_down'], 'v_w_ada': out['v_w_ada'], 'v_b_ada': out['v_b_ada'], 'v_g_pre_mix': out['v_g_pre_mix'], 'v_g_post_mix': out['v_g_post_mix'], 'v_w_in': out['v_w_in'], 'v_w_conv_ssm': out['v_w_conv_ssm'], 'v_b_conv_ssm': out['v_b_conv_ssm'], 'v_dt_bias_fwd': out['v_dt_bias_fwd'], 'v_dt_bias_bwd': out['v_dt_bias_bwd'], 'v_a_log_fwd': out['v_a_log_fwd'], 'v_a_log_bwd': out['v_a_log_bwd'], 'v_d_skip': out['v_d_skip'], 'v_g_ssm_norm': out['v_g_ssm_norm'], 'v_w_ssm_out': out['v_w_ssm_out'], 'v_b_glu': out['v_b_glu'], 'v_w_dw': out['v_w_dw'], 'v_b_dw': out['v_b_dw'], 'v_ln_g': out['v_ln_g'], 'v_ln_b': out['v_ln_b'], 'v_w_conv_out': out['v_w_conv_out'], 'v_b_conv_out': out['v_b_conv_out'], 'v_b_gate': out['v_b_gate'], 'v_w_mix_out': out['v_w_mix_out'], 'v_g_pre_ffn': out['v_g_pre_ffn'], 'v_g_post_ffn': out['v_g_post_ffn'], 'v_w_gate_up': out['v_w_gate_up'], 'v_w_down': out['v_w_down']}


def _loss(weights, diff, rest, loss_target):
    with _jax.named_scope("forward"):
        args = {**rest, TWIN_DIFF_INPUT: diff, **{k: w.astype(_WEIGHT_DTYPES[k]) for k, w in weights.items()}}
        y = _forward(args)
    with _jax.named_scope("loss_head"):
        err = _jnp.square(y.astype(_jnp.float32) - loss_target)
        return 0.5 * _jnp.sum(_jnp.mean(err, axis=-1)) if err.ndim else 0.5 * err


def _adamw(w, g, m, v):
    m = ADAM_B1 * m + (1.0 - ADAM_B1) * g
    v = ADAM_B2 * v + (1.0 - ADAM_B2) * _jnp.square(g)
    m_hat = m / (1.0 - ADAM_B1 ** ADAM_STEP)
    v_hat = v / (1.0 - ADAM_B2 ** ADAM_STEP)
    delta = -ADAM_LR * (m_hat / (_jnp.sqrt(v_hat) + ADAM_EPS) + ADAM_WD * w)
    return delta, m, v


def reference(x, c, w_ada, b_ada, g_pre_mix, g_post_mix, w_in, w_conv_ssm, b_conv_ssm, dt_bias_fwd, dt_bias_bwd, a_log_fwd, a_log_bwd, d_skip, g_ssm_norm, w_ssm_out, b_glu, w_dw, b_dw, ln_g, ln_b, w_conv_out, b_conv_out, b_gate, w_mix_out, g_pre_ffn, g_post_ffn, w_gate_up, w_down, loss_target, m_w_ada, m_b_ada, m_g_pre_mix, m_g_post_mix, m_w_in, m_w_conv_ssm, m_b_conv_ssm, m_dt_bias_fwd, m_dt_bias_bwd, m_a_log_fwd, m_a_log_bwd, m_d_skip, m_g_ssm_norm, m_w_ssm_out, m_b_glu, m_w_dw, m_b_dw, m_ln_g, m_ln_b, m_w_conv_out, m_b_conv_out, m_b_gate, m_w_mix_out, m_g_pre_ffn, m_g_post_ffn, m_w_gate_up, m_w_down, v_w_ada, v_b_ada, v_g_pre_mix, v_g_post_mix, v_w_in, v_w_conv_ssm, v_b_conv_ssm, v_dt_bias_fwd, v_dt_bias_bwd, v_a_log_fwd, v_a_log_bwd, v_d_skip, v_g_ssm_norm, v_w_ssm_out, v_b_glu, v_w_dw, v_b_dw, v_ln_g, v_ln_b, v_w_conv_out, v_b_conv_out, v_b_gate, v_w_mix_out, v_g_pre_ffn, v_g_post_ffn, v_w_gate_up, v_w_down):
    given = dict(x=x, c=c, w_ada=w_ada, b_ada=b_ada, g_pre_mix=g_pre_mix, g_post_mix=g_post_mix, w_in=w_in, w_conv_ssm=w_conv_ssm, b_conv_ssm=b_conv_ssm, dt_bias_fwd=dt_bias_fwd, dt_bias_bwd=dt_bias_bwd, a_log_fwd=a_log_fwd, a_log_bwd=a_log_bwd, d_skip=d_skip, g_ssm_norm=g_ssm_norm, w_ssm_out=w_ssm_out, b_glu=b_glu, w_dw=w_dw, b_dw=b_dw, ln_g=ln_g, ln_b=ln_b, w_conv_out=w_conv_out, b_conv_out=b_conv_out, b_gate=b_gate, w_mix_out=w_mix_out, g_pre_ffn=g_pre_ffn, g_post_ffn=g_post_ffn, w_gate_up=w_gate_up, w_down=w_down, loss_target=loss_target, m_w_ada=m_w_ada, m_b_ada=m_b_ada, m_g_pre_mix=m_g_pre_mix, m_g_post_mix=m_g_post_mix, m_w_in=m_w_in, m_w_conv_ssm=m_w_conv_ssm, m_b_conv_ssm=m_b_conv_ssm, m_dt_bias_fwd=m_dt_bias_fwd, m_dt_bias_bwd=m_dt_bias_bwd, m_a_log_fwd=m_a_log_fwd, m_a_log_bwd=m_a_log_bwd, m_d_skip=m_d_skip, m_g_ssm_norm=m_g_ssm_norm, m_w_ssm_out=m_w_ssm_out, m_b_glu=m_b_glu, m_w_dw=m_w_dw, m_b_dw=m_b_dw, m_ln_g=m_ln_g, m_ln_b=m_ln_b, m_w_conv_out=m_w_conv_out, m_b_conv_out=m_b_conv_out, m_b_gate=m_b_gate, m_w_mix_out=m_w_mix_out, m_g_pre_ffn=m_g_pre_ffn, m_g_post_ffn=m_g_post_ffn, m_w_gate_up=m_w_gate_up, m_w_down=m_w_down, v_w_ada=v_w_ada, v_b_ada=v_b_ada, v_g_pre_mix=v_g_pre_mix, v_g_post_mix=v_g_post_mix, v_w_in=v_w_in, v_w_conv_ssm=v_w_conv_ssm, v_b_conv_ssm=v_b_conv_ssm, v_dt_bias_fwd=v_dt_bias_fwd, v_dt_bias_bwd=v_dt_bias_bwd, v_a_log_fwd=v_a_log_fwd, v_a_log_bwd=v_a_log_bwd, v_d_skip=v_d_skip, v_g_ssm_norm=v_g_ssm_norm, v_w_ssm_out=v_w_ssm_out, v_b_glu=v_b_glu, v_w_dw=v_w_dw, v_b_dw=v_b_dw, v_ln_g=v_ln_g, v_ln_b=v_ln_b, v_w_conv_out=v_w_conv_out, v_b_conv_out=v_b_conv_out, v_b_gate=v_b_gate, v_w_mix_out=v_w_mix_out, v_g_pre_ffn=v_g_pre_ffn, v_g_post_ffn=v_g_post_ffn, v_w_gate_up=v_w_gate_up, v_w_down=v_w_down)
    weights = {n: given[n] for n in TWIN_WEIGHTS}
    shared = {n: given[n] for n in SHARED_INPUTS}
    per_example = {n: given[n] for n in ['x', 'c']}
    grad_fn = _jax.value_and_grad(_loss, argnums=(0, 1))

    def one_microbatch(ex, loss_target):
        ex = dict(ex)
        diff = ex.pop(TWIN_DIFF_INPUT)
        return grad_fn(weights, diff, {**shared, **ex}, loss_target)

    if N_MICROBATCH == 1:
        loss, (grad_w, grad_x) = one_microbatch(per_example, given["loss_target"])
    else:
        def body(carry, xs):
            loss_sum, grad_sum = carry
            l_k, (gw_k, gx_k) = one_microbatch(xs[0], xs[1])
            with _jax.named_scope("update"):
                return (loss_sum + l_k, _jax.tree.map(_jnp.add, grad_sum, gw_k)), gx_k

        init = (_jnp.zeros((), _jnp.float32), _jax.tree.map(_jnp.zeros_like, weights))
        (loss, grad_w), grad_x = _jax.lax.scan(body, init, (per_example, given["loss_target"]))
    with _jax.named_scope("update"):
        delta_w, new_m, new_v = {}, {}, {}
        for n in TWIN_WEIGHTS:
            delta_w[n], new_m[n], new_v[n] = _adamw(weights[n], grad_w[n], given["m_" + n], given["v_" + n])
    return (loss, grad_x, *[grad_w[n] for n in TWIN_WEIGHTS], *[delta_w[n] for n in TWIN_WEIGHTS],
            *[new_m[n] for n in TWIN_WEIGHTS], *[new_v[n] for n in TWIN_WEIGHTS])
```

```python
import functools

import jax
import jax.numpy as jnp
from jax import lax
from jax.experimental import pallas as pl
from jax.experimental.pallas import tpu as pltpu

F32 = jnp.float32
BF16 = jnp.bfloat16

N_GROUPS = 8
HEAD_DIM = 64
D_STATE = 128
CHUNK = 128
EPS = 1e-6
N_MOD = 6
ADAM_LR = 0.001
ADAM_B1 = 0.9
ADAM_B2 = 0.999
ADAM_EPS = 1e-08
ADAM_WD = 0.01
ADAM_STEP = 10

V7X_VMEM_BYTES = 64 * 1024 * 1024
VMEM_LIMIT = V7X_VMEM_BYTES - 8 * 1024 * 1024
ROW_TILE_BUDGET = 20 * 1024 * 1024
LANES = 128
NEG = -1e30
MESH = pl.DeviceIdType.MESH
N_CHIPS = 4
N_DEV = 8


def _params(sem):
    return pltpu.CompilerParams(dimension_semantics=sem, vmem_limit_bytes=VMEM_LIMIT)


def _sigmoid(x):
    return 1.0 / (1.0 + jnp.exp(-x))


def _silu(x):
    return x * _sigmoid(x)


def _dsilu(x):
    s = _sigmoid(x)
    return s * (1.0 + x * (1.0 - s))


def _softplus(x):
    return jnp.maximum(x, 0.0) + jnp.log(1.0 + jnp.exp(-jnp.abs(x)))


def _sum0(a):
    return jnp.sum(a, axis=0, keepdims=True)


def _mean1(a):
    return jnp.mean(a, axis=1, keepdims=True)


def _rowwise(name, body, rows, bcasts, out_rows, out_accs=(), tile=None):
    rows = [r if isinstance(r, tuple) else (r, r.shape[1], 0) for r in rows]
    s = rows[0][0].shape[0]
    if tile is None:
        per_row = sum(w * a.dtype.itemsize for a, w, _ in rows) + sum(w * jnp.dtype(dt).itemsize for w, dt in out_rows)
        tile = 1024
        while tile > 16 and (tile * per_row * 2 > ROW_TILE_BUDGET or s % tile):
            tile //= 2
        if s * per_row * 2 <= ROW_TILE_BUDGET:
            tile = s
    assert s % tile == 0
    n_in = len(rows) + len(bcasts)
    n_o = len(out_rows)

    def kern(*refs):
        res = body(*[r[...] for r in refs[:n_in]])
        outs = refs[n_in:]
        for o, v in zip(outs[:n_o], res[:n_o]):
            o[...] = v.astype(o.dtype)
        if out_accs:
            @pl.when(pl.program_id(0) == 0)
            def _():
                for o in outs[n_o:]:
                    o[...] = jnp.zeros_like(o)
            for o, v in zip(outs[n_o:], res[n_o:]):
                o[...] += v

    in_specs = [pl.BlockSpec((tile, w), functools.partial(lambda i, cb: (i, cb), cb=cb)) for _, w, cb in rows]
    in_specs += [pl.BlockSpec(b.shape, functools.partial(lambda i, nd: (0,) * nd, nd=b.ndim)) for b in bcasts]
    out_shape = [jax.ShapeDtypeStruct((s, w), dt) for w, dt in out_rows]
    out_shape += [jax.ShapeDtypeStruct((1, w), F32) for w in out_accs]
    out_specs = [pl.BlockSpec((tile, w), lambda i: (i, 0)) for w, _ in out_rows]
    out_specs += [pl.BlockSpec((1, w), lambda i: (0, 0)) for w in out_accs]
    return pl.pallas_call(
        kern, name=name, grid=(s // tile,), in_specs=in_specs, out_specs=out_specs, out_shape=out_shape,
        compiler_params=_params(("arbitrary",) if out_accs else ("parallel",)),
    )(*[a for a, _, _ in rows], *bcasts)


def _tile(n, pref):
    if n <= pref:
        return n
    t = (pref // LANES) * LANES
    while t >= LANES:
        if n % t == 0:
            return t
        t -= LANES
    return n


def _matmul(name, a, b, *, ta=False, tb=False, out_dtype=F32, bias=None, add=None, b_blocks=1, out_blocks=1,
            tm=1024, tn=1024, tk=2048):
    m, k = (a.shape[1], a.shape[0]) if ta else a.shape
    if b_blocks > 1:
        rows_b, cols_b = b.shape[1], b.shape[2] * b_blocks
    else:
        rows_b, cols_b = b.shape
    n, kb = (rows_b, cols_b) if tb else (cols_b, rows_b)
    assert k == kb, (name, a.shape, b.shape)
    tm, tn, tk = _tile(m, tm), _tile(n, tn), _tile(k, tk)
    if b_blocks > 1:
        per = cols_b // b_blocks
        if tb:
            tk = _tile(per, tk)
        else:
            tn = _tile(per, tn)
    if out_blocks > 1:
        tn = _tile(n // out_blocks, tn)
    nk = k // tk
    grid = (m // tm, n // tn, nk)

    a_spec = pl.BlockSpec((tk, tm), lambda i, j, kk: (kk, i)) if ta else pl.BlockSpec((tm, tk), lambda i, j, kk: (i, kk))
    if b_blocks > 1:
        if tb:
            nb = per // tk
            b_spec = pl.BlockSpec((1, tn, tk), lambda i, j, kk: (kk // nb, j, kk % nb))
        else:
            nb = per // tn
            b_spec = pl.BlockSpec((1, tk, tn), lambda i, j, kk: (j // nb, kk, j % nb))
    else:
        b_spec = pl.BlockSpec((tn, tk), lambda i, j, kk: (j, kk)) if tb else pl.BlockSpec((tk, tn), lambda i, j, kk: (kk, j))
    in_specs = [a_spec, b_spec]
    operands = [a, b]
    if bias is not None:
        in_specs.append(pl.BlockSpec((1, tn), lambda i, j, kk: (0, j)))
        operands.append(bias)
    if add is not None:
        in_specs.append(pl.BlockSpec((tm, tn), lambda i, j, kk: (i, j)))
        operands.append(add)
    if out_blocks > 1:
        nbo = (n // out_blocks) // tn
        out_spec = pl.BlockSpec((1, tm, tn), lambda i, j, kk: (j // nbo, i, j % nbo))
        out_shape = jax.ShapeDtypeStruct((out_blocks, m, n // out_blocks), out_dtype)
    else:
        out_spec = pl.BlockSpec((tm, tn), lambda i, j, kk: (i, j))
        out_shape = jax.ShapeDtypeStruct((m, n), out_dtype)
    dims = (((0 if ta else 1,), (1 if tb else 0,)), ((), ()))
    has_bias, has_add = bias is not None, add is not None

    def kern(*refs):
        a_ref, b_ref = refs[0], refs[1]
        pos = 2
        bias_ref = add_ref = None
        if has_bias:
            bias_ref = refs[pos]
            pos += 1
        if has_add:
            add_ref = refs[pos]
            pos += 1
        o_ref = refs[pos]
        acc_ref = refs[pos + 1] if nk > 1 else None
        av = a_ref[...].astype(BF16)
        bv = (b_ref[0] if b_blocks > 1 else b_ref[...]).astype(BF16)
        p = lax.dot_general(av, bv, dims, preferred_element_type=F32)

        def finish(acc):
            if has_bias:
                acc = acc + bias_ref[...]
            if has_add:
                acc = acc + add_ref[...]
            if out_blocks > 1:
                o_ref[0] = acc.astype(o_ref.dtype)
            else:
                o_ref[...] = acc.astype(o_ref.dtype)

        if nk == 1:
            finish(p)
        else:
            kk = pl.program_id(2)

            @pl.when(kk == 0)
            def _():
                acc_ref[...] = p

            @pl.when(kk > 0)
            def _():
                acc_ref[...] += p

            @pl.when(kk == nk - 1)
            def _():
                finish(acc_ref[...])

    return pl.pallas_call(
        kern, name=name, grid=grid, in_specs=in_specs, out_specs=out_spec, out_shape=out_shape,
        scratch_shapes=[pltpu.VMEM((tm, tn), F32)] if nk > 1 else [],
        compiler_params=_params(("parallel", "parallel", "arbitrary")),
    )(*operands)


CONV_HALO = 16
CONV_ROWS = 256


def _shifted(win, s, rows):
    n = win.shape[0]
    return (pltpu.roll(win, (n - s) % n, axis=0) if s % n else win)[:rows]


def _dwconv_fwd(name, x, w, b, *, silu, out_dtype=F32):
    s, c = x.shape
    k = w.shape[0]
    pad = (k - 1) // 2
    assert pad <= CONV_HALO and c % LANES == 0
    t = min(CONV_ROWS, s)
    n_chunks = s // t

    def kern(x_ref, w_ref, b_ref, o_ref, xp_ref):
        zeros = jnp.zeros((CONV_HALO, LANES), F32)
        xp_ref[0:CONV_HALO, :] = zeros
        xp_ref[CONV_HALO + s:CONV_HALO + s + CONV_HALO, :] = zeros
        xp_ref[CONV_HALO:CONV_HALO + s, :] = x_ref[...]
        bv = b_ref[...]

        def chunk(i, carry):
            base = pl.multiple_of(i * t, 8)
            win = xp_ref[pl.ds(base, t + 2 * CONV_HALO), :]
            acc = jnp.zeros((t, LANES), F32)
            for j in range(k):
                acc = acc + _shifted(win, CONV_HALO - pad + j, t) * w_ref[pl.ds(j, 1), :]
            acc = acc + bv
            o_ref[pl.ds(base, t), :] = (_silu(acc) if silu else acc).astype(o_ref.dtype)
            return carry

        lax.fori_loop(0, n_chunks, chunk, 0)

    return pl.pallas_call(
        kern, name=name, grid=(c // LANES,),
        in_specs=[pl.BlockSpec((s, LANES), lambda i: (0, i)), pl.BlockSpec((k, LANES), lambda i: (0, i)),
                  pl.BlockSpec((1, LANES), lambda i: (0, i))],
        out_specs=pl.BlockSpec((s, LANES), lambda i: (0, i)),
        out_shape=jax.ShapeDtypeStruct((s, c), out_dtype),
        scratch_shapes=[pltpu.VMEM((s + 2 * CONV_HALO, LANES), F32)],
        compiler_params=_params(("parallel",)),
    )(x, w, b)


def _dwconv_bwd(name, x, w, b, dout, *, silu, dx_dtype=F32):
    s, c = x.shape
    k = w.shape[0]
    pad = (k - 1) // 2
    t = min(CONV_ROWS, s)
    n_chunks = s // t

    def kern(x_ref, w_ref, b_ref, do_ref, dx_ref, dw_ref, db_ref, xp_ref, dp_ref):
        zeros = jnp.zeros((CONV_HALO, LANES), F32)
        for ref in (xp_ref, dp_ref):
            ref[0:CONV_HALO, :] = zeros
            ref[CONV_HALO + s:CONV_HALO + s + CONV_HALO, :] = zeros
        xp_ref[CONV_HALO:CONV_HALO + s, :] = x_ref[...]
        bv = b_ref[...]
        dw_ref[...] = jnp.zeros_like(dw_ref)

        def pre_chunk(i, dbias):
            base = pl.multiple_of(i * t, 8)
            win = xp_ref[pl.ds(base, t + 2 * CONV_HALO), :]
            dpre = do_ref[pl.ds(base, t), :].astype(F32)
            if silu:
                acc = jnp.zeros((t, LANES), F32)
                for j in range(k):
                    acc = acc + _shifted(win, CONV_HALO - pad + j, t) * w_ref[pl.ds(j, 1), :]
                dpre = dpre * _dsilu(acc + bv)
            dp_ref[pl.ds(base + CONV_HALO, t), :] = dpre
            for j in range(k):
                dw_ref[pl.ds(j, 1), :] += _sum0(dpre * _shifted(win, CONV_HALO - pad + j, t))
            return dbias + _sum0(dpre)

        db_ref[...] = lax.fori_loop(0, n_chunks, pre_chunk, jnp.zeros((1, LANES), F32))

        def dx_chunk(i, carry):
            base = pl.multiple_of(i * t, 8)
            win = dp_ref[pl.ds(base, t + 2 * CONV_HALO), :]
            acc = jnp.zeros((t, LANES), F32)
            for j in range(k):
                acc = acc + _shifted(win, CONV_HALO + pad - j, t) * w_ref[pl.ds(j, 1), :]
            dx_ref[pl.ds(base, t), :] = acc.astype(dx_ref.dtype)
            return carry

        lax.fori_loop(0, n_chunks, dx_chunk, 0)

    col = lambda rows: pl.BlockSpec((rows, LANES), lambda i: (0, i))
    return pl.pallas_call(
        kern, name=name, grid=(c // LANES,),
        in_specs=[col(s), col(k), col(1), col(s)],
        out_specs=[col(s), col(k), col(1)],
        out_shape=[jax.ShapeDtypeStruct((s, c), dx_dtype), jax.ShapeDtypeStruct((k, c), F32),
                   jax.ShapeDtypeStruct((1, c), F32)],
        scratch_shapes=[pltpu.VMEM((s + 2 * CONV_HALO, LANES), F32), pltpu.VMEM((s + 2 * CONV_HALO, LANES), F32)],
        compiler_params=_params(("parallel",)),
    )(x, w, b, dout)


def _ssd_pieces(reverse, j_heads):
    li = lax.broadcasted_iota(jnp.int32, (CHUNK, CHUNK), 0)
    si = lax.broadcasted_iota(jnp.int32, (CHUNK, CHUNK), 1)
    mask = (li <= si) if reverse else (li >= si)
    lo = si < HEAD_DIM
    sub_lo = lax.broadcasted_iota(jnp.int32, (CHUNK, 1), 0) < HEAD_DIM
    jl = lax.broadcasted_iota(jnp.int32, (CHUNK, j_heads), 1)
    js = lax.broadcasted_iota(jnp.int32, (j_heads, CHUNK), 0)

    def col(a, j):
        return jnp.sum(jnp.where(jl == j, a, 0.0), axis=1, keepdims=True)

    def row(a, j):
        return jnp.sum(jnp.where(js == j, a, 0.0), axis=0, keepdims=True)

    return mask, lo, sub_lo, jl, col, row


_NT = (((1,), (1,)), ((), ()))
_TN = (((0,), (0,)), ((), ()))


def _dot(a, b, dims=None):
    if dims is None:
        return jnp.dot(a, b, preferred_element_type=F32)
    return lax.dot_general(a, b, dims, preferred_element_type=F32)


def _ssd_specs(reverse_order, nc, j_heads, d_ssm):
    gw = j_heads * HEAD_DIM
    b_off = d_ssm // D_STATE
    c_off = b_off + N_GROUPS
    zz = (lambda z: nc - 1 - z) if reverse_order else (lambda z: z)
    xs_spec = pl.BlockSpec((CHUNK, gw), lambda g, z: (zz(z), g))
    b_spec = pl.BlockSpec((CHUNK, D_STATE), lambda g, z: (zz(z), b_off + g))
    c_spec = pl.BlockSpec((CHUNK, D_STATE), lambda g, z: (zz(z), c_off + g))
    head_spec = pl.BlockSpec((1, CHUNK, j_heads), lambda g, z: (g, zz(z), 0))
    headt_spec = pl.BlockSpec((1, j_heads, CHUNK), lambda g, z: (g, 0, zz(z)))
    state_spec = pl.BlockSpec((1, 1, j_heads // 2, CHUNK, CHUNK), lambda g, z: (g, zz(z), 0, 0, 0))
    grp_spec = pl.BlockSpec((CHUNK, D_STATE), lambda g, z: (zz(z), g))
    return xs_spec, b_spec, c_spec, head_spec, headt_spec, state_spec, grp_spec


def _ssd_fwd(name, xbc_c, hd, *, reverse, d_ssm):
    s = xbc_c.shape[0]
    nc = s // CHUNK
    j_heads = hd["dt"].shape[2]
    jp = j_heads // 2
    xs_spec, b_spec, c_spec, head_spec, headt_spec, state_spec, _ = _ssd_specs(reverse, nc, j_heads, d_ssm)

    def kern(xs_ref, b_ref, c_ref, dt_ref, gam_ref, gamt_ref, din_ref, dst_ref, et_ref, y_ref, hs_ref, h_ref):
        @pl.when(pl.program_id(1) == 0)
        def _():
            h_ref[...] = jnp.zeros_like(h_ref)

        mask, lo, sub_lo, _, col, row = _ssd_pieces(reverse, j_heads)
        bb = b_ref[...].astype(BF16)
        cb = c_ref[...].astype(BF16)
        cbt = _dot(cb, bb, _NT)
        dt, gam, gamt, din, dst, et = dt_ref[0], gam_ref[0], gamt_ref[0], din_ref[0], dst_ref[0], et_ref[0]
        for p in range(jp):
            ja, jb = 2 * p, 2 * p + 1
            sel = lambda a: jnp.where(lo, col(a, ja), col(a, jb))
            lanes = slice(p * 2 * HEAD_DIM, (p + 1) * 2 * HEAD_DIM)
            xdt = xs_ref[:, lanes] * sel(dt)
            xb = xdt.astype(BF16)
            ys = []
            for j in (ja, jb):
                decay = jnp.exp(jnp.where(mask, col(gam, j) - row(gamt, j), NEG))
                ys.append(_dot((cbt * decay).astype(BF16), xb))
            h2 = h_ref[p]
            y_off = _dot(cb, h2.astype(BF16), _NT) * sel(din)
            y_ref[:, lanes] = jnp.where(lo, ys[0], ys[1]) + y_off
            hs_ref[0, 0, p] = h2
            s_new = _dot((xdt * sel(dst)).astype(BF16), bb, _TN)
            h_ref[p] = h2 * jnp.where(sub_lo, col(et, ja), col(et, jb)) + s_new

    return pl.pallas_call(
        kern, name=name, grid=(N_GROUPS, nc),
        in_specs=[xs_spec, b_spec, c_spec, head_spec, head_spec, headt_spec, head_spec, head_spec, head_spec],
        out_specs=[xs_spec, state_spec],
        out_shape=[jax.ShapeDtypeStruct((s, d_ssm), F32),
                   jax.ShapeDtypeStruct((N_GROUPS, nc, jp, CHUNK, CHUNK), F32)],
        scratch_shapes=[pltpu.VMEM((jp, 2 * HEAD_DIM, D_STATE), F32)],
        compiler_params=_params(("parallel", "arbitrary")),
    )(xbc_c, xbc_c, xbc_c, hd["dt"], hd["gam"], hd["gamT"], hd["din"], hd["dst"], hd["etot"])


def _ssd_bwd(name, xbc_c, dy, hs, hd, *, reverse, d_ssm):
    s = xbc_c.shape[0]
    nc = s // CHUNK
    j_heads = hd["dt"].shape[2]
    jp = j_heads // 2
    xs_spec, b_spec, c_spec, head_spec, headt_spec, state_spec, grp_spec = _ssd_specs(not reverse, nc, j_heads, d_ssm)

    def kern(xs_ref, b_ref, c_ref, dy_ref, hs_ref, dt_ref, gam_ref, gamt_ref, din_ref, dst_ref, et_ref,
             dxs_ref, db_ref, dc_ref, ddt_ref, da_ref, dh_ref):
        @pl.when(pl.program_id(1) == 0)
        def _():
            dh_ref[...] = jnp.zeros_like(dh_ref)

        mask, lo, sub_lo, jl, col, row = _ssd_pieces(reverse, j_heads)
        bb = b_ref[...].astype(BF16)
        cb = c_ref[...].astype(BF16)
        cbt = _dot(cb, bb, _NT)
        dt, gam, gamt, din, dst, et = dt_ref[0], gam_ref[0], gamt_ref[0], din_ref[0], dst_ref[0], et_ref[0]
        dcbt = jnp.zeros((CHUNK, CHUNK), F32)
        db_acc = jnp.zeros((CHUNK, D_STATE), F32)
        dc_acc = jnp.zeros((CHUNK, D_STATE), F32)
        ddt_acc = jnp.zeros((CHUNK, j_heads), F32)
        q_acc = jnp.zeros((CHUNK, j_heads), F32)
        c_acc = jnp.zeros((CHUNK, j_heads), F32)
        t_acc = jnp.zeros((CHUNK, j_heads), F32)
        for p in range(jp):
            ja, jb = 2 * p, 2 * p + 1
            sel = lambda a: jnp.where(lo, col(a, ja), col(a, jb))
            lanes = slice(p * 2 * HEAD_DIM, (p + 1) * 2 * HEAD_DIM)
            x2 = xs_ref[:, lanes]
            dt2, din2, dst2 = sel(dt), sel(din), sel(dst)
            xdt = x2 * dt2
            xb = xdt.astype(BF16)
            dy2 = dy_ref[:, lanes]
            dyb = dy2.astype(BF16)
            h2 = hs_ref[0, 0, p]
            hb = h2.astype(BF16)
            dh_out = dh_ref[p]
            dhb = dh_out.astype(BF16)
            y_off = _dot(cb, hb, _NT) * din2
            dx_off = _dot(bb, dhb, _NT) * dst2
            state_pair = jnp.sum(dh_out * h2, axis=1, keepdims=True)
            t_r = dy2 * y_off
            t_c = xdt * dx_off
            dxs = []
            for idx, j in enumerate((ja, jb)):
                hm = lo if idx == 0 else jnp.logical_not(lo)
                onehot = (jl == j).astype(F32)
                decay = jnp.exp(jnp.where(mask, col(gam, j) - row(gamt, j), NEG))
                m = (cbt * decay).astype(BF16)
                dxs.append(_dot(m, dyb, _TN))
                dy_head = jnp.where(hm, dy2, 0.0).astype(BF16)
                dm = decay * _dot(dy_head, xb, _NT)
                dcbt = dcbt + dm
                e = cbt * dm
                rk = jnp.sum(e, axis=1, keepdims=True) - jnp.sum(e.T, axis=1, keepdims=True)
                r = jnp.sum(jnp.where(hm, t_r, 0.0), axis=1, keepdims=True)
                q_acc = q_acc + (rk + r) * onehot
                c_acc = c_acc + jnp.sum(jnp.where(hm, t_c, 0.0), axis=1, keepdims=True) * onehot
                tz = jnp.sum(jnp.where(sub_lo if idx == 0 else jnp.logical_not(sub_lo), state_pair, 0.0),
                             axis=0, keepdims=True)
                t_acc = t_acc + (col(et, j) * tz) * onehot
            dx2 = jnp.where(lo, dxs[0], dxs[1]) + dx_off
            dyd = (dy2 * din2).astype(BF16)
            xd = (xdt * dst2).astype(BF16)
            dc_acc = dc_acc + _dot(dyd, hb)
            db_acc = db_acc + _dot(xd, dhb)
            dh_ref[p] = dh_out * jnp.where(sub_lo, col(et, ja), col(et, jb)) + _dot(dyd, cb, _TN)
            dxs_ref[:, lanes] = dx2 * dt2
            t_dt = dx2 * x2
            for idx, j in enumerate((ja, jb)):
                hm = lo if idx == 0 else jnp.logical_not(lo)
                ddt_acc = ddt_acc + jnp.sum(jnp.where(hm, t_dt, 0.0), axis=1, keepdims=True) * (jl == j).astype(F32)
        dcb = dcbt.astype(BF16)
        dc_ref[...] = dc_acc + _dot(dcb, bb)
        db_ref[...] = db_acc + _dot(dcb, cb, _TN)
        ddt_ref[0] = ddt_acc
        li = lax.broadcasted_iota(jnp.int32, (CHUNK, CHUNK), 0)
        si = lax.broadcasted_iota(jnp.int32, (CHUNK, CHUNK), 1)
        incl = ((si <= li) if reverse else (si >= li)).astype(F32)
        excl = ((si > li) if reverse else (si < li)).astype(F32)
        hi = lax.Precision.HIGHEST
        da_ref[0] = (jnp.dot(incl, q_acc, preferred_element_type=F32, precision=hi)
                     + jnp.dot(excl, c_acc, preferred_element_type=F32, precision=hi) + t_acc)

    gn = N_GROUPS * D_STATE
    return pl.pallas_call(
        kern, name=name, grid=(N_GROUPS, nc),
        in_specs=[xs_spec, b_spec, c_spec, xs_spec, state_spec, head_spec, head_spec, headt_spec, head_spec,
                  head_spec, head_spec],
        out_specs=[xs_spec, grp_spec, grp_spec, head_spec, head_spec],
        out_shape=[jax.ShapeDtypeStruct((s, d_ssm), F32), jax.ShapeDtypeStruct((s, gn), F32),
                   jax.ShapeDtypeStruct((s, gn), F32), jax.ShapeDtypeStruct((N_GROUPS, s, j_heads), F32),
                   jax.ShapeDtypeStruct((N_GROUPS, s, j_heads), F32)],
        scratch_shapes=[pltpu.VMEM((jp, 2 * HEAD_DIM, D_STATE), F32)],
        compiler_params=_params(("parallel", "arbitrary")),
    )(xbc_c, xbc_c, xbc_c, dy, hs, hd["dt"], hd["gam"], hd["gamT"], hd["din"], hd["dst"], hd["etot"])


def _dt_prepare(dt_raw, dt_bias, a_neg):
    h2 = dt_raw.shape[1]
    n_heads = h2 // 2

    def body(raw, bias, a_head):
        dt = _softplus(raw + bias)
        a = dt * a_head
        li = lax.broadcasted_iota(jnp.int32, (CHUNK, CHUNK), 0)
        si = lax.broadcasted_iota(jnp.int32, (CHUNK, CHUNK), 1)
        tri = (li >= si).astype(F32)
        cs = jnp.dot(tri, a, preferred_element_type=F32, precision=lax.Precision.HIGHEST)
        tot = _sum0(a)
        fwd = lax.broadcasted_iota(jnp.int32, (CHUNK, h2), 1) < n_heads
        gam = jnp.where(fwd, cs, a - cs)
        din = jnp.where(fwd, jnp.exp(cs), jnp.exp(tot + gam))
        dst = jnp.where(fwd, jnp.exp(tot - cs), jnp.exp(cs - a))
        etot = jnp.broadcast_to(jnp.exp(tot), (CHUNK, h2))
        return dt, gam, din, dst, etot

    return _rowwise("dt_prepare", body, [dt_raw], [dt_bias, a_neg], [(h2, F32)] * 5, tile=CHUNK)


def _by_group(arr, n_heads, direction):
    s = arr.shape[0]
    part = arr[:, direction * n_heads:(direction + 1) * n_heads]
    return part.reshape(s, N_GROUPS, n_heads // N_GROUPS).transpose(1, 0, 2)


def _from_group(arr):
    g, s, j = arr.shape
    return arr.transpose(1, 0, 2).reshape(s, g * j)


def _dt_backward(da, ddt, dt, dt_raw, dt_bias, a_neg):
    h2 = da.shape[1]

    def body(dav, ddtv, dtv, raw, bias, a_head):
        draw = (ddtv + dav * a_head) * _sigmoid(raw + bias)
        return draw, _sum0(draw), _sum0(dav * dtv) * a_head
    return _rowwise("dt_backward", body, [da, ddt, dt, dt_raw], [dt_bias, a_neg], [(h2, BF16)], [h2, h2])


def _rms(x):
    r = lax.rsqrt(_mean1(x * x) + EPS)
    return x * r, r


def _rms_bwd(dy, y, r):
    return r * (dy - y * _mean1(dy * y))


def _norm_mod_fwd(name, x, g, sc, sh):
    def body(xv, gv, scv, shv):
        y, _ = _rms(xv)
        return ((y * gv) * (1.0 + scv) + shv,)
    return _rowwise(name, body, [x], [g, sc, sh], [(x.shape[1], BF16)])[0]


def _norm_mod_bwd(name, x, dh, dpass, g, sc):
    d = x.shape[1]

    def body(xv, dhv, dpv, gv, scv):
        y, r = _rms(xv)
        dn = dhv * (1.0 + scv)
        dx = _rms_bwd(dn * gv, y, r) + dpv
        return dx, _sum0(dn * y), _sum0(dhv * (y * gv)), _sum0(dhv)
    return _rowwise(name, body, [x, dh, dpass], [g, sc], [(d, F32)], [d, d, d])


def _gated_residual_fwd(name, x, m, gate, gp):
    def body(xv, mv, gatev, gpv):
        y, _ = _rms(mv)
        return (xv + gatev * (y * gpv),)
    return _rowwise(name, body, [x, m], [gate, gp], [(x.shape[1], F32)])[0]


def _gated_residual_bwd(name, m, dx1, gate, gp):
    d = m.shape[1]

    def body(mv, dv, gatev, gpv):
        y, r = _rms(mv)
        dn = dv * gatev
        return _rms_bwd(dn * gpv, y, r), _sum0(dv * (y * gpv)), _sum0(dn * y)
    return _rowwise(name, body, [m, dx1], [gate, gp], [(d, BF16)], [d, d])


def _final_residual_loss(x1, f, tgt, gate, gp):
    d = x1.shape[1]

    def body(xv, fv, tv, gatev, gpv):
        y, r = _rms(fv)
        n = y * gpv
        err = xv + gatev * n - tv
        dx2 = err * (1.0 / d)
        dn = dx2 * gatev
        sq = jnp.sum(_sum0(err * err), axis=1, keepdims=True)
        return dx2, _rms_bwd(dn * gpv, y, r), jnp.broadcast_to(sq, (1, LANES)), _sum0(dx2 * n), _sum0(dn * y)
    return _rowwise("final_residual_loss", body, [x1, f, tgt], [gate, gp], [(d, F32), (d, BF16)], [LANES, d, d])


def _swiglu_fwd(gu):
    f = gu.shape[1] // 2

    def body(v):
        return (_silu(v[:, :f]) * v[:, f:],)
    return _rowwise("swiglu_fwd", body, [gu], [], [(f, BF16)])[0]


def _swiglu_bwd(gu, dact):
    f = gu.shape[1] // 2

    def body(v, dv):
        gt, up = v[:, :f], v[:, f:]
        return (jnp.concatenate([dv * up * _dsilu(gt), dv * _silu(gt)], axis=1),)
    return _rowwise("swiglu_bwd", body, [gu, dact], [], [(2 * f, BF16)])[0]


def _glu_fwd(glu_in, b_glu):
    c = glu_in.shape[1] // 2

    def body(v, bv):
        t = v + bv
        return (t[:, :c] * _sigmoid(t[:, c:]),)
    return _rowwise("glu_fwd", body, [glu_in], [b_glu], [(c, F32)])[0]


def _glu_bwd(glu_in, b_glu, du0):
    c = glu_in.shape[1] // 2

    def body(v, dv, bv):
        t = v + bv
        a, s = t[:, :c], _sigmoid(t[:, c:])
        dg = jnp.concatenate([dv * s, dv * a * s * (1.0 - s)], axis=1)
        return dg, _sum0(dg)
    return _rowwise("glu_bwd", body, [glu_in, du0], [b_glu], [(2 * c, BF16)], [2 * c])


def _ln_parts(u1):
    xc = u1 - _mean1(u1)
    r = lax.rsqrt(_mean1(xc * xc) + EPS)
    return xc * r, r


def _ln_silu_fwd(u1, ln_g, ln_b):
    def body(v, gv, bv):
        yh, _ = _ln_parts(v)
        return (_silu(yh * gv + bv),)
    return _rowwise("ln_silu_fwd", body, [u1], [ln_g, ln_b], [(u1.shape[1], BF16)])[0]


def _ln_silu_bwd(u1, du, ln_g, ln_b):
    d = u1.shape[1]

    def body(v, dv, gv, bv):
        yh, r = _ln_parts(v)
        dl = dv * _dsilu(yh * gv + bv)
        dyh = dl * gv
        du1 = r * (dyh - _mean1(dyh) - yh * _mean1(dyh * yh))
        return du1, _sum0(dl * yh), _sum0(dl)
    return _rowwise("ln_silu_bwd", body, [u1, du], [ln_g, ln_b], [(d, F32)], [d, d])


def _gate_merge_fwd(y_a, y_b, gl, b_gate):
    d = y_a.shape[1]

    def body(ya, yb, glv, bv):
        s = _sigmoid(glv + bv)
        return (s[:, :d] * ya + s[:, d:] * yb,)
    return _rowwise("gate_merge_fwd", body, [y_a, y_b, gl], [b_gate], [(d, BF16)])[0]


def _gate_merge_bwd(dmixin, y_a, y_b, gl, b_gate):
    d = y_a.shape[1]

    def body(dv, ya, yb, glv, bv):
        s = _sigmoid(glv + bv)
        sa, sb = s[:, :d], s[:, d:]
        dya, dyb = dv * sa, dv * sb
        dgl = jnp.concatenate([dv * ya * sa * (1.0 - sa), dv * yb * sb * (1.0 - sb)], axis=1)
        return dya, dyb, dgl, _sum0(dgl), _sum0(dyb)
    return _rowwise("gate_merge_bwd", body, [dmixin, y_a, y_b, gl], [b_gate],
                    [(d, BF16), (d, BF16), (2 * d, BF16)], [2 * d, d])


def _group_slices(d_ssm):
    gw = d_ssm // N_GROUPS
    return [slice(g * gw, (g + 1) * gw) for g in range(N_GROUPS)]


def _gated_norm_fwd(y_f, y_b, xbc_c, z, d_skip_x, g_ssm):
    d_ssm = y_f.shape[1]

    def body(yf, yb, xs, zv, dsk, gv):
        y = yf + yb + dsk * xs
        v = y * _silu(zv)
        outs = []
        for sl in _group_slices(d_ssm):
            w, _ = _rms(v[:, sl])
            outs.append(w)
        return y, jnp.concatenate(outs, axis=1) * gv
    return _rowwise("gated_norm_fwd", body, [y_f, y_b, (xbc_c, d_ssm, 0), z], [d_skip_x, g_ssm],
                    [(d_ssm, F32), (d_ssm, BF16)])


def _gated_norm_bwd(y, z, dyn, xbc_c, d_skip_x, g_ssm):
    d_ssm = y.shape[1]

    def body(yv, zv, dv, xs, dsk, gv):
        sz = _silu(zv)
        v = yv * sz
        dw = dv * gv
        dvs, ws = [], []
        for sl in _group_slices(d_ssm):
            w, r = _rms(v[:, sl])
            ws.append(w)
            dvs.append(_rms_bwd(dw[:, sl], w, r))
        dvv = jnp.concatenate(dvs, axis=1)
        dy = dvv * sz
        return dy, dvv * yv * _dsilu(zv), _sum0(dv * jnp.concatenate(ws, axis=1)), _sum0(dy * xs)
    return _rowwise("gated_norm_bwd", body, [y, z, dyn, (xbc_c, d_ssm, 0)], [d_skip_x, g_ssm],
                    [(d_ssm, F32), (d_ssm, BF16)], [d_ssm, d_ssm])


def _ssd_grad_merge(dxs_f, dxs_b, dy, db_f, db_b, dc_f, dc_b, d_skip_x):
    d_ssm = dy.shape[1]
    width = d_ssm + 2 * N_GROUPS * D_STATE

    def body(xf, xb, dv, bf, bb, cf, cbv, dsk):
        return (jnp.concatenate([xf + xb + dsk * dv, bf + bb, cf + cbv], axis=1),)
    return _rowwise("ssd_grad_merge", body, [dxs_f, dxs_b, dy, db_f, db_b, dc_f, dc_b], [d_skip_x], [(width, F32)])[0]


def _adamw(name, w, g, m, v):
    c = w.shape[1]
    c1 = 1.0 - ADAM_B1 ** ADAM_STEP
    c2 = 1.0 - ADAM_B2 ** ADAM_STEP

    def body(wv, gv, mv, vv):
        mn = ADAM_B1 * mv + (1.0 - ADAM_B1) * gv
        vn = ADAM_B2 * vv + (1.0 - ADAM_B2) * (gv * gv)
        delta = -ADAM_LR * ((mn / c1) / (jnp.sqrt(vn / c2) + ADAM_EPS) + ADAM_WD * wv)
        return delta, mn, vn
    return _rowwise(name, body, [w, g, m, v], [], [(c, F32)] * 3)


def _local_step(x, tgt, mod, wts, sm):
    s, d = x.shape
    d_ssm = 2 * d
    n_heads = d_ssm // HEAD_DIM
    d_xbc = d_ssm + 2 * N_GROUPS * D_STATE
    sec = [0, d_ssm, d_ssm + d_xbc, d_ssm + d_xbc + 2 * n_heads, d_ssm + d_xbc + 2 * n_heads + 2 * d]
    sec.append(sec[-1] + 2 * d)
    sh1, sc1, g1, sh2, sc2, g2 = [mod[:, i * d:(i + 1) * d] for i in range(N_MOD)]
    win_t = wts["w_in_t"]
    win_secs = [win_t[sec[i]:sec[i + 1]] for i in range(5)]

    h1 = _norm_mod_fwd("pre_mix_norm", x, sm["g_pre_mix"], sc1, sh1)
    z, xbc, dt_raw, glu_in, gate_l = [
        _matmul(f"proj_{nm}", h1, w, tb=True) for nm, w in zip(("z", "xbc", "dt", "glu", "gate"), win_secs)]
    xbc_c = _dwconv_fwd("ssm_conv_fwd", xbc, sm["w_conv_ssm"], sm["b_conv_ssm"], silu=True)
    dt, gam, din, dst, etot = _dt_prepare(dt_raw, sm["dt_bias"], sm["a_neg"])
    heads = []
    for direction in (0, 1):
        hd = {k: _by_group(v, n_heads, direction) for k, v in
              (("dt", dt), ("gam", gam), ("din", din), ("dst", dst), ("etot", etot))}
        hd["gamT"] = hd["gam"].transpose(0, 2, 1)
        heads.append(hd)
    y_f, hs_f = _ssd_fwd("ssd_fwd_f", xbc_c, heads[0], reverse=False, d_ssm=d_ssm)
    y_r, hs_r = _ssd_fwd("ssd_fwd_r", xbc_c, heads[1], reverse=True, d_ssm=d_ssm)
    y_ssd, yn = _gated_norm_fwd(y_f, y_r, xbc_c, z, sm["d_skip_x"], sm["g_ssm_norm"])
    y_a = _matmul("ssm_out", yn, wts["w_ssm_out"])
    u0 = _glu_fwd(glu_in, sm["b_glu"])
    u1 = _dwconv_fwd("dw_conv_fwd", u0, sm["w_dw"], sm["b_dw"], silu=False)
    u = _ln_silu_fwd(u1, sm["ln_g"], sm["ln_b"])
    y_b = _matmul("conv_out", u, wts["w_conv_out"], bias=sm["b_conv_out"])
    mixin = _gate_merge_fwd(y_a, y_b, gate_l, sm["b_gate"])
    mix = _matmul("mix_out", mixin, wts["w_mix_out"])
    x1 = _gated_residual_fwd("post_mix_residual", x, mix, g1, sm["g_post_mix"])
    h2 = _norm_mod_fwd("pre_ffn_norm", x1, sm["g_pre_ffn"], sc2, sh2)
    gu = _matmul("ffn_gate_up", h2, wts["w_gate_up"], b_blocks=N_CHIPS)
    act = _swiglu_fwd(gu)
    f = _matmul("ffn_down", act, wts["w_down"])

    dx2, df, sq, d_g2, d_gpf = _final_residual_loss(x1, f, tgt, g2, sm["g_post_ffn"])
    dact = _matmul("d_act", df, wts["w_down"], tb=True)
    g_w_down = _matmul("g_w_down", act, df, ta=True, out_dtype=BF16)
    dgu = _swiglu_bwd(gu, dact)
    dh2 = _matmul("d_h2", dgu, wts["w_gate_up"], tb=True, b_blocks=N_CHIPS)
    g_w_gate_up = _matmul("g_w_gate_up", h2, dgu, ta=True, out_dtype=BF16, out_blocks=N_CHIPS)
    dx1, d_gpre_ffn, d_sc2, d_sh2 = _norm_mod_bwd("pre_ffn_norm_bwd", x1, dh2, dx2, sm["g_pre_ffn"], sc2)
    dmix, d_g1, d_gpm = _gated_residual_bwd("post_mix_residual_bwd", mix, dx1, g1, sm["g_post_mix"])
    dmixin = _matmul("d_mixin", dmix, wts["w_mix_out"], tb=True)
    g_w_mix = _matmul("g_w_mix_out", mixin, dmix, ta=True, out_dtype=BF16)
    dy_a, dy_b, dgate_l, d_bgate, d_bco = _gate_merge_bwd(dmixin, y_a, y_b, gate_l, sm["b_gate"])
    du = _matmul("d_u", dy_b, wts["w_conv_out"], tb=True)
    g_w_co = _matmul("g_w_conv_out", u, dy_b, ta=True, out_dtype=BF16)
    du1, d_lng, d_lnb = _ln_silu_bwd(u1, du, sm["ln_g"], sm["ln_b"])
    du0, d_wdw, d_bdw = _dwconv_bwd("dw_conv_bwd", u0, sm["w_dw"], sm["b_dw"], du1, silu=False)
    dglu, d_bglu = _glu_bwd(glu_in, sm["b_glu"], du0)
    dyn = _matmul("d_yn", dy_a, wts["w_ssm_out"], tb=True)
    g_w_ssm = _matmul("g_w_ssm_out", yn, dy_a, ta=True, out_dtype=BF16)
    dy_ssd, dz, d_gssm, d_dskip_x = _gated_norm_bwd(y_ssd, z, dyn, xbc_c, sm["d_skip_x"], sm["g_ssm_norm"])
    dxs_f, db_f, dc_f, ddt_f, da_f = _ssd_bwd("ssd_bwd_f", xbc_c, dy_ssd, hs_f, heads[0], reverse=False, d_ssm=d_ssm)
    dxs_r, db_r, dc_r, ddt_r, da_r = _ssd_bwd("ssd_bwd_r", xbc_c, dy_ssd, hs_r, heads[1], reverse=True, d_ssm=d_ssm)
    da = jnp.concatenate([_from_group(da_f), _from_group(da_r)], axis=1)
    ddt = jnp.concatenate([_from_group(ddt_f), _from_group(ddt_r)], axis=1)
    ddt_raw, d_dtbias, d_alog = _dt_backward(da, ddt, dt, dt_raw, sm["dt_bias"], sm["a_neg"])
    dxbc_c = _ssd_grad_merge(dxs_f, dxs_r, dy_ssd, db_f, db_r, dc_f, dc_r, sm["d_skip_x"])
    dxbc, d_wconv, d_bconv = _dwconv_bwd("ssm_conv_bwd", xbc, sm["w_conv_ssm"], sm["b_conv_ssm"], dxbc_c,
                                         silu=True, dx_dtype=BF16)
    dsecs = [dz, dxbc, ddt_raw, dglu, dgate_l]
    dh1 = None
    g_win = []
    for nm, dsec, w in zip(("z", "xbc", "dt", "glu", "gate"), dsecs, win_secs):
        dh1 = _matmul(f"d_h1_{nm}", dsec, w, add=dh1)
        g_win.append(_matmul(f"g_w_in_{nm}", dsec, h1, ta=True, out_dtype=BF16))
    grad_x, d_gpre_mix, d_sc1, d_sh1 = _norm_mod_bwd("pre_mix_norm_bwd", x, dh1, dx1, sm["g_pre_mix"], sc1)

    dmod = jnp.concatenate([d_sh1, d_sc1, d_g1, d_sh2, d_sc2, d_g2], axis=1)
    big = {"w_in_t": jnp.concatenate(g_win, axis=0), "w_ssm_out": g_w_ssm, "w_conv_out": g_w_co,
           "w_mix_out": g_w_mix, "w_gate_up": g_w_gate_up, "w_down": g_w_down}
    small = {"g_pre_mix": d_gpre_mix, "g_post_mix": d_gpm, "w_conv_ssm": d_wconv, "b_conv_ssm": d_bconv,
             "dt_bias": d_dtbias, "a_log": d_alog, "d_skip_x": d_dskip_x, "g_ssm_norm": d_gssm, "b_glu": d_bglu,
             "w_dw": d_wdw, "b_dw": d_bdw, "ln_g": d_lng, "ln_b": d_lnb, "b_conv_out": d_bco, "b_gate": d_bgate,
             "g_pre_ffn": d_gpre_ffn, "g_post_ffn": d_gpf}
    return sq, grad_x, big, small, dmod


ANY = pl.BlockSpec(memory_space=pl.ANY)
WHOLE_VMEM = pl.BlockSpec(memory_space=pltpu.VMEM)


def _mesh_place():
    x, y, c = lax.axis_index("x"), lax.axis_index("y"), lax.axis_index("c")
    other_chips = [(1 - x, y), (x, 1 - y), (1 - x, 1 - y)]
    return x, y, c, other_chips


def _remote(src, dst, send_sems, recv_sems, k, device):
    return pltpu.make_async_remote_copy(src_ref=src, dst_ref=dst, send_sem=send_sems.at[k], recv_sem=recv_sems.at[k],
                                        device_id=device, device_id_type=MESH)


def _gather_devices(name, block):
    m_per, n = block.shape

    def body(x_ref, out_ref, send_sems, recv_sems, local_sem):
        x, y, c, chips = _mesh_place()
        me, sibling = (x, y, c), (x, y, 1 - c)

        def rows(px, py, pc):
            return out_ref.at[pl.ds((4 * px + 2 * py + pc) * m_per, m_per), :]

        def copy(k, blk, to, src=None):
            return _remote(rows(*blk) if src is None else src, rows(*blk), send_sems, recv_sems, k, to)

        mine = pltpu.make_async_copy(x_ref, rows(*me), local_sem)
        mine.start()
        first = [copy(0, me, sibling, src=x_ref)]
        first += [copy(1 + j, me, (*chip, c), src=x_ref) for j, chip in enumerate(chips)]
        for cp in first:
            cp.start()
        passed = [copy(4 + j, (*chip, c), sibling) for j, chip in enumerate(chips)]
        for j, chip in enumerate(chips):
            copy(1 + j, (*chip, c), me).wait_recv()
            passed[j].start()
        copy(0, sibling, me).wait_recv()
        for j, chip in enumerate(chips):
            copy(4 + j, (*chip, 1 - c), me).wait_recv()
        for cp in first + passed:
            cp.wait_send()
        mine.wait()

    return pl.pallas_call(
        body, name=name, out_shape=jax.ShapeDtypeStruct((N_DEV * m_per, n), block.dtype),
        in_specs=[WHOLE_VMEM], out_specs=WHOLE_VMEM,
        scratch_shapes=[pltpu.SemaphoreType.DMA((7,)), pltpu.SemaphoreType.DMA((7,)), pltpu.SemaphoreType.DMA],
        compiler_params=pltpu.CompilerParams(vmem_limit_bytes=VMEM_LIMIT),
    )(block)


def _gather_chips(name, shards):
    n = len(shards)

    def body(*refs):
        ins, outs = refs[:n], refs[n:2 * n]
        send_sems, recv_sems, local_sems = refs[2 * n:]
        x, y, c, chips = _mesh_place()
        me = 2 * x + y
        sibling = (x, y, 1 - c)

        def half(i, h):
            hr = ins[i].shape[0] // 2
            return pl.ds(h * hr, hr)

        local = [pltpu.make_async_copy(ins[i], outs[i].at[me], local_sems.at[i]) for i in range(n)]
        for cp in local:
            cp.start()
        sends = []
        for i in range(n):
            for j, (cx, cy) in enumerate(chips):
                sends.append(_remote(ins[i].at[half(i, c)], outs[i].at[me, half(i, c)], send_sems, recv_sems,
                                     6 * i + j, (cx, cy, c)))
                sends[-1].start()
        for i in range(n):
            for j, (cx, cy) in enumerate(chips):
                blk = outs[i].at[2 * cx + cy, half(i, c)]
                _remote(blk, blk, send_sems, recv_sems, 6 * i + j, (cx, cy, c)).wait_recv()
                sends.append(_remote(blk, blk, send_sems, recv_sems, 6 * i + 3 + j, sibling))
                sends[-1].start()
        for i in range(n):
            for j, (cx, cy) in enumerate(chips):
                blk = outs[i].at[2 * cx + cy, half(i, 1 - c)]
                _remote(blk, blk, send_sems, recv_sems, 6 * i + 3 + j, sibling).wait_recv()
        for cp in sends:
            cp.wait_send()
        for cp in local:
            cp.wait()

    return pl.pallas_call(
        body, name=name, out_shape=[jax.ShapeDtypeStruct((N_CHIPS,) + s.shape, s.dtype) for s in shards],
        in_specs=[ANY] * n, out_specs=[ANY] * n,
        scratch_shapes=[pltpu.SemaphoreType.DMA((6 * n,)), pltpu.SemaphoreType.DMA((6 * n,)),
                        pltpu.SemaphoreType.DMA((n,))],
    )(*shards)


def _send_sibling_halves(name, grads):
    n = len(grads)

    def body(*refs):
        ins, outs = refs[:n], refs[n:2 * n]
        send_sems, recv_sems = refs[2 * n:]
        x, y, c, _ = _mesh_place()
        sibling = (x, y, 1 - c)
        copies = []
        for i in range(n):
            for j in range(N_CHIPS):
                copies.append(_remote(ins[i].at[j, 1 - c], outs[i].at[j], send_sems, recv_sems, N_CHIPS * i + j, sibling))
                copies[-1].start()
        for cp in copies:
            cp.wait_recv()
        for cp in copies:
            cp.wait_send()

    return pl.pallas_call(
        body, name=name,
        out_shape=[jax.ShapeDtypeStruct((g.shape[0],) + g.shape[2:], g.dtype) for g in grads],
        in_specs=[ANY] * n, out_specs=[ANY] * n,
        scratch_shapes=[pltpu.SemaphoreType.DMA((N_CHIPS * n,)), pltpu.SemaphoreType.DMA((N_CHIPS * n,))],
    )(*grads)


def _send_chip_shards(name, sums):
    n = len(sums)

    def body(*refs):
        ins, outs = refs[:n], refs[n:2 * n]
        send_sems, recv_sems = refs[2 * n:]
        x, y, c, chips = _mesh_place()
        copies = []
        for i in range(n):
            for j, (cx, cy) in enumerate(chips):
                copies.append(_remote(ins[i].at[2 * cx + cy], outs[i].at[j], send_sems, recv_sems, 3 * i + j, (cx, cy, c)))
                copies[-1].start()
        for cp in copies:
            cp.wait_recv()
        for cp in copies:
            cp.wait_send()

    return pl.pallas_call(
        body, name=name,
        out_shape=[jax.ShapeDtypeStruct((3,) + g.shape[1:], g.dtype) for g in sums],
        in_specs=[ANY] * n, out_specs=[ANY] * n,
        scratch_shapes=[pltpu.SemaphoreType.DMA((3 * n,)), pltpu.SemaphoreType.DMA((3 * n,))],
    )(*sums)


def _exchange_halves(name, halves):
    n = len(halves)

    def body(*refs):
        ins, outs = refs[:n], refs[n:2 * n]
        send_sems, recv_sems, local_sems = refs[2 * n:]
        x, y, c, _ = _mesh_place()
        sibling = (x, y, 1 - c)
        local = [pltpu.make_async_copy(ins[i], outs[i].at[c], local_sems.at[i]) for i in range(n)]
        remote = [_remote(ins[i], outs[i].at[c], send_sems, recv_sems, i, sibling) for i in range(n)]
        for cp in local + remote:
            cp.start()
        for i in range(n):
            _remote(ins[i], outs[i].at[1 - c], send_sems, recv_sems, i, sibling).wait_recv()
        for cp in remote:
            cp.wait_send()
        for cp in local:
            cp.wait()

    return pl.pallas_call(
        body, name=name,
        out_shape=[jax.ShapeDtypeStruct((2,) + h.shape, h.dtype) for h in halves],
        in_specs=[ANY] * n, out_specs=[ANY] * n,
        scratch_shapes=[pltpu.SemaphoreType.DMA((n,)), pltpu.SemaphoreType.DMA((n,)), pltpu.SemaphoreType.DMA((n,))],
    )(*halves)


def _divisor_tile(rows, row_bytes, quantum=16):
    best = rows
    for t in range(quantum, rows + 1, quantum):
        if rows % t == 0 and 2 * t * row_bytes <= ROW_TILE_BUDGET:
            best = t
    return best


def _add_sibling(name, g4, t1, core):
    nb, _, hr, cols = g4.shape
    t = _divisor_tile(hr, cols * 6)

    def kern(core_ref, g_ref, t_ref, o_ref):
        o_ref[0] = (g_ref[0, 0].astype(F32) + t_ref[0].astype(F32)).astype(o_ref.dtype)

    return pl.pallas_call(
        kern, name=name,
        grid_spec=pltpu.PrefetchScalarGridSpec(
            num_scalar_prefetch=1, grid=(nb, hr // t),
            in_specs=[pl.BlockSpec((1, 1, t, cols), lambda j, i, core_ref: (j, core_ref[0], i, 0)),
                      pl.BlockSpec((1, t, cols), lambda j, i, core_ref: (j, i, 0))],
            out_specs=pl.BlockSpec((1, t, cols), lambda j, i, core_ref: (j, i, 0))),
        out_shape=jax.ShapeDtypeStruct((nb, hr, cols), g4.dtype),
        compiler_params=_params(("parallel", "parallel")),
    )(core, g4, t1)


def _add_chips(name, s1, t3, chip):
    _, hr, cols = s1.shape
    t = _divisor_tile(hr, cols * 12)

    def kern(chip_ref, s_ref, t_ref, o_ref):
        acc = s_ref[0].astype(F32)
        for j in range(3):
            acc = acc + t_ref[j].astype(F32)
        o_ref[...] = acc

    return pl.pallas_call(
        kern, name=name,
        grid_spec=pltpu.PrefetchScalarGridSpec(
            num_scalar_prefetch=1, grid=(hr // t,),
            in_specs=[pl.BlockSpec((1, t, cols), lambda i, chip_ref: (chip_ref[0], i, 0)),
                      pl.BlockSpec((3, t, cols), lambda i, chip_ref: (0, i, 0))],
            out_specs=pl.BlockSpec((t, cols), lambda i, chip_ref: (i, 0))),
        out_shape=jax.ShapeDtypeStruct((hr, cols), F32),
        compiler_params=_params(("parallel",)),
    )(chip, s1, t3)


def _reduce_scatter(grads, core, chip):
    g4 = [g.reshape(N_CHIPS, 2, g.shape[1] // 2, g.shape[2]) for g in grads]
    t1 = _send_sibling_halves("grads_to_sibling", g4)
    s1 = [_add_sibling(f"chip_sum_{i}", g, t, core) for i, (g, t) in enumerate(zip(g4, t1))]
    t3 = _send_chip_shards("grads_to_chips", s1)
    halves = [_add_chips(f"shard_sum_{i}", s, t, chip) for i, (s, t) in enumerate(zip(s1, t3))]
    full = _exchange_halves("grad_halves_to_sibling", halves)
    return [f.reshape(f.shape[1] * 2, f.shape[2]) for f in full]


def _pack(arrays, width):
    parts = []
    for a in arrays:
        flat = a.reshape(-1).astype(F32)
        pad = (-flat.shape[0]) % width
        parts.append(jnp.pad(flat, (0, pad)).reshape(-1, width))
    rows = sum(p.shape[0] for p in parts)
    parts.append(jnp.zeros(((-rows) % 8, width), F32))
    return jnp.concatenate(parts, axis=0)


def _unpack(block, shapes, width):
    out, r = [], 0
    for shp in shapes:
        size = 1
        for s_ in shp:
            size *= s_
        rows = -(-size // width)
        out.append(block[r:r + rows].reshape(-1)[:size].reshape(shp))
        r += rows
    return out


SMALL_PARAMS = ("b_ada", "g_pre_mix", "g_post_mix", "b_conv_ssm", "dt_bias_fwd", "dt_bias_bwd", "a_log_fwd", "a_log_bwd",
                "d_skip", "g_ssm_norm", "b_glu", "b_dw", "ln_g", "ln_b", "b_conv_out", "b_gate", "g_pre_ffn", "g_post_ffn")
SHARDED_SMALL = ("w_conv_ssm", "w_dw")
MATRICES = ("w_in", "w_ssm_out", "w_conv_out", "w_mix_out", "w_gate_up", "w_down")
ALL_PARAMS = ("w_ada", "b_ada", "g_pre_mix", "g_post_mix", "w_in", "w_conv_ssm", "b_conv_ssm", "dt_bias_fwd", "dt_bias_bwd",
              "a_log_fwd", "a_log_bwd", "d_skip", "g_ssm_norm", "w_ssm_out", "b_glu", "w_dw", "b_dw", "ln_g", "ln_b",
              "w_conv_out", "b_conv_out", "b_gate", "w_mix_out", "g_pre_ffn", "g_post_ffn", "w_gate_up", "w_down")
COND_ROWS = 48
COND_CONV_ROW = 8
COND_DW_ROW = 16
MOD_ROWS = 16


def kernel(x, c, w_ada, b_ada, g_pre_mix, g_post_mix, w_in, w_conv_ssm, b_conv_ssm, dt_bias_fwd, dt_bias_bwd, a_log_fwd, a_log_bwd, d_skip, g_ssm_norm, w_ssm_out, b_glu, w_dw, b_dw, ln_g, ln_b, w_conv_out, b_conv_out, b_gate, w_mix_out, g_pre_ffn, g_post_ffn, w_gate_up, w_down, loss_target, m_w_ada, m_b_ada, m_g_pre_mix, m_g_post_mix, m_w_in, m_w_conv_ssm, m_b_conv_ssm, m_dt_bias_fwd, m_dt_bias_bwd, m_a_log_fwd, m_a_log_bwd, m_d_skip, m_g_ssm_norm, m_w_ssm_out, m_b_glu, m_w_dw, m_b_dw, m_ln_g, m_ln_b, m_w_conv_out, m_b_conv_out, m_b_gate, m_w_mix_out, m_g_pre_ffn, m_g_post_ffn, m_w_gate_up, m_w_down, v_w_ada, v_b_ada, v_g_pre_mix, v_g_post_mix, v_w_in, v_w_conv_ssm, v_b_conv_ssm, v_dt_bias_fwd, v_dt_bias_bwd, v_a_log_fwd, v_a_log_bwd, v_d_skip, v_g_ssm_norm, v_w_ssm_out, v_b_glu, v_w_dw, v_b_dw, v_ln_g, v_ln_b, v_w_conv_out, v_b_conv_out, v_b_gate, v_w_mix_out, v_g_pre_ffn, v_g_post_ffn, v_w_gate_up, v_w_down):
    given = dict(locals())
    wgt = {n: given[n][0] for n in ALL_PARAMS}
    mom = {n: given["m_" + n][0] for n in ALL_PARAMS}
    var = {n: given["v_" + n][0] for n in ALL_PARAMS}
    xs, tgt = x[0], loss_target[0]
    s, d = xs.shape
    d_ssm = 2 * d
    n_heads = d_ssm // HEAD_DIM
    d_xbc = d_ssm + 2 * N_GROUPS * D_STATE
    xi, yi, ci = lax.axis_index("x"), lax.axis_index("y"), lax.axis_index("c")
    chip = 2 * xi + yi
    dev = 2 * chip + ci
    core_arr = jnp.reshape(ci, (1,)).astype(jnp.int32)
    chip_arr = jnp.reshape(chip, (1,)).astype(jnp.int32)
    k_conv, k_dw = wgt["w_conv_ssm"].shape[0], wgt["w_dw"].shape[0]
    xbc_shard, dw_shard = d_xbc // N_CHIPS, d // N_CHIPS

    width1 = max(d, xbc_shard)
    blk = jnp.zeros((COND_ROWS, width1), F32)
    blk = blk.at[0, :d].set(c[0])
    blk = blk.at[COND_CONV_ROW:COND_CONV_ROW + k_conv, :xbc_shard].set(wgt["w_conv_ssm"])
    blk = blk.at[COND_DW_ROW:COND_DW_ROW + k_dw, :dw_shard].set(wgt["w_dw"])
    g1 = _gather_devices("gather_cond", blk).reshape(N_DEV, COND_ROWS, width1)
    c_all = g1[:, 0, :d]
    w_conv_full = jnp.concatenate([g1[2 * k, COND_CONV_ROW:COND_CONV_ROW + k_conv, :xbc_shard] for k in range(N_CHIPS)], axis=1)
    w_dw_full = jnp.concatenate([g1[2 * k, COND_DW_ROW:COND_DW_ROW + k_dw, :dw_shard] for k in range(N_CHIPS)], axis=1)
    c_act = jnp.pad(c_all * _sigmoid(c_all), ((0, MOD_ROWS - N_DEV), (0, 0)))

    mod_part = _matmul("ada_mod", c_act, wgt["w_ada"])
    g2 = _gather_devices("gather_mod", mod_part).reshape(N_DEV, MOD_ROWS, mod_part.shape[1])
    mod_all = jnp.concatenate([g2[2 * k, :N_DEV] for k in range(N_CHIPS)], axis=1) + wgt["b_ada"][None]
    mod = lax.dynamic_slice_in_dim(mod_all, dev, 1, axis=0)

    shards = [wgt["w_in"].T.astype(BF16)] + [wgt[n].astype(BF16) for n in MATRICES[1:]]
    gathered = _gather_chips("gather_weights", shards)
    wts = {"w_in_t": gathered[0].reshape(-1, d), "w_ssm_out": gathered[1].reshape(-1, d),
           "w_conv_out": gathered[2].reshape(-1, d), "w_mix_out": gathered[3].reshape(-1, d),
           "w_gate_up": gathered[4], "w_down": gathered[5].reshape(-1, d)}
    row = lambda v: v.reshape(1, -1)
    sm = {"g_pre_mix": row(wgt["g_pre_mix"]), "g_post_mix": row(wgt["g_post_mix"]), "w_conv_ssm": w_conv_full,
          "b_conv_ssm": row(wgt["b_conv_ssm"]),
          "dt_bias": row(jnp.concatenate([wgt["dt_bias_fwd"], wgt["dt_bias_bwd"]])),
          "a_neg": row(-jnp.exp(jnp.concatenate([wgt["a_log_fwd"], wgt["a_log_bwd"]]))),
          "d_skip_x": row(jnp.repeat(wgt["d_skip"], HEAD_DIM)), "g_ssm_norm": row(wgt["g_ssm_norm"]),
          "b_glu": row(wgt["b_glu"]), "w_dw": w_dw_full, "b_dw": row(wgt["b_dw"]), "ln_g": row(wgt["ln_g"]),
          "ln_b": row(wgt["ln_b"]), "b_conv_out": row(wgt["b_conv_out"]), "b_gate": row(wgt["b_gate"]),
          "g_pre_ffn": row(wgt["g_pre_ffn"]), "g_post_ffn": row(wgt["g_post_ffn"])}

    sq, grad_x, big, small, dmod = _local_step(xs, tgt, mod, wts, sm)
    loss = lax.psum((0.5 / d) * sq[0, 0], ("x", "y", "c"))

    local_small = {"b_ada": dmod, "g_pre_mix": small["g_pre_mix"], "g_post_mix": small["g_post_mix"],
                   "b_conv_ssm": small["b_conv_ssm"], "dt_bias_fwd": small["dt_bias"][:, :n_heads],
                   "dt_bias_bwd": small["dt_bias"][:, n_heads:], "a_log_fwd": small["a_log"][:, :n_heads],
                   "a_log_bwd": small["a_log"][:, n_heads:],
                   "d_skip": jnp.sum(small["d_skip_x"].reshape(n_heads, HEAD_DIM), axis=1),
                   "g_ssm_norm": small["g_ssm_norm"], "b_glu": small["b_glu"], "b_dw": small["b_dw"],
                   "ln_g": small["ln_g"], "ln_b": small["ln_b"], "b_conv_out": small["b_conv_out"],
                   "b_gate": small["b_gate"], "g_pre_ffn": small["g_pre_ffn"], "g_post_ffn": small["g_post_ffn"],
                   "w_conv_ssm": small["w_conv_ssm"], "w_dw": small["w_dw"]}
    names = SMALL_PARAMS + SHARDED_SMALL
    pack = _pack([local_small[n] for n in names], d)
    rows_p = pack.shape[0]
    g3 = _gather_devices("gather_small_grads", pack).reshape(N_DEV, rows_p, d)
    total = _rowwise("sum_small_grads", lambda *blocks: (functools.reduce(lambda a, b: a + b, blocks),),
                     [g3[i] for i in range(N_DEV)], [], [(d, F32)])[0]
    full_shapes = [wgt[n].shape for n in SMALL_PARAMS] + [(k_conv, d_xbc), (k_dw, d)]
    summed = dict(zip(names, _unpack(total, full_shapes, d)))
    grads = {n: summed[n] for n in SMALL_PARAMS}
    grads["w_conv_ssm"] = lax.dynamic_slice_in_dim(summed["w_conv_ssm"], chip * xbc_shard, xbc_shard, axis=1)
    grads["w_dw"] = lax.dynamic_slice_in_dim(summed["w_dw"], chip * dw_shard, dw_shard, axis=1)

    dmod_all = g3[:, :N_MOD, :].reshape(N_DEV, N_MOD * d)
    ada_cols = wgt["w_ada"].shape[1]
    dmod_cols = jnp.pad(lax.dynamic_slice_in_dim(dmod_all, chip * ada_cols, ada_cols, axis=1),
                        ((0, MOD_ROWS - N_DEV), (0, 0)))
    grads["w_ada"] = _matmul("g_w_ada", c_act, dmod_cols, ta=True)

    big_list = [big["w_in_t"].reshape(N_CHIPS, -1, d)] + [
        big[n] if n == "w_gate_up" else big[n].reshape(N_CHIPS, -1, d) for n in MATRICES[1:]]
    reduced = _reduce_scatter(big_list, core_arr, chip_arr)
    grads["w_in"] = reduced[0].T
    for n, g in zip(MATRICES[1:], reduced[1:]):
        grads[n] = g

    delta, new_m, new_v = {}, {}, {}
    for n in ("w_ada",) + MATRICES:
        delta[n], new_m[n], new_v[n] = _adamw("adamw_" + n, wgt[n], grads[n], mom[n], var[n])
    for group, width, tag in ((SMALL_PARAMS, d, "small"), (SHARDED_SMALL, LANES, "conv")):
        shapes = [wgt[n].shape for n in group]
        packs = [_pack([src[n] for n in group], width) for src in (wgt, grads, mom, var)]
        outs = _adamw("adamw_" + tag, *packs)
        for res, o in zip((delta, new_m, new_v), outs):
            res.update(zip(group, _unpack(o, shapes, width)))

    lead = lambda a: a[None]
    return (loss, grad_x[None], *[lead(grads[n]) for n in ALL_PARAMS], *[lead(delta[n]) for n in ALL_PARAMS],
            *[lead(new_m[n]) for n in ALL_PARAMS], *[lead(new_v[n]) for n in ALL_PARAMS])
```

```python
import functools

import jax
import jax.numpy as jnp
from jax import lax
from jax.experimental import pallas as pl
from jax.experimental.pallas import tpu as pltpu

F32 = jnp.float32
BF16 = jnp.bfloat16

N_GROUPS = 8
HEAD_DIM = 64
D_STATE = 128
CHUNK = 128
EPS = 1e-6
N_MOD = 6
ADAM_LR = 0.001
ADAM_B1 = 0.9
ADAM_B2 = 0.999
ADAM_EPS = 1e-08
ADAM_WD = 0.01
ADAM_STEP = 10

V7X_VMEM_BYTES = 64 * 1024 * 1024
VMEM_LIMIT = V7X_VMEM_BYTES - 8 * 1024 * 1024
ROW_TILE_BUDGET = 20 * 1024 * 1024
LANES = 128
NEG = -1e30
MESH = pl.DeviceIdType.MESH
N_CHIPS = 4
N_DEV = 8


def _params(sem):
    return pltpu.CompilerParams(dimension_semantics=sem, vmem_limit_bytes=VMEM_LIMIT)


def _sigmoid(x):
    return 1.0 / (1.0 + jnp.exp(-x))


def _silu(x):
    return x * _sigmoid(x)


def _dsilu(x):
    s = _sigmoid(x)
    return s * (1.0 + x * (1.0 - s))


def _softplus(x):
    return jnp.maximum(x, 0.0) + jnp.log(1.0 + jnp.exp(-jnp.abs(x)))


def _sum0(a):
    return jnp.sum(a, axis=0, keepdims=True)


def _mean1(a):
    return jnp.mean(a, axis=1, keepdims=True)


def _rowwise(name, body, rows, bcasts, out_rows, out_accs=(), tile=None):
    rows = [r if isinstance(r, tuple) else (r, r.shape[1], 0) for r in rows]
    s = rows[0][0].shape[0]
    if tile is None:
        per_row = sum(w * a.dtype.itemsize for a, w, _ in rows) + sum(w * jnp.dtype(dt).itemsize for w, dt in out_rows)
        tile = 1024
        while tile > 16 and (tile * per_row * 2 > ROW_TILE_BUDGET or s % tile):
            tile //= 2
        if s * per_row * 2 <= ROW_TILE_BUDGET:
            tile = s
    assert s % tile == 0
    n_in = len(rows) + len(bcasts)
    n_o = len(out_rows)

    def kern(*refs):
        res = body(*[r[...] for r in refs[:n_in]])
        outs = refs[n_in:]
        for o, v in zip(outs[:n_o], res[:n_o]):
            o[...] = v.astype(o.dtype)
        if out_accs:
            @pl.when(pl.program_id(0) == 0)
            def _():
                for o in outs[n_o:]:
                    o[...] = jnp.zeros_like(o)
            for o, v in zip(outs[n_o:], res[n_o:]):
                o[...] += v

    in_specs = [pl.BlockSpec((tile, w), functools.partial(lambda i, cb: (i, cb), cb=cb)) for _, w, cb in rows]
    in_specs += [pl.BlockSpec(b.shape, functools.partial(lambda i, nd: (0,) * nd, nd=b.ndim)) for b in bcasts]
    out_shape = [jax.ShapeDtypeStruct((s, w), dt) for w, dt in out_rows]
    out_shape += [jax.ShapeDtypeStruct((1, w), F32) for w in out_accs]
    out_specs = [pl.BlockSpec((tile, w), lambda i: (i, 0)) for w, _ in out_rows]
    out_specs += [pl.BlockSpec((1, w), lambda i: (0, 0)) for w in out_accs]
    return pl.pallas_call(
        kern, name=name, grid=(s // tile,), in_specs=in_specs, out_specs=out_specs, out_shape=out_shape,
        compiler_params=_params(("arbitrary",) if out_accs else ("parallel",)),
    )(*[a for a, _, _ in rows], *bcasts)


def _tile(n, pref):
    if n <= pref:
        return n
    t = (pref // LANES) * LANES
    while t >= LANES:
        if n % t == 0:
            return t
        t -= LANES
    return n


def _matmul(name, a, b, *, ta=False, tb=False, out_dtype=F32, bias=None, add=None, b_blocks=1, out_blocks=1,
            tm=1024, tn=1408, tk=2048):
    m, k = (a.shape[1], a.shape[0]) if ta else a.shape
    if b_blocks > 1:
        rows_b, cols_b = b.shape[1], b.shape[2] * b_blocks
    else:
        rows_b, cols_b = b.shape
    n, kb = (rows_b, cols_b) if tb else (cols_b, rows_b)
    assert k == kb, (name, a.shape, b.shape)
    tm, tn, tk = _tile(m, tm), _tile(n, tn), _tile(k, tk)
    if b_blocks > 1:
        per = cols_b // b_blocks
        if tb:
            tk = _tile(per, tk)
        else:
            tn = _tile(per, tn)
    if out_blocks > 1:
        tn = _tile(n // out_blocks, tn)
    nk = k // tk
    grid = (m // tm, n // tn, nk)

    a_spec = pl.BlockSpec((tk, tm), lambda i, j, kk: (kk, i)) if ta else pl.BlockSpec((tm, tk), lambda i, j, kk: (i, kk))
    if b_blocks > 1:
        if tb:
            nb = per // tk
            b_spec = pl.BlockSpec((1, tn, tk), lambda i, j, kk: (kk // nb, j, kk % nb))
        else:
            nb = per // tn
            b_spec = pl.BlockSpec((1, tk, tn), lambda i, j, kk: (j // nb, kk, j % nb))
    else:
        b_spec = pl.BlockSpec((tn, tk), lambda i, j, kk: (j, kk)) if tb else pl.BlockSpec((tk, tn), lambda i, j, kk: (kk, j))
    in_specs = [a_spec, b_spec]
    operands = [a, b]
    if bias is not None:
        in_specs.append(pl.BlockSpec((1, tn), lambda i, j, kk: (0, j)))
        operands.append(bias)
    if add is not None:
        in_specs.append(pl.BlockSpec((tm, tn), lambda i, j, kk: (i, j)))
        operands.append(add)
    if out_blocks > 1:
        nbo = (n // out_blocks) // tn
        out_spec = pl.BlockSpec((1, tm, tn), lambda i, j, kk: (j // nbo, i, j % nbo))
        out_shape = jax.ShapeDtypeStruct((out_blocks, m, n // out_blocks), out_dtype)
    else:
        out_spec = pl.BlockSpec((tm, tn), lambda i, j, kk: (i, j))
        out_shape = jax.ShapeDtypeStruct((m, n), out_dtype)
    dims = (((0 if ta else 1,), (1 if tb else 0,)), ((), ()))
    has_bias, has_add = bias is not None, add is not None

    def kern(*refs):
        a_ref, b_ref = refs[0], refs[1]
        pos = 2
        bias_ref = add_ref = None
        if has_bias:
            bias_ref = refs[pos]
            pos += 1
        if has_add:
            add_ref = refs[pos]
            pos += 1
        o_ref = refs[pos]
        acc_ref = refs[pos + 1] if nk > 1 else None
        av = a_ref[...].astype(BF16)
        bv = (b_ref[0] if b_blocks > 1 else b_ref[...]).astype(BF16)
        p = lax.dot_general(av, bv, dims, preferred_element_type=F32)

        def finish(acc):
            if has_bias:
                acc = acc + bias_ref[...]
            if has_add:
                acc = acc + add_ref[...]
            if out_blocks > 1:
                o_ref[0] = acc.astype(o_ref.dtype)
            else:
                o_ref[...] = acc.astype(o_ref.dtype)

        if nk == 1:
            finish(p)
        else:
            kk = pl.program_id(2)

            @pl.when(kk == 0)
            def _():
                acc_ref[...] = p

            @pl.when(kk > 0)
            def _():
                acc_ref[...] += p

            @pl.when(kk == nk - 1)
            def _():
                finish(acc_ref[...])

    return pl.pallas_call(
        kern, name=name, grid=grid, in_specs=in_specs, out_specs=out_spec, out_shape=out_shape,
        scratch_shapes=[pltpu.VMEM((tm, tn), F32)] if nk > 1 else [],
        compiler_params=_params(("parallel", "parallel", "arbitrary")),
    )(*operands)


CONV_HALO = 16
CONV_ROWS = 256


def _shifted(win, s, rows):
    n = win.shape[0]
    return (pltpu.roll(win, (n - s) % n, axis=0) if s % n else win)[:rows]


def _dwconv_fwd(name, x, w, b, *, silu, out_dtype=F32):
    s, c = x.shape
    k = w.shape[0]
    pad = (k - 1) // 2
    assert pad <= CONV_HALO and c % LANES == 0
    t = min(CONV_ROWS, s)
    n_chunks = s // t

    def kern(x_ref, w_ref, b_ref, o_ref, xp_ref):
        zeros = jnp.zeros((CONV_HALO, LANES), F32)
        xp_ref[0:CONV_HALO, :] = zeros
        xp_ref[CONV_HALO + s:CONV_HALO + s + CONV_HALO, :] = zeros
        xp_ref[CONV_HALO:CONV_HALO + s, :] = x_ref[...]
        bv = b_ref[...]

        def chunk(i, carry):
            base = pl.multiple_of(i * t, 8)
            win = xp_ref[pl.ds(base, t + 2 * CONV_HALO), :]
            acc = jnp.zeros((t, LANES), F32)
            for j in range(k):
                acc = acc + _shifted(win, CONV_HALO - pad + j, t) * w_ref[pl.ds(j, 1), :]
            acc = acc + bv
            o_ref[pl.ds(base, t), :] = (_silu(acc) if silu else acc).astype(o_ref.dtype)
            return carry

        lax.fori_loop(0, n_chunks, chunk, 0)

    return pl.pallas_call(
        kern, name=name, grid=(c // LANES,),
        in_specs=[pl.BlockSpec((s, LANES), lambda i: (0, i)), pl.BlockSpec((k, LANES), lambda i: (0, i)),
                  pl.BlockSpec((1, LANES), lambda i: (0, i))],
        out_specs=pl.BlockSpec((s, LANES), lambda i: (0, i)),
        out_shape=jax.ShapeDtypeStruct((s, c), out_dtype),
        scratch_shapes=[pltpu.VMEM((s + 2 * CONV_HALO, LANES), F32)],
        compiler_params=_params(("parallel",)),
    )(x, w, b)


def _dwconv_bwd(name, x, w, b, dout, *, silu, dx_dtype=F32):
    s, c = x.shape
    k = w.shape[0]
    pad = (k - 1) // 2
    t = min(CONV_ROWS, s)
    n_chunks = s // t

    def kern(x_ref, w_ref, b_ref, do_ref, dx_ref, dw_ref, db_ref, xp_ref, dp_ref):
        zeros = jnp.zeros((CONV_HALO, LANES), F32)
        for ref in (xp_ref, dp_ref):
            ref[0:CONV_HALO, :] = zeros
            ref[CONV_HALO + s:CONV_HALO + s + CONV_HALO, :] = zeros
        xp_ref[CONV_HALO:CONV_HALO + s, :] = x_ref[...]
        bv = b_ref[...]
        dw_ref[...] = jnp.zeros_like(dw_ref)

        def pre_chunk(i, dbias):
            base = pl.multiple_of(i * t, 8)
            win = xp_ref[pl.ds(base, t + 2 * CONV_HALO), :]
            dpre = do_ref[pl.ds(base, t), :].astype(F32)
            if silu:
                acc = jnp.zeros((t, LANES), F32)
                for j in range(k):
                    acc = acc + _shifted(win, CONV_HALO - pad + j, t) * w_ref[pl.ds(j, 1), :]
                dpre = dpre * _dsilu(acc + bv)
            dp_ref[pl.ds(base + CONV_HALO, t), :] = dpre
            for j in range(k):
                dw_ref[pl.ds(j, 1), :] += _sum0(dpre * _shifted(win, CONV_HALO - pad + j, t))
            return dbias + _sum0(dpre)

        db_ref[...] = lax.fori_loop(0, n_chunks, pre_chunk, jnp.zeros((1, LANES), F32))

        def dx_chunk(i, carry):
            base = pl.multiple_of(i * t, 8)
            win = dp_ref[pl.ds(base, t + 2 * CONV_HALO), :]
            acc = jnp.zeros((t, LANES), F32)
            for j in range(k):
                acc = acc + _shifted(win, CONV_HALO + pad - j, t) * w_ref[pl.ds(j, 1), :]
            dx_ref[pl.ds(base, t), :] = acc.astype(dx_ref.dtype)
            return carry

        lax.fori_loop(0, n_chunks, dx_chunk, 0)

    col = lambda rows: pl.BlockSpec((rows, LANES), lambda i: (0, i))
    return pl.pallas_call(
        kern, name=name, grid=(c // LANES,),
        in_specs=[col(s), col(k), col(1), col(s)],
        out_specs=[col(s), col(k), col(1)],
        out_shape=[jax.ShapeDtypeStruct((s, c), dx_dtype), jax.ShapeDtypeStruct((k, c), F32),
                   jax.ShapeDtypeStruct((1, c), F32)],
        scratch_shapes=[pltpu.VMEM((s + 2 * CONV_HALO, LANES), F32), pltpu.VMEM((s + 2 * CONV_HALO, LANES), F32)],
        compiler_params=_params(("parallel",)),
    )(x, w, b, dout)


_NT =(((1,), (1,)), ((), ()))
_TN = (((0,), (0,)), ((), ()))


def _dot(a, b, dims=None):
    if dims is None:
        return jnp.dot(a, b, preferred_element_type=F32)
    return lax.dot_general(a, b, dims, preferred_element_type=F32)


HEAD_QUANTITIES = ("dt", "gam", "din", "dst")


def _split3(a):
    hi = lax.reduce_precision(a, 8, 7)
    mid = lax.reduce_precision(a - hi, 8, 7)
    lo = lax.reduce_precision(a - hi - mid, 8, 7)
    return hi.astype(BF16), mid.astype(BF16), lo.astype(BF16)


def _head_inputs(per_head, n_heads, direction):
    j_heads = n_heads // N_GROUPS
    assert 12 * j_heads <= LANES
    grp = {k: _by_group(v, n_heads, direction) for k, v in per_head.items()}
    pieces = [p for k in HEAD_QUANTITIES for p in _split3(grp[k])]
    s = pieces[0].shape[1]
    q = jnp.concatenate(pieces + [jnp.zeros((N_GROUPS, s, LANES - 12 * j_heads), BF16)], axis=2)
    gw = j_heads * HEAD_DIM
    et_chunks = grp["etot"][:, ::CHUNK, :]
    etot_x = jnp.broadcast_to(jnp.repeat(et_chunks, HEAD_DIM, axis=2)[:, :, None, :], (N_GROUPS, s // CHUNK, 8, gw))
    r = jnp.arange(LANES)[:, None]

    def expand(quantity, width):
        head_of_lane = jnp.arange(j_heads * width)[None] // width
        hit = (r // (3 * j_heads) == quantity) & (r % j_heads == head_of_lane) & (r < 12 * j_heads)
        return hit.astype(BF16)

    sel_cols = (jnp.arange(gw)[:, None] // HEAD_DIM == jnp.arange(LANES)[None]).astype(BF16)
    etot_rows = jnp.broadcast_to(jnp.pad(et_chunks, ((0, 0), (0, 0), (0, LANES - j_heads)))[:, :, None, :],
                                 (N_GROUPS, s // CHUNK, 8, LANES))
    return {"q": q, "gamT": grp["gam"].transpose(0, 2, 1), "etot_x": etot_x, "etot_rows": etot_rows,
            "ex_dt": expand(0, HEAD_DIM), "ex_gam": expand(1, CHUNK), "ex_din": expand(2, HEAD_DIM),
            "ex_dst": expand(3, HEAD_DIM), "sel_cols": sel_cols}


def _scan_specs(reverse_order, nc, j_heads, d_ssm):
    gw = j_heads * HEAD_DIM
    b_off = d_ssm // D_STATE
    c_off = b_off + N_GROUPS
    zz = (lambda z: nc - 1 - z) if reverse_order else (lambda z: z)
    const = lambda shape: pl.BlockSpec(shape, lambda g, z: (0,) * len(shape))
    return {
        "xs": pl.BlockSpec((CHUNK, gw), lambda g, z: (zz(z), g)),
        "b": pl.BlockSpec((CHUNK, D_STATE), lambda g, z: (zz(z), b_off + g)),
        "c": pl.BlockSpec((CHUNK, D_STATE), lambda g, z: (zz(z), c_off + g)),
        "q": pl.BlockSpec((1, CHUNK, LANES), lambda g, z: (g, zz(z), 0)),
        "head_t": pl.BlockSpec((1, j_heads, CHUNK), lambda g, z: (g, 0, zz(z))),
        "etot_x": pl.BlockSpec((1, 1, 8, gw), lambda g, z: (g, zz(z), 0, 0)),
        "state": pl.BlockSpec((1, 1, D_STATE, gw), lambda g, z: (g, zz(z), 0, 0)),
        "grp": pl.BlockSpec((CHUNK, D_STATE), lambda g, z: (zz(z), g)),
        "ex": const((LANES, gw)), "ex_gam": const((LANES, j_heads * CHUNK)), "sel": const((gw, LANES)),
        "etot_rows": pl.BlockSpec((1, 1, 8, LANES), lambda g, z: (g, zz(z), 0, 0)),
        "head_cols": pl.BlockSpec((1, CHUNK, LANES), lambda g, z: (g, zz(z), 0)),
    }


def _scan_masks(reverse):
    li = lax.broadcasted_iota(jnp.int32, (CHUNK, CHUNK), 0)
    si = lax.broadcasted_iota(jnp.int32, (CHUNK, CHUNK), 1)
    mask = (li <= si) if reverse else (li >= si)
    mask_t = (si <= li) if reverse else (si >= li)
    return li, si, mask, mask_t, si < HEAD_DIM


def _scan_fwd(name, xbc_c, hi, *, reverse, d_ssm):
    s = xbc_c.shape[0]
    nc = s // CHUNK
    j_heads = hi["gamT"].shape[1]
    gw = j_heads * HEAD_DIM
    sp = _scan_specs(reverse, nc, j_heads, d_ssm)

    def kern(xs_ref, b_ref, c_ref, q_ref, gamt_ref, etx_ref, exdt_ref, exgam_ref, exdin_ref, exdst_ref,
             y_ref, hs_ref, h_ref):
        @pl.when(pl.program_id(1) == 0)
        def _():
            h_ref[...] = jnp.zeros_like(h_ref)

        _, _, mask, _, lo = _scan_masks(reverse)
        bb = b_ref[...].astype(BF16)
        cb = c_ref[...].astype(BF16)
        cbt = _dot(cb, bb, _NT)
        q = q_ref[0]
        dtx, dinx, dstx = _dot(q, exdt_ref[...]), _dot(q, exdin_ref[...]), _dot(q, exdst_ref[...])
        gcol = _dot(q, exgam_ref[...])
        xdt = xs_ref[...] * dtx
        ht = h_ref[...]
        y_off = _dot(cb, ht.astype(BF16)) * dinx
        hs_ref[0, 0] = ht
        for p in range(j_heads // 2):
            lanes = slice(p * CHUNK, (p + 1) * CHUNK)
            x2 = xdt[:, lanes]
            acc = y_off[:, lanes]
            for idx, j in enumerate((2 * p, 2 * p + 1)):
                decay = jnp.exp(jnp.where(mask, gcol[:, j * CHUNK:(j + 1) * CHUNK] - gamt_ref[0, pl.ds(j, 1), :], NEG))
                x_head = jnp.where(lo if idx == 0 else jnp.logical_not(lo), x2, 0.0).astype(BF16)
                acc = acc + _dot((cbt * decay).astype(BF16), x_head)
            y_ref[:, lanes] = acc
        h_ref[...] = ht * etx_ref[0, 0, 0:1, :] + _dot(bb, (xdt * dstx).astype(BF16), _TN)

    return pl.pallas_call(
        kern, name=name, grid=(N_GROUPS, nc),
        in_specs=[sp["xs"], sp["b"], sp["c"], sp["q"], sp["head_t"], sp["etot_x"], sp["ex"], sp["ex_gam"], sp["ex"], sp["ex"]],
        out_specs=[sp["xs"], sp["state"]],
        out_shape=[jax.ShapeDtypeStruct((s, d_ssm), F32), jax.ShapeDtypeStruct((N_GROUPS, nc, D_STATE, gw), F32)],
        scratch_shapes=[pltpu.VMEM((D_STATE, gw), F32)],
        compiler_params=_params(("parallel", "arbitrary")),
    )(xbc_c, xbc_c, xbc_c, hi["q"], hi["gamT"], hi["etot_x"], hi["ex_dt"], hi["ex_gam"], hi["ex_din"], hi["ex_dst"])


def _scan_bwd(name, xbc_c, dy, hs, hi, *, reverse, d_ssm):
    s = xbc_c.shape[0]
    nc = s // CHUNK
    j_heads = hi["gamT"].shape[1]
    gw = j_heads * HEAD_DIM
    sp = _scan_specs(not reverse, nc, j_heads, d_ssm)
    hp = lax.Precision.HIGHEST

    def kern(xs_ref, b_ref, c_ref, dy_ref, hs_ref, q_ref, gamt_ref, etr_ref, etx_ref, exdt_ref, exgam_ref, exdin_ref,
             exdst_ref, sel_ref, dxs_ref, db_ref, dc_ref, ddt_ref, da_ref, dh_ref, tmp_ref):
        @pl.when(pl.program_id(1) == 0)
        def _():
            dh_ref[...] = jnp.zeros_like(dh_ref)

        li, si, mask, mask_t, lo = _scan_masks(reverse)
        bb = b_ref[...].astype(BF16)
        cb = c_ref[...].astype(BF16)
        cbt = _dot(cb, bb, _NT)
        cbt_t = _dot(bb, cb, _NT)
        q = q_ref[0]
        dtx, dinx, dstx = _dot(q, exdt_ref[...]), _dot(q, exdin_ref[...]), _dot(q, exdst_ref[...])
        gcol = _dot(q, exgam_ref[...])
        x_all = xs_ref[...]
        dy_all = dy_ref[...]
        xdt = x_all * dtx
        xb = xdt.astype(BF16)
        ht = hs_ref[0, 0]
        hb = ht.astype(BF16)
        dht = dh_ref[...]
        dhb = dht.astype(BF16)
        y_off = _dot(cb, hb) * dinx
        dx_off = _dot(bb, dhb) * dstx
        dyd = (dy_all * dinx).astype(BF16)
        xd = (xdt * dstx).astype(BF16)
        dc_acc = _dot(dyd, hb, _NT)
        db_acc = _dot(xd, dhb, _NT)
        dh_ref[...] = dht * etx_ref[0, 0, 0:1, :] + _dot(cb, dyd, _TN)
        sel = sel_ref[...]
        q_cols = _dot((dy_all * y_off).astype(BF16), sel)
        c_cols = _dot((xdt * dx_off).astype(BF16), sel)
        dh_h = dht * ht
        dh_h_hi = dh_h.astype(BF16)
        through = _sum0(_dot(dh_h_hi, sel) + _dot((dh_h - dh_h_hi.astype(F32)).astype(BF16), sel))
        dcbt = jnp.zeros((CHUNK, CHUNK), F32)
        for p in range(j_heads // 2):
            lanes = slice(p * CHUNK, (p + 1) * CHUNK)
            x2b = xb[:, lanes]
            dy2 = dy_all[:, lanes]
            acc = dx_off[:, lanes]
            for idx, j in enumerate((2 * p, 2 * p + 1)):
                gc = gcol[:, j * CHUNK:(j + 1) * CHUNK]
                gr = gamt_ref[0, pl.ds(j, 1), :]
                decay = jnp.exp(jnp.where(mask, gc - gr, NEG))
                decay_t = jnp.exp(jnp.where(mask_t, gr - gc, NEG))
                dy_head = jnp.where(lo if idx == 0 else jnp.logical_not(lo), dy2, 0.0).astype(BF16)
                acc = acc + _dot((cbt_t * decay_t).astype(BF16), dy_head)
                dm = decay * _dot(dy_head, x2b, _NT)
                dcbt = dcbt + dm
                e = (cbt * dm).astype(BF16)
                lane_j = jnp.where(si == j, 1.0, 0.0).astype(BF16)
                q_cols = q_cols + _dot(e, lane_j) - _dot(e, lane_j, _TN)
            dxs_ref[:, lanes] = acc * dtx[:, lanes]
            tmp_ref[:, lanes] = acc * x_all[:, lanes]
        dcb = dcbt.astype(BF16)
        dc_ref[...] = dc_acc + _dot(dcb, bb)
        db_ref[...] = db_acc + _dot(dcb, cb, _TN)
        ddt_ref[0] = _dot(tmp_ref[...].astype(BF16), sel)
        incl = ((si <= li) if reverse else (si >= li)).astype(F32)
        excl = ((si > li) if reverse else (si < li)).astype(F32)
        da_ref[0] = (jnp.dot(incl, q_cols, preferred_element_type=F32, precision=hp)
                     + jnp.dot(excl, c_cols, preferred_element_type=F32, precision=hp)
                     + through * etr_ref[0, 0, 0:1, :])

    gn = N_GROUPS * D_STATE
    return pl.pallas_call(
        kern, name=name, grid=(N_GROUPS, nc),
        in_specs=[sp["xs"], sp["b"], sp["c"], sp["xs"], sp["state"], sp["q"], sp["head_t"], sp["etot_rows"], sp["etot_x"],
                  sp["ex"], sp["ex_gam"], sp["ex"], sp["ex"], sp["sel"]],
        out_specs=[sp["xs"], sp["grp"], sp["grp"], sp["head_cols"], sp["head_cols"]],
        out_shape=[jax.ShapeDtypeStruct((s, d_ssm), F32), jax.ShapeDtypeStruct((s, gn), F32),
                   jax.ShapeDtypeStruct((s, gn), F32), jax.ShapeDtypeStruct((N_GROUPS, s, LANES), F32),
                   jax.ShapeDtypeStruct((N_GROUPS, s, LANES), F32)],
        scratch_shapes=[pltpu.VMEM((D_STATE, gw), F32), pltpu.VMEM((CHUNK, gw), F32)],
        compiler_params=_params(("parallel", "arbitrary")),
    )(xbc_c, xbc_c, xbc_c, dy, hs, hi["q"], hi["gamT"], hi["etot_rows"], hi["etot_x"], hi["ex_dt"], hi["ex_gam"],
      hi["ex_din"], hi["ex_dst"], hi["sel_cols"])


def _dt_prepare(dt_raw, dt_bias, a_neg):
    h2 = dt_raw.shape[1]
    n_heads = h2 // 2

    def body(raw, bias, a_head):
        dt = _softplus(raw + bias)
        a = dt * a_head
        li = lax.broadcasted_iota(jnp.int32, (CHUNK, CHUNK), 0)
        si = lax.broadcasted_iota(jnp.int32, (CHUNK, CHUNK), 1)
        tri = (li >= si).astype(F32)
        cs = jnp.dot(tri, a, preferred_element_type=F32, precision=lax.Precision.HIGHEST)
        tot = _sum0(a)
        fwd = lax.broadcasted_iota(jnp.int32, (CHUNK, h2), 1) < n_heads
        gam = jnp.where(fwd, cs, a - cs)
        din = jnp.where(fwd, jnp.exp(cs), jnp.exp(tot + gam))
        dst = jnp.where(fwd, jnp.exp(tot - cs), jnp.exp(cs - a))
        etot = jnp.broadcast_to(jnp.exp(tot), (CHUNK, h2))
        return dt, gam, din, dst, etot

    return _rowwise("dt_prepare", body, [dt_raw], [dt_bias, a_neg], [(h2, F32)] * 5, tile=CHUNK)


def _by_group(arr, n_heads, direction):
    s = arr.shape[0]
    part = arr[:, direction * n_heads:(direction + 1) * n_heads]
    return part.reshape(s, N_GROUPS, n_heads // N_GROUPS).transpose(1, 0, 2)


def _from_group_cols(arr, j_heads):
    g, s, _ = arr.shape
    return arr[:, :, :j_heads].transpose(1, 0, 2).reshape(s, g * j_heads)


def _dt_backward(da, ddt, dt, dt_raw, dt_bias, a_neg):
    h2 = da.shape[1]

    def body(dav, ddtv, dtv, raw, bias, a_head):
        draw = (ddtv + dav * a_head) * _sigmoid(raw + bias)
        return draw, _sum0(draw), _sum0(dav * dtv) * a_head
    return _rowwise("dt_backward", body, [da, ddt, dt, dt_raw], [dt_bias, a_neg], [(h2, BF16)], [h2, h2])


def _rms(x):
    r = lax.rsqrt(_mean1(x * x) + EPS)
    return x * r, r


def _rms_bwd(dy, y, r):
    return r * (dy - y * _mean1(dy * y))


def _norm_mod_fwd(name, x, g, sc, sh):
    def body(xv, gv, scv, shv):
        y, _ = _rms(xv)
        return ((y * gv) * (1.0 + scv) + shv,)
    return _rowwise(name, body, [x], [g, sc, sh], [(x.shape[1], BF16)])[0]


def _norm_mod_bwd(name, x, dh, dpass, g, sc):
    d = x.shape[1]

    def body(xv, dhv, dpv, gv, scv):
        y, r = _rms(xv)
        dn = dhv * (1.0 + scv)
        dx = _rms_bwd(dn * gv, y, r) + dpv
        return dx, _sum0(dn * y), _sum0(dhv * (y * gv)), _sum0(dhv)
    return _rowwise(name, body, [x, dh, dpass], [g, sc], [(d, F32)], [d, d, d])


def _gated_residual_fwd(name, x, m, gate, gp):
    def body(xv, mv, gatev, gpv):
        y, _ = _rms(mv)
        return (xv + gatev * (y * gpv),)
    return _rowwise(name, body, [x, m], [gate, gp], [(x.shape[1], F32)])[0]


def _gated_residual_bwd(name, m, dx1, gate, gp):
    d = m.shape[1]

    def body(mv, dv, gatev, gpv):
        y, r = _rms(mv)
        dn = dv * gatev
        return _rms_bwd(dn * gpv, y, r), _sum0(dv * (y * gpv)), _sum0(dn * y)
    return _rowwise(name, body, [m, dx1], [gate, gp], [(d, BF16)], [d, d])


def _final_residual_loss(x1, f, tgt, gate, gp):
    d = x1.shape[1]

    def body(xv, fv, tv, gatev, gpv):
        y, r = _rms(fv)
        n = y * gpv
        err = xv + gatev * n - tv
        dx2 = err * (1.0 / d)
        dn = dx2 * gatev
        sq = jnp.sum(_sum0(err * err), axis=1, keepdims=True)
        return dx2, _rms_bwd(dn * gpv, y, r), jnp.broadcast_to(sq, (1, LANES)), _sum0(dx2 * n), _sum0(dn * y)
    return _rowwise("final_residual_loss", body, [x1, f, tgt], [gate, gp], [(d, F32), (d, BF16)], [LANES, d, d])


def _swiglu_fwd(gu):
    f = gu.shape[1] // 2

    def body(v):
        return (_silu(v[:, :f]) * v[:, f:],)
    return _rowwise("swiglu_fwd", body, [gu], [], [(f, BF16)])[0]


def _swiglu_bwd(gu, dact):
    f = gu.shape[1] // 2

    def body(v, dv):
        gt, up = v[:, :f], v[:, f:]
        return (jnp.concatenate([dv * up * _dsilu(gt), dv * _silu(gt)], axis=1),)
    return _rowwise("swiglu_bwd", body, [gu, dact], [], [(2 * f, BF16)])[0]


def _glu_fwd(glu_in, b_glu):
    c = glu_in.shape[1] // 2

    def body(v, bv):
        t = v + bv
        return (t[:, :c] * _sigmoid(t[:, c:]),)
    return _rowwise("glu_fwd", body, [glu_in], [b_glu], [(c, F32)])[0]


def _glu_bwd(glu_in, b_glu, du0):
    c = glu_in.shape[1] // 2

    def body(v, dv, bv):
        t = v + bv
        a, s = t[:, :c], _sigmoid(t[:, c:])
        dg = jnp.concatenate([dv * s, dv * a * s * (1.0 - s)], axis=1)
        return dg, _sum0(dg)
    return _rowwise("glu_bwd", body, [glu_in, du0], [b_glu], [(2 * c, BF16)], [2 * c])


def _ln_parts(u1):
    xc = u1 - _mean1(u1)
    r = lax.rsqrt(_mean1(xc * xc) + EPS)
    return xc * r, r


def _ln_silu_fwd(u1, ln_g, ln_b):
    def body(v, gv, bv):
        yh, _ = _ln_parts(v)
        return (_silu(yh * gv + bv),)
    return _rowwise("ln_silu_fwd", body, [u1], [ln_g, ln_b], [(u1.shape[1], BF16)])[0]


def _ln_silu_bwd(u1, du, ln_g, ln_b):
    d = u1.shape[1]

    def body(v, dv, gv, bv):
        yh, r = _ln_parts(v)
        dl = dv * _dsilu(yh * gv + bv)
        dyh = dl * gv
        du1 = r * (dyh - _mean1(dyh) - yh * _mean1(dyh * yh))
        return du1, _sum0(dl * yh), _sum0(dl)
    return _rowwise("ln_silu_bwd", body, [u1, du], [ln_g, ln_b], [(d, F32)], [d, d])


def _gate_merge_fwd(y_a, y_b, gl, b_gate):
    d = y_a.shape[1]

    def body(ya, yb, glv, bv):
        s = _sigmoid(glv + bv)
        return (s[:, :d] * ya + s[:, d:] * yb,)
    return _rowwise("gate_merge_fwd", body, [y_a, y_b, gl], [b_gate], [(d, BF16)])[0]


def _gate_merge_bwd(dmixin, y_a, y_b, gl, b_gate):
    d = y_a.shape[1]

    def body(dv, ya, yb, glv, bv):
        s = _sigmoid(glv + bv)
        sa, sb = s[:, :d], s[:, d:]
        dya, dyb = dv * sa, dv * sb
        dgl = jnp.concatenate([dv * ya * sa * (1.0 - sa), dv * yb * sb * (1.0 - sb)], axis=1)
        return dya, dyb, dgl, _sum0(dgl), _sum0(dyb)
    return _rowwise("gate_merge_bwd", body, [dmixin, y_a, y_b, gl], [b_gate],
                    [(d, BF16), (d, BF16), (2 * d, BF16)], [2 * d, d])


def _group_slices(d_ssm):
    gw = d_ssm // N_GROUPS
    return [slice(g * gw, (g + 1) * gw) for g in range(N_GROUPS)]


def _gated_norm_fwd(y_f, y_b, xbc_c, z, d_skip_x, g_ssm):
    d_ssm = y_f.shape[1]

    def body(yf, yb, xs, zv, dsk, gv):
        y = yf + yb + dsk * xs
        v = y * _silu(zv)
        outs = []
        for sl in _group_slices(d_ssm):
            w, _ = _rms(v[:, sl])
            outs.append(w)
        return y, jnp.concatenate(outs, axis=1) * gv
    return _rowwise("gated_norm_fwd", body, [y_f, y_b, (xbc_c, d_ssm, 0), z], [d_skip_x, g_ssm],
                    [(d_ssm, F32), (d_ssm, BF16)])


def _gated_norm_bwd(y, z, dyn, xbc_c, d_skip_x, g_ssm):
    d_ssm = y.shape[1]

    def body(yv, zv, dv, xs, dsk, gv):
        sz = _silu(zv)
        v = yv * sz
        dw = dv * gv
        dvs, ws = [], []
        for sl in _group_slices(d_ssm):
            w, r = _rms(v[:, sl])
            ws.append(w)
            dvs.append(_rms_bwd(dw[:, sl], w, r))
        dvv = jnp.concatenate(dvs, axis=1)
        dy = dvv * sz
        return dy, dvv * yv * _dsilu(zv), _sum0(dv * jnp.concatenate(ws, axis=1)), _sum0(dy * xs)
    return _rowwise("gated_norm_bwd", body, [y, z, dyn, (xbc_c, d_ssm, 0)], [d_skip_x, g_ssm],
                    [(d_ssm, F32), (d_ssm, BF16)], [d_ssm, d_ssm])


def _ssd_grad_merge(dxs_f, dxs_b, dy, db_f, db_b, dc_f, dc_b, d_skip_x):
    d_ssm = dy.shape[1]
    width = d_ssm + 2 * N_GROUPS * D_STATE

    def body(xf, xb, dv, bf, bb, cf, cbv, dsk):
        return (jnp.concatenate([xf + xb + dsk * dv, bf + bb, cf + cbv], axis=1),)
    return _rowwise("ssd_grad_merge", body, [dxs_f, dxs_b, dy, db_f, db_b, dc_f, dc_b], [d_skip_x], [(width, F32)])[0]


def _adamw(name, w, g, m, v):
    c = w.shape[1]
    c1 = 1.0 - ADAM_B1 ** ADAM_STEP
    c2 = 1.0 - ADAM_B2 ** ADAM_STEP

    def body(wv, gv, mv, vv):
        mn = ADAM_B1 * mv + (1.0 - ADAM_B1) * gv
        vn = ADAM_B2 * vv + (1.0 - ADAM_B2) * (gv * gv)
        delta = -ADAM_LR * ((mn / c1) / (jnp.sqrt(vn / c2) + ADAM_EPS) + ADAM_WD * wv)
        return delta, mn, vn
    return _rowwise(name, body, [w, g, m, v], [], [(c, F32)] * 3)


def _local_step(x, tgt, mod, wts, sm):
    s, d = x.shape
    d_ssm = 2 * d
    n_heads = d_ssm // HEAD_DIM
    d_xbc = d_ssm + 2 * N_GROUPS * D_STATE
    sec = [0, d_ssm, d_ssm + d_xbc, d_ssm + d_xbc + 2 * n_heads, d_ssm + d_xbc + 2 * n_heads + 2 * d]
    sec.append(sec[-1] + 2 * d)
    sh1, sc1, g1, sh2, sc2, g2 = [mod[:, i * d:(i + 1) * d] for i in range(N_MOD)]
    win_t = wts["w_in_t"]
    win_secs = [win_t[sec[i]:sec[i + 1]] for i in range(5)]

    h1 = _norm_mod_fwd("pre_mix_norm", x, sm["g_pre_mix"], sc1, sh1)
    z, xbc, dt_raw, glu_in, gate_l = [
        _matmul(f"proj_{nm}", h1, w, tb=True) for nm, w in zip(("z", "xbc", "dt", "glu", "gate"), win_secs)]
    xbc_c = _dwconv_fwd("ssm_conv_fwd", xbc, sm["w_conv_ssm"], sm["b_conv_ssm"], silu=True)
    dt, gam, din, dst, etot = _dt_prepare(dt_raw, sm["dt_bias"], sm["a_neg"])
    per_head = {"dt": dt, "gam": gam, "din": din, "dst": dst, "etot": etot}
    heads = [_head_inputs(per_head, n_heads, direction) for direction in (0, 1)]
    y_f, hs_f = _scan_fwd("ssd_fwd_f", xbc_c, heads[0], reverse=False, d_ssm=d_ssm)
    y_r, hs_r = _scan_fwd("ssd_fwd_r", xbc_c, heads[1], reverse=True, d_ssm=d_ssm)
    y_ssd, yn = _gated_norm_fwd(y_f, y_r, xbc_c, z, sm["d_skip_x"], sm["g_ssm_norm"])
    y_a = _matmul("ssm_out", yn, wts["w_ssm_out"])
    u0 = _glu_fwd(glu_in, sm["b_glu"])
    u1 = _dwconv_fwd("dw_conv_fwd", u0, sm["w_dw"], sm["b_dw"], silu=False)
    u = _ln_silu_fwd(u1, sm["ln_g"], sm["ln_b"])
    y_b = _matmul("conv_out", u, wts["w_conv_out"], bias=sm["b_conv_out"])
    mixin = _gate_merge_fwd(y_a, y_b, gate_l, sm["b_gate"])
    mix = _matmul("mix_out", mixin, wts["w_mix_out"])
    x1 = _gated_residual_fwd("post_mix_residual", x, mix, g1, sm["g_post_mix"])
    h2 = _norm_mod_fwd("pre_ffn_norm", x1, sm["g_pre_ffn"], sc2, sh2)
    gu = _matmul("ffn_gate_up", h2, wts["w_gate_up"], b_blocks=N_CHIPS)
    act = _swiglu_fwd(gu)
    f = _matmul("ffn_down", act, wts["w_down"])

    dx2, df, sq, d_g2, d_gpf = _final_residual_loss(x1, f, tgt, g2, sm["g_post_ffn"])
    dact = _matmul("d_act", df, wts["w_down"], tb=True)
    g_w_down = _matmul("g_w_down", act, df, ta=True, out_dtype=BF16)
    dgu = _swiglu_bwd(gu, dact)
    dh2 = _matmul("d_h2", dgu, wts["w_gate_up"], tb=True, b_blocks=N_CHIPS)
    g_w_gate_up = _matmul("g_w_gate_up", h2, dgu, ta=True, out_dtype=BF16, out_blocks=N_CHIPS)
    dx1, d_gpre_ffn, d_sc2, d_sh2 = _norm_mod_bwd("pre_ffn_norm_bwd", x1, dh2, dx2, sm["g_pre_ffn"], sc2)
    dmix, d_g1, d_gpm = _gated_residual_bwd("post_mix_residual_bwd", mix, dx1, g1, sm["g_post_mix"])
    dmixin = _matmul("d_mixin", dmix, wts["w_mix_out"], tb=True)
    g_w_mix = _matmul("g_w_mix_out", mixin, dmix, ta=True, out_dtype=BF16)
    dy_a, dy_b, dgate_l, d_bgate, d_bco = _gate_merge_bwd(dmixin, y_a, y_b, gate_l, sm["b_gate"])
    du = _matmul("d_u", dy_b, wts["w_conv_out"], tb=True)
    g_w_co = _matmul("g_w_conv_out", u, dy_b, ta=True, out_dtype=BF16)
    du1, d_lng, d_lnb = _ln_silu_bwd(u1, du, sm["ln_g"], sm["ln_b"])
    du0, d_wdw, d_bdw = _dwconv_bwd("dw_conv_bwd", u0, sm["w_dw"], sm["b_dw"], du1, silu=False)
    dglu, d_bglu = _glu_bwd(glu_in, sm["b_glu"], du0)
    dyn = _matmul("d_yn", dy_a, wts["w_ssm_out"], tb=True)
    g_w_ssm = _matmul("g_w_ssm_out", yn, dy_a, ta=True, out_dtype=BF16)
    dy_ssd, dz, d_gssm, d_dskip_x = _gated_norm_bwd(y_ssd, z, dyn, xbc_c, sm["d_skip_x"], sm["g_ssm_norm"])
    dxs_f, db_f, dc_f, ddt_f, da_f = _scan_bwd("ssd_bwd_f", xbc_c, dy_ssd, hs_f, heads[0], reverse=False, d_ssm=d_ssm)
    dxs_r, db_r, dc_r, ddt_r, da_r = _scan_bwd("ssd_bwd_r", xbc_c, dy_ssd, hs_r, heads[1], reverse=True, d_ssm=d_ssm)
    j_heads = n_heads // N_GROUPS
    da = jnp.concatenate([_from_group_cols(da_f, j_heads), _from_group_cols(da_r, j_heads)], axis=1)
    ddt = jnp.concatenate([_from_group_cols(ddt_f, j_heads), _from_group_cols(ddt_r, j_heads)], axis=1)
    ddt_raw, d_dtbias, d_alog = _dt_backward(da, ddt, dt, dt_raw, sm["dt_bias"], sm["a_neg"])
    dxbc_c = _ssd_grad_merge(dxs_f, dxs_r, dy_ssd, db_f, db_r, dc_f, dc_r, sm["d_skip_x"])
    dxbc, d_wconv, d_bconv = _dwconv_bwd("ssm_conv_bwd", xbc, sm["w_conv_ssm"], sm["b_conv_ssm"], dxbc_c,
                                         silu=True, dx_dtype=BF16)
    dsecs = [dz, dxbc, ddt_raw, dglu, dgate_l]
    dh1 = None
    g_win = []
    for nm, dsec, w in zip(("z", "xbc", "dt", "glu", "gate"), dsecs, win_secs):
        dh1 = _matmul(f"d_h1_{nm}", dsec, w, add=dh1)
        g_win.append(_matmul(f"g_w_in_{nm}", dsec, h1, ta=True, out_dtype=BF16))
    grad_x, d_gpre_mix, d_sc1, d_sh1 = _norm_mod_bwd("pre_mix_norm_bwd", x, dh1, dx1, sm["g_pre_mix"], sc1)

    dmod = jnp.concatenate([d_sh1, d_sc1, d_g1, d_sh2, d_sc2, d_g2], axis=1)
    big = {"w_in_t": jnp.concatenate(g_win, axis=0), "w_ssm_out": g_w_ssm, "w_conv_out": g_w_co,
           "w_mix_out": g_w_mix, "w_gate_up": g_w_gate_up, "w_down": g_w_down}
    small = {"g_pre_mix": d_gpre_mix, "g_post_mix": d_gpm, "w_conv_ssm": d_wconv, "b_conv_ssm": d_bconv,
             "dt_bias": d_dtbias, "a_log": d_alog, "d_skip_x": d_dskip_x, "g_ssm_norm": d_gssm, "b_glu": d_bglu,
             "w_dw": d_wdw, "b_dw": d_bdw, "ln_g": d_lng, "ln_b": d_lnb, "b_conv_out": d_bco, "b_gate": d_bgate,
             "g_pre_ffn": d_gpre_ffn, "g_post_ffn": d_gpf}
    return sq, grad_x, big, small, dmod


ANY = pl.BlockSpec(memory_space=pl.ANY)
WHOLE_VMEM = pl.BlockSpec(memory_space=pltpu.VMEM)


def _mesh_place():
    x, y, c = lax.axis_index("x"), lax.axis_index("y"), lax.axis_index("c")
    other_chips = [(1 - x, y), (x, 1 - y), (1 - x, 1 - y)]
    return x, y, c, other_chips


def _remote(src, dst, send_sems, recv_sems, k, device):
    return pltpu.make_async_remote_copy(src_ref=src, dst_ref=dst, send_sem=send_sems.at[k], recv_sem=recv_sems.at[k],
                                        device_id=device, device_id_type=MESH)


def _gather_devices(name, block):
    m_per, n = block.shape

    def body(x_ref, out_ref, send_sems, recv_sems, local_sem):
        x, y, c, chips = _mesh_place()
        me, sibling = (x, y, c), (x, y, 1 - c)

        def rows(px, py, pc):
            return out_ref.at[pl.ds((4 * px + 2 * py + pc) * m_per, m_per), :]

        def copy(k, blk, to, src=None):
            return _remote(rows(*blk) if src is None else src, rows(*blk), send_sems, recv_sems, k, to)

        mine = pltpu.make_async_copy(x_ref, rows(*me), local_sem)
        mine.start()
        first = [copy(0, me, sibling, src=x_ref)]
        first += [copy(1 + j, me, (*chip, c), src=x_ref) for j, chip in enumerate(chips)]
        for cp in first:
            cp.start()
        passed = [copy(4 + j, (*chip, c), sibling) for j, chip in enumerate(chips)]
        for j, chip in enumerate(chips):
            copy(1 + j, (*chip, c), me).wait_recv()
            passed[j].start()
        copy(0, sibling, me).wait_recv()
        for j, chip in enumerate(chips):
            copy(4 + j, (*chip, 1 - c), me).wait_recv()
        for cp in first + passed:
            cp.wait_send()
        mine.wait()

    return pl.pallas_call(
        body, name=name, out_shape=jax.ShapeDtypeStruct((N_DEV * m_per, n), block.dtype),
        in_specs=[WHOLE_VMEM], out_specs=WHOLE_VMEM,
        scratch_shapes=[pltpu.SemaphoreType.DMA((7,)), pltpu.SemaphoreType.DMA((7,)), pltpu.SemaphoreType.DMA],
        compiler_params=pltpu.CompilerParams(vmem_limit_bytes=VMEM_LIMIT),
    )(block)


def _gather_chips(name, shards):
    n = len(shards)

    def body(*refs):
        ins, outs = refs[:n], refs[n:2 * n]
        send_sems, recv_sems = refs[2 * n:]
        x, y, c, chips = _mesh_place()
        me = 2 * x + y
        sibling = (x, y, 1 - c)

        def half(i, h):
            hr = ins[i].shape[0] // 2
            return pl.ds(h * hr, hr)

        sends = []
        for i in range(n):
            for j, (cx, cy) in enumerate(chips):
                sends.append(_remote(ins[i].at[half(i, c)], outs[i].at[me, half(i, c)], send_sems, recv_sems,
                                     6 * i + j, (cx, cy, c)))
                sends[-1].start()
        for i in range(n):
            sends.append(_remote(ins[i], outs[i].at[me], send_sems, recv_sems, 6 * n + i, sibling))
            sends[-1].start()
        for i in range(n):
            for j, (cx, cy) in enumerate(chips):
                blk = outs[i].at[2 * cx + cy, half(i, c)]
                _remote(blk, blk, send_sems, recv_sems, 6 * i + j, (cx, cy, c)).wait_recv()
                sends.append(_remote(blk, blk, send_sems, recv_sems, 6 * i + 3 + j, sibling))
                sends[-1].start()
        for i in range(n):
            for j, (cx, cy) in enumerate(chips):
                blk = outs[i].at[2 * cx + cy, half(i, 1 - c)]
                _remote(blk, blk, send_sems, recv_sems, 6 * i + 3 + j, sibling).wait_recv()
        for i in range(n):
            _remote(ins[i], outs[i].at[me], send_sems, recv_sems, 6 * n + i, sibling).wait_recv()
        for cp in sends:
            cp.wait_send()

    return pl.pallas_call(
        body, name=name, out_shape=[jax.ShapeDtypeStruct((N_CHIPS,) + s.shape, s.dtype) for s in shards],
        in_specs=[ANY] * n, out_specs=[ANY] * n,
        scratch_shapes=[pltpu.SemaphoreType.DMA((7 * n,)), pltpu.SemaphoreType.DMA((7 * n,))],
    )(*shards)


def _send_sibling_halves(name, grads):
    n = len(grads)

    def body(*refs):
        ins, outs = refs[:n], refs[n:2 * n]
        send_sems, recv_sems = refs[2 * n:]
        x, y, c, _ = _mesh_place()
        sibling = (x, y, 1 - c)
        copies = []
        for i in range(n):
            for j in range(N_CHIPS):
                copies.append(_remote(ins[i].at[j, 1 - c], outs[i].at[j], send_sems, recv_sems, N_CHIPS * i + j, sibling))
                copies[-1].start()
        for cp in copies:
            cp.wait_recv()
        for cp in copies:
            cp.wait_send()

    return pl.pallas_call(
        body, name=name,
        out_shape=[jax.ShapeDtypeStruct((g.shape[0],) + g.shape[2:], g.dtype) for g in grads],
        in_specs=[ANY] * n, out_specs=[ANY] * n,
        scratch_shapes=[pltpu.SemaphoreType.DMA((N_CHIPS * n,)), pltpu.SemaphoreType.DMA((N_CHIPS * n,))],
    )(*grads)


def _send_chip_shards(name, sums):
    n = len(sums)

    def body(*refs):
        ins, outs = refs[:n], refs[n:2 * n]
        send_sems, recv_sems = refs[2 * n:]
        x, y, c, chips = _mesh_place()
        copies = []
        for i in range(n):
            for j, (cx, cy) in enumerate(chips):
                copies.append(_remote(ins[i].at[2 * cx + cy], outs[i].at[j], send_sems, recv_sems, 3 * i + j, (cx, cy, c)))
                copies[-1].start()
        for cp in copies:
            cp.wait_recv()
        for cp in copies:
            cp.wait_send()

    return pl.pallas_call(
        body, name=name,
        out_shape=[jax.ShapeDtypeStruct((3,) + g.shape[1:], g.dtype) for g in sums],
        in_specs=[ANY] * n, out_specs=[ANY] * n,
        scratch_shapes=[pltpu.SemaphoreType.DMA((3 * n,)), pltpu.SemaphoreType.DMA((3 * n,))],
    )(*sums)


def _exchange_halves(name, shards):
    n = len(shards)

    def body(*refs):
        outs = refs[n:2 * n]
        send_sems, recv_sems = refs[2 * n:]
        x, y, c, _ = _mesh_place()
        sibling = (x, y, 1 - c)
        remote = [_remote(outs[i].at[c], outs[i].at[c], send_sems, recv_sems, i, sibling) for i in range(n)]
        for cp in remote:
            cp.start()
        for i in range(n):
            _remote(outs[i].at[1 - c], outs[i].at[1 - c], send_sems, recv_sems, i, sibling).wait_recv()
        for cp in remote:
            cp.wait_send()

    return pl.pallas_call(
        body, name=name,
        out_shape=[jax.ShapeDtypeStruct(h.shape, h.dtype) for h in shards],
        in_specs=[ANY] * n, out_specs=[ANY] * n, input_output_aliases={i: i for i in range(n)},
        scratch_shapes=[pltpu.SemaphoreType.DMA((n,)), pltpu.SemaphoreType.DMA((n,))],
    )(*shards)


def _divisor_tile(rows, row_bytes, quantum=16):
    best = rows
    for t in range(quantum, rows + 1, quantum):
        if rows % t == 0 and 2 * t * row_bytes <= ROW_TILE_BUDGET:
            best = t
    return best


def _add_sibling(name, g4, t1, core):
    nb, _, hr, cols = g4.shape
    t = _divisor_tile(hr, cols * 6)

    def kern(core_ref, g_ref, t_ref, o_ref):
        o_ref[0] = (g_ref[0, 0].astype(F32) + t_ref[0].astype(F32)).astype(o_ref.dtype)

    return pl.pallas_call(
        kern, name=name,
        grid_spec=pltpu.PrefetchScalarGridSpec(
            num_scalar_prefetch=1, grid=(nb, hr // t),
            in_specs=[pl.BlockSpec((1, 1, t, cols), lambda j, i, core_ref: (j, core_ref[0], i, 0)),
                      pl.BlockSpec((1, t, cols), lambda j, i, core_ref: (j, i, 0))],
            out_specs=pl.BlockSpec((1, t, cols), lambda j, i, core_ref: (j, i, 0))),
        out_shape=jax.ShapeDtypeStruct((nb, hr, cols), g4.dtype),
        compiler_params=_params(("parallel", "parallel")),
    )(core, g4, t1)


def _add_chips(name, s1, t3, place):
    _, hr, cols = s1.shape
    t = _divisor_tile(hr, cols * 12)

    def kern(place_ref, s_ref, t_ref, o_ref):
        acc = s_ref[0].astype(F32)
        for j in range(3):
            acc = acc + t_ref[j].astype(F32)
        o_ref[0] = acc

    return pl.pallas_call(
        kern, name=name,
        grid_spec=pltpu.PrefetchScalarGridSpec(
            num_scalar_prefetch=1, grid=(hr // t,),
            in_specs=[pl.BlockSpec((1, t, cols), lambda i, place_ref: (place_ref[0], i, 0)),
                      pl.BlockSpec((3, t, cols), lambda i, place_ref: (0, i, 0))],
            out_specs=pl.BlockSpec((1, t, cols), lambda i, place_ref: (place_ref[1], i, 0))),
        out_shape=jax.ShapeDtypeStruct((2, hr, cols), F32),
        compiler_params=_params(("parallel",)),
    )(place, s1, t3)


def _reduce_scatter(grads, core, chip):
    g4 = [g.reshape(N_CHIPS, 2, g.shape[1] // 2, g.shape[2]) for g in grads]
    t1 = _send_sibling_halves("grads_to_sibling", g4)
    s1 = [_add_sibling(f"chip_sum_{i}", g, t, core) for i, (g, t) in enumerate(zip(g4, t1))]
    t3 = _send_chip_shards("grads_to_chips", s1)
    place = jnp.concatenate([chip, core])
    halves = [_add_chips(f"shard_sum_{i}", s, t, place) for i, (s, t) in enumerate(zip(s1, t3))]
    full = _exchange_halves("grad_halves_to_sibling", halves)
    return [f.reshape(f.shape[1] * 2, f.shape[2]) for f in full]


def _pack_rows(size, width):
    return -(-size // (8 * width)) * 8


def _pack(arrays, width):
    parts = []
    for a in arrays:
        flat = a.reshape(-1).astype(F32)
        rows = _pack_rows(flat.shape[0], width)
        parts.append(jnp.pad(flat, (0, rows * width - flat.shape[0])).reshape(rows, width))
    return jnp.concatenate(parts, axis=0)


def _unpack(block, shapes, width):
    out, r = [], 0
    for shp in shapes:
        size = 1
        for s_ in shp:
            size *= s_
        rows = _pack_rows(size, width)
        out.append(block[r:r + rows].reshape(-1)[:size].reshape(shp))
        r += rows
    return out


SMALL_PARAMS = ("b_ada", "g_pre_mix", "g_post_mix", "b_conv_ssm", "dt_bias_fwd", "dt_bias_bwd", "a_log_fwd", "a_log_bwd",
                "d_skip", "g_ssm_norm", "b_glu", "b_dw", "ln_g", "ln_b", "b_conv_out", "b_gate", "g_pre_ffn", "g_post_ffn")
SHARDED_SMALL = ("w_conv_ssm", "w_dw")
MATRICES = ("w_in", "w_ssm_out", "w_conv_out", "w_mix_out", "w_gate_up", "w_down")
ALL_PARAMS = ("w_ada", "b_ada", "g_pre_mix", "g_post_mix", "w_in", "w_conv_ssm", "b_conv_ssm", "dt_bias_fwd", "dt_bias_bwd",
              "a_log_fwd", "a_log_bwd", "d_skip", "g_ssm_norm", "w_ssm_out", "b_glu", "w_dw", "b_dw", "ln_g", "ln_b",
              "w_conv_out", "b_conv_out", "b_gate", "w_mix_out", "g_pre_ffn", "g_post_ffn", "w_gate_up", "w_down")
COND_ROWS = 48
COND_CONV_ROW = 8
COND_DW_ROW = 16
MOD_ROWS = 16


def kernel(x, c, w_ada, b_ada, g_pre_mix, g_post_mix, w_in, w_conv_ssm, b_conv_ssm, dt_bias_fwd, dt_bias_bwd, a_log_fwd, a_log_bwd, d_skip, g_ssm_norm, w_ssm_out, b_glu, w_dw, b_dw, ln_g, ln_b, w_conv_out, b_conv_out, b_gate, w_mix_out, g_pre_ffn, g_post_ffn, w_gate_up, w_down, loss_target, m_w_ada, m_b_ada, m_g_pre_mix, m_g_post_mix, m_w_in, m_w_conv_ssm, m_b_conv_ssm, m_dt_bias_fwd, m_dt_bias_bwd, m_a_log_fwd, m_a_log_bwd, m_d_skip, m_g_ssm_norm, m_w_ssm_out, m_b_glu, m_w_dw, m_b_dw, m_ln_g, m_ln_b, m_w_conv_out, m_b_conv_out, m_b_gate, m_w_mix_out, m_g_pre_ffn, m_g_post_ffn, m_w_gate_up, m_w_down, v_w_ada, v_b_ada, v_g_pre_mix, v_g_post_mix, v_w_in, v_w_conv_ssm, v_b_conv_ssm, v_dt_bias_fwd, v_dt_bias_bwd, v_a_log_fwd, v_a_log_bwd, v_d_skip, v_g_ssm_norm, v_w_ssm_out, v_b_glu, v_w_dw, v_b_dw, v_ln_g, v_ln_b, v_w_conv_out, v_b_conv_out, v_b_gate, v_w_mix_out, v_g_pre_ffn, v_g_post_ffn, v_w_gate_up, v_w_down):
    given = dict(locals())
    wgt = {n: given[n][0] for n in ALL_PARAMS}
    mom = {n: given["m_" + n][0] for n in ALL_PARAMS}
    var = {n: given["v_" + n][0] for n in ALL_PARAMS}
    xs, tgt = x[0], loss_target[0]
    s, d = xs.shape
    d_ssm = 2 * d
    n_heads = d_ssm // HEAD_DIM
    d_xbc = d_ssm + 2 * N_GROUPS * D_STATE
    xi, yi, ci = lax.axis_index("x"), lax.axis_index("y"), lax.axis_index("c")
    chip = 2 * xi + yi
    dev = 2 * chip + ci
    core_arr = jnp.reshape(ci, (1,)).astype(jnp.int32)
    chip_arr = jnp.reshape(chip, (1,)).astype(jnp.int32)
    k_conv, k_dw = wgt["w_conv_ssm"].shape[0], wgt["w_dw"].shape[0]
    xbc_shard, dw_shard = d_xbc // N_CHIPS, d // N_CHIPS

    width1 = max(d, xbc_shard)
    blk = jnp.zeros((COND_ROWS, width1), F32)
    blk = blk.at[0, :d].set(c[0])
    blk = blk.at[COND_CONV_ROW:COND_CONV_ROW + k_conv, :xbc_shard].set(wgt["w_conv_ssm"])
    blk = blk.at[COND_DW_ROW:COND_DW_ROW + k_dw, :dw_shard].set(wgt["w_dw"])
    g1 = _gather_devices("gather_cond", blk).reshape(N_DEV, COND_ROWS, width1)
    c_all = g1[:, 0, :d]
    w_conv_full = jnp.concatenate([g1[2 * k, COND_CONV_ROW:COND_CONV_ROW + k_conv, :xbc_shard] for k in range(N_CHIPS)], axis=1)
    w_dw_full = jnp.concatenate([g1[2 * k, COND_DW_ROW:COND_DW_ROW + k_dw, :dw_shard] for k in range(N_CHIPS)], axis=1)
    c_act = jnp.pad(c_all * _sigmoid(c_all), ((0, MOD_ROWS - N_DEV), (0, 0)))

    mod_part = _matmul("ada_mod", c_act, wgt["w_ada"])
    g2 = _gather_devices("gather_mod", mod_part).reshape(N_DEV, MOD_ROWS, mod_part.shape[1])
    mod_all = jnp.concatenate([g2[2 * k, :N_DEV] for k in range(N_CHIPS)], axis=1) + wgt["b_ada"][None]
    mod = lax.dynamic_slice_in_dim(mod_all, dev, 1, axis=0)

    shards = [wgt["w_in"].T.astype(BF16)] + [wgt[n].astype(BF16) for n in MATRICES[1:]]
    gathered = _gather_chips("gather_weights", shards)
    wts = {"w_in_t": gathered[0].reshape(-1, d), "w_ssm_out": gathered[1].reshape(-1, d),
           "w_conv_out": gathered[2].reshape(-1, d), "w_mix_out": gathered[3].reshape(-1, d),
           "w_gate_up": gathered[4], "w_down": gathered[5].reshape(-1, d)}
    row = lambda v: v.reshape(1, -1)
    sm = {"g_pre_mix": row(wgt["g_pre_mix"]), "g_post_mix": row(wgt["g_post_mix"]), "w_conv_ssm": w_conv_full,
          "b_conv_ssm": row(wgt["b_conv_ssm"]),
          "dt_bias": row(jnp.concatenate([wgt["dt_bias_fwd"], wgt["dt_bias_bwd"]])),
          "a_neg": row(-jnp.exp(jnp.concatenate([wgt["a_log_fwd"], wgt["a_log_bwd"]]))),
          "d_skip_x": row(jnp.repeat(wgt["d_skip"], HEAD_DIM)), "g_ssm_norm": row(wgt["g_ssm_norm"]),
          "b_glu": row(wgt["b_glu"]), "w_dw": w_dw_full, "b_dw": row(wgt["b_dw"]), "ln_g": row(wgt["ln_g"]),
          "ln_b": row(wgt["ln_b"]), "b_conv_out": row(wgt["b_conv_out"]), "b_gate": row(wgt["b_gate"]),
          "g_pre_ffn": row(wgt["g_pre_ffn"]), "g_post_ffn": row(wgt["g_post_ffn"])}

    sq, grad_x, big, small, dmod = _local_step(xs, tgt, mod, wts, sm)
    loss = lax.psum((0.5 / d) * sq[0, 0], ("x", "y", "c"))

    local_small = {"b_ada": dmod, "g_pre_mix": small["g_pre_mix"], "g_post_mix": small["g_post_mix"],
                   "b_conv_ssm": small["b_conv_ssm"], "dt_bias_fwd": small["dt_bias"][:, :n_heads],
                   "dt_bias_bwd": small["dt_bias"][:, n_heads:], "a_log_fwd": small["a_log"][:, :n_heads],
                   "a_log_bwd": small["a_log"][:, n_heads:],
                   "d_skip": jnp.sum(small["d_skip_x"].reshape(n_heads, HEAD_DIM), axis=1),
                   "g_ssm_norm": small["g_ssm_norm"], "b_glu": small["b_glu"], "b_dw": small["b_dw"],
                   "ln_g": small["ln_g"], "ln_b": small["ln_b"], "b_conv_out": small["b_conv_out"],
                   "b_gate": small["b_gate"], "g_pre_ffn": small["g_pre_ffn"], "g_post_ffn": small["g_post_ffn"],
                   "w_conv_ssm": small["w_conv_ssm"], "w_dw": small["w_dw"]}
    names = SMALL_PARAMS + SHARDED_SMALL
    pack = _pack([local_small[n] for n in names], d)
    rows_p = pack.shape[0]
    g3 = _gather_devices("gather_small_grads", pack).reshape(N_DEV, rows_p, d)
    total = _rowwise("sum_small_grads", lambda *blocks: (functools.reduce(lambda a, b: a + b, blocks),),
                     [g3[i] for i in range(N_DEV)], [], [(d, F32)])[0]
    full_shapes = [wgt[n].shape for n in SMALL_PARAMS] + [(k_conv, d_xbc), (k_dw, d)]
    summed = dict(zip(names, _unpack(total, full_shapes, d)))
    grads = {n: summed[n] for n in SMALL_PARAMS}
    grads["w_conv_ssm"] = lax.dynamic_slice_in_dim(summed["w_conv_ssm"], chip * xbc_shard, xbc_shard, axis=1)
    grads["w_dw"] = lax.dynamic_slice_in_dim(summed["w_dw"], chip * dw_shard, dw_shard, axis=1)

    dmod_all = g3[:, :N_MOD, :].reshape(N_DEV, N_MOD * d)
    ada_cols = wgt["w_ada"].shape[1]
    dmod_cols = jnp.pad(lax.dynamic_slice_in_dim(dmod_all, chip * ada_cols, ada_cols, axis=1),
                        ((0, MOD_ROWS - N_DEV), (0, 0)))
    grads["w_ada"] = _matmul("g_w_ada", c_act, dmod_cols, ta=True)

    big_list = [big["w_in_t"].reshape(N_CHIPS, -1, d)] + [
        big[n] if n == "w_gate_up" else big[n].reshape(N_CHIPS, -1, d) for n in MATRICES[1:]]
    reduced = _reduce_scatter(big_list, core_arr, chip_arr)
    grads["w_in"] = reduced[0].T
    for n, g in zip(MATRICES[1:], reduced[1:]):
        grads[n] = g

    delta, new_m, new_v = {}, {}, {}
    for n in ("w_ada",) + MATRICES:
        delta[n], new_m[n], new_v[n] = _adamw("adamw_" + n, wgt[n], grads[n], mom[n], var[n])
    for group, width, tag in ((SMALL_PARAMS, d, "small"), (SHARDED_SMALL, LANES, "conv")):
        shapes = [wgt[n].shape for n in group]
        packs = [_pack([src[n] for n in group], width) for src in (wgt, grads, mom, var)]
        outs = _adamw("adamw_" + tag, *packs)
        for res, o in zip((delta, new_m, new_v), outs):
            res.update(zip(group, _unpack(o, shapes, width)))

    lead = lambda a: a[None]
    return (loss, grad_x[None], *[lead(grads[n]) for n in ALL_PARAMS], *[lead(delta[n]) for n in ALL_PARAMS],
            *[lead(new_m[n]) for n in ALL_PARAMS], *[lead(new_v[n]) for n in ALL_PARAMS])
```

```python
import functools

import jax
import jax.numpy as jnp
from jax import lax
from jax.experimental import pallas as pl
from jax.experimental.pallas import tpu as pltpu

F32 = jnp.float32
BF16 = jnp.bfloat16

N_GROUPS = 8
HEAD_DIM = 64
D_STATE = 128
CHUNK = 128
EPS = 1e-6
N_MOD = 6
ADAM_LR = 0.001
ADAM_B1 = 0.9
ADAM_B2 = 0.999
ADAM_EPS = 1e-08
ADAM_WD = 0.01
ADAM_STEP = 10

V7X_VMEM_BYTES = 64 * 1024 * 1024
VMEM_LIMIT = V7X_VMEM_BYTES - 8 * 1024 * 1024
ROW_TILE_BUDGET = 20 * 1024 * 1024
LANES = 128
NEG = -1e30
MESH = pl.DeviceIdType.MESH
N_CHIPS = 4
N_DEV = 8


def _params(sem):
    return pltpu.CompilerParams(dimension_semantics=sem, vmem_limit_bytes=VMEM_LIMIT)


def _sigmoid(x):
    return 1.0 / (1.0 + jnp.exp(-x))


def _silu(x):
    return x * _sigmoid(x)


def _dsilu(x):
    s = _sigmoid(x)
    return s * (1.0 + x * (1.0 - s))


def _softplus(x):
    return jnp.maximum(x, 0.0) + jnp.log(1.0 + jnp.exp(-jnp.abs(x)))


def _sum0(a):
    return jnp.sum(a, axis=0, keepdims=True)


def _mean1(a):
    return jnp.mean(a, axis=1, keepdims=True)


def _rowwise(name, body, rows, bcasts, out_rows, out_accs=(), tile=None):
    rows = [r if isinstance(r, tuple) else (r, r.shape[1], 0) for r in rows]
    s = rows[0][0].shape[0]
    if tile is None:
        per_row = sum(w * a.dtype.itemsize for a, w, _ in rows) + sum(w * jnp.dtype(dt).itemsize for w, dt in out_rows)
        tile = 1024
        while tile > 16 and (tile * per_row * 2 > ROW_TILE_BUDGET or s % tile):
            tile //= 2
        if s * per_row * 2 <= ROW_TILE_BUDGET:
            tile = s
    assert s % tile == 0
    n_in = len(rows) + len(bcasts)
    n_o = len(out_rows)

    def kern(*refs):
        res = body(*[r[...].astype(F32) for r in refs[:n_in]])
        outs = refs[n_in:]
        for o, v in zip(outs[:n_o], res[:n_o]):
            o[...] = v.astype(o.dtype)
        if out_accs:
            @pl.when(pl.program_id(0) == 0)
            def _():
                for o in outs[n_o:]:
                    o[...] = jnp.zeros_like(o)
            for o, v in zip(outs[n_o:], res[n_o:]):
                o[...] += v

    in_specs = [pl.BlockSpec((tile, w), functools.partial(lambda i, cb: (i, cb), cb=cb)) for _, w, cb in rows]
    in_specs += [pl.BlockSpec(b.shape, functools.partial(lambda i, nd: (0,) * nd, nd=b.ndim)) for b in bcasts]
    out_shape = [jax.ShapeDtypeStruct((s, w), dt) for w, dt in out_rows]
    out_shape += [jax.ShapeDtypeStruct((1, w), F32) for w in out_accs]
    out_specs = [pl.BlockSpec((tile, w), lambda i: (i, 0)) for w, _ in out_rows]
    out_specs += [pl.BlockSpec((1, w), lambda i: (0, 0)) for w in out_accs]
    return pl.pallas_call(
        kern, name=name, grid=(s // tile,), in_specs=in_specs, out_specs=out_specs, out_shape=out_shape,
        compiler_params=_params(("arbitrary",) if out_accs else ("parallel",)),
    )(*[a for a, _, _ in rows], *bcasts)


def _tile(n, pref):
    if n <= pref:
        return n
    t = (pref // LANES) * LANES
    while t >= LANES:
        if n % t == 0:
            return t
        t -= LANES
    return n


def _matmul(name, a, b, *, ta=False, tb=False, out_dtype=F32, bias=None, add=None, b_blocks=1, out_blocks=1,
            tm=1024, tn=1408, tk=2048):
    m, k = (a.shape[1], a.shape[0]) if ta else a.shape
    if b_blocks > 1:
        rows_b, cols_b = b.shape[1], b.shape[2] * b_blocks
    else:
        rows_b, cols_b = b.shape
    n, kb = (rows_b, cols_b) if tb else (cols_b, rows_b)
    assert k == kb, (name, a.shape, b.shape)
    tm, tn, tk = _tile(m, tm), _tile(n, tn), _tile(k, tk)
    if b_blocks > 1:
        per = cols_b // b_blocks
        if tb:
            tk = _tile(per, tk)
        else:
            tn = _tile(per, tn)
    if out_blocks > 1:
        tn = _tile(n // out_blocks, tn)
    nk = k // tk
    grid = (m // tm, n // tn, nk)

    a_spec = pl.BlockSpec((tk, tm), lambda i, j, kk: (kk, i)) if ta else pl.BlockSpec((tm, tk), lambda i, j, kk: (i, kk))
    if b_blocks > 1:
        if tb:
            nb = per // tk
            b_spec = pl.BlockSpec((1, tn, tk), lambda i, j, kk: (kk // nb, j, kk % nb))
        else:
            nb = per // tn
            b_spec = pl.BlockSpec((1, tk, tn), lambda i, j, kk: (j // nb, kk, j % nb))
    else:
        b_spec = pl.BlockSpec((tn, tk), lambda i, j, kk: (j, kk)) if tb else pl.BlockSpec((tk, tn), lambda i, j, kk: (kk, j))
    in_specs = [a_spec, b_spec]
    operands = [a, b]
    if bias is not None:
        in_specs.append(pl.BlockSpec((1, tn), lambda i, j, kk: (0, j)))
        operands.append(bias)
    if add is not None:
        in_specs.append(pl.BlockSpec((tm, tn), lambda i, j, kk: (i, j)))
        operands.append(add)
    if out_blocks > 1:
        nbo = (n // out_blocks) // tn
        out_spec = pl.BlockSpec((1, tm, tn), lambda i, j, kk: (j // nbo, i, j % nbo))
        out_shape = jax.ShapeDtypeStruct((out_blocks, m, n // out_blocks), out_dtype)
    else:
        out_spec = pl.BlockSpec((tm, tn), lambda i, j, kk: (i, j))
        out_shape = jax.ShapeDtypeStruct((m, n), out_dtype)
    dims = (((0 if ta else 1,), (1 if tb else 0,)), ((), ()))
    has_bias, has_add = bias is not None, add is not None

    def kern(*refs):
        a_ref, b_ref = refs[0], refs[1]
        pos = 2
        bias_ref = add_ref = None
        if has_bias:
            bias_ref = refs[pos]
            pos += 1
        if has_add:
            add_ref = refs[pos]
            pos += 1
        o_ref = refs[pos]
        acc_ref = refs[pos + 1] if nk > 1 else None
        av = a_ref[...].astype(BF16)
        bv = (b_ref[0] if b_blocks > 1 else b_ref[...]).astype(BF16)
        p = lax.dot_general(av, bv, dims, preferred_element_type=F32)

        def finish(acc):
            if has_bias:
                acc = acc + bias_ref[...]
            if has_add:
                acc = acc + add_ref[...]
            if out_blocks > 1:
                o_ref[0] = acc.astype(o_ref.dtype)
            else:
                o_ref[...] = acc.astype(o_ref.dtype)

        if nk == 1:
            finish(p)
        else:
            kk = pl.program_id(2)

            @pl.when(kk == 0)
            def _():
                acc_ref[...] = p

            @pl.when(kk > 0)
            def _():
                acc_ref[...] += p

            @pl.when(kk == nk - 1)
            def _():
                finish(acc_ref[...])

    return pl.pallas_call(
        kern, name=name, grid=grid, in_specs=in_specs, out_specs=out_spec, out_shape=out_shape,
        scratch_shapes=[pltpu.VMEM((tm, tn), F32)] if nk > 1 else [],
        compiler_params=_params(("parallel", "parallel", "arbitrary")),
    )(*operands)


CONV_HALO = 16
CONV_ROWS = 256


def _shifted(win, s, rows):
    n = win.shape[0]
    return (pltpu.roll(win, (n - s) % n, axis=0) if s % n else win)[:rows]


def _dwconv_fwd(name, x, w, b, *, silu, out_dtype=F32):
    s, c = x.shape
    k = w.shape[0]
    pad = (k - 1) // 2
    assert pad <= CONV_HALO and c % LANES == 0
    t = min(CONV_ROWS, s)
    n_chunks = s // t

    def kern(x_ref, w_ref, b_ref, o_ref, xp_ref):
        zeros = jnp.zeros((CONV_HALO, LANES), F32)
        xp_ref[0:CONV_HALO, :] = zeros
        xp_ref[CONV_HALO + s:CONV_HALO + s + CONV_HALO, :] = zeros
        xp_ref[CONV_HALO:CONV_HALO + s, :] = x_ref[...].astype(F32)
        bv = b_ref[...]

        def chunk(i, carry):
            base = pl.multiple_of(i * t, 16)
            win = xp_ref[pl.ds(base, t + 2 * CONV_HALO), :]
            acc = jnp.zeros((t, LANES), F32)
            for j in range(k):
                acc = acc + _shifted(win, CONV_HALO - pad + j, t) * w_ref[pl.ds(j, 1), :]
            acc = acc + bv
            o_ref[pl.ds(base, t), :] = (_silu(acc) if silu else acc).astype(o_ref.dtype)
            return carry

        lax.fori_loop(0, n_chunks, chunk, 0)

    return pl.pallas_call(
        kern, name=name, grid=(c // LANES,),
        in_specs=[pl.BlockSpec((s, LANES), lambda i: (0, i)), pl.BlockSpec((k, LANES), lambda i: (0, i)),
                  pl.BlockSpec((1, LANES), lambda i: (0, i))],
        out_specs=pl.BlockSpec((s, LANES), lambda i: (0, i)),
        out_shape=jax.ShapeDtypeStruct((s, c), out_dtype),
        scratch_shapes=[pltpu.VMEM((s + 2 * CONV_HALO, LANES), F32)],
        compiler_params=_params(("parallel",)),
    )(x, w, b)


def _dwconv_bwd(name, x, w, b, dout, *, silu, dx_dtype=F32):
    s, c = x.shape
    k = w.shape[0]
    pad = (k - 1) // 2
    t = min(CONV_ROWS, s)
    n_chunks = s // t

    def kern(x_ref, w_ref, b_ref, do_ref, dx_ref, dw_ref, db_ref, xp_ref, dp_ref):
        zeros = jnp.zeros((CONV_HALO, LANES), F32)
        for ref in (xp_ref, dp_ref):
            ref[0:CONV_HALO, :] = zeros
            ref[CONV_HALO + s:CONV_HALO + s + CONV_HALO, :] = zeros
        xp_ref[CONV_HALO:CONV_HALO + s, :] = x_ref[...].astype(F32)
        bv = b_ref[...]
        dw_ref[...] = jnp.zeros_like(dw_ref)

        def pre_chunk(i, dbias):
            base = pl.multiple_of(i * t, 16)
            win = xp_ref[pl.ds(base, t + 2 * CONV_HALO), :]
            dpre = do_ref[pl.ds(base, t), :].astype(F32)
            if silu:
                acc = jnp.zeros((t, LANES), F32)
                for j in range(k):
                    acc = acc + _shifted(win, CONV_HALO - pad + j, t) * w_ref[pl.ds(j, 1), :]
                dpre = dpre * _dsilu(acc + bv)
            dp_ref[pl.ds(base + CONV_HALO, t), :] = dpre
            for j in range(k):
                dw_ref[pl.ds(j, 1), :] += _sum0(dpre * _shifted(win, CONV_HALO - pad + j, t))
            return dbias + _sum0(dpre)

        db_ref[...] = lax.fori_loop(0, n_chunks, pre_chunk, jnp.zeros((1, LANES), F32))

        def dx_chunk(i, carry):
            base = pl.multiple_of(i * t, 16)
            win = dp_ref[pl.ds(base, t + 2 * CONV_HALO), :]
            acc = jnp.zeros((t, LANES), F32)
            for j in range(k):
                acc = acc + _shifted(win, CONV_HALO + pad - j, t) * w_ref[pl.ds(j, 1), :]
            dx_ref[pl.ds(base, t), :] = acc.astype(dx_ref.dtype)
            return carry

        lax.fori_loop(0, n_chunks, dx_chunk, 0)

    col = lambda rows: pl.BlockSpec((rows, LANES), lambda i: (0, i))
    return pl.pallas_call(
        kern, name=name, grid=(c // LANES,),
        in_specs=[col(s), col(k), col(1), col(s)],
        out_specs=[col(s), col(k), col(1)],
        out_shape=[jax.ShapeDtypeStruct((s, c), dx_dtype), jax.ShapeDtypeStruct((k, c), F32),
                   jax.ShapeDtypeStruct((1, c), F32)],
        scratch_shapes=[pltpu.VMEM((s + 2 * CONV_HALO, LANES), F32), pltpu.VMEM((s + 2 * CONV_HALO, LANES), F32)],
        compiler_params=_params(("parallel",)),
    )(x, w, b, dout)


_NT =(((1,), (1,)), ((), ()))
_TN = (((0,), (0,)), ((), ()))


def _dot(a, b, dims=None):
    if dims is None:
        return jnp.dot(a, b, preferred_element_type=F32)
    return lax.dot_general(a, b, dims, preferred_element_type=F32)


HEAD_QUANTITIES = 4
GROUPS_PER_STEP = 2


def _scan_tables(n_heads):
    j_heads = n_heads // N_GROUPS
    used = 3 * HEAD_QUANTITIES * j_heads
    assert used <= LANES and j_heads % 2 == 0 and N_GROUPS % GROUPS_PER_STEP == 0
    gw = j_heads * HEAD_DIM
    r = jnp.arange(LANES)[:, None]

    def expand(quantity, width):
        head_of_lane = jnp.arange(j_heads * width)[None] // width
        return ((r // (3 * j_heads) == quantity) & (r % j_heads == head_of_lane) & (r < used)).astype(BF16)

    sel_cols = (jnp.arange(gw)[:, None] // HEAD_DIM == jnp.arange(LANES)[None]).astype(BF16)
    h2 = 2 * n_heads
    rows = jnp.arange(3 * HEAD_QUANTITIES * h2)
    head = rows % n_heads
    col = ((rows % h2) // n_heads * N_GROUPS + head // j_heads) * LANES + (rows // h2) * j_heads + head % j_heads
    route = (col[:, None] == jnp.arange(2 * N_GROUPS * LANES)[None]).astype(BF16)
    return {"ex_dt": expand(0, HEAD_DIM), "ex_gam": expand(1, CHUNK), "ex_din": expand(2, HEAD_DIM),
            "ex_dst": expand(3, HEAD_DIM), "sel_cols": sel_cols, "route": route}


def _chunk_decay(etot, n_heads, direction):
    j_heads = n_heads // N_GROUPS
    ed = etot[:, :, direction * n_heads:(direction + 1) * n_heads]
    per_group = jnp.pad(ed.reshape(ed.shape[0], 8, N_GROUPS, j_heads), ((0, 0), (0, 0), (0, 0), (0, LANES - j_heads)))
    return jnp.repeat(ed, HEAD_DIM, axis=2), per_group.reshape(ed.shape[0], 8, N_GROUPS * LANES)


def _scan_specs(reverse_order, direction, nc, j_heads, d_ssm):
    gps = GROUPS_PER_STEP
    gw = j_heads * HEAD_DIM
    b_off = d_ssm // (gps * D_STATE)
    c_off = b_off + N_GROUPS // gps
    d_off = direction * (N_GROUPS // gps)
    zz = (lambda z: nc - 1 - z) if reverse_order else (lambda z: z)
    const = lambda shape: pl.BlockSpec(shape, lambda g, z: (0,) * len(shape))
    return {
        "xs": pl.BlockSpec((CHUNK, gps * gw), lambda g, z: (zz(z), g)),
        "b": pl.BlockSpec((CHUNK, gps * D_STATE), lambda g, z: (zz(z), b_off + g)),
        "c": pl.BlockSpec((CHUNK, gps * D_STATE), lambda g, z: (zz(z), c_off + g)),
        "q": pl.BlockSpec((CHUNK, gps * LANES), lambda g, z: (zz(z), d_off + g)),
        "gam_t": pl.BlockSpec((gps * j_heads, CHUNK), lambda g, z: (d_off + g, zz(z))),
        "etot_x": pl.BlockSpec((1, 8, gps * gw), lambda g, z: (zz(z), 0, g)),
        "etot_g": pl.BlockSpec((1, 8, gps * LANES), lambda g, z: (zz(z), 0, g)),
        "state": pl.BlockSpec((gps, 1, D_STATE, gw), lambda g, z: (g, zz(z), 0, 0)),
        "grp": pl.BlockSpec((CHUNK, gps * D_STATE), lambda g, z: (zz(z), g)),
        "ex": const((LANES, gw)), "ex_gam": const((LANES, j_heads * CHUNK)), "sel": const((gw, LANES)),
    }


def _scan_masks(reverse):
    li = lax.broadcasted_iota(jnp.int32, (CHUNK, CHUNK), 0)
    si = lax.broadcasted_iota(jnp.int32, (CHUNK, CHUNK), 1)
    mask = (li <= si) if reverse else (li >= si)
    mask_t = (si <= li) if reverse else (si >= li)
    return li, si, mask, mask_t, si < HEAD_DIM


def _scan_fwd(name, xbc_c, q_all, gam_t, etot_x, tb, *, direction, d_ssm):
    s = xbc_c.shape[0]
    nc = s // CHUNK
    j_heads = tb["ex_dt"].shape[1] // HEAD_DIM
    gw = j_heads * HEAD_DIM
    gps = GROUPS_PER_STEP
    reverse = direction == 1
    sp = _scan_specs(reverse, direction, nc, j_heads, d_ssm)

    def kern(xs_ref, b_ref, c_ref, q_ref, gamt_ref, etx_ref, exdt_ref, exgam_ref, exdin_ref, exdst_ref,
             y_ref, hs_ref, h_ref):
        @pl.when(pl.program_id(1) == 0)
        def _():
            h_ref[...] = jnp.zeros_like(h_ref)

        _, _, mask, _, lo = _scan_masks(reverse)
        for gi in range(gps):
            bb = b_ref[:, gi * D_STATE:(gi + 1) * D_STATE].astype(BF16)
            cb = c_ref[:, gi * D_STATE:(gi + 1) * D_STATE].astype(BF16)
            cbt = _dot(cb, bb, _NT)
            q = q_ref[:, gi * LANES:(gi + 1) * LANES]
            dtx, dinx, dstx = _dot(q, exdt_ref[...]), _dot(q, exdin_ref[...]), _dot(q, exdst_ref[...])
            gcol = _dot(q, exgam_ref[...])
            xdt = xs_ref[:, gi * gw:(gi + 1) * gw].astype(F32) * dtx
            ht = h_ref[gi]
            y_off = _dot(cb, ht.astype(BF16)) * dinx
            hs_ref[gi, 0] = ht
            for p in range(j_heads // 2):
                lanes = slice(p * CHUNK, (p + 1) * CHUNK)
                x2 = xdt[:, lanes]
                acc = y_off[:, lanes]
                for idx, j in enumerate((2 * p, 2 * p + 1)):
                    g_row = gamt_ref[pl.ds(gi * j_heads + j, 1), :]
                    decay = jnp.exp(jnp.where(mask, gcol[:, j * CHUNK:(j + 1) * CHUNK] - g_row, NEG))
                    x_head = jnp.where(lo if idx == 0 else jnp.logical_not(lo), x2, 0.0).astype(BF16)
                    acc = acc + _dot((cbt * decay).astype(BF16), x_head)
                y_ref[:, gi * gw + p * CHUNK:gi * gw + (p + 1) * CHUNK] = acc.astype(y_ref.dtype)
            h_ref[gi] = ht * etx_ref[0, 0:1, gi * gw:(gi + 1) * gw] + _dot(bb, (xdt * dstx).astype(BF16), _TN)

    return pl.pallas_call(
        kern, name=name, grid=(N_GROUPS // gps, nc),
        in_specs=[sp["xs"], sp["b"], sp["c"], sp["q"], sp["gam_t"], sp["etot_x"], sp["ex"], sp["ex_gam"], sp["ex"], sp["ex"]],
        out_specs=[sp["xs"], sp["state"]],
        out_shape=[jax.ShapeDtypeStruct((s, d_ssm), BF16), jax.ShapeDtypeStruct((N_GROUPS, nc, D_STATE, gw), F32)],
        scratch_shapes=[pltpu.VMEM((gps, D_STATE, gw), F32)],
        compiler_params=_params(("parallel", "arbitrary")),
    )(xbc_c, xbc_c, xbc_c, q_all, gam_t, etot_x, tb["ex_dt"], tb["ex_gam"], tb["ex_din"], tb["ex_dst"])


def _scan_bwd(name, xbc_c, dy, hs, q_all, gam_t, etot_x, etot_g, tb, *, direction, d_ssm):
    s = xbc_c.shape[0]
    nc = s // CHUNK
    j_heads = tb["ex_dt"].shape[1] // HEAD_DIM
    gw = j_heads * HEAD_DIM
    gps = GROUPS_PER_STEP
    reverse = direction == 1
    sp = _scan_specs(not reverse, direction, nc, j_heads, d_ssm)
    hp = lax.Precision.HIGHEST

    def kern(xs_ref, b_ref, c_ref, dy_ref, hs_ref, q_ref, gamt_ref, etg_ref, etx_ref, exdt_ref, exgam_ref, exdin_ref,
             exdst_ref, sel_ref, dxs_ref, db_ref, dc_ref, ddt_ref, da_ref, dh_ref, tmp_ref):
        @pl.when(pl.program_id(1) == 0)
        def _():
            dh_ref[...] = jnp.zeros_like(dh_ref)

        li, si, mask, mask_t, lo = _scan_masks(reverse)
        sel = sel_ref[...]
        incl = ((si <= li) if reverse else (si >= li)).astype(F32)
        excl = ((si > li) if reverse else (si < li)).astype(F32)
        for gi in range(gps):
            grp_lanes = slice(gi * D_STATE, (gi + 1) * D_STATE)
            bb = b_ref[:, grp_lanes].astype(BF16)
            cb = c_ref[:, grp_lanes].astype(BF16)
            cbt = _dot(cb, bb, _NT)
            cbt_t = _dot(bb, cb, _NT)
            q = q_ref[:, gi * LANES:(gi + 1) * LANES]
            dtx, dinx, dstx = _dot(q, exdt_ref[...]), _dot(q, exdin_ref[...]), _dot(q, exdst_ref[...])
            gcol = _dot(q, exgam_ref[...])
            x_all = xs_ref[:, gi * gw:(gi + 1) * gw].astype(F32)
            dy_all = dy_ref[:, gi * gw:(gi + 1) * gw].astype(F32)
            xdt = x_all * dtx
            xb = xdt.astype(BF16)
            ht = hs_ref[gi, 0]
            hb = ht.astype(BF16)
            dht = dh_ref[gi]
            dhb = dht.astype(BF16)
            y_off = _dot(cb, hb) * dinx
            dx_off = _dot(bb, dhb) * dstx
            dyd = (dy_all * dinx).astype(BF16)
            xd = (xdt * dstx).astype(BF16)
            dc_acc = _dot(dyd, hb, _NT)
            db_acc = _dot(xd, dhb, _NT)
            dh_ref[gi] = dht * etx_ref[0, 0:1, gi * gw:(gi + 1) * gw] + _dot(cb, dyd, _TN)
            q_cols = _dot((dy_all * y_off).astype(BF16), sel)
            c_cols = _dot((xdt * dx_off).astype(BF16), sel)
            through = _sum0(_dot((dht * ht).astype(BF16), sel))
            dcbt = jnp.zeros((CHUNK, CHUNK), F32)
            for p in range(j_heads // 2):
                lanes = slice(p * CHUNK, (p + 1) * CHUNK)
                out_lanes = slice(gi * gw + p * CHUNK, gi * gw + (p + 1) * CHUNK)
                x2b = xb[:, lanes]
                dy2 = dy_all[:, lanes]
                acc = dx_off[:, lanes]
                for idx, j in enumerate((2 * p, 2 * p + 1)):
                    gc = gcol[:, j * CHUNK:(j + 1) * CHUNK]
                    gr = gamt_ref[pl.ds(gi * j_heads + j, 1), :]
                    decay = jnp.exp(jnp.where(mask, gc - gr, NEG))
                    decay_t = jnp.exp(jnp.where(mask_t, gr - gc, NEG))
                    dy_head = jnp.where(lo if idx == 0 else jnp.logical_not(lo), dy2, 0.0).astype(BF16)
                    acc = acc + _dot((cbt_t * decay_t).astype(BF16), dy_head)
                    dm = decay * _dot(dy_head, x2b, _NT)
                    dcbt = dcbt + dm
                    e = (cbt * dm).astype(BF16)
                    in_lane_j = si == j
                    q_cols = (q_cols + jnp.where(in_lane_j, jnp.sum(e.astype(F32), axis=1, keepdims=True), 0.0)
                              - _dot(e, jnp.where(in_lane_j, 1.0, 0.0).astype(BF16), _TN))
                dxs_ref[:, out_lanes] = (acc * dtx[:, lanes]).astype(dxs_ref.dtype)
                tmp_ref[:, lanes] = acc * x_all[:, lanes]
            dcb = dcbt.astype(BF16)
            dc_ref[:, grp_lanes] = (dc_acc + _dot(dcb, bb)).astype(dc_ref.dtype)
            db_ref[:, grp_lanes] = (db_acc + _dot(dcb, cb, _TN)).astype(db_ref.dtype)
            ddt_ref[:, gi * LANES:(gi + 1) * LANES] = _dot(tmp_ref[...].astype(BF16), sel)
            da_ref[:, gi * LANES:(gi + 1) * LANES] = (
                jnp.dot(incl, q_cols, preferred_element_type=F32, precision=hp)
                + jnp.dot(excl, c_cols, preferred_element_type=F32, precision=hp)
                + through * etg_ref[0, 0:1, gi * LANES:(gi + 1) * LANES])

    gn = N_GROUPS * D_STATE
    return pl.pallas_call(
        kern, name=name, grid=(N_GROUPS // gps, nc),
        in_specs=[sp["xs"], sp["b"], sp["c"], sp["xs"], sp["state"], sp["q"], sp["gam_t"], sp["etot_g"], sp["etot_x"],
                  sp["ex"], sp["ex_gam"], sp["ex"], sp["ex"], sp["sel"]],
        out_specs=[sp["xs"], sp["grp"], sp["grp"], sp["grp"], sp["grp"]],
        out_shape=[jax.ShapeDtypeStruct((s, d_ssm), BF16), jax.ShapeDtypeStruct((s, gn), BF16),
                   jax.ShapeDtypeStruct((s, gn), BF16), jax.ShapeDtypeStruct((s, gn), F32),
                   jax.ShapeDtypeStruct((s, gn), F32)],
        scratch_shapes=[pltpu.VMEM((gps, D_STATE, gw), F32), pltpu.VMEM((CHUNK, gw), F32)],
        compiler_params=_params(("parallel", "arbitrary")),
    )(xbc_c, xbc_c, xbc_c, dy, hs, q_all, gam_t, etot_g, etot_x, tb["ex_dt"], tb["ex_gam"], tb["ex_din"], tb["ex_dst"],
      tb["sel_cols"])


def _dt_prepare(dt_raw, dt_bias, a_neg, route):
    s, h2 = dt_raw.shape
    n_heads = h2 // 2
    qw = route.shape[1]

    def kern(raw_ref, bias_ref, a_ref, route_ref, dt_ref, q_ref, gamt_ref, etot_ref):
        dt = _softplus(raw_ref[...] + bias_ref[...])
        a = dt * a_ref[...]
        li = lax.broadcasted_iota(jnp.int32, (CHUNK, CHUNK), 0)
        si = lax.broadcasted_iota(jnp.int32, (CHUNK, CHUNK), 1)
        tri = (li >= si).astype(F32)
        cs = jnp.dot(tri, a, preferred_element_type=F32, precision=lax.Precision.HIGHEST)
        tot = _sum0(a)
        fwd = lax.broadcasted_iota(jnp.int32, (CHUNK, h2), 1) < n_heads
        gam = jnp.where(fwd, cs, a - cs)
        din = jnp.where(fwd, jnp.exp(cs), jnp.exp(tot + gam))
        dst = jnp.where(fwd, jnp.exp(tot - cs), jnp.exp(cs - a))
        pieces = []
        for v in (dt, gam, din, dst):
            hi = v.astype(BF16)
            rest = v - hi.astype(F32)
            mid = rest.astype(BF16)
            pieces += [hi, mid, (rest - mid.astype(F32)).astype(BF16)]
        q_ref[...] = _dot(jnp.concatenate(pieces, axis=1), route_ref[...]).astype(BF16)
        dt_ref[...] = dt
        gamt_ref[...] = gam.T
        etot_ref[...] = jnp.broadcast_to(jnp.exp(tot), (8, h2))

    nc = s // CHUNK
    rows = lambda w: pl.BlockSpec((CHUNK, w), lambda i: (i, 0))
    whole = lambda a: pl.BlockSpec(a.shape, lambda i: (0, 0))
    return pl.pallas_call(
        kern, name="dt_prepare", grid=(nc,),
        in_specs=[rows(h2), whole(dt_bias), whole(a_neg), whole(route)],
        out_specs=[rows(h2), rows(qw), pl.BlockSpec((h2, CHUNK), lambda i: (0, i)), pl.BlockSpec((8, h2), lambda i: (i, 0))],
        out_shape=[jax.ShapeDtypeStruct((s, h2), F32), jax.ShapeDtypeStruct((s, qw), BF16),
                   jax.ShapeDtypeStruct((h2, s), F32), jax.ShapeDtypeStruct((nc * 8, h2), F32)],
        compiler_params=_params(("parallel",)),
    )(dt_raw, dt_bias, a_neg, route)


def _from_group_lanes(arr, j_heads):
    s = arr.shape[0]
    return arr.reshape(s, N_GROUPS, LANES)[:, :, :j_heads].reshape(s, N_GROUPS * j_heads)


def _dt_backward(da, ddt, dt, dt_raw, dt_bias, a_neg):
    h2 = da.shape[1]

    def body(dav, ddtv, dtv, raw, bias, a_head):
        draw = (ddtv + dav * a_head) * _sigmoid(raw + bias)
        return draw, _sum0(draw), _sum0(dav * dtv) * a_head
    return _rowwise("dt_backward", body, [da, ddt, dt, dt_raw], [dt_bias, a_neg], [(h2, BF16)], [h2, h2])


def _rms(x):
    r = lax.rsqrt(_mean1(x * x) + EPS)
    return x * r, r


def _rms_bwd(dy, y, r):
    return r * (dy - y * _mean1(dy * y))


def _norm_mod_fwd(name, x, g, sc, sh):
    def body(xv, gv, scv, shv):
        y, _ = _rms(xv)
        return ((y * gv) * (1.0 + scv) + shv,)
    return _rowwise(name, body, [x], [g, sc, sh], [(x.shape[1], BF16)])[0]


def _norm_mod_bwd(name, x, dh, dpass, g, sc):
    d = x.shape[1]

    def body(xv, dhv, dpv, gv, scv):
        y, r = _rms(xv)
        dn = dhv * (1.0 + scv)
        dx = _rms_bwd(dn * gv, y, r) + dpv
        return dx, _sum0(dn * y), _sum0(dhv * (y * gv)), _sum0(dhv)
    return _rowwise(name, body, [x, dh, dpass], [g, sc], [(d, F32)], [d, d, d])


def _gated_residual_fwd(name, x, m, gate, gp):
    def body(xv, mv, gatev, gpv):
        y, _ = _rms(mv)
        return (xv + gatev * (y * gpv),)
    return _rowwise(name, body, [x, m], [gate, gp], [(x.shape[1], F32)])[0]


def _gated_residual_bwd(name, m, dx1, gate, gp):
    d = m.shape[1]

    def body(mv, dv, gatev, gpv):
        y, r = _rms(mv)
        dn = dv * gatev
        return _rms_bwd(dn * gpv, y, r), _sum0(dv * (y * gpv)), _sum0(dn * y)
    return _rowwise(name, body, [m, dx1], [gate, gp], [(d, BF16)], [d, d])


def _final_residual_loss(x1, f, tgt, gate, gp):
    d = x1.shape[1]

    def body(xv, fv, tv, gatev, gpv):
        y, r = _rms(fv)
        n = y * gpv
        err = xv + gatev * n - tv
        dx2 = err * (1.0 / d)
        dn = dx2 * gatev
        sq = jnp.sum(_sum0(err * err), axis=1, keepdims=True)
        return dx2, _rms_bwd(dn * gpv, y, r), jnp.broadcast_to(sq, (1, LANES)), _sum0(dx2 * n), _sum0(dn * y)
    return _rowwise("final_residual_loss", body, [x1, f, tgt], [gate, gp], [(d, F32), (d, BF16)], [LANES, d, d])


def _swiglu_fwd(gu):
    f = gu.shape[1] // 2

    def body(v):
        return (_silu(v[:, :f]) * v[:, f:],)
    return _rowwise("swiglu_fwd", body, [gu], [], [(f, BF16)])[0]


def _swiglu_bwd(gu, dact):
    f = gu.shape[1] // 2

    def body(v, dv):
        gt, up = v[:, :f], v[:, f:]
        return (jnp.concatenate([dv * up * _dsilu(gt), dv * _silu(gt)], axis=1),)
    return _rowwise("swiglu_bwd", body, [gu, dact], [], [(2 * f, BF16)])[0]


def _glu_fwd(glu_in, b_glu):
    c = glu_in.shape[1] // 2

    def body(v, bv):
        t = v + bv
        return (t[:, :c] * _sigmoid(t[:, c:]),)
    return _rowwise("glu_fwd", body, [glu_in], [b_glu], [(c, F32)])[0]


def _glu_bwd(glu_in, b_glu, du0):
    c = glu_in.shape[1] // 2

    def body(v, dv, bv):
        t = v + bv
        a, s = t[:, :c], _sigmoid(t[:, c:])
        dg = jnp.concatenate([dv * s, dv * a * s * (1.0 - s)], axis=1)
        return dg, _sum0(dg)
    return _rowwise("glu_bwd", body, [glu_in, du0], [b_glu], [(2 * c, BF16)], [2 * c])


def _ln_parts(u1):
    xc = u1 - _mean1(u1)
    r = lax.rsqrt(_mean1(xc * xc) + EPS)
    return xc * r, r


def _ln_silu_fwd(u1, ln_g, ln_b):
    def body(v, gv, bv):
        yh, _ = _ln_parts(v)
        return (_silu(yh * gv + bv),)
    return _rowwise("ln_silu_fwd", body, [u1], [ln_g, ln_b], [(u1.shape[1], BF16)])[0]


def _ln_silu_bwd(u1, du, ln_g, ln_b):
    d = u1.shape[1]

    def body(v, dv, gv, bv):
        yh, r = _ln_parts(v)
        dl = dv * _dsilu(yh * gv + bv)
        dyh = dl * gv
        du1 = r * (dyh - _mean1(dyh) - yh * _mean1(dyh * yh))
        return du1, _sum0(dl * yh), _sum0(dl)
    return _rowwise("ln_silu_bwd", body, [u1, du], [ln_g, ln_b], [(d, F32)], [d, d])


def _gate_merge_fwd(y_a, y_b, gl, b_gate):
    d = y_a.shape[1]

    def body(ya, yb, glv, bv):
        s = _sigmoid(glv + bv)
        return (s[:, :d] * ya + s[:, d:] * yb,)
    return _rowwise("gate_merge_fwd", body, [y_a, y_b, gl], [b_gate], [(d, BF16)])[0]


def _gate_merge_bwd(dmixin, y_a, y_b, gl, b_gate):
    d = y_a.shape[1]

    def body(dv, ya, yb, glv, bv):
        s = _sigmoid(glv + bv)
        sa, sb = s[:, :d], s[:, d:]
        dya, dyb = dv * sa, dv * sb
        dgl = jnp.concatenate([dv * ya * sa * (1.0 - sa), dv * yb * sb * (1.0 - sb)], axis=1)
        return dya, dyb, dgl, _sum0(dgl), _sum0(dyb)
    return _rowwise("gate_merge_bwd", body, [dmixin, y_a, y_b, gl], [b_gate],
                    [(d, BF16), (d, BF16), (2 * d, BF16)], [2 * d, d])


def _group_slices(d_ssm):
    gw = d_ssm // N_GROUPS
    return [slice(g * gw, (g + 1) * gw) for g in range(N_GROUPS)]


def _gated_norm_fwd(y_f, y_b, xbc_c, z, d_skip_x, g_ssm):
    d_ssm = y_f.shape[1]

    def body(yf, yb, xs, zv, dsk, gv):
        y = yf + yb + dsk * xs
        v = y * _silu(zv)
        outs = []
        for sl in _group_slices(d_ssm):
            w, _ = _rms(v[:, sl])
            outs.append(w)
        return y, jnp.concatenate(outs, axis=1) * gv
    return _rowwise("gated_norm_fwd", body, [y_f, y_b, (xbc_c, d_ssm, 0), z], [d_skip_x, g_ssm],
                    [(d_ssm, F32), (d_ssm, BF16)])


def _gated_norm_bwd(y, z, dyn, xbc_c, d_skip_x, g_ssm):
    d_ssm = y.shape[1]

    def body(yv, zv, dv, xs, dsk, gv):
        sz = _silu(zv)
        v = yv * sz
        dw = dv * gv
        dvs, ws = [], []
        for sl in _group_slices(d_ssm):
            w, r = _rms(v[:, sl])
            ws.append(w)
            dvs.append(_rms_bwd(dw[:, sl], w, r))
        dvv = jnp.concatenate(dvs, axis=1)
        dy = dvv * sz
        return dy, dvv * yv * _dsilu(zv), _sum0(dv * jnp.concatenate(ws, axis=1)), _sum0(dy * xs)
    return _rowwise("gated_norm_bwd", body, [y, z, dyn, (xbc_c, d_ssm, 0)], [d_skip_x, g_ssm],
                    [(d_ssm, BF16), (d_ssm, BF16)], [d_ssm, d_ssm])


def _ssd_grad_merge(dxs_f, dxs_b, dy, db_f, db_b, dc_f, dc_b, d_skip_x):
    d_ssm = dy.shape[1]
    width = d_ssm + 2 * N_GROUPS * D_STATE

    def body(xf, xb, dv, bf, bb, cf, cbv, dsk):
        return (jnp.concatenate([xf + xb + dsk * dv, bf + bb, cf + cbv], axis=1),)
    return _rowwise("ssd_grad_merge", body, [dxs_f, dxs_b, dy, db_f, db_b, dc_f, dc_b], [d_skip_x], [(width, BF16)])[0]


def _adamw(name, w, g, m, v):
    c = w.shape[1]
    c1 = 1.0 - ADAM_B1 ** ADAM_STEP
    c2 = 1.0 - ADAM_B2 ** ADAM_STEP

    def body(wv, gv, mv, vv):
        mn = ADAM_B1 * mv + (1.0 - ADAM_B1) * gv
        vn = ADAM_B2 * vv + (1.0 - ADAM_B2) * (gv * gv)
        delta = -ADAM_LR * ((mn / c1) / (jnp.sqrt(vn / c2) + ADAM_EPS) + ADAM_WD * wv)
        return delta, mn, vn
    return _rowwise(name, body, [w, g, m, v], [], [(c, F32)] * 3)


def _local_step(x, tgt, mod, wts, sm):
    s, d = x.shape
    d_ssm = 2 * d
    n_heads = d_ssm // HEAD_DIM
    d_xbc = d_ssm + 2 * N_GROUPS * D_STATE
    sec = [0, d_ssm, d_ssm + d_xbc, d_ssm + d_xbc + 2 * n_heads, d_ssm + d_xbc + 2 * n_heads + 2 * d]
    sec.append(sec[-1] + 2 * d)
    sh1, sc1, g1, sh2, sc2, g2 = [mod[:, i * d:(i + 1) * d] for i in range(N_MOD)]
    win_t = wts["w_in_t"]
    win_secs = [win_t[sec[i]:sec[i + 1]] for i in range(5)]

    h1 = _norm_mod_fwd("pre_mix_norm", x, sm["g_pre_mix"], sc1, sh1)
    z, xbc, dt_raw, glu_in, gate_l = [
        _matmul(f"proj_{nm}", h1, w, tb=True, out_dtype=F32 if nm == "dt" else BF16)
        for nm, w in zip(("z", "xbc", "dt", "glu", "gate"), win_secs)]
    xbc_c = _dwconv_fwd("ssm_conv_fwd", xbc, sm["w_conv_ssm"], sm["b_conv_ssm"], silu=True, out_dtype=BF16)
    tables = _scan_tables(n_heads)
    dt, q_all, gam_t, etot = _dt_prepare(dt_raw, sm["dt_bias"], sm["a_neg"], tables["route"])
    etot = etot.reshape(s // CHUNK, 8, 2 * n_heads)
    (etx_f, etg_f), (etx_r, etg_r) = [_chunk_decay(etot, n_heads, direction) for direction in (0, 1)]
    y_f, hs_f = _scan_fwd("ssd_fwd_f", xbc_c, q_all, gam_t, etx_f, tables, direction=0, d_ssm=d_ssm)
    y_r, hs_r = _scan_fwd("ssd_fwd_r", xbc_c, q_all, gam_t, etx_r, tables, direction=1, d_ssm=d_ssm)
    y_ssd, yn = _gated_norm_fwd(y_f, y_r, xbc_c, z, sm["d_skip_x"], sm["g_ssm_norm"])
    y_a = _matmul("ssm_out", yn, wts["w_ssm_out"])
    u0 = _glu_fwd(glu_in, sm["b_glu"])
    u1 = _dwconv_fwd("dw_conv_fwd", u0, sm["w_dw"], sm["b_dw"], silu=False)
    u = _ln_silu_fwd(u1, sm["ln_g"], sm["ln_b"])
    y_b = _matmul("conv_out", u, wts["w_conv_out"], bias=sm["b_conv_out"])
    mixin = _gate_merge_fwd(y_a, y_b, gate_l, sm["b_gate"])
    mix = _matmul("mix_out", mixin, wts["w_mix_out"])
    x1 = _gated_residual_fwd("post_mix_residual", x, mix, g1, sm["g_post_mix"])
    h2 = _norm_mod_fwd("pre_ffn_norm", x1, sm["g_pre_ffn"], sc2, sh2)
    gu = _matmul("ffn_gate_up", h2, wts["w_gate_up"], b_blocks=N_CHIPS, out_dtype=BF16)
    act = _swiglu_fwd(gu)
    f = _matmul("ffn_down", act, wts["w_down"])

    dx2, df, sq, d_g2, d_gpf = _final_residual_loss(x1, f, tgt, g2, sm["g_post_ffn"])
    dact = _matmul("d_act", df, wts["w_down"], tb=True, out_dtype=BF16)
    g_w_down = _matmul("g_w_down", act, df, ta=True, out_dtype=BF16)
    dgu = _swiglu_bwd(gu, dact)
    dh2 = _matmul("d_h2", dgu, wts["w_gate_up"], tb=True, b_blocks=N_CHIPS)
    g_w_gate_up = _matmul("g_w_gate_up", h2, dgu, ta=True, out_dtype=BF16, out_blocks=N_CHIPS)
    dx1, d_gpre_ffn, d_sc2, d_sh2 = _norm_mod_bwd("pre_ffn_norm_bwd", x1, dh2, dx2, sm["g_pre_ffn"], sc2)
    dmix, d_g1, d_gpm = _gated_residual_bwd("post_mix_residual_bwd", mix, dx1, g1, sm["g_post_mix"])
    dmixin = _matmul("d_mixin", dmix, wts["w_mix_out"], tb=True, out_dtype=BF16)
    g_w_mix = _matmul("g_w_mix_out", mixin, dmix, ta=True, out_dtype=BF16)
    dy_a, dy_b, dgate_l, d_bgate, d_bco = _gate_merge_bwd(dmixin, y_a, y_b, gate_l, sm["b_gate"])
    du = _matmul("d_u", dy_b, wts["w_conv_out"], tb=True, out_dtype=BF16)
    g_w_co = _matmul("g_w_conv_out", u, dy_b, ta=True, out_dtype=BF16)
    du1, d_lng, d_lnb = _ln_silu_bwd(u1, du, sm["ln_g"], sm["ln_b"])
    du0, d_wdw, d_bdw = _dwconv_bwd("dw_conv_bwd", u0, sm["w_dw"], sm["b_dw"], du1, silu=False)
    dglu, d_bglu = _glu_bwd(glu_in, sm["b_glu"], du0)
    dyn = _matmul("d_yn", dy_a, wts["w_ssm_out"], tb=True, out_dtype=BF16)
    g_w_ssm = _matmul("g_w_ssm_out", yn, dy_a, ta=True, out_dtype=BF16)
    dy_ssd, dz, d_gssm, d_dskip_x = _gated_norm_bwd(y_ssd, z, dyn, xbc_c, sm["d_skip_x"], sm["g_ssm_norm"])
    dxs_f, db_f, dc_f, ddt_f, da_f = _scan_bwd("ssd_bwd_f", xbc_c, dy_ssd, hs_f, q_all, gam_t, etx_f, etg_f, tables,
                                               direction=0, d_ssm=d_ssm)
    dxs_r, db_r, dc_r, ddt_r, da_r = _scan_bwd("ssd_bwd_r", xbc_c, dy_ssd, hs_r, q_all, gam_t, etx_r, etg_r, tables,
                                               direction=1, d_ssm=d_ssm)
    j_heads = n_heads // N_GROUPS
    da = jnp.concatenate([_from_group_lanes(da_f, j_heads), _from_group_lanes(da_r, j_heads)], axis=1)
    ddt = jnp.concatenate([_from_group_lanes(ddt_f, j_heads), _from_group_lanes(ddt_r, j_heads)], axis=1)
    ddt_raw, d_dtbias, d_alog = _dt_backward(da, ddt, dt, dt_raw, sm["dt_bias"], sm["a_neg"])
    dxbc_c = _ssd_grad_merge(dxs_f, dxs_r, dy_ssd, db_f, db_r, dc_f, dc_r, sm["d_skip_x"])
    dxbc, d_wconv, d_bconv = _dwconv_bwd("ssm_conv_bwd", xbc, sm["w_conv_ssm"], sm["b_conv_ssm"], dxbc_c,
                                         silu=True, dx_dtype=BF16)
    dsecs = [dz, dxbc, ddt_raw, dglu, dgate_l]
    dh1 = None
    g_win = []
    for nm, dsec, w in zip(("z", "xbc", "dt", "glu", "gate"), dsecs, win_secs):
        dh1 = _matmul(f"d_h1_{nm}", dsec, w, add=dh1)
        g_win.append(_matmul(f"g_w_in_{nm}", dsec, h1, ta=True, out_dtype=BF16))
    grad_x, d_gpre_mix, d_sc1, d_sh1 = _norm_mod_bwd("pre_mix_norm_bwd", x, dh1, dx1, sm["g_pre_mix"], sc1)

    dmod = jnp.concatenate([d_sh1, d_sc1, d_g1, d_sh2, d_sc2, d_g2], axis=1)
    big = {"w_in_t": jnp.concatenate(g_win, axis=0), "w_ssm_out": g_w_ssm, "w_conv_out": g_w_co,
           "w_mix_out": g_w_mix, "w_gate_up": g_w_gate_up, "w_down": g_w_down}
    small = {"g_pre_mix": d_gpre_mix, "g_post_mix": d_gpm, "w_conv_ssm": d_wconv, "b_conv_ssm": d_bconv,
             "dt_bias": d_dtbias, "a_log": d_alog, "d_skip_x": d_dskip_x, "g_ssm_norm": d_gssm, "b_glu": d_bglu,
             "w_dw": d_wdw, "b_dw": d_bdw, "ln_g": d_lng, "ln_b": d_lnb, "b_conv_out": d_bco, "b_gate": d_bgate,
             "g_pre_ffn": d_gpre_ffn, "g_post_ffn": d_gpf}
    return sq, grad_x, big, small, dmod


ANY = pl.BlockSpec(memory_space=pl.ANY)
WHOLE_VMEM = pl.BlockSpec(memory_space=pltpu.VMEM)


def _mesh_place():
    x, y, c = lax.axis_index("x"), lax.axis_index("y"), lax.axis_index("c")
    other_chips = [(1 - x, y), (x, 1 - y), (1 - x, 1 - y)]
    return x, y, c, other_chips


def _remote(src, dst, send_sems, recv_sems, k, device):
    return pltpu.make_async_remote_copy(src_ref=src, dst_ref=dst, send_sem=send_sems.at[k], recv_sem=recv_sems.at[k],
                                        device_id=device, device_id_type=MESH)


def _gather_devices(name, block):
    m_per, n = block.shape

    def body(x_ref, out_ref, send_sems, recv_sems, local_sem):
        x, y, c, chips = _mesh_place()
        me, sibling = (x, y, c), (x, y, 1 - c)

        def rows(px, py, pc):
            return out_ref.at[pl.ds((4 * px + 2 * py + pc) * m_per, m_per), :]

        def copy(k, blk, to, src=None):
            return _remote(rows(*blk) if src is None else src, rows(*blk), send_sems, recv_sems, k, to)

        mine = pltpu.make_async_copy(x_ref, rows(*me), local_sem)
        mine.start()
        first = [copy(0, me, sibling, src=x_ref)]
        first += [copy(1 + j, me, (*chip, c), src=x_ref) for j, chip in enumerate(chips)]
        for cp in first:
            cp.start()
        passed = [copy(4 + j, (*chip, c), sibling) for j, chip in enumerate(chips)]
        for j, chip in enumerate(chips):
            copy(1 + j, (*chip, c), me).wait_recv()
            passed[j].start()
        copy(0, sibling, me).wait_recv()
        for j, chip in enumerate(chips):
            copy(4 + j, (*chip, 1 - c), me).wait_recv()
        for cp in first + passed:
            cp.wait_send()
        mine.wait()

    return pl.pallas_call(
        body, name=name, out_shape=jax.ShapeDtypeStruct((N_DEV * m_per, n), block.dtype),
        in_specs=[WHOLE_VMEM], out_specs=WHOLE_VMEM,
        scratch_shapes=[pltpu.SemaphoreType.DMA((7,)), pltpu.SemaphoreType.DMA((7,)), pltpu.SemaphoreType.DMA],
        compiler_params=pltpu.CompilerParams(vmem_limit_bytes=VMEM_LIMIT),
    )(block)


def _gather_chips(name, shards):
    n = len(shards)

    def body(*refs):
        ins, outs = refs[:n], refs[n:2 * n]
        send_sems, recv_sems = refs[2 * n:]
        x, y, c, chips = _mesh_place()
        me = 2 * x + y
        sibling = (x, y, 1 - c)

        def half(i, h):
            hr = ins[i].shape[0] // 2
            return pl.ds(h * hr, hr)

        sends = []
        for i in range(n):
            for j, (cx, cy) in enumerate(chips):
                sends.append(_remote(ins[i].at[half(i, c)], outs[i].at[me, half(i, c)], send_sems, recv_sems,
                                     6 * i + j, (cx, cy, c)))
                sends[-1].start()
        for i in range(n):
            sends.append(_remote(ins[i], outs[i].at[me], send_sems, recv_sems, 6 * n + i, sibling))
            sends[-1].start()
        for i in range(n):
            for j, (cx, cy) in enumerate(chips):
                blk = outs[i].at[2 * cx + cy, half(i, c)]
                _remote(blk, blk, send_sems, recv_sems, 6 * i + j, (cx, cy, c)).wait_recv()
                sends.append(_remote(blk, blk, send_sems, recv_sems, 6 * i + 3 + j, sibling))
                sends[-1].start()
        for i in range(n):
            for j, (cx, cy) in enumerate(chips):
                blk = outs[i].at[2 * cx + cy, half(i, 1 - c)]
                _remote(blk, blk, send_sems, recv_sems, 6 * i + 3 + j, sibling).wait_recv()
        for i in range(n):
            _remote(ins[i], outs[i].at[me], send_sems, recv_sems, 6 * n + i, sibling).wait_recv()
        for cp in sends:
            cp.wait_send()

    return pl.pallas_call(
        body, name=name, out_shape=[jax.ShapeDtypeStruct((N_CHIPS,) + s.shape, s.dtype) for s in shards],
        in_specs=[ANY] * n, out_specs=[ANY] * n,
        scratch_shapes=[pltpu.SemaphoreType.DMA((7 * n,)), pltpu.SemaphoreType.DMA((7 * n,))],
    )(*shards)


def _send_sibling_halves(name, grads):
    n = len(grads)

    def body(*refs):
        ins, outs = refs[:n], refs[n:2 * n]
        send_sems, recv_sems = refs[2 * n:]
        x, y, c, _ = _mesh_place()
        sibling = (x, y, 1 - c)
        copies = []
        for i in range(n):
            for j in range(N_CHIPS):
                copies.append(_remote(ins[i].at[j, 1 - c], outs[i].at[j], send_sems, recv_sems, N_CHIPS * i + j, sibling))
                copies[-1].start()
        for cp in copies:
            cp.wait_recv()
        for cp in copies:
            cp.wait_send()

    return pl.pallas_call(
        body, name=name,
        out_shape=[jax.ShapeDtypeStruct((g.shape[0],) + g.shape[2:], g.dtype) for g in grads],
        in_specs=[ANY] * n, out_specs=[ANY] * n,
        scratch_shapes=[pltpu.SemaphoreType.DMA((N_CHIPS * n,)), pltpu.SemaphoreType.DMA((N_CHIPS * n,))],
    )(*grads)


def _send_chip_shards(name, sums):
    n = len(sums)

    def body(*refs):
        ins, outs = refs[:n], refs[n:2 * n]
        send_sems, recv_sems = refs[2 * n:]
        x, y, c, chips = _mesh_place()
        copies = []
        for i in range(n):
            for j, (cx, cy) in enumerate(chips):
                copies.append(_remote(ins[i].at[2 * cx + cy], outs[i].at[j], send_sems, recv_sems, 3 * i + j, (cx, cy, c)))
                copies[-1].start()
        for cp in copies:
            cp.wait_recv()
        for cp in copies:
            cp.wait_send()

    return pl.pallas_call(
        body, name=name,
        out_shape=[jax.ShapeDtypeStruct((3,) + g.shape[1:], g.dtype) for g in sums],
        in_specs=[ANY] * n, out_specs=[ANY] * n,
        scratch_shapes=[pltpu.SemaphoreType.DMA((3 * n,)), pltpu.SemaphoreType.DMA((3 * n,))],
    )(*sums)


def _exchange_halves(name, shards):
    n = len(shards)

    def body(*refs):
        outs = refs[n:2 * n]
        send_sems, recv_sems = refs[2 * n:]
        x, y, c, _ = _mesh_place()
        sibling = (x, y, 1 - c)
        remote = [_remote(outs[i].at[c], outs[i].at[c], send_sems, recv_sems, i, sibling) for i in range(n)]
        for cp in remote:
            cp.start()
        for i in range(n):
            _remote(outs[i].at[1 - c], outs[i].at[1 - c], send_sems, recv_sems, i, sibling).wait_recv()
        for cp in remote:
            cp.wait_send()

    return pl.pallas_call(
        body, name=name,
        out_shape=[jax.ShapeDtypeStruct(h.shape, h.dtype) for h in shards],
        in_specs=[ANY] * n, out_specs=[ANY] * n, input_output_aliases={i: i for i in range(n)},
        scratch_shapes=[pltpu.SemaphoreType.DMA((n,)), pltpu.SemaphoreType.DMA((n,))],
    )(*shards)


def _divisor_tile(rows, row_bytes, quantum=16):
    best = rows
    for t in range(quantum, rows + 1, quantum):
        if rows % t == 0 and 2 * t * row_bytes <= ROW_TILE_BUDGET:
            best = t
    return best


def _add_sibling(name, g4, t1, core):
    nb, _, hr, cols = g4.shape
    t = _divisor_tile(hr, cols * 6)

    def kern(core_ref, g_ref, t_ref, o_ref):
        o_ref[0] = (g_ref[0, 0].astype(F32) + t_ref[0].astype(F32)).astype(o_ref.dtype)

    return pl.pallas_call(
        kern, name=name,
        grid_spec=pltpu.PrefetchScalarGridSpec(
            num_scalar_prefetch=1, grid=(nb, hr // t),
            in_specs=[pl.BlockSpec((1, 1, t, cols), lambda j, i, core_ref: (j, core_ref[0], i, 0)),
                      pl.BlockSpec((1, t, cols), lambda j, i, core_ref: (j, i, 0))],
            out_specs=pl.BlockSpec((1, t, cols), lambda j, i, core_ref: (j, i, 0))),
        out_shape=jax.ShapeDtypeStruct((nb, hr, cols), g4.dtype),
        compiler_params=_params(("parallel", "parallel")),
    )(core, g4, t1)


def _add_chips(name, s1, t3, place):
    _, hr, cols = s1.shape
    t = _divisor_tile(hr, cols * 12)

    def kern(place_ref, s_ref, t_ref, o_ref):
        acc = s_ref[0].astype(F32)
        for j in range(3):
            acc = acc + t_ref[j].astype(F32)
        o_ref[0] = acc

    return pl.pallas_call(
        kern, name=name,
        grid_spec=pltpu.PrefetchScalarGridSpec(
            num_scalar_prefetch=1, grid=(hr // t,),
            in_specs=[pl.BlockSpec((1, t, cols), lambda i, place_ref: (place_ref[0], i, 0)),
                      pl.BlockSpec((3, t, cols), lambda i, place_ref: (0, i, 0))],
            out_specs=pl.BlockSpec((1, t, cols), lambda i, place_ref: (place_ref[1], i, 0))),
        out_shape=jax.ShapeDtypeStruct((2, hr, cols), F32),
        compiler_params=_params(("parallel",)),
    )(place, s1, t3)


def _reduce_scatter(grads, core, chip):
    g4 = [g.reshape(N_CHIPS, 2, g.shape[1] // 2, g.shape[2]) for g in grads]
    t1 = _send_sibling_halves("grads_to_sibling", g4)
    s1 = [_add_sibling(f"chip_sum_{i}", g, t, core) for i, (g, t) in enumerate(zip(g4, t1))]
    t3 = _send_chip_shards("grads_to_chips", s1)
    place = jnp.concatenate([chip, core])
    halves = [_add_chips(f"shard_sum_{i}", s, t, place) for i, (s, t) in enumerate(zip(s1, t3))]
    full = _exchange_halves("grad_halves_to_sibling", halves)
    return [f.reshape(f.shape[1] * 2, f.shape[2]) for f in full]


def _pack_rows(size, width):
    return -(-size // (8 * width)) * 8


def _pack(arrays, width):
    parts = []
    for a in arrays:
        flat = a.reshape(-1).astype(F32)
        rows = _pack_rows(flat.shape[0], width)
        parts.append(jnp.pad(flat, (0, rows * width - flat.shape[0])).reshape(rows, width))
    return jnp.concatenate(parts, axis=0)


def _unpack(block, shapes, width):
    out, r = [], 0
    for shp in shapes:
        size = 1
        for s_ in shp:
            size *= s_
        rows = _pack_rows(size, width)
        out.append(block[r:r + rows].reshape(-1)[:size].reshape(shp))
        r += rows
    return out


SMALL_PARAMS = ("b_ada", "g_pre_mix", "g_post_mix", "b_conv_ssm", "dt_bias_fwd", "dt_bias_bwd", "a_log_fwd", "a_log_bwd",
                "d_skip", "g_ssm_norm", "b_glu", "b_dw", "ln_g", "ln_b", "b_conv_out", "b_gate", "g_pre_ffn", "g_post_ffn")
SHARDED_SMALL = ("w_conv_ssm", "w_dw")
MATRICES = ("w_in", "w_ssm_out", "w_conv_out", "w_mix_out", "w_gate_up", "w_down")
ALL_PARAMS = ("w_ada", "b_ada", "g_pre_mix", "g_post_mix", "w_in", "w_conv_ssm", "b_conv_ssm", "dt_bias_fwd", "dt_bias_bwd",
              "a_log_fwd", "a_log_bwd", "d_skip", "g_ssm_norm", "w_ssm_out", "b_glu", "w_dw", "b_dw", "ln_g", "ln_b",
              "w_conv_out", "b_conv_out", "b_gate", "w_mix_out", "g_pre_ffn", "g_post_ffn", "w_gate_up", "w_down")
COND_ROWS = 48
COND_CONV_ROW = 8
COND_DW_ROW = 16
MOD_ROWS = 16


def kernel(x, c, w_ada, b_ada, g_pre_mix, g_post_mix, w_in, w_conv_ssm, b_conv_ssm, dt_bias_fwd, dt_bias_bwd, a_log_fwd, a_log_bwd, d_skip, g_ssm_norm, w_ssm_out, b_glu, w_dw, b_dw, ln_g, ln_b, w_conv_out, b_conv_out, b_gate, w_mix_out, g_pre_ffn, g_post_ffn, w_gate_up, w_down, loss_target, m_w_ada, m_b_ada, m_g_pre_mix, m_g_post_mix, m_w_in, m_w_conv_ssm, m_b_conv_ssm, m_dt_bias_fwd, m_dt_bias_bwd, m_a_log_fwd, m_a_log_bwd, m_d_skip, m_g_ssm_norm, m_w_ssm_out, m_b_glu, m_w_dw, m_b_dw, m_ln_g, m_ln_b, m_w_conv_out, m_b_conv_out, m_b_gate, m_w_mix_out, m_g_pre_ffn, m_g_post_ffn, m_w_gate_up, m_w_down, v_w_ada, v_b_ada, v_g_pre_mix, v_g_post_mix, v_w_in, v_w_conv_ssm, v_b_conv_ssm, v_dt_bias_fwd, v_dt_bias_bwd, v_a_log_fwd, v_a_log_bwd, v_d_skip, v_g_ssm_norm, v_w_ssm_out, v_b_glu, v_w_dw, v_b_dw, v_ln_g, v_ln_b, v_w_conv_out, v_b_conv_out, v_b_gate, v_w_mix_out, v_g_pre_ffn, v_g_post_ffn, v_w_gate_up, v_w_down):
    given = dict(locals())
    wgt = {n: given[n][0] for n in ALL_PARAMS}
    mom = {n: given["m_" + n][0] for n in ALL_PARAMS}
    var = {n: given["v_" + n][0] for n in ALL_PARAMS}
    xs, tgt = x[0], loss_target[0]
    s, d = xs.shape
    d_ssm = 2 * d
    n_heads = d_ssm // HEAD_DIM
    d_xbc = d_ssm + 2 * N_GROUPS * D_STATE
    xi, yi, ci = lax.axis_index("x"), lax.axis_index("y"), lax.axis_index("c")
    chip = 2 * xi + yi
    dev = 2 * chip + ci
    core_arr = jnp.reshape(ci, (1,)).astype(jnp.int32)
    chip_arr = jnp.reshape(chip, (1,)).astype(jnp.int32)
    k_conv, k_dw = wgt["w_conv_ssm"].shape[0], wgt["w_dw"].shape[0]
    xbc_shard, dw_shard = d_xbc // N_CHIPS, d // N_CHIPS

    width1 = max(d, xbc_shard)
    blk = jnp.zeros((COND_ROWS, width1), F32)
    blk = blk.at[0, :d].set(c[0])
    blk = blk.at[COND_CONV_ROW:COND_CONV_ROW + k_conv, :xbc_shard].set(wgt["w_conv_ssm"])
    blk = blk.at[COND_DW_ROW:COND_DW_ROW + k_dw, :dw_shard].set(wgt["w_dw"])
    g1 = _gather_devices("gather_cond", blk).reshape(N_DEV, COND_ROWS, width1)
    c_all = g1[:, 0, :d]
    w_conv_full = jnp.concatenate([g1[2 * k, COND_CONV_ROW:COND_CONV_ROW + k_conv, :xbc_shard] for k in range(N_CHIPS)], axis=1)
    w_dw_full = jnp.concatenate([g1[2 * k, COND_DW_ROW:COND_DW_ROW + k_dw, :dw_shard] for k in range(N_CHIPS)], axis=1)
    c_act = jnp.pad(c_all * _sigmoid(c_all), ((0, MOD_ROWS - N_DEV), (0, 0)))

    mod_part = _matmul("ada_mod", c_act, wgt["w_ada"])
    g2 = _gather_devices("gather_mod", mod_part).reshape(N_DEV, MOD_ROWS, mod_part.shape[1])
    mod_all = jnp.concatenate([g2[2 * k, :N_DEV] for k in range(N_CHIPS)], axis=1) + wgt["b_ada"][None]
    mod = lax.dynamic_slice_in_dim(mod_all, dev, 1, axis=0)

    shards = [wgt["w_in"].T.astype(BF16)] + [wgt[n].astype(BF16) for n in MATRICES[1:]]
    gathered = _gather_chips("gather_weights", shards)
    wts = {"w_in_t": gathered[0].reshape(-1, d), "w_ssm_out": gathered[1].reshape(-1, d),
           "w_conv_out": gathered[2].reshape(-1, d), "w_mix_out": gathered[3].reshape(-1, d),
           "w_gate_up": gathered[4], "w_down": gathered[5].reshape(-1, d)}
    row = lambda v: v.reshape(1, -1)
    sm = {"g_pre_mix": row(wgt["g_pre_mix"]), "g_post_mix": row(wgt["g_post_mix"]), "w_conv_ssm": w_conv_full,
          "b_conv_ssm": row(wgt["b_conv_ssm"]),
          "dt_bias": row(jnp.concatenate([wgt["dt_bias_fwd"], wgt["dt_bias_bwd"]])),
          "a_neg": row(-jnp.exp(jnp.concatenate([wgt["a_log_fwd"], wgt["a_log_bwd"]]))),
          "d_skip_x": row(jnp.repeat(wgt["d_skip"], HEAD_DIM)), "g_ssm_norm": row(wgt["g_ssm_norm"]),
          "b_glu": row(wgt["b_glu"]), "w_dw": w_dw_full, "b_dw": row(wgt["b_dw"]), "ln_g": row(wgt["ln_g"]),
          "ln_b": row(wgt["ln_b"]), "b_conv_out": row(wgt["b_conv_out"]), "b_gate": row(wgt["b_gate"]),
          "g_pre_ffn": row(wgt["g_pre_ffn"]), "g_post_ffn": row(wgt["g_post_ffn"])}

    sq, grad_x, big, small, dmod = _local_step(xs, tgt, mod, wts, sm)
    loss = lax.psum((0.5 / d) * sq[0, 0], ("x", "y", "c"))

    local_small = {"b_ada": dmod, "g_pre_mix": small["g_pre_mix"], "g_post_mix": small["g_post_mix"],
                   "b_conv_ssm": small["b_conv_ssm"], "dt_bias_fwd": small["dt_bias"][:, :n_heads],
                   "dt_bias_bwd": small["dt_bias"][:, n_heads:], "a_log_fwd": small["a_log"][:, :n_heads],
                   "a_log_bwd": small["a_log"][:, n_heads:],
                   "d_skip": jnp.sum(small["d_skip_x"].reshape(n_heads, HEAD_DIM), axis=1),
                   "g_ssm_norm": small["g_ssm_norm"], "b_glu": small["b_glu"], "b_dw": small["b_dw"],
                   "ln_g": small["ln_g"], "ln_b": small["ln_b"], "b_conv_out": small["b_conv_out"],
                   "b_gate": small["b_gate"], "g_pre_ffn": small["g_pre_ffn"], "g_post_ffn": small["g_post_ffn"],
                   "w_conv_ssm": small["w_conv_ssm"], "w_dw": small["w_dw"]}
    names = SMALL_PARAMS + SHARDED_SMALL
    pack = _pack([local_small[n] for n in names], d)
    rows_p = pack.shape[0]
    g3 = _gather_devices("gather_small_grads", pack).reshape(N_DEV, rows_p, d)
    total = _rowwise("sum_small_grads", lambda *blocks: (functools.reduce(lambda a, b: a + b, blocks),),
                     [g3[i] for i in range(N_DEV)], [], [(d, F32)])[0]
    full_shapes = [wgt[n].shape for n in SMALL_PARAMS] + [(k_conv, d_xbc), (k_dw, d)]
    summed = dict(zip(names, _unpack(total, full_shapes, d)))
    grads = {n: summed[n] for n in SMALL_PARAMS}
    grads["w_conv_ssm"] = lax.dynamic_slice_in_dim(summed["w_conv_ssm"], chip * xbc_shard, xbc_shard, axis=1)
    grads["w_dw"] = lax.dynamic_slice_in_dim(summed["w_dw"], chip * dw_shard, dw_shard, axis=1)

    dmod_all = g3[:, :N_MOD, :].reshape(N_DEV, N_MOD * d)
    ada_cols = wgt["w_ada"].shape[1]
    dmod_cols = jnp.pad(lax.dynamic_slice_in_dim(dmod_all, chip * ada_cols, ada_cols, axis=1),
                        ((0, MOD_ROWS - N_DEV), (0, 0)))
    grads["w_ada"] = _matmul("g_w_ada", c_act, dmod_cols, ta=True)

    big_list = [big["w_in_t"].reshape(N_CHIPS, -1, d)] + [
        big[n] if n == "w_gate_up" else big[n].reshape(N_CHIPS, -1, d) for n in MATRICES[1:]]
    reduced = _reduce_scatter(big_list, core_arr, chip_arr)
    grads["w_in"] = reduced[0].T
    for n, g in zip(MATRICES[1:], reduced[1:]):
        grads[n] = g

    delta, new_m, new_v = {}, {}, {}
    for n in ("w_ada",) + MATRICES:
        delta[n], new_m[n], new_v[n] = _adamw("adamw_" + n, wgt[n], grads[n], mom[n], var[n])
    for group, width, tag in ((SMALL_PARAMS, d, "small"), (SHARDED_SMALL, LANES, "conv")):
        shapes = [wgt[n].shape for n in group]
        packs = [_pack([src[n] for n in group], width) for src in (wgt, grads, mom, var)]
        outs = _adamw("adamw_" + tag, *packs)
        for res, o in zip((delta, new_m, new_v), outs):
            res.update(zip(group, _unpack(o, shapes, width)))

    lead = lambda a: a[None]
    return (loss, grad_x[None], *[lead(grads[n]) for n in ALL_PARAMS], *[lead(delta[n]) for n in ALL_PARAMS],
            *[lead(new_m[n]) for n in ALL_PARAMS], *[lead(new_v[n]) for n in ALL_PARAMS])
```

```python
import functools

import jax
import jax.numpy as jnp
from jax import lax
from jax.experimental import pallas as pl
from jax.experimental.pallas import tpu as pltpu

F32 = jnp.float32
BF16 = jnp.bfloat16

N_GROUPS = 8
HEAD_DIM = 64
D_STATE = 128
CHUNK = 128
EPS = 1e-6
N_MOD = 6
ADAM_LR = 0.001
ADAM_B1 = 0.9
ADAM_B2 = 0.999
ADAM_EPS = 1e-08
ADAM_WD = 0.01
ADAM_STEP = 10

V7X_VMEM_BYTES = 64 * 1024 * 1024
VMEM_LIMIT = V7X_VMEM_BYTES - 8 * 1024 * 1024
ROW_TILE_BUDGET = 20 * 1024 * 1024
LANES = 128
NEG = -1e30
MESH = pl.DeviceIdType.MESH
N_CHIPS = 4
N_DEV = 8


def _params(sem):
    return pltpu.CompilerParams(dimension_semantics=sem, vmem_limit_bytes=VMEM_LIMIT)


def _sigmoid(x):
    return 1.0 / (1.0 + jnp.exp(-x))


def _silu(x):
    return x * _sigmoid(x)


def _dsilu(x):
    s = _sigmoid(x)
    return s * (1.0 + x * (1.0 - s))


def _softplus(x):
    return jnp.maximum(x, 0.0) + jnp.log(1.0 + jnp.exp(-jnp.abs(x)))


def _sum0(a):
    return jnp.sum(a, axis=0, keepdims=True)


def _mean1(a):
    return jnp.mean(a, axis=1, keepdims=True)


def _rowwise(name, body, rows, bcasts, out_rows, out_accs=(), tile=None):
    rows = [r if isinstance(r, tuple) else (r, r.shape[1], 0) for r in rows]
    s = rows[0][0].shape[0]
    if tile is None:
        per_row = sum(w * a.dtype.itemsize for a, w, _ in rows) + sum(w * jnp.dtype(dt).itemsize for w, dt in out_rows)
        tile = 1024
        while tile > 16 and (tile * per_row * 2 > ROW_TILE_BUDGET or s % tile):
            tile //= 2
        if s * per_row * 2 <= ROW_TILE_BUDGET:
            tile = s
    assert s % tile == 0
    n_in = len(rows) + len(bcasts)
    n_o = len(out_rows)

    def kern(*refs):
        res = body(*[r[...].astype(F32) for r in refs[:n_in]])
        outs = refs[n_in:]
        for o, v in zip(outs[:n_o], res[:n_o]):
            o[...] = v.astype(o.dtype)
        if out_accs:
            @pl.when(pl.program_id(0) == 0)
            def _():
                for o in outs[n_o:]:
                    o[...] = jnp.zeros_like(o)
            for o, v in zip(outs[n_o:], res[n_o:]):
                o[...] += v

    in_specs = [pl.BlockSpec((tile, w), functools.partial(lambda i, cb: (i, cb), cb=cb)) for _, w, cb in rows]
    in_specs += [pl.BlockSpec(b.shape, functools.partial(lambda i, nd: (0,) * nd, nd=b.ndim)) for b in bcasts]
    out_shape = [jax.ShapeDtypeStruct((s, w), dt) for w, dt in out_rows]
    out_shape += [jax.ShapeDtypeStruct((1, w), F32) for w in out_accs]
    out_specs = [pl.BlockSpec((tile, w), lambda i: (i, 0)) for w, _ in out_rows]
    out_specs += [pl.BlockSpec((1, w), lambda i: (0, 0)) for w in out_accs]
    return pl.pallas_call(
        kern, name=name, grid=(s // tile,), in_specs=in_specs, out_specs=out_specs, out_shape=out_shape,
        compiler_params=_params(("arbitrary",) if out_accs else ("parallel",)),
    )(*[a for a, _, _ in rows], *bcasts)


def _tile(n, pref):
    if n <= pref:
        return n
    t = (pref // LANES) * LANES
    while t >= LANES:
        if n % t == 0:
            return t
        t -= LANES
    return n


def _matmul(name, a, b, *, ta=False, tb=False, out_dtype=F32, bias=None, add=None, b_blocks=1, out_blocks=1,
            tm=1024, tn=1408, tk=2048):
    m, k = (a.shape[1], a.shape[0]) if ta else a.shape
    if b_blocks > 1:
        rows_b, cols_b = b.shape[1], b.shape[2] * b_blocks
    else:
        rows_b, cols_b = b.shape
    n, kb = (rows_b, cols_b) if tb else (cols_b, rows_b)
    assert k == kb, (name, a.shape, b.shape)
    tm, tn, tk = _tile(m, tm), _tile(n, tn), _tile(k, tk)
    if b_blocks > 1:
        per = cols_b // b_blocks
        if tb:
            tk = _tile(per, tk)
        else:
            tn = _tile(per, tn)
    if out_blocks > 1:
        tn = _tile(n // out_blocks, tn)
    nk = k // tk
    grid = (m // tm, n // tn, nk)

    a_spec = pl.BlockSpec((tk, tm), lambda i, j, kk: (kk, i)) if ta else pl.BlockSpec((tm, tk), lambda i, j, kk: (i, kk))
    if b_blocks > 1:
        if tb:
            nb = per // tk
            b_spec = pl.BlockSpec((1, tn, tk), lambda i, j, kk: (kk // nb, j, kk % nb))
        else:
            nb = per // tn
            b_spec = pl.BlockSpec((1, tk, tn), lambda i, j, kk: (j // nb, kk, j % nb))
    else:
        b_spec = pl.BlockSpec((tn, tk), lambda i, j, kk: (j, kk)) if tb else pl.BlockSpec((tk, tn), lambda i, j, kk: (kk, j))
    in_specs = [a_spec, b_spec]
    operands = [a, b]
    if bias is not None:
        in_specs.append(pl.BlockSpec((1, tn), lambda i, j, kk: (0, j)))
        operands.append(bias)
    if add is not None:
        in_specs.append(pl.BlockSpec((tm, tn), lambda i, j, kk: (i, j)))
        operands.append(add)
    if out_blocks > 1:
        nbo = (n // out_blocks) // tn
        out_spec = pl.BlockSpec((1, tm, tn), lambda i, j, kk: (j // nbo, i, j % nbo))
        out_shape = jax.ShapeDtypeStruct((out_blocks, m, n // out_blocks), out_dtype)
    else:
        out_spec = pl.BlockSpec((tm, tn), lambda i, j, kk: (i, j))
        out_shape = jax.ShapeDtypeStruct((m, n), out_dtype)
    dims = (((0 if ta else 1,), (1 if tb else 0,)), ((), ()))
    has_bias, has_add = bias is not None, add is not None

    def kern(*refs):
        a_ref, b_ref = refs[0], refs[1]
        pos = 2
        bias_ref = add_ref = None
        if has_bias:
            bias_ref = refs[pos]
            pos += 1
        if has_add:
            add_ref = refs[pos]
            pos += 1
        o_ref = refs[pos]
        acc_ref = refs[pos + 1] if nk > 1 else None
        av = a_ref[...].astype(BF16)
        bv = (b_ref[0] if b_blocks > 1 else b_ref[...]).astype(BF16)
        p = lax.dot_general(av, bv, dims, preferred_element_type=F32)

        def finish(acc):
            if has_bias:
                acc = acc + bias_ref[...]
            if has_add:
                acc = acc + add_ref[...]
            if out_blocks > 1:
                o_ref[0] = acc.astype(o_ref.dtype)
            else:
                o_ref[...] = acc.astype(o_ref.dtype)

        if nk == 1:
            finish(p)
        else:
            kk = pl.program_id(2)

            @pl.when(kk == 0)
            def _():
                acc_ref[...] = p

            @pl.when(kk > 0)
            def _():
                acc_ref[...] += p

            @pl.when(kk == nk - 1)
            def _():
                finish(acc_ref[...])

    return pl.pallas_call(
        kern, name=name, grid=grid, in_specs=in_specs, out_specs=out_spec, out_shape=out_shape,
        scratch_shapes=[pltpu.VMEM((tm, tn), F32)] if nk > 1 else [],
        compiler_params=_params(("parallel", "parallel", "arbitrary")),
    )(*operands)


CONV_HALO = 16
CONV_ROWS = 128
SUBLANES = 8


def _taps(win, shifts, rows):
    n = win.shape[0]
    groups = {}
    for j, s in enumerate(shifts):
        groups.setdefault(s % SUBLANES, []).append((j, s - s % SUBLANES))
    for rem, taps in sorted(groups.items()):
        rolled = pltpu.roll(win, n - rem, axis=0) if rem else win
        for j, off in taps:
            yield j, rolled[off:off + rows]


def _dwconv_fwd(name, x, w, b, *, silu, out_dtype=F32):
    s, c = x.shape
    k = w.shape[0]
    pad = (k - 1) // 2
    assert pad <= CONV_HALO and c % LANES == 0
    t = min(CONV_ROWS, s)
    n_chunks = s // t
    fwd_shifts = [CONV_HALO - pad + j for j in range(k)]

    def kern(x_ref, w_ref, b_ref, o_ref, xp_ref):
        zeros = jnp.zeros((CONV_HALO, LANES), F32)
        xp_ref[0:CONV_HALO, :] = zeros
        xp_ref[CONV_HALO + s:CONV_HALO + s + CONV_HALO, :] = zeros
        xp_ref[CONV_HALO:CONV_HALO + s, :] = x_ref[...].astype(F32)
        bv = b_ref[...]

        def chunk(i, carry):
            base = pl.multiple_of(i * t, 16)
            win = xp_ref[pl.ds(base, t + 2 * CONV_HALO), :]
            acc = jnp.zeros((t, LANES), F32)
            for j, xs in _taps(win, fwd_shifts, t):
                acc = acc + xs * w_ref[pl.ds(j, 1), :]
            acc = acc + bv
            o_ref[pl.ds(base, t), :] = (_silu(acc) if silu else acc).astype(o_ref.dtype)
            return carry

        lax.fori_loop(0, n_chunks, chunk, 0)

    return pl.pallas_call(
        kern, name=name, grid=(c // LANES,),
        in_specs=[pl.BlockSpec((s, LANES), lambda i: (0, i)), pl.BlockSpec((k, LANES), lambda i: (0, i)),
                  pl.BlockSpec((1, LANES), lambda i: (0, i))],
        out_specs=pl.BlockSpec((s, LANES), lambda i: (0, i)),
        out_shape=jax.ShapeDtypeStruct((s, c), out_dtype),
        scratch_shapes=[pltpu.VMEM((s + 2 * CONV_HALO, LANES), F32)],
        compiler_params=_params(("parallel",)),
    )(x, w, b)


def _dwconv_bwd(name, x, w, b, dout, *, silu, dx_dtype=F32):
    s, c = x.shape
    k = w.shape[0]
    pad = (k - 1) // 2
    t = min(CONV_ROWS, s)
    n_chunks = s // t
    fwd_shifts = [CONV_HALO - pad + j for j in range(k)]
    bwd_shifts = [CONV_HALO + pad - j for j in range(k)]

    def kern(x_ref, w_ref, b_ref, do_ref, dx_ref, dw_ref, db_ref, xp_ref, dp_ref):
        zeros = jnp.zeros((CONV_HALO, LANES), F32)
        for ref in (xp_ref, dp_ref):
            ref[0:CONV_HALO, :] = zeros
            ref[CONV_HALO + s:CONV_HALO + s + CONV_HALO, :] = zeros
        xp_ref[CONV_HALO:CONV_HALO + s, :] = x_ref[...].astype(F32)
        bv = b_ref[...]
        dw_ref[...] = jnp.zeros_like(dw_ref)

        def pre_chunk(i, dbias):
            base = pl.multiple_of(i * t, 16)
            win = xp_ref[pl.ds(base, t + 2 * CONV_HALO), :]
            dpre = do_ref[pl.ds(base, t), :].astype(F32)
            if silu:
                acc = jnp.zeros((t, LANES), F32)
                for j, xs in _taps(win, fwd_shifts, t):
                    acc = acc + xs * w_ref[pl.ds(j, 1), :]
                dpre = dpre * _dsilu(acc + bv)
            dp_ref[pl.ds(base + CONV_HALO, t), :] = dpre
            for j, xs in _taps(win, fwd_shifts, t):
                dw_ref[pl.ds(j, 1), :] += _sum0(dpre * xs)
            return dbias + _sum0(dpre)

        db_ref[...] = lax.fori_loop(0, n_chunks, pre_chunk, jnp.zeros((1, LANES), F32))

        def dx_chunk(i, carry):
            base = pl.multiple_of(i * t, 16)
            win = dp_ref[pl.ds(base, t + 2 * CONV_HALO), :]
            acc = jnp.zeros((t, LANES), F32)
            for j, dps in _taps(win, bwd_shifts, t):
                acc = acc + dps * w_ref[pl.ds(j, 1), :]
            dx_ref[pl.ds(base, t), :] = acc.astype(dx_ref.dtype)
            return carry

        lax.fori_loop(0, n_chunks, dx_chunk, 0)

    col = lambda rows: pl.BlockSpec((rows, LANES), lambda i: (0, i))
    return pl.pallas_call(
        kern, name=name, grid=(c // LANES,),
        in_specs=[col(s), col(k), col(1), col(s)],
        out_specs=[col(s), col(k), col(1)],
        out_shape=[jax.ShapeDtypeStruct((s, c), dx_dtype), jax.ShapeDtypeStruct((k, c), F32),
                   jax.ShapeDtypeStruct((1, c), F32)],
        scratch_shapes=[pltpu.VMEM((s + 2 * CONV_HALO, LANES), F32), pltpu.VMEM((s + 2 * CONV_HALO, LANES), F32)],
        compiler_params=_params(("parallel",)),
    )(x, w, b, dout)


_NT =(((1,), (1,)), ((), ()))
_TN = (((0,), (0,)), ((), ()))


def _dot(a, b, dims=None):
    if dims is None:
        return jnp.dot(a, b, preferred_element_type=F32)
    return lax.dot_general(a, b, dims, preferred_element_type=F32)


HEAD_QUANTITIES = 4
GROUPS_PER_STEP = 2


def _scan_tables(n_heads):
    j_heads = n_heads // N_GROUPS
    used = 3 * HEAD_QUANTITIES * j_heads
    assert used <= LANES and j_heads % 2 == 0 and N_GROUPS % GROUPS_PER_STEP == 0
    gw = j_heads * HEAD_DIM
    r = jnp.arange(LANES)[:, None]

    def expand(quantity, width):
        head_of_lane = jnp.arange(j_heads * width)[None] // width
        return ((r // (3 * j_heads) == quantity) & (r % j_heads == head_of_lane) & (r < used)).astype(BF16)

    sel_cols = (jnp.arange(gw)[:, None] // HEAD_DIM == jnp.arange(LANES)[None]).astype(BF16)
    h2 = 2 * n_heads
    rows = jnp.arange(3 * HEAD_QUANTITIES * h2)
    head = rows % n_heads
    col = ((rows % h2) // n_heads * N_GROUPS + head // j_heads) * LANES + (rows // h2) * j_heads + head % j_heads
    route = (col[:, None] == jnp.arange(2 * N_GROUPS * LANES)[None]).astype(BF16)
    return {"ex_dt": expand(0, HEAD_DIM), "ex_gam": expand(1, CHUNK), "ex_din": expand(2, HEAD_DIM),
            "ex_dst": expand(3, HEAD_DIM), "sel_cols": sel_cols, "route": route}


def _chunk_decay(etot, n_heads, direction):
    j_heads = n_heads // N_GROUPS
    ed = etot[:, :, direction * n_heads:(direction + 1) * n_heads]
    per_group = jnp.pad(ed.reshape(ed.shape[0], 8, N_GROUPS, j_heads), ((0, 0), (0, 0), (0, 0), (0, LANES - j_heads)))
    return jnp.repeat(ed, HEAD_DIM, axis=2), per_group.reshape(ed.shape[0], 8, N_GROUPS * LANES)


def _scan_specs(reverse_order, direction, nc, j_heads, d_ssm):
    gps = GROUPS_PER_STEP
    gw = j_heads * HEAD_DIM
    b_off = d_ssm // (gps * D_STATE)
    c_off = b_off + N_GROUPS // gps
    d_off = direction * (N_GROUPS // gps)
    zz = (lambda z: nc - 1 - z) if reverse_order else (lambda z: z)
    const = lambda shape: pl.BlockSpec(shape, lambda g, z: (0,) * len(shape))
    return {
        "xs": pl.BlockSpec((CHUNK, gps * gw), lambda g, z: (zz(z), g)),
        "b": pl.BlockSpec((CHUNK, gps * D_STATE), lambda g, z: (zz(z), b_off + g)),
        "c": pl.BlockSpec((CHUNK, gps * D_STATE), lambda g, z: (zz(z), c_off + g)),
        "q": pl.BlockSpec((CHUNK, gps * LANES), lambda g, z: (zz(z), d_off + g)),
        "gam_t": pl.BlockSpec((gps * j_heads, CHUNK), lambda g, z: (d_off + g, zz(z))),
        "etot_x": pl.BlockSpec((1, 8, gps * gw), lambda g, z: (zz(z), 0, g)),
        "etot_g": pl.BlockSpec((1, 8, gps * LANES), lambda g, z: (zz(z), 0, g)),
        "state": pl.BlockSpec((gps, 1, D_STATE, gw), lambda g, z: (g, zz(z), 0, 0)),
        "grp": pl.BlockSpec((CHUNK, gps * D_STATE), lambda g, z: (zz(z), g)),
        "ex": const((LANES, gw)), "ex_gam": const((LANES, j_heads * CHUNK)), "sel": const((gw, LANES)),
    }


def _scan_masks(reverse):
    li = lax.broadcasted_iota(jnp.int32, (CHUNK, CHUNK), 0)
    si = lax.broadcasted_iota(jnp.int32, (CHUNK, CHUNK), 1)
    mask = (li <= si) if reverse else (li >= si)
    mask_t = (si <= li) if reverse else (si >= li)
    return li, si, mask, mask_t, si < HEAD_DIM


def _plan_phase(plan, phase, n_steps, ins, outs, sems):
    if plan is None or phase not in plan.phases:
        return
    g, z = pl.program_id(0), pl.program_id(1)
    when = {"start": (g == 0) & (z == 0), "middle": (g == n_steps[0] // 2) & (z == 0),
            "end": (g == n_steps[0] - 1) & (z == n_steps[1] - 1)}[phase]

    @pl.when(when)
    def _():
        plan.phases[phase](ins, outs, *sems)


def _plan_call_parts(plan):
    if plan is None:
        return [], [], [], [], [], ("parallel", "arbitrary")
    n_in, n_out = len(plan.operands), len(plan.out_shapes)
    return plan.operands, [ANY] * n_in, plan.out_shapes, [ANY] * n_out, plan.sems(), ("arbitrary", "arbitrary")


def _scan_fwd(name, xbc_c, q_all, gam_t, etot_x, tb, *, direction, d_ssm, plan=None):
    s = xbc_c.shape[0]
    nc = s // CHUNK
    j_heads = tb["ex_dt"].shape[1] // HEAD_DIM
    gw = j_heads * HEAD_DIM
    gps = GROUPS_PER_STEP
    reverse = direction == 1
    sp = _scan_specs(reverse, direction, nc, j_heads, d_ssm)
    p_ops, p_in_specs, p_out_shapes, p_out_specs, p_scratch, semantics = _plan_call_parts(plan)
    n_pi, n_po = len(p_ops), len(p_out_shapes)
    n_steps = (N_GROUPS // gps, nc)

    def kern(*refs):
        xs_ref, b_ref, c_ref, q_ref, gamt_ref, etx_ref, exdt_ref, exgam_ref, exdin_ref, exdst_ref = refs[:10]
        plan_ins = refs[10:10 + n_pi]
        y_ref, hs_ref = refs[10 + n_pi:12 + n_pi]
        plan_outs = refs[12 + n_pi:12 + n_pi + n_po]
        h_ref = refs[12 + n_pi + n_po]
        plan_sems = refs[13 + n_pi + n_po:]
        _plan_phase(plan, "start", n_steps, plan_ins, plan_outs, plan_sems)
        _plan_phase(plan, "middle", n_steps, plan_ins, plan_outs, plan_sems)

        @pl.when(pl.program_id(1) == 0)
        def _():
            h_ref[...] = jnp.zeros_like(h_ref)

        _, _, mask, _, lo = _scan_masks(reverse)
        for gi in range(gps):
            bb = b_ref[:, gi * D_STATE:(gi + 1) * D_STATE].astype(BF16)
            cb = c_ref[:, gi * D_STATE:(gi + 1) * D_STATE].astype(BF16)
            cbt = _dot(cb, bb, _NT)
            q = q_ref[:, gi * LANES:(gi + 1) * LANES]
            dtx, dinx, dstx = _dot(q, exdt_ref[...]), _dot(q, exdin_ref[...]), _dot(q, exdst_ref[...])
            gcol = _dot(q, exgam_ref[...])
            xdt = xs_ref[:, gi * gw:(gi + 1) * gw].astype(F32) * dtx
            ht = h_ref[gi]
            y_off = _dot(cb, ht.astype(BF16)) * dinx
            hs_ref[gi, 0] = ht
            for p in range(j_heads // 2):
                lanes = slice(p * CHUNK, (p + 1) * CHUNK)
                x2 = xdt[:, lanes]
                acc = y_off[:, lanes]
                for idx, j in enumerate((2 * p, 2 * p + 1)):
                    g_row = gamt_ref[pl.ds(gi * j_heads + j, 1), :]
                    decay = jnp.exp(jnp.where(mask, gcol[:, j * CHUNK:(j + 1) * CHUNK] - g_row, NEG))
                    x_head = jnp.where(lo if idx == 0 else jnp.logical_not(lo), x2, 0.0).astype(BF16)
                    acc = acc + _dot((cbt * decay).astype(BF16), x_head)
                y_ref[:, gi * gw + p * CHUNK:gi * gw + (p + 1) * CHUNK] = acc.astype(y_ref.dtype)
            h_ref[gi] = ht * etx_ref[0, 0:1, gi * gw:(gi + 1) * gw] + _dot(bb, (xdt * dstx).astype(BF16), _TN)
        _plan_phase(plan, "end", n_steps, plan_ins, plan_outs, plan_sems)

    res = pl.pallas_call(
        kern, name=name, grid=n_steps,
        in_specs=[sp["xs"], sp["b"], sp["c"], sp["q"], sp["gam_t"], sp["etot_x"], sp["ex"], sp["ex_gam"], sp["ex"],
                  sp["ex"]] + p_in_specs,
        out_specs=[sp["xs"], sp["state"]] + p_out_specs,
        out_shape=[jax.ShapeDtypeStruct((s, d_ssm), BF16),
                   jax.ShapeDtypeStruct((N_GROUPS, nc, D_STATE, gw), F32)] + p_out_shapes,
        scratch_shapes=[pltpu.VMEM((gps, D_STATE, gw), F32)] + p_scratch,
        compiler_params=_params(semantics),
    )(xbc_c, xbc_c, xbc_c, q_all, gam_t, etot_x, tb["ex_dt"], tb["ex_gam"], tb["ex_din"], tb["ex_dst"], *p_ops)
    return res[0], res[1], list(res[2:])


def _scan_bwd(name, xbc_c, dy, hs, q_all, gam_t, etot_x, etot_g, tb, *, direction, d_ssm, plan=None):
    s = xbc_c.shape[0]
    nc = s // CHUNK
    j_heads = tb["ex_dt"].shape[1] // HEAD_DIM
    gw = j_heads * HEAD_DIM
    gps = GROUPS_PER_STEP
    reverse = direction == 1
    sp = _scan_specs(not reverse, direction, nc, j_heads, d_ssm)
    hp = lax.Precision.HIGHEST
    p_ops, p_in_specs, p_out_shapes, p_out_specs, p_scratch, semantics = _plan_call_parts(plan)
    n_pi, n_po = len(p_ops), len(p_out_shapes)
    n_steps = (N_GROUPS // gps, nc)

    def kern(*refs):
        (xs_ref, b_ref, c_ref, dy_ref, hs_ref, q_ref, gamt_ref, etg_ref, etx_ref, exdt_ref, exgam_ref, exdin_ref,
         exdst_ref, sel_ref) = refs[:14]
        plan_ins = refs[14:14 + n_pi]
        dxs_ref, db_ref, dc_ref, ddt_ref, da_ref = refs[14 + n_pi:19 + n_pi]
        plan_outs = refs[19 + n_pi:19 + n_pi + n_po]
        dh_ref, tmp_ref = refs[19 + n_pi + n_po:21 + n_pi + n_po]
        plan_sems = refs[21 + n_pi + n_po:]
        _plan_phase(plan, "start", n_steps, plan_ins, plan_outs, plan_sems)
        _plan_phase(plan, "middle", n_steps, plan_ins, plan_outs, plan_sems)

        @pl.when(pl.program_id(1) == 0)
        def _():
            dh_ref[...] = jnp.zeros_like(dh_ref)

        li, si, mask, mask_t, lo = _scan_masks(reverse)
        sel = sel_ref[...]
        incl = ((si <= li) if reverse else (si >= li)).astype(F32)
        excl = ((si > li) if reverse else (si < li)).astype(F32)
        for gi in range(gps):
            grp_lanes = slice(gi * D_STATE, (gi + 1) * D_STATE)
            bb = b_ref[:, grp_lanes].astype(BF16)
            cb = c_ref[:, grp_lanes].astype(BF16)
            cbt = _dot(cb, bb, _NT)
            cbt_t = _dot(bb, cb, _NT)
            q = q_ref[:, gi * LANES:(gi + 1) * LANES]
            dtx, dinx, dstx = _dot(q, exdt_ref[...]), _dot(q, exdin_ref[...]), _dot(q, exdst_ref[...])
            gcol = _dot(q, exgam_ref[...])
            x_all = xs_ref[:, gi * gw:(gi + 1) * gw].astype(F32)
            dy_all = dy_ref[:, gi * gw:(gi + 1) * gw].astype(F32)
            xdt = x_all * dtx
            xb = xdt.astype(BF16)
            ht = hs_ref[gi, 0]
            hb = ht.astype(BF16)
            dht = dh_ref[gi]
            dhb = dht.astype(BF16)
            y_off = _dot(cb, hb) * dinx
            dx_off = _dot(bb, dhb) * dstx
            dyd = (dy_all * dinx).astype(BF16)
            xd = (xdt * dstx).astype(BF16)
            dc_acc = _dot(dyd, hb, _NT)
            db_acc = _dot(xd, dhb, _NT)
            dh_ref[gi] = dht * etx_ref[0, 0:1, gi * gw:(gi + 1) * gw] + _dot(cb, dyd, _TN)
            q_cols = _dot((dy_all * y_off).astype(BF16), sel)
            c_cols = _dot((xdt * dx_off).astype(BF16), sel)
            through = _sum0(_dot((dht * ht).astype(BF16), sel))
            dcbt = jnp.zeros((CHUNK, CHUNK), F32)
            for p in range(j_heads // 2):
                lanes = slice(p * CHUNK, (p + 1) * CHUNK)
                out_lanes = slice(gi * gw + p * CHUNK, gi * gw + (p + 1) * CHUNK)
                x2b = xb[:, lanes]
                dy2 = dy_all[:, lanes]
                acc = dx_off[:, lanes]
                for idx, j in enumerate((2 * p, 2 * p + 1)):
                    gc = gcol[:, j * CHUNK:(j + 1) * CHUNK]
                    gr = gamt_ref[pl.ds(gi * j_heads + j, 1), :]
                    decay = jnp.exp(jnp.where(mask, gc - gr, NEG))
                    decay_t = jnp.exp(jnp.where(mask_t, gr - gc, NEG))
                    dy_head = jnp.where(lo if idx == 0 else jnp.logical_not(lo), dy2, 0.0).astype(BF16)
                    acc = acc + _dot((cbt_t * decay_t).astype(BF16), dy_head)
                    dm = decay * _dot(dy_head, x2b, _NT)
                    dcbt = dcbt + dm
                    e = (cbt * dm).astype(BF16)
                    in_lane_j = si == j
                    q_cols = (q_cols + jnp.where(in_lane_j, jnp.sum(e.astype(F32), axis=1, keepdims=True), 0.0)
                              - _dot(e, jnp.where(in_lane_j, 1.0, 0.0).astype(BF16), _TN))
                dxs_ref[:, out_lanes] = (acc * dtx[:, lanes]).astype(dxs_ref.dtype)
                tmp_ref[:, lanes] = acc * x_all[:, lanes]
            dcb = dcbt.astype(BF16)
            dc_ref[:, grp_lanes] = (dc_acc + _dot(dcb, bb)).astype(dc_ref.dtype)
            db_ref[:, grp_lanes] = (db_acc + _dot(dcb, cb, _TN)).astype(db_ref.dtype)
            ddt_ref[:, gi * LANES:(gi + 1) * LANES] = _dot(tmp_ref[...].astype(BF16), sel)
            da_ref[:, gi * LANES:(gi + 1) * LANES] = (
                jnp.dot(incl, q_cols, preferred_element_type=F32, precision=hp)
                + jnp.dot(excl, c_cols, preferred_element_type=F32, precision=hp)
                + through * etg_ref[0, 0:1, gi * LANES:(gi + 1) * LANES])
        _plan_phase(plan, "end", n_steps, plan_ins, plan_outs, plan_sems)

    gn = N_GROUPS * D_STATE
    res = pl.pallas_call(
        kern, name=name, grid=n_steps,
        in_specs=[sp["xs"], sp["b"], sp["c"], sp["xs"], sp["state"], sp["q"], sp["gam_t"], sp["etot_g"], sp["etot_x"],
                  sp["ex"], sp["ex_gam"], sp["ex"], sp["ex"], sp["sel"]] + p_in_specs,
        out_specs=[sp["xs"], sp["grp"], sp["grp"], sp["grp"], sp["grp"]] + p_out_specs,
        out_shape=[jax.ShapeDtypeStruct((s, d_ssm), BF16), jax.ShapeDtypeStruct((s, gn), BF16),
                   jax.ShapeDtypeStruct((s, gn), BF16), jax.ShapeDtypeStruct((s, gn), F32),
                   jax.ShapeDtypeStruct((s, gn), F32)] + p_out_shapes,
        scratch_shapes=[pltpu.VMEM((gps, D_STATE, gw), F32), pltpu.VMEM((CHUNK, gw), F32)] + p_scratch,
        compiler_params=_params(semantics),
    )(xbc_c, xbc_c, xbc_c, dy, hs, q_all, gam_t, etot_g, etot_x, tb["ex_dt"], tb["ex_gam"], tb["ex_din"], tb["ex_dst"],
      tb["sel_cols"], *p_ops)
    return res[:5], list(res[5:])


def _dt_prepare(dt_raw, dt_bias, a_neg, route):
    s, h2 = dt_raw.shape
    n_heads = h2 // 2
    qw = route.shape[1]

    def kern(raw_ref, bias_ref, a_ref, route_ref, dt_ref, q_ref, gamt_ref, etot_ref):
        dt = _softplus(raw_ref[...] + bias_ref[...])
        a = dt * a_ref[...]
        li = lax.broadcasted_iota(jnp.int32, (CHUNK, CHUNK), 0)
        si = lax.broadcasted_iota(jnp.int32, (CHUNK, CHUNK), 1)
        tri = (li >= si).astype(F32)
        cs = jnp.dot(tri, a, preferred_element_type=F32, precision=lax.Precision.HIGHEST)
        tot = _sum0(a)
        fwd = lax.broadcasted_iota(jnp.int32, (CHUNK, h2), 1) < n_heads
        gam = jnp.where(fwd, cs, a - cs)
        din = jnp.where(fwd, jnp.exp(cs), jnp.exp(tot + gam))
        dst = jnp.where(fwd, jnp.exp(tot - cs), jnp.exp(cs - a))
        pieces = []
        for v in (dt, gam, din, dst):
            hi = v.astype(BF16)
            rest = v - hi.astype(F32)
            mid = rest.astype(BF16)
            pieces += [hi, mid, (rest - mid.astype(F32)).astype(BF16)]
        q_ref[...] = _dot(jnp.concatenate(pieces, axis=1), route_ref[...]).astype(BF16)
        dt_ref[...] = dt
        gamt_ref[...] = gam.T
        etot_ref[...] = jnp.broadcast_to(jnp.exp(tot), (8, h2))

    nc = s // CHUNK
    rows = lambda w: pl.BlockSpec((CHUNK, w), lambda i: (i, 0))
    whole = lambda a: pl.BlockSpec(a.shape, lambda i: (0, 0))
    return pl.pallas_call(
        kern, name="dt_prepare", grid=(nc,),
        in_specs=[rows(h2), whole(dt_bias), whole(a_neg), whole(route)],
        out_specs=[rows(h2), rows(qw), pl.BlockSpec((h2, CHUNK), lambda i: (0, i)), pl.BlockSpec((8, h2), lambda i: (i, 0))],
        out_shape=[jax.ShapeDtypeStruct((s, h2), F32), jax.ShapeDtypeStruct((s, qw), BF16),
                   jax.ShapeDtypeStruct((h2, s), F32), jax.ShapeDtypeStruct((nc * 8, h2), F32)],
        compiler_params=_params(("parallel",)),
    )(dt_raw, dt_bias, a_neg, route)


def _from_group_lanes(arr, j_heads):
    s = arr.shape[0]
    return arr.reshape(s, N_GROUPS, LANES)[:, :, :j_heads].reshape(s, N_GROUPS * j_heads)


def _dt_backward(da, ddt, dt, dt_raw, dt_bias, a_neg):
    h2 = da.shape[1]

    def body(dav, ddtv, dtv, raw, bias, a_head):
        draw = (ddtv + dav * a_head) * _sigmoid(raw + bias)
        return draw, _sum0(draw), _sum0(dav * dtv) * a_head
    return _rowwise("dt_backward", body, [da, ddt, dt, dt_raw], [dt_bias, a_neg], [(h2, BF16)], [h2, h2])


def _rms(x):
    r = lax.rsqrt(_mean1(x * x) + EPS)
    return x * r, r


def _rms_bwd(dy, y, r):
    return r * (dy - y * _mean1(dy * y))


def _norm_mod_fwd(name, x, g, sc, sh):
    def body(xv, gv, scv, shv):
        y, _ = _rms(xv)
        return ((y * gv) * (1.0 + scv) + shv,)
    return _rowwise(name, body, [x], [g, sc, sh], [(x.shape[1], BF16)])[0]


def _norm_mod_bwd(name, x, dh, dpass, g, sc):
    d = x.shape[1]

    def body(xv, dhv, dpv, gv, scv):
        y, r = _rms(xv)
        dn = dhv * (1.0 + scv)
        dx = _rms_bwd(dn * gv, y, r) + dpv
        return dx, _sum0(dn * y), _sum0(dhv * (y * gv)), _sum0(dhv)
    return _rowwise(name, body, [x, dh, dpass], [g, sc], [(d, F32)], [d, d, d])


def _gated_residual_fwd(name, x, m, gate, gp):
    def body(xv, mv, gatev, gpv):
        y, _ = _rms(mv)
        return (xv + gatev * (y * gpv),)
    return _rowwise(name, body, [x, m], [gate, gp], [(x.shape[1], F32)])[0]


def _gated_residual_bwd(name, m, dx1, gate, gp):
    d = m.shape[1]

    def body(mv, dv, gatev, gpv):
        y, r = _rms(mv)
        dn = dv * gatev
        return _rms_bwd(dn * gpv, y, r), _sum0(dv * (y * gpv)), _sum0(dn * y)
    return _rowwise(name, body, [m, dx1], [gate, gp], [(d, BF16)], [d, d])


def _final_residual_loss(x1, f, tgt, gate, gp):
    d = x1.shape[1]

    def body(xv, fv, tv, gatev, gpv):
        y, r = _rms(fv)
        n = y * gpv
        err = xv + gatev * n - tv
        dx2 = err * (1.0 / d)
        dn = dx2 * gatev
        sq = jnp.sum(_sum0(err * err), axis=1, keepdims=True)
        return dx2, _rms_bwd(dn * gpv, y, r), jnp.broadcast_to(sq, (1, LANES)), _sum0(dx2 * n), _sum0(dn * y)
    return _rowwise("final_residual_loss", body, [x1, f, tgt], [gate, gp], [(d, F32), (d, BF16)], [LANES, d, d])


def _swiglu_fwd(gu):
    f = gu.shape[1] // 2

    def body(v):
        return (_silu(v[:, :f]) * v[:, f:],)
    return _rowwise("swiglu_fwd", body, [gu], [], [(f, BF16)])[0]


def _swiglu_bwd(gu, dact):
    f = gu.shape[1] // 2

    def body(v, dv):
        gt, up = v[:, :f], v[:, f:]
        return (jnp.concatenate([dv * up * _dsilu(gt), dv * _silu(gt)], axis=1),)
    return _rowwise("swiglu_bwd", body, [gu, dact], [], [(2 * f, BF16)])[0]


def _glu_fwd(glu_in, b_glu):
    c = glu_in.shape[1] // 2

    def body(v, bv):
        t = v + bv
        return (t[:, :c] * _sigmoid(t[:, c:]),)
    return _rowwise("glu_fwd", body, [glu_in], [b_glu], [(c, F32)])[0]


def _glu_bwd(glu_in, b_glu, du0):
    c = glu_in.shape[1] // 2

    def body(v, dv, bv):
        t = v + bv
        a, s = t[:, :c], _sigmoid(t[:, c:])
        dg = jnp.concatenate([dv * s, dv * a * s * (1.0 - s)], axis=1)
        return dg, _sum0(dg)
    return _rowwise("glu_bwd", body, [glu_in, du0], [b_glu], [(2 * c, BF16)], [2 * c])


def _ln_parts(u1):
    xc = u1 - _mean1(u1)
    r = lax.rsqrt(_mean1(xc * xc) + EPS)
    return xc * r, r


def _ln_silu_fwd(u1, ln_g, ln_b):
    def body(v, gv, bv):
        yh, _ = _ln_parts(v)
        return (_silu(yh * gv + bv),)
    return _rowwise("ln_silu_fwd", body, [u1], [ln_g, ln_b], [(u1.shape[1], BF16)])[0]


def _ln_silu_bwd(u1, du, ln_g, ln_b):
    d = u1.shape[1]

    def body(v, dv, gv, bv):
        yh, r = _ln_parts(v)
        dl = dv * _dsilu(yh * gv + bv)
        dyh = dl * gv
        du1 = r * (dyh - _mean1(dyh) - yh * _mean1(dyh * yh))
        return du1, _sum0(dl * yh), _sum0(dl)
    return _rowwise("ln_silu_bwd", body, [u1, du], [ln_g, ln_b], [(d, F32)], [d, d])


def _gate_merge_fwd(y_a, y_b, gl, b_gate):
    d = y_a.shape[1]

    def body(ya, yb, glv, bv):
        s = _sigmoid(glv + bv)
        return (s[:, :d] * ya + s[:, d:] * yb,)
    return _rowwise("gate_merge_fwd", body, [y_a, y_b, gl], [b_gate], [(d, BF16)])[0]


def _gate_merge_bwd(dmixin, y_a, y_b, gl, b_gate):
    d = y_a.shape[1]

    def body(dv, ya, yb, glv, bv):
        s = _sigmoid(glv + bv)
        sa, sb = s[:, :d], s[:, d:]
        dya, dyb = dv * sa, dv * sb
        dgl = jnp.concatenate([dv * ya * sa * (1.0 - sa), dv * yb * sb * (1.0 - sb)], axis=1)
        return dya, dyb, dgl, _sum0(dgl), _sum0(dyb)
    return _rowwise("gate_merge_bwd", body, [dmixin, y_a, y_b, gl], [b_gate],
                    [(d, BF16), (d, BF16), (2 * d, BF16)], [2 * d, d])


def _group_slices(d_ssm):
    gw = d_ssm // N_GROUPS
    return [slice(g * gw, (g + 1) * gw) for g in range(N_GROUPS)]


def _gated_norm_fwd(y_f, y_b, xbc_c, z, d_skip_x, g_ssm):
    d_ssm = y_f.shape[1]

    def body(yf, yb, xs, zv, dsk, gv):
        y = yf + yb + dsk * xs
        v = y * _silu(zv)
        outs = []
        for sl in _group_slices(d_ssm):
            w, _ = _rms(v[:, sl])
            outs.append(w)
        return y, jnp.concatenate(outs, axis=1) * gv
    return _rowwise("gated_norm_fwd", body, [y_f, y_b, (xbc_c, d_ssm, 0), z], [d_skip_x, g_ssm],
                    [(d_ssm, F32), (d_ssm, BF16)])


def _gated_norm_bwd(y, z, dyn, xbc_c, d_skip_x, g_ssm):
    d_ssm = y.shape[1]

    def body(yv, zv, dv, xs, dsk, gv):
        sz = _silu(zv)
        v = yv * sz
        dw = dv * gv
        dvs, ws = [], []
        for sl in _group_slices(d_ssm):
            w, r = _rms(v[:, sl])
            ws.append(w)
            dvs.append(_rms_bwd(dw[:, sl], w, r))
        dvv = jnp.concatenate(dvs, axis=1)
        dy = dvv * sz
        return dy, dvv * yv * _dsilu(zv), _sum0(dv * jnp.concatenate(ws, axis=1)), _sum0(dy * xs)
    return _rowwise("gated_norm_bwd", body, [y, z, dyn, (xbc_c, d_ssm, 0)], [d_skip_x, g_ssm],
                    [(d_ssm, BF16), (d_ssm, BF16)], [d_ssm, d_ssm])


def _ssd_grad_merge(dxs_f, dxs_b, dy, db_f, db_b, dc_f, dc_b, d_skip_x):
    d_ssm = dy.shape[1]
    width = d_ssm + 2 * N_GROUPS * D_STATE

    def body(xf, xb, dv, bf, bb, cf, cbv, dsk):
        return (jnp.concatenate([xf + xb + dsk * dv, bf + bb, cf + cbv], axis=1),)
    return _rowwise("ssd_grad_merge", body, [dxs_f, dxs_b, dy, db_f, db_b, dc_f, dc_b], [d_skip_x], [(width, BF16)])[0]


def _adamw(name, w, g, m, v):
    c = w.shape[1]
    c1 = 1.0 - ADAM_B1 ** ADAM_STEP
    c2 = 1.0 - ADAM_B2 ** ADAM_STEP

    def body(wv, gv, mv, vv):
        mn = ADAM_B1 * mv + (1.0 - ADAM_B1) * gv
        vn = ADAM_B2 * vv + (1.0 - ADAM_B2) * (gv * gv)
        delta = -ADAM_LR * ((mn / c1) / (jnp.sqrt(vn / c2) + ADAM_EPS) + ADAM_WD * wv)
        return delta, mn, vn
    return _rowwise(name, body, [w, g, m, v], [], [(c, F32)] * 3)


def _local_step(x, tgt, mod, wts, sm, late=None, core=None):
    s, d = x.shape
    d_ssm = 2 * d
    n_heads = d_ssm // HEAD_DIM
    d_xbc = d_ssm + 2 * N_GROUPS * D_STATE
    sec = [0, d_ssm, d_ssm + d_xbc, d_ssm + d_xbc + 2 * n_heads, d_ssm + d_xbc + 2 * n_heads + 2 * d]
    sec.append(sec[-1] + 2 * d)
    sh1, sc1, g1, sh2, sc2, g2 = [mod[:, i * d:(i + 1) * d] for i in range(N_MOD)]
    win_t = wts["w_in_t"]
    win_secs = [win_t[sec[i]:sec[i + 1]] for i in range(5)]

    h1 = _norm_mod_fwd("pre_mix_norm", x, sm["g_pre_mix"], sc1, sh1)
    z, xbc, dt_raw, glu_in, gate_l = [
        _matmul(f"proj_{nm}", h1, w, tb=True, out_dtype=F32 if nm == "dt" else BF16)
        for nm, w in zip(("z", "xbc", "dt", "glu", "gate"), win_secs)]
    xbc_c = _dwconv_fwd("ssm_conv_fwd", xbc, sm["w_conv_ssm"], sm["b_conv_ssm"], silu=True, out_dtype=BF16)
    tables = _scan_tables(n_heads)
    dt, q_all, gam_t, etot = _dt_prepare(dt_raw, sm["dt_bias"], sm["a_neg"], tables["route"])
    etot = etot.reshape(s // CHUNK, 8, 2 * n_heads)
    (etx_f, etg_f), (etx_r, etg_r) = [_chunk_decay(etot, n_heads, direction) for direction in (0, 1)]
    plan_f = plan_r = None
    if late is not None:
        plan_f = _gather_plan([late[n] for n in MATRICES[1:4]])
        plan_r = _gather_plan([late[n] for n in MATRICES[4:]])
    y_f, hs_f, got_f = _scan_fwd("ssd_fwd_f", xbc_c, q_all, gam_t, etx_f, tables, direction=0, d_ssm=d_ssm, plan=plan_f)
    y_r, hs_r, got_r = _scan_fwd("ssd_fwd_r", xbc_c, q_all, gam_t, etx_r, tables, direction=1, d_ssm=d_ssm, plan=plan_r)
    if late is not None:
        wts = dict(wts, w_ssm_out=got_f[0].reshape(-1, d), w_conv_out=got_f[1].reshape(-1, d),
                   w_mix_out=got_f[2].reshape(-1, d), w_gate_up=got_r[0], w_down=got_r[1].reshape(-1, d))
    y_ssd, yn = _gated_norm_fwd(y_f, y_r, xbc_c, z, sm["d_skip_x"], sm["g_ssm_norm"])
    y_a = _matmul("ssm_out", yn, wts["w_ssm_out"])
    u0 = _glu_fwd(glu_in, sm["b_glu"])
    u1 = _dwconv_fwd("dw_conv_fwd", u0, sm["w_dw"], sm["b_dw"], silu=False)
    u = _ln_silu_fwd(u1, sm["ln_g"], sm["ln_b"])
    y_b = _matmul("conv_out", u, wts["w_conv_out"], bias=sm["b_conv_out"])
    mixin = _gate_merge_fwd(y_a, y_b, gate_l, sm["b_gate"])
    mix = _matmul("mix_out", mixin, wts["w_mix_out"])
    x1 = _gated_residual_fwd("post_mix_residual", x, mix, g1, sm["g_post_mix"])
    h2 = _norm_mod_fwd("pre_ffn_norm", x1, sm["g_pre_ffn"], sc2, sh2)
    gu = _matmul("ffn_gate_up", h2, wts["w_gate_up"], b_blocks=N_CHIPS, out_dtype=BF16)
    act = _swiglu_fwd(gu)
    f = _matmul("ffn_down", act, wts["w_down"])

    dx2, df, sq, d_g2, d_gpf = _final_residual_loss(x1, f, tgt, g2, sm["g_post_ffn"])
    dact = _matmul("d_act", df, wts["w_down"], tb=True, out_dtype=BF16)
    g_w_down = _matmul("g_w_down", act, df, ta=True, out_dtype=BF16)
    dgu = _swiglu_bwd(gu, dact)
    dh2 = _matmul("d_h2", dgu, wts["w_gate_up"], tb=True, b_blocks=N_CHIPS)
    g_w_gate_up = _matmul("g_w_gate_up", h2, dgu, ta=True, out_dtype=BF16, out_blocks=N_CHIPS)
    dx1, d_gpre_ffn, d_sc2, d_sh2 = _norm_mod_bwd("pre_ffn_norm_bwd", x1, dh2, dx2, sm["g_pre_ffn"], sc2)
    dmix, d_g1, d_gpm = _gated_residual_bwd("post_mix_residual_bwd", mix, dx1, g1, sm["g_post_mix"])
    dmixin = _matmul("d_mixin", dmix, wts["w_mix_out"], tb=True, out_dtype=BF16)
    g_w_mix = _matmul("g_w_mix_out", mixin, dmix, ta=True, out_dtype=BF16)
    dy_a, dy_b, dgate_l, d_bgate, d_bco = _gate_merge_bwd(dmixin, y_a, y_b, gate_l, sm["b_gate"])
    du = _matmul("d_u", dy_b, wts["w_conv_out"], tb=True, out_dtype=BF16)
    g_w_co = _matmul("g_w_conv_out", u, dy_b, ta=True, out_dtype=BF16)
    du1, d_lng, d_lnb = _ln_silu_bwd(u1, du, sm["ln_g"], sm["ln_b"])
    du0, d_wdw, d_bdw = _dwconv_bwd("dw_conv_bwd", u0, sm["w_dw"], sm["b_dw"], du1, silu=False)
    dglu, d_bglu = _glu_bwd(glu_in, sm["b_glu"], du0)
    dyn = _matmul("d_yn", dy_a, wts["w_ssm_out"], tb=True, out_dtype=BF16)
    g_w_ssm = _matmul("g_w_ssm_out", yn, dy_a, ta=True, out_dtype=BF16)
    dy_ssd, dz, d_gssm, d_dskip_x = _gated_norm_bwd(y_ssd, z, dyn, xbc_c, sm["d_skip_x"], sm["g_ssm_norm"])
    early = [g_w_ssm.reshape(N_CHIPS, -1, d), g_w_co.reshape(N_CHIPS, -1, d), g_w_mix.reshape(N_CHIPS, -1, d),
             g_w_gate_up, g_w_down.reshape(N_CHIPS, -1, d)]
    plan_b = sums = None
    if late is not None:
        sums = _chip_sums("early", early, core)
        plan_b = _send_chips_plan(sums)
    (dxs_f, db_f, dc_f, ddt_f, da_f), received = _scan_bwd(
        "ssd_bwd_f", xbc_c, dy_ssd, hs_f, q_all, gam_t, etx_f, etg_f, tables, direction=0, d_ssm=d_ssm, plan=plan_b)
    (dxs_r, db_r, dc_r, ddt_r, da_r), _ = _scan_bwd(
        "ssd_bwd_r", xbc_c, dy_ssd, hs_r, q_all, gam_t, etx_r, etg_r, tables, direction=1, d_ssm=d_ssm)
    j_heads = n_heads // N_GROUPS
    da = jnp.concatenate([_from_group_lanes(da_f, j_heads), _from_group_lanes(da_r, j_heads)], axis=1)
    ddt = jnp.concatenate([_from_group_lanes(ddt_f, j_heads), _from_group_lanes(ddt_r, j_heads)], axis=1)
    ddt_raw, d_dtbias, d_alog = _dt_backward(da, ddt, dt, dt_raw, sm["dt_bias"], sm["a_neg"])
    dxbc_c = _ssd_grad_merge(dxs_f, dxs_r, dy_ssd, db_f, db_r, dc_f, dc_r, sm["d_skip_x"])
    dxbc, d_wconv, d_bconv = _dwconv_bwd("ssm_conv_bwd", xbc, sm["w_conv_ssm"], sm["b_conv_ssm"], dxbc_c,
                                         silu=True, dx_dtype=BF16)
    dsecs = [dz, dxbc, ddt_raw, dglu, dgate_l]
    dh1 = None
    g_win = []
    for nm, dsec, w in zip(("z", "xbc", "dt", "glu", "gate"), dsecs, win_secs):
        dh1 = _matmul(f"d_h1_{nm}", dsec, w, add=dh1)
        g_win.append(_matmul(f"g_w_in_{nm}", dsec, h1, ta=True, out_dtype=BF16))
    grad_x, d_gpre_mix, d_sc1, d_sh1 = _norm_mod_bwd("pre_mix_norm_bwd", x, dh1, dx1, sm["g_pre_mix"], sc1)

    dmod = jnp.concatenate([d_sh1, d_sc1, d_g1, d_sh2, d_sc2, d_g2], axis=1)
    big = {"w_in_t": jnp.concatenate(g_win, axis=0)}
    if late is None:
        big.update(w_ssm_out=g_w_ssm, w_conv_out=g_w_co, w_mix_out=g_w_mix, w_gate_up=g_w_gate_up, w_down=g_w_down)
    else:
        big["pending"] = (sums, received)
    small = {"g_pre_mix": d_gpre_mix, "g_post_mix": d_gpm, "w_conv_ssm": d_wconv, "b_conv_ssm": d_bconv,
             "dt_bias": d_dtbias, "a_log": d_alog, "d_skip_x": d_dskip_x, "g_ssm_norm": d_gssm, "b_glu": d_bglu,
             "w_dw": d_wdw, "b_dw": d_bdw, "ln_g": d_lng, "ln_b": d_lnb, "b_conv_out": d_bco, "b_gate": d_bgate,
             "g_pre_ffn": d_gpre_ffn, "g_post_ffn": d_gpf}
    return sq, grad_x, big, small, dmod


ANY = pl.BlockSpec(memory_space=pl.ANY)
WHOLE_VMEM = pl.BlockSpec(memory_space=pltpu.VMEM)


def _mesh_place():
    x, y, c = lax.axis_index("x"), lax.axis_index("y"), lax.axis_index("c")
    other_chips = [(1 - x, y), (x, 1 - y), (1 - x, 1 - y)]
    return x, y, c, other_chips


def _remote(src, dst, send_sems, recv_sems, k, device):
    return pltpu.make_async_remote_copy(src_ref=src, dst_ref=dst, send_sem=send_sems.at[k], recv_sem=recv_sems.at[k],
                                        device_id=device, device_id_type=MESH)


def _gather_devices(name, block):
    m_per, n = block.shape

    def body(x_ref, out_ref, send_sems, recv_sems, local_sem):
        x, y, c, chips = _mesh_place()
        me, sibling = (x, y, c), (x, y, 1 - c)

        def rows(px, py, pc):
            return out_ref.at[pl.ds((4 * px + 2 * py + pc) * m_per, m_per), :]

        def copy(k, blk, to, src=None):
            return _remote(rows(*blk) if src is None else src, rows(*blk), send_sems, recv_sems, k, to)

        mine = pltpu.make_async_copy(x_ref, rows(*me), local_sem)
        mine.start()
        first = [copy(0, me, sibling, src=x_ref)]
        first += [copy(1 + j, me, (*chip, c), src=x_ref) for j, chip in enumerate(chips)]
        for cp in first:
            cp.start()
        passed = [copy(4 + j, (*chip, c), sibling) for j, chip in enumerate(chips)]
        for j, chip in enumerate(chips):
            copy(1 + j, (*chip, c), me).wait_recv()
            passed[j].start()
        copy(0, sibling, me).wait_recv()
        for j, chip in enumerate(chips):
            copy(4 + j, (*chip, 1 - c), me).wait_recv()
        for cp in first + passed:
            cp.wait_send()
        mine.wait()

    return pl.pallas_call(
        body, name=name, out_shape=jax.ShapeDtypeStruct((N_DEV * m_per, n), block.dtype),
        in_specs=[WHOLE_VMEM], out_specs=WHOLE_VMEM,
        scratch_shapes=[pltpu.SemaphoreType.DMA((7,)), pltpu.SemaphoreType.DMA((7,)), pltpu.SemaphoreType.DMA],
        compiler_params=pltpu.CompilerParams(vmem_limit_bytes=VMEM_LIMIT),
    )(block)


class _Plan:
    def __init__(self, operands, out_shapes, copies, phases):
        self.operands, self.out_shapes, self.copies, self.phases = list(operands), list(out_shapes), copies, phases

    def sems(self):
        return [pltpu.SemaphoreType.DMA((self.copies,)), pltpu.SemaphoreType.DMA((self.copies,))]


def _run_plan(name, plan):
    n_in, n_out = len(plan.operands), len(plan.out_shapes)

    def body(*refs):
        ins, outs = refs[:n_in], refs[n_in:n_in + n_out]
        send_sems, recv_sems = refs[n_in + n_out:]
        for phase in ("start", "middle", "end"):
            if phase in plan.phases:
                plan.phases[phase](ins, outs, send_sems, recv_sems)

    return list(pl.pallas_call(body, name=name, out_shape=plan.out_shapes, in_specs=[ANY] * n_in,
                               out_specs=[ANY] * n_out, scratch_shapes=plan.sems())(*plan.operands))


def _gather_plan(shards):
    n = len(shards)

    def copies(kinds, ins, outs, send_sems, recv_sems):
        x, y, c, chips = _mesh_place()
        me = 2 * x + y
        sibling = (x, y, 1 - c)

        def half(i, h):
            hr = ins[i].shape[0] // 2
            return pl.ds(h * hr, hr)

        def block(i, j, h):
            cx, cy = chips[j]
            return outs[i].at[2 * cx + cy, half(i, h)]

        make = {
            "over_ici": lambda i, j: _remote(ins[i].at[half(i, c)], outs[i].at[me, half(i, c)], send_sems, recv_sems,
                                             6 * i + j, (*chips[j], c)),
            "arrived": lambda i, j: _remote(block(i, j, c), block(i, j, c), send_sems, recv_sems, 6 * i + j, (*chips[j], c)),
            "passed_on": lambda i, j: _remote(block(i, j, c), block(i, j, c), send_sems, recv_sems, 6 * i + 3 + j, sibling),
            "from_sibling": lambda i, j: _remote(block(i, j, 1 - c), block(i, j, 1 - c), send_sems, recv_sems,
                                                 6 * i + 3 + j, sibling),
        }
        res = []
        for kind in kinds:
            if kind == "own":
                res.append([_remote(ins[i], outs[i].at[me], send_sems, recv_sems, 6 * n + i, sibling) for i in range(n)])
            else:
                res.append([make[kind](i, j) for i in range(n) for j in range(3)])
        return res

    def start(*refs):
        over_ici, own = copies(("over_ici", "own"), *refs)
        for cp in over_ici + own:
            cp.start()

    def middle(*refs):
        arrived, passed_on = copies(("arrived", "passed_on"), *refs)
        for got, fwd in zip(arrived, passed_on):
            got.wait_recv()
            fwd.start()

    def end(*refs):
        from_sibling, own_in = copies(("from_sibling", "own"), *refs)
        for cp in from_sibling + own_in:
            cp.wait_recv()
        over_ici, passed_on, own_out = copies(("over_ici", "passed_on", "own"), *refs)
        for cp in over_ici + passed_on + own_out:
            cp.wait_send()

    return _Plan(shards, [jax.ShapeDtypeStruct((N_CHIPS,) + s.shape, s.dtype) for s in shards], 7 * n,
                 {"start": start, "middle": middle, "end": end})


def _send_sibling_halves(name, grads):
    n = len(grads)

    def body(*refs):
        ins, outs = refs[:n], refs[n:2 * n]
        send_sems, recv_sems = refs[2 * n:]
        x, y, c, _ = _mesh_place()
        sibling = (x, y, 1 - c)
        copies = []
        for i in range(n):
            for j in range(N_CHIPS):
                copies.append(_remote(ins[i].at[j, 1 - c], outs[i].at[j], send_sems, recv_sems, N_CHIPS * i + j, sibling))
                copies[-1].start()
        for cp in copies:
            cp.wait_recv()
        for cp in copies:
            cp.wait_send()

    return pl.pallas_call(
        body, name=name,
        out_shape=[jax.ShapeDtypeStruct((g.shape[0],) + g.shape[2:], g.dtype) for g in grads],
        in_specs=[ANY] * n, out_specs=[ANY] * n,
        scratch_shapes=[pltpu.SemaphoreType.DMA((N_CHIPS * n,)), pltpu.SemaphoreType.DMA((N_CHIPS * n,))],
    )(*grads)


def _send_chips_plan(sums):
    n = len(sums)

    def copies(ins, outs, send_sems, recv_sems):
        x, y, c, chips = _mesh_place()
        return [_remote(ins[i].at[2 * cx + cy], outs[i].at[j], send_sems, recv_sems, 3 * i + j, (cx, cy, c))
                for i in range(n) for j, (cx, cy) in enumerate(chips)]

    def start(*refs):
        for cp in copies(*refs):
            cp.start()

    def end(*refs):
        for cp in copies(*refs):
            cp.wait_recv()
        for cp in copies(*refs):
            cp.wait_send()

    return _Plan(sums, [jax.ShapeDtypeStruct((3,) + g.shape[1:], g.dtype) for g in sums], 3 * n,
                 {"start": start, "end": end})


def _exchange_halves(name, shards):
    n = len(shards)

    def body(*refs):
        outs = refs[n:2 * n]
        send_sems, recv_sems = refs[2 * n:]
        x, y, c, _ = _mesh_place()
        sibling = (x, y, 1 - c)
        remote = [_remote(outs[i].at[c], outs[i].at[c], send_sems, recv_sems, i, sibling) for i in range(n)]
        for cp in remote:
            cp.start()
        for i in range(n):
            _remote(outs[i].at[1 - c], outs[i].at[1 - c], send_sems, recv_sems, i, sibling).wait_recv()
        for cp in remote:
            cp.wait_send()

    return pl.pallas_call(
        body, name=name,
        out_shape=[jax.ShapeDtypeStruct(h.shape, h.dtype) for h in shards],
        in_specs=[ANY] * n, out_specs=[ANY] * n, input_output_aliases={i: i for i in range(n)},
        scratch_shapes=[pltpu.SemaphoreType.DMA((n,)), pltpu.SemaphoreType.DMA((n,))],
    )(*shards)


def _divisor_tile(rows, row_bytes, quantum=16):
    best = rows
    for t in range(quantum, rows + 1, quantum):
        if rows % t == 0 and 2 * t * row_bytes <= ROW_TILE_BUDGET:
            best = t
    return best


def _add_sibling(name, g4, t1, core):
    nb, _, hr, cols = g4.shape
    t = _divisor_tile(hr, cols * 6)

    def kern(core_ref, g_ref, t_ref, o_ref):
        o_ref[0] = (g_ref[0, 0].astype(F32) + t_ref[0].astype(F32)).astype(o_ref.dtype)

    return pl.pallas_call(
        kern, name=name,
        grid_spec=pltpu.PrefetchScalarGridSpec(
            num_scalar_prefetch=1, grid=(nb, hr // t),
            in_specs=[pl.BlockSpec((1, 1, t, cols), lambda j, i, core_ref: (j, core_ref[0], i, 0)),
                      pl.BlockSpec((1, t, cols), lambda j, i, core_ref: (j, i, 0))],
            out_specs=pl.BlockSpec((1, t, cols), lambda j, i, core_ref: (j, i, 0))),
        out_shape=jax.ShapeDtypeStruct((nb, hr, cols), g4.dtype),
        compiler_params=_params(("parallel", "parallel")),
    )(core, g4, t1)


def _add_chips(name, s1, t3, place):
    _, hr, cols = s1.shape
    t = _divisor_tile(hr, cols * 12)

    def kern(place_ref, s_ref, t_ref, o_ref):
        acc = s_ref[0].astype(F32)
        for j in range(3):
            acc = acc + t_ref[j].astype(F32)
        o_ref[0] = acc

    return pl.pallas_call(
        kern, name=name,
        grid_spec=pltpu.PrefetchScalarGridSpec(
            num_scalar_prefetch=1, grid=(hr // t,),
            in_specs=[pl.BlockSpec((1, t, cols), lambda i, place_ref: (place_ref[0], i, 0)),
                      pl.BlockSpec((3, t, cols), lambda i, place_ref: (0, i, 0))],
            out_specs=pl.BlockSpec((1, t, cols), lambda i, place_ref: (place_ref[1], i, 0))),
        out_shape=jax.ShapeDtypeStruct((2, hr, cols), F32),
        compiler_params=_params(("parallel",)),
    )(place, s1, t3)


def _chip_sums(tag, grads, core):
    g4 = [g.reshape(N_CHIPS, 2, g.shape[1] // 2, g.shape[2]) for g in grads]
    t1 = _send_sibling_halves("grads_to_sibling_" + tag, g4)
    return [_add_sibling(f"chip_sum_{tag}_{i}", g, t, core) for i, (g, t) in enumerate(zip(g4, t1))]


def _shard_sums(sums, received, core, chip):
    place = jnp.concatenate([chip, core])
    halves = [_add_chips(f"shard_sum_{i}", s, t, place) for i, (s, t) in enumerate(zip(sums, received))]
    full = _exchange_halves("grad_halves_to_sibling", halves)
    return [f.reshape(f.shape[1] * 2, f.shape[2]) for f in full]


def _pack_rows(size, width):
    return -(-size // (8 * width)) * 8


def _pack(arrays, width):
    parts = []
    for a in arrays:
        flat = a.reshape(-1).astype(F32)
        rows = _pack_rows(flat.shape[0], width)
        parts.append(jnp.pad(flat, (0, rows * width - flat.shape[0])).reshape(rows, width))
    return jnp.concatenate(parts, axis=0)


def _unpack(block, shapes, width):
    out, r = [], 0
    for shp in shapes:
        size = 1
        for s_ in shp:
            size *= s_
        rows = _pack_rows(size, width)
        out.append(block[r:r + rows].reshape(-1)[:size].reshape(shp))
        r += rows
    return out


SMALL_PARAMS = ("b_ada", "g_pre_mix", "g_post_mix", "b_conv_ssm", "dt_bias_fwd", "dt_bias_bwd", "a_log_fwd", "a_log_bwd",
                "d_skip", "g_ssm_norm", "b_glu", "b_dw", "ln_g", "ln_b", "b_conv_out", "b_gate", "g_pre_ffn", "g_post_ffn")
SHARDED_SMALL = ("w_conv_ssm", "w_dw")
MATRICES = ("w_in", "w_ssm_out", "w_conv_out", "w_mix_out", "w_gate_up", "w_down")
ALL_PARAMS = ("w_ada", "b_ada", "g_pre_mix", "g_post_mix", "w_in", "w_conv_ssm", "b_conv_ssm", "dt_bias_fwd", "dt_bias_bwd",
              "a_log_fwd", "a_log_bwd", "d_skip", "g_ssm_norm", "w_ssm_out", "b_glu", "w_dw", "b_dw", "ln_g", "ln_b",
              "w_conv_out", "b_conv_out", "b_gate", "w_mix_out", "g_pre_ffn", "g_post_ffn", "w_gate_up", "w_down")
COND_ROWS = 48
COND_CONV_ROW = 8
COND_DW_ROW = 16
MOD_ROWS = 16


def kernel(x, c, w_ada, b_ada, g_pre_mix, g_post_mix, w_in, w_conv_ssm, b_conv_ssm, dt_bias_fwd, dt_bias_bwd, a_log_fwd, a_log_bwd, d_skip, g_ssm_norm, w_ssm_out, b_glu, w_dw, b_dw, ln_g, ln_b, w_conv_out, b_conv_out, b_gate, w_mix_out, g_pre_ffn, g_post_ffn, w_gate_up, w_down, loss_target, m_w_ada, m_b_ada, m_g_pre_mix, m_g_post_mix, m_w_in, m_w_conv_ssm, m_b_conv_ssm, m_dt_bias_fwd, m_dt_bias_bwd, m_a_log_fwd, m_a_log_bwd, m_d_skip, m_g_ssm_norm, m_w_ssm_out, m_b_glu, m_w_dw, m_b_dw, m_ln_g, m_ln_b, m_w_conv_out, m_b_conv_out, m_b_gate, m_w_mix_out, m_g_pre_ffn, m_g_post_ffn, m_w_gate_up, m_w_down, v_w_ada, v_b_ada, v_g_pre_mix, v_g_post_mix, v_w_in, v_w_conv_ssm, v_b_conv_ssm, v_dt_bias_fwd, v_dt_bias_bwd, v_a_log_fwd, v_a_log_bwd, v_d_skip, v_g_ssm_norm, v_w_ssm_out, v_b_glu, v_w_dw, v_b_dw, v_ln_g, v_ln_b, v_w_conv_out, v_b_conv_out, v_b_gate, v_w_mix_out, v_g_pre_ffn, v_g_post_ffn, v_w_gate_up, v_w_down):
    given = dict(locals())
    wgt = {n: given[n][0] for n in ALL_PARAMS}
    mom = {n: given["m_" + n][0] for n in ALL_PARAMS}
    var = {n: given["v_" + n][0] for n in ALL_PARAMS}
    xs, tgt = x[0], loss_target[0]
    s, d = xs.shape
    d_ssm = 2 * d
    n_heads = d_ssm // HEAD_DIM
    d_xbc = d_ssm + 2 * N_GROUPS * D_STATE
    xi, yi, ci = lax.axis_index("x"), lax.axis_index("y"), lax.axis_index("c")
    chip = 2 * xi + yi
    dev = 2 * chip + ci
    core_arr = jnp.reshape(ci, (1,)).astype(jnp.int32)
    chip_arr = jnp.reshape(chip, (1,)).astype(jnp.int32)
    k_conv, k_dw = wgt["w_conv_ssm"].shape[0], wgt["w_dw"].shape[0]
    xbc_shard, dw_shard = d_xbc // N_CHIPS, d // N_CHIPS

    width1 = max(d, xbc_shard)
    blk = jnp.zeros((COND_ROWS, width1), F32)
    blk = blk.at[0, :d].set(c[0])
    blk = blk.at[COND_CONV_ROW:COND_CONV_ROW + k_conv, :xbc_shard].set(wgt["w_conv_ssm"])
    blk = blk.at[COND_DW_ROW:COND_DW_ROW + k_dw, :dw_shard].set(wgt["w_dw"])
    g1 = _gather_devices("gather_cond", blk).reshape(N_DEV, COND_ROWS, width1)
    c_all = g1[:, 0, :d]
    w_conv_full = jnp.concatenate([g1[2 * k, COND_CONV_ROW:COND_CONV_ROW + k_conv, :xbc_shard] for k in range(N_CHIPS)], axis=1)
    w_dw_full = jnp.concatenate([g1[2 * k, COND_DW_ROW:COND_DW_ROW + k_dw, :dw_shard] for k in range(N_CHIPS)], axis=1)
    c_act = jnp.pad(c_all * _sigmoid(c_all), ((0, MOD_ROWS - N_DEV), (0, 0)))

    mod_part = _matmul("ada_mod", c_act, wgt["w_ada"])
    g2 = _gather_devices("gather_mod", mod_part).reshape(N_DEV, MOD_ROWS, mod_part.shape[1])
    mod_all = jnp.concatenate([g2[2 * k, :N_DEV] for k in range(N_CHIPS)], axis=1) + wgt["b_ada"][None]
    mod = lax.dynamic_slice_in_dim(mod_all, dev, 1, axis=0)

    shards = [wgt["w_in"].T.astype(BF16)] + [wgt[n].astype(BF16) for n in MATRICES[1:]]
    wts = {"w_in_t": _run_plan("gather_w_in", _gather_plan(shards[:1]))[0].reshape(-1, d)}
    late = dict(zip(MATRICES[1:], shards[1:]))
    row = lambda v: v.reshape(1, -1)
    sm = {"g_pre_mix": row(wgt["g_pre_mix"]), "g_post_mix": row(wgt["g_post_mix"]), "w_conv_ssm": w_conv_full,
          "b_conv_ssm": row(wgt["b_conv_ssm"]),
          "dt_bias": row(jnp.concatenate([wgt["dt_bias_fwd"], wgt["dt_bias_bwd"]])),
          "a_neg": row(-jnp.exp(jnp.concatenate([wgt["a_log_fwd"], wgt["a_log_bwd"]]))),
          "d_skip_x": row(jnp.repeat(wgt["d_skip"], HEAD_DIM)), "g_ssm_norm": row(wgt["g_ssm_norm"]),
          "b_glu": row(wgt["b_glu"]), "w_dw": w_dw_full, "b_dw": row(wgt["b_dw"]), "ln_g": row(wgt["ln_g"]),
          "ln_b": row(wgt["ln_b"]), "b_conv_out": row(wgt["b_conv_out"]), "b_gate": row(wgt["b_gate"]),
          "g_pre_ffn": row(wgt["g_pre_ffn"]), "g_post_ffn": row(wgt["g_post_ffn"])}

    sq, grad_x, big, small, dmod = _local_step(xs, tgt, mod, wts, sm, late=late, core=core_arr)
    loss = lax.psum((0.5 / d) * sq[0, 0], ("x", "y", "c"))

    local_small = {"b_ada": dmod, "g_pre_mix": small["g_pre_mix"], "g_post_mix": small["g_post_mix"],
                   "b_conv_ssm": small["b_conv_ssm"], "dt_bias_fwd": small["dt_bias"][:, :n_heads],
                   "dt_bias_bwd": small["dt_bias"][:, n_heads:], "a_log_fwd": small["a_log"][:, :n_heads],
                   "a_log_bwd": small["a_log"][:, n_heads:],
                   "d_skip": jnp.sum(small["d_skip_x"].reshape(n_heads, HEAD_DIM), axis=1),
                   "g_ssm_norm": small["g_ssm_norm"], "b_glu": small["b_glu"], "b_dw": small["b_dw"],
                   "ln_g": small["ln_g"], "ln_b": small["ln_b"], "b_conv_out": small["b_conv_out"],
                   "b_gate": small["b_gate"], "g_pre_ffn": small["g_pre_ffn"], "g_post_ffn": small["g_post_ffn"],
                   "w_conv_ssm": small["w_conv_ssm"], "w_dw": small["w_dw"]}
    names = SMALL_PARAMS + SHARDED_SMALL
    pack = _pack([local_small[n] for n in names], d)
    rows_p = pack.shape[0]
    g3 = _gather_devices("gather_small_grads", pack).reshape(N_DEV, rows_p, d)
    total = _rowwise("sum_small_grads", lambda *blocks: (functools.reduce(lambda a, b: a + b, blocks),),
                     [g3[i] for i in range(N_DEV)], [], [(d, F32)])[0]
    full_shapes = [wgt[n].shape for n in SMALL_PARAMS] + [(k_conv, d_xbc), (k_dw, d)]
    summed = dict(zip(names, _unpack(total, full_shapes, d)))
    grads = {n: summed[n] for n in SMALL_PARAMS}
    grads["w_conv_ssm"] = lax.dynamic_slice_in_dim(summed["w_conv_ssm"], chip * xbc_shard, xbc_shard, axis=1)
    grads["w_dw"] = lax.dynamic_slice_in_dim(summed["w_dw"], chip * dw_shard, dw_shard, axis=1)

    dmod_all = g3[:, :N_MOD, :].reshape(N_DEV, N_MOD * d)
    ada_cols = wgt["w_ada"].shape[1]
    dmod_cols = jnp.pad(lax.dynamic_slice_in_dim(dmod_all, chip * ada_cols, ada_cols, axis=1),
                        ((0, MOD_ROWS - N_DEV), (0, 0)))
    grads["w_ada"] = _matmul("g_w_ada", c_act, dmod_cols, ta=True)

    sums_in = _chip_sums("w_in", [big["w_in_t"].reshape(N_CHIPS, -1, d)], core_arr)
    received_in = _run_plan("grads_to_chips_w_in", _send_chips_plan(sums_in))
    sums_early, received_early = big["pending"]
    reduced = _shard_sums(sums_in + sums_early, received_in + received_early, core_arr, chip_arr)
    grads["w_in"] = reduced[0].T
    for n, g in zip(MATRICES[1:], reduced[1:]):
        grads[n] = g

    delta, new_m, new_v = {}, {}, {}
    for n in ("w_ada",) + MATRICES:
        delta[n], new_m[n], new_v[n] = _adamw("adamw_" + n, wgt[n], grads[n], mom[n], var[n])
    for group, width, tag in ((SMALL_PARAMS, d, "small"), (SHARDED_SMALL, LANES, "conv")):
        shapes = [wgt[n].shape for n in group]
        packs = [_pack([src[n] for n in group], width) for src in (wgt, grads, mom, var)]
        outs = _adamw("adamw_" + tag, *packs)
        for res, o in zip((delta, new_m, new_v), outs):
            res.update(zip(group, _unpack(o, shapes, width)))

    lead = lambda a: a[None]
    return (loss, grad_x[None], *[lead(grads[n]) for n in ALL_PARAMS], *[lead(delta[n]) for n in ALL_PARAMS],
            *[lead(new_m[n]) for n in ALL_PARAMS], *[lead(new_v[n]) for n in ALL_PARAMS])
```

```python
import functools

import jax
import jax.numpy as jnp
from jax import lax
from jax.experimental import pallas as pl
from jax.experimental.pallas import tpu as pltpu

F32 = jnp.float32
BF16 = jnp.bfloat16

N_GROUPS = 8
HEAD_DIM = 64
D_STATE = 128
CHUNK = 128
EPS = 1e-6
N_MOD = 6
ADAM_LR = 0.001
ADAM_B1 = 0.9
ADAM_B2 = 0.999
ADAM_EPS = 1e-08
ADAM_WD = 0.01
ADAM_STEP = 10

V7X_VMEM_BYTES = 64 * 1024 * 1024
VMEM_LIMIT = V7X_VMEM_BYTES - 8 * 1024 * 1024
ROW_TILE_BUDGET = 20 * 1024 * 1024
LANES = 128
NEG = -1e30
MESH = pl.DeviceIdType.MESH
N_CHIPS = 4
N_DEV = 8


def _params(sem):
    return pltpu.CompilerParams(dimension_semantics=sem, vmem_limit_bytes=VMEM_LIMIT)


def _sigmoid(x):
    return 1.0 / (1.0 + jnp.exp(-x))


def _silu(x):
    return x * _sigmoid(x)


def _dsilu(x):
    s = _sigmoid(x)
    return s * (1.0 + x * (1.0 - s))


def _softplus(x):
    return jnp.maximum(x, 0.0) + jnp.log(1.0 + jnp.exp(-jnp.abs(x)))


def _sum0(a):
    return jnp.sum(a, axis=0, keepdims=True)


def _mean1(a):
    return jnp.mean(a, axis=1, keepdims=True)


def _rowwise(name, body, rows, bcasts, out_rows, out_accs=(), tile=None):
    rows = [r if isinstance(r, tuple) else (r, r.shape[1], 0) for r in rows]
    s = rows[0][0].shape[0]
    if tile is None:
        per_row = sum(w * a.dtype.itemsize for a, w, _ in rows) + sum(w * jnp.dtype(dt).itemsize for w, dt in out_rows)
        tile = 1024
        while tile > 16 and (tile * per_row * 2 > ROW_TILE_BUDGET or s % tile):
            tile //= 2
        if s * per_row * 2 <= ROW_TILE_BUDGET:
            tile = s
    assert s % tile == 0
    n_in = len(rows) + len(bcasts)
    n_o = len(out_rows)

    def kern(*refs):
        res = body(*[r[...].astype(F32) for r in refs[:n_in]])
        outs = refs[n_in:]
        for o, v in zip(outs[:n_o], res[:n_o]):
            o[...] = v.astype(o.dtype)
        if out_accs:
            @pl.when(pl.program_id(0) == 0)
            def _():
                for o in outs[n_o:]:
                    o[...] = jnp.zeros_like(o)
            for o, v in zip(outs[n_o:], res[n_o:]):
                o[...] += v

    in_specs = [pl.BlockSpec((tile, w), functools.partial(lambda i, cb: (i, cb), cb=cb)) for _, w, cb in rows]
    in_specs += [pl.BlockSpec(b.shape, functools.partial(lambda i, nd: (0,) * nd, nd=b.ndim)) for b in bcasts]
    out_shape = [jax.ShapeDtypeStruct((s, w), dt) for w, dt in out_rows]
    out_shape += [jax.ShapeDtypeStruct((1, w), F32) for w in out_accs]
    out_specs = [pl.BlockSpec((tile, w), lambda i: (i, 0)) for w, _ in out_rows]
    out_specs += [pl.BlockSpec((1, w), lambda i: (0, 0)) for w in out_accs]
    return pl.pallas_call(
        kern, name=name, grid=(s // tile,), in_specs=in_specs, out_specs=out_specs, out_shape=out_shape,
        compiler_params=_params(("arbitrary",) if out_accs else ("parallel",)),
    )(*[a for a, _, _ in rows], *bcasts)


def _tile(n, pref):
    if n <= pref:
        return n
    t = (pref // LANES) * LANES
    while t >= LANES:
        if n % t == 0:
            return t
        t -= LANES
    return n


def _matmul(name, a, b, *, ta=False, tb=False, out_dtype=F32, bias=None, add=None, b_blocks=1, out_blocks=1,
            b_rows=None, out_rows=None, tm=1024, tn=1408, tk=2816):
    m, k = (a.shape[1], a.shape[0]) if ta else a.shape
    if b_blocks > 1:
        rows_b, cols_b = b.shape[1], b.shape[2] * b_blocks
    else:
        rows_b, cols_b = b.shape
    if b_rows is not None:
        rows_b = b_rows[1]
    n, kb = (rows_b, cols_b) if tb else (cols_b, rows_b)
    assert k == kb, (name, a.shape, b.shape)
    tm, tn, tk = _tile(m, tm), _tile(n, tn), _tile(k, tk)
    if b_blocks > 1:
        per = cols_b // b_blocks
        if tb:
            tk = _tile(per, tk)
        else:
            tn = _tile(per, tn)
    if out_blocks > 1:
        tn = _tile(n // out_blocks, tn)
    nk = k // tk
    grid = (m // tm, n // tn, nk)

    a_spec = pl.BlockSpec((tk, tm), lambda i, j, kk: (kk, i)) if ta else pl.BlockSpec((tm, tk), lambda i, j, kk: (i, kk))
    if b_blocks > 1:
        if tb:
            nb = per // tk
            b_spec = pl.BlockSpec((1, tn, tk), lambda i, j, kk: (kk // nb, j, kk % nb))
        else:
            nb = per // tn
            b_spec = pl.BlockSpec((1, tk, tn), lambda i, j, kk: (j // nb, kk, j % nb))
    elif b_rows is not None:
        first = b_rows[0]
        if tb:
            b_spec = pl.BlockSpec((pl.Element(tn), pl.Element(tk)),
                                  lambda i, j, kk: (pl.multiple_of(first + j * tn, LANES), kk * tk))
        else:
            b_spec = pl.BlockSpec((pl.Element(tk), pl.Element(tn)),
                                  lambda i, j, kk: (pl.multiple_of(first + kk * tk, LANES), j * tn))
    else:
        b_spec = pl.BlockSpec((tn, tk), lambda i, j, kk: (j, kk)) if tb else pl.BlockSpec((tk, tn), lambda i, j, kk: (kk, j))
    in_specs = [a_spec, b_spec]
    operands = [a, b]
    if bias is not None:
        in_specs.append(pl.BlockSpec((1, tn), lambda i, j, kk: (0, j)))
        operands.append(bias)
    if add is not None:
        in_specs.append(pl.BlockSpec((tm, tn), lambda i, j, kk: (i, j)))
        operands.append(add)
    aliases = {}
    if out_blocks > 1:
        nbo = (n // out_blocks) // tn
        out_spec = pl.BlockSpec((1, tm, tn), lambda i, j, kk: (j // nbo, i, j % nbo))
        out_shape = jax.ShapeDtypeStruct((out_blocks, m, n // out_blocks), out_dtype)
    elif out_rows is not None:
        first_out, total, previous = out_rows
        out_spec = pl.BlockSpec((pl.Element(tm), pl.Element(tn)),
                                lambda i, j, kk: (pl.multiple_of(first_out + i * tm, LANES), j * tn))
        out_shape = jax.ShapeDtypeStruct((total, n), out_dtype)
        if previous is not None:
            aliases = {len(operands): 0}
            in_specs.append(pl.BlockSpec(memory_space=pl.ANY))
            operands.append(previous)
    else:
        out_spec = pl.BlockSpec((tm, tn), lambda i, j, kk: (i, j))
        out_shape = jax.ShapeDtypeStruct((m, n), out_dtype)
    dims = (((0 if ta else 1,), (1 if tb else 0,)), ((), ()))
    has_bias, has_add, has_previous = bias is not None, add is not None, bool(aliases)

    def kern(*refs):
        a_ref, b_ref = refs[0], refs[1]
        pos = 2
        bias_ref = add_ref = None
        if has_bias:
            bias_ref = refs[pos]
            pos += 1
        if has_add:
            add_ref = refs[pos]
            pos += 1
        if has_previous:
            pos += 1
        o_ref = refs[pos]
        acc_ref = refs[pos + 1] if nk > 1 else None
        av = a_ref[...].astype(BF16)
        bv = (b_ref[0] if b_blocks > 1 else b_ref[...]).astype(BF16)
        p = lax.dot_general(av, bv, dims, preferred_element_type=F32)

        def finish(acc):
            if has_bias:
                acc = acc + bias_ref[...]
            if has_add:
                acc = acc + add_ref[...]
            if out_blocks > 1:
                o_ref[0] = acc.astype(o_ref.dtype)
            else:
                o_ref[...] = acc.astype(o_ref.dtype)

        if nk == 1:
            finish(p)
        else:
            kk = pl.program_id(2)

            @pl.when(kk == 0)
            def _():
                acc_ref[...] = p

            @pl.when(kk > 0)
            def _():
                acc_ref[...] += p

            @pl.when(kk == nk - 1)
            def _():
                finish(acc_ref[...])

    return pl.pallas_call(
        kern, name=name, grid=grid, in_specs=in_specs, out_specs=out_spec, out_shape=out_shape,
        input_output_aliases=aliases, scratch_shapes=[pltpu.VMEM((tm, tn), F32)] if nk > 1 else [],
        compiler_params=_params(("parallel", "parallel", "arbitrary")),
    )(*operands)


CONV_HALO = 16
CONV_ROWS = 256


def _taps(win, shifts, rows):
    n = win.shape[0]
    for j, s in enumerate(shifts):
        yield j, (pltpu.roll(win, (n - s) % n, axis=0) if s % n else win)[:rows]


def _dwconv_fwd(name, x, w, b, *, silu, out_dtype=F32):
    s, c = x.shape
    k = w.shape[0]
    pad = (k - 1) // 2
    assert pad <= CONV_HALO and c % LANES == 0
    t = min(CONV_ROWS, s)
    n_chunks = s // t
    fwd_shifts = [CONV_HALO - pad + j for j in range(k)]

    def kern(x_ref, w_ref, b_ref, o_ref, xp_ref):
        zeros = jnp.zeros((CONV_HALO, LANES), F32)
        xp_ref[0:CONV_HALO, :] = zeros
        xp_ref[CONV_HALO + s:CONV_HALO + s + CONV_HALO, :] = zeros
        xp_ref[CONV_HALO:CONV_HALO + s, :] = x_ref[...].astype(F32)
        bv = b_ref[...]

        def chunk(i, carry):
            base = pl.multiple_of(i * t, 16)
            win = xp_ref[pl.ds(base, t + 2 * CONV_HALO), :]
            acc = jnp.zeros((t, LANES), F32)
            for j, xs in _taps(win, fwd_shifts, t):
                acc = acc + xs * w_ref[pl.ds(j, 1), :]
            acc = acc + bv
            o_ref[pl.ds(base, t), :] = (_silu(acc) if silu else acc).astype(o_ref.dtype)
            return carry

        lax.fori_loop(0, n_chunks, chunk, 0)

    return pl.pallas_call(
        kern, name=name, grid=(c // LANES,),
        in_specs=[pl.BlockSpec((s, LANES), lambda i: (0, i)), pl.BlockSpec((k, LANES), lambda i: (0, i)),
                  pl.BlockSpec((1, LANES), lambda i: (0, i))],
        out_specs=pl.BlockSpec((s, LANES), lambda i: (0, i)),
        out_shape=jax.ShapeDtypeStruct((s, c), out_dtype),
        scratch_shapes=[pltpu.VMEM((s + 2 * CONV_HALO, LANES), F32)],
        compiler_params=_params(("parallel",)),
    )(x, w, b)


def _dwconv_bwd(name, x, w, b, dout, *, silu, dx_dtype=F32):
    s, c = x.shape
    k = w.shape[0]
    pad = (k - 1) // 2
    t = min(CONV_ROWS, s)
    n_chunks = s // t
    fwd_shifts = [CONV_HALO - pad + j for j in range(k)]
    bwd_shifts = [CONV_HALO + pad - j for j in range(k)]

    def kern(x_ref, w_ref, b_ref, do_ref, dx_ref, dw_ref, db_ref, xp_ref, dp_ref):
        zeros = jnp.zeros((CONV_HALO, LANES), F32)
        for ref in (xp_ref, dp_ref):
            ref[0:CONV_HALO, :] = zeros
            ref[CONV_HALO + s:CONV_HALO + s + CONV_HALO, :] = zeros
        xp_ref[CONV_HALO:CONV_HALO + s, :] = x_ref[...].astype(F32)
        bv = b_ref[...]
        dw_ref[...] = jnp.zeros_like(dw_ref)

        def pre_chunk(i, dbias):
            base = pl.multiple_of(i * t, 16)
            win = xp_ref[pl.ds(base, t + 2 * CONV_HALO), :]
            dpre = do_ref[pl.ds(base, t), :].astype(F32)
            if silu:
                acc = jnp.zeros((t, LANES), F32)
                for j, xs in _taps(win, fwd_shifts, t):
                    acc = acc + xs * w_ref[pl.ds(j, 1), :]
                dpre = dpre * _dsilu(acc + bv)
            dp_ref[pl.ds(base + CONV_HALO, t), :] = dpre
            for j, xs in _taps(win, fwd_shifts, t):
                dw_ref[pl.ds(j, 1), :] += _sum0(dpre * xs)
            return dbias + _sum0(dpre)

        db_ref[...] = lax.fori_loop(0, n_chunks, pre_chunk, jnp.zeros((1, LANES), F32))

        def dx_chunk(i, carry):
            base = pl.multiple_of(i * t, 16)
            win = dp_ref[pl.ds(base, t + 2 * CONV_HALO), :]
            acc = jnp.zeros((t, LANES), F32)
            for j, dps in _taps(win, bwd_shifts, t):
                acc = acc + dps * w_ref[pl.ds(j, 1), :]
            dx_ref[pl.ds(base, t), :] = acc.astype(dx_ref.dtype)
            return carry

        lax.fori_loop(0, n_chunks, dx_chunk, 0)

    col = lambda rows: pl.BlockSpec((rows, LANES), lambda i: (0, i))
    return pl.pallas_call(
        kern, name=name, grid=(c // LANES,),
        in_specs=[col(s), col(k), col(1), col(s)],
        out_specs=[col(s), col(k), col(1)],
        out_shape=[jax.ShapeDtypeStruct((s, c), dx_dtype), jax.ShapeDtypeStruct((k, c), F32),
                   jax.ShapeDtypeStruct((1, c), F32)],
        scratch_shapes=[pltpu.VMEM((s + 2 * CONV_HALO, LANES), F32), pltpu.VMEM((s + 2 * CONV_HALO, LANES), F32)],
        compiler_params=_params(("parallel",)),
    )(x, w, b, dout)


_NT =(((1,), (1,)), ((), ()))
_TN = (((0,), (0,)), ((), ()))


def _dot(a, b, dims=None):
    if dims is None:
        return jnp.dot(a, b, preferred_element_type=F32)
    return lax.dot_general(a, b, dims, preferred_element_type=F32)


HEAD_QUANTITIES = 4
GROUPS_PER_STEP = 2


def _scan_tables(n_heads):
    j_heads = n_heads // N_GROUPS
    used = 3 * HEAD_QUANTITIES * j_heads
    assert used <= LANES and j_heads % 2 == 0 and N_GROUPS % GROUPS_PER_STEP == 0
    gw = j_heads * HEAD_DIM
    r = jnp.arange(LANES)[:, None]

    def expand(quantity, width):
        head_of_lane = jnp.arange(j_heads * width)[None] // width
        return ((r // (3 * j_heads) == quantity) & (r % j_heads == head_of_lane) & (r < used)).astype(BF16)

    sel_cols = (jnp.arange(gw)[:, None] // HEAD_DIM == jnp.arange(LANES)[None]).astype(BF16)
    h2 = 2 * n_heads
    rows = jnp.arange(3 * HEAD_QUANTITIES * h2)
    head = rows % n_heads
    col = ((rows % h2) // n_heads * N_GROUPS + head // j_heads) * LANES + (rows // h2) * j_heads + head % j_heads
    route = (col[:, None] == jnp.arange(2 * N_GROUPS * LANES)[None]).astype(BF16)
    return {"ex_dt": expand(0, HEAD_DIM), "ex_gam": expand(1, CHUNK), "ex_din": expand(2, HEAD_DIM),
            "ex_dst": expand(3, HEAD_DIM), "sel_cols": sel_cols, "route": route}


def _chunk_decay(etot, n_heads, direction):
    j_heads = n_heads // N_GROUPS
    ed = etot[:, :, direction * n_heads:(direction + 1) * n_heads]
    per_group = jnp.pad(ed.reshape(ed.shape[0], 8, N_GROUPS, j_heads), ((0, 0), (0, 0), (0, 0), (0, LANES - j_heads)))
    return jnp.repeat(ed, HEAD_DIM, axis=2), per_group.reshape(ed.shape[0], 8, N_GROUPS * LANES)


def _scan_specs(reverse_order, direction, nc, j_heads, d_ssm):
    gps = GROUPS_PER_STEP
    gw = j_heads * HEAD_DIM
    b_off = d_ssm // (gps * D_STATE)
    c_off = b_off + N_GROUPS // gps
    d_off = direction * (N_GROUPS // gps)
    zz = (lambda z: nc - 1 - z) if reverse_order else (lambda z: z)
    const = lambda shape: pl.BlockSpec(shape, lambda g, z: (0,) * len(shape))
    return {
        "xs": pl.BlockSpec((CHUNK, gps * gw), lambda g, z: (zz(z), g)),
        "b": pl.BlockSpec((CHUNK, gps * D_STATE), lambda g, z: (zz(z), b_off + g)),
        "c": pl.BlockSpec((CHUNK, gps * D_STATE), lambda g, z: (zz(z), c_off + g)),
        "q": pl.BlockSpec((CHUNK, gps * LANES), lambda g, z: (zz(z), d_off + g)),
        "gam_t": pl.BlockSpec((gps * j_heads, CHUNK), lambda g, z: (d_off + g, zz(z))),
        "etot_x": pl.BlockSpec((1, 8, gps * gw), lambda g, z: (zz(z), 0, g)),
        "etot_g": pl.BlockSpec((1, 8, gps * LANES), lambda g, z: (zz(z), 0, g)),
        "state": pl.BlockSpec((gps, 1, D_STATE, gw), lambda g, z: (g, zz(z), 0, 0)),
        "grp": pl.BlockSpec((CHUNK, gps * D_STATE), lambda g, z: (zz(z), g)),
        "ex": const((LANES, gw)), "ex_gam": const((LANES, j_heads * CHUNK)), "sel": const((gw, LANES)),
    }


def _scan_masks(reverse):
    li = lax.broadcasted_iota(jnp.int32, (CHUNK, CHUNK), 0)
    si = lax.broadcasted_iota(jnp.int32, (CHUNK, CHUNK), 1)
    mask = (li <= si) if reverse else (li >= si)
    mask_t = (si <= li) if reverse else (si >= li)
    return li, si, mask, mask_t, si < HEAD_DIM


def _plan_phase(plan, phase, n_steps, ins, outs, sems):
    if plan is None or phase not in plan.phases:
        return
    g, z = pl.program_id(0), pl.program_id(1)
    when = {"start": (g == 0) & (z == 0), "middle": (g == n_steps[0] // 2) & (z == 0),
            "end": (g == n_steps[0] - 1) & (z == n_steps[1] - 1)}[phase]

    @pl.when(when)
    def _():
        plan.phases[phase](ins, outs, *sems)


def _plan_call_parts(plan):
    if plan is None:
        return [], [], [], [], [], ("parallel", "arbitrary")
    n_in, n_out = len(plan.operands), len(plan.out_shapes)
    return plan.operands, [ANY] * n_in, plan.out_shapes, [ANY] * n_out, plan.sems(), ("arbitrary", "arbitrary")


def _scan_fwd(name, xbc_c, q_all, gam_t, etot_x, tb, *, direction, d_ssm, plan=None):
    s = xbc_c.shape[0]
    nc = s // CHUNK
    j_heads = tb["ex_dt"].shape[1] // HEAD_DIM
    gw = j_heads * HEAD_DIM
    gps = GROUPS_PER_STEP
    reverse = direction == 1
    sp = _scan_specs(reverse, direction, nc, j_heads, d_ssm)
    p_ops, p_in_specs, p_out_shapes, p_out_specs, p_scratch, semantics = _plan_call_parts(plan)
    n_pi, n_po = len(p_ops), len(p_out_shapes)
    n_steps = (N_GROUPS // gps, nc)

    def kern(*refs):
        xs_ref, b_ref, c_ref, q_ref, gamt_ref, etx_ref, exdt_ref, exgam_ref, exdin_ref, exdst_ref = refs[:10]
        plan_ins = refs[10:10 + n_pi]
        y_ref, hs_ref = refs[10 + n_pi:12 + n_pi]
        plan_outs = refs[12 + n_pi:12 + n_pi + n_po]
        h_ref = refs[12 + n_pi + n_po]
        plan_sems = refs[13 + n_pi + n_po:]
        _plan_phase(plan, "start", n_steps, plan_ins, plan_outs, plan_sems)
        _plan_phase(plan, "middle", n_steps, plan_ins, plan_outs, plan_sems)

        @pl.when(pl.program_id(1) == 0)
        def _():
            h_ref[...] = jnp.zeros_like(h_ref)

        _, _, mask, _, lo = _scan_masks(reverse)
        for gi in range(gps):
            bb = b_ref[:, gi * D_STATE:(gi + 1) * D_STATE].astype(BF16)
            cb = c_ref[:, gi * D_STATE:(gi + 1) * D_STATE].astype(BF16)
            cbt = _dot(cb, bb, _NT)
            q = q_ref[:, gi * LANES:(gi + 1) * LANES]
            dtx, dinx, dstx = _dot(q, exdt_ref[...]), _dot(q, exdin_ref[...]), _dot(q, exdst_ref[...])
            gcol = _dot(q, exgam_ref[...])
            xdt = xs_ref[:, gi * gw:(gi + 1) * gw].astype(F32) * dtx
            ht = h_ref[gi]
            y_off = _dot(cb, ht.astype(BF16)) * dinx
            hs_ref[gi, 0] = ht
            for p in range(j_heads // 2):
                lanes = slice(p * CHUNK, (p + 1) * CHUNK)
                x2 = xdt[:, lanes]
                acc = y_off[:, lanes]
                for idx, j in enumerate((2 * p, 2 * p + 1)):
                    g_row = gamt_ref[pl.ds(gi * j_heads + j, 1), :]
                    decay = jnp.exp(jnp.where(mask, gcol[:, j * CHUNK:(j + 1) * CHUNK] - g_row, NEG))
                    x_head = jnp.where(lo if idx == 0 else jnp.logical_not(lo), x2, 0.0).astype(BF16)
                    acc = acc + _dot((cbt * decay).astype(BF16), x_head)
                y_ref[:, gi * gw + p * CHUNK:gi * gw + (p + 1) * CHUNK] = acc.astype(y_ref.dtype)
            h_ref[gi] = ht * etx_ref[0, 0:1, gi * gw:(gi + 1) * gw] + _dot(bb, (xdt * dstx).astype(BF16), _TN)
        _plan_phase(plan, "end", n_steps, plan_ins, plan_outs, plan_sems)

    res = pl.pallas_call(
        kern, name=name, grid=n_steps,
        in_specs=[sp["xs"], sp["b"], sp["c"], sp["q"], sp["gam_t"], sp["etot_x"], sp["ex"], sp["ex_gam"], sp["ex"],
                  sp["ex"]] + p_in_specs,
        out_specs=[sp["xs"], sp["state"]] + p_out_specs,
        out_shape=[jax.ShapeDtypeStruct((s, d_ssm), BF16),
                   jax.ShapeDtypeStruct((N_GROUPS, nc, D_STATE, gw), F32)] + p_out_shapes,
        scratch_shapes=[pltpu.VMEM((gps, D_STATE, gw), F32)] + p_scratch,
        compiler_params=_params(semantics),
    )(xbc_c, xbc_c, xbc_c, q_all, gam_t, etot_x, tb["ex_dt"], tb["ex_gam"], tb["ex_din"], tb["ex_dst"], *p_ops)
    return res[0], res[1], list(res[2:])


def _scan_bwd(name, xbc_c, dy, hs, q_all, gam_t, etot_x, etot_g, tb, *, direction, d_ssm, plan=None):
    s = xbc_c.shape[0]
    nc = s // CHUNK
    j_heads = tb["ex_dt"].shape[1] // HEAD_DIM
    gw = j_heads * HEAD_DIM
    gps = GROUPS_PER_STEP
    reverse = direction == 1
    sp = _scan_specs(not reverse, direction, nc, j_heads, d_ssm)
    hp = lax.Precision.HIGHEST
    p_ops, p_in_specs, p_out_shapes, p_out_specs, p_scratch, semantics = _plan_call_parts(plan)
    n_pi, n_po = len(p_ops), len(p_out_shapes)
    n_steps = (N_GROUPS // gps, nc)

    def kern(*refs):
        (xs_ref, b_ref, c_ref, dy_ref, hs_ref, q_ref, gamt_ref, etg_ref, etx_ref, exdt_ref, exgam_ref, exdin_ref,
         exdst_ref, sel_ref) = refs[:14]
        plan_ins = refs[14:14 + n_pi]
        dxs_ref, db_ref, dc_ref, ddt_ref, da_ref = refs[14 + n_pi:19 + n_pi]
        plan_outs = refs[19 + n_pi:19 + n_pi + n_po]
        dh_ref, tmp_ref = refs[19 + n_pi + n_po:21 + n_pi + n_po]
        plan_sems = refs[21 + n_pi + n_po:]
        _plan_phase(plan, "start", n_steps, plan_ins, plan_outs, plan_sems)
        _plan_phase(plan, "middle", n_steps, plan_ins, plan_outs, plan_sems)

        @pl.when(pl.program_id(1) == 0)
        def _():
            dh_ref[...] = jnp.zeros_like(dh_ref)

        li, si, mask, mask_t, lo = _scan_masks(reverse)
        sel = sel_ref[...]
        incl = ((si <= li) if reverse else (si >= li)).astype(F32)
        excl = ((si > li) if reverse else (si < li)).astype(F32)
        for gi in range(gps):
            grp_lanes = slice(gi * D_STATE, (gi + 1) * D_STATE)
            bb = b_ref[:, grp_lanes].astype(BF16)
            cb = c_ref[:, grp_lanes].astype(BF16)
            cbt = _dot(cb, bb, _NT)
            cbt_t = _dot(bb, cb, _NT)
            q = q_ref[:, gi * LANES:(gi + 1) * LANES]
            dtx, dinx, dstx = _dot(q, exdt_ref[...]), _dot(q, exdin_ref[...]), _dot(q, exdst_ref[...])
            gcol = _dot(q, exgam_ref[...])
            x_all = xs_ref[:, gi * gw:(gi + 1) * gw].astype(F32)
            dy_all = dy_ref[:, gi * gw:(gi + 1) * gw].astype(F32)
            xdt = x_all * dtx
            xb = xdt.astype(BF16)
            ht = hs_ref[gi, 0]
            hb = ht.astype(BF16)
            dht = dh_ref[gi]
            dhb = dht.astype(BF16)
            y_off = _dot(cb, hb) * dinx
            dx_off = _dot(bb, dhb) * dstx
            dyd = (dy_all * dinx).astype(BF16)
            xd = (xdt * dstx).astype(BF16)
            dc_acc = _dot(dyd, hb, _NT)
            db_acc = _dot(xd, dhb, _NT)
            dh_ref[gi] = dht * etx_ref[0, 0:1, gi * gw:(gi + 1) * gw] + _dot(cb, dyd, _TN)
            q_cols = _dot((dy_all * y_off).astype(BF16), sel)
            c_cols = _dot((xdt * dx_off).astype(BF16), sel)
            through = _sum0(_dot((dht * ht).astype(BF16), sel))
            dcbt = jnp.zeros((CHUNK, CHUNK), F32)
            for p in range(j_heads // 2):
                lanes = slice(p * CHUNK, (p + 1) * CHUNK)
                out_lanes = slice(gi * gw + p * CHUNK, gi * gw + (p + 1) * CHUNK)
                x2b = xb[:, lanes]
                dy2 = dy_all[:, lanes]
                acc = dx_off[:, lanes]
                for idx, j in enumerate((2 * p, 2 * p + 1)):
                    gc = gcol[:, j * CHUNK:(j + 1) * CHUNK]
                    gr = gamt_ref[pl.ds(gi * j_heads + j, 1), :]
                    decay = jnp.exp(jnp.where(mask, gc - gr, NEG))
                    decay_t = jnp.exp(jnp.where(mask_t, gr - gc, NEG))
                    dy_head = jnp.where(lo if idx == 0 else jnp.logical_not(lo), dy2, 0.0).astype(BF16)
                    acc = acc + _dot((cbt_t * decay_t).astype(BF16), dy_head)
                    dm = decay * _dot(dy_head, x2b, _NT)
                    dcbt = dcbt + dm
                    e = (cbt * dm).astype(BF16)
                    in_lane_j = si == j
                    q_cols = (q_cols + jnp.where(in_lane_j, jnp.sum(e.astype(F32), axis=1, keepdims=True), 0.0)
                              - _dot(e, jnp.where(in_lane_j, 1.0, 0.0).astype(BF16), _TN))
                dxs_ref[:, out_lanes] = (acc * dtx[:, lanes]).astype(dxs_ref.dtype)
                tmp_ref[:, lanes] = acc * x_all[:, lanes]
            dcb = dcbt.astype(BF16)
            dc_ref[:, grp_lanes] = (dc_acc + _dot(dcb, bb)).astype(dc_ref.dtype)
            db_ref[:, grp_lanes] = (db_acc + _dot(dcb, cb, _TN)).astype(db_ref.dtype)
            ddt_ref[:, gi * LANES:(gi + 1) * LANES] = _dot(tmp_ref[...].astype(BF16), sel)
            da_ref[:, gi * LANES:(gi + 1) * LANES] = (
                jnp.dot(incl, q_cols, preferred_element_type=F32, precision=hp)
                + jnp.dot(excl, c_cols, preferred_element_type=F32, precision=hp)
                + through * etg_ref[0, 0:1, gi * LANES:(gi + 1) * LANES])
        _plan_phase(plan, "end", n_steps, plan_ins, plan_outs, plan_sems)

    gn = N_GROUPS * D_STATE
    res = pl.pallas_call(
        kern, name=name, grid=n_steps,
        in_specs=[sp["xs"], sp["b"], sp["c"], sp["xs"], sp["state"], sp["q"], sp["gam_t"], sp["etot_g"], sp["etot_x"],
                  sp["ex"], sp["ex_gam"], sp["ex"], sp["ex"], sp["sel"]] + p_in_specs,
        out_specs=[sp["xs"], sp["grp"], sp["grp"], sp["grp"], sp["grp"]] + p_out_specs,
        out_shape=[jax.ShapeDtypeStruct((s, d_ssm), BF16), jax.ShapeDtypeStruct((s, gn), BF16),
                   jax.ShapeDtypeStruct((s, gn), BF16), jax.ShapeDtypeStruct((s, gn), F32),
                   jax.ShapeDtypeStruct((s, gn), F32)] + p_out_shapes,
        scratch_shapes=[pltpu.VMEM((gps, D_STATE, gw), F32), pltpu.VMEM((CHUNK, gw), F32)] + p_scratch,
        compiler_params=_params(semantics),
    )(xbc_c, xbc_c, xbc_c, dy, hs, q_all, gam_t, etot_g, etot_x, tb["ex_dt"], tb["ex_gam"], tb["ex_din"], tb["ex_dst"],
      tb["sel_cols"], *p_ops)
    return res[:5], list(res[5:])


def _dt_prepare(dt_raw, dt_bias, a_neg, route):
    s, h2 = dt_raw.shape
    n_heads = h2 // 2
    qw = route.shape[1]

    def kern(raw_ref, bias_ref, a_ref, route_ref, dt_ref, q_ref, gamt_ref, etot_ref):
        dt = _softplus(raw_ref[...] + bias_ref[...])
        a = dt * a_ref[...]
        li = lax.broadcasted_iota(jnp.int32, (CHUNK, CHUNK), 0)
        si = lax.broadcasted_iota(jnp.int32, (CHUNK, CHUNK), 1)
        tri = (li >= si).astype(F32)
        cs = jnp.dot(tri, a, preferred_element_type=F32, precision=lax.Precision.HIGHEST)
        tot = _sum0(a)
        fwd = lax.broadcasted_iota(jnp.int32, (CHUNK, h2), 1) < n_heads
        gam = jnp.where(fwd, cs, a - cs)
        din = jnp.where(fwd, jnp.exp(cs), jnp.exp(tot + gam))
        dst = jnp.where(fwd, jnp.exp(tot - cs), jnp.exp(cs - a))
        pieces = []
        for v in (dt, gam, din, dst):
            hi = v.astype(BF16)
            rest = v - hi.astype(F32)
            mid = rest.astype(BF16)
            pieces += [hi, mid, (rest - mid.astype(F32)).astype(BF16)]
        q_ref[...] = _dot(jnp.concatenate(pieces, axis=1), route_ref[...]).astype(BF16)
        dt_ref[...] = dt
        gamt_ref[...] = gam.T
        etot_ref[...] = jnp.broadcast_to(jnp.exp(tot), (8, h2))

    nc = s // CHUNK
    rows = lambda w: pl.BlockSpec((CHUNK, w), lambda i: (i, 0))
    whole = lambda a: pl.BlockSpec(a.shape, lambda i: (0, 0))
    return pl.pallas_call(
        kern, name="dt_prepare", grid=(nc,),
        in_specs=[rows(h2), whole(dt_bias), whole(a_neg), whole(route)],
        out_specs=[rows(h2), rows(qw), pl.BlockSpec((h2, CHUNK), lambda i: (0, i)), pl.BlockSpec((8, h2), lambda i: (i, 0))],
        out_shape=[jax.ShapeDtypeStruct((s, h2), F32), jax.ShapeDtypeStruct((s, qw), BF16),
                   jax.ShapeDtypeStruct((h2, s), F32), jax.ShapeDtypeStruct((nc * 8, h2), F32)],
        compiler_params=_params(("parallel",)),
    )(dt_raw, dt_bias, a_neg, route)


def _from_group_lanes(arr, j_heads):
    s = arr.shape[0]
    return arr.reshape(s, N_GROUPS, LANES)[:, :, :j_heads].reshape(s, N_GROUPS * j_heads)


def _dt_backward(da, ddt, dt, dt_raw, dt_bias, a_neg):
    h2 = da.shape[1]

    def body(dav, ddtv, dtv, raw, bias, a_head):
        draw = (ddtv + dav * a_head) * _sigmoid(raw + bias)
        return draw, _sum0(draw), _sum0(dav * dtv) * a_head
    return _rowwise("dt_backward", body, [da, ddt, dt, dt_raw], [dt_bias, a_neg], [(h2, BF16)], [h2, h2])


def _rms(x):
    r = lax.rsqrt(_mean1(x * x) + EPS)
    return x * r, r


def _rms_bwd(dy, y, r):
    return r * (dy - y * _mean1(dy * y))


def _norm_mod_fwd(name, x, g, sc, sh):
    def body(xv, gv, scv, shv):
        y, _ = _rms(xv)
        return ((y * gv) * (1.0 + scv) + shv,)
    return _rowwise(name, body, [x], [g, sc, sh], [(x.shape[1], BF16)])[0]


def _norm_mod_bwd(name, x, dh, dpass, g, sc):
    d = x.shape[1]

    def body(xv, dhv, dpv, gv, scv):
        y, r = _rms(xv)
        dn = dhv * (1.0 + scv)
        dx = _rms_bwd(dn * gv, y, r) + dpv
        return dx, _sum0(dn * y), _sum0(dhv * (y * gv)), _sum0(dhv)
    return _rowwise(name, body, [x, dh, dpass], [g, sc], [(d, F32)], [d, d, d])


def _gated_residual_fwd(name, x, m, gate, gp):
    def body(xv, mv, gatev, gpv):
        y, _ = _rms(mv)
        return (xv + gatev * (y * gpv),)
    return _rowwise(name, body, [x, m], [gate, gp], [(x.shape[1], F32)])[0]


def _gated_residual_bwd(name, m, dx1, gate, gp):
    d = m.shape[1]

    def body(mv, dv, gatev, gpv):
        y, r = _rms(mv)
        dn = dv * gatev
        return _rms_bwd(dn * gpv, y, r), _sum0(dv * (y * gpv)), _sum0(dn * y)
    return _rowwise(name, body, [m, dx1], [gate, gp], [(d, BF16)], [d, d])


def _final_residual_loss(x1, f, tgt, gate, gp):
    d = x1.shape[1]

    def body(xv, fv, tv, gatev, gpv):
        y, r = _rms(fv)
        n = y * gpv
        err = xv + gatev * n - tv
        dx2 = err * (1.0 / d)
        dn = dx2 * gatev
        sq = jnp.sum(_sum0(err * err), axis=1, keepdims=True)
        return dx2, _rms_bwd(dn * gpv, y, r), jnp.broadcast_to(sq, (1, LANES)), _sum0(dx2 * n), _sum0(dn * y)
    return _rowwise("final_residual_loss", body, [x1, f, tgt], [gate, gp], [(d, F32), (d, BF16)], [LANES, d, d])


def _swiglu_fwd(gu):
    f = gu.shape[1] // 2

    def body(v):
        return (_silu(v[:, :f]) * v[:, f:],)
    return _rowwise("swiglu_fwd", body, [gu], [], [(f, BF16)])[0]


def _swiglu_bwd(gu, dact):
    f = gu.shape[1] // 2

    def body(v, dv):
        gt, up = v[:, :f], v[:, f:]
        return (jnp.concatenate([dv * up * _dsilu(gt), dv * _silu(gt)], axis=1),)
    return _rowwise("swiglu_bwd", body, [gu, dact], [], [(2 * f, BF16)])[0]


def _glu_fwd(glu_in, b_glu):
    c = glu_in.shape[1] // 2

    def body(v, bv):
        t = v + bv
        return (t[:, :c] * _sigmoid(t[:, c:]),)
    return _rowwise("glu_fwd", body, [glu_in], [b_glu], [(c, F32)])[0]


def _glu_bwd(glu_in, b_glu, du0):
    c = glu_in.shape[1] // 2

    def body(v, dv, bv):
        t = v + bv
        a, s = t[:, :c], _sigmoid(t[:, c:])
        dg = jnp.concatenate([dv * s, dv * a * s * (1.0 - s)], axis=1)
        return dg, _sum0(dg)
    return _rowwise("glu_bwd", body, [glu_in, du0], [b_glu], [(2 * c, BF16)], [2 * c])


def _ln_parts(u1):
    xc = u1 - _mean1(u1)
    r = lax.rsqrt(_mean1(xc * xc) + EPS)
    return xc * r, r


def _ln_silu_fwd(u1, ln_g, ln_b):
    def body(v, gv, bv):
        yh, _ = _ln_parts(v)
        return (_silu(yh * gv + bv),)
    return _rowwise("ln_silu_fwd", body, [u1], [ln_g, ln_b], [(u1.shape[1], BF16)])[0]


def _ln_silu_bwd(u1, du, ln_g, ln_b):
    d = u1.shape[1]

    def body(v, dv, gv, bv):
        yh, r = _ln_parts(v)
        dl = dv * _dsilu(yh * gv + bv)
        dyh = dl * gv
        du1 = r * (dyh - _mean1(dyh) - yh * _mean1(dyh * yh))
        return du1, _sum0(dl * yh), _sum0(dl)
    return _rowwise("ln_silu_bwd", body, [u1, du], [ln_g, ln_b], [(d, F32)], [d, d])


def _gate_merge_fwd(y_a, y_b, gl, b_gate):
    d = y_a.shape[1]

    def body(ya, yb, glv, bv):
        s = _sigmoid(glv + bv)
        return (s[:, :d] * ya + s[:, d:] * yb,)
    return _rowwise("gate_merge_fwd", body, [y_a, y_b, gl], [b_gate], [(d, BF16)])[0]


def _gate_merge_bwd(dmixin, y_a, y_b, gl, b_gate):
    d = y_a.shape[1]

    def body(dv, ya, yb, glv, bv):
        s = _sigmoid(glv + bv)
        sa, sb = s[:, :d], s[:, d:]
        dya, dyb = dv * sa, dv * sb
        dgl = jnp.concatenate([dv * ya * sa * (1.0 - sa), dv * yb * sb * (1.0 - sb)], axis=1)
        return dya, dyb, dgl, _sum0(dgl), _sum0(dyb)
    return _rowwise("gate_merge_bwd", body, [dmixin, y_a, y_b, gl], [b_gate],
                    [(d, BF16), (d, BF16), (2 * d, BF16)], [2 * d, d])


def _group_slices(d_ssm):
    gw = d_ssm // N_GROUPS
    return [slice(g * gw, (g + 1) * gw) for g in range(N_GROUPS)]


def _gated_norm_fwd(y_f, y_b, xbc_c, z, d_skip_x, g_ssm):
    d_ssm = y_f.shape[1]

    def body(yf, yb, xs, zv, dsk, gv):
        y = yf + yb + dsk * xs
        v = y * _silu(zv)
        outs = []
        for sl in _group_slices(d_ssm):
            w, _ = _rms(v[:, sl])
            outs.append(w)
        return y, jnp.concatenate(outs, axis=1) * gv
    return _rowwise("gated_norm_fwd", body, [y_f, y_b, (xbc_c, d_ssm, 0), z], [d_skip_x, g_ssm],
                    [(d_ssm, F32), (d_ssm, BF16)])


def _gated_norm_bwd(y, z, dyn, xbc_c, d_skip_x, g_ssm):
    d_ssm = y.shape[1]

    def body(yv, zv, dv, xs, dsk, gv):
        sz = _silu(zv)
        v = yv * sz
        dw = dv * gv
        dvs, ws = [], []
        for sl in _group_slices(d_ssm):
            w, r = _rms(v[:, sl])
            ws.append(w)
            dvs.append(_rms_bwd(dw[:, sl], w, r))
        dvv = jnp.concatenate(dvs, axis=1)
        dy = dvv * sz
        return dy, dvv * yv * _dsilu(zv), _sum0(dv * jnp.concatenate(ws, axis=1)), _sum0(dy * xs)
    return _rowwise("gated_norm_bwd", body, [y, z, dyn, (xbc_c, d_ssm, 0)], [d_skip_x, g_ssm],
                    [(d_ssm, BF16), (d_ssm, BF16)], [d_ssm, d_ssm])


def _ssd_grad_merge(dxs_f, dxs_b, dy, db_f, db_b, dc_f, dc_b, d_skip_x):
    d_ssm = dy.shape[1]
    width = d_ssm + 2 * N_GROUPS * D_STATE

    def body(xf, xb, dv, bf, bb, cf, cbv, dsk):
        return (jnp.concatenate([xf + xb + dsk * dv, bf + bb, cf + cbv], axis=1),)
    return _rowwise("ssd_grad_merge", body, [dxs_f, dxs_b, dy, db_f, db_b, dc_f, dc_b], [d_skip_x], [(width, BF16)])[0]


def _adamw(name, w, g, m, v):
    c = w.shape[1]
    c1 = 1.0 - ADAM_B1 ** ADAM_STEP
    c2 = 1.0 - ADAM_B2 ** ADAM_STEP

    def body(wv, gv, mv, vv):
        mn = ADAM_B1 * mv + (1.0 - ADAM_B1) * gv
        vn = ADAM_B2 * vv + (1.0 - ADAM_B2) * (gv * gv)
        delta = -ADAM_LR * ((mn / c1) / (jnp.sqrt(vn / c2) + ADAM_EPS) + ADAM_WD * wv)
        return delta, mn, vn
    return _rowwise(name, body, [w, g, m, v], [], [(c, F32)] * 3)


def _local_step(x, tgt, mod, wts, sm, late=None):
    s, d = x.shape
    d_ssm = 2 * d
    n_heads = d_ssm // HEAD_DIM
    d_xbc = d_ssm + 2 * N_GROUPS * D_STATE
    sec = [0, d_ssm, d_ssm + d_xbc, d_ssm + d_xbc + 2 * n_heads, d_ssm + d_xbc + 2 * n_heads + 2 * d]
    sec.append(sec[-1] + 2 * d)
    sh1, sc1, g1, sh2, sc2, g2 = [mod[:, i * d:(i + 1) * d] for i in range(N_MOD)]
    win_t = wts["w_in_t"]
    sections = [(nm, (sec[i], sec[i + 1] - sec[i])) for i, nm in enumerate(("z", "xbc", "dt", "glu", "gate"))]

    h1 = _norm_mod_fwd("pre_mix_norm", x, sm["g_pre_mix"], sc1, sh1)
    z, xbc, dt_raw, glu_in, gate_l = [
        _matmul(f"proj_{nm}", h1, win_t, tb=True, b_rows=rows, out_dtype=F32 if nm == "dt" else BF16)
        for nm, rows in sections]
    xbc_c = _dwconv_fwd("ssm_conv_fwd", xbc, sm["w_conv_ssm"], sm["b_conv_ssm"], silu=True, out_dtype=BF16)
    tables = _scan_tables(n_heads)
    dt, q_all, gam_t, etot = _dt_prepare(dt_raw, sm["dt_bias"], sm["a_neg"], tables["route"])
    etot = etot.reshape(s // CHUNK, 8, 2 * n_heads)
    (etx_f, etg_f), (etx_r, etg_r) = [_chunk_decay(etot, n_heads, direction) for direction in (0, 1)]
    plan_f = plan_r = None
    if late is not None:
        plan_f = _gather_plan([late[n] for n in MATRICES[1:4] + MATRICES[5:]])
        plan_r = _gather_plan([late["w_gate_up"]])
    y_f, hs_f, got_f = _scan_fwd("ssd_fwd_f", xbc_c, q_all, gam_t, etx_f, tables, direction=0, d_ssm=d_ssm, plan=plan_f)
    y_r, hs_r, got_r = _scan_fwd("ssd_fwd_r", xbc_c, q_all, gam_t, etx_r, tables, direction=1, d_ssm=d_ssm, plan=plan_r)
    if late is not None:
        wts = dict(wts, w_ssm_out=got_f[0].reshape(-1, d), w_conv_out=got_f[1].reshape(-1, d),
                   w_mix_out=got_f[2].reshape(-1, d), w_down=got_f[3].reshape(-1, d), w_gate_up=got_r[0])
    y_ssd, yn = _gated_norm_fwd(y_f, y_r, xbc_c, z, sm["d_skip_x"], sm["g_ssm_norm"])
    y_a = _matmul("ssm_out", yn, wts["w_ssm_out"])
    u0 = _glu_fwd(glu_in, sm["b_glu"])
    u1 = _dwconv_fwd("dw_conv_fwd", u0, sm["w_dw"], sm["b_dw"], silu=False)
    u = _ln_silu_fwd(u1, sm["ln_g"], sm["ln_b"])
    y_b = _matmul("conv_out", u, wts["w_conv_out"], bias=sm["b_conv_out"])
    mixin = _gate_merge_fwd(y_a, y_b, gate_l, sm["b_gate"])
    mix = _matmul("mix_out", mixin, wts["w_mix_out"])
    x1 = _gated_residual_fwd("post_mix_residual", x, mix, g1, sm["g_post_mix"])
    h2 = _norm_mod_fwd("pre_ffn_norm", x1, sm["g_pre_ffn"], sc2, sh2)
    gu = _matmul("ffn_gate_up", h2, wts["w_gate_up"], b_blocks=N_CHIPS, out_dtype=BF16)
    act = _swiglu_fwd(gu)
    f = _matmul("ffn_down", act, wts["w_down"])

    dx2, df, sq, d_g2, d_gpf = _final_residual_loss(x1, f, tgt, g2, sm["g_post_ffn"])
    dact = _matmul("d_act", df, wts["w_down"], tb=True, out_dtype=BF16)
    g_w_down = _matmul("g_w_down", act, df, ta=True, out_dtype=BF16)
    dgu = _swiglu_bwd(gu, dact)
    dh2 = _matmul("d_h2", dgu, wts["w_gate_up"], tb=True, b_blocks=N_CHIPS)
    g_w_gate_up = _matmul("g_w_gate_up", h2, dgu, ta=True, out_dtype=BF16, out_blocks=N_CHIPS)
    dx1, d_gpre_ffn, d_sc2, d_sh2 = _norm_mod_bwd("pre_ffn_norm_bwd", x1, dh2, dx2, sm["g_pre_ffn"], sc2)
    dmix, d_g1, d_gpm = _gated_residual_bwd("post_mix_residual_bwd", mix, dx1, g1, sm["g_post_mix"])
    dmixin = _matmul("d_mixin", dmix, wts["w_mix_out"], tb=True, out_dtype=BF16)
    g_w_mix = _matmul("g_w_mix_out", mixin, dmix, ta=True, out_dtype=BF16)
    dy_a, dy_b, dgate_l, d_bgate, d_bco = _gate_merge_bwd(dmixin, y_a, y_b, gate_l, sm["b_gate"])
    du = _matmul("d_u", dy_b, wts["w_conv_out"], tb=True, out_dtype=BF16)
    g_w_co = _matmul("g_w_conv_out", u, dy_b, ta=True, out_dtype=BF16)
    du1, d_lng, d_lnb = _ln_silu_bwd(u1, du, sm["ln_g"], sm["ln_b"])
    du0, d_wdw, d_bdw = _dwconv_bwd("dw_conv_bwd", u0, sm["w_dw"], sm["b_dw"], du1, silu=False)
    dglu, d_bglu = _glu_bwd(glu_in, sm["b_glu"], du0)
    dyn = _matmul("d_yn", dy_a, wts["w_ssm_out"], tb=True, out_dtype=BF16)
    g_w_ssm = _matmul("g_w_ssm_out", yn, dy_a, ta=True, out_dtype=BF16)
    dy_ssd, dz, d_gssm, d_dskip_x = _gated_norm_bwd(y_ssd, z, dyn, xbc_c, sm["d_skip_x"], sm["g_ssm_norm"])
    early = [g_w_ssm.reshape(N_CHIPS, -1, d), g_w_co.reshape(N_CHIPS, -1, d), g_w_mix.reshape(N_CHIPS, -1, d),
             g_w_gate_up, g_w_down.reshape(N_CHIPS, -1, d)]
    plan_b = sums = None
    if late is not None:
        sums = _chip_sums("early", early)
        plan_b = _send_chips_plan(sums)
    (dxs_f, db_f, dc_f, ddt_f, da_f), received = _scan_bwd(
        "ssd_bwd_f", xbc_c, dy_ssd, hs_f, q_all, gam_t, etx_f, etg_f, tables, direction=0, d_ssm=d_ssm, plan=plan_b)
    (dxs_r, db_r, dc_r, ddt_r, da_r), _ = _scan_bwd(
        "ssd_bwd_r", xbc_c, dy_ssd, hs_r, q_all, gam_t, etx_r, etg_r, tables, direction=1, d_ssm=d_ssm)
    j_heads = n_heads // N_GROUPS
    da = jnp.concatenate([_from_group_lanes(da_f, j_heads), _from_group_lanes(da_r, j_heads)], axis=1)
    ddt = jnp.concatenate([_from_group_lanes(ddt_f, j_heads), _from_group_lanes(ddt_r, j_heads)], axis=1)
    ddt_raw, d_dtbias, d_alog = _dt_backward(da, ddt, dt, dt_raw, sm["dt_bias"], sm["a_neg"])
    dxbc_c = _ssd_grad_merge(dxs_f, dxs_r, dy_ssd, db_f, db_r, dc_f, dc_r, sm["d_skip_x"])
    dxbc, d_wconv, d_bconv = _dwconv_bwd("ssm_conv_bwd", xbc, sm["w_conv_ssm"], sm["b_conv_ssm"], dxbc_c,
                                         silu=True, dx_dtype=BF16)
    dsecs = [dz, dxbc, ddt_raw, dglu, dgate_l]
    dh1 = g_win = None
    for (nm, rows), dsec in zip(sections, dsecs):
        dh1 = _matmul(f"d_h1_{nm}", dsec, win_t, b_rows=rows, add=dh1)
        g_win = _matmul(f"g_w_in_{nm}", dsec, h1, ta=True, out_dtype=BF16, out_rows=(rows[0], sec[-1], g_win))
    grad_x, d_gpre_mix, d_sc1, d_sh1 = _norm_mod_bwd("pre_mix_norm_bwd", x, dh1, dx1, sm["g_pre_mix"], sc1)

    dmod = jnp.concatenate([d_sh1, d_sc1, d_g1, d_sh2, d_sc2, d_g2], axis=1)
    big = {"w_in_t": g_win}
    if late is None:
        big.update(w_ssm_out=g_w_ssm, w_conv_out=g_w_co, w_mix_out=g_w_mix, w_gate_up=g_w_gate_up, w_down=g_w_down)
    else:
        big["pending"] = (sums, received)
    small = {"g_pre_mix": d_gpre_mix, "g_post_mix": d_gpm, "w_conv_ssm": d_wconv, "b_conv_ssm": d_bconv,
             "dt_bias": d_dtbias, "a_log": d_alog, "d_skip_x": d_dskip_x, "g_ssm_norm": d_gssm, "b_glu": d_bglu,
             "w_dw": d_wdw, "b_dw": d_bdw, "ln_g": d_lng, "ln_b": d_lnb, "b_conv_out": d_bco, "b_gate": d_bgate,
             "g_pre_ffn": d_gpre_ffn, "g_post_ffn": d_gpf}
    return sq, grad_x, big, small, dmod


ANY = pl.BlockSpec(memory_space=pl.ANY)
WHOLE_VMEM = pl.BlockSpec(memory_space=pltpu.VMEM)


def _mesh_place():
    x, y, c = lax.axis_index("x"), lax.axis_index("y"), lax.axis_index("c")
    other_chips = [(1 - x, y), (x, 1 - y), (1 - x, 1 - y)]
    return x, y, c, other_chips


def _remote(src, dst, send_sems, recv_sems, k, device):
    return pltpu.make_async_remote_copy(src_ref=src, dst_ref=dst, send_sem=send_sems.at[k], recv_sem=recv_sems.at[k],
                                        device_id=device, device_id_type=MESH)


def _gather_devices(name, block):
    m_per, n = block.shape

    def body(x_ref, out_ref, send_sems, recv_sems, local_sem):
        x, y, c, chips = _mesh_place()
        me, sibling = (x, y, c), (x, y, 1 - c)

        def rows(px, py, pc):
            return out_ref.at[pl.ds((4 * px + 2 * py + pc) * m_per, m_per), :]

        def copy(k, blk, to, src=None):
            return _remote(rows(*blk) if src is None else src, rows(*blk), send_sems, recv_sems, k, to)

        mine = pltpu.make_async_copy(x_ref, rows(*me), local_sem)
        mine.start()
        first = [copy(0, me, sibling, src=x_ref)]
        first += [copy(1 + j, me, (*chip, c), src=x_ref) for j, chip in enumerate(chips)]
        for cp in first:
            cp.start()
        passed = [copy(4 + j, (*chip, c), sibling) for j, chip in enumerate(chips)]
        for j, chip in enumerate(chips):
            copy(1 + j, (*chip, c), me).wait_recv()
            passed[j].start()
        copy(0, sibling, me).wait_recv()
        for j, chip in enumerate(chips):
            copy(4 + j, (*chip, 1 - c), me).wait_recv()
        for cp in first + passed:
            cp.wait_send()
        mine.wait()

    return pl.pallas_call(
        body, name=name, out_shape=jax.ShapeDtypeStruct((N_DEV * m_per, n), block.dtype),
        in_specs=[WHOLE_VMEM], out_specs=WHOLE_VMEM,
        scratch_shapes=[pltpu.SemaphoreType.DMA((7,)), pltpu.SemaphoreType.DMA((7,)), pltpu.SemaphoreType.DMA],
        compiler_params=pltpu.CompilerParams(vmem_limit_bytes=VMEM_LIMIT),
    )(block)


class _Plan:
    def __init__(self, operands, out_shapes, copies, phases):
        self.operands, self.out_shapes, self.copies, self.phases = list(operands), list(out_shapes), copies, phases

    def sems(self):
        return [pltpu.SemaphoreType.DMA((self.copies,)), pltpu.SemaphoreType.DMA((self.copies,))]


def _run_plan(name, plan):
    n_in, n_out = len(plan.operands), len(plan.out_shapes)

    def body(*refs):
        ins, outs = refs[:n_in], refs[n_in:n_in + n_out]
        send_sems, recv_sems = refs[n_in + n_out:]
        for phase in ("start", "middle", "end"):
            if phase in plan.phases:
                plan.phases[phase](ins, outs, send_sems, recv_sems)

    return list(pl.pallas_call(body, name=name, out_shape=plan.out_shapes, in_specs=[ANY] * n_in,
                               out_specs=[ANY] * n_out, scratch_shapes=plan.sems())(*plan.operands))


def _gather_plan(shards):
    n = len(shards)

    def copies(kinds, ins, outs, send_sems, recv_sems):
        x, y, c, chips = _mesh_place()
        me = 2 * x + y
        sibling = (x, y, 1 - c)

        def half(i, h):
            hr = ins[i].shape[0] // 2
            return pl.ds(h * hr, hr)

        def block(i, j, h):
            cx, cy = chips[j]
            return outs[i].at[2 * cx + cy, half(i, h)]

        make = {
            "over_ici": lambda i, j: _remote(ins[i].at[half(i, c)], outs[i].at[me, half(i, c)], send_sems, recv_sems,
                                             6 * i + j, (*chips[j], c)),
            "arrived": lambda i, j: _remote(block(i, j, c), block(i, j, c), send_sems, recv_sems, 6 * i + j, (*chips[j], c)),
            "passed_on": lambda i, j: _remote(block(i, j, c), block(i, j, c), send_sems, recv_sems, 6 * i + 3 + j, sibling),
            "from_sibling": lambda i, j: _remote(block(i, j, 1 - c), block(i, j, 1 - c), send_sems, recv_sems,
                                                 6 * i + 3 + j, sibling),
        }
        res = []
        for kind in kinds:
            if kind == "own":
                res.append([_remote(ins[i], outs[i].at[me], send_sems, recv_sems, 6 * n + i, sibling) for i in range(n)])
            else:
                res.append([make[kind](i, j) for i in range(n) for j in range(3)])
        return res

    def start(*refs):
        over_ici, own = copies(("over_ici", "own"), *refs)
        for cp in over_ici + own:
            cp.start()

    def middle(*refs):
        arrived, passed_on = copies(("arrived", "passed_on"), *refs)
        for got, fwd in zip(arrived, passed_on):
            got.wait_recv()
            fwd.start()

    def end(*refs):
        from_sibling, own_in = copies(("from_sibling", "own"), *refs)
        for cp in from_sibling + own_in:
            cp.wait_recv()
        over_ici, passed_on, own_out = copies(("over_ici", "passed_on", "own"), *refs)
        for cp in over_ici + passed_on + own_out:
            cp.wait_send()

    return _Plan(shards, [jax.ShapeDtypeStruct((N_CHIPS,) + s.shape, s.dtype) for s in shards], 7 * n,
                 {"start": start, "middle": middle, "end": end})


def _send_sibling_halves(name, grads):
    n = len(grads)

    def body(*refs):
        ins, outs = refs[:n], refs[n:2 * n]
        send_sems, recv_sems = refs[2 * n:]
        x, y, c, _ = _mesh_place()
        sibling = (x, y, 1 - c)
        copies = []
        for i in range(n):
            for j in range(N_CHIPS):
                copies.append(_remote(ins[i].at[j, 1 - c], outs[i].at[j], send_sems, recv_sems, N_CHIPS * i + j, sibling))
                copies[-1].start()
        for cp in copies:
            cp.wait_recv()
        for cp in copies:
            cp.wait_send()

    return pl.pallas_call(
        body, name=name,
        out_shape=[jax.ShapeDtypeStruct((g.shape[0],) + g.shape[2:], g.dtype) for g in grads],
        in_specs=[ANY] * n, out_specs=[ANY] * n,
        scratch_shapes=[pltpu.SemaphoreType.DMA((N_CHIPS * n,)), pltpu.SemaphoreType.DMA((N_CHIPS * n,))],
    )(*grads)


def _send_chips_plan(sums):
    n = len(sums)

    def copies(ins, outs, send_sems, recv_sems):
        x, y, c, chips = _mesh_place()
        return [_remote(ins[i].at[2 * cx + cy], outs[i].at[j], send_sems, recv_sems, 3 * i + j, (cx, cy, c))
                for i in range(n) for j, (cx, cy) in enumerate(chips)]

    def start(*refs):
        for cp in copies(*refs):
            cp.start()

    def end(*refs):
        for cp in copies(*refs):
            cp.wait_recv()
        for cp in copies(*refs):
            cp.wait_send()

    return _Plan(sums, [jax.ShapeDtypeStruct((3,) + g.shape[1:], g.dtype) for g in sums], 3 * n,
                 {"start": start, "end": end})


def _exchange_halves(name, shards):
    n = len(shards)

    def body(*refs):
        outs = refs[n:2 * n]
        send_sems, recv_sems = refs[2 * n:]
        x, y, c, _ = _mesh_place()
        sibling = (x, y, 1 - c)
        remote = [_remote(outs[i].at[c], outs[i].at[c], send_sems, recv_sems, i, sibling) for i in range(n)]
        for cp in remote:
            cp.start()
        for i in range(n):
            _remote(outs[i].at[1 - c], outs[i].at[1 - c], send_sems, recv_sems, i, sibling).wait_recv()
        for cp in remote:
            cp.wait_send()

    return pl.pallas_call(
        body, name=name,
        out_shape=[jax.ShapeDtypeStruct(h.shape, h.dtype) for h in shards],
        in_specs=[ANY] * n, out_specs=[ANY] * n, input_output_aliases={i: i for i in range(n)},
        scratch_shapes=[pltpu.SemaphoreType.DMA((n,)), pltpu.SemaphoreType.DMA((n,))],
    )(*shards)


def _divisor_tile(rows, row_bytes, quantum=16):
    best = rows
    for t in range(quantum, rows + 1, quantum):
        if rows % t == 0 and 2 * t * row_bytes <= ROW_TILE_BUDGET:
            best = t
    return best


def _add_sibling(name, g4, t1):
    nb, _, hr, cols = g4.shape
    t = _divisor_tile(hr, cols * 6)

    def kern(g_ref, t_ref, o_ref):
        o_ref[0] = (g_ref[0, 0].astype(F32) + t_ref[0].astype(F32)).astype(o_ref.dtype)

    return pl.pallas_call(
        kern, name=name, grid=(nb, hr // t),
        in_specs=[pl.BlockSpec((1, 1, t, cols), lambda j, i: (j, lax.axis_index("c"), i, 0)),
                  pl.BlockSpec((1, t, cols), lambda j, i: (j, i, 0))],
        out_specs=pl.BlockSpec((1, t, cols), lambda j, i: (j, i, 0)),
        out_shape=jax.ShapeDtypeStruct((nb, hr, cols), g4.dtype),
        compiler_params=_params(("parallel", "parallel")),
    )(g4, t1)


def _add_chips(name, s1, t3):
    _, hr, cols = s1.shape
    t = _divisor_tile(hr, cols * 12)

    def kern(s_ref, t_ref, o_ref):
        acc = s_ref[0].astype(F32)
        for j in range(3):
            acc = acc + t_ref[j].astype(F32)
        o_ref[0] = acc

    return pl.pallas_call(
        kern, name=name, grid=(hr // t,),
        in_specs=[pl.BlockSpec((1, t, cols), lambda i: (2 * lax.axis_index("x") + lax.axis_index("y"), i, 0)),
                  pl.BlockSpec((3, t, cols), lambda i: (0, i, 0))],
        out_specs=pl.BlockSpec((1, t, cols), lambda i: (lax.axis_index("c"), i, 0)),
        out_shape=jax.ShapeDtypeStruct((2, hr, cols), F32),
        compiler_params=_params(("parallel",)),
    )(s1, t3)


def _chip_sums(tag, grads):
    g4 = [g.reshape(N_CHIPS, 2, g.shape[1] // 2, g.shape[2]) for g in grads]
    t1 = _send_sibling_halves("grads_to_sibling_" + tag, g4)
    return [_add_sibling(f"chip_sum_{tag}_{i}", g, t) for i, (g, t) in enumerate(zip(g4, t1))]


def _shard_sums(sums, received):
    halves = [_add_chips(f"shard_sum_{i}", s, t) for i, (s, t) in enumerate(zip(sums, received))]
    full = _exchange_halves("grad_halves_to_sibling", halves)
    return [f.reshape(f.shape[1] * 2, f.shape[2]) for f in full]


def _pack_rows(size, width):
    return -(-size // (8 * width)) * 8


def _pack(arrays, width):
    parts = []
    for a in arrays:
        flat = a.reshape(-1).astype(F32)
        rows = _pack_rows(flat.shape[0], width)
        parts.append(jnp.pad(flat, (0, rows * width - flat.shape[0])).reshape(rows, width))
    return jnp.concatenate(parts, axis=0)


def _unpack(block, shapes, width):
    out, r = [], 0
    for shp in shapes:
        size = 1
        for s_ in shp:
            size *= s_
        rows = _pack_rows(size, width)
        out.append(block[r:r + rows].reshape(-1)[:size].reshape(shp))
        r += rows
    return out


SMALL_PARAMS = ("b_ada", "g_pre_mix", "g_post_mix", "b_conv_ssm", "dt_bias_fwd", "dt_bias_bwd", "a_log_fwd", "a_log_bwd",
                "d_skip", "g_ssm_norm", "b_glu", "b_dw", "ln_g", "ln_b", "b_conv_out", "b_gate", "g_pre_ffn", "g_post_ffn")
SHARDED_SMALL = ("w_conv_ssm", "w_dw")
MATRICES = ("w_in", "w_ssm_out", "w_conv_out", "w_mix_out", "w_gate_up", "w_down")
ALL_PARAMS = ("w_ada", "b_ada", "g_pre_mix", "g_post_mix", "w_in", "w_conv_ssm", "b_conv_ssm", "dt_bias_fwd", "dt_bias_bwd",
              "a_log_fwd", "a_log_bwd", "d_skip", "g_ssm_norm", "w_ssm_out", "b_glu", "w_dw", "b_dw", "ln_g", "ln_b",
              "w_conv_out", "b_conv_out", "b_gate", "w_mix_out", "g_pre_ffn", "g_post_ffn", "w_gate_up", "w_down")
COND_ROWS = 48
COND_CONV_ROW = 8
COND_DW_ROW = 16
MOD_ROWS = 16


def kernel(x, c, w_ada, b_ada, g_pre_mix, g_post_mix, w_in, w_conv_ssm, b_conv_ssm, dt_bias_fwd, dt_bias_bwd, a_log_fwd, a_log_bwd, d_skip, g_ssm_norm, w_ssm_out, b_glu, w_dw, b_dw, ln_g, ln_b, w_conv_out, b_conv_out, b_gate, w_mix_out, g_pre_ffn, g_post_ffn, w_gate_up, w_down, loss_target, m_w_ada, m_b_ada, m_g_pre_mix, m_g_post_mix, m_w_in, m_w_conv_ssm, m_b_conv_ssm, m_dt_bias_fwd, m_dt_bias_bwd, m_a_log_fwd, m_a_log_bwd, m_d_skip, m_g_ssm_norm, m_w_ssm_out, m_b_glu, m_w_dw, m_b_dw, m_ln_g, m_ln_b, m_w_conv_out, m_b_conv_out, m_b_gate, m_w_mix_out, m_g_pre_ffn, m_g_post_ffn, m_w_gate_up, m_w_down, v_w_ada, v_b_ada, v_g_pre_mix, v_g_post_mix, v_w_in, v_w_conv_ssm, v_b_conv_ssm, v_dt_bias_fwd, v_dt_bias_bwd, v_a_log_fwd, v_a_log_bwd, v_d_skip, v_g_ssm_norm, v_w_ssm_out, v_b_glu, v_w_dw, v_b_dw, v_ln_g, v_ln_b, v_w_conv_out, v_b_conv_out, v_b_gate, v_w_mix_out, v_g_pre_ffn, v_g_post_ffn, v_w_gate_up, v_w_down):
    given = dict(locals())
    wgt = {n: given[n][0] for n in ALL_PARAMS}
    mom = {n: given["m_" + n][0] for n in ALL_PARAMS}
    var = {n: given["v_" + n][0] for n in ALL_PARAMS}
    xs, tgt = x[0], loss_target[0]
    s, d = xs.shape
    d_ssm = 2 * d
    n_heads = d_ssm // HEAD_DIM
    d_xbc = d_ssm + 2 * N_GROUPS * D_STATE
    xi, yi, ci = lax.axis_index("x"), lax.axis_index("y"), lax.axis_index("c")
    chip = 2 * xi + yi
    dev = 2 * chip + ci
    k_conv, k_dw = wgt["w_conv_ssm"].shape[0], wgt["w_dw"].shape[0]
    xbc_shard, dw_shard = d_xbc // N_CHIPS, d // N_CHIPS

    width1 = max(d, xbc_shard)
    blk = jnp.zeros((COND_ROWS, width1), F32)
    blk = blk.at[0, :d].set(c[0])
    blk = blk.at[COND_CONV_ROW:COND_CONV_ROW + k_conv, :xbc_shard].set(wgt["w_conv_ssm"])
    blk = blk.at[COND_DW_ROW:COND_DW_ROW + k_dw, :dw_shard].set(wgt["w_dw"])
    g1 = _gather_devices("gather_cond", blk).reshape(N_DEV, COND_ROWS, width1)
    c_all = g1[:, 0, :d]
    w_conv_full = jnp.concatenate([g1[2 * k, COND_CONV_ROW:COND_CONV_ROW + k_conv, :xbc_shard] for k in range(N_CHIPS)], axis=1)
    w_dw_full = jnp.concatenate([g1[2 * k, COND_DW_ROW:COND_DW_ROW + k_dw, :dw_shard] for k in range(N_CHIPS)], axis=1)
    c_act = jnp.pad(c_all * _sigmoid(c_all), ((0, MOD_ROWS - N_DEV), (0, 0)))

    mod_part = _matmul("ada_mod", c_act, wgt["w_ada"])
    g2 = _gather_devices("gather_mod", mod_part).reshape(N_DEV, MOD_ROWS, mod_part.shape[1])
    mod_all = jnp.concatenate([g2[2 * k, :N_DEV] for k in range(N_CHIPS)], axis=1) + wgt["b_ada"][None]
    mod = lax.dynamic_slice_in_dim(mod_all, dev, 1, axis=0)

    shards = [wgt["w_in"].T.astype(BF16)] + [wgt[n].astype(BF16) for n in MATRICES[1:]]
    wts = {"w_in_t": _run_plan("gather_w_in", _gather_plan(shards[:1]))[0].reshape(-1, d)}
    late = dict(zip(MATRICES[1:], shards[1:]))
    row = lambda v: v.reshape(1, -1)
    sm = {"g_pre_mix": row(wgt["g_pre_mix"]), "g_post_mix": row(wgt["g_post_mix"]), "w_conv_ssm": w_conv_full,
          "b_conv_ssm": row(wgt["b_conv_ssm"]),
          "dt_bias": row(jnp.concatenate([wgt["dt_bias_fwd"], wgt["dt_bias_bwd"]])),
          "a_neg": row(-jnp.exp(jnp.concatenate([wgt["a_log_fwd"], wgt["a_log_bwd"]]))),
          "d_skip_x": row(jnp.repeat(wgt["d_skip"], HEAD_DIM)), "g_ssm_norm": row(wgt["g_ssm_norm"]),
          "b_glu": row(wgt["b_glu"]), "w_dw": w_dw_full, "b_dw": row(wgt["b_dw"]), "ln_g": row(wgt["ln_g"]),
          "ln_b": row(wgt["ln_b"]), "b_conv_out": row(wgt["b_conv_out"]), "b_gate": row(wgt["b_gate"]),
          "g_pre_ffn": row(wgt["g_pre_ffn"]), "g_post_ffn": row(wgt["g_post_ffn"])}

    sq, grad_x, big, small, dmod = _local_step(xs, tgt, mod, wts, sm, late=late)
    loss = lax.psum((0.5 / d) * sq[0, 0], ("x", "y", "c"))

    local_small = {"b_ada": dmod, "g_pre_mix": small["g_pre_mix"], "g_post_mix": small["g_post_mix"],
                   "b_conv_ssm": small["b_conv_ssm"], "dt_bias_fwd": small["dt_bias"][:, :n_heads],
                   "dt_bias_bwd": small["dt_bias"][:, n_heads:], "a_log_fwd": small["a_log"][:, :n_heads],
                   "a_log_bwd": small["a_log"][:, n_heads:],
                   "d_skip": jnp.sum(small["d_skip_x"].reshape(n_heads, HEAD_DIM), axis=1),
                   "g_ssm_norm": small["g_ssm_norm"], "b_glu": small["b_glu"], "b_dw": small["b_dw"],
                   "ln_g": small["ln_g"], "ln_b": small["ln_b"], "b_conv_out": small["b_conv_out"],
                   "b_gate": small["b_gate"], "g_pre_ffn": small["g_pre_ffn"], "g_post_ffn": small["g_post_ffn"],
                   "w_conv_ssm": small["w_conv_ssm"], "w_dw": small["w_dw"]}
    names = SMALL_PARAMS + SHARDED_SMALL
    pack = _pack([local_small[n] for n in names], d)
    rows_p = pack.shape[0]
    g3 = _gather_devices("gather_small_grads", pack).reshape(N_DEV, rows_p, d)
    total = _rowwise("sum_small_grads", lambda *blocks: (functools.reduce(lambda a, b: a + b, blocks),),
                     [g3[i] for i in range(N_DEV)], [], [(d, F32)])[0]
    full_shapes = [wgt[n].shape for n in SMALL_PARAMS] + [(k_conv, d_xbc), (k_dw, d)]
    summed = dict(zip(names, _unpack(total, full_shapes, d)))
    grads = {n: summed[n] for n in SMALL_PARAMS}
    grads["w_conv_ssm"] = lax.dynamic_slice_in_dim(summed["w_conv_ssm"], chip * xbc_shard, xbc_shard, axis=1)
    grads["w_dw"] = lax.dynamic_slice_in_dim(summed["w_dw"], chip * dw_shard, dw_shard, axis=1)

    dmod_all = g3[:, :N_MOD, :].reshape(N_DEV, N_MOD * d)
    ada_cols = wgt["w_ada"].shape[1]
    dmod_cols = jnp.pad(lax.dynamic_slice_in_dim(dmod_all, chip * ada_cols, ada_cols, axis=1),
                        ((0, MOD_ROWS - N_DEV), (0, 0)))
    grads["w_ada"] = _matmul("g_w_ada", c_act, dmod_cols, ta=True)

    sums_in = _chip_sums("w_in", [big["w_in_t"].reshape(N_CHIPS, -1, d)])
    received_in = _run_plan("grads_to_chips_w_in", _send_chips_plan(sums_in))
    sums_early, received_early = big["pending"]
    reduced = _shard_sums(sums_in + sums_early, received_in + received_early)
    grads["w_in"] = reduced[0].T
    for n, g in zip(MATRICES[1:], reduced[1:]):
        grads[n] = g

    delta, new_m, new_v = {}, {}, {}
    for n in ("w_ada",) + MATRICES:
        delta[n], new_m[n], new_v[n] = _adamw("adamw_" + n, wgt[n], grads[n], mom[n], var[n])
    for group, width, tag in ((SMALL_PARAMS, d, "small"), (SHARDED_SMALL, LANES, "conv")):
        shapes = [wgt[n].shape for n in group]
        packs = [_pack([src[n] for n in group], width) for src in (wgt, grads, mom, var)]
        outs = _adamw("adamw_" + tag, *packs)
        for res, o in zip((delta, new_m, new_v), outs):
            res.update(zip(group, _unpack(o, shapes, width)))

    lead = lambda a: a[None]
    return (loss, grad_x[None], *[lead(grads[n]) for n in ALL_PARAMS], *[lead(delta[n]) for n in ALL_PARAMS],
            *[lead(new_m[n]) for n in ALL_PARAMS], *[lead(new_v[n]) for n in ALL_PARAMS])
```

```python
import functools

import jax
import jax.numpy as jnp
from jax import lax
from jax.experimental import pallas as pl
from jax.experimental.pallas import tpu as pltpu

F32 = jnp.float32
BF16 = jnp.bfloat16

N_GROUPS = 8
HEAD_DIM = 64
D_STATE = 128
CHUNK = 128
EPS = 1e-6
N_MOD = 6
ADAM_LR = 0.001
ADAM_B1 = 0.9
ADAM_B2 = 0.999
ADAM_EPS = 1e-08
ADAM_WD = 0.01
ADAM_STEP = 10

V7X_VMEM_BYTES = 64 * 1024 * 1024
VMEM_LIMIT = V7X_VMEM_BYTES - 8 * 1024 * 1024
ROW_TILE_BUDGET = 20 * 1024 * 1024
LANES = 128
NEG = -1e30
MESH = pl.DeviceIdType.MESH
N_CHIPS = 4
N_DEV = 8


def _params(sem):
    return pltpu.CompilerParams(dimension_semantics=sem, vmem_limit_bytes=VMEM_LIMIT)


def _sigmoid(x):
    return 1.0 / (1.0 + jnp.exp(-x))


def _silu(x):
    return x * _sigmoid(x)


def _dsilu(x):
    s = _sigmoid(x)
    return s * (1.0 + x * (1.0 - s))


def _softplus(x):
    return jnp.maximum(x, 0.0) + jnp.log(1.0 + jnp.exp(-jnp.abs(x)))


def _sum0(a):
    return jnp.sum(a, axis=0, keepdims=True)


def _mean1(a):
    return jnp.mean(a, axis=1, keepdims=True)


def _rowwise(name, body, rows, bcasts, out_rows, out_accs=(), tile=None, plan=None):
    rows = [r if isinstance(r, tuple) else (r, r.shape[1], 0) for r in rows]
    s = rows[0][0].shape[0]
    if tile is None:
        per_row = sum(w * a.dtype.itemsize for a, w, _ in rows) + sum(w * jnp.dtype(dt).itemsize for w, dt in out_rows)
        tile = 1024
        while tile > 16 and (tile * per_row * 2 > ROW_TILE_BUDGET or s % tile):
            tile //= 2
        if s * per_row * 2 <= ROW_TILE_BUDGET:
            tile = s
    assert s % tile == 0
    n_in = len(rows) + len(bcasts)
    n_o = len(out_rows)
    n_out = n_o + len(out_accs)
    p_ops, p_in_specs, p_out_shapes, p_out_specs, p_scratch, _ = _plan_call_parts(plan)
    n_pi, n_po = len(p_ops), len(p_out_shapes)
    n_steps = (s // tile,)

    def kern(*refs):
        plan_ins = refs[n_in:n_in + n_pi]
        outs = refs[n_in + n_pi:n_in + n_pi + n_out]
        plan_outs = refs[n_in + n_pi + n_out:n_in + n_pi + n_out + n_po]
        plan_sems = refs[n_in + n_pi + n_out + n_po:]
        _plan_phase(plan, "start", n_steps, plan_ins, plan_outs, plan_sems)
        _plan_phase(plan, "middle", n_steps, plan_ins, plan_outs, plan_sems)
        res = body(*[r[...].astype(F32) for r in refs[:n_in]])
        for o, v in zip(outs[:n_o], res[:n_o]):
            o[...] = v.astype(o.dtype)
        if out_accs:
            @pl.when(pl.program_id(0) == 0)
            def _():
                for o in outs[n_o:]:
                    o[...] = jnp.zeros_like(o)
            for o, v in zip(outs[n_o:], res[n_o:]):
                o[...] += v
        _plan_phase(plan, "end", n_steps, plan_ins, plan_outs, plan_sems)

    in_specs = [pl.BlockSpec((tile, w), functools.partial(lambda i, cb: (i, cb), cb=cb)) for _, w, cb in rows]
    in_specs += [pl.BlockSpec(b.shape, functools.partial(lambda i, nd: (0,) * nd, nd=b.ndim)) for b in bcasts]
    out_shape = [jax.ShapeDtypeStruct((s, w), dt) for w, dt in out_rows]
    out_shape += [jax.ShapeDtypeStruct((1, w), F32) for w in out_accs]
    out_specs = [pl.BlockSpec((tile, w), lambda i: (i, 0)) for w, _ in out_rows]
    out_specs += [pl.BlockSpec((1, w), lambda i: (0, 0)) for w in out_accs]
    return pl.pallas_call(
        kern, name=name, grid=n_steps, in_specs=in_specs + p_in_specs, out_specs=out_specs + p_out_specs,
        out_shape=out_shape + p_out_shapes, scratch_shapes=p_scratch,
        compiler_params=_params(("arbitrary",) if out_accs or plan is not None else ("parallel",)),
    )(*[a for a, _, _ in rows], *bcasts, *p_ops)


def _tile(n, pref):
    if n <= pref:
        return n
    t = (pref // LANES) * LANES
    while t >= LANES:
        if n % t == 0:
            return t
        t -= LANES
    return n


def _matmul(name, a, b, *, ta=False, tb=False, out_dtype=F32, bias=None, add=None, b_blocks=1, out_blocks=1,
            b_rows=None, out_rows=None, tm=1024, tn=1408, tk=2816):
    m, k = (a.shape[1], a.shape[0]) if ta else a.shape
    if b_blocks > 1:
        rows_b, cols_b = b.shape[1], b.shape[2] * b_blocks
    else:
        rows_b, cols_b = b.shape
    if b_rows is not None:
        rows_b = b_rows[1]
    n, kb = (rows_b, cols_b) if tb else (cols_b, rows_b)
    assert k == kb, (name, a.shape, b.shape)
    tm, tn, tk = _tile(m, tm), _tile(n, tn), _tile(k, tk)
    if b_blocks > 1:
        per = cols_b // b_blocks
        if tb:
            tk = _tile(per, tk)
        else:
            tn = _tile(per, tn)
    if out_blocks > 1:
        tn = _tile(n // out_blocks, tn)
    nk = k // tk
    grid = (m // tm, n // tn, nk)

    a_spec = pl.BlockSpec((tk, tm), lambda i, j, kk: (kk, i)) if ta else pl.BlockSpec((tm, tk), lambda i, j, kk: (i, kk))
    if b_blocks > 1:
        if tb:
            nb = per // tk
            b_spec = pl.BlockSpec((1, tn, tk), lambda i, j, kk: (kk // nb, j, kk % nb))
        else:
            nb = per // tn
            b_spec = pl.BlockSpec((1, tk, tn), lambda i, j, kk: (j // nb, kk, j % nb))
    elif b_rows is not None:
        first = b_rows[0]
        if tb:
            b_spec = pl.BlockSpec((pl.Element(tn), pl.Element(tk)),
                                  lambda i, j, kk: (pl.multiple_of(first + j * tn, LANES), kk * tk))
        else:
            b_spec = pl.BlockSpec((pl.Element(tk), pl.Element(tn)),
                                  lambda i, j, kk: (pl.multiple_of(first + kk * tk, LANES), j * tn))
    else:
        b_spec = pl.BlockSpec((tn, tk), lambda i, j, kk: (j, kk)) if tb else pl.BlockSpec((tk, tn), lambda i, j, kk: (kk, j))
    in_specs = [a_spec, b_spec]
    operands = [a, b]
    if bias is not None:
        in_specs.append(pl.BlockSpec((1, tn), lambda i, j, kk: (0, j)))
        operands.append(bias)
    if add is not None:
        in_specs.append(pl.BlockSpec((tm, tn), lambda i, j, kk: (i, j)))
        operands.append(add)
    aliases = {}
    if out_blocks > 1:
        nbo = (n // out_blocks) // tn
        out_spec = pl.BlockSpec((1, tm, tn), lambda i, j, kk: (j // nbo, i, j % nbo))
        out_shape = jax.ShapeDtypeStruct((out_blocks, m, n // out_blocks), out_dtype)
    elif out_rows is not None:
        first_out, total, previous = out_rows
        out_spec = pl.BlockSpec((pl.Element(tm), pl.Element(tn)),
                                lambda i, j, kk: (pl.multiple_of(first_out + i * tm, LANES), j * tn))
        out_shape = jax.ShapeDtypeStruct((total, n), out_dtype)
        if previous is not None:
            aliases = {len(operands): 0}
            in_specs.append(pl.BlockSpec(memory_space=pl.ANY))
            operands.append(previous)
    else:
        out_spec = pl.BlockSpec((tm, tn), lambda i, j, kk: (i, j))
        out_shape = jax.ShapeDtypeStruct((m, n), out_dtype)
    dims = (((0 if ta else 1,), (1 if tb else 0,)), ((), ()))
    has_bias, has_add, has_previous = bias is not None, add is not None, bool(aliases)

    def kern(*refs):
        a_ref, b_ref = refs[0], refs[1]
        pos = 2
        bias_ref = add_ref = None
        if has_bias:
            bias_ref = refs[pos]
            pos += 1
        if has_add:
            add_ref = refs[pos]
            pos += 1
        if has_previous:
            pos += 1
        o_ref = refs[pos]
        acc_ref = refs[pos + 1] if nk > 1 else None
        av = a_ref[...].astype(BF16)
        bv = (b_ref[0] if b_blocks > 1 else b_ref[...]).astype(BF16)
        p = lax.dot_general(av, bv, dims, preferred_element_type=F32)

        def finish(acc):
            if has_bias:
                acc = acc + bias_ref[...]
            if has_add:
                acc = acc + add_ref[...]
            if out_blocks > 1:
                o_ref[0] = acc.astype(o_ref.dtype)
            else:
                o_ref[...] = acc.astype(o_ref.dtype)

        if nk == 1:
            finish(p)
        else:
            kk = pl.program_id(2)

            @pl.when(kk == 0)
            def _():
                acc_ref[...] = p

            @pl.when(kk > 0)
            def _():
                acc_ref[...] += p

            @pl.when(kk == nk - 1)
            def _():
                finish(acc_ref[...])

    return pl.pallas_call(
        kern, name=name, grid=grid, in_specs=in_specs, out_specs=out_spec, out_shape=out_shape,
        input_output_aliases=aliases, scratch_shapes=[pltpu.VMEM((tm, tn), F32)] if nk > 1 else [],
        compiler_params=_params(("parallel", "parallel", "arbitrary")),
    )(*operands)


CONV_HALO = 16
CONV_ROWS = 256


def _taps(win, shifts, rows):
    n = win.shape[0]
    for j, s in enumerate(shifts):
        yield j, (pltpu.roll(win, (n - s) % n, axis=0) if s % n else win)[:rows]


def _dwconv_fwd(name, x, w, b, *, silu, out_dtype=F32, plan=None):
    s, c = x.shape
    k = w.shape[0]
    pad = (k - 1) // 2
    assert pad <= CONV_HALO and c % LANES == 0
    t = min(CONV_ROWS, s)
    n_chunks = s // t
    fwd_shifts = [CONV_HALO - pad + j for j in range(k)]
    p_ops, p_in_specs, p_out_shapes, p_out_specs, p_scratch, _ = _plan_call_parts(plan)
    n_pi, n_po = len(p_ops), len(p_out_shapes)
    n_steps = (c // LANES,)

    def kern(*refs):
        x_ref, w_ref, b_ref = refs[:3]
        plan_ins = refs[3:3 + n_pi]
        o_ref = refs[3 + n_pi]
        plan_outs = refs[4 + n_pi:4 + n_pi + n_po]
        xp_ref = refs[4 + n_pi + n_po]
        plan_sems = refs[5 + n_pi + n_po:]
        _plan_phase(plan, "start", n_steps, plan_ins, plan_outs, plan_sems)
        _plan_phase(plan, "middle", n_steps, plan_ins, plan_outs, plan_sems)
        zeros = jnp.zeros((CONV_HALO, LANES), F32)
        xp_ref[0:CONV_HALO, :] = zeros
        xp_ref[CONV_HALO + s:CONV_HALO + s + CONV_HALO, :] = zeros
        xp_ref[CONV_HALO:CONV_HALO + s, :] = x_ref[...].astype(F32)
        bv = b_ref[...]

        def chunk(i, carry):
            base = pl.multiple_of(i * t, 16)
            win = xp_ref[pl.ds(base, t + 2 * CONV_HALO), :]
            acc = jnp.zeros((t, LANES), F32)
            for j, xs in _taps(win, fwd_shifts, t):
                acc = acc + xs * w_ref[pl.ds(j, 1), :]
            acc = acc + bv
            o_ref[pl.ds(base, t), :] = (_silu(acc) if silu else acc).astype(o_ref.dtype)
            return carry

        lax.fori_loop(0, n_chunks, chunk, 0)
        _plan_phase(plan, "end", n_steps, plan_ins, plan_outs, plan_sems)

    res = pl.pallas_call(
        kern, name=name, grid=n_steps,
        in_specs=[pl.BlockSpec((s, LANES), lambda i: (0, i)), pl.BlockSpec((k, LANES), lambda i: (0, i)),
                  pl.BlockSpec((1, LANES), lambda i: (0, i))] + p_in_specs,
        out_specs=[pl.BlockSpec((s, LANES), lambda i: (0, i))] + p_out_specs,
        out_shape=[jax.ShapeDtypeStruct((s, c), out_dtype)] + p_out_shapes,
        scratch_shapes=[pltpu.VMEM((s + 2 * CONV_HALO, LANES), F32)] + p_scratch,
        compiler_params=_params(("parallel",) if plan is None else ("arbitrary",)),
    )(x, w, b, *p_ops)
    return res[0] if plan is None else (res[0], list(res[1:]))


def _dwconv_bwd(name, x, w, b, dout, *, silu, dx_dtype=F32):
    s, c = x.shape
    k = w.shape[0]
    pad = (k - 1) // 2
    t = min(CONV_ROWS, s)
    n_chunks = s // t
    fwd_shifts = [CONV_HALO - pad + j for j in range(k)]
    bwd_shifts = [CONV_HALO + pad - j for j in range(k)]

    def kern(x_ref, w_ref, b_ref, do_ref, dx_ref, dw_ref, db_ref, xp_ref, dp_ref):
        zeros = jnp.zeros((CONV_HALO, LANES), F32)
        for ref in (xp_ref, dp_ref):
            ref[0:CONV_HALO, :] = zeros
            ref[CONV_HALO + s:CONV_HALO + s + CONV_HALO, :] = zeros
        xp_ref[CONV_HALO:CONV_HALO + s, :] = x_ref[...].astype(F32)
        bv = b_ref[...]
        dw_ref[...] = jnp.zeros_like(dw_ref)

        def pre_chunk(i, dbias):
            base = pl.multiple_of(i * t, 16)
            win = xp_ref[pl.ds(base, t + 2 * CONV_HALO), :]
            dpre = do_ref[pl.ds(base, t), :].astype(F32)
            if silu:
                acc = jnp.zeros((t, LANES), F32)
                for j, xs in _taps(win, fwd_shifts, t):
                    acc = acc + xs * w_ref[pl.ds(j, 1), :]
                dpre = dpre * _dsilu(acc + bv)
            dp_ref[pl.ds(base + CONV_HALO, t), :] = dpre
            for j, xs in _taps(win, fwd_shifts, t):
                dw_ref[pl.ds(j, 1), :] += _sum0(dpre * xs)
            return dbias + _sum0(dpre)

        db_ref[...] = lax.fori_loop(0, n_chunks, pre_chunk, jnp.zeros((1, LANES), F32))

        def dx_chunk(i, carry):
            base = pl.multiple_of(i * t, 16)
            win = dp_ref[pl.ds(base, t + 2 * CONV_HALO), :]
            acc = jnp.zeros((t, LANES), F32)
            for j, dps in _taps(win, bwd_shifts, t):
                acc = acc + dps * w_ref[pl.ds(j, 1), :]
            dx_ref[pl.ds(base, t), :] = acc.astype(dx_ref.dtype)
            return carry

        lax.fori_loop(0, n_chunks, dx_chunk, 0)

    col = lambda rows: pl.BlockSpec((rows, LANES), lambda i: (0, i))
    return pl.pallas_call(
        kern, name=name, grid=(c // LANES,),
        in_specs=[col(s), col(k), col(1), col(s)],
        out_specs=[col(s), col(k), col(1)],
        out_shape=[jax.ShapeDtypeStruct((s, c), dx_dtype), jax.ShapeDtypeStruct((k, c), F32),
                   jax.ShapeDtypeStruct((1, c), F32)],
        scratch_shapes=[pltpu.VMEM((s + 2 * CONV_HALO, LANES), F32), pltpu.VMEM((s + 2 * CONV_HALO, LANES), F32)],
        compiler_params=_params(("parallel",)),
    )(x, w, b, dout)


_NT =(((1,), (1,)), ((), ()))
_TN = (((0,), (0,)), ((), ()))


def _dot(a, b, dims=None):
    if dims is None:
        return jnp.dot(a, b, preferred_element_type=F32)
    return lax.dot_general(a, b, dims, preferred_element_type=F32)


HEAD_QUANTITIES = 4
GROUPS_PER_STEP = 2


def _scan_tables(n_heads):
    j_heads = n_heads // N_GROUPS
    used = 3 * HEAD_QUANTITIES * j_heads
    assert used <= LANES and j_heads % 2 == 0 and N_GROUPS % GROUPS_PER_STEP == 0
    gw = j_heads * HEAD_DIM
    r = jnp.arange(LANES)[:, None]

    def expand(quantity, width):
        head_of_lane = jnp.arange(j_heads * width)[None] // width
        return ((r // (3 * j_heads) == quantity) & (r % j_heads == head_of_lane) & (r < used)).astype(BF16)

    sel_cols = (jnp.arange(gw)[:, None] // HEAD_DIM == jnp.arange(LANES)[None]).astype(BF16)
    h2 = 2 * n_heads
    rows = jnp.arange(3 * HEAD_QUANTITIES * h2)
    head = rows % n_heads
    col = ((rows % h2) // n_heads * N_GROUPS + head // j_heads) * LANES + (rows // h2) * j_heads + head % j_heads
    route = (col[:, None] == jnp.arange(2 * N_GROUPS * LANES)[None]).astype(BF16)
    return {"ex_dt": expand(0, HEAD_DIM), "ex_gam": expand(1, CHUNK), "ex_din": expand(2, HEAD_DIM),
            "ex_dst": expand(3, HEAD_DIM), "sel_cols": sel_cols, "route": route}


def _chunk_decay(etot, n_heads, direction):
    j_heads = n_heads // N_GROUPS
    ed = etot[:, :, direction * n_heads:(direction + 1) * n_heads]
    per_group = jnp.pad(ed.reshape(ed.shape[0], 8, N_GROUPS, j_heads), ((0, 0), (0, 0), (0, 0), (0, LANES - j_heads)))
    return jnp.repeat(ed, HEAD_DIM, axis=2), per_group.reshape(ed.shape[0], 8, N_GROUPS * LANES)


def _scan_specs(reverse_order, direction, nc, j_heads, d_ssm):
    gps = GROUPS_PER_STEP
    gw = j_heads * HEAD_DIM
    b_off = d_ssm // (gps * D_STATE)
    c_off = b_off + N_GROUPS // gps
    d_off = direction * (N_GROUPS // gps)
    zz = (lambda z: nc - 1 - z) if reverse_order else (lambda z: z)
    const = lambda shape: pl.BlockSpec(shape, lambda g, z: (0,) * len(shape))
    return {
        "xs": pl.BlockSpec((CHUNK, gps * gw), lambda g, z: (zz(z), g)),
        "b": pl.BlockSpec((CHUNK, gps * D_STATE), lambda g, z: (zz(z), b_off + g)),
        "c": pl.BlockSpec((CHUNK, gps * D_STATE), lambda g, z: (zz(z), c_off + g)),
        "q": pl.BlockSpec((CHUNK, gps * LANES), lambda g, z: (zz(z), d_off + g)),
        "gam_t": pl.BlockSpec((gps * j_heads, CHUNK), lambda g, z: (d_off + g, zz(z))),
        "etot_x": pl.BlockSpec((1, 8, gps * gw), lambda g, z: (zz(z), 0, g)),
        "etot_g": pl.BlockSpec((1, 8, gps * LANES), lambda g, z: (zz(z), 0, g)),
        "state": pl.BlockSpec((gps, 1, D_STATE, gw), lambda g, z: (g, zz(z), 0, 0)),
        "grp": pl.BlockSpec((CHUNK, gps * D_STATE), lambda g, z: (zz(z), g)),
        "ex": const((LANES, gw)), "ex_gam": const((LANES, j_heads * CHUNK)), "sel": const((gw, LANES)),
    }


def _scan_masks(reverse):
    li = lax.broadcasted_iota(jnp.int32, (CHUNK, CHUNK), 0)
    si = lax.broadcasted_iota(jnp.int32, (CHUNK, CHUNK), 1)
    mask = (li <= si) if reverse else (li >= si)
    mask_t = (si <= li) if reverse else (si >= li)
    return li, si, mask, mask_t, si < HEAD_DIM


def _plan_phase(plan, phase, n_steps, ins, outs, sems):
    if plan is None or phase not in plan.phases:
        return
    g = pl.program_id(0)
    if len(n_steps) == 1:
        when = {"start": g == 0, "middle": g == n_steps[0] // 2, "end": g == n_steps[0] - 1}[phase]
    else:
        z = pl.program_id(1)
        when = {"start": (g == 0) & (z == 0), "middle": (g == n_steps[0] // 2) & (z == 0),
                "end": (g == n_steps[0] - 1) & (z == n_steps[1] - 1)}[phase]

    @pl.when(when)
    def _():
        plan.phases[phase](ins, outs, *sems)


def _plan_call_parts(plan):
    if plan is None:
        return [], [], [], [], [], ("parallel", "arbitrary")
    n_in, n_out = len(plan.operands), len(plan.out_shapes)
    return plan.operands, [ANY] * n_in, plan.out_shapes, [ANY] * n_out, plan.sems(), ("arbitrary", "arbitrary")


def _scan_fwd(name, xbc_c, q_all, gam_t, etot_x, tb, *, direction, d_ssm, plan=None):
    s = xbc_c.shape[0]
    nc = s // CHUNK
    j_heads = tb["ex_dt"].shape[1] // HEAD_DIM
    gw = j_heads * HEAD_DIM
    gps = GROUPS_PER_STEP
    reverse = direction == 1
    sp = _scan_specs(reverse, direction, nc, j_heads, d_ssm)
    p_ops, p_in_specs, p_out_shapes, p_out_specs, p_scratch, semantics = _plan_call_parts(plan)
    n_pi, n_po = len(p_ops), len(p_out_shapes)
    n_steps = (N_GROUPS // gps, nc)

    def kern(*refs):
        xs_ref, b_ref, c_ref, q_ref, gamt_ref, etx_ref, exdt_ref, exgam_ref, exdin_ref, exdst_ref = refs[:10]
        plan_ins = refs[10:10 + n_pi]
        y_ref, hs_ref = refs[10 + n_pi:12 + n_pi]
        plan_outs = refs[12 + n_pi:12 + n_pi + n_po]
        h_ref = refs[12 + n_pi + n_po]
        plan_sems = refs[13 + n_pi + n_po:]
        _plan_phase(plan, "start", n_steps, plan_ins, plan_outs, plan_sems)
        _plan_phase(plan, "middle", n_steps, plan_ins, plan_outs, plan_sems)

        @pl.when(pl.program_id(1) == 0)
        def _():
            h_ref[...] = jnp.zeros_like(h_ref)

        _, _, mask, _, lo = _scan_masks(reverse)
        for gi in range(gps):
            bb = b_ref[:, gi * D_STATE:(gi + 1) * D_STATE].astype(BF16)
            cb = c_ref[:, gi * D_STATE:(gi + 1) * D_STATE].astype(BF16)
            cbt = _dot(cb, bb, _NT)
            q = q_ref[:, gi * LANES:(gi + 1) * LANES]
            dtx, dinx, dstx = _dot(q, exdt_ref[...]), _dot(q, exdin_ref[...]), _dot(q, exdst_ref[...])
            gcol = _dot(q, exgam_ref[...])
            xdt = xs_ref[:, gi * gw:(gi + 1) * gw].astype(F32) * dtx
            ht = h_ref[gi]
            y_off = _dot(cb, ht.astype(BF16)) * dinx
            hs_ref[gi, 0] = ht
            for p in range(j_heads // 2):
                lanes = slice(p * CHUNK, (p + 1) * CHUNK)
                x2 = xdt[:, lanes]
                acc = y_off[:, lanes]
                for idx, j in enumerate((2 * p, 2 * p + 1)):
                    g_row = gamt_ref[pl.ds(gi * j_heads + j, 1), :]
                    decay = jnp.exp(jnp.where(mask, gcol[:, j * CHUNK:(j + 1) * CHUNK] - g_row, NEG))
                    x_head = jnp.where(lo if idx == 0 else jnp.logical_not(lo), x2, 0.0).astype(BF16)
                    acc = acc + _dot((cbt * decay).astype(BF16), x_head)
                y_ref[:, gi * gw + p * CHUNK:gi * gw + (p + 1) * CHUNK] = acc.astype(y_ref.dtype)
            h_ref[gi] = ht * etx_ref[0, 0:1, gi * gw:(gi + 1) * gw] + _dot(bb, (xdt * dstx).astype(BF16), _TN)
        _plan_phase(plan, "end", n_steps, plan_ins, plan_outs, plan_sems)

    res = pl.pallas_call(
        kern, name=name, grid=n_steps,
        in_specs=[sp["xs"], sp["b"], sp["c"], sp["q"], sp["gam_t"], sp["etot_x"], sp["ex"], sp["ex_gam"], sp["ex"],
                  sp["ex"]] + p_in_specs,
        out_specs=[sp["xs"], sp["state"]] + p_out_specs,
        out_shape=[jax.ShapeDtypeStruct((s, d_ssm), BF16),
                   jax.ShapeDtypeStruct((N_GROUPS, nc, D_STATE, gw), F32)] + p_out_shapes,
        scratch_shapes=[pltpu.VMEM((gps, D_STATE, gw), F32)] + p_scratch,
        compiler_params=_params(semantics),
    )(xbc_c, xbc_c, xbc_c, q_all, gam_t, etot_x, tb["ex_dt"], tb["ex_gam"], tb["ex_din"], tb["ex_dst"], *p_ops)
    return res[0], res[1], list(res[2:])


def _scan_bwd(name, xbc_c, dy, hs, q_all, gam_t, etot_x, etot_g, tb, *, direction, d_ssm, plan=None):
    s = xbc_c.shape[0]
    nc = s // CHUNK
    j_heads = tb["ex_dt"].shape[1] // HEAD_DIM
    gw = j_heads * HEAD_DIM
    gps = GROUPS_PER_STEP
    reverse = direction == 1
    sp = _scan_specs(not reverse, direction, nc, j_heads, d_ssm)
    hp = lax.Precision.HIGHEST
    p_ops, p_in_specs, p_out_shapes, p_out_specs, p_scratch, semantics = _plan_call_parts(plan)
    n_pi, n_po = len(p_ops), len(p_out_shapes)
    n_steps = (N_GROUPS // gps, nc)

    def kern(*refs):
        (xs_ref, b_ref, c_ref, dy_ref, hs_ref, q_ref, gamt_ref, etg_ref, etx_ref, exdt_ref, exgam_ref, exdin_ref,
         exdst_ref, sel_ref) = refs[:14]
        plan_ins = refs[14:14 + n_pi]
        dxs_ref, db_ref, dc_ref, ddt_ref, da_ref = refs[14 + n_pi:19 + n_pi]
        plan_outs = refs[19 + n_pi:19 + n_pi + n_po]
        dh_ref, tmp_ref = refs[19 + n_pi + n_po:21 + n_pi + n_po]
        plan_sems = refs[21 + n_pi + n_po:]
        _plan_phase(plan, "start", n_steps, plan_ins, plan_outs, plan_sems)
        _plan_phase(plan, "middle", n_steps, plan_ins, plan_outs, plan_sems)

        @pl.when(pl.program_id(1) == 0)
        def _():
            dh_ref[...] = jnp.zeros_like(dh_ref)

        li, si, mask, mask_t, lo = _scan_masks(reverse)
        sel = sel_ref[...]
        incl = ((si <= li) if reverse else (si >= li)).astype(F32)
        excl = ((si > li) if reverse else (si < li)).astype(F32)
        for gi in range(gps):
            grp_lanes = slice(gi * D_STATE, (gi + 1) * D_STATE)
            bb = b_ref[:, grp_lanes].astype(BF16)
            cb = c_ref[:, grp_lanes].astype(BF16)
            cbt = _dot(cb, bb, _NT)
            cbt_t = _dot(bb, cb, _NT)
            q = q_ref[:, gi * LANES:(gi + 1) * LANES]
            dtx, dinx, dstx = _dot(q, exdt_ref[...]), _dot(q, exdin_ref[...]), _dot(q, exdst_ref[...])
            gcol = _dot(q, exgam_ref[...])
            x_all = xs_ref[:, gi * gw:(gi + 1) * gw].astype(F32)
            dy_all = dy_ref[:, gi * gw:(gi + 1) * gw].astype(F32)
            xdt = x_all * dtx
            xb = xdt.astype(BF16)
            ht = hs_ref[gi, 0]
            hb = ht.astype(BF16)
            dht = dh_ref[gi]
            dhb = dht.astype(BF16)
            y_off = _dot(cb, hb) * dinx
            dx_off = _dot(bb, dhb) * dstx
            dyd = (dy_all * dinx).astype(BF16)
            xd = (xdt * dstx).astype(BF16)
            dc_acc = _dot(dyd, hb, _NT)
            db_acc = _dot(xd, dhb, _NT)
            dh_ref[gi] = dht * etx_ref[0, 0:1, gi * gw:(gi + 1) * gw] + _dot(cb, dyd, _TN)
            q_cols = _dot((dy_all * y_off).astype(BF16), sel)
            c_cols = _dot((xdt * dx_off).astype(BF16), sel)
            through = _sum0(_dot((dht * ht).astype(BF16), sel))
            dcbt = jnp.zeros((CHUNK, CHUNK), F32)
            for p in range(j_heads // 2):
                lanes = slice(p * CHUNK, (p + 1) * CHUNK)
                out_lanes = slice(gi * gw + p * CHUNK, gi * gw + (p + 1) * CHUNK)
                x2b = xb[:, lanes]
                dy2 = dy_all[:, lanes]
                acc = dx_off[:, lanes]
                for idx, j in enumerate((2 * p, 2 * p + 1)):
                    gc = gcol[:, j * CHUNK:(j + 1) * CHUNK]
                    gr = gamt_ref[pl.ds(gi * j_heads + j, 1), :]
                    decay = jnp.exp(jnp.where(mask, gc - gr, NEG))
                    decay_t = jnp.exp(jnp.where(mask_t, gr - gc, NEG))
                    dy_head = jnp.where(lo if idx == 0 else jnp.logical_not(lo), dy2, 0.0).astype(BF16)
                    acc = acc + _dot((cbt_t * decay_t).astype(BF16), dy_head)
                    dm = decay * _dot(dy_head, x2b, _NT)
                    dcbt = dcbt + dm
                    e = (cbt * dm).astype(BF16)
                    in_lane_j = si == j
                    q_cols = (q_cols + jnp.where(in_lane_j, jnp.sum(e.astype(F32), axis=1, keepdims=True), 0.0)
                              - _dot(e, jnp.where(in_lane_j, 1.0, 0.0).astype(BF16), _TN))
                dxs_ref[:, out_lanes] = (acc * dtx[:, lanes]).astype(dxs_ref.dtype)
                tmp_ref[:, lanes] = acc * x_all[:, lanes]
            dcb = dcbt.astype(BF16)
            dc_ref[:, grp_lanes] = (dc_acc + _dot(dcb, bb)).astype(dc_ref.dtype)
            db_ref[:, grp_lanes] = (db_acc + _dot(dcb, cb, _TN)).astype(db_ref.dtype)
            ddt_ref[:, gi * LANES:(gi + 1) * LANES] = _dot(tmp_ref[...].astype(BF16), sel)
            da_ref[:, gi * LANES:(gi + 1) * LANES] = (
                jnp.dot(incl, q_cols, preferred_element_type=F32, precision=hp)
                + jnp.dot(excl, c_cols, preferred_element_type=F32, precision=hp)
                + through * etg_ref[0, 0:1, gi * LANES:(gi + 1) * LANES])
        _plan_phase(plan, "end", n_steps, plan_ins, plan_outs, plan_sems)

    gn = N_GROUPS * D_STATE
    res = pl.pallas_call(
        kern, name=name, grid=n_steps,
        in_specs=[sp["xs"], sp["b"], sp["c"], sp["xs"], sp["state"], sp["q"], sp["gam_t"], sp["etot_g"], sp["etot_x"],
                  sp["ex"], sp["ex_gam"], sp["ex"], sp["ex"], sp["sel"]] + p_in_specs,
        out_specs=[sp["xs"], sp["grp"], sp["grp"], sp["grp"], sp["grp"]] + p_out_specs,
        out_shape=[jax.ShapeDtypeStruct((s, d_ssm), BF16), jax.ShapeDtypeStruct((s, gn), BF16),
                   jax.ShapeDtypeStruct((s, gn), BF16), jax.ShapeDtypeStruct((s, gn), F32),
                   jax.ShapeDtypeStruct((s, gn), F32)] + p_out_shapes,
        scratch_shapes=[pltpu.VMEM((gps, D_STATE, gw), F32), pltpu.VMEM((CHUNK, gw), F32)] + p_scratch,
        compiler_params=_params(semantics),
    )(xbc_c, xbc_c, xbc_c, dy, hs, q_all, gam_t, etot_g, etot_x, tb["ex_dt"], tb["ex_gam"], tb["ex_din"], tb["ex_dst"],
      tb["sel_cols"], *p_ops)
    return res[:5], list(res[5:])


def _dt_prepare(dt_raw, dt_bias, a_neg, route):
    s, h2 = dt_raw.shape
    n_heads = h2 // 2
    qw = route.shape[1]

    def kern(raw_ref, bias_ref, a_ref, route_ref, dt_ref, q_ref, gamt_ref, etot_ref):
        dt = _softplus(raw_ref[...] + bias_ref[...])
        a = dt * a_ref[...]
        li = lax.broadcasted_iota(jnp.int32, (CHUNK, CHUNK), 0)
        si = lax.broadcasted_iota(jnp.int32, (CHUNK, CHUNK), 1)
        tri = (li >= si).astype(F32)
        cs = jnp.dot(tri, a, preferred_element_type=F32, precision=lax.Precision.HIGHEST)
        tot = _sum0(a)
        fwd = lax.broadcasted_iota(jnp.int32, (CHUNK, h2), 1) < n_heads
        gam = jnp.where(fwd, cs, a - cs)
        din = jnp.where(fwd, jnp.exp(cs), jnp.exp(tot + gam))
        dst = jnp.where(fwd, jnp.exp(tot - cs), jnp.exp(cs - a))
        pieces = []
        for v in (dt, gam, din, dst):
            hi = v.astype(BF16)
            rest = v - hi.astype(F32)
            mid = rest.astype(BF16)
            pieces += [hi, mid, (rest - mid.astype(F32)).astype(BF16)]
        q_ref[...] = _dot(jnp.concatenate(pieces, axis=1), route_ref[...]).astype(BF16)
        dt_ref[...] = dt
        gamt_ref[...] = gam.T
        etot_ref[...] = jnp.broadcast_to(jnp.exp(tot), (8, h2))

    nc = s // CHUNK
    rows = lambda w: pl.BlockSpec((CHUNK, w), lambda i: (i, 0))
    whole = lambda a: pl.BlockSpec(a.shape, lambda i: (0, 0))
    return pl.pallas_call(
        kern, name="dt_prepare", grid=(nc,),
        in_specs=[rows(h2), whole(dt_bias), whole(a_neg), whole(route)],
        out_specs=[rows(h2), rows(qw), pl.BlockSpec((h2, CHUNK), lambda i: (0, i)), pl.BlockSpec((8, h2), lambda i: (i, 0))],
        out_shape=[jax.ShapeDtypeStruct((s, h2), F32), jax.ShapeDtypeStruct((s, qw), BF16),
                   jax.ShapeDtypeStruct((h2, s), F32), jax.ShapeDtypeStruct((nc * 8, h2), F32)],
        compiler_params=_params(("parallel",)),
    )(dt_raw, dt_bias, a_neg, route)


def _from_group_lanes(arr, j_heads):
    s = arr.shape[0]
    return arr.reshape(s, N_GROUPS, LANES)[:, :, :j_heads].reshape(s, N_GROUPS * j_heads)


def _dt_backward(da, ddt, dt, dt_raw, dt_bias, a_neg):
    h2 = da.shape[1]

    def body(dav, ddtv, dtv, raw, bias, a_head):
        draw = (ddtv + dav * a_head) * _sigmoid(raw + bias)
        return draw, _sum0(draw), _sum0(dav * dtv) * a_head
    return _rowwise("dt_backward", body, [da, ddt, dt, dt_raw], [dt_bias, a_neg], [(h2, BF16)], [h2, h2])


def _rms(x):
    r = lax.rsqrt(_mean1(x * x) + EPS)
    return x * r, r


def _rms_bwd(dy, y, r):
    return r * (dy - y * _mean1(dy * y))


def _norm_mod_fwd(name, x, g, sc, sh):
    def body(xv, gv, scv, shv):
        y, _ = _rms(xv)
        return ((y * gv) * (1.0 + scv) + shv,)
    return _rowwise(name, body, [x], [g, sc, sh], [(x.shape[1], BF16)])[0]


def _norm_mod_bwd(name, x, dh, dpass, g, sc, plan=None):
    d = x.shape[1]

    def body(xv, dhv, dpv, gv, scv):
        y, r = _rms(xv)
        dn = dhv * (1.0 + scv)
        dx = _rms_bwd(dn * gv, y, r) + dpv
        return dx, _sum0(dn * y), _sum0(dhv * (y * gv)), _sum0(dhv)
    return _rowwise(name, body, [x, dh, dpass], [g, sc], [(d, F32)], [d, d, d], plan=plan)


def _gated_residual_fwd(name, x, m, gate, gp):
    def body(xv, mv, gatev, gpv):
        y, _ = _rms(mv)
        return (xv + gatev * (y * gpv),)
    return _rowwise(name, body, [x, m], [gate, gp], [(x.shape[1], F32)])[0]


def _gated_residual_bwd(name, m, dx1, gate, gp):
    d = m.shape[1]

    def body(mv, dv, gatev, gpv):
        y, r = _rms(mv)
        dn = dv * gatev
        return _rms_bwd(dn * gpv, y, r), _sum0(dv * (y * gpv)), _sum0(dn * y)
    return _rowwise(name, body, [m, dx1], [gate, gp], [(d, BF16)], [d, d])


def _final_residual_loss(x1, f, tgt, gate, gp):
    d = x1.shape[1]

    def body(xv, fv, tv, gatev, gpv):
        y, r = _rms(fv)
        n = y * gpv
        err = xv + gatev * n - tv
        dx2 = err * (1.0 / d)
        dn = dx2 * gatev
        sq = jnp.sum(_sum0(err * err), axis=1, keepdims=True)
        return dx2, _rms_bwd(dn * gpv, y, r), jnp.broadcast_to(sq, (1, LANES)), _sum0(dx2 * n), _sum0(dn * y)
    return _rowwise("final_residual_loss", body, [x1, f, tgt], [gate, gp], [(d, F32), (d, BF16)], [LANES, d, d])


def _swiglu_fwd(gu):
    f = gu.shape[1] // 2

    def body(v):
        return (_silu(v[:, :f]) * v[:, f:],)
    return _rowwise("swiglu_fwd", body, [gu], [], [(f, BF16)])[0]


def _swiglu_bwd(gu, dact):
    f = gu.shape[1] // 2

    def body(v, dv):
        gt, up = v[:, :f], v[:, f:]
        return (jnp.concatenate([dv * up * _dsilu(gt), dv * _silu(gt)], axis=1),)
    return _rowwise("swiglu_bwd", body, [gu, dact], [], [(2 * f, BF16)])[0]


def _glu_fwd(glu_in, b_glu):
    c = glu_in.shape[1] // 2

    def body(v, bv):
        t = v + bv
        return (t[:, :c] * _sigmoid(t[:, c:]),)
    return _rowwise("glu_fwd", body, [glu_in], [b_glu], [(c, F32)])[0]


def _glu_bwd(glu_in, b_glu, du0):
    c = glu_in.shape[1] // 2

    def body(v, dv, bv):
        t = v + bv
        a, s = t[:, :c], _sigmoid(t[:, c:])
        dg = jnp.concatenate([dv * s, dv * a * s * (1.0 - s)], axis=1)
        return dg, _sum0(dg)
    return _rowwise("glu_bwd", body, [glu_in, du0], [b_glu], [(2 * c, BF16)], [2 * c])


def _ln_parts(u1):
    xc = u1 - _mean1(u1)
    r = lax.rsqrt(_mean1(xc * xc) + EPS)
    return xc * r, r


def _ln_silu_fwd(u1, ln_g, ln_b):
    def body(v, gv, bv):
        yh, _ = _ln_parts(v)
        return (_silu(yh * gv + bv),)
    return _rowwise("ln_silu_fwd", body, [u1], [ln_g, ln_b], [(u1.shape[1], BF16)])[0]


def _ln_silu_bwd(u1, du, ln_g, ln_b):
    d = u1.shape[1]

    def body(v, dv, gv, bv):
        yh, r = _ln_parts(v)
        dl = dv * _dsilu(yh * gv + bv)
        dyh = dl * gv
        du1 = r * (dyh - _mean1(dyh) - yh * _mean1(dyh * yh))
        return du1, _sum0(dl * yh), _sum0(dl)
    return _rowwise("ln_silu_bwd", body, [u1, du], [ln_g, ln_b], [(d, F32)], [d, d])


def _gate_merge_fwd(y_a, y_b, gl, b_gate):
    d = y_a.shape[1]

    def body(ya, yb, glv, bv):
        s = _sigmoid(glv + bv)
        return (s[:, :d] * ya + s[:, d:] * yb,)
    return _rowwise("gate_merge_fwd", body, [y_a, y_b, gl], [b_gate], [(d, BF16)])[0]


def _gate_merge_bwd(dmixin, y_a, y_b, gl, b_gate):
    d = y_a.shape[1]

    def body(dv, ya, yb, glv, bv):
        s = _sigmoid(glv + bv)
        sa, sb = s[:, :d], s[:, d:]
        dya, dyb = dv * sa, dv * sb
        dgl = jnp.concatenate([dv * ya * sa * (1.0 - sa), dv * yb * sb * (1.0 - sb)], axis=1)
        return dya, dyb, dgl, _sum0(dgl), _sum0(dyb)
    return _rowwise("gate_merge_bwd", body, [dmixin, y_a, y_b, gl], [b_gate],
                    [(d, BF16), (d, BF16), (2 * d, BF16)], [2 * d, d])


def _group_slices(d_ssm):
    gw = d_ssm // N_GROUPS
    return [slice(g * gw, (g + 1) * gw) for g in range(N_GROUPS)]


def _gated_norm_fwd(y_f, y_b, xbc_c, z, d_skip_x, g_ssm):
    d_ssm = y_f.shape[1]

    def body(yf, yb, xs, zv, dsk, gv):
        y = yf + yb + dsk * xs
        v = y * _silu(zv)
        outs = []
        for sl in _group_slices(d_ssm):
            w, _ = _rms(v[:, sl])
            outs.append(w)
        return y, jnp.concatenate(outs, axis=1) * gv
    return _rowwise("gated_norm_fwd", body, [y_f, y_b, (xbc_c, d_ssm, 0), z], [d_skip_x, g_ssm],
                    [(d_ssm, F32), (d_ssm, BF16)])


def _gated_norm_bwd(y, z, dyn, xbc_c, d_skip_x, g_ssm):
    d_ssm = y.shape[1]

    def body(yv, zv, dv, xs, dsk, gv):
        sz = _silu(zv)
        v = yv * sz
        dw = dv * gv
        dvs, ws = [], []
        for sl in _group_slices(d_ssm):
            w, r = _rms(v[:, sl])
            ws.append(w)
            dvs.append(_rms_bwd(dw[:, sl], w, r))
        dvv = jnp.concatenate(dvs, axis=1)
        dy = dvv * sz
        return dy, dvv * yv * _dsilu(zv), _sum0(dv * jnp.concatenate(ws, axis=1)), _sum0(dy * xs)
    return _rowwise("gated_norm_bwd", body, [y, z, dyn, (xbc_c, d_ssm, 0)], [d_skip_x, g_ssm],
                    [(d_ssm, BF16), (d_ssm, BF16)], [d_ssm, d_ssm])


def _ssd_grad_merge(dxs_f, dxs_b, dy, db_f, db_b, dc_f, dc_b, d_skip_x):
    d_ssm = dy.shape[1]
    width = d_ssm + 2 * N_GROUPS * D_STATE

    def body(xf, xb, dv, bf, bb, cf, cbv, dsk):
        return (jnp.concatenate([xf + xb + dsk * dv, bf + bb, cf + cbv], axis=1),)
    return _rowwise("ssd_grad_merge", body, [dxs_f, dxs_b, dy, db_f, db_b, dc_f, dc_b], [d_skip_x], [(width, BF16)])[0]


def _adamw(name, w, g, m, v):
    c = w.shape[1]
    c1 = 1.0 - ADAM_B1 ** ADAM_STEP
    c2 = 1.0 - ADAM_B2 ** ADAM_STEP

    def body(wv, gv, mv, vv):
        mn = ADAM_B1 * mv + (1.0 - ADAM_B1) * gv
        vn = ADAM_B2 * vv + (1.0 - ADAM_B2) * (gv * gv)
        delta = -ADAM_LR * ((mn / c1) / (jnp.sqrt(vn / c2) + ADAM_EPS) + ADAM_WD * wv)
        return delta, mn, vn
    return _rowwise(name, body, [w, g, m, v], [], [(c, F32)] * 3)


def _local_step(x, tgt, mod, wts, sm, late=None):
    s, d = x.shape
    d_ssm = 2 * d
    n_heads = d_ssm // HEAD_DIM
    d_xbc = d_ssm + 2 * N_GROUPS * D_STATE
    sec = [0, d_ssm, d_ssm + d_xbc, d_ssm + d_xbc + 2 * n_heads, d_ssm + d_xbc + 2 * n_heads + 2 * d]
    sec.append(sec[-1] + 2 * d)
    sh1, sc1, g1, sh2, sc2, g2 = [mod[:, i * d:(i + 1) * d] for i in range(N_MOD)]
    win_t = wts["w_in_t"]
    sections = [(nm, (sec[i], sec[i + 1] - sec[i])) for i, nm in enumerate(("z", "xbc", "dt", "glu", "gate"))]

    h1 = _norm_mod_fwd("pre_mix_norm", x, sm["g_pre_mix"], sc1, sh1)
    z, xbc, dt_raw, glu_in, gate_l = [
        _matmul(f"proj_{nm}", h1, win_t, tb=True, b_rows=rows, out_dtype=F32 if nm == "dt" else BF16)
        for nm, rows in sections]
    plan_c = plan_f = plan_r = None
    if late is not None:
        plan_c = _gather_plan([late[n] for n in MATRICES[1:4]])
        plan_f = _gather_plan([late["w_down"]])
        plan_r = _gather_plan([late["w_gate_up"]])
    xbc_c = _dwconv_fwd("ssm_conv_fwd", xbc, sm["w_conv_ssm"], sm["b_conv_ssm"], silu=True, out_dtype=BF16, plan=plan_c)
    if late is not None:
        xbc_c, got_c = xbc_c
    tables = _scan_tables(n_heads)
    dt, q_all, gam_t, etot = _dt_prepare(dt_raw, sm["dt_bias"], sm["a_neg"], tables["route"])
    etot = etot.reshape(s // CHUNK, 8, 2 * n_heads)
    (etx_f, etg_f), (etx_r, etg_r) = [_chunk_decay(etot, n_heads, direction) for direction in (0, 1)]
    y_f, hs_f, got_f = _scan_fwd("ssd_fwd_f", xbc_c, q_all, gam_t, etx_f, tables, direction=0, d_ssm=d_ssm, plan=plan_f)
    y_r, hs_r, got_r = _scan_fwd("ssd_fwd_r", xbc_c, q_all, gam_t, etx_r, tables, direction=1, d_ssm=d_ssm, plan=plan_r)
    if late is not None:
        wts = dict(wts, w_ssm_out=got_c[0].reshape(-1, d), w_conv_out=got_c[1].reshape(-1, d),
                   w_mix_out=got_c[2].reshape(-1, d), w_down=got_f[0].reshape(-1, d), w_gate_up=got_r[0])
    y_ssd, yn = _gated_norm_fwd(y_f, y_r, xbc_c, z, sm["d_skip_x"], sm["g_ssm_norm"])
    y_a = _matmul("ssm_out", yn, wts["w_ssm_out"])
    u0 = _glu_fwd(glu_in, sm["b_glu"])
    u1 = _dwconv_fwd("dw_conv_fwd", u0, sm["w_dw"], sm["b_dw"], silu=False)
    u = _ln_silu_fwd(u1, sm["ln_g"], sm["ln_b"])
    y_b = _matmul("conv_out", u, wts["w_conv_out"], bias=sm["b_conv_out"])
    mixin = _gate_merge_fwd(y_a, y_b, gate_l, sm["b_gate"])
    mix = _matmul("mix_out", mixin, wts["w_mix_out"])
    x1 = _gated_residual_fwd("post_mix_residual", x, mix, g1, sm["g_post_mix"])
    h2 = _norm_mod_fwd("pre_ffn_norm", x1, sm["g_pre_ffn"], sc2, sh2)
    gu = _matmul("ffn_gate_up", h2, wts["w_gate_up"], b_blocks=N_CHIPS, out_dtype=BF16)
    act = _swiglu_fwd(gu)
    f = _matmul("ffn_down", act, wts["w_down"])

    dx2, df, sq, d_g2, d_gpf = _final_residual_loss(x1, f, tgt, g2, sm["g_post_ffn"])
    dact = _matmul("d_act", df, wts["w_down"], tb=True, out_dtype=BF16)
    g_w_down = _matmul("g_w_down", act, df, ta=True, out_dtype=BF16)
    dgu = _swiglu_bwd(gu, dact)
    dh2 = _matmul("d_h2", dgu, wts["w_gate_up"], tb=True, b_blocks=N_CHIPS)
    g_w_gate_up = _matmul("g_w_gate_up", h2, dgu, ta=True, out_dtype=BF16, out_blocks=N_CHIPS)
    dx1, d_gpre_ffn, d_sc2, d_sh2 = _norm_mod_bwd("pre_ffn_norm_bwd", x1, dh2, dx2, sm["g_pre_ffn"], sc2)
    dmix, d_g1, d_gpm = _gated_residual_bwd("post_mix_residual_bwd", mix, dx1, g1, sm["g_post_mix"])
    dmixin = _matmul("d_mixin", dmix, wts["w_mix_out"], tb=True, out_dtype=BF16)
    g_w_mix = _matmul("g_w_mix_out", mixin, dmix, ta=True, out_dtype=BF16)
    dy_a, dy_b, dgate_l, d_bgate, d_bco = _gate_merge_bwd(dmixin, y_a, y_b, gate_l, sm["b_gate"])
    du = _matmul("d_u", dy_b, wts["w_conv_out"], tb=True, out_dtype=BF16)
    g_w_co = _matmul("g_w_conv_out", u, dy_b, ta=True, out_dtype=BF16)
    du1, d_lng, d_lnb = _ln_silu_bwd(u1, du, sm["ln_g"], sm["ln_b"])
    du0, d_wdw, d_bdw = _dwconv_bwd("dw_conv_bwd", u0, sm["w_dw"], sm["b_dw"], du1, silu=False)
    dglu, d_bglu = _glu_bwd(glu_in, sm["b_glu"], du0)
    dyn = _matmul("d_yn", dy_a, wts["w_ssm_out"], tb=True, out_dtype=BF16)
    g_w_ssm = _matmul("g_w_ssm_out", yn, dy_a, ta=True, out_dtype=BF16)
    dy_ssd, dz, d_gssm, d_dskip_x = _gated_norm_bwd(y_ssd, z, dyn, xbc_c, sm["d_skip_x"], sm["g_ssm_norm"])
    early = [g_w_ssm.reshape(N_CHIPS, -1, d), g_w_co.reshape(N_CHIPS, -1, d), g_w_mix.reshape(N_CHIPS, -1, d),
             g_w_gate_up, g_w_down.reshape(N_CHIPS, -1, d)]
    plan_b = sums = None
    if late is not None:
        sums = _chip_sums("early", early)
        plan_b = _send_chips_plan(sums)
    (dxs_f, db_f, dc_f, ddt_f, da_f), received = _scan_bwd(
        "ssd_bwd_f", xbc_c, dy_ssd, hs_f, q_all, gam_t, etx_f, etg_f, tables, direction=0, d_ssm=d_ssm, plan=plan_b)
    (dxs_r, db_r, dc_r, ddt_r, da_r), _ = _scan_bwd(
        "ssd_bwd_r", xbc_c, dy_ssd, hs_r, q_all, gam_t, etx_r, etg_r, tables, direction=1, d_ssm=d_ssm)
    j_heads = n_heads // N_GROUPS
    da = jnp.concatenate([_from_group_lanes(da_f, j_heads), _from_group_lanes(da_r, j_heads)], axis=1)
    ddt = jnp.concatenate([_from_group_lanes(ddt_f, j_heads), _from_group_lanes(ddt_r, j_heads)], axis=1)
    ddt_raw, d_dtbias, d_alog = _dt_backward(da, ddt, dt, dt_raw, sm["dt_bias"], sm["a_neg"])
    dxbc_c = _ssd_grad_merge(dxs_f, dxs_r, dy_ssd, db_f, db_r, dc_f, dc_r, sm["d_skip_x"])
    dxbc, d_wconv, d_bconv = _dwconv_bwd("ssm_conv_bwd", xbc, sm["w_conv_ssm"], sm["b_conv_ssm"], dxbc_c,
                                         silu=True, dx_dtype=BF16)
    dsecs = [dz, dxbc, ddt_raw, dglu, dgate_l]
    dh1 = g_win = None
    for (nm, rows), dsec in zip(sections, dsecs):
        dh1 = _matmul(f"d_h1_{nm}", dsec, win_t, b_rows=rows, add=dh1)
        g_win = _matmul(f"g_w_in_{nm}", dsec, h1, ta=True, out_dtype=BF16, out_rows=(rows[0], sec[-1], g_win))
    plan_in = sums_in = None
    if late is not None:
        sums_in = _chip_sums("w_in", [g_win.reshape(N_CHIPS, -1, d)])
        plan_in = _send_chips_plan(sums_in)
    grad_x, d_gpre_mix, d_sc1, d_sh1, *received_in = _norm_mod_bwd(
        "pre_mix_norm_bwd", x, dh1, dx1, sm["g_pre_mix"], sc1, plan=plan_in)

    dmod = jnp.concatenate([d_sh1, d_sc1, d_g1, d_sh2, d_sc2, d_g2], axis=1)
    if late is None:
        big = {"w_in_t": g_win, "w_ssm_out": g_w_ssm, "w_conv_out": g_w_co, "w_mix_out": g_w_mix,
               "w_gate_up": g_w_gate_up, "w_down": g_w_down}
    else:
        big = {"pending": (sums_in + sums, received_in + received)}
    small = {"g_pre_mix": d_gpre_mix, "g_post_mix": d_gpm, "w_conv_ssm": d_wconv, "b_conv_ssm": d_bconv,
             "dt_bias": d_dtbias, "a_log": d_alog, "d_skip_x": d_dskip_x, "g_ssm_norm": d_gssm, "b_glu": d_bglu,
             "w_dw": d_wdw, "b_dw": d_bdw, "ln_g": d_lng, "ln_b": d_lnb, "b_conv_out": d_bco, "b_gate": d_bgate,
             "g_pre_ffn": d_gpre_ffn, "g_post_ffn": d_gpf}
    return sq, grad_x, big, small, dmod


ANY = pl.BlockSpec(memory_space=pl.ANY)
WHOLE_VMEM = pl.BlockSpec(memory_space=pltpu.VMEM)


def _mesh_place():
    x, y, c = lax.axis_index("x"), lax.axis_index("y"), lax.axis_index("c")
    other_chips = [(1 - x, y), (x, 1 - y), (1 - x, 1 - y)]
    return x, y, c, other_chips


def _remote(src, dst, send_sems, recv_sems, k, device):
    return pltpu.make_async_remote_copy(src_ref=src, dst_ref=dst, send_sem=send_sems.at[k], recv_sem=recv_sems.at[k],
                                        device_id=device, device_id_type=MESH)


def _gather_devices(name, block):
    m_per, n = block.shape

    def body(x_ref, out_ref, send_sems, recv_sems, local_sem):
        x, y, c, chips = _mesh_place()
        me, sibling = (x, y, c), (x, y, 1 - c)

        def rows(px, py, pc):
            return out_ref.at[pl.ds((4 * px + 2 * py + pc) * m_per, m_per), :]

        def copy(k, blk, to, src=None):
            return _remote(rows(*blk) if src is None else src, rows(*blk), send_sems, recv_sems, k, to)

        mine = pltpu.make_async_copy(x_ref, rows(*me), local_sem)
        mine.start()
        first = [copy(0, me, sibling, src=x_ref)]
        first += [copy(1 + j, me, (*chip, c), src=x_ref) for j, chip in enumerate(chips)]
        for cp in first:
            cp.start()
        passed = [copy(4 + j, (*chip, c), sibling) for j, chip in enumerate(chips)]
        for j, chip in enumerate(chips):
            copy(1 + j, (*chip, c), me).wait_recv()
            passed[j].start()
        copy(0, sibling, me).wait_recv()
        for j, chip in enumerate(chips):
            copy(4 + j, (*chip, 1 - c), me).wait_recv()
        for cp in first + passed:
            cp.wait_send()
        mine.wait()

    return pl.pallas_call(
        body, name=name, out_shape=jax.ShapeDtypeStruct((N_DEV * m_per, n), block.dtype),
        in_specs=[WHOLE_VMEM], out_specs=WHOLE_VMEM,
        scratch_shapes=[pltpu.SemaphoreType.DMA((7,)), pltpu.SemaphoreType.DMA((7,)), pltpu.SemaphoreType.DMA],
        compiler_params=pltpu.CompilerParams(vmem_limit_bytes=VMEM_LIMIT),
    )(block)


class _Plan:
    def __init__(self, operands, out_shapes, copies, phases):
        self.operands, self.out_shapes, self.copies, self.phases = list(operands), list(out_shapes), copies, phases

    def sems(self):
        return [pltpu.SemaphoreType.DMA((self.copies,)), pltpu.SemaphoreType.DMA((self.copies,))]


def _run_plan(name, plan):
    n_in, n_out = len(plan.operands), len(plan.out_shapes)

    def body(*refs):
        ins, outs = refs[:n_in], refs[n_in:n_in + n_out]
        send_sems, recv_sems = refs[n_in + n_out:]
        for phase in ("start", "middle", "end"):
            if phase in plan.phases:
                plan.phases[phase](ins, outs, send_sems, recv_sems)

    return list(pl.pallas_call(body, name=name, out_shape=plan.out_shapes, in_specs=[ANY] * n_in,
                               out_specs=[ANY] * n_out, scratch_shapes=plan.sems())(*plan.operands))


def _gather_plan(shards):
    n = len(shards)

    def copies(kinds, ins, outs, send_sems, recv_sems):
        x, y, c, chips = _mesh_place()
        me = 2 * x + y
        sibling = (x, y, 1 - c)

        def half(i, h):
            hr = ins[i].shape[0] // 2
            return pl.ds(h * hr, hr)

        def block(i, j, h):
            cx, cy = chips[j]
            return outs[i].at[2 * cx + cy, half(i, h)]

        make = {
            "over_ici": lambda i, j: _remote(ins[i].at[half(i, c)], outs[i].at[me, half(i, c)], send_sems, recv_sems,
                                             6 * i + j, (*chips[j], c)),
            "arrived": lambda i, j: _remote(block(i, j, c), block(i, j, c), send_sems, recv_sems, 6 * i + j, (*chips[j], c)),
            "passed_on": lambda i, j: _remote(block(i, j, c), block(i, j, c), send_sems, recv_sems, 6 * i + 3 + j, sibling),
            "from_sibling": lambda i, j: _remote(block(i, j, 1 - c), block(i, j, 1 - c), send_sems, recv_sems,
                                                 6 * i + 3 + j, sibling),
        }
        res = []
        for kind in kinds:
            if kind == "own":
                res.append([_remote(ins[i], outs[i].at[me], send_sems, recv_sems, 6 * n + i, sibling) for i in range(n)])
            else:
                res.append([make[kind](i, j) for i in range(n) for j in range(3)])
        return res

    def start(*refs):
        over_ici, own = copies(("over_ici", "own"), *refs)
        for cp in over_ici + own:
            cp.start()

    def middle(*refs):
        arrived, passed_on = copies(("arrived", "passed_on"), *refs)
        for got, fwd in zip(arrived, passed_on):
            got.wait_recv()
            fwd.start()

    def end(*refs):
        from_sibling, own_in = copies(("from_sibling", "own"), *refs)
        for cp in from_sibling + own_in:
            cp.wait_recv()
        over_ici, passed_on, own_out = copies(("over_ici", "passed_on", "own"), *refs)
        for cp in over_ici + passed_on + own_out:
            cp.wait_send()

    return _Plan(shards, [jax.ShapeDtypeStruct((N_CHIPS,) + s.shape, s.dtype) for s in shards], 7 * n,
                 {"start": start, "middle": middle, "end": end})


def _send_sibling_halves(name, grads):
    n = len(grads)

    def body(*refs):
        ins, outs = refs[:n], refs[n:2 * n]
        send_sems, recv_sems = refs[2 * n:]
        x, y, c, _ = _mesh_place()
        sibling = (x, y, 1 - c)
        copies = []
        for i in range(n):
            for j in range(N_CHIPS):
                copies.append(_remote(ins[i].at[j, 1 - c], outs[i].at[j], send_sems, recv_sems, N_CHIPS * i + j, sibling))
                copies[-1].start()
        for cp in copies:
            cp.wait_recv()
        for cp in copies:
            cp.wait_send()

    return pl.pallas_call(
        body, name=name,
        out_shape=[jax.ShapeDtypeStruct((g.shape[0],) + g.shape[2:], g.dtype) for g in grads],
        in_specs=[ANY] * n, out_specs=[ANY] * n,
        scratch_shapes=[pltpu.SemaphoreType.DMA((N_CHIPS * n,)), pltpu.SemaphoreType.DMA((N_CHIPS * n,))],
    )(*grads)


def _send_chips_plan(sums):
    n = len(sums)

    def copies(ins, outs, send_sems, recv_sems):
        x, y, c, chips = _mesh_place()
        return [_remote(ins[i].at[2 * cx + cy], outs[i].at[j], send_sems, recv_sems, 3 * i + j, (cx, cy, c))
                for i in range(n) for j, (cx, cy) in enumerate(chips)]

    def start(*refs):
        for cp in copies(*refs):
            cp.start()

    def end(*refs):
        for cp in copies(*refs):
            cp.wait_recv()
        for cp in copies(*refs):
            cp.wait_send()

    return _Plan(sums, [jax.ShapeDtypeStruct((3,) + g.shape[1:], g.dtype) for g in sums], 3 * n,
                 {"start": start, "end": end})


def _exchange_halves(name, shards):
    n = len(shards)

    def body(*refs):
        outs = refs[n:2 * n]
        send_sems, recv_sems = refs[2 * n:]
        x, y, c, _ = _mesh_place()
        sibling = (x, y, 1 - c)
        remote = [_remote(outs[i].at[c], outs[i].at[c], send_sems, recv_sems, i, sibling) for i in range(n)]
        for cp in remote:
            cp.start()
        for i in range(n):
            _remote(outs[i].at[1 - c], outs[i].at[1 - c], send_sems, recv_sems, i, sibling).wait_recv()
        for cp in remote:
            cp.wait_send()

    return pl.pallas_call(
        body, name=name,
        out_shape=[jax.ShapeDtypeStruct(h.shape, h.dtype) for h in shards],
        in_specs=[ANY] * n, out_specs=[ANY] * n, input_output_aliases={i: i for i in range(n)},
        scratch_shapes=[pltpu.SemaphoreType.DMA((n,)), pltpu.SemaphoreType.DMA((n,))],
    )(*shards)


def _divisor_tile(rows, row_bytes, quantum=16):
    best = rows
    for t in range(quantum, rows + 1, quantum):
        if rows % t == 0 and 2 * t * row_bytes <= ROW_TILE_BUDGET:
            best = t
    return best


def _add_sibling(name, g4, t1):
    nb, _, hr, cols = g4.shape
    t = _divisor_tile(hr, cols * 6)

    def kern(g_ref, t_ref, o_ref):
        o_ref[0] = (g_ref[0, 0].astype(F32) + t_ref[0].astype(F32)).astype(o_ref.dtype)

    return pl.pallas_call(
        kern, name=name, grid=(nb, hr // t),
        in_specs=[pl.BlockSpec((1, 1, t, cols), lambda j, i: (j, lax.axis_index("c"), i, 0)),
                  pl.BlockSpec((1, t, cols), lambda j, i: (j, i, 0))],
        out_specs=pl.BlockSpec((1, t, cols), lambda j, i: (j, i, 0)),
        out_shape=jax.ShapeDtypeStruct((nb, hr, cols), g4.dtype),
        compiler_params=_params(("parallel", "parallel")),
    )(g4, t1)


def _add_chips(name, s1, t3):
    _, hr, cols = s1.shape
    t = _divisor_tile(hr, cols * 12)

    def kern(s_ref, t_ref, o_ref):
        acc = s_ref[0].astype(F32)
        for j in range(3):
            acc = acc + t_ref[j].astype(F32)
        o_ref[0] = acc

    return pl.pallas_call(
        kern, name=name, grid=(hr // t,),
        in_specs=[pl.BlockSpec((1, t, cols), lambda i: (2 * lax.axis_index("x") + lax.axis_index("y"), i, 0)),
                  pl.BlockSpec((3, t, cols), lambda i: (0, i, 0))],
        out_specs=pl.BlockSpec((1, t, cols), lambda i: (lax.axis_index("c"), i, 0)),
        out_shape=jax.ShapeDtypeStruct((2, hr, cols), F32),
        compiler_params=_params(("parallel",)),
    )(s1, t3)


def _chip_sums(tag, grads):
    g4 = [g.reshape(N_CHIPS, 2, g.shape[1] // 2, g.shape[2]) for g in grads]
    t1 = _send_sibling_halves("grads_to_sibling_" + tag, g4)
    return [_add_sibling(f"chip_sum_{tag}_{i}", g, t) for i, (g, t) in enumerate(zip(g4, t1))]


def _shard_sums(sums, received):
    halves = [_add_chips(f"shard_sum_{i}", s, t) for i, (s, t) in enumerate(zip(sums, received))]
    full = _exchange_halves("grad_halves_to_sibling", halves)
    return [f.reshape(f.shape[1] * 2, f.shape[2]) for f in full]


def _pack_rows(size, width):
    return -(-size // (8 * width)) * 8


def _pack(arrays, width):
    parts = []
    for a in arrays:
        flat = a.reshape(-1).astype(F32)
        rows = _pack_rows(flat.shape[0], width)
        parts.append(jnp.pad(flat, (0, rows * width - flat.shape[0])).reshape(rows, width))
    return jnp.concatenate(parts, axis=0)


def _unpack(block, shapes, width):
    out, r = [], 0
    for shp in shapes:
        size = 1
        for s_ in shp:
            size *= s_
        rows = _pack_rows(size, width)
        out.append(block[r:r + rows].reshape(-1)[:size].reshape(shp))
        r += rows
    return out


SMALL_PARAMS = ("b_ada", "g_pre_mix", "g_post_mix", "b_conv_ssm", "dt_bias_fwd", "dt_bias_bwd", "a_log_fwd", "a_log_bwd",
                "d_skip", "g_ssm_norm", "b_glu", "b_dw", "ln_g", "ln_b", "b_conv_out", "b_gate", "g_pre_ffn", "g_post_ffn")
SHARDED_SMALL = ("w_conv_ssm", "w_dw")
MATRICES = ("w_in", "w_ssm_out", "w_conv_out", "w_mix_out", "w_gate_up", "w_down")
ALL_PARAMS = ("w_ada", "b_ada", "g_pre_mix", "g_post_mix", "w_in", "w_conv_ssm", "b_conv_ssm", "dt_bias_fwd", "dt_bias_bwd",
              "a_log_fwd", "a_log_bwd", "d_skip", "g_ssm_norm", "w_ssm_out", "b_glu", "w_dw", "b_dw", "ln_g", "ln_b",
              "w_conv_out", "b_conv_out", "b_gate", "w_mix_out", "g_pre_ffn", "g_post_ffn", "w_gate_up", "w_down")
COND_ROWS = 48
COND_CONV_ROW = 8
COND_DW_ROW = 16
MOD_ROWS = 16


def kernel(x, c, w_ada, b_ada, g_pre_mix, g_post_mix, w_in, w_conv_ssm, b_conv_ssm, dt_bias_fwd, dt_bias_bwd, a_log_fwd, a_log_bwd, d_skip, g_ssm_norm, w_ssm_out, b_glu, w_dw, b_dw, ln_g, ln_b, w_conv_out, b_conv_out, b_gate, w_mix_out, g_pre_ffn, g_post_ffn, w_gate_up, w_down, loss_target, m_w_ada, m_b_ada, m_g_pre_mix, m_g_post_mix, m_w_in, m_w_conv_ssm, m_b_conv_ssm, m_dt_bias_fwd, m_dt_bias_bwd, m_a_log_fwd, m_a_log_bwd, m_d_skip, m_g_ssm_norm, m_w_ssm_out, m_b_glu, m_w_dw, m_b_dw, m_ln_g, m_ln_b, m_w_conv_out, m_b_conv_out, m_b_gate, m_w_mix_out, m_g_pre_ffn, m_g_post_ffn, m_w_gate_up, m_w_down, v_w_ada, v_b_ada, v_g_pre_mix, v_g_post_mix, v_w_in, v_w_conv_ssm, v_b_conv_ssm, v_dt_bias_fwd, v_dt_bias_bwd, v_a_log_fwd, v_a_log_bwd, v_d_skip, v_g_ssm_norm, v_w_ssm_out, v_b_glu, v_w_dw, v_b_dw, v_ln_g, v_ln_b, v_w_conv_out, v_b_conv_out, v_b_gate, v_w_mix_out, v_g_pre_ffn, v_g_post_ffn, v_w_gate_up, v_w_down):
    given = dict(locals())
    wgt = {n: given[n][0] for n in ALL_PARAMS}
    mom = {n: given["m_" + n][0] for n in ALL_PARAMS}
    var = {n: given["v_" + n][0] for n in ALL_PARAMS}
    xs, tgt = x[0], loss_target[0]
    s, d = xs.shape
    d_ssm = 2 * d
    n_heads = d_ssm // HEAD_DIM
    d_xbc = d_ssm + 2 * N_GROUPS * D_STATE
    xi, yi, ci = lax.axis_index("x"), lax.axis_index("y"), lax.axis_index("c")
    chip = 2 * xi + yi
    dev = 2 * chip + ci
    k_conv, k_dw = wgt["w_conv_ssm"].shape[0], wgt["w_dw"].shape[0]
    xbc_shard, dw_shard = d_xbc // N_CHIPS, d // N_CHIPS

    width1 = max(d, xbc_shard)
    blk = jnp.zeros((COND_ROWS, width1), F32)
    blk = blk.at[0, :d].set(c[0])
    blk = blk.at[COND_CONV_ROW:COND_CONV_ROW + k_conv, :xbc_shard].set(wgt["w_conv_ssm"])
    blk = blk.at[COND_DW_ROW:COND_DW_ROW + k_dw, :dw_shard].set(wgt["w_dw"])
    g1 = _gather_devices("gather_cond", blk).reshape(N_DEV, COND_ROWS, width1)
    c_all = g1[:, 0, :d]
    w_conv_full = jnp.concatenate([g1[2 * k, COND_CONV_ROW:COND_CONV_ROW + k_conv, :xbc_shard] for k in range(N_CHIPS)], axis=1)
    w_dw_full = jnp.concatenate([g1[2 * k, COND_DW_ROW:COND_DW_ROW + k_dw, :dw_shard] for k in range(N_CHIPS)], axis=1)
    c_act = jnp.pad(c_all * _sigmoid(c_all), ((0, MOD_ROWS - N_DEV), (0, 0)))

    mod_part = _matmul("ada_mod", c_act, wgt["w_ada"])
    g2 = _gather_devices("gather_mod", mod_part).reshape(N_DEV, MOD_ROWS, mod_part.shape[1])
    mod_all = jnp.concatenate([g2[2 * k, :N_DEV] for k in range(N_CHIPS)], axis=1) + wgt["b_ada"][None]
    mod = lax.dynamic_slice_in_dim(mod_all, dev, 1, axis=0)

    shards = [wgt["w_in"].T.astype(BF16)] + [wgt[n].astype(BF16) for n in MATRICES[1:]]
    wts = {"w_in_t": _run_plan("gather_w_in", _gather_plan(shards[:1]))[0].reshape(-1, d)}
    late = dict(zip(MATRICES[1:], shards[1:]))
    row = lambda v: v.reshape(1, -1)
    sm = {"g_pre_mix": row(wgt["g_pre_mix"]), "g_post_mix": row(wgt["g_post_mix"]), "w_conv_ssm": w_conv_full,
          "b_conv_ssm": row(wgt["b_conv_ssm"]),
          "dt_bias": row(jnp.concatenate([wgt["dt_bias_fwd"], wgt["dt_bias_bwd"]])),
          "a_neg": row(-jnp.exp(jnp.concatenate([wgt["a_log_fwd"], wgt["a_log_bwd"]]))),
          "d_skip_x": row(jnp.repeat(wgt["d_skip"], HEAD_DIM)), "g_ssm_norm": row(wgt["g_ssm_norm"]),
          "b_glu": row(wgt["b_glu"]), "w_dw": w_dw_full, "b_dw": row(wgt["b_dw"]), "ln_g": row(wgt["ln_g"]),
          "ln_b": row(wgt["ln_b"]), "b_conv_out": row(wgt["b_conv_out"]), "b_gate": row(wgt["b_gate"]),
          "g_pre_ffn": row(wgt["g_pre_ffn"]), "g_post_ffn": row(wgt["g_post_ffn"])}

    sq, grad_x, big, small, dmod = _local_step(xs, tgt, mod, wts, sm, late=late)
    loss = lax.psum((0.5 / d) * sq[0, 0], ("x", "y", "c"))

    local_small = {"b_ada": dmod, "g_pre_mix": small["g_pre_mix"], "g_post_mix": small["g_post_mix"],
                   "b_conv_ssm": small["b_conv_ssm"], "dt_bias_fwd": small["dt_bias"][:, :n_heads],
                   "dt_bias_bwd": small["dt_bias"][:, n_heads:], "a_log_fwd": small["a_log"][:, :n_heads],
                   "a_log_bwd": small["a_log"][:, n_heads:],
                   "d_skip": jnp.sum(small["d_skip_x"].reshape(n_heads, HEAD_DIM), axis=1),
                   "g_ssm_norm": small["g_ssm_norm"], "b_glu": small["b_glu"], "b_dw": small["b_dw"],
                   "ln_g": small["ln_g"], "ln_b": small["ln_b"], "b_conv_out": small["b_conv_out"],
                   "b_gate": small["b_gate"], "g_pre_ffn": small["g_pre_ffn"], "g_post_ffn": small["g_post_ffn"],
                   "w_conv_ssm": small["w_conv_ssm"], "w_dw": small["w_dw"]}
    names = SMALL_PARAMS + SHARDED_SMALL
    pack = _pack([local_small[n] for n in names], d)
    rows_p = pack.shape[0]
    g3 = _gather_devices("gather_small_grads", pack).reshape(N_DEV, rows_p, d)
    total = _rowwise("sum_small_grads", lambda *blocks: (functools.reduce(lambda a, b: a + b, blocks),),
                     [g3[i] for i in range(N_DEV)], [], [(d, F32)])[0]
    full_shapes = [wgt[n].shape for n in SMALL_PARAMS] + [(k_conv, d_xbc), (k_dw, d)]
    summed = dict(zip(names, _unpack(total, full_shapes, d)))
    grads = {n: summed[n] for n in SMALL_PARAMS}
    grads["w_conv_ssm"] = lax.dynamic_slice_in_dim(summed["w_conv_ssm"], chip * xbc_shard, xbc_shard, axis=1)
    grads["w_dw"] = lax.dynamic_slice_in_dim(summed["w_dw"], chip * dw_shard, dw_shard, axis=1)

    dmod_all = g3[:, :N_MOD, :].reshape(N_DEV, N_MOD * d)
    ada_cols = wgt["w_ada"].shape[1]
    dmod_cols = jnp.pad(lax.dynamic_slice_in_dim(dmod_all, chip * ada_cols, ada_cols, axis=1),
                        ((0, MOD_ROWS - N_DEV), (0, 0)))
    grads["w_ada"] = _matmul("g_w_ada", c_act, dmod_cols, ta=True)

    reduced = _shard_sums(*big["pending"])
    grads["w_in"] = reduced[0].T
    for n, g in zip(MATRICES[1:], reduced[1:]):
        grads[n] = g

    delta, new_m, new_v = {}, {}, {}
    for n in ("w_ada",) + MATRICES:
        delta[n], new_m[n], new_v[n] = _adamw("adamw_" + n, wgt[n], grads[n], mom[n], var[n])
    for group, width, tag in ((SMALL_PARAMS, d, "small"), (SHARDED_SMALL, LANES, "conv")):
        shapes = [wgt[n].shape for n in group]
        packs = [_pack([src[n] for n in group], width) for src in (wgt, grads, mom, var)]
        outs = _adamw("adamw_" + tag, *packs)
        for res, o in zip((delta, new_m, new_v), outs):
            res.update(zip(group, _unpack(o, shapes, width)))

    lead = lambda a: a[None]
    return (loss, grad_x[None], *[lead(grads[n]) for n in ALL_PARAMS], *[lead(delta[n]) for n in ALL_PARAMS],
            *[lead(new_m[n]) for n in ALL_PARAMS], *[lead(new_v[n]) for n in ALL_PARAMS])
```

```python
import functools

import jax
import jax.numpy as jnp
from jax import lax
from jax.experimental import pallas as pl
from jax.experimental.pallas import tpu as pltpu

F32 = jnp.float32
BF16 = jnp.bfloat16

N_GROUPS = 8
HEAD_DIM = 64
D_STATE = 128
CHUNK = 128
EPS = 1e-6
N_MOD = 6
ADAM_LR = 0.001
ADAM_B1 = 0.9
ADAM_B2 = 0.999
ADAM_EPS = 1e-08
ADAM_WD = 0.01
ADAM_STEP = 10

V7X_VMEM_BYTES = 64 * 1024 * 1024
VMEM_LIMIT = V7X_VMEM_BYTES - 8 * 1024 * 1024
ROW_TILE_BUDGET = 20 * 1024 * 1024
LANES = 128
NEG = -1e30
MESH = pl.DeviceIdType.MESH
N_CHIPS = 4
N_DEV = 8


def _params(sem):
    return pltpu.CompilerParams(dimension_semantics=sem, vmem_limit_bytes=VMEM_LIMIT)


def _sigmoid(x):
    return 1.0 / (1.0 + jnp.exp(-x))


def _silu(x):
    return x * _sigmoid(x)


def _dsilu(x):
    s = _sigmoid(x)
    return s * (1.0 + x * (1.0 - s))


def _softplus(x):
    return jnp.maximum(x, 0.0) + jnp.log(1.0 + jnp.exp(-jnp.abs(x)))


def _sum0(a):
    return jnp.sum(a, axis=0, keepdims=True)


def _mean1(a):
    return jnp.mean(a, axis=1, keepdims=True)


def _rowwise(name, body, rows, bcasts, out_rows, out_accs=(), tile=None, plan=None):
    rows = [r if isinstance(r, tuple) else (r, r.shape[1], 0) for r in rows]
    s = rows[0][0].shape[0]
    if tile is None:
        per_row = sum(w * a.dtype.itemsize for a, w, _ in rows) + sum(w * jnp.dtype(dt).itemsize for w, dt in out_rows)
        tile = 1024
        while tile > 16 and (tile * per_row * 2 > ROW_TILE_BUDGET or s % tile):
            tile //= 2
        if s * per_row * 2 <= ROW_TILE_BUDGET:
            tile = s
    assert s % tile == 0
    n_in = len(rows) + len(bcasts)
    n_o = len(out_rows)
    n_out = n_o + len(out_accs)
    p_ops, p_in_specs, p_out_shapes, p_out_specs, p_scratch, _ = _plan_call_parts(plan)
    n_pi, n_po = len(p_ops), len(p_out_shapes)
    n_steps = (s // tile,)

    def kern(*refs):
        plan_ins = refs[n_in:n_in + n_pi]
        outs = refs[n_in + n_pi:n_in + n_pi + n_out]
        plan_outs = refs[n_in + n_pi + n_out:n_in + n_pi + n_out + n_po]
        plan_sems = refs[n_in + n_pi + n_out + n_po:]
        _plan_phase(plan, "start", n_steps, plan_ins, plan_outs, plan_sems)
        _plan_phase(plan, "middle", n_steps, plan_ins, plan_outs, plan_sems)
        res = body(*[r[...].astype(F32) for r in refs[:n_in]])
        for o, v in zip(outs[:n_o], res[:n_o]):
            o[...] = v.astype(o.dtype)
        if out_accs:
            @pl.when(pl.program_id(0) == 0)
            def _():
                for o in outs[n_o:]:
                    o[...] = jnp.zeros_like(o)
            for o, v in zip(outs[n_o:], res[n_o:]):
                o[...] += v
        _plan_phase(plan, "end", n_steps, plan_ins, plan_outs, plan_sems)

    in_specs = [pl.BlockSpec((tile, w), functools.partial(lambda i, cb: (i, cb), cb=cb)) for _, w, cb in rows]
    in_specs += [pl.BlockSpec(b.shape, functools.partial(lambda i, nd: (0,) * nd, nd=b.ndim)) for b in bcasts]
    out_shape = [jax.ShapeDtypeStruct((s, w), dt) for w, dt in out_rows]
    out_shape += [jax.ShapeDtypeStruct((1, w), F32) for w in out_accs]
    out_specs = [pl.BlockSpec((tile, w), lambda i: (i, 0)) for w, _ in out_rows]
    out_specs += [pl.BlockSpec((1, w), lambda i: (0, 0)) for w in out_accs]
    return pl.pallas_call(
        kern, name=name, grid=n_steps, in_specs=in_specs + p_in_specs, out_specs=out_specs + p_out_specs,
        out_shape=out_shape + p_out_shapes, scratch_shapes=p_scratch,
        compiler_params=_params(("arbitrary",) if out_accs or plan is not None else ("parallel",)),
    )(*[a for a, _, _ in rows], *bcasts, *p_ops)


def _tile(n, pref):
    if n <= pref:
        return n
    t = (pref // LANES) * LANES
    while t >= LANES:
        if n % t == 0:
            return t
        t -= LANES
    return n


def _matmul(name, a, b, *, ta=False, tb=False, out_dtype=F32, bias=None, add=None, b_blocks=1, out_blocks=1,
            b_rows=None, out_rows=None, plan=None, tm=1024, tn=1408, tk=2816):
    m, k = (a.shape[1], a.shape[0]) if ta else a.shape
    if b_blocks > 1:
        rows_b, cols_b = b.shape[1], b.shape[2] * b_blocks
    else:
        rows_b, cols_b = b.shape
    if b_rows is not None:
        rows_b = b_rows[1]
    n, kb = (rows_b, cols_b) if tb else (cols_b, rows_b)
    assert k == kb, (name, a.shape, b.shape)
    tm, tn, tk = _tile(m, tm), _tile(n, tn), _tile(k, tk)
    if b_blocks > 1:
        per = cols_b // b_blocks
        if tb:
            tk = _tile(per, tk)
        else:
            tn = _tile(per, tn)
    if out_blocks > 1:
        tn = _tile(n // out_blocks, tn)
    nk = k // tk
    grid = (m // tm, n // tn, nk)

    a_spec = pl.BlockSpec((tk, tm), lambda i, j, kk: (kk, i)) if ta else pl.BlockSpec((tm, tk), lambda i, j, kk: (i, kk))
    if b_blocks > 1:
        if tb:
            nb = per // tk
            b_spec = pl.BlockSpec((1, tn, tk), lambda i, j, kk: (kk // nb, j, kk % nb))
        else:
            nb = per // tn
            b_spec = pl.BlockSpec((1, tk, tn), lambda i, j, kk: (j // nb, kk, j % nb))
    elif b_rows is not None:
        first = b_rows[0]
        if tb:
            b_spec = pl.BlockSpec((pl.Element(tn), pl.Element(tk)),
                                  lambda i, j, kk: (pl.multiple_of(first + j * tn, LANES), kk * tk))
        else:
            b_spec = pl.BlockSpec((pl.Element(tk), pl.Element(tn)),
                                  lambda i, j, kk: (pl.multiple_of(first + kk * tk, LANES), j * tn))
    else:
        b_spec = pl.BlockSpec((tn, tk), lambda i, j, kk: (j, kk)) if tb else pl.BlockSpec((tk, tn), lambda i, j, kk: (kk, j))
    in_specs = [a_spec, b_spec]
    operands = [a, b]
    if bias is not None:
        in_specs.append(pl.BlockSpec((1, tn), lambda i, j, kk: (0, j)))
        operands.append(bias)
    if add is not None:
        in_specs.append(pl.BlockSpec((tm, tn), lambda i, j, kk: (i, j)))
        operands.append(add)
    aliases = {}
    if out_blocks > 1:
        nbo = (n // out_blocks) // tn
        out_spec = pl.BlockSpec((1, tm, tn), lambda i, j, kk: (j // nbo, i, j % nbo))
        out_shape = jax.ShapeDtypeStruct((out_blocks, m, n // out_blocks), out_dtype)
    elif out_rows is not None:
        first_out, total, previous = out_rows
        out_spec = pl.BlockSpec((pl.Element(tm), pl.Element(tn)),
                                lambda i, j, kk: (pl.multiple_of(first_out + i * tm, LANES), j * tn))
        out_shape = jax.ShapeDtypeStruct((total, n), out_dtype)
        if previous is not None:
            aliases = {len(operands): 0}
            in_specs.append(pl.BlockSpec(memory_space=pl.ANY))
            operands.append(previous)
    else:
        out_spec = pl.BlockSpec((tm, tn), lambda i, j, kk: (i, j))
        out_shape = jax.ShapeDtypeStruct((m, n), out_dtype)
    dims = (((0 if ta else 1,), (1 if tb else 0,)), ((), ()))
    has_bias, has_add, has_previous = bias is not None, add is not None, bool(aliases)
    p_ops, p_in_specs, p_out_shapes, p_out_specs, p_scratch, _ = _plan_call_parts(plan)
    n_pi, n_po = len(p_ops), len(p_out_shapes)

    def kern(*refs):
        a_ref, b_ref = refs[0], refs[1]
        pos = 2
        bias_ref = add_ref = None
        if has_bias:
            bias_ref = refs[pos]
            pos += 1
        if has_add:
            add_ref = refs[pos]
            pos += 1
        if has_previous:
            pos += 1
        plan_ins = refs[pos:pos + n_pi]
        o_ref = refs[pos + n_pi]
        plan_outs = refs[pos + n_pi + 1:pos + n_pi + 1 + n_po]
        pos += n_pi + 1 + n_po
        acc_ref = refs[pos] if nk > 1 else None
        plan_sems = refs[pos + (1 if nk > 1 else 0):]
        _plan_phase(plan, "start", grid, plan_ins, plan_outs, plan_sems)
        av = a_ref[...].astype(BF16)
        bv = (b_ref[0] if b_blocks > 1 else b_ref[...]).astype(BF16)
        p = lax.dot_general(av, bv, dims, preferred_element_type=F32)

        def finish(acc):
            if has_bias:
                acc = acc + bias_ref[...]
            if has_add:
                acc = acc + add_ref[...]
            if out_blocks > 1:
                o_ref[0] = acc.astype(o_ref.dtype)
            else:
                o_ref[...] = acc.astype(o_ref.dtype)

        if nk == 1:
            finish(p)
        else:
            kk = pl.program_id(2)

            @pl.when(kk == 0)
            def _():
                acc_ref[...] = p

            @pl.when(kk > 0)
            def _():
                acc_ref[...] += p

            @pl.when(kk == nk - 1)
            def _():
                finish(acc_ref[...])
        _plan_phase(plan, "end", grid, plan_ins, plan_outs, plan_sems)

    res = pl.pallas_call(
        kern, name=name, grid=grid, in_specs=in_specs + p_in_specs, out_specs=[out_spec] + p_out_specs,
        out_shape=[out_shape] + p_out_shapes, input_output_aliases=aliases,
        scratch_shapes=([pltpu.VMEM((tm, tn), F32)] if nk > 1 else []) + p_scratch,
        compiler_params=_params(("parallel", "parallel", "arbitrary") if plan is None else ("arbitrary",) * 3),
    )(*operands, *p_ops)
    return res[0] if plan is None else (res[0], list(res[1:]))


CONV_HALO = 16
CONV_ROWS = 256


def _taps(win, shifts, rows):
    n = win.shape[0]
    for j, s in enumerate(shifts):
        yield j, (pltpu.roll(win, (n - s) % n, axis=0) if s % n else win)[:rows]


def _dwconv_fwd(name, x, w, b, *, silu, out_dtype=F32, plan=None):
    s, c = x.shape
    k = w.shape[0]
    pad = (k - 1) // 2
    assert pad <= CONV_HALO and c % LANES == 0
    t = min(CONV_ROWS, s)
    n_chunks = s // t
    fwd_shifts = [CONV_HALO - pad + j for j in range(k)]
    p_ops, p_in_specs, p_out_shapes, p_out_specs, p_scratch, _ = _plan_call_parts(plan)
    n_pi, n_po = len(p_ops), len(p_out_shapes)
    n_steps = (c // LANES,)

    def kern(*refs):
        x_ref, w_ref, b_ref = refs[:3]
        plan_ins = refs[3:3 + n_pi]
        o_ref = refs[3 + n_pi]
        plan_outs = refs[4 + n_pi:4 + n_pi + n_po]
        xp_ref = refs[4 + n_pi + n_po]
        plan_sems = refs[5 + n_pi + n_po:]
        _plan_phase(plan, "start", n_steps, plan_ins, plan_outs, plan_sems)
        _plan_phase(plan, "middle", n_steps, plan_ins, plan_outs, plan_sems)
        zeros = jnp.zeros((CONV_HALO, LANES), F32)
        xp_ref[0:CONV_HALO, :] = zeros
        xp_ref[CONV_HALO + s:CONV_HALO + s + CONV_HALO, :] = zeros
        xp_ref[CONV_HALO:CONV_HALO + s, :] = x_ref[...].astype(F32)
        bv = b_ref[...]

        def chunk(i, carry):
            base = pl.multiple_of(i * t, 16)
            win = xp_ref[pl.ds(base, t + 2 * CONV_HALO), :]
            acc = jnp.zeros((t, LANES), F32)
            for j, xs in _taps(win, fwd_shifts, t):
                acc = acc + xs * w_ref[pl.ds(j, 1), :]
            acc = acc + bv
            o_ref[pl.ds(base, t), :] = (_silu(acc) if silu else acc).astype(o_ref.dtype)
            return carry

        lax.fori_loop(0, n_chunks, chunk, 0)
        _plan_phase(plan, "end", n_steps, plan_ins, plan_outs, plan_sems)

    res = pl.pallas_call(
        kern, name=name, grid=n_steps,
        in_specs=[pl.BlockSpec((s, LANES), lambda i: (0, i)), pl.BlockSpec((k, LANES), lambda i: (0, i)),
                  pl.BlockSpec((1, LANES), lambda i: (0, i))] + p_in_specs,
        out_specs=[pl.BlockSpec((s, LANES), lambda i: (0, i))] + p_out_specs,
        out_shape=[jax.ShapeDtypeStruct((s, c), out_dtype)] + p_out_shapes,
        scratch_shapes=[pltpu.VMEM((s + 2 * CONV_HALO, LANES), F32)] + p_scratch,
        compiler_params=_params(("parallel",) if plan is None else ("arbitrary",)),
    )(x, w, b, *p_ops)
    return res[0] if plan is None else (res[0], list(res[1:]))


def _dwconv_bwd(name, x, w, b, dout, *, silu, dx_dtype=F32):
    s, c = x.shape
    k = w.shape[0]
    pad = (k - 1) // 2
    t = min(CONV_ROWS, s)
    n_chunks = s // t
    fwd_shifts = [CONV_HALO - pad + j for j in range(k)]
    bwd_shifts = [CONV_HALO + pad - j for j in range(k)]

    def kern(x_ref, w_ref, b_ref, do_ref, dx_ref, dw_ref, db_ref, xp_ref, dp_ref):
        zeros = jnp.zeros((CONV_HALO, LANES), F32)
        for ref in (xp_ref, dp_ref):
            ref[0:CONV_HALO, :] = zeros
            ref[CONV_HALO + s:CONV_HALO + s + CONV_HALO, :] = zeros
        xp_ref[CONV_HALO:CONV_HALO + s, :] = x_ref[...].astype(F32)
        bv = b_ref[...]
        dw_ref[...] = jnp.zeros_like(dw_ref)

        def pre_chunk(i, dbias):
            base = pl.multiple_of(i * t, 16)
            win = xp_ref[pl.ds(base, t + 2 * CONV_HALO), :]
            dpre = do_ref[pl.ds(base, t), :].astype(F32)
            if silu:
                acc = jnp.zeros((t, LANES), F32)
                for j, xs in _taps(win, fwd_shifts, t):
                    acc = acc + xs * w_ref[pl.ds(j, 1), :]
                dpre = dpre * _dsilu(acc + bv)
            dp_ref[pl.ds(base + CONV_HALO, t), :] = dpre
            for j, xs in _taps(win, fwd_shifts, t):
                dw_ref[pl.ds(j, 1), :] += _sum0(dpre * xs)
            return dbias + _sum0(dpre)

        db_ref[...] = lax.fori_loop(0, n_chunks, pre_chunk, jnp.zeros((1, LANES), F32))

        def dx_chunk(i, carry):
            base = pl.multiple_of(i * t, 16)
            win = dp_ref[pl.ds(base, t + 2 * CONV_HALO), :]
            acc = jnp.zeros((t, LANES), F32)
            for j, dps in _taps(win, bwd_shifts, t):
                acc = acc + dps * w_ref[pl.ds(j, 1), :]
            dx_ref[pl.ds(base, t), :] = acc.astype(dx_ref.dtype)
            return carry

        lax.fori_loop(0, n_chunks, dx_chunk, 0)

    col = lambda rows: pl.BlockSpec((rows, LANES), lambda i: (0, i))
    return pl.pallas_call(
        kern, name=name, grid=(c // LANES,),
        in_specs=[col(s), col(k), col(1), col(s)],
        out_specs=[col(s), col(k), col(1)],
        out_shape=[jax.ShapeDtypeStruct((s, c), dx_dtype), jax.ShapeDtypeStruct((k, c), F32),
                   jax.ShapeDtypeStruct((1, c), F32)],
        scratch_shapes=[pltpu.VMEM((s + 2 * CONV_HALO, LANES), F32), pltpu.VMEM((s + 2 * CONV_HALO, LANES), F32)],
        compiler_params=_params(("parallel",)),
    )(x, w, b, dout)


_NT =(((1,), (1,)), ((), ()))
_TN = (((0,), (0,)), ((), ()))


def _dot(a, b, dims=None):
    if dims is None:
        return jnp.dot(a, b, preferred_element_type=F32)
    return lax.dot_general(a, b, dims, preferred_element_type=F32)


HEAD_QUANTITIES = 4
GROUPS_PER_STEP = 2


def _scan_tables(n_heads):
    j_heads = n_heads // N_GROUPS
    used = 3 * HEAD_QUANTITIES * j_heads
    assert used <= LANES and j_heads % 2 == 0 and N_GROUPS % GROUPS_PER_STEP == 0
    gw = j_heads * HEAD_DIM
    r = jnp.arange(LANES)[:, None]

    def expand(quantity, width):
        head_of_lane = jnp.arange(j_heads * width)[None] // width
        return ((r // (3 * j_heads) == quantity) & (r % j_heads == head_of_lane) & (r < used)).astype(BF16)

    sel_cols = (jnp.arange(gw)[:, None] // HEAD_DIM == jnp.arange(LANES)[None]).astype(BF16)
    h2 = 2 * n_heads
    rows = jnp.arange(3 * HEAD_QUANTITIES * h2)
    head = rows % n_heads
    col = ((rows % h2) // n_heads * N_GROUPS + head // j_heads) * LANES + (rows // h2) * j_heads + head % j_heads
    route = (col[:, None] == jnp.arange(2 * N_GROUPS * LANES)[None]).astype(BF16)
    return {"ex_dt": expand(0, HEAD_DIM), "ex_gam": expand(1, CHUNK), "ex_din": expand(2, HEAD_DIM),
            "ex_dst": expand(3, HEAD_DIM), "sel_cols": sel_cols, "route": route}


def _chunk_decay(etot, n_heads, direction):
    j_heads = n_heads // N_GROUPS
    ed = etot[:, :, direction * n_heads:(direction + 1) * n_heads]
    per_group = jnp.pad(ed.reshape(ed.shape[0], 8, N_GROUPS, j_heads), ((0, 0), (0, 0), (0, 0), (0, LANES - j_heads)))
    return jnp.repeat(ed, HEAD_DIM, axis=2), per_group.reshape(ed.shape[0], 8, N_GROUPS * LANES)


def _scan_specs(reverse_order, direction, nc, j_heads, d_ssm):
    gps = GROUPS_PER_STEP
    gw = j_heads * HEAD_DIM
    b_off = d_ssm // (gps * D_STATE)
    c_off = b_off + N_GROUPS // gps
    d_off = direction * (N_GROUPS // gps)
    zz = (lambda z: nc - 1 - z) if reverse_order else (lambda z: z)
    const = lambda shape: pl.BlockSpec(shape, lambda g, z: (0,) * len(shape))
    return {
        "xs": pl.BlockSpec((CHUNK, gps * gw), lambda g, z: (zz(z), g)),
        "b": pl.BlockSpec((CHUNK, gps * D_STATE), lambda g, z: (zz(z), b_off + g)),
        "c": pl.BlockSpec((CHUNK, gps * D_STATE), lambda g, z: (zz(z), c_off + g)),
        "q": pl.BlockSpec((CHUNK, gps * LANES), lambda g, z: (zz(z), d_off + g)),
        "gam_t": pl.BlockSpec((gps * j_heads, CHUNK), lambda g, z: (d_off + g, zz(z))),
        "etot_x": pl.BlockSpec((1, 8, gps * gw), lambda g, z: (zz(z), 0, g)),
        "etot_g": pl.BlockSpec((1, 8, gps * LANES), lambda g, z: (zz(z), 0, g)),
        "state": pl.BlockSpec((gps, 1, D_STATE, gw), lambda g, z: (g, zz(z), 0, 0)),
        "grp": pl.BlockSpec((CHUNK, gps * D_STATE), lambda g, z: (zz(z), g)),
        "ex": const((LANES, gw)), "ex_gam": const((LANES, j_heads * CHUNK)), "sel": const((gw, LANES)),
    }


def _scan_masks(reverse):
    li = lax.broadcasted_iota(jnp.int32, (CHUNK, CHUNK), 0)
    si = lax.broadcasted_iota(jnp.int32, (CHUNK, CHUNK), 1)
    mask = (li <= si) if reverse else (li >= si)
    mask_t = (si <= li) if reverse else (si >= li)
    return li, si, mask, mask_t, si < HEAD_DIM


def _plan_phase(plan, phase, n_steps, ins, outs, sems):
    if plan is None or phase not in plan.phases:
        return
    ids = [pl.program_id(i) for i in range(len(n_steps))]
    first = {"start": 0, "middle": n_steps[0] // 2, "end": n_steps[0] - 1}[phase]
    when = ids[0] == first
    for i, n in zip(ids[1:], n_steps[1:]):
        when = when & (i == (n - 1 if phase == "end" else 0))

    @pl.when(when)
    def _():
        plan.phases[phase](ins, outs, *sems)


def _plan_call_parts(plan):
    if plan is None:
        return [], [], [], [], [], ("parallel", "arbitrary")
    n_in, n_out = len(plan.operands), len(plan.out_shapes)
    return plan.operands, [ANY] * n_in, plan.out_shapes, [ANY] * n_out, plan.sems(), ("arbitrary", "arbitrary")


def _scan_fwd(name, xbc_c, q_all, gam_t, etot_x, tb, *, direction, d_ssm, plan=None):
    s = xbc_c.shape[0]
    nc = s // CHUNK
    j_heads = tb["ex_dt"].shape[1] // HEAD_DIM
    gw = j_heads * HEAD_DIM
    gps = GROUPS_PER_STEP
    reverse = direction == 1
    sp = _scan_specs(reverse, direction, nc, j_heads, d_ssm)
    p_ops, p_in_specs, p_out_shapes, p_out_specs, p_scratch, semantics = _plan_call_parts(plan)
    n_pi, n_po = len(p_ops), len(p_out_shapes)
    n_steps = (N_GROUPS // gps, nc)

    def kern(*refs):
        xs_ref, b_ref, c_ref, q_ref, gamt_ref, etx_ref, exdt_ref, exgam_ref, exdin_ref, exdst_ref = refs[:10]
        plan_ins = refs[10:10 + n_pi]
        y_ref, hs_ref = refs[10 + n_pi:12 + n_pi]
        plan_outs = refs[12 + n_pi:12 + n_pi + n_po]
        h_ref = refs[12 + n_pi + n_po]
        plan_sems = refs[13 + n_pi + n_po:]
        _plan_phase(plan, "start", n_steps, plan_ins, plan_outs, plan_sems)
        _plan_phase(plan, "middle", n_steps, plan_ins, plan_outs, plan_sems)

        @pl.when(pl.program_id(1) == 0)
        def _():
            h_ref[...] = jnp.zeros_like(h_ref)

        _, _, mask, _, lo = _scan_masks(reverse)
        for gi in range(gps):
            bb = b_ref[:, gi * D_STATE:(gi + 1) * D_STATE].astype(BF16)
            cb = c_ref[:, gi * D_STATE:(gi + 1) * D_STATE].astype(BF16)
            cbt = _dot(cb, bb, _NT)
            q = q_ref[:, gi * LANES:(gi + 1) * LANES]
            dtx, dinx, dstx = _dot(q, exdt_ref[...]), _dot(q, exdin_ref[...]), _dot(q, exdst_ref[...])
            gcol = _dot(q, exgam_ref[...])
            xdt = xs_ref[:, gi * gw:(gi + 1) * gw].astype(F32) * dtx
            ht = h_ref[gi]
            y_off = _dot(cb, ht.astype(BF16)) * dinx
            hs_ref[gi, 0] = ht
            for p in range(j_heads // 2):
                lanes = slice(p * CHUNK, (p + 1) * CHUNK)
                x2 = xdt[:, lanes]
                acc = y_off[:, lanes]
                for idx, j in enumerate((2 * p, 2 * p + 1)):
                    g_row = gamt_ref[pl.ds(gi * j_heads + j, 1), :]
                    decay = jnp.exp(jnp.where(mask, gcol[:, j * CHUNK:(j + 1) * CHUNK] - g_row, NEG))
                    x_head = jnp.where(lo if idx == 0 else jnp.logical_not(lo), x2, 0.0).astype(BF16)
                    acc = acc + _dot((cbt * decay).astype(BF16), x_head)
                y_ref[:, gi * gw + p * CHUNK:gi * gw + (p + 1) * CHUNK] = acc.astype(y_ref.dtype)
            h_ref[gi] = ht * etx_ref[0, 0:1, gi * gw:(gi + 1) * gw] + _dot(bb, (xdt * dstx).astype(BF16), _TN)
        _plan_phase(plan, "end", n_steps, plan_ins, plan_outs, plan_sems)

    res = pl.pallas_call(
        kern, name=name, grid=n_steps,
        in_specs=[sp["xs"], sp["b"], sp["c"], sp["q"], sp["gam_t"], sp["etot_x"], sp["ex"], sp["ex_gam"], sp["ex"],
                  sp["ex"]] + p_in_specs,
        out_specs=[sp["xs"], sp["state"]] + p_out_specs,
        out_shape=[jax.ShapeDtypeStruct((s, d_ssm), BF16),
                   jax.ShapeDtypeStruct((N_GROUPS, nc, D_STATE, gw), F32)] + p_out_shapes,
        scratch_shapes=[pltpu.VMEM((gps, D_STATE, gw), F32)] + p_scratch,
        compiler_params=_params(semantics),
    )(xbc_c, xbc_c, xbc_c, q_all, gam_t, etot_x, tb["ex_dt"], tb["ex_gam"], tb["ex_din"], tb["ex_dst"], *p_ops)
    return res[0], res[1], list(res[2:])


def _scan_bwd(name, xbc_c, dy, hs, q_all, gam_t, etot_x, etot_g, tb, *, direction, d_ssm, plan=None):
    s = xbc_c.shape[0]
    nc = s // CHUNK
    j_heads = tb["ex_dt"].shape[1] // HEAD_DIM
    gw = j_heads * HEAD_DIM
    gps = GROUPS_PER_STEP
    reverse = direction == 1
    sp = _scan_specs(not reverse, direction, nc, j_heads, d_ssm)
    hp = lax.Precision.HIGHEST
    p_ops, p_in_specs, p_out_shapes, p_out_specs, p_scratch, semantics = _plan_call_parts(plan)
    n_pi, n_po = len(p_ops), len(p_out_shapes)
    n_steps = (N_GROUPS // gps, nc)

    def kern(*refs):
        (xs_ref, b_ref, c_ref, dy_ref, hs_ref, q_ref, gamt_ref, etg_ref, etx_ref, exdt_ref, exgam_ref, exdin_ref,
         exdst_ref, sel_ref) = refs[:14]
        plan_ins = refs[14:14 + n_pi]
        dxs_ref, db_ref, dc_ref, ddt_ref, da_ref = refs[14 + n_pi:19 + n_pi]
        plan_outs = refs[19 + n_pi:19 + n_pi + n_po]
        dh_ref, tmp_ref = refs[19 + n_pi + n_po:21 + n_pi + n_po]
        plan_sems = refs[21 + n_pi + n_po:]
        _plan_phase(plan, "start", n_steps, plan_ins, plan_outs, plan_sems)
        _plan_phase(plan, "middle", n_steps, plan_ins, plan_outs, plan_sems)

        @pl.when(pl.program_id(1) == 0)
        def _():
            dh_ref[...] = jnp.zeros_like(dh_ref)

        li, si, mask, mask_t, lo = _scan_masks(reverse)
        sel = sel_ref[...]
        incl = ((si <= li) if reverse else (si >= li)).astype(F32)
        excl = ((si > li) if reverse else (si < li)).astype(F32)
        for gi in range(gps):
            grp_lanes = slice(gi * D_STATE, (gi + 1) * D_STATE)
            bb = b_ref[:, grp_lanes].astype(BF16)
            cb = c_ref[:, grp_lanes].astype(BF16)
            cbt = _dot(cb, bb, _NT)
            cbt_t = _dot(bb, cb, _NT)
            q = q_ref[:, gi * LANES:(gi + 1) * LANES]
            dtx, dinx, dstx = _dot(q, exdt_ref[...]), _dot(q, exdin_ref[...]), _dot(q, exdst_ref[...])
            gcol = _dot(q, exgam_ref[...])
            x_all = xs_ref[:, gi * gw:(gi + 1) * gw].astype(F32)
            dy_all = dy_ref[:, gi * gw:(gi + 1) * gw].astype(F32)
            xdt = x_all * dtx
            xb = xdt.astype(BF16)
            ht = hs_ref[gi, 0]
            hb = ht.astype(BF16)
            dht = dh_ref[gi]
            dhb = dht.astype(BF16)
            y_off = _dot(cb, hb) * dinx
            dx_off = _dot(bb, dhb) * dstx
            dyd = (dy_all * dinx).astype(BF16)
            xd = (xdt * dstx).astype(BF16)
            dc_acc = _dot(dyd, hb, _NT)
            db_acc = _dot(xd, dhb, _NT)
            dh_ref[gi] = dht * etx_ref[0, 0:1, gi * gw:(gi + 1) * gw] + _dot(cb, dyd, _TN)
            q_cols = _dot((dy_all * y_off).astype(BF16), sel)
            c_cols = _dot((xdt * dx_off).astype(BF16), sel)
            through = _sum0(_dot((dht * ht).astype(BF16), sel))
            dcbt = jnp.zeros((CHUNK, CHUNK), F32)
            for p in range(j_heads // 2):
                lanes = slice(p * CHUNK, (p + 1) * CHUNK)
                out_lanes = slice(gi * gw + p * CHUNK, gi * gw + (p + 1) * CHUNK)
                x2b = xb[:, lanes]
                dy2 = dy_all[:, lanes]
                acc = dx_off[:, lanes]
                for idx, j in enumerate((2 * p, 2 * p + 1)):
                    gc = gcol[:, j * CHUNK:(j + 1) * CHUNK]
                    gr = gamt_ref[pl.ds(gi * j_heads + j, 1), :]
                    decay = jnp.exp(jnp.where(mask, gc - gr, NEG))
                    decay_t = jnp.exp(jnp.where(mask_t, gr - gc, NEG))
                    dy_head = jnp.where(lo if idx == 0 else jnp.logical_not(lo), dy2, 0.0).astype(BF16)
                    acc = acc + _dot((cbt_t * decay_t).astype(BF16), dy_head)
                    dm = decay * _dot(dy_head, x2b, _NT)
                    dcbt = dcbt + dm
                    e = (cbt * dm).astype(BF16)
                    in_lane_j = si == j
                    q_cols = (q_cols + jnp.where(in_lane_j, jnp.sum(e.astype(F32), axis=1, keepdims=True), 0.0)
                              - _dot(e, jnp.where(in_lane_j, 1.0, 0.0).astype(BF16), _TN))
                dxs_ref[:, out_lanes] = (acc * dtx[:, lanes]).astype(dxs_ref.dtype)
                tmp_ref[:, lanes] = acc * x_all[:, lanes]
            dcb = dcbt.astype(BF16)
            dc_ref[:, grp_lanes] = (dc_acc + _dot(dcb, bb)).astype(dc_ref.dtype)
            db_ref[:, grp_lanes] = (db_acc + _dot(dcb, cb, _TN)).astype(db_ref.dtype)
            ddt_ref[:, gi * LANES:(gi + 1) * LANES] = _dot(tmp_ref[...].astype(BF16), sel)
            da_ref[:, gi * LANES:(gi + 1) * LANES] = (
                jnp.dot(incl, q_cols, preferred_element_type=F32, precision=hp)
                + jnp.dot(excl, c_cols, preferred_element_type=F32, precision=hp)
                + through * etg_ref[0, 0:1, gi * LANES:(gi + 1) * LANES])
        _plan_phase(plan, "end", n_steps, plan_ins, plan_outs, plan_sems)

    gn = N_GROUPS * D_STATE
    res = pl.pallas_call(
        kern, name=name, grid=n_steps,
        in_specs=[sp["xs"], sp["b"], sp["c"], sp["xs"], sp["state"], sp["q"], sp["gam_t"], sp["etot_g"], sp["etot_x"],
                  sp["ex"], sp["ex_gam"], sp["ex"], sp["ex"], sp["sel"]] + p_in_specs,
        out_specs=[sp["xs"], sp["grp"], sp["grp"], sp["grp"], sp["grp"]] + p_out_specs,
        out_shape=[jax.ShapeDtypeStruct((s, d_ssm), BF16), jax.ShapeDtypeStruct((s, gn), BF16),
                   jax.ShapeDtypeStruct((s, gn), BF16), jax.ShapeDtypeStruct((s, gn), F32),
                   jax.ShapeDtypeStruct((s, gn), F32)] + p_out_shapes,
        scratch_shapes=[pltpu.VMEM((gps, D_STATE, gw), F32), pltpu.VMEM((CHUNK, gw), F32)] + p_scratch,
        compiler_params=_params(semantics),
    )(xbc_c, xbc_c, xbc_c, dy, hs, q_all, gam_t, etot_g, etot_x, tb["ex_dt"], tb["ex_gam"], tb["ex_din"], tb["ex_dst"],
      tb["sel_cols"], *p_ops)
    return res[:5], list(res[5:])


def _dt_prepare(dt_raw, dt_bias, a_neg, route):
    s, h2 = dt_raw.shape
    n_heads = h2 // 2
    qw = route.shape[1]

    def kern(raw_ref, bias_ref, a_ref, route_ref, dt_ref, q_ref, gamt_ref, etot_ref):
        dt = _softplus(raw_ref[...] + bias_ref[...])
        a = dt * a_ref[...]
        li = lax.broadcasted_iota(jnp.int32, (CHUNK, CHUNK), 0)
        si = lax.broadcasted_iota(jnp.int32, (CHUNK, CHUNK), 1)
        tri = (li >= si).astype(F32)
        cs = jnp.dot(tri, a, preferred_element_type=F32, precision=lax.Precision.HIGHEST)
        tot = _sum0(a)
        fwd = lax.broadcasted_iota(jnp.int32, (CHUNK, h2), 1) < n_heads
        gam = jnp.where(fwd, cs, a - cs)
        din = jnp.where(fwd, jnp.exp(cs), jnp.exp(tot + gam))
        dst = jnp.where(fwd, jnp.exp(tot - cs), jnp.exp(cs - a))
        pieces = []
        for v in (dt, gam, din, dst):
            hi = v.astype(BF16)
            rest = v - hi.astype(F32)
            mid = rest.astype(BF16)
            pieces += [hi, mid, (rest - mid.astype(F32)).astype(BF16)]
        q_ref[...] = _dot(jnp.concatenate(pieces, axis=1), route_ref[...]).astype(BF16)
        dt_ref[...] = dt
        gamt_ref[...] = gam.T
        etot_ref[...] = jnp.broadcast_to(jnp.exp(tot), (8, h2))

    nc = s // CHUNK
    rows = lambda w: pl.BlockSpec((CHUNK, w), lambda i: (i, 0))
    whole = lambda a: pl.BlockSpec(a.shape, lambda i: (0, 0))
    return pl.pallas_call(
        kern, name="dt_prepare", grid=(nc,),
        in_specs=[rows(h2), whole(dt_bias), whole(a_neg), whole(route)],
        out_specs=[rows(h2), rows(qw), pl.BlockSpec((h2, CHUNK), lambda i: (0, i)), pl.BlockSpec((8, h2), lambda i: (i, 0))],
        out_shape=[jax.ShapeDtypeStruct((s, h2), F32), jax.ShapeDtypeStruct((s, qw), BF16),
                   jax.ShapeDtypeStruct((h2, s), F32), jax.ShapeDtypeStruct((nc * 8, h2), F32)],
        compiler_params=_params(("parallel",)),
    )(dt_raw, dt_bias, a_neg, route)


def _from_group_lanes(arr, j_heads):
    s = arr.shape[0]
    return arr.reshape(s, N_GROUPS, LANES)[:, :, :j_heads].reshape(s, N_GROUPS * j_heads)


def _dt_backward(da, ddt, dt, dt_raw, dt_bias, a_neg):
    h2 = da.shape[1]

    def body(dav, ddtv, dtv, raw, bias, a_head):
        draw = (ddtv + dav * a_head) * _sigmoid(raw + bias)
        return draw, _sum0(draw), _sum0(dav * dtv) * a_head
    return _rowwise("dt_backward", body, [da, ddt, dt, dt_raw], [dt_bias, a_neg], [(h2, BF16)], [h2, h2])


def _rms(x):
    r = lax.rsqrt(_mean1(x * x) + EPS)
    return x * r, r


def _rms_bwd(dy, y, r):
    return r * (dy - y * _mean1(dy * y))


def _norm_mod_fwd(name, x, g, sc, sh):
    def body(xv, gv, scv, shv):
        y, _ = _rms(xv)
        return ((y * gv) * (1.0 + scv) + shv,)
    return _rowwise(name, body, [x], [g, sc, sh], [(x.shape[1], BF16)])[0]


def _norm_mod_bwd(name, x, dh, dpass, g, sc, plan=None):
    d = x.shape[1]

    def body(xv, dhv, dpv, gv, scv):
        y, r = _rms(xv)
        dn = dhv * (1.0 + scv)
        dx = _rms_bwd(dn * gv, y, r) + dpv
        return dx, _sum0(dn * y), _sum0(dhv * (y * gv)), _sum0(dhv)
    return _rowwise(name, body, [x, dh, dpass], [g, sc], [(d, F32)], [d, d, d], plan=plan)


def _gated_residual_fwd(name, x, m, gate, gp):
    def body(xv, mv, gatev, gpv):
        y, _ = _rms(mv)
        return (xv + gatev * (y * gpv),)
    return _rowwise(name, body, [x, m], [gate, gp], [(x.shape[1], F32)])[0]


def _gated_residual_bwd(name, m, dx1, gate, gp):
    d = m.shape[1]

    def body(mv, dv, gatev, gpv):
        y, r = _rms(mv)
        dn = dv * gatev
        return _rms_bwd(dn * gpv, y, r), _sum0(dv * (y * gpv)), _sum0(dn * y)
    return _rowwise(name, body, [m, dx1], [gate, gp], [(d, BF16)], [d, d])


def _final_residual_loss(x1, f, tgt, gate, gp):
    d = x1.shape[1]

    def body(xv, fv, tv, gatev, gpv):
        y, r = _rms(fv)
        n = y * gpv
        err = xv + gatev * n - tv
        dx2 = err * (1.0 / d)
        dn = dx2 * gatev
        sq = jnp.sum(_sum0(err * err), axis=1, keepdims=True)
        return dx2, _rms_bwd(dn * gpv, y, r), jnp.broadcast_to(sq, (1, LANES)), _sum0(dx2 * n), _sum0(dn * y)
    return _rowwise("final_residual_loss", body, [x1, f, tgt], [gate, gp], [(d, F32), (d, BF16)], [LANES, d, d])


def _swiglu_fwd(gu):
    f = gu.shape[1] // 2

    def body(v):
        return (_silu(v[:, :f]) * v[:, f:],)
    return _rowwise("swiglu_fwd", body, [gu], [], [(f, BF16)])[0]


def _swiglu_bwd(gu, dact):
    f = gu.shape[1] // 2

    def body(v, dv):
        gt, up = v[:, :f], v[:, f:]
        return (jnp.concatenate([dv * up * _dsilu(gt), dv * _silu(gt)], axis=1),)
    return _rowwise("swiglu_bwd", body, [gu, dact], [], [(2 * f, BF16)])[0]


def _glu_fwd(glu_in, b_glu):
    c = glu_in.shape[1] // 2

    def body(v, bv):
        t = v + bv
        return (t[:, :c] * _sigmoid(t[:, c:]),)
    return _rowwise("glu_fwd", body, [glu_in], [b_glu], [(c, F32)])[0]


def _glu_bwd(glu_in, b_glu, du0):
    c = glu_in.shape[1] // 2

    def body(v, dv, bv):
        t = v + bv
        a, s = t[:, :c], _sigmoid(t[:, c:])
        dg = jnp.concatenate([dv * s, dv * a * s * (1.0 - s)], axis=1)
        return dg, _sum0(dg)
    return _rowwise("glu_bwd", body, [glu_in, du0], [b_glu], [(2 * c, BF16)], [2 * c])


def _ln_parts(u1):
    xc = u1 - _mean1(u1)
    r = lax.rsqrt(_mean1(xc * xc) + EPS)
    return xc * r, r


def _ln_silu_fwd(u1, ln_g, ln_b):
    def body(v, gv, bv):
        yh, _ = _ln_parts(v)
        return (_silu(yh * gv + bv),)
    return _rowwise("ln_silu_fwd", body, [u1], [ln_g, ln_b], [(u1.shape[1], BF16)])[0]


def _ln_silu_bwd(u1, du, ln_g, ln_b):
    d = u1.shape[1]

    def body(v, dv, gv, bv):
        yh, r = _ln_parts(v)
        dl = dv * _dsilu(yh * gv + bv)
        dyh = dl * gv
        du1 = r * (dyh - _mean1(dyh) - yh * _mean1(dyh * yh))
        return du1, _sum0(dl * yh), _sum0(dl)
    return _rowwise("ln_silu_bwd", body, [u1, du], [ln_g, ln_b], [(d, F32)], [d, d])


def _gate_merge_fwd(y_a, y_b, gl, b_gate):
    d = y_a.shape[1]

    def body(ya, yb, glv, bv):
        s = _sigmoid(glv + bv)
        return (s[:, :d] * ya + s[:, d:] * yb,)
    return _rowwise("gate_merge_fwd", body, [y_a, y_b, gl], [b_gate], [(d, BF16)])[0]


def _gate_merge_bwd(dmixin, y_a, y_b, gl, b_gate):
    d = y_a.shape[1]

    def body(dv, ya, yb, glv, bv):
        s = _sigmoid(glv + bv)
        sa, sb = s[:, :d], s[:, d:]
        dya, dyb = dv * sa, dv * sb
        dgl = jnp.concatenate([dv * ya * sa * (1.0 - sa), dv * yb * sb * (1.0 - sb)], axis=1)
        return dya, dyb, dgl, _sum0(dgl), _sum0(dyb)
    return _rowwise("gate_merge_bwd", body, [dmixin, y_a, y_b, gl], [b_gate],
                    [(d, BF16), (d, BF16), (2 * d, BF16)], [2 * d, d])


def _group_slices(d_ssm):
    gw = d_ssm // N_GROUPS
    return [slice(g * gw, (g + 1) * gw) for g in range(N_GROUPS)]


def _gated_norm_fwd(y_f, y_b, xbc_c, z, d_skip_x, g_ssm):
    d_ssm = y_f.shape[1]

    def body(yf, yb, xs, zv, dsk, gv):
        y = yf + yb + dsk * xs
        v = y * _silu(zv)
        outs = []
        for sl in _group_slices(d_ssm):
            w, _ = _rms(v[:, sl])
            outs.append(w)
        return y, jnp.concatenate(outs, axis=1) * gv
    return _rowwise("gated_norm_fwd", body, [y_f, y_b, (xbc_c, d_ssm, 0), z], [d_skip_x, g_ssm],
                    [(d_ssm, F32), (d_ssm, BF16)])


def _gated_norm_bwd(y, z, dyn, xbc_c, d_skip_x, g_ssm):
    d_ssm = y.shape[1]

    def body(yv, zv, dv, xs, dsk, gv):
        sz = _silu(zv)
        v = yv * sz
        dw = dv * gv
        dvs, ws = [], []
        for sl in _group_slices(d_ssm):
            w, r = _rms(v[:, sl])
            ws.append(w)
            dvs.append(_rms_bwd(dw[:, sl], w, r))
        dvv = jnp.concatenate(dvs, axis=1)
        dy = dvv * sz
        return dy, dvv * yv * _dsilu(zv), _sum0(dv * jnp.concatenate(ws, axis=1)), _sum0(dy * xs)
    return _rowwise("gated_norm_bwd", body, [y, z, dyn, (xbc_c, d_ssm, 0)], [d_skip_x, g_ssm],
                    [(d_ssm, BF16), (d_ssm, BF16)], [d_ssm, d_ssm])


def _ssd_grad_merge(dxs_f, dxs_b, dy, db_f, db_b, dc_f, dc_b, d_skip_x):
    d_ssm = dy.shape[1]
    width = d_ssm + 2 * N_GROUPS * D_STATE

    def body(xf, xb, dv, bf, bb, cf, cbv, dsk):
        return (jnp.concatenate([xf + xb + dsk * dv, bf + bb, cf + cbv], axis=1),)
    return _rowwise("ssd_grad_merge", body, [dxs_f, dxs_b, dy, db_f, db_b, dc_f, dc_b], [d_skip_x], [(width, BF16)])[0]


def _adamw(name, w, g, m, v):
    c = w.shape[1]
    c1 = 1.0 - ADAM_B1 ** ADAM_STEP
    c2 = 1.0 - ADAM_B2 ** ADAM_STEP

    def body(wv, gv, mv, vv):
        mn = ADAM_B1 * mv + (1.0 - ADAM_B1) * gv
        vn = ADAM_B2 * vv + (1.0 - ADAM_B2) * (gv * gv)
        delta = -ADAM_LR * ((mn / c1) / (jnp.sqrt(vn / c2) + ADAM_EPS) + ADAM_WD * wv)
        return delta, mn, vn
    return _rowwise(name, body, [w, g, m, v], [], [(c, F32)] * 3)


def _local_step(x, tgt, mod, wts, sm, late=None):
    s, d = x.shape
    d_ssm = 2 * d
    n_heads = d_ssm // HEAD_DIM
    d_xbc = d_ssm + 2 * N_GROUPS * D_STATE
    sec = [0, d_ssm, d_ssm + d_xbc, d_ssm + d_xbc + 2 * n_heads, d_ssm + d_xbc + 2 * n_heads + 2 * d]
    sec.append(sec[-1] + 2 * d)
    sh1, sc1, g1, sh2, sc2, g2 = [mod[:, i * d:(i + 1) * d] for i in range(N_MOD)]
    win_t = wts["w_in_t"]
    sections = [(nm, (sec[i], sec[i + 1] - sec[i])) for i, nm in enumerate(("z", "xbc", "dt", "glu", "gate"))]

    h1 = _norm_mod_fwd("pre_mix_norm", x, sm["g_pre_mix"], sc1, sh1)
    z, xbc, dt_raw, glu_in, gate_l = [
        _matmul(f"proj_{nm}", h1, win_t, tb=True, b_rows=rows, out_dtype=F32 if nm == "dt" else BF16)
        for nm, rows in sections]
    plan_c = plan_f = plan_r = None
    if late is not None:
        plan_c = _gather_plan([late[n] for n in MATRICES[1:4]])
        plan_f = _gather_plan([late["w_down"]])
        plan_r = _gather_plan([late["w_gate_up"]])
    xbc_c = _dwconv_fwd("ssm_conv_fwd", xbc, sm["w_conv_ssm"], sm["b_conv_ssm"], silu=True, out_dtype=BF16, plan=plan_c)
    if late is not None:
        xbc_c, got_c = xbc_c
    tables = _scan_tables(n_heads)
    dt, q_all, gam_t, etot = _dt_prepare(dt_raw, sm["dt_bias"], sm["a_neg"], tables["route"])
    etot = etot.reshape(s // CHUNK, 8, 2 * n_heads)
    (etx_f, etg_f), (etx_r, etg_r) = [_chunk_decay(etot, n_heads, direction) for direction in (0, 1)]
    y_f, hs_f, got_f = _scan_fwd("ssd_fwd_f", xbc_c, q_all, gam_t, etx_f, tables, direction=0, d_ssm=d_ssm, plan=plan_f)
    y_r, hs_r, got_r = _scan_fwd("ssd_fwd_r", xbc_c, q_all, gam_t, etx_r, tables, direction=1, d_ssm=d_ssm, plan=plan_r)
    if late is not None:
        wts = dict(wts, w_ssm_out=got_c[0].reshape(-1, d), w_conv_out=got_c[1].reshape(-1, d),
                   w_mix_out=got_c[2].reshape(-1, d), w_down=got_f[0].reshape(-1, d), w_gate_up=got_r[0])
    y_ssd, yn = _gated_norm_fwd(y_f, y_r, xbc_c, z, sm["d_skip_x"], sm["g_ssm_norm"])
    y_a = _matmul("ssm_out", yn, wts["w_ssm_out"])
    u0 = _glu_fwd(glu_in, sm["b_glu"])
    u1 = _dwconv_fwd("dw_conv_fwd", u0, sm["w_dw"], sm["b_dw"], silu=False)
    u = _ln_silu_fwd(u1, sm["ln_g"], sm["ln_b"])
    y_b = _matmul("conv_out", u, wts["w_conv_out"], bias=sm["b_conv_out"])
    mixin = _gate_merge_fwd(y_a, y_b, gate_l, sm["b_gate"])
    mix = _matmul("mix_out", mixin, wts["w_mix_out"])
    x1 = _gated_residual_fwd("post_mix_residual", x, mix, g1, sm["g_post_mix"])
    h2 = _norm_mod_fwd("pre_ffn_norm", x1, sm["g_pre_ffn"], sc2, sh2)
    gu = _matmul("ffn_gate_up", h2, wts["w_gate_up"], b_blocks=N_CHIPS, out_dtype=BF16)
    act = _swiglu_fwd(gu)
    f = _matmul("ffn_down", act, wts["w_down"])

    dx2, df, sq, d_g2, d_gpf = _final_residual_loss(x1, f, tgt, g2, sm["g_post_ffn"])
    dact = _matmul("d_act", df, wts["w_down"], tb=True, out_dtype=BF16)
    g_w_down = _matmul("g_w_down", act, df, ta=True, out_dtype=BF16)
    dgu = _swiglu_bwd(gu, dact)
    dh2 = _matmul("d_h2", dgu, wts["w_gate_up"], tb=True, b_blocks=N_CHIPS)
    g_w_gate_up = _matmul("g_w_gate_up", h2, dgu, ta=True, out_dtype=BF16, out_blocks=N_CHIPS)
    dx1, d_gpre_ffn, d_sc2, d_sh2 = _norm_mod_bwd("pre_ffn_norm_bwd", x1, dh2, dx2, sm["g_pre_ffn"], sc2)
    dmix, d_g1, d_gpm = _gated_residual_bwd("post_mix_residual_bwd", mix, dx1, g1, sm["g_post_mix"])
    dmixin = _matmul("d_mixin", dmix, wts["w_mix_out"], tb=True, out_dtype=BF16)
    g_w_mix = _matmul("g_w_mix_out", mixin, dmix, ta=True, out_dtype=BF16)
    dy_a, dy_b, dgate_l, d_bgate, d_bco = _gate_merge_bwd(dmixin, y_a, y_b, gate_l, sm["b_gate"])
    du = _matmul("d_u", dy_b, wts["w_conv_out"], tb=True, out_dtype=BF16)
    g_w_co = _matmul("g_w_conv_out", u, dy_b, ta=True, out_dtype=BF16)
    du1, d_lng, d_lnb = _ln_silu_bwd(u1, du, sm["ln_g"], sm["ln_b"])
    du0, d_wdw, d_bdw = _dwconv_bwd("dw_conv_bwd", u0, sm["w_dw"], sm["b_dw"], du1, silu=False)
    dglu, d_bglu = _glu_bwd(glu_in, sm["b_glu"], du0)
    dyn = _matmul("d_yn", dy_a, wts["w_ssm_out"], tb=True, out_dtype=BF16)
    g_w_ssm = _matmul("g_w_ssm_out", yn, dy_a, ta=True, out_dtype=BF16)
    dy_ssd, dz, d_gssm, d_dskip_x = _gated_norm_bwd(y_ssd, z, dyn, xbc_c, sm["d_skip_x"], sm["g_ssm_norm"])
    early = [g_w_ssm.reshape(N_CHIPS, -1, d), g_w_co.reshape(N_CHIPS, -1, d), g_w_mix.reshape(N_CHIPS, -1, d),
             g_w_gate_up, g_w_down.reshape(N_CHIPS, -1, d)]
    plan_b = sums = None
    if late is not None:
        sums = _chip_sums("early", early)
        plan_b = _send_chips_plan(sums)
    (dxs_f, db_f, dc_f, ddt_f, da_f), received = _scan_bwd(
        "ssd_bwd_f", xbc_c, dy_ssd, hs_f, q_all, gam_t, etx_f, etg_f, tables, direction=0, d_ssm=d_ssm, plan=plan_b)
    (dxs_r, db_r, dc_r, ddt_r, da_r), _ = _scan_bwd(
        "ssd_bwd_r", xbc_c, dy_ssd, hs_r, q_all, gam_t, etx_r, etg_r, tables, direction=1, d_ssm=d_ssm)
    j_heads = n_heads // N_GROUPS
    da = jnp.concatenate([_from_group_lanes(da_f, j_heads), _from_group_lanes(da_r, j_heads)], axis=1)
    ddt = jnp.concatenate([_from_group_lanes(ddt_f, j_heads), _from_group_lanes(ddt_r, j_heads)], axis=1)
    ddt_raw, d_dtbias, d_alog = _dt_backward(da, ddt, dt, dt_raw, sm["dt_bias"], sm["a_neg"])
    dxbc_c = _ssd_grad_merge(dxs_f, dxs_r, dy_ssd, db_f, db_r, dc_f, dc_r, sm["d_skip_x"])
    dxbc, d_wconv, d_bconv = _dwconv_bwd("ssm_conv_bwd", xbc, sm["w_conv_ssm"], sm["b_conv_ssm"], dxbc_c,
                                         silu=True, dx_dtype=BF16)
    dsecs = [dz, dxbc, ddt_raw, dglu, dgate_l]
    dh1 = g_win = None
    for (nm, rows), dsec in zip(sections, dsecs):
        g_win = _matmul(f"g_w_in_{nm}", dsec, h1, ta=True, out_dtype=BF16, out_rows=(rows[0], sec[-1], g_win))
    hosts, sums_in, received_in = {}, None, [None] * 3
    if late is not None:
        sums_in = _chip_sums("w_in", [g_win.reshape(N_CHIPS, -1, d)])
        hosts = {"xbc": 0, "z": 1, "glu": 2}
    for (nm, rows), dsec in zip(sections, dsecs):
        plan_in = _send_chips_plan(sums_in, (hosts[nm],)) if nm in hosts else None
        dh1 = _matmul(f"d_h1_{nm}", dsec, win_t, b_rows=rows, add=dh1, plan=plan_in)
        if plan_in is not None:
            dh1, (received_in[hosts[nm]],) = dh1
    grad_x, d_gpre_mix, d_sc1, d_sh1 = _norm_mod_bwd("pre_mix_norm_bwd", x, dh1, dx1, sm["g_pre_mix"], sc1)

    dmod = jnp.concatenate([d_sh1, d_sc1, d_g1, d_sh2, d_sc2, d_g2], axis=1)
    if late is None:
        big = {"w_in_t": g_win, "w_ssm_out": g_w_ssm, "w_conv_out": g_w_co, "w_mix_out": g_w_mix,
               "w_gate_up": g_w_gate_up, "w_down": g_w_down}
    else:
        big = {"pending": (sums_in + sums, [received_in] + [[t, t, t] for t in received])}
    small = {"g_pre_mix": d_gpre_mix, "g_post_mix": d_gpm, "w_conv_ssm": d_wconv, "b_conv_ssm": d_bconv,
             "dt_bias": d_dtbias, "a_log": d_alog, "d_skip_x": d_dskip_x, "g_ssm_norm": d_gssm, "b_glu": d_bglu,
             "w_dw": d_wdw, "b_dw": d_bdw, "ln_g": d_lng, "ln_b": d_lnb, "b_conv_out": d_bco, "b_gate": d_bgate,
             "g_pre_ffn": d_gpre_ffn, "g_post_ffn": d_gpf}
    return sq, grad_x, big, small, dmod


ANY = pl.BlockSpec(memory_space=pl.ANY)
WHOLE_VMEM = pl.BlockSpec(memory_space=pltpu.VMEM)


def _mesh_place():
    x, y, c = lax.axis_index("x"), lax.axis_index("y"), lax.axis_index("c")
    other_chips = [(1 - x, y), (x, 1 - y), (1 - x, 1 - y)]
    return x, y, c, other_chips


def _remote(src, dst, send_sems, recv_sems, k, device):
    return pltpu.make_async_remote_copy(src_ref=src, dst_ref=dst, send_sem=send_sems.at[k], recv_sem=recv_sems.at[k],
                                        device_id=device, device_id_type=MESH)


def _gather_devices(name, block):
    m_per, n = block.shape

    def body(x_ref, out_ref, send_sems, recv_sems, local_sem):
        x, y, c, chips = _mesh_place()
        me, sibling = (x, y, c), (x, y, 1 - c)

        def rows(px, py, pc):
            return out_ref.at[pl.ds((4 * px + 2 * py + pc) * m_per, m_per), :]

        def copy(k, blk, to, src=None):
            return _remote(rows(*blk) if src is None else src, rows(*blk), send_sems, recv_sems, k, to)

        mine = pltpu.make_async_copy(x_ref, rows(*me), local_sem)
        mine.start()
        first = [copy(0, me, sibling, src=x_ref)]
        first += [copy(1 + j, me, (*chip, c), src=x_ref) for j, chip in enumerate(chips)]
        for cp in first:
            cp.start()
        passed = [copy(4 + j, (*chip, c), sibling) for j, chip in enumerate(chips)]
        for j, chip in enumerate(chips):
            copy(1 + j, (*chip, c), me).wait_recv()
            passed[j].start()
        copy(0, sibling, me).wait_recv()
        for j, chip in enumerate(chips):
            copy(4 + j, (*chip, 1 - c), me).wait_recv()
        for cp in first + passed:
            cp.wait_send()
        mine.wait()

    return pl.pallas_call(
        body, name=name, out_shape=jax.ShapeDtypeStruct((N_DEV * m_per, n), block.dtype),
        in_specs=[WHOLE_VMEM], out_specs=WHOLE_VMEM,
        scratch_shapes=[pltpu.SemaphoreType.DMA((7,)), pltpu.SemaphoreType.DMA((7,)), pltpu.SemaphoreType.DMA],
        compiler_params=pltpu.CompilerParams(vmem_limit_bytes=VMEM_LIMIT),
    )(block)


class _Plan:
    def __init__(self, operands, out_shapes, copies, phases):
        self.operands, self.out_shapes, self.copies, self.phases = list(operands), list(out_shapes), copies, phases

    def sems(self):
        return [pltpu.SemaphoreType.DMA((self.copies,)), pltpu.SemaphoreType.DMA((self.copies,))]


def _run_plan(name, plan):
    n_in, n_out = len(plan.operands), len(plan.out_shapes)

    def body(*refs):
        ins, outs = refs[:n_in], refs[n_in:n_in + n_out]
        send_sems, recv_sems = refs[n_in + n_out:]
        for phase in ("start", "middle", "end"):
            if phase in plan.phases:
                plan.phases[phase](ins, outs, send_sems, recv_sems)

    return list(pl.pallas_call(body, name=name, out_shape=plan.out_shapes, in_specs=[ANY] * n_in,
                               out_specs=[ANY] * n_out, scratch_shapes=plan.sems())(*plan.operands))


def _gather_plan(shards):
    n = len(shards)

    def copies(kinds, ins, outs, send_sems, recv_sems):
        x, y, c, chips = _mesh_place()
        me = 2 * x + y
        sibling = (x, y, 1 - c)

        def half(i, h):
            hr = ins[i].shape[0] // 2
            return pl.ds(h * hr, hr)

        def block(i, j, h):
            cx, cy = chips[j]
            return outs[i].at[2 * cx + cy, half(i, h)]

        make = {
            "over_ici": lambda i, j: _remote(ins[i].at[half(i, c)], outs[i].at[me, half(i, c)], send_sems, recv_sems,
                                             6 * i + j, (*chips[j], c)),
            "arrived": lambda i, j: _remote(block(i, j, c), block(i, j, c), send_sems, recv_sems, 6 * i + j, (*chips[j], c)),
            "passed_on": lambda i, j: _remote(block(i, j, c), block(i, j, c), send_sems, recv_sems, 6 * i + 3 + j, sibling),
            "from_sibling": lambda i, j: _remote(block(i, j, 1 - c), block(i, j, 1 - c), send_sems, recv_sems,
                                                 6 * i + 3 + j, sibling),
        }
        res = []
        for kind in kinds:
            if kind == "own":
                res.append([_remote(ins[i], outs[i].at[me], send_sems, recv_sems, 6 * n + i, sibling) for i in range(n)])
            else:
                res.append([make[kind](i, j) for i in range(n) for j in range(3)])
        return res

    def start(*refs):
        over_ici, own = copies(("over_ici", "own"), *refs)
        for cp in over_ici + own:
            cp.start()

    def end(*refs):
        arrived, passed_on = copies(("arrived", "passed_on"), *refs)
        for got, fwd in zip(arrived, passed_on):
            got.wait_recv()
            fwd.start()
        from_sibling, own_in = copies(("from_sibling", "own"), *refs)
        for cp in from_sibling + own_in:
            cp.wait_recv()
        over_ici, passed_on, own_out = copies(("over_ici", "passed_on", "own"), *refs)
        for cp in over_ici + passed_on + own_out:
            cp.wait_send()

    return _Plan(shards, [jax.ShapeDtypeStruct((N_CHIPS,) + s.shape, s.dtype) for s in shards], 7 * n,
                 {"start": start, "end": end})


def _send_sibling_halves(name, grads):
    n = len(grads)

    def body(*refs):
        ins, outs = refs[:n], refs[n:2 * n]
        send_sems, recv_sems = refs[2 * n:]
        x, y, c, _ = _mesh_place()
        sibling = (x, y, 1 - c)
        copies = []
        for i in range(n):
            for j in range(N_CHIPS):
                copies.append(_remote(ins[i].at[j, 1 - c], outs[i].at[j], send_sems, recv_sems, N_CHIPS * i + j, sibling))
                copies[-1].start()
        for cp in copies:
            cp.wait_recv()
        for cp in copies:
            cp.wait_send()

    return pl.pallas_call(
        body, name=name,
        out_shape=[jax.ShapeDtypeStruct((g.shape[0],) + g.shape[2:], g.dtype) for g in grads],
        in_specs=[ANY] * n, out_specs=[ANY] * n,
        scratch_shapes=[pltpu.SemaphoreType.DMA((N_CHIPS * n,)), pltpu.SemaphoreType.DMA((N_CHIPS * n,))],
    )(*grads)


def _send_chips_plan(sums, neighbours=(0, 1, 2)):
    n = len(sums)

    def copies(ins, outs, send_sems, recv_sems):
        x, y, c, chips = _mesh_place()
        return [_remote(ins[i].at[2 * chips[j][0] + chips[j][1]], outs[i].at[j], send_sems, recv_sems, 3 * i + j,
                        (*chips[j], c))
                for i in range(n) for j in neighbours]

    def start(*refs):
        for cp in copies(*refs):
            cp.start()

    def end(*refs):
        for cp in copies(*refs):
            cp.wait_recv()
        for cp in copies(*refs):
            cp.wait_send()

    return _Plan(sums, [jax.ShapeDtypeStruct((3,) + g.shape[1:], g.dtype) for g in sums], 3 * n,
                 {"start": start, "end": end})


def _exchange_halves(name, shards):
    n = len(shards)

    def body(*refs):
        outs = refs[n:2 * n]
        send_sems, recv_sems = refs[2 * n:]
        x, y, c, _ = _mesh_place()
        sibling = (x, y, 1 - c)
        remote = [_remote(outs[i].at[c], outs[i].at[c], send_sems, recv_sems, i, sibling) for i in range(n)]
        for cp in remote:
            cp.start()
        for i in range(n):
            _remote(outs[i].at[1 - c], outs[i].at[1 - c], send_sems, recv_sems, i, sibling).wait_recv()
        for cp in remote:
            cp.wait_send()

    return pl.pallas_call(
        body, name=name,
        out_shape=[jax.ShapeDtypeStruct(h.shape, h.dtype) for h in shards],
        in_specs=[ANY] * n, out_specs=[ANY] * n, input_output_aliases={i: i for i in range(n)},
        scratch_shapes=[pltpu.SemaphoreType.DMA((n,)), pltpu.SemaphoreType.DMA((n,))],
    )(*shards)


def _divisor_tile(rows, row_bytes, quantum=16):
    best = rows
    for t in range(quantum, rows + 1, quantum):
        if rows % t == 0 and 2 * t * row_bytes <= ROW_TILE_BUDGET:
            best = t
    return best


def _add_sibling(name, g4, t1):
    nb, _, hr, cols = g4.shape
    t = _divisor_tile(hr, cols * 6)

    def kern(g_ref, t_ref, o_ref):
        o_ref[0] = (g_ref[0, 0].astype(F32) + t_ref[0].astype(F32)).astype(o_ref.dtype)

    return pl.pallas_call(
        kern, name=name, grid=(nb, hr // t),
        in_specs=[pl.BlockSpec((1, 1, t, cols), lambda j, i: (j, lax.axis_index("c"), i, 0)),
                  pl.BlockSpec((1, t, cols), lambda j, i: (j, i, 0))],
        out_specs=pl.BlockSpec((1, t, cols), lambda j, i: (j, i, 0)),
        out_shape=jax.ShapeDtypeStruct((nb, hr, cols), g4.dtype),
        compiler_params=_params(("parallel", "parallel")),
    )(g4, t1)


def _add_chips(name, s1, t3):
    _, hr, cols = s1.shape
    t = _divisor_tile(hr, cols * 12)

    def kern(s_ref, t0_ref, t1_ref, t2_ref, o_ref):
        acc = s_ref[0].astype(F32)
        for t_ref in (t0_ref, t1_ref, t2_ref):
            acc = acc + t_ref[0].astype(F32)
        o_ref[0] = acc

    slot = lambda j: pl.BlockSpec((1, t, cols), functools.partial(lambda i, j: (j, i, 0), j=j))
    return pl.pallas_call(
        kern, name=name, grid=(hr // t,),
        in_specs=[pl.BlockSpec((1, t, cols), lambda i: (2 * lax.axis_index("x") + lax.axis_index("y"), i, 0)),
                  slot(0), slot(1), slot(2)],
        out_specs=pl.BlockSpec((1, t, cols), lambda i: (lax.axis_index("c"), i, 0)),
        out_shape=jax.ShapeDtypeStruct((2, hr, cols), F32),
        compiler_params=_params(("parallel",)),
    )(s1, *t3)


def _chip_sums(tag, grads):
    g4 = [g.reshape(N_CHIPS, 2, g.shape[1] // 2, g.shape[2]) for g in grads]
    t1 = _send_sibling_halves("grads_to_sibling_" + tag, g4)
    return [_add_sibling(f"chip_sum_{tag}_{i}", g, t) for i, (g, t) in enumerate(zip(g4, t1))]


def _shard_sums(sums, received):
    halves = [_add_chips(f"shard_sum_{i}", s, t) for i, (s, t) in enumerate(zip(sums, received))]
    full = _exchange_halves("grad_halves_to_sibling", halves)
    return [f.reshape(f.shape[1] * 2, f.shape[2]) for f in full]


def _pack_rows(size, width):
    return -(-size // (8 * width)) * 8


def _pack(arrays, width):
    parts = []
    for a in arrays:
        flat = a.reshape(-1).astype(F32)
        rows = _pack_rows(flat.shape[0], width)
        parts.append(jnp.pad(flat, (0, rows * width - flat.shape[0])).reshape(rows, width))
    return jnp.concatenate(parts, axis=0)


def _unpack(block, shapes, width):
    out, r = [], 0
    for shp in shapes:
        size = 1
        for s_ in shp:
            size *= s_
        rows = _pack_rows(size, width)
        out.append(block[r:r + rows].reshape(-1)[:size].reshape(shp))
        r += rows
    return out


SMALL_PARAMS = ("b_ada", "g_pre_mix", "g_post_mix", "b_conv_ssm", "dt_bias_fwd", "dt_bias_bwd", "a_log_fwd", "a_log_bwd",
                "d_skip", "g_ssm_norm", "b_glu", "b_dw", "ln_g", "ln_b", "b_conv_out", "b_gate", "g_pre_ffn", "g_post_ffn")
SHARDED_SMALL = ("w_conv_ssm", "w_dw")
MATRICES = ("w_in", "w_ssm_out", "w_conv_out", "w_mix_out", "w_gate_up", "w_down")
ALL_PARAMS = ("w_ada", "b_ada", "g_pre_mix", "g_post_mix", "w_in", "w_conv_ssm", "b_conv_ssm", "dt_bias_fwd", "dt_bias_bwd",
              "a_log_fwd", "a_log_bwd", "d_skip", "g_ssm_norm", "w_ssm_out", "b_glu", "w_dw", "b_dw", "ln_g", "ln_b",
              "w_conv_out", "b_conv_out", "b_gate", "w_mix_out", "g_pre_ffn", "g_post_ffn", "w_gate_up", "w_down")
COND_ROWS = 48
COND_CONV_ROW = 8
COND_DW_ROW = 16
MOD_ROWS = 16


def kernel(x, c, w_ada, b_ada, g_pre_mix, g_post_mix, w_in, w_conv_ssm, b_conv_ssm, dt_bias_fwd, dt_bias_bwd, a_log_fwd, a_log_bwd, d_skip, g_ssm_norm, w_ssm_out, b_glu, w_dw, b_dw, ln_g, ln_b, w_conv_out, b_conv_out, b_gate, w_mix_out, g_pre_ffn, g_post_ffn, w_gate_up, w_down, loss_target, m_w_ada, m_b_ada, m_g_pre_mix, m_g_post_mix, m_w_in, m_w_conv_ssm, m_b_conv_ssm, m_dt_bias_fwd, m_dt_bias_bwd, m_a_log_fwd, m_a_log_bwd, m_d_skip, m_g_ssm_norm, m_w_ssm_out, m_b_glu, m_w_dw, m_b_dw, m_ln_g, m_ln_b, m_w_conv_out, m_b_conv_out, m_b_gate, m_w_mix_out, m_g_pre_ffn, m_g_post_ffn, m_w_gate_up, m_w_down, v_w_ada, v_b_ada, v_g_pre_mix, v_g_post_mix, v_w_in, v_w_conv_ssm, v_b_conv_ssm, v_dt_bias_fwd, v_dt_bias_bwd, v_a_log_fwd, v_a_log_bwd, v_d_skip, v_g_ssm_norm, v_w_ssm_out, v_b_glu, v_w_dw, v_b_dw, v_ln_g, v_ln_b, v_w_conv_out, v_b_conv_out, v_b_gate, v_w_mix_out, v_g_pre_ffn, v_g_post_ffn, v_w_gate_up, v_w_down):
    given = dict(locals())
    wgt = {n: given[n][0] for n in ALL_PARAMS}
    mom = {n: given["m_" + n][0] for n in ALL_PARAMS}
    var = {n: given["v_" + n][0] for n in ALL_PARAMS}
    xs, tgt = x[0], loss_target[0]
    s, d = xs.shape
    d_ssm = 2 * d
    n_heads = d_ssm // HEAD_DIM
    d_xbc = d_ssm + 2 * N_GROUPS * D_STATE
    xi, yi, ci = lax.axis_index("x"), lax.axis_index("y"), lax.axis_index("c")
    chip = 2 * xi + yi
    dev = 2 * chip + ci
    k_conv, k_dw = wgt["w_conv_ssm"].shape[0], wgt["w_dw"].shape[0]
    xbc_shard, dw_shard = d_xbc // N_CHIPS, d // N_CHIPS

    width1 = max(d, xbc_shard)
    blk = jnp.zeros((COND_ROWS, width1), F32)
    blk = blk.at[0, :d].set(c[0])
    blk = blk.at[COND_CONV_ROW:COND_CONV_ROW + k_conv, :xbc_shard].set(wgt["w_conv_ssm"])
    blk = blk.at[COND_DW_ROW:COND_DW_ROW + k_dw, :dw_shard].set(wgt["w_dw"])
    g1 = _gather_devices("gather_cond", blk).reshape(N_DEV, COND_ROWS, width1)
    c_all = g1[:, 0, :d]
    w_conv_full = jnp.concatenate([g1[2 * k, COND_CONV_ROW:COND_CONV_ROW + k_conv, :xbc_shard] for k in range(N_CHIPS)], axis=1)
    w_dw_full = jnp.concatenate([g1[2 * k, COND_DW_ROW:COND_DW_ROW + k_dw, :dw_shard] for k in range(N_CHIPS)], axis=1)
    c_act = jnp.pad(c_all * _sigmoid(c_all), ((0, MOD_ROWS - N_DEV), (0, 0)))

    mod_part = _matmul("ada_mod", c_act, wgt["w_ada"])
    g2 = _gather_devices("gather_mod", mod_part).reshape(N_DEV, MOD_ROWS, mod_part.shape[1])
    mod_all = jnp.concatenate([g2[2 * k, :N_DEV] for k in range(N_CHIPS)], axis=1) + wgt["b_ada"][None]
    mod = lax.dynamic_slice_in_dim(mod_all, dev, 1, axis=0)

    shards = [wgt["w_in"].T.astype(BF16)] + [wgt[n].astype(BF16) for n in MATRICES[1:]]
    wts = {"w_in_t": _run_plan("gather_w_in", _gather_plan(shards[:1]))[0].reshape(-1, d)}
    late = dict(zip(MATRICES[1:], shards[1:]))
    row = lambda v: v.reshape(1, -1)
    sm = {"g_pre_mix": row(wgt["g_pre_mix"]), "g_post_mix": row(wgt["g_post_mix"]), "w_conv_ssm": w_conv_full,
          "b_conv_ssm": row(wgt["b_conv_ssm"]),
          "dt_bias": row(jnp.concatenate([wgt["dt_bias_fwd"], wgt["dt_bias_bwd"]])),
          "a_neg": row(-jnp.exp(jnp.concatenate([wgt["a_log_fwd"], wgt["a_log_bwd"]]))),
          "d_skip_x": row(jnp.repeat(wgt["d_skip"], HEAD_DIM)), "g_ssm_norm": row(wgt["g_ssm_norm"]),
          "b_glu": row(wgt["b_glu"]), "w_dw": w_dw_full, "b_dw": row(wgt["b_dw"]), "ln_g": row(wgt["ln_g"]),
          "ln_b": row(wgt["ln_b"]), "b_conv_out": row(wgt["b_conv_out"]), "b_gate": row(wgt["b_gate"]),
          "g_pre_ffn": row(wgt["g_pre_ffn"]), "g_post_ffn": row(wgt["g_post_ffn"])}

    sq, grad_x, big, small, dmod = _local_step(xs, tgt, mod, wts, sm, late=late)
    loss = lax.psum((0.5 / d) * sq[0, 0], ("x", "y", "c"))

    local_small = {"b_ada": dmod, "g_pre_mix": small["g_pre_mix"], "g_post_mix": small["g_post_mix"],
                   "b_conv_ssm": small["b_conv_ssm"], "dt_bias_fwd": small["dt_bias"][:, :n_heads],
                   "dt_bias_bwd": small["dt_bias"][:, n_heads:], "a_log_fwd": small["a_log"][:, :n_heads],
                   "a_log_bwd": small["a_log"][:, n_heads:],
                   "d_skip": jnp.sum(small["d_skip_x"].reshape(n_heads, HEAD_DIM), axis=1),
                   "g_ssm_norm": small["g_ssm_norm"], "b_glu": small["b_glu"], "b_dw": small["b_dw"],
                   "ln_g": small["ln_g"], "ln_b": small["ln_b"], "b_conv_out": small["b_conv_out"],
                   "b_gate": small["b_gate"], "g_pre_ffn": small["g_pre_ffn"], "g_post_ffn": small["g_post_ffn"],
                   "w_conv_ssm": small["w_conv_ssm"], "w_dw": small["w_dw"]}
    names = SMALL_PARAMS + SHARDED_SMALL
    pack = _pack([local_small[n] for n in names], d)
    rows_p = pack.shape[0]
    g3 = _gather_devices("gather_small_grads", pack).reshape(N_DEV, rows_p, d)
    total = _rowwise("sum_small_grads", lambda *blocks: (functools.reduce(lambda a, b: a + b, blocks),),
                     [g3[i] for i in range(N_DEV)], [], [(d, F32)])[0]
    full_shapes = [wgt[n].shape for n in SMALL_PARAMS] + [(k_conv, d_xbc), (k_dw, d)]
    summed = dict(zip(names, _unpack(total, full_shapes, d)))
    grads = {n: summed[n] for n in SMALL_PARAMS}
    grads["w_conv_ssm"] = lax.dynamic_slice_in_dim(summed["w_conv_ssm"], chip * xbc_shard, xbc_shard, axis=1)
    grads["w_dw"] = lax.dynamic_slice_in_dim(summed["w_dw"], chip * dw_shard, dw_shard, axis=1)

    dmod_all = g3[:, :N_MOD, :].reshape(N_DEV, N_MOD * d)
    ada_cols = wgt["w_ada"].shape[1]
    dmod_cols = jnp.pad(lax.dynamic_slice_in_dim(dmod_all, chip * ada_cols, ada_cols, axis=1),
                        ((0, MOD_ROWS - N_DEV), (0, 0)))
    grads["w_ada"] = _matmul("g_w_ada", c_act, dmod_cols, ta=True)

    reduced = _shard_sums(*big["pending"])
    grads["w_in"] = reduced[0].T
    for n, g in zip(MATRICES[1:], reduced[1:]):
        grads[n] = g

    delta, new_m, new_v = {}, {}, {}
    for n in ("w_ada",) + MATRICES:
        delta[n], new_m[n], new_v[n] = _adamw("adamw_" + n, wgt[n], grads[n], mom[n], var[n])
    for group, width, tag in ((SMALL_PARAMS, d, "small"), (SHARDED_SMALL, LANES, "conv")):
        shapes = [wgt[n].shape for n in group]
        packs = [_pack([src[n] for n in group], width) for src in (wgt, grads, mom, var)]
        outs = _adamw("adamw_" + tag, *packs)
        for res, o in zip((delta, new_m, new_v), outs):
            res.update(zip(group, _unpack(o, shapes, width)))

    lead = lambda a: a[None]
    return (loss, grad_x[None], *[lead(grads[n]) for n in ALL_PARAMS], *[lead(delta[n]) for n in ALL_PARAMS],
            *[lead(new_m[n]) for n in ALL_PARAMS], *[lead(new_v[n]) for n in ALL_PARAMS])
```

```python
import functools

import jax
import jax.numpy as jnp
from jax import lax
from jax.experimental import pallas as pl
from jax.experimental.pallas import tpu as pltpu

F32 = jnp.float32
BF16 = jnp.bfloat16

N_GROUPS = 8
HEAD_DIM = 64
D_STATE = 128
CHUNK = 128
EPS = 1e-6
N_MOD = 6
ADAM_LR = 0.001
ADAM_B1 = 0.9
ADAM_B2 = 0.999
ADAM_EPS = 1e-08
ADAM_WD = 0.01
ADAM_STEP = 10

V7X_VMEM_BYTES = 64 * 1024 * 1024
VMEM_LIMIT = V7X_VMEM_BYTES - 8 * 1024 * 1024
ROW_TILE_BUDGET = 20 * 1024 * 1024
LANES = 128
NEG = -1e30
MESH = pl.DeviceIdType.MESH
N_CHIPS = 4
N_DEV = 8


def _params(sem):
    return pltpu.CompilerParams(dimension_semantics=sem, vmem_limit_bytes=VMEM_LIMIT)


def _sigmoid(x):
    return 1.0 / (1.0 + jnp.exp(-x))


def _silu(x):
    return x * _sigmoid(x)


def _dsilu(x):
    s = _sigmoid(x)
    return s * (1.0 + x * (1.0 - s))


def _softplus(x):
    return jnp.maximum(x, 0.0) + jnp.log(1.0 + jnp.exp(-jnp.abs(x)))


def _sum0(a):
    return jnp.sum(a, axis=0, keepdims=True)


def _mean1(a):
    return jnp.mean(a, axis=1, keepdims=True)


def _rowwise(name, body, rows, bcasts, out_rows, out_accs=(), tile=None, plan=None):
    rows = [r if isinstance(r, tuple) else (r, r.shape[1], 0) for r in rows]
    s = rows[0][0].shape[0]
    if tile is None:
        per_row = sum(w * a.dtype.itemsize for a, w, _ in rows) + sum(w * jnp.dtype(dt).itemsize for w, dt in out_rows)
        tile = 1024
        while tile > 16 and (tile * per_row * 2 > ROW_TILE_BUDGET or s % tile):
            tile //= 2
        if s * per_row * 2 <= ROW_TILE_BUDGET:
            tile = s
    assert s % tile == 0
    n_in = len(rows) + len(bcasts)
    n_o = len(out_rows)
    n_out = n_o + len(out_accs)
    p_ops, p_in_specs, p_out_shapes, p_out_specs, p_scratch, _ = _plan_call_parts(plan)
    n_pi, n_po = len(p_ops), len(p_out_shapes)
    n_steps = (s // tile,)

    def kern(*refs):
        plan_ins = refs[n_in:n_in + n_pi]
        outs = refs[n_in + n_pi:n_in + n_pi + n_out]
        plan_outs = refs[n_in + n_pi + n_out:n_in + n_pi + n_out + n_po]
        plan_sems = refs[n_in + n_pi + n_out + n_po:]
        _plan_phase(plan, "start", n_steps, plan_ins, plan_outs, plan_sems)
        _plan_phase(plan, "middle", n_steps, plan_ins, plan_outs, plan_sems)
        res = body(*[r[...].astype(F32) for r in refs[:n_in]])
        for o, v in zip(outs[:n_o], res[:n_o]):
            o[...] = v.astype(o.dtype)
        if out_accs:
            @pl.when(pl.program_id(0) == 0)
            def _():
                for o in outs[n_o:]:
                    o[...] = jnp.zeros_like(o)
            for o, v in zip(outs[n_o:], res[n_o:]):
                o[...] += v
        _plan_phase(plan, "end", n_steps, plan_ins, plan_outs, plan_sems)

    in_specs = [pl.BlockSpec((tile, w), functools.partial(lambda i, cb: (i, cb), cb=cb)) for _, w, cb in rows]
    in_specs += [pl.BlockSpec(b.shape, functools.partial(lambda i, nd: (0,) * nd, nd=b.ndim)) for b in bcasts]
    out_shape = [jax.ShapeDtypeStruct((s, w), dt) for w, dt in out_rows]
    out_shape += [jax.ShapeDtypeStruct((1, w), F32) for w in out_accs]
    out_specs = [pl.BlockSpec((tile, w), lambda i: (i, 0)) for w, _ in out_rows]
    out_specs += [pl.BlockSpec((1, w), lambda i: (0, 0)) for w in out_accs]
    return pl.pallas_call(
        kern, name=name, grid=n_steps, in_specs=in_specs + p_in_specs, out_specs=out_specs + p_out_specs,
        out_shape=out_shape + p_out_shapes, scratch_shapes=p_scratch,
        compiler_params=_params(("arbitrary",) if out_accs or plan is not None else ("parallel",)),
    )(*[a for a, _, _ in rows], *bcasts, *p_ops)


def _tile(n, pref):
    if n <= pref:
        return n
    t = (pref // LANES) * LANES
    while t >= LANES:
        if n % t == 0:
            return t
        t -= LANES
    return n


def _matmul(name, a, b, *, ta=False, tb=False, out_dtype=F32, bias=None, add=None, b_blocks=1, out_blocks=1,
            b_rows=None, out_rows=None, plan=None, tm=1024, tn=1408, tk=2816):
    m, k = (a.shape[1], a.shape[0]) if ta else a.shape
    if b_blocks > 1:
        rows_b, cols_b = b.shape[1], b.shape[2] * b_blocks
    else:
        rows_b, cols_b = b.shape
    if b_rows is not None:
        rows_b = b_rows[1]
    n, kb = (rows_b, cols_b) if tb else (cols_b, rows_b)
    assert k == kb, (name, a.shape, b.shape)
    tm, tn, tk = _tile(m, tm), _tile(n, tn), _tile(k, tk)
    if b_blocks > 1:
        per = cols_b // b_blocks
        if tb:
            tk = _tile(per, tk)
        else:
            tn = _tile(per, tn)
    if out_blocks > 1:
        tn = _tile(n // out_blocks, tn)
    nk = k // tk
    grid = (m // tm, n // tn, nk)

    a_spec = pl.BlockSpec((tk, tm), lambda i, j, kk: (kk, i)) if ta else pl.BlockSpec((tm, tk), lambda i, j, kk: (i, kk))
    if b_blocks > 1:
        if tb:
            nb = per // tk
            b_spec = pl.BlockSpec((1, tn, tk), lambda i, j, kk: (kk // nb, j, kk % nb))
        else:
            nb = per // tn
            b_spec = pl.BlockSpec((1, tk, tn), lambda i, j, kk: (j // nb, kk, j % nb))
    elif b_rows is not None:
        first = b_rows[0]
        if tb:
            b_spec = pl.BlockSpec((pl.Element(tn), pl.Element(tk)),
                                  lambda i, j, kk: (pl.multiple_of(first + j * tn, LANES), kk * tk))
        else:
            b_spec = pl.BlockSpec((pl.Element(tk), pl.Element(tn)),
                                  lambda i, j, kk: (pl.multiple_of(first + kk * tk, LANES), j * tn))
    else:
        b_spec = pl.BlockSpec((tn, tk), lambda i, j, kk: (j, kk)) if tb else pl.BlockSpec((tk, tn), lambda i, j, kk: (kk, j))
    in_specs = [a_spec, b_spec]
    operands = [a, b]
    if bias is not None:
        in_specs.append(pl.BlockSpec((1, tn), lambda i, j, kk: (0, j)))
        operands.append(bias)
    if add is not None:
        in_specs.append(pl.BlockSpec((tm, tn), lambda i, j, kk: (i, j)))
        operands.append(add)
    aliases = {}
    if out_blocks > 1:
        nbo = (n // out_blocks) // tn
        out_spec = pl.BlockSpec((1, tm, tn), lambda i, j, kk: (j // nbo, i, j % nbo))
        out_shape = jax.ShapeDtypeStruct((out_blocks, m, n // out_blocks), out_dtype)
    elif out_rows is not None:
        first_out, total, previous = out_rows
        out_spec = pl.BlockSpec((pl.Element(tm), pl.Element(tn)),
                                lambda i, j, kk: (pl.multiple_of(first_out + i * tm, LANES), j * tn))
        out_shape = jax.ShapeDtypeStruct((total, n), out_dtype)
        if previous is not None:
            aliases = {len(operands): 0}
            in_specs.append(pl.BlockSpec(memory_space=pl.ANY))
            operands.append(previous)
    else:
        out_spec = pl.BlockSpec((tm, tn), lambda i, j, kk: (i, j))
        out_shape = jax.ShapeDtypeStruct((m, n), out_dtype)
    dims = (((0 if ta else 1,), (1 if tb else 0,)), ((), ()))
    has_bias, has_add, has_previous = bias is not None, add is not None, bool(aliases)
    p_ops, p_in_specs, p_out_shapes, p_out_specs, p_scratch, _ = _plan_call_parts(plan)
    n_pi, n_po = len(p_ops), len(p_out_shapes)

    def kern(*refs):
        a_ref, b_ref = refs[0], refs[1]
        pos = 2
        bias_ref = add_ref = None
        if has_bias:
            bias_ref = refs[pos]
            pos += 1
        if has_add:
            add_ref = refs[pos]
            pos += 1
        if has_previous:
            pos += 1
        plan_ins = refs[pos:pos + n_pi]
        o_ref = refs[pos + n_pi]
        plan_outs = refs[pos + n_pi + 1:pos + n_pi + 1 + n_po]
        pos += n_pi + 1 + n_po
        acc_ref = refs[pos] if nk > 1 else None
        plan_sems = refs[pos + (1 if nk > 1 else 0):]
        _plan_phase(plan, "start", grid, plan_ins, plan_outs, plan_sems)
        av = a_ref[...].astype(BF16)
        bv = (b_ref[0] if b_blocks > 1 else b_ref[...]).astype(BF16)
        p = lax.dot_general(av, bv, dims, preferred_element_type=F32)

        def finish(acc):
            if has_bias:
                acc = acc + bias_ref[...]
            if has_add:
                acc = acc + add_ref[...]
            if out_blocks > 1:
                o_ref[0] = acc.astype(o_ref.dtype)
            else:
                o_ref[...] = acc.astype(o_ref.dtype)

        if nk == 1:
            finish(p)
        else:
            kk = pl.program_id(2)

            @pl.when(kk == 0)
            def _():
                acc_ref[...] = p

            @pl.when(kk > 0)
            def _():
                acc_ref[...] += p

            @pl.when(kk == nk - 1)
            def _():
                finish(acc_ref[...])
        _plan_phase(plan, "end", grid, plan_ins, plan_outs, plan_sems)

    res = pl.pallas_call(
        kern, name=name, grid=grid, in_specs=in_specs + p_in_specs, out_specs=[out_spec] + p_out_specs,
        out_shape=[out_shape] + p_out_shapes, input_output_aliases=aliases,
        scratch_shapes=([pltpu.VMEM((tm, tn), F32)] if nk > 1 else []) + p_scratch,
        compiler_params=_params(("parallel", "parallel", "arbitrary") if plan is None else ("arbitrary",) * 3),
    )(*operands, *p_ops)
    return res[0] if plan is None else (res[0], list(res[1:]))


CONV_HALO = 16
CONV_ROWS = 256


def _taps(win, shifts, rows):
    n = win.shape[0]
    for j, s in enumerate(shifts):
        yield j, (pltpu.roll(win, (n - s) % n, axis=0) if s % n else win)[:rows]


def _dwconv_fwd(name, x, w, b, *, silu, out_dtype=F32, plan=None):
    s, c = x.shape
    k = w.shape[0]
    pad = (k - 1) // 2
    assert pad <= CONV_HALO and c % LANES == 0
    t = min(CONV_ROWS, s)
    n_chunks = s // t
    fwd_shifts = [CONV_HALO - pad + j for j in range(k)]
    p_ops, p_in_specs, p_out_shapes, p_out_specs, p_scratch, _ = _plan_call_parts(plan)
    n_pi, n_po = len(p_ops), len(p_out_shapes)
    n_steps = (c // LANES,)

    def kern(*refs):
        x_ref, w_ref, b_ref = refs[:3]
        plan_ins = refs[3:3 + n_pi]
        o_ref = refs[3 + n_pi]
        plan_outs = refs[4 + n_pi:4 + n_pi + n_po]
        xp_ref = refs[4 + n_pi + n_po]
        plan_sems = refs[5 + n_pi + n_po:]
        _plan_phase(plan, "start", n_steps, plan_ins, plan_outs, plan_sems)
        _plan_phase(plan, "middle", n_steps, plan_ins, plan_outs, plan_sems)
        zeros = jnp.zeros((CONV_HALO, LANES), F32)
        xp_ref[0:CONV_HALO, :] = zeros
        xp_ref[CONV_HALO + s:CONV_HALO + s + CONV_HALO, :] = zeros
        xp_ref[CONV_HALO:CONV_HALO + s, :] = x_ref[...].astype(F32)
        bv = b_ref[...]

        def chunk(i, carry):
            base = pl.multiple_of(i * t, 16)
            win = xp_ref[pl.ds(base, t + 2 * CONV_HALO), :]
            acc = jnp.zeros((t, LANES), F32)
            for j, xs in _taps(win, fwd_shifts, t):
                acc = acc + xs * w_ref[pl.ds(j, 1), :]
            acc = acc + bv
            o_ref[pl.ds(base, t), :] = (_silu(acc) if silu else acc).astype(o_ref.dtype)
            return carry

        lax.fori_loop(0, n_chunks, chunk, 0)
        _plan_phase(plan, "end", n_steps, plan_ins, plan_outs, plan_sems)

    res = pl.pallas_call(
        kern, name=name, grid=n_steps,
        in_specs=[pl.BlockSpec((s, LANES), lambda i: (0, i)), pl.BlockSpec((k, LANES), lambda i: (0, i)),
                  pl.BlockSpec((1, LANES), lambda i: (0, i))] + p_in_specs,
        out_specs=[pl.BlockSpec((s, LANES), lambda i: (0, i))] + p_out_specs,
        out_shape=[jax.ShapeDtypeStruct((s, c), out_dtype)] + p_out_shapes,
        scratch_shapes=[pltpu.VMEM((s + 2 * CONV_HALO, LANES), F32)] + p_scratch,
        compiler_params=_params(("parallel",) if plan is None else ("arbitrary",)),
    )(x, w, b, *p_ops)
    return res[0] if plan is None else (res[0], list(res[1:]))


def _dwconv_bwd(name, x, w, b, dout, *, silu, dx_dtype=F32):
    s, c = x.shape
    k = w.shape[0]
    pad = (k - 1) // 2
    t = min(CONV_ROWS, s)
    n_chunks = s // t
    fwd_shifts = [CONV_HALO - pad + j for j in range(k)]
    bwd_shifts = [CONV_HALO + pad - j for j in range(k)]

    def kern(x_ref, w_ref, b_ref, do_ref, dx_ref, dw_ref, db_ref, xp_ref, dp_ref):
        zeros = jnp.zeros((CONV_HALO, LANES), F32)
        for ref in (xp_ref, dp_ref):
            ref[0:CONV_HALO, :] = zeros
            ref[CONV_HALO + s:CONV_HALO + s + CONV_HALO, :] = zeros
        xp_ref[CONV_HALO:CONV_HALO + s, :] = x_ref[...].astype(F32)
        bv = b_ref[...]
        dw_ref[...] = jnp.zeros_like(dw_ref)

        def pre_chunk(i, dbias):
            base = pl.multiple_of(i * t, 16)
            win = xp_ref[pl.ds(base, t + 2 * CONV_HALO), :]
            dpre = do_ref[pl.ds(base, t), :].astype(F32)
            if silu:
                acc = jnp.zeros((t, LANES), F32)
                for j, xs in _taps(win, fwd_shifts, t):
                    acc = acc + xs * w_ref[pl.ds(j, 1), :]
                dpre = dpre * _dsilu(acc + bv)
            dp_ref[pl.ds(base + CONV_HALO, t), :] = dpre
            for j, xs in _taps(win, fwd_shifts, t):
                dw_ref[pl.ds(j, 1), :] += _sum0(dpre * xs)
            return dbias + _sum0(dpre)

        db_ref[...] = lax.fori_loop(0, n_chunks, pre_chunk, jnp.zeros((1, LANES), F32))

        def dx_chunk(i, carry):
            base = pl.multiple_of(i * t, 16)
            win = dp_ref[pl.ds(base, t + 2 * CONV_HALO), :]
            acc = jnp.zeros((t, LANES), F32)
            for j, dps in _taps(win, bwd_shifts, t):
                acc = acc + dps * w_ref[pl.ds(j, 1), :]
            dx_ref[pl.ds(base, t), :] = acc.astype(dx_ref.dtype)
            return carry

        lax.fori_loop(0, n_chunks, dx_chunk, 0)

    col = lambda rows: pl.BlockSpec((rows, LANES), lambda i: (0, i))
    return pl.pallas_call(
        kern, name=name, grid=(c // LANES,),
        in_specs=[col(s), col(k), col(1), col(s)],
        out_specs=[col(s), col(k), col(1)],
        out_shape=[jax.ShapeDtypeStruct((s, c), dx_dtype), jax.ShapeDtypeStruct((k, c), F32),
                   jax.ShapeDtypeStruct((1, c), F32)],
        scratch_shapes=[pltpu.VMEM((s + 2 * CONV_HALO, LANES), F32), pltpu.VMEM((s + 2 * CONV_HALO, LANES), F32)],
        compiler_params=_params(("parallel",)),
    )(x, w, b, dout)


_NT =(((1,), (1,)), ((), ()))
_TN = (((0,), (0,)), ((), ()))


def _dot(a, b, dims=None):
    if dims is None:
        return jnp.dot(a, b, preferred_element_type=F32)
    return lax.dot_general(a, b, dims, preferred_element_type=F32)


HEAD_QUANTITIES = 4
GROUPS_PER_STEP = 4


def _scan_tables(n_heads):
    j_heads = n_heads // N_GROUPS
    used = 3 * HEAD_QUANTITIES * j_heads
    assert used <= LANES and j_heads % 2 == 0 and N_GROUPS % GROUPS_PER_STEP == 0
    gw = j_heads * HEAD_DIM
    r = jnp.arange(LANES)[:, None]

    def expand(quantity, width):
        head_of_lane = jnp.arange(j_heads * width)[None] // width
        return ((r // (3 * j_heads) == quantity) & (r % j_heads == head_of_lane) & (r < used)).astype(BF16)

    sel_cols = (jnp.arange(gw)[:, None] // HEAD_DIM == jnp.arange(LANES)[None]).astype(BF16)
    h2 = 2 * n_heads
    rows = jnp.arange(3 * HEAD_QUANTITIES * h2)
    head = rows % n_heads
    col = ((rows % h2) // n_heads * N_GROUPS + head // j_heads) * LANES + (rows // h2) * j_heads + head % j_heads
    route = (col[:, None] == jnp.arange(2 * N_GROUPS * LANES)[None]).astype(BF16)
    return {"ex_dt": expand(0, HEAD_DIM), "ex_gam": expand(1, CHUNK), "ex_din": expand(2, HEAD_DIM),
            "ex_dst": expand(3, HEAD_DIM), "sel_cols": sel_cols, "route": route}


def _chunk_decay(etot, n_heads, direction):
    j_heads = n_heads // N_GROUPS
    ed = etot[:, :, direction * n_heads:(direction + 1) * n_heads]
    per_group = jnp.pad(ed.reshape(ed.shape[0], 8, N_GROUPS, j_heads), ((0, 0), (0, 0), (0, 0), (0, LANES - j_heads)))
    return jnp.repeat(ed, HEAD_DIM, axis=2), per_group.reshape(ed.shape[0], 8, N_GROUPS * LANES)


def _scan_specs(reverse_order, direction, nc, j_heads, d_ssm):
    gps = GROUPS_PER_STEP
    gw = j_heads * HEAD_DIM
    b_off = d_ssm // (gps * D_STATE)
    c_off = b_off + N_GROUPS // gps
    d_off = direction * (N_GROUPS // gps)
    zz = (lambda z: nc - 1 - z) if reverse_order else (lambda z: z)
    const = lambda shape: pl.BlockSpec(shape, lambda g, z: (0,) * len(shape))
    return {
        "xs": pl.BlockSpec((CHUNK, gps * gw), lambda g, z: (zz(z), g)),
        "b": pl.BlockSpec((CHUNK, gps * D_STATE), lambda g, z: (zz(z), b_off + g)),
        "c": pl.BlockSpec((CHUNK, gps * D_STATE), lambda g, z: (zz(z), c_off + g)),
        "q": pl.BlockSpec((CHUNK, gps * LANES), lambda g, z: (zz(z), d_off + g)),
        "gam_t": pl.BlockSpec((gps * j_heads, CHUNK), lambda g, z: (d_off + g, zz(z))),
        "etot_x": pl.BlockSpec((1, 8, gps * gw), lambda g, z: (zz(z), 0, g)),
        "etot_g": pl.BlockSpec((1, 8, gps * LANES), lambda g, z: (zz(z), 0, g)),
        "state": pl.BlockSpec((gps, 1, D_STATE, gw), lambda g, z: (g, zz(z), 0, 0)),
        "grp": pl.BlockSpec((CHUNK, gps * D_STATE), lambda g, z: (zz(z), g)),
        "ex": const((LANES, gw)), "ex_gam": const((LANES, j_heads * CHUNK)), "sel": const((gw, LANES)),
    }


def _scan_masks(reverse):
    li = lax.broadcasted_iota(jnp.int32, (CHUNK, CHUNK), 0)
    si = lax.broadcasted_iota(jnp.int32, (CHUNK, CHUNK), 1)
    mask = (li <= si) if reverse else (li >= si)
    mask_t = (si <= li) if reverse else (si >= li)
    return li, si, mask, mask_t, si < HEAD_DIM


def _plan_phase(plan, phase, n_steps, ins, outs, sems):
    if plan is None or phase not in plan.phases:
        return
    ids = [pl.program_id(i) for i in range(len(n_steps))]
    first = {"start": 0, "middle": n_steps[0] // 2, "end": n_steps[0] - 1}[phase]
    when = ids[0] == first
    for i, n in zip(ids[1:], n_steps[1:]):
        when = when & (i == (n - 1 if phase == "end" else 0))

    @pl.when(when)
    def _():
        plan.phases[phase](ins, outs, *sems)


def _plan_call_parts(plan):
    if plan is None:
        return [], [], [], [], [], ("parallel", "arbitrary")
    n_in, n_out = len(plan.operands), len(plan.out_shapes)
    return plan.operands, [ANY] * n_in, plan.out_shapes, [ANY] * n_out, plan.sems(), ("arbitrary", "arbitrary")


def _scan_fwd(name, xbc_c, q_all, gam_t, etot_x, tb, *, direction, d_ssm, plan=None):
    s = xbc_c.shape[0]
    nc = s // CHUNK
    j_heads = tb["ex_dt"].shape[1] // HEAD_DIM
    gw = j_heads * HEAD_DIM
    gps = GROUPS_PER_STEP
    reverse = direction == 1
    sp = _scan_specs(reverse, direction, nc, j_heads, d_ssm)
    p_ops, p_in_specs, p_out_shapes, p_out_specs, p_scratch, semantics = _plan_call_parts(plan)
    n_pi, n_po = len(p_ops), len(p_out_shapes)
    n_steps = (N_GROUPS // gps, nc)

    def kern(*refs):
        xs_ref, b_ref, c_ref, q_ref, gamt_ref, etx_ref, exdt_ref, exgam_ref, exdin_ref, exdst_ref = refs[:10]
        plan_ins = refs[10:10 + n_pi]
        y_ref, hs_ref = refs[10 + n_pi:12 + n_pi]
        plan_outs = refs[12 + n_pi:12 + n_pi + n_po]
        h_ref = refs[12 + n_pi + n_po]
        plan_sems = refs[13 + n_pi + n_po:]
        _plan_phase(plan, "start", n_steps, plan_ins, plan_outs, plan_sems)
        _plan_phase(plan, "middle", n_steps, plan_ins, plan_outs, plan_sems)

        @pl.when(pl.program_id(1) == 0)
        def _():
            h_ref[...] = jnp.zeros_like(h_ref)

        _, _, mask, _, lo = _scan_masks(reverse)
        for gi in range(gps):
            bb = b_ref[:, gi * D_STATE:(gi + 1) * D_STATE].astype(BF16)
            cb = c_ref[:, gi * D_STATE:(gi + 1) * D_STATE].astype(BF16)
            cbt = _dot(cb, bb, _NT)
            q = q_ref[:, gi * LANES:(gi + 1) * LANES]
            dtx, dinx, dstx = _dot(q, exdt_ref[...]), _dot(q, exdin_ref[...]), _dot(q, exdst_ref[...])
            gcol = _dot(q, exgam_ref[...])
            xdt = xs_ref[:, gi * gw:(gi + 1) * gw].astype(F32) * dtx
            ht = h_ref[gi]
            y_off = _dot(cb, ht.astype(BF16)) * dinx
            hs_ref[gi, 0] = ht
            for p in range(j_heads // 2):
                lanes = slice(p * CHUNK, (p + 1) * CHUNK)
                x2 = xdt[:, lanes]
                acc = y_off[:, lanes]
                for idx, j in enumerate((2 * p, 2 * p + 1)):
                    g_row = gamt_ref[pl.ds(gi * j_heads + j, 1), :]
                    decay = jnp.exp(jnp.where(mask, gcol[:, j * CHUNK:(j + 1) * CHUNK] - g_row, NEG))
                    x_head = jnp.where(lo if idx == 0 else jnp.logical_not(lo), x2, 0.0).astype(BF16)
                    acc = acc + _dot((cbt * decay).astype(BF16), x_head)
                y_ref[:, gi * gw + p * CHUNK:gi * gw + (p + 1) * CHUNK] = acc.astype(y_ref.dtype)
            h_ref[gi] = ht * etx_ref[0, 0:1, gi * gw:(gi + 1) * gw] + _dot(bb, (xdt * dstx).astype(BF16), _TN)
        _plan_phase(plan, "end", n_steps, plan_ins, plan_outs, plan_sems)

    res = pl.pallas_call(
        kern, name=name, grid=n_steps,
        in_specs=[sp["xs"], sp["b"], sp["c"], sp["q"], sp["gam_t"], sp["etot_x"], sp["ex"], sp["ex_gam"], sp["ex"],
                  sp["ex"]] + p_in_specs,
        out_specs=[sp["xs"], sp["state"]] + p_out_specs,
        out_shape=[jax.ShapeDtypeStruct((s, d_ssm), BF16),
                   jax.ShapeDtypeStruct((N_GROUPS, nc, D_STATE, gw), F32)] + p_out_shapes,
        scratch_shapes=[pltpu.VMEM((gps, D_STATE, gw), F32)] + p_scratch,
        compiler_params=_params(semantics),
    )(xbc_c, xbc_c, xbc_c, q_all, gam_t, etot_x, tb["ex_dt"], tb["ex_gam"], tb["ex_din"], tb["ex_dst"], *p_ops)
    return res[0], res[1], list(res[2:])


def _scan_bwd(name, xbc_c, dy, hs, q_all, gam_t, etot_x, etot_g, tb, *, direction, d_ssm, plan=None):
    s = xbc_c.shape[0]
    nc = s // CHUNK
    j_heads = tb["ex_dt"].shape[1] // HEAD_DIM
    gw = j_heads * HEAD_DIM
    gps = GROUPS_PER_STEP
    reverse = direction == 1
    sp = _scan_specs(not reverse, direction, nc, j_heads, d_ssm)
    hp = lax.Precision.HIGHEST
    p_ops, p_in_specs, p_out_shapes, p_out_specs, p_scratch, semantics = _plan_call_parts(plan)
    n_pi, n_po = len(p_ops), len(p_out_shapes)
    n_steps = (N_GROUPS // gps, nc)

    def kern(*refs):
        (xs_ref, b_ref, c_ref, dy_ref, hs_ref, q_ref, gamt_ref, etg_ref, etx_ref, exdt_ref, exgam_ref, exdin_ref,
         exdst_ref, sel_ref) = refs[:14]
        plan_ins = refs[14:14 + n_pi]
        dxs_ref, db_ref, dc_ref, ddt_ref, da_ref = refs[14 + n_pi:19 + n_pi]
        plan_outs = refs[19 + n_pi:19 + n_pi + n_po]
        dh_ref, tmp_ref = refs[19 + n_pi + n_po:21 + n_pi + n_po]
        plan_sems = refs[21 + n_pi + n_po:]
        _plan_phase(plan, "start", n_steps, plan_ins, plan_outs, plan_sems)
        _plan_phase(plan, "middle", n_steps, plan_ins, plan_outs, plan_sems)

        @pl.when(pl.program_id(1) == 0)
        def _():
            dh_ref[...] = jnp.zeros_like(dh_ref)

        li, si, mask, mask_t, lo = _scan_masks(reverse)
        sel = sel_ref[...]
        incl = ((si <= li) if reverse else (si >= li)).astype(F32)
        excl = ((si > li) if reverse else (si < li)).astype(F32)
        for gi in range(gps):
            grp_lanes = slice(gi * D_STATE, (gi + 1) * D_STATE)
            bb = b_ref[:, grp_lanes].astype(BF16)
            cb = c_ref[:, grp_lanes].astype(BF16)
            cbt = _dot(cb, bb, _NT)
            cbt_t = _dot(bb, cb, _NT)
            q = q_ref[:, gi * LANES:(gi + 1) * LANES]
            dtx, dinx, dstx = _dot(q, exdt_ref[...]), _dot(q, exdin_ref[...]), _dot(q, exdst_ref[...])
            gcol = _dot(q, exgam_ref[...])
            x_all = xs_ref[:, gi * gw:(gi + 1) * gw].astype(F32)
            dy_all = dy_ref[:, gi * gw:(gi + 1) * gw].astype(F32)
            xdt = x_all * dtx
            xb = xdt.astype(BF16)
            ht = hs_ref[gi, 0]
            hb = ht.astype(BF16)
            dht = dh_ref[gi]
            dhb = dht.astype(BF16)
            y_off = _dot(cb, hb) * dinx
            dx_off = _dot(bb, dhb) * dstx
            dyd = (dy_all * dinx).astype(BF16)
            xd = (xdt * dstx).astype(BF16)
            dc_acc = _dot(dyd, hb, _NT)
            db_acc = _dot(xd, dhb, _NT)
            dh_ref[gi] = dht * etx_ref[0, 0:1, gi * gw:(gi + 1) * gw] + _dot(cb, dyd, _TN)
            q_cols = _dot((dy_all * y_off).astype(BF16), sel)
            c_cols = _dot((xdt * dx_off).astype(BF16), sel)
            through = _sum0(_dot((dht * ht).astype(BF16), sel))
            dcbt = jnp.zeros((CHUNK, CHUNK), F32)
            for p in range(j_heads // 2):
                lanes = slice(p * CHUNK, (p + 1) * CHUNK)
                out_lanes = slice(gi * gw + p * CHUNK, gi * gw + (p + 1) * CHUNK)
                x2b = xb[:, lanes]
                dy2 = dy_all[:, lanes]
                acc = dx_off[:, lanes]
                for idx, j in enumerate((2 * p, 2 * p + 1)):
                    gc = gcol[:, j * CHUNK:(j + 1) * CHUNK]
                    gr = gamt_ref[pl.ds(gi * j_heads + j, 1), :]
                    decay = jnp.exp(jnp.where(mask, gc - gr, NEG))
                    decay_t = jnp.exp(jnp.where(mask_t, gr - gc, NEG))
                    dy_head = jnp.where(lo if idx == 0 else jnp.logical_not(lo), dy2, 0.0).astype(BF16)
                    acc = acc + _dot((cbt_t * decay_t).astype(BF16), dy_head)
                    dm = decay * _dot(dy_head, x2b, _NT)
                    dcbt = dcbt + dm
                    e = (cbt * dm).astype(BF16)
                    in_lane_j = si == j
                    q_cols = (q_cols + jnp.where(in_lane_j, jnp.sum(e.astype(F32), axis=1, keepdims=True), 0.0)
                              - _dot(e, jnp.where(in_lane_j, 1.0, 0.0).astype(BF16), _TN))
                dxs_ref[:, out_lanes] = (acc * dtx[:, lanes]).astype(dxs_ref.dtype)
                tmp_ref[:, lanes] = acc * x_all[:, lanes]
            dcb = dcbt.astype(BF16)
            dc_ref[:, grp_lanes] = (dc_acc + _dot(dcb, bb)).astype(dc_ref.dtype)
            db_ref[:, grp_lanes] = (db_acc + _dot(dcb, cb, _TN)).astype(db_ref.dtype)
            ddt_ref[:, gi * LANES:(gi + 1) * LANES] = _dot(tmp_ref[...].astype(BF16), sel)
            da_ref[:, gi * LANES:(gi + 1) * LANES] = (
                jnp.dot(incl, q_cols, preferred_element_type=F32, precision=hp)
                + jnp.dot(excl, c_cols, preferred_element_type=F32, precision=hp)
                + through * etg_ref[0, 0:1, gi * LANES:(gi + 1) * LANES])
        _plan_phase(plan, "end", n_steps, plan_ins, plan_outs, plan_sems)

    gn = N_GROUPS * D_STATE
    res = pl.pallas_call(
        kern, name=name, grid=n_steps,
        in_specs=[sp["xs"], sp["b"], sp["c"], sp["xs"], sp["state"], sp["q"], sp["gam_t"], sp["etot_g"], sp["etot_x"],
                  sp["ex"], sp["ex_gam"], sp["ex"], sp["ex"], sp["sel"]] + p_in_specs,
        out_specs=[sp["xs"], sp["grp"], sp["grp"], sp["grp"], sp["grp"]] + p_out_specs,
        out_shape=[jax.ShapeDtypeStruct((s, d_ssm), BF16), jax.ShapeDtypeStruct((s, gn), BF16),
                   jax.ShapeDtypeStruct((s, gn), BF16), jax.ShapeDtypeStruct((s, gn), F32),
                   jax.ShapeDtypeStruct((s, gn), F32)] + p_out_shapes,
        scratch_shapes=[pltpu.VMEM((gps, D_STATE, gw), F32), pltpu.VMEM((CHUNK, gw), F32)] + p_scratch,
        compiler_params=_params(semantics),
    )(xbc_c, xbc_c, xbc_c, dy, hs, q_all, gam_t, etot_g, etot_x, tb["ex_dt"], tb["ex_gam"], tb["ex_din"], tb["ex_dst"],
      tb["sel_cols"], *p_ops)
    return res[:5], list(res[5:])


def _dt_prepare(dt_raw, dt_bias, a_neg, route):
    s, h2 = dt_raw.shape
    n_heads = h2 // 2
    qw = route.shape[1]

    def kern(raw_ref, bias_ref, a_ref, route_ref, dt_ref, q_ref, gamt_ref, etot_ref):
        dt = _softplus(raw_ref[...] + bias_ref[...])
        a = dt * a_ref[...]
        li = lax.broadcasted_iota(jnp.int32, (CHUNK, CHUNK), 0)
        si = lax.broadcasted_iota(jnp.int32, (CHUNK, CHUNK), 1)
        tri = (li >= si).astype(F32)
        cs = jnp.dot(tri, a, preferred_element_type=F32, precision=lax.Precision.HIGHEST)
        tot = _sum0(a)
        fwd = lax.broadcasted_iota(jnp.int32, (CHUNK, h2), 1) < n_heads
        gam = jnp.where(fwd, cs, a - cs)
        din = jnp.where(fwd, jnp.exp(cs), jnp.exp(tot + gam))
        dst = jnp.where(fwd, jnp.exp(tot - cs), jnp.exp(cs - a))
        pieces = []
        for v in (dt, gam, din, dst):
            hi = v.astype(BF16)
            rest = v - hi.astype(F32)
            mid = rest.astype(BF16)
            pieces += [hi, mid, (rest - mid.astype(F32)).astype(BF16)]
        q_ref[...] = _dot(jnp.concatenate(pieces, axis=1), route_ref[...]).astype(BF16)
        dt_ref[...] = dt
        gamt_ref[...] = gam.T
        etot_ref[...] = jnp.broadcast_to(jnp.exp(tot), (8, h2))

    nc = s // CHUNK
    rows = lambda w: pl.BlockSpec((CHUNK, w), lambda i: (i, 0))
    whole = lambda a: pl.BlockSpec(a.shape, lambda i: (0, 0))
    return pl.pallas_call(
        kern, name="dt_prepare", grid=(nc,),
        in_specs=[rows(h2), whole(dt_bias), whole(a_neg), whole(route)],
        out_specs=[rows(h2), rows(qw), pl.BlockSpec((h2, CHUNK), lambda i: (0, i)), pl.BlockSpec((8, h2), lambda i: (i, 0))],
        out_shape=[jax.ShapeDtypeStruct((s, h2), F32), jax.ShapeDtypeStruct((s, qw), BF16),
                   jax.ShapeDtypeStruct((h2, s), F32), jax.ShapeDtypeStruct((nc * 8, h2), F32)],
        compiler_params=_params(("parallel",)),
    )(dt_raw, dt_bias, a_neg, route)


def _from_group_lanes(arr, j_heads):
    s = arr.shape[0]
    return arr.reshape(s, N_GROUPS, LANES)[:, :, :j_heads].reshape(s, N_GROUPS * j_heads)


def _dt_backward(da, ddt, dt, dt_raw, dt_bias, a_neg):
    h2 = da.shape[1]

    def body(dav, ddtv, dtv, raw, bias, a_head):
        draw = (ddtv + dav * a_head) * _sigmoid(raw + bias)
        return draw, _sum0(draw), _sum0(dav * dtv) * a_head
    return _rowwise("dt_backward", body, [da, ddt, dt, dt_raw], [dt_bias, a_neg], [(h2, BF16)], [h2, h2])


def _rms(x):
    r = lax.rsqrt(_mean1(x * x) + EPS)
    return x * r, r


def _rms_bwd(dy, y, r):
    return r * (dy - y * _mean1(dy * y))


def _norm_mod_fwd(name, x, g, sc, sh):
    def body(xv, gv, scv, shv):
        y, _ = _rms(xv)
        return ((y * gv) * (1.0 + scv) + shv,)
    return _rowwise(name, body, [x], [g, sc, sh], [(x.shape[1], BF16)])[0]


def _norm_mod_bwd(name, x, dh, dpass, g, sc, plan=None):
    d = x.shape[1]

    def body(xv, dhv, dpv, gv, scv):
        y, r = _rms(xv)
        dn = dhv * (1.0 + scv)
        dx = _rms_bwd(dn * gv, y, r) + dpv
        return dx, _sum0(dn * y), _sum0(dhv * (y * gv)), _sum0(dhv)
    return _rowwise(name, body, [x, dh, dpass], [g, sc], [(d, F32)], [d, d, d], plan=plan)


def _gated_residual_fwd(name, x, m, gate, gp):
    def body(xv, mv, gatev, gpv):
        y, _ = _rms(mv)
        return (xv + gatev * (y * gpv),)
    return _rowwise(name, body, [x, m], [gate, gp], [(x.shape[1], F32)])[0]


def _gated_residual_bwd(name, m, dx1, gate, gp):
    d = m.shape[1]

    def body(mv, dv, gatev, gpv):
        y, r = _rms(mv)
        dn = dv * gatev
        return _rms_bwd(dn * gpv, y, r), _sum0(dv * (y * gpv)), _sum0(dn * y)
    return _rowwise(name, body, [m, dx1], [gate, gp], [(d, BF16)], [d, d])


def _final_residual_loss(x1, f, tgt, gate, gp):
    d = x1.shape[1]

    def body(xv, fv, tv, gatev, gpv):
        y, r = _rms(fv)
        n = y * gpv
        err = xv + gatev * n - tv
        dx2 = err * (1.0 / d)
        dn = dx2 * gatev
        sq = jnp.sum(_sum0(err * err), axis=1, keepdims=True)
        return dx2, _rms_bwd(dn * gpv, y, r), jnp.broadcast_to(sq, (1, LANES)), _sum0(dx2 * n), _sum0(dn * y)
    return _rowwise("final_residual_loss", body, [x1, f, tgt], [gate, gp], [(d, F32), (d, BF16)], [LANES, d, d])


def _swiglu_fwd(gu):
    f = gu.shape[1] // 2

    def body(v):
        return (_silu(v[:, :f]) * v[:, f:],)
    return _rowwise("swiglu_fwd", body, [gu], [], [(f, BF16)])[0]


def _swiglu_bwd(gu, dact):
    f = gu.shape[1] // 2

    def body(v, dv):
        gt, up = v[:, :f], v[:, f:]
        return (jnp.concatenate([dv * up * _dsilu(gt), dv * _silu(gt)], axis=1),)
    return _rowwise("swiglu_bwd", body, [gu, dact], [], [(2 * f, BF16)])[0]


def _glu_fwd(glu_in, b_glu):
    c = glu_in.shape[1] // 2

    def body(v, bv):
        t = v + bv
        return (t[:, :c] * _sigmoid(t[:, c:]),)
    return _rowwise("glu_fwd", body, [glu_in], [b_glu], [(c, F32)])[0]


def _glu_bwd(glu_in, b_glu, du0):
    c = glu_in.shape[1] // 2

    def body(v, dv, bv):
        t = v + bv
        a, s = t[:, :c], _sigmoid(t[:, c:])
        dg = jnp.concatenate([dv * s, dv * a * s * (1.0 - s)], axis=1)
        return dg, _sum0(dg)
    return _rowwise("glu_bwd", body, [glu_in, du0], [b_glu], [(2 * c, BF16)], [2 * c])


def _ln_parts(u1):
    xc = u1 - _mean1(u1)
    r = lax.rsqrt(_mean1(xc * xc) + EPS)
    return xc * r, r


def _ln_silu_fwd(u1, ln_g, ln_b):
    def body(v, gv, bv):
        yh, _ = _ln_parts(v)
        return (_silu(yh * gv + bv),)
    return _rowwise("ln_silu_fwd", body, [u1], [ln_g, ln_b], [(u1.shape[1], BF16)])[0]


def _ln_silu_bwd(u1, du, ln_g, ln_b):
    d = u1.shape[1]

    def body(v, dv, gv, bv):
        yh, r = _ln_parts(v)
        dl = dv * _dsilu(yh * gv + bv)
        dyh = dl * gv
        du1 = r * (dyh - _mean1(dyh) - yh * _mean1(dyh * yh))
        return du1, _sum0(dl * yh), _sum0(dl)
    return _rowwise("ln_silu_bwd", body, [u1, du], [ln_g, ln_b], [(d, F32)], [d, d])


def _gate_merge_fwd(y_a, y_b, gl, b_gate):
    d = y_a.shape[1]

    def body(ya, yb, glv, bv):
        s = _sigmoid(glv + bv)
        return (s[:, :d] * ya + s[:, d:] * yb,)
    return _rowwise("gate_merge_fwd", body, [y_a, y_b, gl], [b_gate], [(d, BF16)])[0]


def _gate_merge_bwd(dmixin, y_a, y_b, gl, b_gate):
    d = y_a.shape[1]

    def body(dv, ya, yb, glv, bv):
        s = _sigmoid(glv + bv)
        sa, sb = s[:, :d], s[:, d:]
        dya, dyb = dv * sa, dv * sb
        dgl = jnp.concatenate([dv * ya * sa * (1.0 - sa), dv * yb * sb * (1.0 - sb)], axis=1)
        return dya, dyb, dgl, _sum0(dgl), _sum0(dyb)
    return _rowwise("gate_merge_bwd", body, [dmixin, y_a, y_b, gl], [b_gate],
                    [(d, BF16), (d, BF16), (2 * d, BF16)], [2 * d, d])


def _group_slices(d_ssm):
    gw = d_ssm // N_GROUPS
    return [slice(g * gw, (g + 1) * gw) for g in range(N_GROUPS)]


def _gated_norm_fwd(y_f, y_b, xbc_c, z, d_skip_x, g_ssm):
    d_ssm = y_f.shape[1]

    def body(yf, yb, xs, zv, dsk, gv):
        y = yf + yb + dsk * xs
        v = y * _silu(zv)
        outs = []
        for sl in _group_slices(d_ssm):
            w, _ = _rms(v[:, sl])
            outs.append(w)
        return y, jnp.concatenate(outs, axis=1) * gv
    return _rowwise("gated_norm_fwd", body, [y_f, y_b, (xbc_c, d_ssm, 0), z], [d_skip_x, g_ssm],
                    [(d_ssm, BF16), (d_ssm, BF16)])


def _gated_norm_bwd(y, z, dyn, xbc_c, d_skip_x, g_ssm):
    d_ssm = y.shape[1]

    def body(yv, zv, dv, xs, dsk, gv):
        sz = _silu(zv)
        v = yv * sz
        dw = dv * gv
        dvs, ws = [], []
        for sl in _group_slices(d_ssm):
            w, r = _rms(v[:, sl])
            ws.append(w)
            dvs.append(_rms_bwd(dw[:, sl], w, r))
        dvv = jnp.concatenate(dvs, axis=1)
        dy = dvv * sz
        return dy, dvv * yv * _dsilu(zv), _sum0(dv * jnp.concatenate(ws, axis=1)), _sum0(dy * xs)
    return _rowwise("gated_norm_bwd", body, [y, z, dyn, (xbc_c, d_ssm, 0)], [d_skip_x, g_ssm],
                    [(d_ssm, BF16), (d_ssm, BF16)], [d_ssm, d_ssm])


def _ssd_grad_merge(dxs_f, dxs_b, dy, db_f, db_b, dc_f, dc_b, d_skip_x):
    d_ssm = dy.shape[1]
    width = d_ssm + 2 * N_GROUPS * D_STATE

    def body(xf, xb, dv, bf, bb, cf, cbv, dsk):
        return (jnp.concatenate([xf + xb + dsk * dv, bf + bb, cf + cbv], axis=1),)
    return _rowwise("ssd_grad_merge", body, [dxs_f, dxs_b, dy, db_f, db_b, dc_f, dc_b], [d_skip_x], [(width, BF16)])[0]


def _adamw(name, w, g, m, v):
    c = w.shape[1]
    c1 = 1.0 - ADAM_B1 ** ADAM_STEP
    c2 = 1.0 - ADAM_B2 ** ADAM_STEP

    def body(wv, gv, mv, vv):
        mn = ADAM_B1 * mv + (1.0 - ADAM_B1) * gv
        vn = ADAM_B2 * vv + (1.0 - ADAM_B2) * (gv * gv)
        delta = -ADAM_LR * ((mn / c1) / (jnp.sqrt(vn / c2) + ADAM_EPS) + ADAM_WD * wv)
        return delta, mn, vn
    return _rowwise(name, body, [w, g, m, v], [], [(c, F32)] * 3)


def _local_step(x, tgt, mod, wts, sm, late=None):
    s, d = x.shape
    d_ssm = 2 * d
    n_heads = d_ssm // HEAD_DIM
    d_xbc = d_ssm + 2 * N_GROUPS * D_STATE
    sec = [0, d_ssm, d_ssm + d_xbc, d_ssm + d_xbc + 2 * n_heads, d_ssm + d_xbc + 2 * n_heads + 2 * d]
    sec.append(sec[-1] + 2 * d)
    sh1, sc1, g1, sh2, sc2, g2 = [mod[:, i * d:(i + 1) * d] for i in range(N_MOD)]
    win_t = wts["w_in_t"]
    sections = [(nm, (sec[i], sec[i + 1] - sec[i])) for i, nm in enumerate(("z", "xbc", "dt", "glu", "gate"))]

    h1 = _norm_mod_fwd("pre_mix_norm", x, sm["g_pre_mix"], sc1, sh1)
    z, xbc, dt_raw, glu_in, gate_l = [
        _matmul(f"proj_{nm}", h1, win_t, tb=True, b_rows=rows, out_dtype=F32 if nm == "dt" else BF16)
        for nm, rows in sections]
    plan_c = plan_f = plan_r = None
    if late is not None:
        plan_c = _gather_plan([late[n] for n in MATRICES[1:4]])
        plan_f = _gather_plan([late["w_down"]])
        plan_r = _gather_plan([late["w_gate_up"]])
    xbc_c = _dwconv_fwd("ssm_conv_fwd", xbc, sm["w_conv_ssm"], sm["b_conv_ssm"], silu=True, out_dtype=BF16, plan=plan_c)
    if late is not None:
        xbc_c, got_c = xbc_c
    tables = _scan_tables(n_heads)
    dt, q_all, gam_t, etot = _dt_prepare(dt_raw, sm["dt_bias"], sm["a_neg"], tables["route"])
    etot = etot.reshape(s // CHUNK, 8, 2 * n_heads)
    (etx_f, etg_f), (etx_r, etg_r) = [_chunk_decay(etot, n_heads, direction) for direction in (0, 1)]
    y_f, hs_f, got_f = _scan_fwd("ssd_fwd_f", xbc_c, q_all, gam_t, etx_f, tables, direction=0, d_ssm=d_ssm, plan=plan_f)
    y_r, hs_r, got_r = _scan_fwd("ssd_fwd_r", xbc_c, q_all, gam_t, etx_r, tables, direction=1, d_ssm=d_ssm, plan=plan_r)
    if late is not None:
        wts = dict(wts, w_ssm_out=got_c[0].reshape(-1, d), w_conv_out=got_c[1].reshape(-1, d),
                   w_mix_out=got_c[2].reshape(-1, d), w_down=got_f[0].reshape(-1, d), w_gate_up=got_r[0])
    y_ssd, yn = _gated_norm_fwd(y_f, y_r, xbc_c, z, sm["d_skip_x"], sm["g_ssm_norm"])
    y_a = _matmul("ssm_out", yn, wts["w_ssm_out"])
    u0 = _glu_fwd(glu_in, sm["b_glu"])
    u1 = _dwconv_fwd("dw_conv_fwd", u0, sm["w_dw"], sm["b_dw"], silu=False)
    u = _ln_silu_fwd(u1, sm["ln_g"], sm["ln_b"])
    y_b = _matmul("conv_out", u, wts["w_conv_out"], bias=sm["b_conv_out"])
    mixin = _gate_merge_fwd(y_a, y_b, gate_l, sm["b_gate"])
    mix = _matmul("mix_out", mixin, wts["w_mix_out"])
    x1 = _gated_residual_fwd("post_mix_residual", x, mix, g1, sm["g_post_mix"])
    h2 = _norm_mod_fwd("pre_ffn_norm", x1, sm["g_pre_ffn"], sc2, sh2)
    gu = _matmul("ffn_gate_up", h2, wts["w_gate_up"], b_blocks=N_CHIPS, out_dtype=BF16)
    act = _swiglu_fwd(gu)
    f = _matmul("ffn_down", act, wts["w_down"])

    dx2, df, sq, d_g2, d_gpf = _final_residual_loss(x1, f, tgt, g2, sm["g_post_ffn"])
    dact = _matmul("d_act", df, wts["w_down"], tb=True, out_dtype=BF16)
    g_w_down = _matmul("g_w_down", act, df, ta=True, out_dtype=BF16)
    dgu = _swiglu_bwd(gu, dact)
    dh2 = _matmul("d_h2", dgu, wts["w_gate_up"], tb=True, b_blocks=N_CHIPS)
    g_w_gate_up = _matmul("g_w_gate_up", h2, dgu, ta=True, out_dtype=BF16, out_blocks=N_CHIPS)
    dx1, d_gpre_ffn, d_sc2, d_sh2 = _norm_mod_bwd("pre_ffn_norm_bwd", x1, dh2, dx2, sm["g_pre_ffn"], sc2)
    dmix, d_g1, d_gpm = _gated_residual_bwd("post_mix_residual_bwd", mix, dx1, g1, sm["g_post_mix"])
    dmixin = _matmul("d_mixin", dmix, wts["w_mix_out"], tb=True, out_dtype=BF16)
    g_w_mix = _matmul("g_w_mix_out", mixin, dmix, ta=True, out_dtype=BF16)
    dy_a, dy_b, dgate_l, d_bgate, d_bco = _gate_merge_bwd(dmixin, y_a, y_b, gate_l, sm["b_gate"])
    du = _matmul("d_u", dy_b, wts["w_conv_out"], tb=True, out_dtype=BF16)
    g_w_co = _matmul("g_w_conv_out", u, dy_b, ta=True, out_dtype=BF16)
    du1, d_lng, d_lnb = _ln_silu_bwd(u1, du, sm["ln_g"], sm["ln_b"])
    du0, d_wdw, d_bdw = _dwconv_bwd("dw_conv_bwd", u0, sm["w_dw"], sm["b_dw"], du1, silu=False)
    dglu, d_bglu = _glu_bwd(glu_in, sm["b_glu"], du0)
    dyn = _matmul("d_yn", dy_a, wts["w_ssm_out"], tb=True, out_dtype=BF16)
    g_w_ssm = _matmul("g_w_ssm_out", yn, dy_a, ta=True, out_dtype=BF16)
    dy_ssd, dz, d_gssm, d_dskip_x = _gated_norm_bwd(y_ssd, z, dyn, xbc_c, sm["d_skip_x"], sm["g_ssm_norm"])
    early = [g_w_ssm.reshape(N_CHIPS, -1, d), g_w_co.reshape(N_CHIPS, -1, d), g_w_mix.reshape(N_CHIPS, -1, d),
             g_w_gate_up, g_w_down.reshape(N_CHIPS, -1, d)]
    plan_b = sums = None
    if late is not None:
        sums = _chip_sums("early", early)
        plan_b = _send_chips_plan(sums)
    (dxs_f, db_f, dc_f, ddt_f, da_f), received = _scan_bwd(
        "ssd_bwd_f", xbc_c, dy_ssd, hs_f, q_all, gam_t, etx_f, etg_f, tables, direction=0, d_ssm=d_ssm, plan=plan_b)
    (dxs_r, db_r, dc_r, ddt_r, da_r), _ = _scan_bwd(
        "ssd_bwd_r", xbc_c, dy_ssd, hs_r, q_all, gam_t, etx_r, etg_r, tables, direction=1, d_ssm=d_ssm)
    j_heads = n_heads // N_GROUPS
    da = jnp.concatenate([_from_group_lanes(da_f, j_heads), _from_group_lanes(da_r, j_heads)], axis=1)
    ddt = jnp.concatenate([_from_group_lanes(ddt_f, j_heads), _from_group_lanes(ddt_r, j_heads)], axis=1)
    ddt_raw, d_dtbias, d_alog = _dt_backward(da, ddt, dt, dt_raw, sm["dt_bias"], sm["a_neg"])
    dxbc_c = _ssd_grad_merge(dxs_f, dxs_r, dy_ssd, db_f, db_r, dc_f, dc_r, sm["d_skip_x"])
    dxbc, d_wconv, d_bconv = _dwconv_bwd("ssm_conv_bwd", xbc, sm["w_conv_ssm"], sm["b_conv_ssm"], dxbc_c,
                                         silu=True, dx_dtype=BF16)
    dsecs = [dz, dxbc, ddt_raw, dglu, dgate_l]
    dh1 = g_win = None
    for (nm, rows), dsec in zip(sections, dsecs):
        g_win = _matmul(f"g_w_in_{nm}", dsec, h1, ta=True, out_dtype=BF16, out_rows=(rows[0], sec[-1], g_win))
    hosts, sums_in, received_in = {}, None, [None] * 3
    if late is not None:
        sums_in = _chip_sums("w_in", [g_win.reshape(N_CHIPS, -1, d)])
        hosts = {"xbc": 0, "z": 1, "glu": 2}
    for (nm, rows), dsec in zip(sections, dsecs):
        plan_in = _send_chips_plan(sums_in, (hosts[nm],)) if nm in hosts else None
        dh1 = _matmul(f"d_h1_{nm}", dsec, win_t, b_rows=rows, add=dh1, plan=plan_in)
        if plan_in is not None:
            dh1, (received_in[hosts[nm]],) = dh1
    grad_x, d_gpre_mix, d_sc1, d_sh1 = _norm_mod_bwd("pre_mix_norm_bwd", x, dh1, dx1, sm["g_pre_mix"], sc1)

    dmod = jnp.concatenate([d_sh1, d_sc1, d_g1, d_sh2, d_sc2, d_g2], axis=1)
    if late is None:
        big = {"w_in_t": g_win, "w_ssm_out": g_w_ssm, "w_conv_out": g_w_co, "w_mix_out": g_w_mix,
               "w_gate_up": g_w_gate_up, "w_down": g_w_down}
    else:
        big = {"pending": (sums_in + sums, [received_in] + [[t, t, t] for t in received])}
    small = {"g_pre_mix": d_gpre_mix, "g_post_mix": d_gpm, "w_conv_ssm": d_wconv, "b_conv_ssm": d_bconv,
             "dt_bias": d_dtbias, "a_log": d_alog, "d_skip_x": d_dskip_x, "g_ssm_norm": d_gssm, "b_glu": d_bglu,
             "w_dw": d_wdw, "b_dw": d_bdw, "ln_g": d_lng, "ln_b": d_lnb, "b_conv_out": d_bco, "b_gate": d_bgate,
             "g_pre_ffn": d_gpre_ffn, "g_post_ffn": d_gpf}
    return sq, grad_x, big, small, dmod


ANY = pl.BlockSpec(memory_space=pl.ANY)
WHOLE_VMEM = pl.BlockSpec(memory_space=pltpu.VMEM)


def _mesh_place():
    x, y, c = lax.axis_index("x"), lax.axis_index("y"), lax.axis_index("c")
    other_chips = [(1 - x, y), (x, 1 - y), (1 - x, 1 - y)]
    return x, y, c, other_chips


def _remote(src, dst, send_sems, recv_sems, k, device):
    return pltpu.make_async_remote_copy(src_ref=src, dst_ref=dst, send_sem=send_sems.at[k], recv_sem=recv_sems.at[k],
                                        device_id=device, device_id_type=MESH)


def _gather_devices(name, block):
    m_per, n = block.shape

    def body(x_ref, out_ref, send_sems, recv_sems, local_sem):
        x, y, c, chips = _mesh_place()
        me, sibling = (x, y, c), (x, y, 1 - c)

        def rows(px, py, pc):
            return out_ref.at[pl.ds((4 * px + 2 * py + pc) * m_per, m_per), :]

        def copy(k, blk, to, src=None):
            return _remote(rows(*blk) if src is None else src, rows(*blk), send_sems, recv_sems, k, to)

        mine = pltpu.make_async_copy(x_ref, rows(*me), local_sem)
        mine.start()
        first = [copy(0, me, sibling, src=x_ref)]
        first += [copy(1 + j, me, (*chip, c), src=x_ref) for j, chip in enumerate(chips)]
        for cp in first:
            cp.start()
        passed = [copy(4 + j, (*chip, c), sibling) for j, chip in enumerate(chips)]
        for j, chip in enumerate(chips):
            copy(1 + j, (*chip, c), me).wait_recv()
            passed[j].start()
        copy(0, sibling, me).wait_recv()
        for j, chip in enumerate(chips):
            copy(4 + j, (*chip, 1 - c), me).wait_recv()
        for cp in first + passed:
            cp.wait_send()
        mine.wait()

    return pl.pallas_call(
        body, name=name, out_shape=jax.ShapeDtypeStruct((N_DEV * m_per, n), block.dtype),
        in_specs=[WHOLE_VMEM], out_specs=WHOLE_VMEM,
        scratch_shapes=[pltpu.SemaphoreType.DMA((7,)), pltpu.SemaphoreType.DMA((7,)), pltpu.SemaphoreType.DMA],
        compiler_params=pltpu.CompilerParams(vmem_limit_bytes=VMEM_LIMIT),
    )(block)


class _Plan:
    def __init__(self, operands, out_shapes, copies, phases):
        self.operands, self.out_shapes, self.copies, self.phases = list(operands), list(out_shapes), copies, phases

    def sems(self):
        return [pltpu.SemaphoreType.DMA((self.copies,)), pltpu.SemaphoreType.DMA((self.copies,))]


def _run_plan(name, plan):
    n_in, n_out = len(plan.operands), len(plan.out_shapes)

    def body(*refs):
        ins, outs = refs[:n_in], refs[n_in:n_in + n_out]
        send_sems, recv_sems = refs[n_in + n_out:]
        for phase in ("start", "middle", "end"):
            if phase in plan.phases:
                plan.phases[phase](ins, outs, send_sems, recv_sems)

    return list(pl.pallas_call(body, name=name, out_shape=plan.out_shapes, in_specs=[ANY] * n_in,
                               out_specs=[ANY] * n_out, scratch_shapes=plan.sems())(*plan.operands))


def _gather_plan(shards):
    n = len(shards)

    def copies(kinds, ins, outs, send_sems, recv_sems):
        x, y, c, chips = _mesh_place()
        me = 2 * x + y
        sibling = (x, y, 1 - c)

        def half(i, h):
            hr = ins[i].shape[0] // 2
            return pl.ds(h * hr, hr)

        def block(i, j, h):
            cx, cy = chips[j]
            return outs[i].at[2 * cx + cy, half(i, h)]

        make = {
            "over_ici": lambda i, j: _remote(ins[i].at[half(i, c)], outs[i].at[me, half(i, c)], send_sems, recv_sems,
                                             6 * i + j, (*chips[j], c)),
            "arrived": lambda i, j: _remote(block(i, j, c), block(i, j, c), send_sems, recv_sems, 6 * i + j, (*chips[j], c)),
            "passed_on": lambda i, j: _remote(block(i, j, c), block(i, j, c), send_sems, recv_sems, 6 * i + 3 + j, sibling),
            "from_sibling": lambda i, j: _remote(block(i, j, 1 - c), block(i, j, 1 - c), send_sems, recv_sems,
                                                 6 * i + 3 + j, sibling),
        }
        res = []
        for kind in kinds:
            if kind == "own":
                res.append([_remote(ins[i], outs[i].at[me], send_sems, recv_sems, 6 * n + i, sibling) for i in range(n)])
            else:
                res.append([make[kind](i, j) for i in range(n) for j in range(3)])
        return res

    def start(*refs):
        over_ici, own = copies(("over_ici", "own"), *refs)
        for cp in over_ici + own:
            cp.start()

    def end(*refs):
        arrived, passed_on = copies(("arrived", "passed_on"), *refs)
        for got, fwd in zip(arrived, passed_on):
            got.wait_recv()
            fwd.start()
        from_sibling, own_in = copies(("from_sibling", "own"), *refs)
        for cp in from_sibling + own_in:
            cp.wait_recv()
        over_ici, passed_on, own_out = copies(("over_ici", "passed_on", "own"), *refs)
        for cp in over_ici + passed_on + own_out:
            cp.wait_send()

    return _Plan(shards, [jax.ShapeDtypeStruct((N_CHIPS,) + s.shape, s.dtype) for s in shards], 7 * n,
                 {"start": start, "end": end})


def _send_sibling_halves(name, grads):
    n = len(grads)

    def body(*refs):
        ins, outs = refs[:n], refs[n:2 * n]
        send_sems, recv_sems = refs[2 * n:]
        x, y, c, _ = _mesh_place()
        sibling = (x, y, 1 - c)
        copies = []
        for i in range(n):
            for j in range(N_CHIPS):
                copies.append(_remote(ins[i].at[j, 1 - c], outs[i].at[j], send_sems, recv_sems, N_CHIPS * i + j, sibling))
                copies[-1].start()
        for cp in copies:
            cp.wait_recv()
        for cp in copies:
            cp.wait_send()

    return pl.pallas_call(
        body, name=name,
        out_shape=[jax.ShapeDtypeStruct((g.shape[0],) + g.shape[2:], g.dtype) for g in grads],
        in_specs=[ANY] * n, out_specs=[ANY] * n,
        scratch_shapes=[pltpu.SemaphoreType.DMA((N_CHIPS * n,)), pltpu.SemaphoreType.DMA((N_CHIPS * n,))],
    )(*grads)


def _send_chips_plan(sums, neighbours=(0, 1, 2)):
    n = len(sums)

    def copies(ins, outs, send_sems, recv_sems):
        x, y, c, chips = _mesh_place()
        return [_remote(ins[i].at[2 * chips[j][0] + chips[j][1]], outs[i].at[j], send_sems, recv_sems, 3 * i + j,
                        (*chips[j], c))
                for i in range(n) for j in neighbours]

    def start(*refs):
        for cp in copies(*refs):
            cp.start()

    def end(*refs):
        for cp in copies(*refs):
            cp.wait_recv()
        for cp in copies(*refs):
            cp.wait_send()

    return _Plan(sums, [jax.ShapeDtypeStruct((3,) + g.shape[1:], g.dtype) for g in sums], 3 * n,
                 {"start": start, "end": end})


def _exchange_halves(name, shards):
    n = len(shards)

    def body(*refs):
        outs = refs[n:2 * n]
        send_sems, recv_sems = refs[2 * n:]
        x, y, c, _ = _mesh_place()
        sibling = (x, y, 1 - c)
        remote = [_remote(outs[i].at[c], outs[i].at[c], send_sems, recv_sems, i, sibling) for i in range(n)]
        for cp in remote:
            cp.start()
        for i in range(n):
            _remote(outs[i].at[1 - c], outs[i].at[1 - c], send_sems, recv_sems, i, sibling).wait_recv()
        for cp in remote:
            cp.wait_send()

    return pl.pallas_call(
        body, name=name,
        out_shape=[jax.ShapeDtypeStruct(h.shape, h.dtype) for h in shards],
        in_specs=[ANY] * n, out_specs=[ANY] * n, input_output_aliases={i: i for i in range(n)},
        scratch_shapes=[pltpu.SemaphoreType.DMA((n,)), pltpu.SemaphoreType.DMA((n,))],
    )(*shards)


def _divisor_tile(rows, row_bytes, quantum=16):
    best = rows
    for t in range(quantum, rows + 1, quantum):
        if rows % t == 0 and 2 * t * row_bytes <= ROW_TILE_BUDGET:
            best = t
    return best


def _add_sibling(name, g4, t1):
    nb, _, hr, cols = g4.shape
    t = _divisor_tile(hr, cols * 6)

    def kern(g_ref, t_ref, o_ref):
        o_ref[0] = (g_ref[0, 0].astype(F32) + t_ref[0].astype(F32)).astype(o_ref.dtype)

    return pl.pallas_call(
        kern, name=name, grid=(nb, hr // t),
        in_specs=[pl.BlockSpec((1, 1, t, cols), lambda j, i: (j, lax.axis_index("c"), i, 0)),
                  pl.BlockSpec((1, t, cols), lambda j, i: (j, i, 0))],
        out_specs=pl.BlockSpec((1, t, cols), lambda j, i: (j, i, 0)),
        out_shape=jax.ShapeDtypeStruct((nb, hr, cols), g4.dtype),
        compiler_params=_params(("parallel", "parallel")),
    )(g4, t1)


def _add_chips(name, s1, t3):
    _, hr, cols = s1.shape
    t = _divisor_tile(hr, cols * 12)

    def kern(s_ref, t0_ref, t1_ref, t2_ref, o_ref):
        acc = s_ref[0].astype(F32)
        for t_ref in (t0_ref, t1_ref, t2_ref):
            acc = acc + t_ref[0].astype(F32)
        o_ref[0] = acc

    slot = lambda j: pl.BlockSpec((1, t, cols), functools.partial(lambda i, j: (j, i, 0), j=j))
    return pl.pallas_call(
        kern, name=name, grid=(hr // t,),
        in_specs=[pl.BlockSpec((1, t, cols), lambda i: (2 * lax.axis_index("x") + lax.axis_index("y"), i, 0)),
                  slot(0), slot(1), slot(2)],
        out_specs=pl.BlockSpec((1, t, cols), lambda i: (lax.axis_index("c"), i, 0)),
        out_shape=jax.ShapeDtypeStruct((2, hr, cols), F32),
        compiler_params=_params(("parallel",)),
    )(s1, *t3)


def _chip_sums(tag, grads):
    g4 = [g.reshape(N_CHIPS, 2, g.shape[1] // 2, g.shape[2]) for g in grads]
    t1 = _send_sibling_halves("grads_to_sibling_" + tag, g4)
    return [_add_sibling(f"chip_sum_{tag}_{i}", g, t) for i, (g, t) in enumerate(zip(g4, t1))]


def _shard_sums(sums, received):
    halves = [_add_chips(f"shard_sum_{i}", s, t) for i, (s, t) in enumerate(zip(sums, received))]
    full = _exchange_halves("grad_halves_to_sibling", halves)
    return [f.reshape(f.shape[1] * 2, f.shape[2]) for f in full]


def _pack_rows(size, width):
    return -(-size // (8 * width)) * 8


def _pack(arrays, width):
    parts = []
    for a in arrays:
        flat = a.reshape(-1).astype(F32)
        rows = _pack_rows(flat.shape[0], width)
        parts.append(jnp.pad(flat, (0, rows * width - flat.shape[0])).reshape(rows, width))
    return jnp.concatenate(parts, axis=0)


def _unpack(block, shapes, width):
    out, r = [], 0
    for shp in shapes:
        size = 1
        for s_ in shp:
            size *= s_
        rows = _pack_rows(size, width)
        out.append(block[r:r + rows].reshape(-1)[:size].reshape(shp))
        r += rows
    return out


SMALL_PARAMS = ("b_ada", "g_pre_mix", "g_post_mix", "b_conv_ssm", "dt_bias_fwd", "dt_bias_bwd", "a_log_fwd", "a_log_bwd",
                "d_skip", "g_ssm_norm", "b_glu", "b_dw", "ln_g", "ln_b", "b_conv_out", "b_gate", "g_pre_ffn", "g_post_ffn")
SHARDED_SMALL = ("w_conv_ssm", "w_dw")
MATRICES = ("w_in", "w_ssm_out", "w_conv_out", "w_mix_out", "w_gate_up", "w_down")
ALL_PARAMS = ("w_ada", "b_ada", "g_pre_mix", "g_post_mix", "w_in", "w_conv_ssm", "b_conv_ssm", "dt_bias_fwd", "dt_bias_bwd",
              "a_log_fwd", "a_log_bwd", "d_skip", "g_ssm_norm", "w_ssm_out", "b_glu", "w_dw", "b_dw", "ln_g", "ln_b",
              "w_conv_out", "b_conv_out", "b_gate", "w_mix_out", "g_pre_ffn", "g_post_ffn", "w_gate_up", "w_down")
COND_ROWS = 48
COND_CONV_ROW = 8
COND_DW_ROW = 16
MOD_ROWS = 16


def kernel(x, c, w_ada, b_ada, g_pre_mix, g_post_mix, w_in, w_conv_ssm, b_conv_ssm, dt_bias_fwd, dt_bias_bwd, a_log_fwd, a_log_bwd, d_skip, g_ssm_norm, w_ssm_out, b_glu, w_dw, b_dw, ln_g, ln_b, w_conv_out, b_conv_out, b_gate, w_mix_out, g_pre_ffn, g_post_ffn, w_gate_up, w_down, loss_target, m_w_ada, m_b_ada, m_g_pre_mix, m_g_post_mix, m_w_in, m_w_conv_ssm, m_b_conv_ssm, m_dt_bias_fwd, m_dt_bias_bwd, m_a_log_fwd, m_a_log_bwd, m_d_skip, m_g_ssm_norm, m_w_ssm_out, m_b_glu, m_w_dw, m_b_dw, m_ln_g, m_ln_b, m_w_conv_out, m_b_conv_out, m_b_gate, m_w_mix_out, m_g_pre_ffn, m_g_post_ffn, m_w_gate_up, m_w_down, v_w_ada, v_b_ada, v_g_pre_mix, v_g_post_mix, v_w_in, v_w_conv_ssm, v_b_conv_ssm, v_dt_bias_fwd, v_dt_bias_bwd, v_a_log_fwd, v_a_log_bwd, v_d_skip, v_g_ssm_norm, v_w_ssm_out, v_b_glu, v_w_dw, v_b_dw, v_ln_g, v_ln_b, v_w_conv_out, v_b_conv_out, v_b_gate, v_w_mix_out, v_g_pre_ffn, v_g_post_ffn, v_w_gate_up, v_w_down):
    given = dict(locals())
    wgt = {n: given[n][0] for n in ALL_PARAMS}
    mom = {n: given["m_" + n][0] for n in ALL_PARAMS}
    var = {n: given["v_" + n][0] for n in ALL_PARAMS}
    xs, tgt = x[0], loss_target[0]
    s, d = xs.shape
    d_ssm = 2 * d
    n_heads = d_ssm // HEAD_DIM
    d_xbc = d_ssm + 2 * N_GROUPS * D_STATE
    xi, yi, ci = lax.axis_index("x"), lax.axis_index("y"), lax.axis_index("c")
    chip = 2 * xi + yi
    dev = 2 * chip + ci
    k_conv, k_dw = wgt["w_conv_ssm"].shape[0], wgt["w_dw"].shape[0]
    xbc_shard, dw_shard = d_xbc // N_CHIPS, d // N_CHIPS

    width1 = max(d, xbc_shard)
    blk = jnp.zeros((COND_ROWS, width1), F32)
    blk = blk.at[0, :d].set(c[0])
    blk = blk.at[COND_CONV_ROW:COND_CONV_ROW + k_conv, :xbc_shard].set(wgt["w_conv_ssm"])
    blk = blk.at[COND_DW_ROW:COND_DW_ROW + k_dw, :dw_shard].set(wgt["w_dw"])
    g1 = _gather_devices("gather_cond", blk).reshape(N_DEV, COND_ROWS, width1)
    c_all = g1[:, 0, :d]
    w_conv_full = jnp.concatenate([g1[2 * k, COND_CONV_ROW:COND_CONV_ROW + k_conv, :xbc_shard] for k in range(N_CHIPS)], axis=1)
    w_dw_full = jnp.concatenate([g1[2 * k, COND_DW_ROW:COND_DW_ROW + k_dw, :dw_shard] for k in range(N_CHIPS)], axis=1)
    c_act = jnp.pad(c_all * _sigmoid(c_all), ((0, MOD_ROWS - N_DEV), (0, 0)))

    mod_part = _matmul("ada_mod", c_act, wgt["w_ada"])
    g2 = _gather_devices("gather_mod", mod_part).reshape(N_DEV, MOD_ROWS, mod_part.shape[1])
    mod_all = jnp.concatenate([g2[2 * k, :N_DEV] for k in range(N_CHIPS)], axis=1) + wgt["b_ada"][None]
    mod = lax.dynamic_slice_in_dim(mod_all, dev, 1, axis=0)

    shards = [wgt["w_in"].T.astype(BF16)] + [wgt[n].astype(BF16) for n in MATRICES[1:]]
    wts = {"w_in_t": _run_plan("gather_w_in", _gather_plan(shards[:1]))[0].reshape(-1, d)}
    late = dict(zip(MATRICES[1:], shards[1:]))
    row = lambda v: v.reshape(1, -1)
    sm = {"g_pre_mix": row(wgt["g_pre_mix"]), "g_post_mix": row(wgt["g_post_mix"]), "w_conv_ssm": w_conv_full,
          "b_conv_ssm": row(wgt["b_conv_ssm"]),
          "dt_bias": row(jnp.concatenate([wgt["dt_bias_fwd"], wgt["dt_bias_bwd"]])),
          "a_neg": row(-jnp.exp(jnp.concatenate([wgt["a_log_fwd"], wgt["a_log_bwd"]]))),
          "d_skip_x": row(jnp.repeat(wgt["d_skip"], HEAD_DIM)), "g_ssm_norm": row(wgt["g_ssm_norm"]),
          "b_glu": row(wgt["b_glu"]), "w_dw": w_dw_full, "b_dw": row(wgt["b_dw"]), "ln_g": row(wgt["ln_g"]),
          "ln_b": row(wgt["ln_b"]), "b_conv_out": row(wgt["b_conv_out"]), "b_gate": row(wgt["b_gate"]),
          "g_pre_ffn": row(wgt["g_pre_ffn"]), "g_post_ffn": row(wgt["g_post_ffn"])}

    sq, grad_x, big, small, dmod = _local_step(xs, tgt, mod, wts, sm, late=late)
    loss = lax.psum((0.5 / d) * sq[0, 0], ("x", "y", "c"))

    local_small = {"b_ada": dmod, "g_pre_mix": small["g_pre_mix"], "g_post_mix": small["g_post_mix"],
                   "b_conv_ssm": small["b_conv_ssm"], "dt_bias_fwd": small["dt_bias"][:, :n_heads],
                   "dt_bias_bwd": small["dt_bias"][:, n_heads:], "a_log_fwd": small["a_log"][:, :n_heads],
                   "a_log_bwd": small["a_log"][:, n_heads:],
                   "d_skip": jnp.sum(small["d_skip_x"].reshape(n_heads, HEAD_DIM), axis=1),
                   "g_ssm_norm": small["g_ssm_norm"], "b_glu": small["b_glu"], "b_dw": small["b_dw"],
                   "ln_g": small["ln_g"], "ln_b": small["ln_b"], "b_conv_out": small["b_conv_out"],
                   "b_gate": small["b_gate"], "g_pre_ffn": small["g_pre_ffn"], "g_post_ffn": small["g_post_ffn"],
                   "w_conv_ssm": small["w_conv_ssm"], "w_dw": small["w_dw"]}
    names = SMALL_PARAMS + SHARDED_SMALL
    pack = _pack([local_small[n] for n in names], d)
    rows_p = pack.shape[0]
    g3 = _gather_devices("gather_small_grads", pack).reshape(N_DEV, rows_p, d)
    total = _rowwise("sum_small_grads", lambda *blocks: (functools.reduce(lambda a, b: a + b, blocks),),
                     [g3[i] for i in range(N_DEV)], [], [(d, F32)])[0]
    full_shapes = [wgt[n].shape for n in SMALL_PARAMS] + [(k_conv, d_xbc), (k_dw, d)]
    summed = dict(zip(names, _unpack(total, full_shapes, d)))
    grads = {n: summed[n] for n in SMALL_PARAMS}
    grads["w_conv_ssm"] = lax.dynamic_slice_in_dim(summed["w_conv_ssm"], chip * xbc_shard, xbc_shard, axis=1)
    grads["w_dw"] = lax.dynamic_slice_in_dim(summed["w_dw"], chip * dw_shard, dw_shard, axis=1)

    dmod_all = g3[:, :N_MOD, :].reshape(N_DEV, N_MOD * d)
    ada_cols = wgt["w_ada"].shape[1]
    dmod_cols = jnp.pad(lax.dynamic_slice_in_dim(dmod_all, chip * ada_cols, ada_cols, axis=1),
                        ((0, MOD_ROWS - N_DEV), (0, 0)))
    grads["w_ada"] = _matmul("g_w_ada", c_act, dmod_cols, ta=True)

    reduced = _shard_sums(*big["pending"])
    grads["w_in"] = reduced[0].T
    for n, g in zip(MATRICES[1:], reduced[1:]):
        grads[n] = g

    delta, new_m, new_v = {}, {}, {}
    for n in ("w_ada",) + MATRICES:
        delta[n], new_m[n], new_v[n] = _adamw("adamw_" + n, wgt[n], grads[n], mom[n], var[n])
    for group, width, tag in ((SMALL_PARAMS, d, "small"), (SHARDED_SMALL, LANES, "conv")):
        shapes = [wgt[n].shape for n in group]
        packs = [_pack([src[n] for n in group], width) for src in (wgt, grads, mom, var)]
        outs = _adamw("adamw_" + tag, *packs)
        for res, o in zip((delta, new_m, new_v), outs):
            res.update(zip(group, _unpack(o, shapes, width)))

    lead = lambda a: a[None]
    return (loss, grad_x[None], *[lead(grads[n]) for n in ALL_PARAMS], *[lead(delta[n]) for n in ALL_PARAMS],
            *[lead(new_m[n]) for n in ALL_PARAMS], *[lead(new_v[n]) for n in ALL_PARAMS])
```

```python
import functools

import jax
import jax.numpy as jnp
from jax import lax
from jax.experimental import pallas as pl
from jax.experimental.pallas import tpu as pltpu

F32 = jnp.float32
BF16 = jnp.bfloat16

N_GROUPS = 8
HEAD_DIM = 64
D_STATE = 128
CHUNK = 128
EPS = 1e-6
N_MOD = 6
ADAM_LR = 0.001
ADAM_B1 = 0.9
ADAM_B2 = 0.999
ADAM_EPS = 1e-08
ADAM_WD = 0.01
ADAM_STEP = 10

V7X_VMEM_BYTES = 64 * 1024 * 1024
VMEM_LIMIT = V7X_VMEM_BYTES - 8 * 1024 * 1024
ROW_TILE_BUDGET = 20 * 1024 * 1024
LANES = 128
NEG = -1e30
MESH = pl.DeviceIdType.MESH
N_CHIPS = 4
N_DEV = 8


def _params(sem):
    return pltpu.CompilerParams(dimension_semantics=sem, vmem_limit_bytes=VMEM_LIMIT)


def _sigmoid(x):
    return 1.0 / (1.0 + jnp.exp(-x))


def _silu(x):
    return x * _sigmoid(x)


def _dsilu(x):
    s = _sigmoid(x)
    return s * (1.0 + x * (1.0 - s))


def _softplus(x):
    return jnp.maximum(x, 0.0) + jnp.log(1.0 + jnp.exp(-jnp.abs(x)))


def _sum0(a):
    return jnp.sum(a, axis=0, keepdims=True)


def _mean1(a):
    return jnp.mean(a, axis=1, keepdims=True)


def _rowwise(name, body, rows, bcasts, out_rows, out_accs=(), tile=None, plan=None):
    rows = [r if isinstance(r, tuple) else (r, r.shape[1], 0) for r in rows]
    s = rows[0][0].shape[0]
    if tile is None:
        per_row = sum(w * a.dtype.itemsize for a, w, _ in rows) + sum(w * jnp.dtype(dt).itemsize for w, dt in out_rows)
        tile = 1024
        while tile > 16 and (tile * per_row * 2 > ROW_TILE_BUDGET or s % tile):
            tile //= 2
        if s * per_row * 2 <= ROW_TILE_BUDGET:
            tile = s
    assert s % tile == 0
    n_in = len(rows) + len(bcasts)
    n_o = len(out_rows)
    n_out = n_o + len(out_accs)
    p_ops, p_in_specs, p_out_shapes, p_out_specs, p_scratch, _ = _plan_call_parts(plan)
    n_pi, n_po = len(p_ops), len(p_out_shapes)
    n_steps = (s // tile,)

    def kern(*refs):
        plan_ins = refs[n_in:n_in + n_pi]
        outs = refs[n_in + n_pi:n_in + n_pi + n_out]
        plan_outs = refs[n_in + n_pi + n_out:n_in + n_pi + n_out + n_po]
        plan_sems = refs[n_in + n_pi + n_out + n_po:]
        _plan_phase(plan, "start", n_steps, plan_ins, plan_outs, plan_sems)
        _plan_phase(plan, "middle", n_steps, plan_ins, plan_outs, plan_sems)
        res = body(*[r[...].astype(F32) for r in refs[:n_in]])
        for o, v in zip(outs[:n_o], res[:n_o]):
            o[...] = v.astype(o.dtype)
        if out_accs:
            @pl.when(pl.program_id(0) == 0)
            def _():
                for o in outs[n_o:]:
                    o[...] = jnp.zeros_like(o)
            for o, v in zip(outs[n_o:], res[n_o:]):
                o[...] += v
        _plan_phase(plan, "end", n_steps, plan_ins, plan_outs, plan_sems)

    in_specs = [pl.BlockSpec((tile, w), functools.partial(lambda i, cb: (i, cb), cb=cb)) for _, w, cb in rows]
    in_specs += [pl.BlockSpec(b.shape, functools.partial(lambda i, nd: (0,) * nd, nd=b.ndim)) for b in bcasts]
    out_shape = [jax.ShapeDtypeStruct((s, w), dt) for w, dt in out_rows]
    out_shape += [jax.ShapeDtypeStruct((1, w), F32) for w in out_accs]
    out_specs = [pl.BlockSpec((tile, w), lambda i: (i, 0)) for w, _ in out_rows]
    out_specs += [pl.BlockSpec((1, w), lambda i: (0, 0)) for w in out_accs]
    return pl.pallas_call(
        kern, name=name, grid=n_steps, in_specs=in_specs + p_in_specs, out_specs=out_specs + p_out_specs,
        out_shape=out_shape + p_out_shapes, scratch_shapes=p_scratch,
        compiler_params=_params(("arbitrary",) if out_accs or plan is not None else ("parallel",)),
    )(*[a for a, _, _ in rows], *bcasts, *p_ops)


def _tile(n, pref):
    if n <= pref:
        return n
    t = (pref // LANES) * LANES
    while t >= LANES:
        if n % t == 0:
            return t
        t -= LANES
    return n


def _matmul(name, a, b, *, ta=False, tb=False, out_dtype=F32, bias=None, add=None, b_blocks=1, out_blocks=1,
            b_rows=None, out_rows=None, plan=None, tm=1024, tn=1408, tk=2816):
    m, k = (a.shape[1], a.shape[0]) if ta else a.shape
    if b_blocks > 1:
        rows_b, cols_b = b.shape[1], b.shape[2] * b_blocks
    else:
        rows_b, cols_b = b.shape
    if b_rows is not None:
        rows_b = b_rows[1]
    n, kb = (rows_b, cols_b) if tb else (cols_b, rows_b)
    assert k == kb, (name, a.shape, b.shape)
    tm, tn, tk = _tile(m, tm), _tile(n, tn), _tile(k, tk)
    if b_blocks > 1:
        per = cols_b // b_blocks
        if tb:
            tk = _tile(per, tk)
        else:
            tn = _tile(per, tn)
    if out_blocks > 1:
        tn = _tile(n // out_blocks, tn)
    nk = k // tk
    grid = (m // tm, n // tn, nk)

    a_spec = pl.BlockSpec((tk, tm), lambda i, j, kk: (kk, i)) if ta else pl.BlockSpec((tm, tk), lambda i, j, kk: (i, kk))
    if b_blocks > 1:
        if tb:
            nb = per // tk
            b_spec = pl.BlockSpec((1, tn, tk), lambda i, j, kk: (kk // nb, j, kk % nb))
        else:
            nb = per // tn
            b_spec = pl.BlockSpec((1, tk, tn), lambda i, j, kk: (j // nb, kk, j % nb))
    elif b_rows is not None:
        first = b_rows[0]
        if tb:
            b_spec = pl.BlockSpec((pl.Element(tn), pl.Element(tk)),
                                  lambda i, j, kk: (pl.multiple_of(first + j * tn, LANES), kk * tk))
        else:
            b_spec = pl.BlockSpec((pl.Element(tk), pl.Element(tn)),
                                  lambda i, j, kk: (pl.multiple_of(first + kk * tk, LANES), j * tn))
    else:
        b_spec = pl.BlockSpec((tn, tk), lambda i, j, kk: (j, kk)) if tb else pl.BlockSpec((tk, tn), lambda i, j, kk: (kk, j))
    in_specs = [a_spec, b_spec]
    operands = [a, b]
    if bias is not None:
        in_specs.append(pl.BlockSpec((1, tn), lambda i, j, kk: (0, j)))
        operands.append(bias)
    if add is not None:
        in_specs.append(pl.BlockSpec((tm, tn), lambda i, j, kk: (i, j)))
        operands.append(add)
    aliases = {}
    if out_blocks > 1:
        nbo = (n // out_blocks) // tn
        out_spec = pl.BlockSpec((1, tm, tn), lambda i, j, kk: (j // nbo, i, j % nbo))
        out_shape = jax.ShapeDtypeStruct((out_blocks, m, n // out_blocks), out_dtype)
    elif out_rows is not None:
        first_out, total, previous = out_rows
        out_spec = pl.BlockSpec((pl.Element(tm), pl.Element(tn)),
                                lambda i, j, kk: (pl.multiple_of(first_out + i * tm, LANES), j * tn))
        out_shape = jax.ShapeDtypeStruct((total, n), out_dtype)
        if previous is not None:
            aliases = {len(operands): 0}
            in_specs.append(pl.BlockSpec(memory_space=pl.ANY))
            operands.append(previous)
    else:
        out_spec = pl.BlockSpec((tm, tn), lambda i, j, kk: (i, j))
        out_shape = jax.ShapeDtypeStruct((m, n), out_dtype)
    dims = (((0 if ta else 1,), (1 if tb else 0,)), ((), ()))
    has_bias, has_add, has_previous = bias is not None, add is not None, bool(aliases)
    p_ops, p_in_specs, p_out_shapes, p_out_specs, p_scratch, _ = _plan_call_parts(plan)
    n_pi, n_po = len(p_ops), len(p_out_shapes)

    def kern(*refs):
        a_ref, b_ref = refs[0], refs[1]
        pos = 2
        bias_ref = add_ref = None
        if has_bias:
            bias_ref = refs[pos]
            pos += 1
        if has_add:
            add_ref = refs[pos]
            pos += 1
        if has_previous:
            pos += 1
        plan_ins = refs[pos:pos + n_pi]
        o_ref = refs[pos + n_pi]
        plan_outs = refs[pos + n_pi + 1:pos + n_pi + 1 + n_po]
        pos += n_pi + 1 + n_po
        acc_ref = refs[pos] if nk > 1 else None
        plan_sems = refs[pos + (1 if nk > 1 else 0):]
        _plan_phase(plan, "start", grid, plan_ins, plan_outs, plan_sems)
        av = a_ref[...].astype(BF16)
        bv = (b_ref[0] if b_blocks > 1 else b_ref[...]).astype(BF16)
        p = lax.dot_general(av, bv, dims, preferred_element_type=F32)

        def finish(acc):
            if has_bias:
                acc = acc + bias_ref[...]
            if has_add:
                acc = acc + add_ref[...]
            if out_blocks > 1:
                o_ref[0] = acc.astype(o_ref.dtype)
            else:
                o_ref[...] = acc.astype(o_ref.dtype)

        if nk == 1:
            finish(p)
        else:
            kk = pl.program_id(2)

            @pl.when(kk == 0)
            def _():
                acc_ref[...] = p

            @pl.when(kk > 0)
            def _():
                acc_ref[...] += p

            @pl.when(kk == nk - 1)
            def _():
                finish(acc_ref[...])
        _plan_phase(plan, "end", grid, plan_ins, plan_outs, plan_sems)

    res = pl.pallas_call(
        kern, name=name, grid=grid, in_specs=in_specs + p_in_specs, out_specs=[out_spec] + p_out_specs,
        out_shape=[out_shape] + p_out_shapes, input_output_aliases=aliases,
        scratch_shapes=([pltpu.VMEM((tm, tn), F32)] if nk > 1 else []) + p_scratch,
        compiler_params=_params(("parallel", "parallel", "arbitrary") if plan is None else ("arbitrary",) * 3),
    )(*operands, *p_ops)
    return res[0] if plan is None else (res[0], list(res[1:]))


CONV_HALO = 16
CONV_ROWS = 256


def _taps(win, shifts, rows):
    n = win.shape[0]
    for j, s in enumerate(shifts):
        yield j, (pltpu.roll(win, (n - s) % n, axis=0) if s % n else win)[:rows]


def _dwconv_fwd(name, x, w, b, *, silu, out_dtype=F32, plan=None):
    s, c = x.shape
    k = w.shape[0]
    pad = (k - 1) // 2
    assert pad <= CONV_HALO and c % LANES == 0
    t = min(CONV_ROWS, s)
    n_chunks = s // t
    fwd_shifts = [CONV_HALO - pad + j for j in range(k)]
    p_ops, p_in_specs, p_out_shapes, p_out_specs, p_scratch, _ = _plan_call_parts(plan)
    n_pi, n_po = len(p_ops), len(p_out_shapes)
    n_steps = (c // LANES,)

    def kern(*refs):
        x_ref, w_ref, b_ref = refs[:3]
        plan_ins = refs[3:3 + n_pi]
        o_ref = refs[3 + n_pi]
        plan_outs = refs[4 + n_pi:4 + n_pi + n_po]
        xp_ref = refs[4 + n_pi + n_po]
        plan_sems = refs[5 + n_pi + n_po:]
        _plan_phase(plan, "start", n_steps, plan_ins, plan_outs, plan_sems)
        _plan_phase(plan, "middle", n_steps, plan_ins, plan_outs, plan_sems)
        zeros = jnp.zeros((CONV_HALO, LANES), F32)
        xp_ref[0:CONV_HALO, :] = zeros
        xp_ref[CONV_HALO + s:CONV_HALO + s + CONV_HALO, :] = zeros
        xp_ref[CONV_HALO:CONV_HALO + s, :] = x_ref[...].astype(F32)
        bv = b_ref[...]

        def chunk(i, carry):
            base = pl.multiple_of(i * t, 16)
            win = xp_ref[pl.ds(base, t + 2 * CONV_HALO), :]
            acc = jnp.zeros((t, LANES), F32)
            for j, xs in _taps(win, fwd_shifts, t):
                acc = acc + xs * w_ref[pl.ds(j, 1), :]
            acc = acc + bv
            o_ref[pl.ds(base, t), :] = (_silu(acc) if silu else acc).astype(o_ref.dtype)
            return carry

        lax.fori_loop(0, n_chunks, chunk, 0)
        _plan_phase(plan, "end", n_steps, plan_ins, plan_outs, plan_sems)

    res = pl.pallas_call(
        kern, name=name, grid=n_steps,
        in_specs=[pl.BlockSpec((s, LANES), lambda i: (0, i)), pl.BlockSpec((k, LANES), lambda i: (0, i)),
                  pl.BlockSpec((1, LANES), lambda i: (0, i))] + p_in_specs,
        out_specs=[pl.BlockSpec((s, LANES), lambda i: (0, i))] + p_out_specs,
        out_shape=[jax.ShapeDtypeStruct((s, c), out_dtype)] + p_out_shapes,
        scratch_shapes=[pltpu.VMEM((s + 2 * CONV_HALO, LANES), F32)] + p_scratch,
        compiler_params=_params(("parallel",) if plan is None else ("arbitrary",)),
    )(x, w, b, *p_ops)
    return res[0] if plan is None else (res[0], list(res[1:]))


def _dwconv_bwd(name, x, w, b, dout, *, silu, dx_dtype=F32):
    s, c = x.shape
    k = w.shape[0]
    pad = (k - 1) // 2
    t = min(CONV_ROWS, s)
    n_chunks = s // t
    fwd_shifts = [CONV_HALO - pad + j for j in range(k)]
    bwd_shifts = [CONV_HALO + pad - j for j in range(k)]

    def kern(x_ref, w_ref, b_ref, do_ref, dx_ref, dw_ref, db_ref, xp_ref, dp_ref):
        zeros = jnp.zeros((CONV_HALO, LANES), F32)
        for ref in (xp_ref, dp_ref):
            ref[0:CONV_HALO, :] = zeros
            ref[CONV_HALO + s:CONV_HALO + s + CONV_HALO, :] = zeros
        xp_ref[CONV_HALO:CONV_HALO + s, :] = x_ref[...].astype(F32)
        bv = b_ref[...]
        dw_ref[...] = jnp.zeros_like(dw_ref)

        def pre_chunk(i, dbias):
            base = pl.multiple_of(i * t, 16)
            win = xp_ref[pl.ds(base, t + 2 * CONV_HALO), :]
            dpre = do_ref[pl.ds(base, t), :].astype(F32)
            if silu:
                acc = jnp.zeros((t, LANES), F32)
                for j, xs in _taps(win, fwd_shifts, t):
                    acc = acc + xs * w_ref[pl.ds(j, 1), :]
                dpre = dpre * _dsilu(acc + bv)
            dp_ref[pl.ds(base + CONV_HALO, t), :] = dpre
            for j, xs in _taps(win, fwd_shifts, t):
                dw_ref[pl.ds(j, 1), :] += _sum0(dpre * xs)
            return dbias + _sum0(dpre)

        db_ref[...] = lax.fori_loop(0, n_chunks, pre_chunk, jnp.zeros((1, LANES), F32))

        def dx_chunk(i, carry):
            base = pl.multiple_of(i * t, 16)
            win = dp_ref[pl.ds(base, t + 2 * CONV_HALO), :]
            acc = jnp.zeros((t, LANES), F32)
            for j, dps in _taps(win, bwd_shifts, t):
                acc = acc + dps * w_ref[pl.ds(j, 1), :]
            dx_ref[pl.ds(base, t), :] = acc.astype(dx_ref.dtype)
            return carry

        lax.fori_loop(0, n_chunks, dx_chunk, 0)

    col = lambda rows: pl.BlockSpec((rows, LANES), lambda i: (0, i))
    return pl.pallas_call(
        kern, name=name, grid=(c // LANES,),
        in_specs=[col(s), col(k), col(1), col(s)],
        out_specs=[col(s), col(k), col(1)],
        out_shape=[jax.ShapeDtypeStruct((s, c), dx_dtype), jax.ShapeDtypeStruct((k, c), F32),
                   jax.ShapeDtypeStruct((1, c), F32)],
        scratch_shapes=[pltpu.VMEM((s + 2 * CONV_HALO, LANES), F32), pltpu.VMEM((s + 2 * CONV_HALO, LANES), F32)],
        compiler_params=_params(("parallel",)),
    )(x, w, b, dout)


_NT =(((1,), (1,)), ((), ()))
_TN = (((0,), (0,)), ((), ()))


def _dot(a, b, dims=None):
    if dims is None:
        return jnp.dot(a, b, preferred_element_type=F32)
    return lax.dot_general(a, b, dims, preferred_element_type=F32)


HEAD_QUANTITIES = 4
GROUPS_PER_STEP = 4


def _scan_tables(n_heads):
    j_heads = n_heads // N_GROUPS
    used = 3 * HEAD_QUANTITIES * j_heads
    assert used <= LANES and j_heads % 2 == 0 and N_GROUPS % GROUPS_PER_STEP == 0
    gw = j_heads * HEAD_DIM
    r = jnp.arange(LANES)[:, None]

    def expand(quantity, width):
        head_of_lane = jnp.arange(j_heads * width)[None] // width
        return ((r // (3 * j_heads) == quantity) & (r % j_heads == head_of_lane) & (r < used)).astype(BF16)

    sel_cols = (jnp.arange(gw)[:, None] // HEAD_DIM == jnp.arange(LANES)[None]).astype(BF16)
    h2 = 2 * n_heads
    rows = jnp.arange(3 * HEAD_QUANTITIES * h2)
    head = rows % n_heads
    col = ((rows % h2) // n_heads * N_GROUPS + head // j_heads) * LANES + (rows // h2) * j_heads + head % j_heads
    route = (col[:, None] == jnp.arange(2 * N_GROUPS * LANES)[None]).astype(BF16)
    return {"ex_dt": expand(0, HEAD_DIM), "ex_gam": expand(1, CHUNK), "ex_din": expand(2, HEAD_DIM),
            "ex_dst": expand(3, HEAD_DIM), "sel_cols": sel_cols, "route": route}


def _chunk_decay(etot, n_heads, direction):
    j_heads = n_heads // N_GROUPS
    ed = etot[:, :, direction * n_heads:(direction + 1) * n_heads]
    per_group = jnp.pad(ed.reshape(ed.shape[0], 8, N_GROUPS, j_heads), ((0, 0), (0, 0), (0, 0), (0, LANES - j_heads)))
    return jnp.repeat(ed, HEAD_DIM, axis=2), per_group.reshape(ed.shape[0], 8, N_GROUPS * LANES)


def _scan_specs(reverse_order, direction, nc, j_heads, d_ssm):
    gps = GROUPS_PER_STEP
    gw = j_heads * HEAD_DIM
    b_off = d_ssm // (gps * D_STATE)
    c_off = b_off + N_GROUPS // gps
    d_off = direction * (N_GROUPS // gps)
    zz = (lambda z: nc - 1 - z) if reverse_order else (lambda z: z)
    const = lambda shape: pl.BlockSpec(shape, lambda g, z: (0,) * len(shape))
    return {
        "xs": pl.BlockSpec((CHUNK, gps * gw), lambda g, z: (zz(z), g)),
        "b": pl.BlockSpec((CHUNK, gps * D_STATE), lambda g, z: (zz(z), b_off + g)),
        "c": pl.BlockSpec((CHUNK, gps * D_STATE), lambda g, z: (zz(z), c_off + g)),
        "q": pl.BlockSpec((CHUNK, gps * LANES), lambda g, z: (zz(z), d_off + g)),
        "gam_t": pl.BlockSpec((gps * j_heads, CHUNK), lambda g, z: (d_off + g, zz(z))),
        "etot_x": pl.BlockSpec((1, 8, gps * gw), lambda g, z: (zz(z), 0, g)),
        "etot_g": pl.BlockSpec((1, 8, gps * LANES), lambda g, z: (zz(z), 0, g)),
        "state": pl.BlockSpec((gps, 1, D_STATE, gw), lambda g, z: (g, zz(z), 0, 0)),
        "grp": pl.BlockSpec((CHUNK, gps * D_STATE), lambda g, z: (zz(z), g)),
        "ex": const((LANES, gw)), "ex_gam": const((LANES, j_heads * CHUNK)), "sel": const((gw, LANES)),
    }


def _scan_masks(reverse):
    li = lax.broadcasted_iota(jnp.int32, (CHUNK, CHUNK), 0)
    si = lax.broadcasted_iota(jnp.int32, (CHUNK, CHUNK), 1)
    mask = (li <= si) if reverse else (li >= si)
    mask_t = (si <= li) if reverse else (si >= li)
    return li, si, mask, mask_t, si < HEAD_DIM


def _plan_phase(plan, phase, n_steps, ins, outs, sems):
    if plan is None or phase not in plan.phases:
        return
    ids = [pl.program_id(i) for i in range(len(n_steps))]
    first = {"start": 0, "middle": n_steps[0] // 2, "end": n_steps[0] - 1}[phase]
    when = ids[0] == first
    for i, n in zip(ids[1:], n_steps[1:]):
        when = when & (i == (n - 1 if phase == "end" else 0))

    @pl.when(when)
    def _():
        plan.phases[phase](ins, outs, *sems)


def _plan_call_parts(plan):
    if plan is None:
        return [], [], [], [], [], ("parallel", "arbitrary")
    n_in, n_out = len(plan.operands), len(plan.out_shapes)
    return plan.operands, [ANY] * n_in, plan.out_shapes, [ANY] * n_out, plan.sems(), ("arbitrary", "arbitrary")


def _scan_fwd(name, xbc_c, q_all, gam_t, etot_x, tb, *, direction, d_ssm, plan=None):
    s = xbc_c.shape[0]
    nc = s // CHUNK
    j_heads = tb["ex_dt"].shape[1] // HEAD_DIM
    gw = j_heads * HEAD_DIM
    gps = GROUPS_PER_STEP
    reverse = direction == 1
    sp = _scan_specs(reverse, direction, nc, j_heads, d_ssm)
    p_ops, p_in_specs, p_out_shapes, p_out_specs, p_scratch, semantics = _plan_call_parts(plan)
    n_pi, n_po = len(p_ops), len(p_out_shapes)
    n_steps = (N_GROUPS // gps, nc)

    def kern(*refs):
        xs_ref, b_ref, c_ref, q_ref, gamt_ref, etx_ref, exdt_ref, exgam_ref, exdin_ref, exdst_ref = refs[:10]
        plan_ins = refs[10:10 + n_pi]
        y_ref, hs_ref = refs[10 + n_pi:12 + n_pi]
        plan_outs = refs[12 + n_pi:12 + n_pi + n_po]
        h_ref = refs[12 + n_pi + n_po]
        plan_sems = refs[13 + n_pi + n_po:]
        _plan_phase(plan, "start", n_steps, plan_ins, plan_outs, plan_sems)
        _plan_phase(plan, "middle", n_steps, plan_ins, plan_outs, plan_sems)

        @pl.when(pl.program_id(1) == 0)
        def _():
            h_ref[...] = jnp.zeros_like(h_ref)

        _, _, mask, _, lo = _scan_masks(reverse)
        for gi in range(gps):
            bb = b_ref[:, gi * D_STATE:(gi + 1) * D_STATE].astype(BF16)
            cb = c_ref[:, gi * D_STATE:(gi + 1) * D_STATE].astype(BF16)
            cbt = _dot(cb, bb, _NT)
            q = q_ref[:, gi * LANES:(gi + 1) * LANES]
            dtx, dinx, dstx = _dot(q, exdt_ref[...]), _dot(q, exdin_ref[...]), _dot(q, exdst_ref[...])
            gcol = _dot(q, exgam_ref[...])
            xdt = xs_ref[:, gi * gw:(gi + 1) * gw].astype(F32) * dtx
            ht = h_ref[gi]
            y_off = _dot(cb, ht.astype(BF16)) * dinx
            hs_ref[gi, 0] = ht
            for p in range(j_heads // 2):
                lanes = slice(p * CHUNK, (p + 1) * CHUNK)
                x2 = xdt[:, lanes]
                acc = y_off[:, lanes]
                for idx, j in enumerate((2 * p, 2 * p + 1)):
                    g_row = gamt_ref[pl.ds(gi * j_heads + j, 1), :]
                    decay = jnp.exp(jnp.where(mask, gcol[:, j * CHUNK:(j + 1) * CHUNK] - g_row, NEG))
                    x_head = jnp.where(lo if idx == 0 else jnp.logical_not(lo), x2, 0.0).astype(BF16)
                    acc = acc + _dot((cbt * decay).astype(BF16), x_head)
                y_ref[:, gi * gw + p * CHUNK:gi * gw + (p + 1) * CHUNK] = acc.astype(y_ref.dtype)
            h_ref[gi] = ht * etx_ref[0, 0:1, gi * gw:(gi + 1) * gw] + _dot(bb, (xdt * dstx).astype(BF16), _TN)
        _plan_phase(plan, "end", n_steps, plan_ins, plan_outs, plan_sems)

    res = pl.pallas_call(
        kern, name=name, grid=n_steps,
        in_specs=[sp["xs"], sp["b"], sp["c"], sp["q"], sp["gam_t"], sp["etot_x"], sp["ex"], sp["ex_gam"], sp["ex"],
                  sp["ex"]] + p_in_specs,
        out_specs=[sp["xs"], sp["state"]] + p_out_specs,
        out_shape=[jax.ShapeDtypeStruct((s, d_ssm), BF16),
                   jax.ShapeDtypeStruct((N_GROUPS, nc, D_STATE, gw), F32)] + p_out_shapes,
        scratch_shapes=[pltpu.VMEM((gps, D_STATE, gw), F32)] + p_scratch,
        compiler_params=_params(semantics),
    )(xbc_c, xbc_c, xbc_c, q_all, gam_t, etot_x, tb["ex_dt"], tb["ex_gam"], tb["ex_din"], tb["ex_dst"], *p_ops)
    return res[0], res[1], list(res[2:])


def _scan_bwd(name, xbc_c, dy, hs, q_all, gam_t, etot_x, etot_g, tb, *, direction, d_ssm, plan=None):
    s = xbc_c.shape[0]
    nc = s // CHUNK
    j_heads = tb["ex_dt"].shape[1] // HEAD_DIM
    gw = j_heads * HEAD_DIM
    gps = GROUPS_PER_STEP
    reverse = direction == 1
    sp = _scan_specs(not reverse, direction, nc, j_heads, d_ssm)
    hp = lax.Precision.HIGHEST
    p_ops, p_in_specs, p_out_shapes, p_out_specs, p_scratch, semantics = _plan_call_parts(plan)
    n_pi, n_po = len(p_ops), len(p_out_shapes)
    n_steps = (N_GROUPS // gps, nc)

    def kern(*refs):
        (xs_ref, b_ref, c_ref, dy_ref, hs_ref, q_ref, gamt_ref, etg_ref, etx_ref, exdt_ref, exgam_ref, exdin_ref,
         exdst_ref, sel_ref) = refs[:14]
        plan_ins = refs[14:14 + n_pi]
        dxs_ref, db_ref, dc_ref, ddt_ref, da_ref = refs[14 + n_pi:19 + n_pi]
        plan_outs = refs[19 + n_pi:19 + n_pi + n_po]
        dh_ref, tmp_ref = refs[19 + n_pi + n_po:21 + n_pi + n_po]
        plan_sems = refs[21 + n_pi + n_po:]
        _plan_phase(plan, "start", n_steps, plan_ins, plan_outs, plan_sems)
        _plan_phase(plan, "middle", n_steps, plan_ins, plan_outs, plan_sems)

        @pl.when(pl.program_id(1) == 0)
        def _():
            dh_ref[...] = jnp.zeros_like(dh_ref)

        li, si, mask, mask_t, lo = _scan_masks(reverse)
        sel = sel_ref[...]
        incl = ((si <= li) if reverse else (si >= li)).astype(F32)
        excl = ((si > li) if reverse else (si < li)).astype(F32)
        for gi in range(gps):
            grp_lanes = slice(gi * D_STATE, (gi + 1) * D_STATE)
            bb = b_ref[:, grp_lanes].astype(BF16)
            cb = c_ref[:, grp_lanes].astype(BF16)
            cbt = _dot(cb, bb, _NT)
            cbt_t = _dot(bb, cb, _NT)
            q = q_ref[:, gi * LANES:(gi + 1) * LANES]
            dtx, dinx, dstx = _dot(q, exdt_ref[...]), _dot(q, exdin_ref[...]), _dot(q, exdst_ref[...])
            gcol = _dot(q, exgam_ref[...])
            x_all = xs_ref[:, gi * gw:(gi + 1) * gw].astype(F32)
            dy_all = dy_ref[:, gi * gw:(gi + 1) * gw].astype(F32)
            xdt = x_all * dtx
            xb = xdt.astype(BF16)
            ht = hs_ref[gi, 0]
            hb = ht.astype(BF16)
            dht = dh_ref[gi]
            dhb = dht.astype(BF16)
            y_off = _dot(cb, hb) * dinx
            dx_off = _dot(bb, dhb) * dstx
            dyd = (dy_all * dinx).astype(BF16)
            xd = (xdt * dstx).astype(BF16)
            dc_acc = _dot(dyd, hb, _NT)
            db_acc = _dot(xd, dhb, _NT)
            dh_ref[gi] = dht * etx_ref[0, 0:1, gi * gw:(gi + 1) * gw] + _dot(cb, dyd, _TN)
            q_cols = _dot((dy_all * y_off).astype(BF16), sel)
            c_cols = _dot((xdt * dx_off).astype(BF16), sel)
            through = _sum0(_dot((dht * ht).astype(BF16), sel))
            dcbt = jnp.zeros((CHUNK, CHUNK), F32)
            for p in range(j_heads // 2):
                lanes = slice(p * CHUNK, (p + 1) * CHUNK)
                out_lanes = slice(gi * gw + p * CHUNK, gi * gw + (p + 1) * CHUNK)
                x2b = xb[:, lanes]
                dy2 = dy_all[:, lanes]
                acc = dx_off[:, lanes]
                for idx, j in enumerate((2 * p, 2 * p + 1)):
                    gc = gcol[:, j * CHUNK:(j + 1) * CHUNK]
                    gr = gamt_ref[pl.ds(gi * j_heads + j, 1), :]
                    decay = jnp.exp(jnp.where(mask, gc - gr, NEG))
                    decay_t = jnp.exp(jnp.where(mask_t, gr - gc, NEG))
                    dy_head = jnp.where(lo if idx == 0 else jnp.logical_not(lo), dy2, 0.0).astype(BF16)
                    acc = acc + _dot((cbt_t * decay_t).astype(BF16), dy_head)
                    dm = decay * _dot(dy_head, x2b, _NT)
                    dcbt = dcbt + dm
                    e = (cbt * dm).astype(BF16)
                    in_lane_j = si == j
                    q_cols = (q_cols + jnp.where(in_lane_j, jnp.sum(e.astype(F32), axis=1, keepdims=True), 0.0)
                              - _dot(e, jnp.where(in_lane_j, 1.0, 0.0).astype(BF16), _TN))
                dxs_ref[:, out_lanes] = (acc * dtx[:, lanes]).astype(dxs_ref.dtype)
                tmp_ref[:, lanes] = acc * x_all[:, lanes]
            dcb = dcbt.astype(BF16)
            dc_ref[:, grp_lanes] = (dc_acc + _dot(dcb, bb)).astype(dc_ref.dtype)
            db_ref[:, grp_lanes] = (db_acc + _dot(dcb, cb, _TN)).astype(db_ref.dtype)
            ddt_ref[:, gi * LANES:(gi + 1) * LANES] = _dot(tmp_ref[...].astype(BF16), sel)
            da_ref[:, gi * LANES:(gi + 1) * LANES] = (
                jnp.dot(incl, q_cols, preferred_element_type=F32, precision=hp)
                + jnp.dot(excl, c_cols, preferred_element_type=F32, precision=hp)
                + through * etg_ref[0, 0:1, gi * LANES:(gi + 1) * LANES])
        _plan_phase(plan, "end", n_steps, plan_ins, plan_outs, plan_sems)

    gn = N_GROUPS * D_STATE
    res = pl.pallas_call(
        kern, name=name, grid=n_steps,
        in_specs=[sp["xs"], sp["b"], sp["c"], sp["xs"], sp["state"], sp["q"], sp["gam_t"], sp["etot_g"], sp["etot_x"],
                  sp["ex"], sp["ex_gam"], sp["ex"], sp["ex"], sp["sel"]] + p_in_specs,
        out_specs=[sp["xs"], sp["grp"], sp["grp"], sp["grp"], sp["grp"]] + p_out_specs,
        out_shape=[jax.ShapeDtypeStruct((s, d_ssm), BF16), jax.ShapeDtypeStruct((s, gn), BF16),
                   jax.ShapeDtypeStruct((s, gn), BF16), jax.ShapeDtypeStruct((s, gn), F32),
                   jax.ShapeDtypeStruct((s, gn), F32)] + p_out_shapes,
        scratch_shapes=[pltpu.VMEM((gps, D_STATE, gw), F32), pltpu.VMEM((CHUNK, gw), F32)] + p_scratch,
        compiler_params=_params(semantics),
    )(xbc_c, xbc_c, xbc_c, dy, hs, q_all, gam_t, etot_g, etot_x, tb["ex_dt"], tb["ex_gam"], tb["ex_din"], tb["ex_dst"],
      tb["sel_cols"], *p_ops)
    return res[:5], list(res[5:])


def _dt_prepare(dt_raw, dt_bias, a_neg, route):
    s, h2 = dt_raw.shape
    n_heads = h2 // 2
    qw = route.shape[1]

    def kern(raw_ref, bias_ref, a_ref, route_ref, dt_ref, q_ref, gamt_ref, etot_ref):
        dt = _softplus(raw_ref[...] + bias_ref[...])
        a = dt * a_ref[...]
        li = lax.broadcasted_iota(jnp.int32, (CHUNK, CHUNK), 0)
        si = lax.broadcasted_iota(jnp.int32, (CHUNK, CHUNK), 1)
        tri = (li >= si).astype(F32)
        cs = jnp.dot(tri, a, preferred_element_type=F32, precision=lax.Precision.HIGHEST)
        tot = _sum0(a)
        fwd = lax.broadcasted_iota(jnp.int32, (CHUNK, h2), 1) < n_heads
        gam = jnp.where(fwd, cs, a - cs)
        din = jnp.where(fwd, jnp.exp(cs), jnp.exp(tot + gam))
        dst = jnp.where(fwd, jnp.exp(tot - cs), jnp.exp(cs - a))
        pieces = []
        for v in (dt, gam, din, dst):
            hi = v.astype(BF16)
            rest = v - hi.astype(F32)
            mid = rest.astype(BF16)
            pieces += [hi, mid, (rest - mid.astype(F32)).astype(BF16)]
        q_ref[...] = _dot(jnp.concatenate(pieces, axis=1), route_ref[...]).astype(BF16)
        dt_ref[...] = dt
        gamt_ref[...] = gam.T
        etot_ref[...] = jnp.broadcast_to(jnp.exp(tot), (8, h2))

    nc = s // CHUNK
    rows = lambda w: pl.BlockSpec((CHUNK, w), lambda i: (i, 0))
    whole = lambda a: pl.BlockSpec(a.shape, lambda i: (0, 0))
    return pl.pallas_call(
        kern, name="dt_prepare", grid=(nc,),
        in_specs=[rows(h2), whole(dt_bias), whole(a_neg), whole(route)],
        out_specs=[rows(h2), rows(qw), pl.BlockSpec((h2, CHUNK), lambda i: (0, i)), pl.BlockSpec((8, h2), lambda i: (i, 0))],
        out_shape=[jax.ShapeDtypeStruct((s, h2), F32), jax.ShapeDtypeStruct((s, qw), BF16),
                   jax.ShapeDtypeStruct((h2, s), F32), jax.ShapeDtypeStruct((nc * 8, h2), F32)],
        compiler_params=_params(("parallel",)),
    )(dt_raw, dt_bias, a_neg, route)


def _dt_backward(da_dirs, ddt_dirs, dt, dt_raw, dt_bias, a_neg):
    h2 = dt.shape[1]
    n_heads = h2 // 2
    j_heads = n_heads // N_GROUPS
    lane = jnp.arange(N_GROUPS * LANES)[:, None]
    head = (lane // LANES) * j_heads + lane % LANES
    pick = [((lane % LANES < j_heads) & (head + direction * n_heads == jnp.arange(h2)[None])).astype(BF16)
            for direction in (0, 1)]

    def compact(wide_f, wide_r, pick_f, pick_r):
        total = 0.0
        for wide, sel in ((wide_f, pick_f), (wide_r, pick_r)):
            hi = wide.astype(BF16)
            total = total + _dot(hi, sel.astype(BF16)) + _dot((wide - hi.astype(F32)).astype(BF16), sel.astype(BF16))
        return total

    def body(da_f, da_r, ddt_f, ddt_r, dtv, raw, pick_f, pick_r, bias, a_head):
        dav = compact(da_f, da_r, pick_f, pick_r)
        ddtv = compact(ddt_f, ddt_r, pick_f, pick_r)
        draw = (ddtv + dav * a_head) * _sigmoid(raw + bias)
        return draw, _sum0(draw), _sum0(dav * dtv) * a_head
    return _rowwise("dt_backward", body, [*da_dirs, *ddt_dirs, dt, dt_raw], [*pick, dt_bias, a_neg], [(h2, BF16)], [h2, h2])


def _rms(x):
    r = lax.rsqrt(_mean1(x * x) + EPS)
    return x * r, r


def _rms_bwd(dy, y, r):
    return r * (dy - y * _mean1(dy * y))


def _norm_mod_fwd(name, x, g, sc, sh):
    def body(xv, gv, scv, shv):
        y, _ = _rms(xv)
        return ((y * gv) * (1.0 + scv) + shv,)
    return _rowwise(name, body, [x], [g, sc, sh], [(x.shape[1], BF16)])[0]


def _norm_mod_bwd(name, x, dh, dpass, g, sc, plan=None):
    d = x.shape[1]

    def body(xv, dhv, dpv, gv, scv):
        y, r = _rms(xv)
        dn = dhv * (1.0 + scv)
        dx = _rms_bwd(dn * gv, y, r) + dpv
        return dx, _sum0(dn * y), _sum0(dhv * (y * gv)), _sum0(dhv)
    return _rowwise(name, body, [x, dh, dpass], [g, sc], [(d, F32)], [d, d, d], plan=plan)


def _gated_residual_fwd(name, x, m, gate, gp):
    def body(xv, mv, gatev, gpv):
        y, _ = _rms(mv)
        return (xv + gatev * (y * gpv),)
    return _rowwise(name, body, [x, m], [gate, gp], [(x.shape[1], F32)])[0]


def _gated_residual_bwd(name, m, dx1, gate, gp):
    d = m.shape[1]

    def body(mv, dv, gatev, gpv):
        y, r = _rms(mv)
        dn = dv * gatev
        return _rms_bwd(dn * gpv, y, r), _sum0(dv * (y * gpv)), _sum0(dn * y)
    return _rowwise(name, body, [m, dx1], [gate, gp], [(d, BF16)], [d, d])


def _final_residual_loss(x1, f, tgt, gate, gp):
    d = x1.shape[1]

    def body(xv, fv, tv, gatev, gpv):
        y, r = _rms(fv)
        n = y * gpv
        err = xv + gatev * n - tv
        dx2 = err * (1.0 / d)
        dn = dx2 * gatev
        sq = jnp.sum(_sum0(err * err), axis=1, keepdims=True)
        return dx2, _rms_bwd(dn * gpv, y, r), jnp.broadcast_to(sq, (1, LANES)), _sum0(dx2 * n), _sum0(dn * y)
    return _rowwise("final_residual_loss", body, [x1, f, tgt], [gate, gp], [(d, F32), (d, BF16)], [LANES, d, d])


def _swiglu_fwd(gu):
    f = gu.shape[1] // 2

    def body(v):
        return (_silu(v[:, :f]) * v[:, f:],)
    return _rowwise("swiglu_fwd", body, [gu], [], [(f, BF16)])[0]


def _swiglu_bwd(gu, dact):
    f = gu.shape[1] // 2

    def body(v, dv):
        gt, up = v[:, :f], v[:, f:]
        return (jnp.concatenate([dv * up * _dsilu(gt), dv * _silu(gt)], axis=1),)
    return _rowwise("swiglu_bwd", body, [gu, dact], [], [(2 * f, BF16)])[0]


def _glu_fwd(glu_in, b_glu):
    c = glu_in.shape[1] // 2

    def body(v, bv):
        t = v + bv
        return (t[:, :c] * _sigmoid(t[:, c:]),)
    return _rowwise("glu_fwd", body, [glu_in], [b_glu], [(c, F32)])[0]


def _glu_bwd(glu_in, b_glu, du0):
    c = glu_in.shape[1] // 2

    def body(v, dv, bv):
        t = v + bv
        a, s = t[:, :c], _sigmoid(t[:, c:])
        dg = jnp.concatenate([dv * s, dv * a * s * (1.0 - s)], axis=1)
        return dg, _sum0(dg)
    return _rowwise("glu_bwd", body, [glu_in, du0], [b_glu], [(2 * c, BF16)], [2 * c])


def _ln_parts(u1):
    xc = u1 - _mean1(u1)
    r = lax.rsqrt(_mean1(xc * xc) + EPS)
    return xc * r, r


def _ln_silu_fwd(u1, ln_g, ln_b):
    def body(v, gv, bv):
        yh, _ = _ln_parts(v)
        return (_silu(yh * gv + bv),)
    return _rowwise("ln_silu_fwd", body, [u1], [ln_g, ln_b], [(u1.shape[1], BF16)])[0]


def _ln_silu_bwd(u1, du, ln_g, ln_b):
    d = u1.shape[1]

    def body(v, dv, gv, bv):
        yh, r = _ln_parts(v)
        dl = dv * _dsilu(yh * gv + bv)
        dyh = dl * gv
        du1 = r * (dyh - _mean1(dyh) - yh * _mean1(dyh * yh))
        return du1, _sum0(dl * yh), _sum0(dl)
    return _rowwise("ln_silu_bwd", body, [u1, du], [ln_g, ln_b], [(d, F32)], [d, d])


def _gate_merge_fwd(y_a, y_b, gl, b_gate):
    d = y_a.shape[1]

    def body(ya, yb, glv, bv):
        s = _sigmoid(glv + bv)
        return (s[:, :d] * ya + s[:, d:] * yb,)
    return _rowwise("gate_merge_fwd", body, [y_a, y_b, gl], [b_gate], [(d, BF16)])[0]


def _gate_merge_bwd(dmixin, y_a, y_b, gl, b_gate):
    d = y_a.shape[1]

    def body(dv, ya, yb, glv, bv):
        s = _sigmoid(glv + bv)
        sa, sb = s[:, :d], s[:, d:]
        dya, dyb = dv * sa, dv * sb
        dgl = jnp.concatenate([dv * ya * sa * (1.0 - sa), dv * yb * sb * (1.0 - sb)], axis=1)
        return dya, dyb, dgl, _sum0(dgl), _sum0(dyb)
    return _rowwise("gate_merge_bwd", body, [dmixin, y_a, y_b, gl], [b_gate],
                    [(d, BF16), (d, BF16), (2 * d, BF16)], [2 * d, d])


def _group_slices(d_ssm):
    gw = d_ssm // N_GROUPS
    return [slice(g * gw, (g + 1) * gw) for g in range(N_GROUPS)]


def _gated_norm_fwd(y_f, y_b, xbc_c, z, d_skip_x, g_ssm):
    d_ssm = y_f.shape[1]

    def body(yf, yb, xs, zv, dsk, gv):
        y = yf + yb + dsk * xs
        v = y * _silu(zv)
        outs = []
        for sl in _group_slices(d_ssm):
            w, _ = _rms(v[:, sl])
            outs.append(w)
        return y, jnp.concatenate(outs, axis=1) * gv
    return _rowwise("gated_norm_fwd", body, [y_f, y_b, (xbc_c, d_ssm, 0), z], [d_skip_x, g_ssm],
                    [(d_ssm, BF16), (d_ssm, BF16)])


def _gated_norm_bwd(y, z, dyn, xbc_c, d_skip_x, g_ssm):
    d_ssm = y.shape[1]

    def body(yv, zv, dv, xs, dsk, gv):
        sz = _silu(zv)
        v = yv * sz
        dw = dv * gv
        dvs, ws = [], []
        for sl in _group_slices(d_ssm):
            w, r = _rms(v[:, sl])
            ws.append(w)
            dvs.append(_rms_bwd(dw[:, sl], w, r))
        dvv = jnp.concatenate(dvs, axis=1)
        dy = dvv * sz
        return dy, dvv * yv * _dsilu(zv), _sum0(dv * jnp.concatenate(ws, axis=1)), _sum0(dy * xs)
    return _rowwise("gated_norm_bwd", body, [y, z, dyn, (xbc_c, d_ssm, 0)], [d_skip_x, g_ssm],
                    [(d_ssm, BF16), (d_ssm, BF16)], [d_ssm, d_ssm])


def _ssd_grad_merge(dxs_f, dxs_b, dy, db_f, db_b, dc_f, dc_b, d_skip_x):
    d_ssm = dy.shape[1]
    width = d_ssm + 2 * N_GROUPS * D_STATE

    def body(xf, xb, dv, bf, bb, cf, cbv, dsk):
        return (jnp.concatenate([xf + xb + dsk * dv, bf + bb, cf + cbv], axis=1),)
    return _rowwise("ssd_grad_merge", body, [dxs_f, dxs_b, dy, db_f, db_b, dc_f, dc_b], [d_skip_x], [(width, BF16)])[0]


def _adamw(name, w, g, m, v):
    c = w.shape[1]
    c1 = 1.0 - ADAM_B1 ** ADAM_STEP
    c2 = 1.0 - ADAM_B2 ** ADAM_STEP

    def body(wv, gv, mv, vv):
        mn = ADAM_B1 * mv + (1.0 - ADAM_B1) * gv
        vn = ADAM_B2 * vv + (1.0 - ADAM_B2) * (gv * gv)
        delta = -ADAM_LR * ((mn / c1) / (jnp.sqrt(vn / c2) + ADAM_EPS) + ADAM_WD * wv)
        return delta, mn, vn
    return _rowwise(name, body, [w, g, m, v], [], [(c, F32)] * 3)


def _local_step(x, tgt, mod, wts, sm, late=None):
    s, d = x.shape
    d_ssm = 2 * d
    n_heads = d_ssm // HEAD_DIM
    d_xbc = d_ssm + 2 * N_GROUPS * D_STATE
    sec = [0, d_ssm, d_ssm + d_xbc, d_ssm + d_xbc + 2 * n_heads, d_ssm + d_xbc + 2 * n_heads + 2 * d]
    sec.append(sec[-1] + 2 * d)
    sh1, sc1, g1, sh2, sc2, g2 = [mod[:, i * d:(i + 1) * d] for i in range(N_MOD)]
    win_t = wts["w_in_t"]
    sections = [(nm, (sec[i], sec[i + 1] - sec[i])) for i, nm in enumerate(("z", "xbc", "dt", "glu", "gate"))]

    h1 = _norm_mod_fwd("pre_mix_norm", x, sm["g_pre_mix"], sc1, sh1)
    z, xbc, dt_raw, glu_in, gate_l = [
        _matmul(f"proj_{nm}", h1, win_t, tb=True, b_rows=rows, out_dtype=F32 if nm == "dt" else BF16)
        for nm, rows in sections]
    plan_c = plan_f = plan_r = None
    if late is not None:
        plan_c = _gather_plan([late[n] for n in MATRICES[1:4]])
        plan_f = _gather_plan([late["w_down"]])
        plan_r = _gather_plan([late["w_gate_up"]])
    xbc_c = _dwconv_fwd("ssm_conv_fwd", xbc, sm["w_conv_ssm"], sm["b_conv_ssm"], silu=True, out_dtype=BF16, plan=plan_c)
    if late is not None:
        xbc_c, got_c = xbc_c
    tables = _scan_tables(n_heads)
    dt, q_all, gam_t, etot = _dt_prepare(dt_raw, sm["dt_bias"], sm["a_neg"], tables["route"])
    etot = etot.reshape(s // CHUNK, 8, 2 * n_heads)
    (etx_f, etg_f), (etx_r, etg_r) = [_chunk_decay(etot, n_heads, direction) for direction in (0, 1)]
    y_f, hs_f, got_f = _scan_fwd("ssd_fwd_f", xbc_c, q_all, gam_t, etx_f, tables, direction=0, d_ssm=d_ssm, plan=plan_f)
    y_r, hs_r, got_r = _scan_fwd("ssd_fwd_r", xbc_c, q_all, gam_t, etx_r, tables, direction=1, d_ssm=d_ssm, plan=plan_r)
    if late is not None:
        wts = dict(wts, w_ssm_out=got_c[0].reshape(-1, d), w_conv_out=got_c[1].reshape(-1, d),
                   w_mix_out=got_c[2].reshape(-1, d), w_down=got_f[0].reshape(-1, d), w_gate_up=got_r[0])
    y_ssd, yn = _gated_norm_fwd(y_f, y_r, xbc_c, z, sm["d_skip_x"], sm["g_ssm_norm"])
    y_a = _matmul("ssm_out", yn, wts["w_ssm_out"])
    u0 = _glu_fwd(glu_in, sm["b_glu"])
    u1 = _dwconv_fwd("dw_conv_fwd", u0, sm["w_dw"], sm["b_dw"], silu=False)
    u = _ln_silu_fwd(u1, sm["ln_g"], sm["ln_b"])
    y_b = _matmul("conv_out", u, wts["w_conv_out"], bias=sm["b_conv_out"])
    mixin = _gate_merge_fwd(y_a, y_b, gate_l, sm["b_gate"])
    mix = _matmul("mix_out", mixin, wts["w_mix_out"])
    x1 = _gated_residual_fwd("post_mix_residual", x, mix, g1, sm["g_post_mix"])
    h2 = _norm_mod_fwd("pre_ffn_norm", x1, sm["g_pre_ffn"], sc2, sh2)
    gu = _matmul("ffn_gate_up", h2, wts["w_gate_up"], b_blocks=N_CHIPS, out_dtype=BF16)
    act = _swiglu_fwd(gu)
    f = _matmul("ffn_down", act, wts["w_down"])

    dx2, df, sq, d_g2, d_gpf = _final_residual_loss(x1, f, tgt, g2, sm["g_post_ffn"])
    dact = _matmul("d_act", df, wts["w_down"], tb=True, out_dtype=BF16)
    g_w_down = _matmul("g_w_down", act, df, ta=True, out_dtype=BF16)
    dgu = _swiglu_bwd(gu, dact)
    dh2 = _matmul("d_h2", dgu, wts["w_gate_up"], tb=True, b_blocks=N_CHIPS)
    g_w_gate_up = _matmul("g_w_gate_up", h2, dgu, ta=True, out_dtype=BF16, out_blocks=N_CHIPS)
    dx1, d_gpre_ffn, d_sc2, d_sh2 = _norm_mod_bwd("pre_ffn_norm_bwd", x1, dh2, dx2, sm["g_pre_ffn"], sc2)
    dmix, d_g1, d_gpm = _gated_residual_bwd("post_mix_residual_bwd", mix, dx1, g1, sm["g_post_mix"])
    dmixin = _matmul("d_mixin", dmix, wts["w_mix_out"], tb=True, out_dtype=BF16)
    g_w_mix = _matmul("g_w_mix_out", mixin, dmix, ta=True, out_dtype=BF16)
    dy_a, dy_b, dgate_l, d_bgate, d_bco = _gate_merge_bwd(dmixin, y_a, y_b, gate_l, sm["b_gate"])
    du = _matmul("d_u", dy_b, wts["w_conv_out"], tb=True, out_dtype=BF16)
    g_w_co = _matmul("g_w_conv_out", u, dy_b, ta=True, out_dtype=BF16)
    du1, d_lng, d_lnb = _ln_silu_bwd(u1, du, sm["ln_g"], sm["ln_b"])
    du0, d_wdw, d_bdw = _dwconv_bwd("dw_conv_bwd", u0, sm["w_dw"], sm["b_dw"], du1, silu=False)
    dglu, d_bglu = _glu_bwd(glu_in, sm["b_glu"], du0)
    dyn = _matmul("d_yn", dy_a, wts["w_ssm_out"], tb=True, out_dtype=BF16)
    g_w_ssm = _matmul("g_w_ssm_out", yn, dy_a, ta=True, out_dtype=BF16)
    dy_ssd, dz, d_gssm, d_dskip_x = _gated_norm_bwd(y_ssd, z, dyn, xbc_c, sm["d_skip_x"], sm["g_ssm_norm"])
    early = [g_w_ssm.reshape(N_CHIPS, -1, d), g_w_co.reshape(N_CHIPS, -1, d), g_w_mix.reshape(N_CHIPS, -1, d),
             g_w_gate_up, g_w_down.reshape(N_CHIPS, -1, d)]
    plan_b = sums = None
    if late is not None:
        sums = _chip_sums("early", early)
        plan_b = _send_chips_plan(sums)
    (dxs_f, db_f, dc_f, ddt_f, da_f), received = _scan_bwd(
        "ssd_bwd_f", xbc_c, dy_ssd, hs_f, q_all, gam_t, etx_f, etg_f, tables, direction=0, d_ssm=d_ssm, plan=plan_b)
    (dxs_r, db_r, dc_r, ddt_r, da_r), _ = _scan_bwd(
        "ssd_bwd_r", xbc_c, dy_ssd, hs_r, q_all, gam_t, etx_r, etg_r, tables, direction=1, d_ssm=d_ssm)
    ddt_raw, d_dtbias, d_alog = _dt_backward((da_f, da_r), (ddt_f, ddt_r), dt, dt_raw, sm["dt_bias"], sm["a_neg"])
    dxbc_c = _ssd_grad_merge(dxs_f, dxs_r, dy_ssd, db_f, db_r, dc_f, dc_r, sm["d_skip_x"])
    dxbc, d_wconv, d_bconv = _dwconv_bwd("ssm_conv_bwd", xbc, sm["w_conv_ssm"], sm["b_conv_ssm"], dxbc_c,
                                         silu=True, dx_dtype=BF16)
    dsecs = [dz, dxbc, ddt_raw, dglu, dgate_l]
    dh1 = g_win = None
    for (nm, rows), dsec in zip(sections, dsecs):
        g_win = _matmul(f"g_w_in_{nm}", dsec, h1, ta=True, out_dtype=BF16, out_rows=(rows[0], sec[-1], g_win))
    hosts, sums_in, received_in = {}, None, [None] * 3
    if late is not None:
        sums_in = _chip_sums("w_in", [g_win.reshape(N_CHIPS, -1, d)])
        hosts = {"xbc": 0, "z": 1, "glu": 2}
    for (nm, rows), dsec in zip(sections, dsecs):
        plan_in = _send_chips_plan(sums_in, (hosts[nm],)) if nm in hosts else None
        dh1 = _matmul(f"d_h1_{nm}", dsec, win_t, b_rows=rows, add=dh1, plan=plan_in)
        if plan_in is not None:
            dh1, (received_in[hosts[nm]],) = dh1
    grad_x, d_gpre_mix, d_sc1, d_sh1 = _norm_mod_bwd("pre_mix_norm_bwd", x, dh1, dx1, sm["g_pre_mix"], sc1)

    dmod = jnp.concatenate([d_sh1, d_sc1, d_g1, d_sh2, d_sc2, d_g2], axis=1)
    if late is None:
        big = {"w_in_t": g_win, "w_ssm_out": g_w_ssm, "w_conv_out": g_w_co, "w_mix_out": g_w_mix,
               "w_gate_up": g_w_gate_up, "w_down": g_w_down}
    else:
        big = {"pending": (sums_in + sums, [received_in] + [[t, t, t] for t in received])}
    small = {"g_pre_mix": d_gpre_mix, "g_post_mix": d_gpm, "w_conv_ssm": d_wconv, "b_conv_ssm": d_bconv,
             "dt_bias": d_dtbias, "a_log": d_alog, "d_skip_x": d_dskip_x, "g_ssm_norm": d_gssm, "b_glu": d_bglu,
             "w_dw": d_wdw, "b_dw": d_bdw, "ln_g": d_lng, "ln_b": d_lnb, "b_conv_out": d_bco, "b_gate": d_bgate,
             "g_pre_ffn": d_gpre_ffn, "g_post_ffn": d_gpf}
    return sq, grad_x, big, small, dmod


ANY = pl.BlockSpec(memory_space=pl.ANY)
WHOLE_VMEM = pl.BlockSpec(memory_space=pltpu.VMEM)


def _mesh_place():
    x, y, c = lax.axis_index("x"), lax.axis_index("y"), lax.axis_index("c")
    other_chips = [(1 - x, y), (x, 1 - y), (1 - x, 1 - y)]
    return x, y, c, other_chips


def _remote(src, dst, send_sems, recv_sems, k, device):
    return pltpu.make_async_remote_copy(src_ref=src, dst_ref=dst, send_sem=send_sems.at[k], recv_sem=recv_sems.at[k],
                                        device_id=device, device_id_type=MESH)


def _gather_devices(name, block):
    m_per, n = block.shape

    def body(x_ref, out_ref, send_sems, recv_sems, local_sem):
        x, y, c, chips = _mesh_place()
        me, sibling = (x, y, c), (x, y, 1 - c)

        def rows(px, py, pc):
            return out_ref.at[pl.ds((4 * px + 2 * py + pc) * m_per, m_per), :]

        def copy(k, blk, to, src=None):
            return _remote(rows(*blk) if src is None else src, rows(*blk), send_sems, recv_sems, k, to)

        mine = pltpu.make_async_copy(x_ref, rows(*me), local_sem)
        mine.start()
        first = [copy(0, me, sibling, src=x_ref)]
        first += [copy(1 + j, me, (*chip, c), src=x_ref) for j, chip in enumerate(chips)]
        for cp in first:
            cp.start()
        passed = [copy(4 + j, (*chip, c), sibling) for j, chip in enumerate(chips)]
        for j, chip in enumerate(chips):
            copy(1 + j, (*chip, c), me).wait_recv()
            passed[j].start()
        copy(0, sibling, me).wait_recv()
        for j, chip in enumerate(chips):
            copy(4 + j, (*chip, 1 - c), me).wait_recv()
        for cp in first + passed:
            cp.wait_send()
        mine.wait()

    return pl.pallas_call(
        body, name=name, out_shape=jax.ShapeDtypeStruct((N_DEV * m_per, n), block.dtype),
        in_specs=[WHOLE_VMEM], out_specs=WHOLE_VMEM,
        scratch_shapes=[pltpu.SemaphoreType.DMA((7,)), pltpu.SemaphoreType.DMA((7,)), pltpu.SemaphoreType.DMA],
        compiler_params=pltpu.CompilerParams(vmem_limit_bytes=VMEM_LIMIT),
    )(block)


class _Plan:
    def __init__(self, operands, out_shapes, copies, phases):
        self.operands, self.out_shapes, self.copies, self.phases = list(operands), list(out_shapes), copies, phases

    def sems(self):
        return [pltpu.SemaphoreType.DMA((self.copies,)), pltpu.SemaphoreType.DMA((self.copies,))]


def _run_plan(name, plan):
    n_in, n_out = len(plan.operands), len(plan.out_shapes)

    def body(*refs):
        ins, outs = refs[:n_in], refs[n_in:n_in + n_out]
        send_sems, recv_sems = refs[n_in + n_out:]
        for phase in ("start", "middle", "end"):
            if phase in plan.phases:
                plan.phases[phase](ins, outs, send_sems, recv_sems)

    return list(pl.pallas_call(body, name=name, out_shape=plan.out_shapes, in_specs=[ANY] * n_in,
                               out_specs=[ANY] * n_out, scratch_shapes=plan.sems())(*plan.operands))


def _gather_plan(shards):
    n = len(shards)

    def copies(kinds, ins, outs, send_sems, recv_sems):
        x, y, c, chips = _mesh_place()
        me = 2 * x + y
        sibling = (x, y, 1 - c)

        def half(i, h):
            hr = ins[i].shape[0] // 2
            return pl.ds(h * hr, hr)

        def block(i, j, h):
            cx, cy = chips[j]
            return outs[i].at[2 * cx + cy, half(i, h)]

        make = {
            "over_ici": lambda i, j: _remote(ins[i].at[half(i, c)], outs[i].at[me, half(i, c)], send_sems, recv_sems,
                                             6 * i + j, (*chips[j], c)),
            "arrived": lambda i, j: _remote(block(i, j, c), block(i, j, c), send_sems, recv_sems, 6 * i + j, (*chips[j], c)),
            "passed_on": lambda i, j: _remote(block(i, j, c), block(i, j, c), send_sems, recv_sems, 6 * i + 3 + j, sibling),
            "from_sibling": lambda i, j: _remote(block(i, j, 1 - c), block(i, j, 1 - c), send_sems, recv_sems,
                                                 6 * i + 3 + j, sibling),
        }
        res = []
        for kind in kinds:
            if kind == "own":
                res.append([_remote(ins[i], outs[i].at[me], send_sems, recv_sems, 6 * n + i, sibling) for i in range(n)])
            else:
                res.append([make[kind](i, j) for i in range(n) for j in range(3)])
        return res

    def start(*refs):
        over_ici, own = copies(("over_ici", "own"), *refs)
        for cp in over_ici + own:
            cp.start()

    def end(*refs):
        arrived, passed_on = copies(("arrived", "passed_on"), *refs)
        for got, fwd in zip(arrived, passed_on):
            got.wait_recv()
            fwd.start()
        from_sibling, own_in = copies(("from_sibling", "own"), *refs)
        for cp in from_sibling + own_in:
            cp.wait_recv()
        over_ici, passed_on, own_out = copies(("over_ici", "passed_on", "own"), *refs)
        for cp in over_ici + passed_on + own_out:
            cp.wait_send()

    return _Plan(shards, [jax.ShapeDtypeStruct((N_CHIPS,) + s.shape, s.dtype) for s in shards], 7 * n,
                 {"start": start, "end": end})


def _send_sibling_halves(name, grads):
    n = len(grads)

    def body(*refs):
        ins, outs = refs[:n], refs[n:2 * n]
        send_sems, recv_sems = refs[2 * n:]
        x, y, c, _ = _mesh_place()
        sibling = (x, y, 1 - c)
        copies = []
        for i in range(n):
            for j in range(N_CHIPS):
                copies.append(_remote(ins[i].at[j, 1 - c], outs[i].at[j], send_sems, recv_sems, N_CHIPS * i + j, sibling))
                copies[-1].start()
        for cp in copies:
            cp.wait_recv()
        for cp in copies:
            cp.wait_send()

    return pl.pallas_call(
        body, name=name,
        out_shape=[jax.ShapeDtypeStruct((g.shape[0],) + g.shape[2:], g.dtype) for g in grads],
        in_specs=[ANY] * n, out_specs=[ANY] * n,
        scratch_shapes=[pltpu.SemaphoreType.DMA((N_CHIPS * n,)), pltpu.SemaphoreType.DMA((N_CHIPS * n,))],
    )(*grads)


def _send_chips_plan(sums, neighbours=(0, 1, 2)):
    n = len(sums)

    def copies(ins, outs, send_sems, recv_sems):
        x, y, c, chips = _mesh_place()
        return [_remote(ins[i].at[2 * chips[j][0] + chips[j][1]], outs[i].at[j], send_sems, recv_sems, 3 * i + j,
                        (*chips[j], c))
                for i in range(n) for j in neighbours]

    def start(*refs):
        for cp in copies(*refs):
            cp.start()

    def end(*refs):
        for cp in copies(*refs):
            cp.wait_recv()
        for cp in copies(*refs):
            cp.wait_send()

    return _Plan(sums, [jax.ShapeDtypeStruct((3,) + g.shape[1:], g.dtype) for g in sums], 3 * n,
                 {"start": start, "end": end})


def _exchange_halves(name, shards):
    n = len(shards)

    def body(*refs):
        outs = refs[n:2 * n]
        send_sems, recv_sems = refs[2 * n:]
        x, y, c, _ = _mesh_place()
        sibling = (x, y, 1 - c)
        remote = [_remote(outs[i].at[c], outs[i].at[c], send_sems, recv_sems, i, sibling) for i in range(n)]
        for cp in remote:
            cp.start()
        for i in range(n):
            _remote(outs[i].at[1 - c], outs[i].at[1 - c], send_sems, recv_sems, i, sibling).wait_recv()
        for cp in remote:
            cp.wait_send()

    return pl.pallas_call(
        body, name=name,
        out_shape=[jax.ShapeDtypeStruct(h.shape, h.dtype) for h in shards],
        in_specs=[ANY] * n, out_specs=[ANY] * n, input_output_aliases={i: i for i in range(n)},
        scratch_shapes=[pltpu.SemaphoreType.DMA((n,)), pltpu.SemaphoreType.DMA((n,))],
    )(*shards)


def _divisor_tile(rows, row_bytes, quantum=16):
    best = rows
    for t in range(quantum, rows + 1, quantum):
        if rows % t == 0 and 2 * t * row_bytes <= ROW_TILE_BUDGET:
            best = t
    return best


def _add_sibling(name, g4, t1):
    nb, _, hr, cols = g4.shape
    t = _divisor_tile(hr, cols * 6)

    def kern(g_ref, t_ref, o_ref):
        o_ref[0] = (g_ref[0, 0].astype(F32) + t_ref[0].astype(F32)).astype(o_ref.dtype)

    return pl.pallas_call(
        kern, name=name, grid=(nb, hr // t),
        in_specs=[pl.BlockSpec((1, 1, t, cols), lambda j, i: (j, lax.axis_index("c"), i, 0)),
                  pl.BlockSpec((1, t, cols), lambda j, i: (j, i, 0))],
        out_specs=pl.BlockSpec((1, t, cols), lambda j, i: (j, i, 0)),
        out_shape=jax.ShapeDtypeStruct((nb, hr, cols), g4.dtype),
        compiler_params=_params(("parallel", "parallel")),
    )(g4, t1)


def _add_chips(name, s1, t3):
    _, hr, cols = s1.shape
    t = _divisor_tile(hr, cols * 12)

    def kern(s_ref, t0_ref, t1_ref, t2_ref, o_ref):
        acc = s_ref[0].astype(F32)
        for t_ref in (t0_ref, t1_ref, t2_ref):
            acc = acc + t_ref[0].astype(F32)
        o_ref[0] = acc

    slot = lambda j: pl.BlockSpec((1, t, cols), functools.partial(lambda i, j: (j, i, 0), j=j))
    return pl.pallas_call(
        kern, name=name, grid=(hr // t,),
        in_specs=[pl.BlockSpec((1, t, cols), lambda i: (2 * lax.axis_index("x") + lax.axis_index("y"), i, 0)),
                  slot(0), slot(1), slot(2)],
        out_specs=pl.BlockSpec((1, t, cols), lambda i: (lax.axis_index("c"), i, 0)),
        out_shape=jax.ShapeDtypeStruct((2, hr, cols), F32),
        compiler_params=_params(("parallel",)),
    )(s1, *t3)


def _chip_sums(tag, grads):
    g4 = [g.reshape(N_CHIPS, 2, g.shape[1] // 2, g.shape[2]) for g in grads]
    t1 = _send_sibling_halves("grads_to_sibling_" + tag, g4)
    return [_add_sibling(f"chip_sum_{tag}_{i}", g, t) for i, (g, t) in enumerate(zip(g4, t1))]


def _shard_sums(sums, received):
    halves = [_add_chips(f"shard_sum_{i}", s, t) for i, (s, t) in enumerate(zip(sums, received))]
    full = _exchange_halves("grad_halves_to_sibling", halves)
    return [f.reshape(f.shape[1] * 2, f.shape[2]) for f in full]


def _pack_rows(size, width):
    return -(-size // (8 * width)) * 8


def _pack(arrays, width):
    parts = []
    for a in arrays:
        flat = a.reshape(-1).astype(F32)
        rows = _pack_rows(flat.shape[0], width)
        parts.append(jnp.pad(flat, (0, rows * width - flat.shape[0])).reshape(rows, width))
    return jnp.concatenate(parts, axis=0)


def _unpack(block, shapes, width):
    out, r = [], 0
    for shp in shapes:
        size = 1
        for s_ in shp:
            size *= s_
        rows = _pack_rows(size, width)
        out.append(block[r:r + rows].reshape(-1)[:size].reshape(shp))
        r += rows
    return out


SMALL_PARAMS = ("b_ada", "g_pre_mix", "g_post_mix", "b_conv_ssm", "dt_bias_fwd", "dt_bias_bwd", "a_log_fwd", "a_log_bwd",
                "d_skip", "g_ssm_norm", "b_glu", "b_dw", "ln_g", "ln_b", "b_conv_out", "b_gate", "g_pre_ffn", "g_post_ffn")
SHARDED_SMALL = ("w_conv_ssm", "w_dw")
MATRICES = ("w_in", "w_ssm_out", "w_conv_out", "w_mix_out", "w_gate_up", "w_down")
ALL_PARAMS = ("w_ada", "b_ada", "g_pre_mix", "g_post_mix", "w_in", "w_conv_ssm", "b_conv_ssm", "dt_bias_fwd", "dt_bias_bwd",
              "a_log_fwd", "a_log_bwd", "d_skip", "g_ssm_norm", "w_ssm_out", "b_glu", "w_dw", "b_dw", "ln_g", "ln_b",
              "w_conv_out", "b_conv_out", "b_gate", "w_mix_out", "g_pre_ffn", "g_post_ffn", "w_gate_up", "w_down")
COND_ROWS = 48
COND_CONV_ROW = 8
COND_DW_ROW = 16
MOD_ROWS = 16


def kernel(x, c, w_ada, b_ada, g_pre_mix, g_post_mix, w_in, w_conv_ssm, b_conv_ssm, dt_bias_fwd, dt_bias_bwd, a_log_fwd, a_log_bwd, d_skip, g_ssm_norm, w_ssm_out, b_glu, w_dw, b_dw, ln_g, ln_b, w_conv_out, b_conv_out, b_gate, w_mix_out, g_pre_ffn, g_post_ffn, w_gate_up, w_down, loss_target, m_w_ada, m_b_ada, m_g_pre_mix, m_g_post_mix, m_w_in, m_w_conv_ssm, m_b_conv_ssm, m_dt_bias_fwd, m_dt_bias_bwd, m_a_log_fwd, m_a_log_bwd, m_d_skip, m_g_ssm_norm, m_w_ssm_out, m_b_glu, m_w_dw, m_b_dw, m_ln_g, m_ln_b, m_w_conv_out, m_b_conv_out, m_b_gate, m_w_mix_out, m_g_pre_ffn, m_g_post_ffn, m_w_gate_up, m_w_down, v_w_ada, v_b_ada, v_g_pre_mix, v_g_post_mix, v_w_in, v_w_conv_ssm, v_b_conv_ssm, v_dt_bias_fwd, v_dt_bias_bwd, v_a_log_fwd, v_a_log_bwd, v_d_skip, v_g_ssm_norm, v_w_ssm_out, v_b_glu, v_w_dw, v_b_dw, v_ln_g, v_ln_b, v_w_conv_out, v_b_conv_out, v_b_gate, v_w_mix_out, v_g_pre_ffn, v_g_post_ffn, v_w_gate_up, v_w_down):
    given = dict(locals())
    wgt = {n: given[n][0] for n in ALL_PARAMS}
    mom = {n: given["m_" + n][0] for n in ALL_PARAMS}
    var = {n: given["v_" + n][0] for n in ALL_PARAMS}
    xs, tgt = x[0], loss_target[0]
    s, d = xs.shape
    d_ssm = 2 * d
    n_heads = d_ssm // HEAD_DIM
    d_xbc = d_ssm + 2 * N_GROUPS * D_STATE
    xi, yi, ci = lax.axis_index("x"), lax.axis_index("y"), lax.axis_index("c")
    chip = 2 * xi + yi
    dev = 2 * chip + ci
    k_conv, k_dw = wgt["w_conv_ssm"].shape[0], wgt["w_dw"].shape[0]
    xbc_shard, dw_shard = d_xbc // N_CHIPS, d // N_CHIPS

    width1 = max(d, xbc_shard)
    blk = jnp.zeros((COND_ROWS, width1), F32)
    blk = blk.at[0, :d].set(c[0])
    blk = blk.at[COND_CONV_ROW:COND_CONV_ROW + k_conv, :xbc_shard].set(wgt["w_conv_ssm"])
    blk = blk.at[COND_DW_ROW:COND_DW_ROW + k_dw, :dw_shard].set(wgt["w_dw"])
    g1 = _gather_devices("gather_cond", blk).reshape(N_DEV, COND_ROWS, width1)
    c_all = g1[:, 0, :d]
    w_conv_full = jnp.concatenate([g1[2 * k, COND_CONV_ROW:COND_CONV_ROW + k_conv, :xbc_shard] for k in range(N_CHIPS)], axis=1)
    w_dw_full = jnp.concatenate([g1[2 * k, COND_DW_ROW:COND_DW_ROW + k_dw, :dw_shard] for k in range(N_CHIPS)], axis=1)
    c_act = jnp.pad(c_all * _sigmoid(c_all), ((0, MOD_ROWS - N_DEV), (0, 0)))

    mod_part = _matmul("ada_mod", c_act, wgt["w_ada"])
    g2 = _gather_devices("gather_mod", mod_part).reshape(N_DEV, MOD_ROWS, mod_part.shape[1])
    mod_all = jnp.concatenate([g2[2 * k, :N_DEV] for k in range(N_CHIPS)], axis=1) + wgt["b_ada"][None]
    mod = lax.dynamic_slice_in_dim(mod_all, dev, 1, axis=0)

    shards = [wgt["w_in"].T.astype(BF16)] + [wgt[n].astype(BF16) for n in MATRICES[1:]]
    wts = {"w_in_t": _run_plan("gather_w_in", _gather_plan(shards[:1]))[0].reshape(-1, d)}
    late = dict(zip(MATRICES[1:], shards[1:]))
    row = lambda v: v.reshape(1, -1)
    sm = {"g_pre_mix": row(wgt["g_pre_mix"]), "g_post_mix": row(wgt["g_post_mix"]), "w_conv_ssm": w_conv_full,
          "b_conv_ssm": row(wgt["b_conv_ssm"]),
          "dt_bias": row(jnp.concatenate([wgt["dt_bias_fwd"], wgt["dt_bias_bwd"]])),
          "a_neg": row(-jnp.exp(jnp.concatenate([wgt["a_log_fwd"], wgt["a_log_bwd"]]))),
          "d_skip_x": row(jnp.repeat(wgt["d_skip"], HEAD_DIM)), "g_ssm_norm": row(wgt["g_ssm_norm"]),
          "b_glu": row(wgt["b_glu"]), "w_dw": w_dw_full, "b_dw": row(wgt["b_dw"]), "ln_g": row(wgt["ln_g"]),
          "ln_b": row(wgt["ln_b"]), "b_conv_out": row(wgt["b_conv_out"]), "b_gate": row(wgt["b_gate"]),
          "g_pre_ffn": row(wgt["g_pre_ffn"]), "g_post_ffn": row(wgt["g_post_ffn"])}

    sq, grad_x, big, small, dmod = _local_step(xs, tgt, mod, wts, sm, late=late)
    loss = lax.psum((0.5 / d) * sq[0, 0], ("x", "y", "c"))

    local_small = {"b_ada": dmod, "g_pre_mix": small["g_pre_mix"], "g_post_mix": small["g_post_mix"],
                   "b_conv_ssm": small["b_conv_ssm"], "dt_bias_fwd": small["dt_bias"][:, :n_heads],
                   "dt_bias_bwd": small["dt_bias"][:, n_heads:], "a_log_fwd": small["a_log"][:, :n_heads],
                   "a_log_bwd": small["a_log"][:, n_heads:],
                   "d_skip": jnp.sum(small["d_skip_x"].reshape(n_heads, HEAD_DIM), axis=1),
                   "g_ssm_norm": small["g_ssm_norm"], "b_glu": small["b_glu"], "b_dw": small["b_dw"],
                   "ln_g": small["ln_g"], "ln_b": small["ln_b"], "b_conv_out": small["b_conv_out"],
                   "b_gate": small["b_gate"], "g_pre_ffn": small["g_pre_ffn"], "g_post_ffn": small["g_post_ffn"],
                   "w_conv_ssm": small["w_conv_ssm"], "w_dw": small["w_dw"]}
    names = SMALL_PARAMS + SHARDED_SMALL
    pack = _pack([local_small[n] for n in names], d)
    rows_p = pack.shape[0]
    g3 = _gather_devices("gather_small_grads", pack).reshape(N_DEV, rows_p, d)
    total = _rowwise("sum_small_grads", lambda *blocks: (functools.reduce(lambda a, b: a + b, blocks),),
                     [g3[i] for i in range(N_DEV)], [], [(d, F32)])[0]
    full_shapes = [wgt[n].shape for n in SMALL_PARAMS] + [(k_conv, d_xbc), (k_dw, d)]
    summed = dict(zip(names, _unpack(total, full_shapes, d)))
    grads = {n: summed[n] for n in SMALL_PARAMS}
    grads["w_conv_ssm"] = lax.dynamic_slice_in_dim(summed["w_conv_ssm"], chip * xbc_shard, xbc_shard, axis=1)
    grads["w_dw"] = lax.dynamic_slice_in_dim(summed["w_dw"], chip * dw_shard, dw_shard, axis=1)

    dmod_all = g3[:, :N_MOD, :].reshape(N_DEV, N_MOD * d)
    ada_cols = wgt["w_ada"].shape[1]
    dmod_cols = jnp.pad(lax.dynamic_slice_in_dim(dmod_all, chip * ada_cols, ada_cols, axis=1),
                        ((0, MOD_ROWS - N_DEV), (0, 0)))
    grads["w_ada"] = _matmul("g_w_ada", c_act, dmod_cols, ta=True)

    reduced = _shard_sums(*big["pending"])
    grads["w_in"] = reduced[0].T
    for n, g in zip(MATRICES[1:], reduced[1:]):
        grads[n] = g

    delta, new_m, new_v = {}, {}, {}
    for n in ("w_ada",) + MATRICES:
        delta[n], new_m[n], new_v[n] = _adamw("adamw_" + n, wgt[n], grads[n], mom[n], var[n])
    for group, width, tag in ((SMALL_PARAMS, d, "small"), (SHARDED_SMALL, LANES, "conv")):
        shapes = [wgt[n].shape for n in group]
        packs = [_pack([src[n] for n in group], width) for src in (wgt, grads, mom, var)]
        outs = _adamw("adamw_" + tag, *packs)
        for res, o in zip((delta, new_m, new_v), outs):
            res.update(zip(group, _unpack(o, shapes, width)))

    lead = lambda a: a[None]
    return (loss, grad_x[None], *[lead(grads[n]) for n in ALL_PARAMS], *[lead(delta[n]) for n in ALL_PARAMS],
            *[lead(new_m[n]) for n in ALL_PARAMS], *[lead(new_v[n]) for n in ALL_PARAMS])
```

```python
import functools

import jax
import jax.numpy as jnp
from jax import lax
from jax.experimental import pallas as pl
from jax.experimental.pallas import tpu as pltpu

F32 = jnp.float32
BF16 = jnp.bfloat16

N_GROUPS = 8
HEAD_DIM = 64
D_STATE = 128
CHUNK = 128
EPS = 1e-6
N_MOD = 6
ADAM_LR = 0.001
ADAM_B1 = 0.9
ADAM_B2 = 0.999
ADAM_EPS = 1e-08
ADAM_WD = 0.01
ADAM_STEP = 10

V7X_VMEM_BYTES = 64 * 1024 * 1024
VMEM_LIMIT = V7X_VMEM_BYTES - 8 * 1024 * 1024
ROW_TILE_BUDGET = 20 * 1024 * 1024
LANES = 128
NEG = -1e30
MESH = pl.DeviceIdType.MESH
N_CHIPS = 4
N_DEV = 8


def _params(sem):
    return pltpu.CompilerParams(dimension_semantics=sem, vmem_limit_bytes=VMEM_LIMIT)


def _sigmoid(x):
    return 1.0 / (1.0 + jnp.exp(-x))


def _silu(x):
    return x * _sigmoid(x)


def _dsilu(x):
    s = _sigmoid(x)
    return s * (1.0 + x * (1.0 - s))


def _softplus(x):
    return jnp.maximum(x, 0.0) + jnp.log(1.0 + jnp.exp(-jnp.abs(x)))


def _sum0(a):
    return jnp.sum(a, axis=0, keepdims=True)


def _mean1(a):
    return jnp.mean(a, axis=1, keepdims=True)


def _rowwise(name, body, rows, bcasts, out_rows, out_accs=(), tile=None, plan=None):
    rows = [r if isinstance(r, tuple) else (r, r.shape[1], 0) for r in rows]
    s = rows[0][0].shape[0]
    if tile is None:
        per_row = sum(w * a.dtype.itemsize for a, w, _ in rows) + sum(w * jnp.dtype(dt).itemsize for w, dt in out_rows)
        tile = 1024
        while tile > 16 and (tile * per_row * 2 > ROW_TILE_BUDGET or s % tile):
            tile //= 2
        if s * per_row * 2 <= ROW_TILE_BUDGET:
            tile = s
    assert s % tile == 0
    n_in = len(rows) + len(bcasts)
    n_o = len(out_rows)
    n_out = n_o + len(out_accs)
    p_ops, p_in_specs, p_out_shapes, p_out_specs, p_scratch, _ = _plan_call_parts(plan)
    n_pi, n_po = len(p_ops), len(p_out_shapes)
    n_steps = (s // tile,)

    def kern(*refs):
        plan_ins = refs[n_in:n_in + n_pi]
        outs = refs[n_in + n_pi:n_in + n_pi + n_out]
        plan_outs = refs[n_in + n_pi + n_out:n_in + n_pi + n_out + n_po]
        plan_sems = refs[n_in + n_pi + n_out + n_po:]
        _plan_phase(plan, "start", n_steps, plan_ins, plan_outs, plan_sems)
        _plan_phase(plan, "middle", n_steps, plan_ins, plan_outs, plan_sems)
        res = body(*[r[...].astype(F32) for r in refs[:n_in]])
        for o, v in zip(outs[:n_o], res[:n_o]):
            o[...] = v.astype(o.dtype)
        if out_accs:
            @pl.when(pl.program_id(0) == 0)
            def _():
                for o in outs[n_o:]:
                    o[...] = jnp.zeros_like(o)
            for o, v in zip(outs[n_o:], res[n_o:]):
                o[...] += v
        _plan_phase(plan, "end", n_steps, plan_ins, plan_outs, plan_sems)

    in_specs = [pl.BlockSpec((tile, w), functools.partial(lambda i, cb: (i, cb), cb=cb)) for _, w, cb in rows]
    in_specs += [pl.BlockSpec(b.shape, functools.partial(lambda i, nd: (0,) * nd, nd=b.ndim)) for b in bcasts]
    out_shape = [jax.ShapeDtypeStruct((s, w), dt) for w, dt in out_rows]
    out_shape += [jax.ShapeDtypeStruct((1, w), F32) for w in out_accs]
    out_specs = [pl.BlockSpec((tile, w), lambda i: (i, 0)) for w, _ in out_rows]
    out_specs += [pl.BlockSpec((1, w), lambda i: (0, 0)) for w in out_accs]
    return pl.pallas_call(
        kern, name=name, grid=n_steps, in_specs=in_specs + p_in_specs, out_specs=out_specs + p_out_specs,
        out_shape=out_shape + p_out_shapes, scratch_shapes=p_scratch,
        compiler_params=_params(("arbitrary",) if out_accs or plan is not None else ("parallel",)),
    )(*[a for a, _, _ in rows], *bcasts, *p_ops)


def _tile(n, pref):
    if n <= pref:
        return n
    t = (pref // LANES) * LANES
    while t >= LANES:
        if n % t == 0:
            return t
        t -= LANES
    return n


def _matmul(name, a, b, *, ta=False, tb=False, out_dtype=F32, bias=None, add=None, b_blocks=1, out_blocks=1,
            b_rows=None, out_rows=None, plan=None, tm=1024, tn=1408, tk=2816):
    m, k = (a.shape[1], a.shape[0]) if ta else a.shape
    if b_blocks > 1:
        rows_b, cols_b = b.shape[1], b.shape[2] * b_blocks
    else:
        rows_b, cols_b = b.shape
    if b_rows is not None:
        rows_b = b_rows[1]
    n, kb = (rows_b, cols_b) if tb else (cols_b, rows_b)
    assert k == kb, (name, a.shape, b.shape)
    tm, tn, tk = _tile(m, tm), _tile(n, tn), _tile(k, tk)
    if b_blocks > 1:
        per = cols_b // b_blocks
        if tb:
            tk = _tile(per, tk)
        else:
            tn = _tile(per, tn)
    if out_blocks > 1:
        tn = _tile(n // out_blocks, tn)
    nk = k // tk
    grid = (m // tm, n // tn, nk)

    a_spec = pl.BlockSpec((tk, tm), lambda i, j, kk: (kk, i)) if ta else pl.BlockSpec((tm, tk), lambda i, j, kk: (i, kk))
    if b_blocks > 1:
        if tb:
            nb = per // tk
            b_spec = pl.BlockSpec((1, tn, tk), lambda i, j, kk: (kk // nb, j, kk % nb))
        else:
            nb = per // tn
            b_spec = pl.BlockSpec((1, tk, tn), lambda i, j, kk: (j // nb, kk, j % nb))
    elif b_rows is not None:
        first = b_rows[0]
        if tb:
            b_spec = pl.BlockSpec((pl.Element(tn), pl.Element(tk)),
                                  lambda i, j, kk: (pl.multiple_of(first + j * tn, LANES), kk * tk))
        else:
            b_spec = pl.BlockSpec((pl.Element(tk), pl.Element(tn)),
                                  lambda i, j, kk: (pl.multiple_of(first + kk * tk, LANES), j * tn))
    else:
        b_spec = pl.BlockSpec((tn, tk), lambda i, j, kk: (j, kk)) if tb else pl.BlockSpec((tk, tn), lambda i, j, kk: (kk, j))
    in_specs = [a_spec, b_spec]
    operands = [a, b]
    if bias is not None:
        in_specs.append(pl.BlockSpec((1, tn), lambda i, j, kk: (0, j)))
        operands.append(bias)
    if add is not None:
        in_specs.append(pl.BlockSpec((tm, tn), lambda i, j, kk: (i, j)))
        operands.append(add)
    aliases = {}
    if out_blocks > 1:
        nbo = (n // out_blocks) // tn
        out_spec = pl.BlockSpec((1, tm, tn), lambda i, j, kk: (j // nbo, i, j % nbo))
        out_shape = jax.ShapeDtypeStruct((out_blocks, m, n // out_blocks), out_dtype)
    elif out_rows is not None:
        first_out, total, previous = out_rows
        out_spec = pl.BlockSpec((pl.Element(tm), pl.Element(tn)),
                                lambda i, j, kk: (pl.multiple_of(first_out + i * tm, LANES), j * tn))
        out_shape = jax.ShapeDtypeStruct((total, n), out_dtype)
        if previous is not None:
            aliases = {len(operands): 0}
            in_specs.append(pl.BlockSpec(memory_space=pl.ANY))
            operands.append(previous)
    else:
        out_spec = pl.BlockSpec((tm, tn), lambda i, j, kk: (i, j))
        out_shape = jax.ShapeDtypeStruct((m, n), out_dtype)
    dims = (((0 if ta else 1,), (1 if tb else 0,)), ((), ()))
    has_bias, has_add, has_previous = bias is not None, add is not None, bool(aliases)
    p_ops, p_in_specs, p_out_shapes, p_out_specs, p_scratch, _ = _plan_call_parts(plan)
    n_pi, n_po = len(p_ops), len(p_out_shapes)

    def kern(*refs):
        a_ref, b_ref = refs[0], refs[1]
        pos = 2
        bias_ref = add_ref = None
        if has_bias:
            bias_ref = refs[pos]
            pos += 1
        if has_add:
            add_ref = refs[pos]
            pos += 1
        if has_previous:
            pos += 1
        plan_ins = refs[pos:pos + n_pi]
        o_ref = refs[pos + n_pi]
        plan_outs = refs[pos + n_pi + 1:pos + n_pi + 1 + n_po]
        pos += n_pi + 1 + n_po
        acc_ref = refs[pos] if nk > 1 else None
        plan_sems = refs[pos + (1 if nk > 1 else 0):]
        _plan_phase(plan, "start", grid, plan_ins, plan_outs, plan_sems)
        av = a_ref[...].astype(BF16)
        bv = (b_ref[0] if b_blocks > 1 else b_ref[...]).astype(BF16)
        p = lax.dot_general(av, bv, dims, preferred_element_type=F32)

        def finish(acc):
            if has_bias:
                acc = acc + bias_ref[...]
            if has_add:
                acc = acc + add_ref[...]
            if out_blocks > 1:
                o_ref[0] = acc.astype(o_ref.dtype)
            else:
                o_ref[...] = acc.astype(o_ref.dtype)

        if nk == 1:
            finish(p)
        else:
            kk = pl.program_id(2)

            @pl.when(kk == 0)
            def _():
                acc_ref[...] = p

            @pl.when(kk > 0)
            def _():
                acc_ref[...] += p

            @pl.when(kk == nk - 1)
            def _():
                finish(acc_ref[...])
        _plan_phase(plan, "end", grid, plan_ins, plan_outs, plan_sems)

    res = pl.pallas_call(
        kern, name=name, grid=grid, in_specs=in_specs + p_in_specs, out_specs=[out_spec] + p_out_specs,
        out_shape=[out_shape] + p_out_shapes, input_output_aliases=aliases,
        scratch_shapes=([pltpu.VMEM((tm, tn), F32)] if nk > 1 else []) + p_scratch,
        compiler_params=_params(("parallel", "parallel", "arbitrary") if plan is None else ("arbitrary",) * 3),
    )(*operands, *p_ops)
    return res[0] if plan is None else (res[0], list(res[1:]))


CONV_HALO = 16
CONV_ROWS = 256


def _taps(win, shifts, rows):
    n = win.shape[0]
    for j, s in enumerate(shifts):
        yield j, (pltpu.roll(win, (n - s) % n, axis=0) if s % n else win)[:rows]


def _dwconv_fwd(name, x, w, b, *, silu, out_dtype=F32, plan=None):
    s, c = x.shape
    k = w.shape[0]
    pad = (k - 1) // 2
    assert pad <= CONV_HALO and c % LANES == 0
    t = min(CONV_ROWS, s)
    n_chunks = s // t
    fwd_shifts = [CONV_HALO - pad + j for j in range(k)]
    p_ops, p_in_specs, p_out_shapes, p_out_specs, p_scratch, _ = _plan_call_parts(plan)
    n_pi, n_po = len(p_ops), len(p_out_shapes)
    n_steps = (c // LANES,)

    def kern(*refs):
        x_ref, w_ref, b_ref = refs[:3]
        plan_ins = refs[3:3 + n_pi]
        o_ref = refs[3 + n_pi]
        plan_outs = refs[4 + n_pi:4 + n_pi + n_po]
        xp_ref = refs[4 + n_pi + n_po]
        plan_sems = refs[5 + n_pi + n_po:]
        _plan_phase(plan, "start", n_steps, plan_ins, plan_outs, plan_sems)
        _plan_phase(plan, "middle", n_steps, plan_ins, plan_outs, plan_sems)
        zeros = jnp.zeros((CONV_HALO, LANES), F32)
        xp_ref[0:CONV_HALO, :] = zeros
        xp_ref[CONV_HALO + s:CONV_HALO + s + CONV_HALO, :] = zeros
        xp_ref[CONV_HALO:CONV_HALO + s, :] = x_ref[...].astype(F32)
        bv = b_ref[...]

        def chunk(i, carry):
            base = pl.multiple_of(i * t, 16)
            win = xp_ref[pl.ds(base, t + 2 * CONV_HALO), :]
            acc = jnp.zeros((t, LANES), F32)
            for j, xs in _taps(win, fwd_shifts, t):
                acc = acc + xs * w_ref[pl.ds(j, 1), :]
            acc = acc + bv
            o_ref[pl.ds(base, t), :] = (_silu(acc) if silu else acc).astype(o_ref.dtype)
            return carry

        lax.fori_loop(0, n_chunks, chunk, 0)
        _plan_phase(plan, "end", n_steps, plan_ins, plan_outs, plan_sems)

    res = pl.pallas_call(
        kern, name=name, grid=n_steps,
        in_specs=[pl.BlockSpec((s, LANES), lambda i: (0, i)), pl.BlockSpec((k, LANES), lambda i: (0, i)),
                  pl.BlockSpec((1, LANES), lambda i: (0, i))] + p_in_specs,
        out_specs=[pl.BlockSpec((s, LANES), lambda i: (0, i))] + p_out_specs,
        out_shape=[jax.ShapeDtypeStruct((s, c), out_dtype)] + p_out_shapes,
        scratch_shapes=[pltpu.VMEM((s + 2 * CONV_HALO, LANES), F32)] + p_scratch,
        compiler_params=_params(("parallel",) if plan is None else ("arbitrary",)),
    )(x, w, b, *p_ops)
    return res[0] if plan is None else (res[0], list(res[1:]))


def _dwconv_bwd(name, x, w, b, dout, *, silu, dx_dtype=F32):
    s, c = x.shape
    k = w.shape[0]
    pad = (k - 1) // 2
    t = min(CONV_ROWS, s)
    n_chunks = s // t
    fwd_shifts = [CONV_HALO - pad + j for j in range(k)]
    bwd_shifts = [CONV_HALO + pad - j for j in range(k)]

    def kern(x_ref, w_ref, b_ref, do_ref, dx_ref, dw_ref, db_ref, xp_ref, dp_ref):
        zeros = jnp.zeros((CONV_HALO, LANES), F32)
        for ref in (xp_ref, dp_ref):
            ref[0:CONV_HALO, :] = zeros
            ref[CONV_HALO + s:CONV_HALO + s + CONV_HALO, :] = zeros
        xp_ref[CONV_HALO:CONV_HALO + s, :] = x_ref[...].astype(F32)
        bv = b_ref[...]
        dw_ref[...] = jnp.zeros_like(dw_ref)

        def pre_chunk(i, dbias):
            base = pl.multiple_of(i * t, 16)
            win = xp_ref[pl.ds(base, t + 2 * CONV_HALO), :]
            dpre = do_ref[pl.ds(base, t), :].astype(F32)
            if silu:
                acc = jnp.zeros((t, LANES), F32)
                for j, xs in _taps(win, fwd_shifts, t):
                    acc = acc + xs * w_ref[pl.ds(j, 1), :]
                dpre = dpre * _dsilu(acc + bv)
            dp_ref[pl.ds(base + CONV_HALO, t), :] = dpre
            for j, xs in _taps(win, fwd_shifts, t):
                dw_ref[pl.ds(j, 1), :] += _sum0(dpre * xs)
            return dbias + _sum0(dpre)

        db_ref[...] = lax.fori_loop(0, n_chunks, pre_chunk, jnp.zeros((1, LANES), F32))

        def dx_chunk(i, carry):
            base = pl.multiple_of(i * t, 16)
            win = dp_ref[pl.ds(base, t + 2 * CONV_HALO), :]
            acc = jnp.zeros((t, LANES), F32)
            for j, dps in _taps(win, bwd_shifts, t):
                acc = acc + dps * w_ref[pl.ds(j, 1), :]
            dx_ref[pl.ds(base, t), :] = acc.astype(dx_ref.dtype)
            return carry

        lax.fori_loop(0, n_chunks, dx_chunk, 0)

    col = lambda rows: pl.BlockSpec((rows, LANES), lambda i: (0, i))
    return pl.pallas_call(
        kern, name=name, grid=(c // LANES,),
        in_specs=[col(s), col(k), col(1), col(s)],
        out_specs=[col(s), col(k), col(1)],
        out_shape=[jax.ShapeDtypeStruct((s, c), dx_dtype), jax.ShapeDtypeStruct((k, c), F32),
                   jax.ShapeDtypeStruct((1, c), F32)],
        scratch_shapes=[pltpu.VMEM((s + 2 * CONV_HALO, LANES), F32), pltpu.VMEM((s + 2 * CONV_HALO, LANES), F32)],
        compiler_params=_params(("parallel",)),
    )(x, w, b, dout)


_NT =(((1,), (1,)), ((), ()))
_TN = (((0,), (0,)), ((), ()))


def _dot(a, b, dims=None):
    if dims is None:
        return jnp.dot(a, b, preferred_element_type=F32)
    return lax.dot_general(a, b, dims, preferred_element_type=F32)


HEAD_QUANTITIES = 4
GROUPS_PER_STEP = 8


def _scan_tables(n_heads):
    j_heads = n_heads // N_GROUPS
    used = 3 * HEAD_QUANTITIES * j_heads
    assert used <= LANES and j_heads % 2 == 0 and N_GROUPS % GROUPS_PER_STEP == 0
    gw = j_heads * HEAD_DIM
    r = jnp.arange(LANES)[:, None]

    def expand(quantity, width):
        head_of_lane = jnp.arange(j_heads * width)[None] // width
        return ((r // (3 * j_heads) == quantity) & (r % j_heads == head_of_lane) & (r < used)).astype(BF16)

    sel_cols = (jnp.arange(gw)[:, None] // HEAD_DIM == jnp.arange(LANES)[None]).astype(BF16)
    h2 = 2 * n_heads
    rows = jnp.arange(3 * HEAD_QUANTITIES * h2)
    head = rows % n_heads
    col = ((rows % h2) // n_heads * N_GROUPS + head // j_heads) * LANES + (rows // h2) * j_heads + head % j_heads
    route = (col[:, None] == jnp.arange(2 * N_GROUPS * LANES)[None]).astype(BF16)
    return {"ex_dt": expand(0, HEAD_DIM), "ex_gam": expand(1, CHUNK), "ex_din": expand(2, HEAD_DIM),
            "ex_dst": expand(3, HEAD_DIM), "sel_cols": sel_cols, "route": route}


def _chunk_decay(etot, n_heads, direction):
    j_heads = n_heads // N_GROUPS
    ed = etot[:, :, direction * n_heads:(direction + 1) * n_heads]
    per_group = jnp.pad(ed.reshape(ed.shape[0], 8, N_GROUPS, j_heads), ((0, 0), (0, 0), (0, 0), (0, LANES - j_heads)))
    return jnp.repeat(ed, HEAD_DIM, axis=2), per_group.reshape(ed.shape[0], 8, N_GROUPS * LANES)


def _scan_specs(reverse_order, direction, nc, j_heads, d_ssm):
    gps = GROUPS_PER_STEP
    gw = j_heads * HEAD_DIM
    b_off = d_ssm // (gps * D_STATE)
    c_off = b_off + N_GROUPS // gps
    d_off = direction * (N_GROUPS // gps)
    zz = (lambda z: nc - 1 - z) if reverse_order else (lambda z: z)
    const = lambda shape: pl.BlockSpec(shape, lambda g, z: (0,) * len(shape))
    return {
        "xs": pl.BlockSpec((CHUNK, gps * gw), lambda g, z: (zz(z), g)),
        "b": pl.BlockSpec((CHUNK, gps * D_STATE), lambda g, z: (zz(z), b_off + g)),
        "c": pl.BlockSpec((CHUNK, gps * D_STATE), lambda g, z: (zz(z), c_off + g)),
        "q": pl.BlockSpec((CHUNK, gps * LANES), lambda g, z: (zz(z), d_off + g)),
        "gam_t": pl.BlockSpec((gps * j_heads, CHUNK), lambda g, z: (d_off + g, zz(z))),
        "etot_x": pl.BlockSpec((1, 8, gps * gw), lambda g, z: (zz(z), 0, g)),
        "etot_g": pl.BlockSpec((1, 8, gps * LANES), lambda g, z: (zz(z), 0, g)),
        "state": pl.BlockSpec((gps, 1, D_STATE, gw), lambda g, z: (g, zz(z), 0, 0)),
        "grp": pl.BlockSpec((CHUNK, gps * D_STATE), lambda g, z: (zz(z), g)),
        "ex": const((LANES, gw)), "ex_gam": const((LANES, j_heads * CHUNK)), "sel": const((gw, LANES)),
    }


def _scan_masks(reverse):
    li = lax.broadcasted_iota(jnp.int32, (CHUNK, CHUNK), 0)
    si = lax.broadcasted_iota(jnp.int32, (CHUNK, CHUNK), 1)
    mask = (li <= si) if reverse else (li >= si)
    mask_t = (si <= li) if reverse else (si >= li)
    return li, si, mask, mask_t, si < HEAD_DIM


def _plan_phase(plan, phase, n_steps, ins, outs, sems):
    if plan is None or phase not in plan.phases:
        return
    ids = [pl.program_id(i) for i in range(len(n_steps))]
    first = {"start": 0, "middle": n_steps[0] // 2, "end": n_steps[0] - 1}[phase]
    when = ids[0] == first
    for i, n in zip(ids[1:], n_steps[1:]):
        when = when & (i == (n - 1 if phase == "end" else 0))

    @pl.when(when)
    def _():
        plan.phases[phase](ins, outs, *sems)


def _plan_call_parts(plan):
    if plan is None:
        return [], [], [], [], [], ("parallel", "arbitrary")
    n_in, n_out = len(plan.operands), len(plan.out_shapes)
    return plan.operands, [ANY] * n_in, plan.out_shapes, [ANY] * n_out, plan.sems(), ("arbitrary", "arbitrary")


def _scan_fwd(name, xbc_c, q_all, gam_t, etot_x, tb, *, direction, d_ssm, plan=None):
    s = xbc_c.shape[0]
    nc = s // CHUNK
    j_heads = tb["ex_dt"].shape[1] // HEAD_DIM
    gw = j_heads * HEAD_DIM
    gps = GROUPS_PER_STEP
    reverse = direction == 1
    sp = _scan_specs(reverse, direction, nc, j_heads, d_ssm)
    p_ops, p_in_specs, p_out_shapes, p_out_specs, p_scratch, semantics = _plan_call_parts(plan)
    n_pi, n_po = len(p_ops), len(p_out_shapes)
    n_steps = (N_GROUPS // gps, nc)

    def kern(*refs):
        xs_ref, b_ref, c_ref, q_ref, gamt_ref, etx_ref, exdt_ref, exgam_ref, exdin_ref, exdst_ref = refs[:10]
        plan_ins = refs[10:10 + n_pi]
        y_ref, hs_ref = refs[10 + n_pi:12 + n_pi]
        plan_outs = refs[12 + n_pi:12 + n_pi + n_po]
        h_ref = refs[12 + n_pi + n_po]
        plan_sems = refs[13 + n_pi + n_po:]
        _plan_phase(plan, "start", n_steps, plan_ins, plan_outs, plan_sems)
        _plan_phase(plan, "middle", n_steps, plan_ins, plan_outs, plan_sems)

        @pl.when(pl.program_id(1) == 0)
        def _():
            h_ref[...] = jnp.zeros_like(h_ref)

        _, _, mask, _, lo = _scan_masks(reverse)
        for gi in range(gps):
            bb = b_ref[:, gi * D_STATE:(gi + 1) * D_STATE].astype(BF16)
            cb = c_ref[:, gi * D_STATE:(gi + 1) * D_STATE].astype(BF16)
            cbt = _dot(cb, bb, _NT)
            q = q_ref[:, gi * LANES:(gi + 1) * LANES]
            dtx, dinx, dstx = _dot(q, exdt_ref[...]), _dot(q, exdin_ref[...]), _dot(q, exdst_ref[...])
            gcol = _dot(q, exgam_ref[...])
            xdt = xs_ref[:, gi * gw:(gi + 1) * gw].astype(F32) * dtx
            ht = h_ref[gi]
            y_off = _dot(cb, ht.astype(BF16)) * dinx
            hs_ref[gi, 0] = ht
            for p in range(j_heads // 2):
                lanes = slice(p * CHUNK, (p + 1) * CHUNK)
                x2 = xdt[:, lanes]
                acc = y_off[:, lanes]
                for idx, j in enumerate((2 * p, 2 * p + 1)):
                    g_row = gamt_ref[pl.ds(gi * j_heads + j, 1), :]
                    decay = jnp.exp(jnp.where(mask, gcol[:, j * CHUNK:(j + 1) * CHUNK] - g_row, NEG))
                    x_head = jnp.where(lo if idx == 0 else jnp.logical_not(lo), x2, 0.0).astype(BF16)
                    acc = acc + _dot((cbt * decay).astype(BF16), x_head)
                y_ref[:, gi * gw + p * CHUNK:gi * gw + (p + 1) * CHUNK] = acc.astype(y_ref.dtype)
            h_ref[gi] = ht * etx_ref[0, 0:1, gi * gw:(gi + 1) * gw] + _dot(bb, (xdt * dstx).astype(BF16), _TN)
        _plan_phase(plan, "end", n_steps, plan_ins, plan_outs, plan_sems)

    res = pl.pallas_call(
        kern, name=name, grid=n_steps,
        in_specs=[sp["xs"], sp["b"], sp["c"], sp["q"], sp["gam_t"], sp["etot_x"], sp["ex"], sp["ex_gam"], sp["ex"],
                  sp["ex"]] + p_in_specs,
        out_specs=[sp["xs"], sp["state"]] + p_out_specs,
        out_shape=[jax.ShapeDtypeStruct((s, d_ssm), BF16),
                   jax.ShapeDtypeStruct((N_GROUPS, nc, D_STATE, gw), F32)] + p_out_shapes,
        scratch_shapes=[pltpu.VMEM((gps, D_STATE, gw), F32)] + p_scratch,
        compiler_params=_params(semantics),
    )(xbc_c, xbc_c, xbc_c, q_all, gam_t, etot_x, tb["ex_dt"], tb["ex_gam"], tb["ex_din"], tb["ex_dst"], *p_ops)
    return res[0], res[1], list(res[2:])


def _scan_bwd(name, xbc_c, dy, hs, q_all, gam_t, etot_x, etot_g, tb, *, direction, d_ssm, plan=None):
    s = xbc_c.shape[0]
    nc = s // CHUNK
    j_heads = tb["ex_dt"].shape[1] // HEAD_DIM
    gw = j_heads * HEAD_DIM
    gps = GROUPS_PER_STEP
    reverse = direction == 1
    sp = _scan_specs(not reverse, direction, nc, j_heads, d_ssm)
    hp = lax.Precision.HIGHEST
    p_ops, p_in_specs, p_out_shapes, p_out_specs, p_scratch, semantics = _plan_call_parts(plan)
    n_pi, n_po = len(p_ops), len(p_out_shapes)
    n_steps = (N_GROUPS // gps, nc)

    def kern(*refs):
        (xs_ref, b_ref, c_ref, dy_ref, hs_ref, q_ref, gamt_ref, etg_ref, etx_ref, exdt_ref, exgam_ref, exdin_ref,
         exdst_ref, sel_ref) = refs[:14]
        plan_ins = refs[14:14 + n_pi]
        dxs_ref, db_ref, dc_ref, ddt_ref, da_ref = refs[14 + n_pi:19 + n_pi]
        plan_outs = refs[19 + n_pi:19 + n_pi + n_po]
        dh_ref, tmp_ref = refs[19 + n_pi + n_po:21 + n_pi + n_po]
        plan_sems = refs[21 + n_pi + n_po:]
        _plan_phase(plan, "start", n_steps, plan_ins, plan_outs, plan_sems)
        _plan_phase(plan, "middle", n_steps, plan_ins, plan_outs, plan_sems)

        @pl.when(pl.program_id(1) == 0)
        def _():
            dh_ref[...] = jnp.zeros_like(dh_ref)

        li, si, mask, mask_t, lo = _scan_masks(reverse)
        sel = sel_ref[...]
        incl = ((si <= li) if reverse else (si >= li)).astype(F32)
        excl = ((si > li) if reverse else (si < li)).astype(F32)
        for gi in range(gps):
            grp_lanes = slice(gi * D_STATE, (gi + 1) * D_STATE)
            bb = b_ref[:, grp_lanes].astype(BF16)
            cb = c_ref[:, grp_lanes].astype(BF16)
            cbt = _dot(cb, bb, _NT)
            cbt_t = _dot(bb, cb, _NT)
            q = q_ref[:, gi * LANES:(gi + 1) * LANES]
            dtx, dinx, dstx = _dot(q, exdt_ref[...]), _dot(q, exdin_ref[...]), _dot(q, exdst_ref[...])
            gcol = _dot(q, exgam_ref[...])
            x_all = xs_ref[:, gi * gw:(gi + 1) * gw].astype(F32)
            dy_all = dy_ref[:, gi * gw:(gi + 1) * gw].astype(F32)
            xdt = x_all * dtx
            xb = xdt.astype(BF16)
            ht = hs_ref[gi, 0]
            hb = ht.astype(BF16)
            dht = dh_ref[gi]
            dhb = dht.astype(BF16)
            y_off = _dot(cb, hb) * dinx
            dx_off = _dot(bb, dhb) * dstx
            dyd = (dy_all * dinx).astype(BF16)
            xd = (xdt * dstx).astype(BF16)
            dc_acc = _dot(dyd, hb, _NT)
            db_acc = _dot(xd, dhb, _NT)
            dh_ref[gi] = dht * etx_ref[0, 0:1, gi * gw:(gi + 1) * gw] + _dot(cb, dyd, _TN)
            q_cols = _dot((dy_all * y_off).astype(BF16), sel)
            c_cols = _dot((xdt * dx_off).astype(BF16), sel)
            through = _sum0(_dot((dht * ht).astype(BF16), sel))
            dcbt = jnp.zeros((CHUNK, CHUNK), F32)
            for p in range(j_heads // 2):
                lanes = slice(p * CHUNK, (p + 1) * CHUNK)
                out_lanes = slice(gi * gw + p * CHUNK, gi * gw + (p + 1) * CHUNK)
                x2b = xb[:, lanes]
                dy2 = dy_all[:, lanes]
                acc = dx_off[:, lanes]
                for idx, j in enumerate((2 * p, 2 * p + 1)):
                    gc = gcol[:, j * CHUNK:(j + 1) * CHUNK]
                    gr = gamt_ref[pl.ds(gi * j_heads + j, 1), :]
                    decay = jnp.exp(jnp.where(mask, gc - gr, NEG))
                    decay_t = jnp.exp(jnp.where(mask_t, gr - gc, NEG))
                    dy_head = jnp.where(lo if idx == 0 else jnp.logical_not(lo), dy2, 0.0).astype(BF16)
                    acc = acc + _dot((cbt_t * decay_t).astype(BF16), dy_head)
                    dm = decay * _dot(dy_head, x2b, _NT)
                    dcbt = dcbt + dm
                    e = (cbt * dm).astype(BF16)
                    in_lane_j = si == j
                    q_cols = (q_cols + jnp.where(in_lane_j, jnp.sum(e.astype(F32), axis=1, keepdims=True), 0.0)
                              - _dot(e, jnp.where(in_lane_j, 1.0, 0.0).astype(BF16), _TN))
                dxs_ref[:, out_lanes] = (acc * dtx[:, lanes]).astype(dxs_ref.dtype)
                tmp_ref[:, lanes] = acc * x_all[:, lanes]
            dcb = dcbt.astype(BF16)
            dc_ref[:, grp_lanes] = (dc_acc + _dot(dcb, bb)).astype(dc_ref.dtype)
            db_ref[:, grp_lanes] = (db_acc + _dot(dcb, cb, _TN)).astype(db_ref.dtype)
            ddt_ref[:, gi * LANES:(gi + 1) * LANES] = _dot(tmp_ref[...].astype(BF16), sel)
            da_ref[:, gi * LANES:(gi + 1) * LANES] = (
                jnp.dot(incl, q_cols, preferred_element_type=F32, precision=hp)
                + jnp.dot(excl, c_cols, preferred_element_type=F32, precision=hp)
                + through * etg_ref[0, 0:1, gi * LANES:(gi + 1) * LANES])
        _plan_phase(plan, "end", n_steps, plan_ins, plan_outs, plan_sems)

    gn = N_GROUPS * D_STATE
    res = pl.pallas_call(
        kern, name=name, grid=n_steps,
        in_specs=[sp["xs"], sp["b"], sp["c"], sp["xs"], sp["state"], sp["q"], sp["gam_t"], sp["etot_g"], sp["etot_x"],
                  sp["ex"], sp["ex_gam"], sp["ex"], sp["ex"], sp["sel"]] + p_in_specs,
        out_specs=[sp["xs"], sp["grp"], sp["grp"], sp["grp"], sp["grp"]] + p_out_specs,
        out_shape=[jax.ShapeDtypeStruct((s, d_ssm), BF16), jax.ShapeDtypeStruct((s, gn), BF16),
                   jax.ShapeDtypeStruct((s, gn), BF16), jax.ShapeDtypeStruct((s, gn), F32),
                   jax.ShapeDtypeStruct((s, gn), F32)] + p_out_shapes,
        scratch_shapes=[pltpu.VMEM((gps, D_STATE, gw), F32), pltpu.VMEM((CHUNK, gw), F32)] + p_scratch,
        compiler_params=_params(semantics),
    )(xbc_c, xbc_c, xbc_c, dy, hs, q_all, gam_t, etot_g, etot_x, tb["ex_dt"], tb["ex_gam"], tb["ex_din"], tb["ex_dst"],
      tb["sel_cols"], *p_ops)
    return res[:5], list(res[5:])


def _dt_prepare(dt_raw, dt_bias, a_neg, route):
    s, h2 = dt_raw.shape
    n_heads = h2 // 2
    qw = route.shape[1]

    def kern(raw_ref, bias_ref, a_ref, route_ref, dt_ref, q_ref, gamt_ref, etot_ref):
        dt = _softplus(raw_ref[...] + bias_ref[...])
        a = dt * a_ref[...]
        li = lax.broadcasted_iota(jnp.int32, (CHUNK, CHUNK), 0)
        si = lax.broadcasted_iota(jnp.int32, (CHUNK, CHUNK), 1)
        tri = (li >= si).astype(F32)
        cs = jnp.dot(tri, a, preferred_element_type=F32, precision=lax.Precision.HIGHEST)
        tot = _sum0(a)
        fwd = lax.broadcasted_iota(jnp.int32, (CHUNK, h2), 1) < n_heads
        gam = jnp.where(fwd, cs, a - cs)
        din = jnp.where(fwd, jnp.exp(cs), jnp.exp(tot + gam))
        dst = jnp.where(fwd, jnp.exp(tot - cs), jnp.exp(cs - a))
        pieces = []
        for v in (dt, gam, din, dst):
            hi = v.astype(BF16)
            rest = v - hi.astype(F32)
            mid = rest.astype(BF16)
            pieces += [hi, mid, (rest - mid.astype(F32)).astype(BF16)]
        q_ref[...] = _dot(jnp.concatenate(pieces, axis=1), route_ref[...]).astype(BF16)
        dt_ref[...] = dt
        gamt_ref[...] = gam.T
        etot_ref[...] = jnp.broadcast_to(jnp.exp(tot), (8, h2))

    nc = s // CHUNK
    rows = lambda w: pl.BlockSpec((CHUNK, w), lambda i: (i, 0))
    whole = lambda a: pl.BlockSpec(a.shape, lambda i: (0, 0))
    return pl.pallas_call(
        kern, name="dt_prepare", grid=(nc,),
        in_specs=[rows(h2), whole(dt_bias), whole(a_neg), whole(route)],
        out_specs=[rows(h2), rows(qw), pl.BlockSpec((h2, CHUNK), lambda i: (0, i)), pl.BlockSpec((8, h2), lambda i: (i, 0))],
        out_shape=[jax.ShapeDtypeStruct((s, h2), F32), jax.ShapeDtypeStruct((s, qw), BF16),
                   jax.ShapeDtypeStruct((h2, s), F32), jax.ShapeDtypeStruct((nc * 8, h2), F32)],
        compiler_params=_params(("parallel",)),
    )(dt_raw, dt_bias, a_neg, route)


def _dt_backward(da_dirs, ddt_dirs, dt, dt_raw, dt_bias, a_neg):
    h2 = dt.shape[1]
    n_heads = h2 // 2
    j_heads = n_heads // N_GROUPS
    lane = jnp.arange(N_GROUPS * LANES)[:, None]
    head = (lane // LANES) * j_heads + lane % LANES
    pick = [((lane % LANES < j_heads) & (head + direction * n_heads == jnp.arange(h2)[None])).astype(BF16)
            for direction in (0, 1)]

    def compact(wide_f, wide_r, pick_f, pick_r):
        total = 0.0
        for wide, sel in ((wide_f, pick_f), (wide_r, pick_r)):
            hi = wide.astype(BF16)
            total = total + _dot(hi, sel.astype(BF16)) + _dot((wide - hi.astype(F32)).astype(BF16), sel.astype(BF16))
        return total

    def body(da_f, da_r, ddt_f, ddt_r, dtv, raw, pick_f, pick_r, bias, a_head):
        dav = compact(da_f, da_r, pick_f, pick_r)
        ddtv = compact(ddt_f, ddt_r, pick_f, pick_r)
        draw = (ddtv + dav * a_head) * _sigmoid(raw + bias)
        return draw, _sum0(draw), _sum0(dav * dtv) * a_head
    return _rowwise("dt_backward", body, [*da_dirs, *ddt_dirs, dt, dt_raw], [*pick, dt_bias, a_neg], [(h2, BF16)], [h2, h2])


def _rms(x):
    r = lax.rsqrt(_mean1(x * x) + EPS)
    return x * r, r


def _rms_bwd(dy, y, r):
    return r * (dy - y * _mean1(dy * y))


def _norm_mod_fwd(name, x, g, sc, sh):
    def body(xv, gv, scv, shv):
        y, _ = _rms(xv)
        return ((y * gv) * (1.0 + scv) + shv,)
    return _rowwise(name, body, [x], [g, sc, sh], [(x.shape[1], BF16)])[0]


def _norm_mod_bwd(name, x, dh, dpass, g, sc, plan=None):
    d = x.shape[1]

    def body(xv, dhv, dpv, gv, scv):
        y, r = _rms(xv)
        dn = dhv * (1.0 + scv)
        dx = _rms_bwd(dn * gv, y, r) + dpv
        return dx, _sum0(dn * y), _sum0(dhv * (y * gv)), _sum0(dhv)
    return _rowwise(name, body, [x, dh, dpass], [g, sc], [(d, F32)], [d, d, d], plan=plan)


def _gated_residual_fwd(name, x, m, gate, gp):
    def body(xv, mv, gatev, gpv):
        y, _ = _rms(mv)
        return (xv + gatev * (y * gpv),)
    return _rowwise(name, body, [x, m], [gate, gp], [(x.shape[1], F32)])[0]


def _gated_residual_bwd(name, m, dx1, gate, gp):
    d = m.shape[1]

    def body(mv, dv, gatev, gpv):
        y, r = _rms(mv)
        dn = dv * gatev
        return _rms_bwd(dn * gpv, y, r), _sum0(dv * (y * gpv)), _sum0(dn * y)
    return _rowwise(name, body, [m, dx1], [gate, gp], [(d, BF16)], [d, d])


def _final_residual_loss(x1, f, tgt, gate, gp):
    d = x1.shape[1]

    def body(xv, fv, tv, gatev, gpv):
        y, r = _rms(fv)
        n = y * gpv
        err = xv + gatev * n - tv
        dx2 = err * (1.0 / d)
        dn = dx2 * gatev
        sq = jnp.sum(_sum0(err * err), axis=1, keepdims=True)
        return dx2, _rms_bwd(dn * gpv, y, r), jnp.broadcast_to(sq, (1, LANES)), _sum0(dx2 * n), _sum0(dn * y)
    return _rowwise("final_residual_loss", body, [x1, f, tgt], [gate, gp], [(d, F32), (d, BF16)], [LANES, d, d])


def _swiglu_fwd(gu):
    f = gu.shape[1] // 2

    def body(v):
        return (_silu(v[:, :f]) * v[:, f:],)
    return _rowwise("swiglu_fwd", body, [gu], [], [(f, BF16)])[0]


def _swiglu_bwd(gu, dact):
    f = gu.shape[1] // 2

    def body(v, dv):
        gt, up = v[:, :f], v[:, f:]
        return (jnp.concatenate([dv * up * _dsilu(gt), dv * _silu(gt)], axis=1),)
    return _rowwise("swiglu_bwd", body, [gu, dact], [], [(2 * f, BF16)])[0]


def _glu_fwd(glu_in, b_glu):
    c = glu_in.shape[1] // 2

    def body(v, bv):
        t = v + bv
        return (t[:, :c] * _sigmoid(t[:, c:]),)
    return _rowwise("glu_fwd", body, [glu_in], [b_glu], [(c, F32)])[0]


def _glu_bwd(glu_in, b_glu, du0):
    c = glu_in.shape[1] // 2

    def body(v, dv, bv):
        t = v + bv
        a, s = t[:, :c], _sigmoid(t[:, c:])
        dg = jnp.concatenate([dv * s, dv * a * s * (1.0 - s)], axis=1)
        return dg, _sum0(dg)
    return _rowwise("glu_bwd", body, [glu_in, du0], [b_glu], [(2 * c, BF16)], [2 * c])


def _ln_parts(u1):
    xc = u1 - _mean1(u1)
    r = lax.rsqrt(_mean1(xc * xc) + EPS)
    return xc * r, r


def _ln_silu_fwd(u1, ln_g, ln_b):
    def body(v, gv, bv):
        yh, _ = _ln_parts(v)
        return (_silu(yh * gv + bv),)
    return _rowwise("ln_silu_fwd", body, [u1], [ln_g, ln_b], [(u1.shape[1], BF16)])[0]


def _ln_silu_bwd(u1, du, ln_g, ln_b):
    d = u1.shape[1]

    def body(v, dv, gv, bv):
        yh, r = _ln_parts(v)
        dl = dv * _dsilu(yh * gv + bv)
        dyh = dl * gv
        du1 = r * (dyh - _mean1(dyh) - yh * _mean1(dyh * yh))
        return du1, _sum0(dl * yh), _sum0(dl)
    return _rowwise("ln_silu_bwd", body, [u1, du], [ln_g, ln_b], [(d, F32)], [d, d])


def _gate_merge_fwd(y_a, y_b, gl, b_gate):
    d = y_a.shape[1]

    def body(ya, yb, glv, bv):
        s = _sigmoid(glv + bv)
        return (s[:, :d] * ya + s[:, d:] * yb,)
    return _rowwise("gate_merge_fwd", body, [y_a, y_b, gl], [b_gate], [(d, BF16)])[0]


def _gate_merge_bwd(dmixin, y_a, y_b, gl, b_gate):
    d = y_a.shape[1]

    def body(dv, ya, yb, glv, bv):
        s = _sigmoid(glv + bv)
        sa, sb = s[:, :d], s[:, d:]
        dya, dyb = dv * sa, dv * sb
        dgl = jnp.concatenate([dv * ya * sa * (1.0 - sa), dv * yb * sb * (1.0 - sb)], axis=1)
        return dya, dyb, dgl, _sum0(dgl), _sum0(dyb)
    return _rowwise("gate_merge_bwd", body, [dmixin, y_a, y_b, gl], [b_gate],
                    [(d, BF16), (d, BF16), (2 * d, BF16)], [2 * d, d])


def _group_slices(d_ssm):
    gw = d_ssm // N_GROUPS
    return [slice(g * gw, (g + 1) * gw) for g in range(N_GROUPS)]


def _gated_norm_fwd(y_f, y_b, xbc_c, z, d_skip_x, g_ssm):
    d_ssm = y_f.shape[1]

    def body(yf, yb, xs, zv, dsk, gv):
        y = yf + yb + dsk * xs
        v = y * _silu(zv)
        outs = []
        for sl in _group_slices(d_ssm):
            w, _ = _rms(v[:, sl])
            outs.append(w)
        return y, jnp.concatenate(outs, axis=1) * gv
    return _rowwise("gated_norm_fwd", body, [y_f, y_b, (xbc_c, d_ssm, 0), z], [d_skip_x, g_ssm],
                    [(d_ssm, BF16), (d_ssm, BF16)])


def _gated_norm_bwd(y, z, dyn, xbc_c, d_skip_x, g_ssm):
    d_ssm = y.shape[1]

    def body(yv, zv, dv, xs, dsk, gv):
        sz = _silu(zv)
        v = yv * sz
        dw = dv * gv
        dvs, ws = [], []
        for sl in _group_slices(d_ssm):
            w, r = _rms(v[:, sl])
            ws.append(w)
            dvs.append(_rms_bwd(dw[:, sl], w, r))
        dvv = jnp.concatenate(dvs, axis=1)
        dy = dvv * sz
        return dy, dvv * yv * _dsilu(zv), _sum0(dv * jnp.concatenate(ws, axis=1)), _sum0(dy * xs)
    return _rowwise("gated_norm_bwd", body, [y, z, dyn, (xbc_c, d_ssm, 0)], [d_skip_x, g_ssm],
                    [(d_ssm, BF16), (d_ssm, BF16)], [d_ssm, d_ssm])


def _ssd_grad_merge(dxs_f, dxs_b, dy, db_f, db_b, dc_f, dc_b, d_skip_x):
    d_ssm = dy.shape[1]
    width = d_ssm + 2 * N_GROUPS * D_STATE

    def body(xf, xb, dv, bf, bb, cf, cbv, dsk):
        return (jnp.concatenate([xf + xb + dsk * dv, bf + bb, cf + cbv], axis=1),)
    return _rowwise("ssd_grad_merge", body, [dxs_f, dxs_b, dy, db_f, db_b, dc_f, dc_b], [d_skip_x], [(width, BF16)])[0]


def _adamw(name, w, g, m, v):
    c = w.shape[1]
    c1 = 1.0 - ADAM_B1 ** ADAM_STEP
    c2 = 1.0 - ADAM_B2 ** ADAM_STEP

    def body(wv, gv, mv, vv):
        mn = ADAM_B1 * mv + (1.0 - ADAM_B1) * gv
        vn = ADAM_B2 * vv + (1.0 - ADAM_B2) * (gv * gv)
        delta = -ADAM_LR * ((mn / c1) / (jnp.sqrt(vn / c2) + ADAM_EPS) + ADAM_WD * wv)
        return delta, mn, vn
    return _rowwise(name, body, [w, g, m, v], [], [(c, F32)] * 3)


def _local_step(x, tgt, mod, wts, sm, late=None):
    s, d = x.shape
    d_ssm = 2 * d
    n_heads = d_ssm // HEAD_DIM
    d_xbc = d_ssm + 2 * N_GROUPS * D_STATE
    sec = [0, d_ssm, d_ssm + d_xbc, d_ssm + d_xbc + 2 * n_heads, d_ssm + d_xbc + 2 * n_heads + 2 * d]
    sec.append(sec[-1] + 2 * d)
    sh1, sc1, g1, sh2, sc2, g2 = [mod[:, i * d:(i + 1) * d] for i in range(N_MOD)]
    win_t = wts["w_in_t"]
    sections = [(nm, (sec[i], sec[i + 1] - sec[i])) for i, nm in enumerate(("z", "xbc", "dt", "glu", "gate"))]

    h1 = _norm_mod_fwd("pre_mix_norm", x, sm["g_pre_mix"], sc1, sh1)
    z, xbc, dt_raw, glu_in, gate_l = [
        _matmul(f"proj_{nm}", h1, win_t, tb=True, b_rows=rows, out_dtype=F32 if nm == "dt" else BF16)
        for nm, rows in sections]
    plan_c = plan_f = plan_r = None
    if late is not None:
        plan_c = _gather_plan([late[n] for n in MATRICES[1:4]])
        plan_f = _gather_plan([late["w_down"]])
        plan_r = _gather_plan([late["w_gate_up"]])
    xbc_c = _dwconv_fwd("ssm_conv_fwd", xbc, sm["w_conv_ssm"], sm["b_conv_ssm"], silu=True, out_dtype=BF16, plan=plan_c)
    if late is not None:
        xbc_c, got_c = xbc_c
    tables = _scan_tables(n_heads)
    dt, q_all, gam_t, etot = _dt_prepare(dt_raw, sm["dt_bias"], sm["a_neg"], tables["route"])
    etot = etot.reshape(s // CHUNK, 8, 2 * n_heads)
    (etx_f, etg_f), (etx_r, etg_r) = [_chunk_decay(etot, n_heads, direction) for direction in (0, 1)]
    y_f, hs_f, got_f = _scan_fwd("ssd_fwd_f", xbc_c, q_all, gam_t, etx_f, tables, direction=0, d_ssm=d_ssm, plan=plan_f)
    y_r, hs_r, got_r = _scan_fwd("ssd_fwd_r", xbc_c, q_all, gam_t, etx_r, tables, direction=1, d_ssm=d_ssm, plan=plan_r)
    if late is not None:
        wts = dict(wts, w_ssm_out=got_c[0].reshape(-1, d), w_conv_out=got_c[1].reshape(-1, d),
                   w_mix_out=got_c[2].reshape(-1, d), w_down=got_f[0].reshape(-1, d), w_gate_up=got_r[0])
    y_ssd, yn = _gated_norm_fwd(y_f, y_r, xbc_c, z, sm["d_skip_x"], sm["g_ssm_norm"])
    y_a = _matmul("ssm_out", yn, wts["w_ssm_out"])
    u0 = _glu_fwd(glu_in, sm["b_glu"])
    u1 = _dwconv_fwd("dw_conv_fwd", u0, sm["w_dw"], sm["b_dw"], silu=False)
    u = _ln_silu_fwd(u1, sm["ln_g"], sm["ln_b"])
    y_b = _matmul("conv_out", u, wts["w_conv_out"], bias=sm["b_conv_out"])
    mixin = _gate_merge_fwd(y_a, y_b, gate_l, sm["b_gate"])
    mix = _matmul("mix_out", mixin, wts["w_mix_out"])
    x1 = _gated_residual_fwd("post_mix_residual", x, mix, g1, sm["g_post_mix"])
    h2 = _norm_mod_fwd("pre_ffn_norm", x1, sm["g_pre_ffn"], sc2, sh2)
    gu = _matmul("ffn_gate_up", h2, wts["w_gate_up"], b_blocks=N_CHIPS, out_dtype=BF16)
    act = _swiglu_fwd(gu)
    f = _matmul("ffn_down", act, wts["w_down"])

    dx2, df, sq, d_g2, d_gpf = _final_residual_loss(x1, f, tgt, g2, sm["g_post_ffn"])
    dact = _matmul("d_act", df, wts["w_down"], tb=True, out_dtype=BF16)
    g_w_down = _matmul("g_w_down", act, df, ta=True, out_dtype=BF16)
    dgu = _swiglu_bwd(gu, dact)
    dh2 = _matmul("d_h2", dgu, wts["w_gate_up"], tb=True, b_blocks=N_CHIPS)
    g_w_gate_up = _matmul("g_w_gate_up", h2, dgu, ta=True, out_dtype=BF16, out_blocks=N_CHIPS)
    dx1, d_gpre_ffn, d_sc2, d_sh2 = _norm_mod_bwd("pre_ffn_norm_bwd", x1, dh2, dx2, sm["g_pre_ffn"], sc2)
    dmix, d_g1, d_gpm = _gated_residual_bwd("post_mix_residual_bwd", mix, dx1, g1, sm["g_post_mix"])
    dmixin = _matmul("d_mixin", dmix, wts["w_mix_out"], tb=True, out_dtype=BF16)
    g_w_mix = _matmul("g_w_mix_out", mixin, dmix, ta=True, out_dtype=BF16)
    dy_a, dy_b, dgate_l, d_bgate, d_bco = _gate_merge_bwd(dmixin, y_a, y_b, gate_l, sm["b_gate"])
    du = _matmul("d_u", dy_b, wts["w_conv_out"], tb=True, out_dtype=BF16)
    g_w_co = _matmul("g_w_conv_out", u, dy_b, ta=True, out_dtype=BF16)
    du1, d_lng, d_lnb = _ln_silu_bwd(u1, du, sm["ln_g"], sm["ln_b"])
    du0, d_wdw, d_bdw = _dwconv_bwd("dw_conv_bwd", u0, sm["w_dw"], sm["b_dw"], du1, silu=False)
    dglu, d_bglu = _glu_bwd(glu_in, sm["b_glu"], du0)
    dyn = _matmul("d_yn", dy_a, wts["w_ssm_out"], tb=True, out_dtype=BF16)
    g_w_ssm = _matmul("g_w_ssm_out", yn, dy_a, ta=True, out_dtype=BF16)
    dy_ssd, dz, d_gssm, d_dskip_x = _gated_norm_bwd(y_ssd, z, dyn, xbc_c, sm["d_skip_x"], sm["g_ssm_norm"])
    early = [g_w_ssm.reshape(N_CHIPS, -1, d), g_w_co.reshape(N_CHIPS, -1, d), g_w_mix.reshape(N_CHIPS, -1, d),
             g_w_gate_up, g_w_down.reshape(N_CHIPS, -1, d)]
    plan_b = sums = None
    if late is not None:
        sums = _chip_sums("early", early)
        plan_b = _send_chips_plan(sums)
    (dxs_f, db_f, dc_f, ddt_f, da_f), received = _scan_bwd(
        "ssd_bwd_f", xbc_c, dy_ssd, hs_f, q_all, gam_t, etx_f, etg_f, tables, direction=0, d_ssm=d_ssm, plan=plan_b)
    (dxs_r, db_r, dc_r, ddt_r, da_r), _ = _scan_bwd(
        "ssd_bwd_r", xbc_c, dy_ssd, hs_r, q_all, gam_t, etx_r, etg_r, tables, direction=1, d_ssm=d_ssm)
    ddt_raw, d_dtbias, d_alog = _dt_backward((da_f, da_r), (ddt_f, ddt_r), dt, dt_raw, sm["dt_bias"], sm["a_neg"])
    dxbc_c = _ssd_grad_merge(dxs_f, dxs_r, dy_ssd, db_f, db_r, dc_f, dc_r, sm["d_skip_x"])
    dxbc, d_wconv, d_bconv = _dwconv_bwd("ssm_conv_bwd", xbc, sm["w_conv_ssm"], sm["b_conv_ssm"], dxbc_c,
                                         silu=True, dx_dtype=BF16)
    dsecs = [dz, dxbc, ddt_raw, dglu, dgate_l]
    dh1 = g_win = None
    for (nm, rows), dsec in zip(sections, dsecs):
        g_win = _matmul(f"g_w_in_{nm}", dsec, h1, ta=True, out_dtype=BF16, out_rows=(rows[0], sec[-1], g_win))
    hosts, sums_in, received_in = {}, None, [None] * 3
    if late is not None:
        sums_in = _chip_sums("w_in", [g_win.reshape(N_CHIPS, -1, d)])
        hosts = {"xbc": 0, "z": 1, "glu": 2}
    for (nm, rows), dsec in zip(sections, dsecs):
        plan_in = _send_chips_plan(sums_in, (hosts[nm],)) if nm in hosts else None
        dh1 = _matmul(f"d_h1_{nm}", dsec, win_t, b_rows=rows, add=dh1, plan=plan_in)
        if plan_in is not None:
            dh1, (received_in[hosts[nm]],) = dh1
    grad_x, d_gpre_mix, d_sc1, d_sh1 = _norm_mod_bwd("pre_mix_norm_bwd", x, dh1, dx1, sm["g_pre_mix"], sc1)

    dmod = jnp.concatenate([d_sh1, d_sc1, d_g1, d_sh2, d_sc2, d_g2], axis=1)
    if late is None:
        big = {"w_in_t": g_win, "w_ssm_out": g_w_ssm, "w_conv_out": g_w_co, "w_mix_out": g_w_mix,
               "w_gate_up": g_w_gate_up, "w_down": g_w_down}
    else:
        big = {"pending": (sums_in + sums, [received_in] + [[t, t, t] for t in received])}
    small = {"g_pre_mix": d_gpre_mix, "g_post_mix": d_gpm, "w_conv_ssm": d_wconv, "b_conv_ssm": d_bconv,
             "dt_bias": d_dtbias, "a_log": d_alog, "d_skip_x": d_dskip_x, "g_ssm_norm": d_gssm, "b_glu": d_bglu,
             "w_dw": d_wdw, "b_dw": d_bdw, "ln_g": d_lng, "ln_b": d_lnb, "b_conv_out": d_bco, "b_gate": d_bgate,
             "g_pre_ffn": d_gpre_ffn, "g_post_ffn": d_gpf}
    return sq, grad_x, big, small, dmod


ANY = pl.BlockSpec(memory_space=pl.ANY)
WHOLE_VMEM = pl.BlockSpec(memory_space=pltpu.VMEM)


def _mesh_place():
    x, y, c = lax.axis_index("x"), lax.axis_index("y"), lax.axis_index("c")
    other_chips = [(1 - x, y), (x, 1 - y), (1 - x, 1 - y)]
    return x, y, c, other_chips


def _remote(src, dst, send_sems, recv_sems, k, device):
    return pltpu.make_async_remote_copy(src_ref=src, dst_ref=dst, send_sem=send_sems.at[k], recv_sem=recv_sems.at[k],
                                        device_id=device, device_id_type=MESH)


def _gather_devices(name, block):
    m_per, n = block.shape

    def body(x_ref, out_ref, send_sems, recv_sems, local_sem):
        x, y, c, chips = _mesh_place()
        me, sibling = (x, y, c), (x, y, 1 - c)

        def rows(px, py, pc):
            return out_ref.at[pl.ds((4 * px + 2 * py + pc) * m_per, m_per), :]

        def copy(k, blk, to, src=None):
            return _remote(rows(*blk) if src is None else src, rows(*blk), send_sems, recv_sems, k, to)

        mine = pltpu.make_async_copy(x_ref, rows(*me), local_sem)
        mine.start()
        first = [copy(0, me, sibling, src=x_ref)]
        first += [copy(1 + j, me, (*chip, c), src=x_ref) for j, chip in enumerate(chips)]
        for cp in first:
            cp.start()
        passed = [copy(4 + j, (*chip, c), sibling) for j, chip in enumerate(chips)]
        for j, chip in enumerate(chips):
            copy(1 + j, (*chip, c), me).wait_recv()
            passed[j].start()
        copy(0, sibling, me).wait_recv()
        for j, chip in enumerate(chips):
            copy(4 + j, (*chip, 1 - c), me).wait_recv()
        for cp in first + passed:
            cp.wait_send()
        mine.wait()

    return pl.pallas_call(
        body, name=name, out_shape=jax.ShapeDtypeStruct((N_DEV * m_per, n), block.dtype),
        in_specs=[WHOLE_VMEM], out_specs=WHOLE_VMEM,
        scratch_shapes=[pltpu.SemaphoreType.DMA((7,)), pltpu.SemaphoreType.DMA((7,)), pltpu.SemaphoreType.DMA],
        compiler_params=pltpu.CompilerParams(vmem_limit_bytes=VMEM_LIMIT),
    )(block)


class _Plan:
    def __init__(self, operands, out_shapes, copies, phases):
        self.operands, self.out_shapes, self.copies, self.phases = list(operands), list(out_shapes), copies, phases

    def sems(self):
        return [pltpu.SemaphoreType.DMA((self.copies,)), pltpu.SemaphoreType.DMA((self.copies,))]


def _run_plan(name, plan):
    n_in, n_out = len(plan.operands), len(plan.out_shapes)

    def body(*refs):
        ins, outs = refs[:n_in], refs[n_in:n_in + n_out]
        send_sems, recv_sems = refs[n_in + n_out:]
        for phase in ("start", "middle", "end"):
            if phase in plan.phases:
                plan.phases[phase](ins, outs, send_sems, recv_sems)

    return list(pl.pallas_call(body, name=name, out_shape=plan.out_shapes, in_specs=[ANY] * n_in,
                               out_specs=[ANY] * n_out, scratch_shapes=plan.sems())(*plan.operands))


def _gather_plan(shards):
    n = len(shards)

    def copies(kinds, ins, outs, send_sems, recv_sems):
        x, y, c, chips = _mesh_place()
        me = 2 * x + y
        sibling = (x, y, 1 - c)

        def half(i, h):
            hr = ins[i].shape[0] // 2
            return pl.ds(h * hr, hr)

        def block(i, j, h):
            cx, cy = chips[j]
            return outs[i].at[2 * cx + cy, half(i, h)]

        make = {
            "over_ici": lambda i, j: _remote(ins[i].at[half(i, c)], outs[i].at[me, half(i, c)], send_sems, recv_sems,
                                             6 * i + j, (*chips[j], c)),
            "arrived": lambda i, j: _remote(block(i, j, c), block(i, j, c), send_sems, recv_sems, 6 * i + j, (*chips[j], c)),
            "passed_on": lambda i, j: _remote(block(i, j, c), block(i, j, c), send_sems, recv_sems, 6 * i + 3 + j, sibling),
            "from_sibling": lambda i, j: _remote(block(i, j, 1 - c), block(i, j, 1 - c), send_sems, recv_sems,
                                                 6 * i + 3 + j, sibling),
        }
        res = []
        for kind in kinds:
            if kind == "own":
                res.append([_remote(ins[i], outs[i].at[me], send_sems, recv_sems, 6 * n + i, sibling) for i in range(n)])
            else:
                res.append([make[kind](i, j) for i in range(n) for j in range(3)])
        return res

    def start(*refs):
        over_ici, own = copies(("over_ici", "own"), *refs)
        for cp in over_ici + own:
            cp.start()

    def end(*refs):
        arrived, passed_on = copies(("arrived", "passed_on"), *refs)
        for got, fwd in zip(arrived, passed_on):
            got.wait_recv()
            fwd.start()
        from_sibling, own_in = copies(("from_sibling", "own"), *refs)
        for cp in from_sibling + own_in:
            cp.wait_recv()
        over_ici, passed_on, own_out = copies(("over_ici", "passed_on", "own"), *refs)
        for cp in over_ici + passed_on + own_out:
            cp.wait_send()

    return _Plan(shards, [jax.ShapeDtypeStruct((N_CHIPS,) + s.shape, s.dtype) for s in shards], 7 * n,
                 {"start": start, "end": end})


def _send_sibling_halves(name, grads):
    n = len(grads)

    def body(*refs):
        ins, outs = refs[:n], refs[n:2 * n]
        send_sems, recv_sems = refs[2 * n:]
        x, y, c, _ = _mesh_place()
        sibling = (x, y, 1 - c)
        copies = []
        for i in range(n):
            for j in range(N_CHIPS):
                copies.append(_remote(ins[i].at[j, 1 - c], outs[i].at[j], send_sems, recv_sems, N_CHIPS * i + j, sibling))
                copies[-1].start()
        for cp in copies:
            cp.wait_recv()
        for cp in copies:
            cp.wait_send()

    return pl.pallas_call(
        body, name=name,
        out_shape=[jax.ShapeDtypeStruct((g.shape[0],) + g.shape[2:], g.dtype) for g in grads],
        in_specs=[ANY] * n, out_specs=[ANY] * n,
        scratch_shapes=[pltpu.SemaphoreType.DMA((N_CHIPS * n,)), pltpu.SemaphoreType.DMA((N_CHIPS * n,))],
    )(*grads)


def _send_chips_plan(sums, neighbours=(0, 1, 2)):
    n = len(sums)

    def copies(ins, outs, send_sems, recv_sems):
        x, y, c, chips = _mesh_place()
        return [_remote(ins[i].at[2 * chips[j][0] + chips[j][1]], outs[i].at[j], send_sems, recv_sems, 3 * i + j,
                        (*chips[j], c))
                for i in range(n) for j in neighbours]

    def start(*refs):
        for cp in copies(*refs):
            cp.start()

    def end(*refs):
        for cp in copies(*refs):
            cp.wait_recv()
        for cp in copies(*refs):
            cp.wait_send()

    return _Plan(sums, [jax.ShapeDtypeStruct((3,) + g.shape[1:], g.dtype) for g in sums], 3 * n,
                 {"start": start, "end": end})


def _exchange_halves(name, shards):
    n = len(shards)

    def body(*refs):
        outs = refs[n:2 * n]
        send_sems, recv_sems = refs[2 * n:]
        x, y, c, _ = _mesh_place()
        sibling = (x, y, 1 - c)
        remote = [_remote(outs[i].at[c], outs[i].at[c], send_sems, recv_sems, i, sibling) for i in range(n)]
        for cp in remote:
            cp.start()
        for i in range(n):
            _remote(outs[i].at[1 - c], outs[i].at[1 - c], send_sems, recv_sems, i, sibling).wait_recv()
        for cp in remote:
            cp.wait_send()

    return pl.pallas_call(
        body, name=name,
        out_shape=[jax.ShapeDtypeStruct(h.shape, h.dtype) for h in shards],
        in_specs=[ANY] * n, out_specs=[ANY] * n, input_output_aliases={i: i for i in range(n)},
        scratch_shapes=[pltpu.SemaphoreType.DMA((n,)), pltpu.SemaphoreType.DMA((n,))],
    )(*shards)


def _divisor_tile(rows, row_bytes, quantum=16):
    best = rows
    for t in range(quantum, rows + 1, quantum):
        if rows % t == 0 and 2 * t * row_bytes <= ROW_TILE_BUDGET:
            best = t
    return best


def _add_sibling(name, g4, t1):
    nb, _, hr, cols = g4.shape
    t = _divisor_tile(hr, cols * 6)

    def kern(g_ref, t_ref, o_ref):
        o_ref[0] = (g_ref[0, 0].astype(F32) + t_ref[0].astype(F32)).astype(o_ref.dtype)

    return pl.pallas_call(
        kern, name=name, grid=(nb, hr // t),
        in_specs=[pl.BlockSpec((1, 1, t, cols), lambda j, i: (j, lax.axis_index("c"), i, 0)),
                  pl.BlockSpec((1, t, cols), lambda j, i: (j, i, 0))],
        out_specs=pl.BlockSpec((1, t, cols), lambda j, i: (j, i, 0)),
        out_shape=jax.ShapeDtypeStruct((nb, hr, cols), g4.dtype),
        compiler_params=_params(("parallel", "parallel")),
    )(g4, t1)


def _add_chips(name, s1, t3):
    _, hr, cols = s1.shape
    t = _divisor_tile(hr, cols * 12)

    def kern(s_ref, t0_ref, t1_ref, t2_ref, o_ref):
        acc = s_ref[0].astype(F32)
        for t_ref in (t0_ref, t1_ref, t2_ref):
            acc = acc + t_ref[0].astype(F32)
        o_ref[0] = acc

    slot = lambda j: pl.BlockSpec((1, t, cols), functools.partial(lambda i, j: (j, i, 0), j=j))
    return pl.pallas_call(
        kern, name=name, grid=(hr // t,),
        in_specs=[pl.BlockSpec((1, t, cols), lambda i: (2 * lax.axis_index("x") + lax.axis_index("y"), i, 0)),
                  slot(0), slot(1), slot(2)],
        out_specs=pl.BlockSpec((1, t, cols), lambda i: (lax.axis_index("c"), i, 0)),
        out_shape=jax.ShapeDtypeStruct((2, hr, cols), F32),
        compiler_params=_params(("parallel",)),
    )(s1, *t3)


def _chip_sums(tag, grads):
    g4 = [g.reshape(N_CHIPS, 2, g.shape[1] // 2, g.shape[2]) for g in grads]
    t1 = _send_sibling_halves("grads_to_sibling_" + tag, g4)
    return [_add_sibling(f"chip_sum_{tag}_{i}", g, t) for i, (g, t) in enumerate(zip(g4, t1))]


def _shard_sums(sums, received):
    halves = [_add_chips(f"shard_sum_{i}", s, t) for i, (s, t) in enumerate(zip(sums, received))]
    full = _exchange_halves("grad_halves_to_sibling", halves)
    return [f.reshape(f.shape[1] * 2, f.shape[2]) for f in full]


def _pack_rows(size, width):
    return -(-size // (8 * width)) * 8


def _pack(arrays, width):
    parts = []
    for a in arrays:
        flat = a.reshape(-1).astype(F32)
        rows = _pack_rows(flat.shape[0], width)
        parts.append(jnp.pad(flat, (0, rows * width - flat.shape[0])).reshape(rows, width))
    return jnp.concatenate(parts, axis=0)


def _unpack(block, shapes, width):
    out, r = [], 0
    for shp in shapes:
        size = 1
        for s_ in shp:
            size *= s_
        rows = _pack_rows(size, width)
        out.append(block[r:r + rows].reshape(-1)[:size].reshape(shp))
        r += rows
    return out


SMALL_PARAMS = ("b_ada", "g_pre_mix", "g_post_mix", "b_conv_ssm", "dt_bias_fwd", "dt_bias_bwd", "a_log_fwd", "a_log_bwd",
                "d_skip", "g_ssm_norm", "b_glu", "b_dw", "ln_g", "ln_b", "b_conv_out", "b_gate", "g_pre_ffn", "g_post_ffn")
SHARDED_SMALL = ("w_conv_ssm", "w_dw")
MATRICES = ("w_in", "w_ssm_out", "w_conv_out", "w_mix_out", "w_gate_up", "w_down")
ALL_PARAMS = ("w_ada", "b_ada", "g_pre_mix", "g_post_mix", "w_in", "w_conv_ssm", "b_conv_ssm", "dt_bias_fwd", "dt_bias_bwd",
              "a_log_fwd", "a_log_bwd", "d_skip", "g_ssm_norm", "w_ssm_out", "b_glu", "w_dw", "b_dw", "ln_g", "ln_b",
              "w_conv_out", "b_conv_out", "b_gate", "w_mix_out", "g_pre_ffn", "g_post_ffn", "w_gate_up", "w_down")
COND_ROWS = 48
COND_CONV_ROW = 8
COND_DW_ROW = 16
MOD_ROWS = 16


def kernel(x, c, w_ada, b_ada, g_pre_mix, g_post_mix, w_in, w_conv_ssm, b_conv_ssm, dt_bias_fwd, dt_bias_bwd, a_log_fwd, a_log_bwd, d_skip, g_ssm_norm, w_ssm_out, b_glu, w_dw, b_dw, ln_g, ln_b, w_conv_out, b_conv_out, b_gate, w_mix_out, g_pre_ffn, g_post_ffn, w_gate_up, w_down, loss_target, m_w_ada, m_b_ada, m_g_pre_mix, m_g_post_mix, m_w_in, m_w_conv_ssm, m_b_conv_ssm, m_dt_bias_fwd, m_dt_bias_bwd, m_a_log_fwd, m_a_log_bwd, m_d_skip, m_g_ssm_norm, m_w_ssm_out, m_b_glu, m_w_dw, m_b_dw, m_ln_g, m_ln_b, m_w_conv_out, m_b_conv_out, m_b_gate, m_w_mix_out, m_g_pre_ffn, m_g_post_ffn, m_w_gate_up, m_w_down, v_w_ada, v_b_ada, v_g_pre_mix, v_g_post_mix, v_w_in, v_w_conv_ssm, v_b_conv_ssm, v_dt_bias_fwd, v_dt_bias_bwd, v_a_log_fwd, v_a_log_bwd, v_d_skip, v_g_ssm_norm, v_w_ssm_out, v_b_glu, v_w_dw, v_b_dw, v_ln_g, v_ln_b, v_w_conv_out, v_b_conv_out, v_b_gate, v_w_mix_out, v_g_pre_ffn, v_g_post_ffn, v_w_gate_up, v_w_down):
    given = dict(locals())
    wgt = {n: given[n][0] for n in ALL_PARAMS}
    mom = {n: given["m_" + n][0] for n in ALL_PARAMS}
    var = {n: given["v_" + n][0] for n in ALL_PARAMS}
    xs, tgt = x[0], loss_target[0]
    s, d = xs.shape
    d_ssm = 2 * d
    n_heads = d_ssm // HEAD_DIM
    d_xbc = d_ssm + 2 * N_GROUPS * D_STATE
    xi, yi, ci = lax.axis_index("x"), lax.axis_index("y"), lax.axis_index("c")
    chip = 2 * xi + yi
    dev = 2 * chip + ci
    k_conv, k_dw = wgt["w_conv_ssm"].shape[0], wgt["w_dw"].shape[0]
    xbc_shard, dw_shard = d_xbc // N_CHIPS, d // N_CHIPS

    width1 = max(d, xbc_shard)
    blk = jnp.zeros((COND_ROWS, width1), F32)
    blk = blk.at[0, :d].set(c[0])
    blk = blk.at[COND_CONV_ROW:COND_CONV_ROW + k_conv, :xbc_shard].set(wgt["w_conv_ssm"])
    blk = blk.at[COND_DW_ROW:COND_DW_ROW + k_dw, :dw_shard].set(wgt["w_dw"])
    g1 = _gather_devices("gather_cond", blk).reshape(N_DEV, COND_ROWS, width1)
    c_all = g1[:, 0, :d]
    w_conv_full = jnp.concatenate([g1[2 * k, COND_CONV_ROW:COND_CONV_ROW + k_conv, :xbc_shard] for k in range(N_CHIPS)], axis=1)
    w_dw_full = jnp.concatenate([g1[2 * k, COND_DW_ROW:COND_DW_ROW + k_dw, :dw_shard] for k in range(N_CHIPS)], axis=1)
    c_act = jnp.pad(c_all * _sigmoid(c_all), ((0, MOD_ROWS - N_DEV), (0, 0)))

    mod_part = _matmul("ada_mod", c_act, wgt["w_ada"])
    g2 = _gather_devices("gather_mod", mod_part).reshape(N_DEV, MOD_ROWS, mod_part.shape[1])
    mod_all = jnp.concatenate([g2[2 * k, :N_DEV] for k in range(N_CHIPS)], axis=1) + wgt["b_ada"][None]
    mod = lax.dynamic_slice_in_dim(mod_all, dev, 1, axis=0)

    shards = [wgt["w_in"].T.astype(BF16)] + [wgt[n].astype(BF16) for n in MATRICES[1:]]
    wts = {"w_in_t": _run_plan("gather_w_in", _gather_plan(shards[:1]))[0].reshape(-1, d)}
    late = dict(zip(MATRICES[1:], shards[1:]))
    row = lambda v: v.reshape(1, -1)
    sm = {"g_pre_mix": row(wgt["g_pre_mix"]), "g_post_mix": row(wgt["g_post_mix"]), "w_conv_ssm": w_conv_full,
          "b_conv_ssm": row(wgt["b_conv_ssm"]),
          "dt_bias": row(jnp.concatenate([wgt["dt_bias_fwd"], wgt["dt_bias_bwd"]])),
          "a_neg": row(-jnp.exp(jnp.concatenate([wgt["a_log_fwd"], wgt["a_log_bwd"]]))),
          "d_skip_x": row(jnp.repeat(wgt["d_skip"], HEAD_DIM)), "g_ssm_norm": row(wgt["g_ssm_norm"]),
          "b_glu": row(wgt["b_glu"]), "w_dw": w_dw_full, "b_dw": row(wgt["b_dw"]), "ln_g": row(wgt["ln_g"]),
          "ln_b": row(wgt["ln_b"]), "b_conv_out": row(wgt["b_conv_out"]), "b_gate": row(wgt["b_gate"]),
          "g_pre_ffn": row(wgt["g_pre_ffn"]), "g_post_ffn": row(wgt["g_post_ffn"])}

    sq, grad_x, big, small, dmod = _local_step(xs, tgt, mod, wts, sm, late=late)
    loss = lax.psum((0.5 / d) * sq[0, 0], ("x", "y", "c"))

    local_small = {"b_ada": dmod, "g_pre_mix": small["g_pre_mix"], "g_post_mix": small["g_post_mix"],
                   "b_conv_ssm": small["b_conv_ssm"], "dt_bias_fwd": small["dt_bias"][:, :n_heads],
                   "dt_bias_bwd": small["dt_bias"][:, n_heads:], "a_log_fwd": small["a_log"][:, :n_heads],
                   "a_log_bwd": small["a_log"][:, n_heads:],
                   "d_skip": jnp.sum(small["d_skip_x"].reshape(n_heads, HEAD_DIM), axis=1),
                   "g_ssm_norm": small["g_ssm_norm"], "b_glu": small["b_glu"], "b_dw": small["b_dw"],
                   "ln_g": small["ln_g"], "ln_b": small["ln_b"], "b_conv_out": small["b_conv_out"],
                   "b_gate": small["b_gate"], "g_pre_ffn": small["g_pre_ffn"], "g_post_ffn": small["g_post_ffn"],
                   "w_conv_ssm": small["w_conv_ssm"], "w_dw": small["w_dw"]}
    names = SMALL_PARAMS + SHARDED_SMALL
    pack = _pack([local_small[n] for n in names], d)
    rows_p = pack.shape[0]
    g3 = _gather_devices("gather_small_grads", pack).reshape(N_DEV, rows_p, d)
    total = _rowwise("sum_small_grads", lambda *blocks: (functools.reduce(lambda a, b: a + b, blocks),),
                     [g3[i] for i in range(N_DEV)], [], [(d, F32)])[0]
    full_shapes = [wgt[n].shape for n in SMALL_PARAMS] + [(k_conv, d_xbc), (k_dw, d)]
    summed = dict(zip(names, _unpack(total, full_shapes, d)))
    grads = {n: summed[n] for n in SMALL_PARAMS}
    grads["w_conv_ssm"] = lax.dynamic_slice_in_dim(summed["w_conv_ssm"], chip * xbc_shard, xbc_shard, axis=1)
    grads["w_dw"] = lax.dynamic_slice_in_dim(summed["w_dw"], chip * dw_shard, dw_shard, axis=1)

    dmod_all = g3[:, :N_MOD, :].reshape(N_DEV, N_MOD * d)
    ada_cols = wgt["w_ada"].shape[1]
    dmod_cols = jnp.pad(lax.dynamic_slice_in_dim(dmod_all, chip * ada_cols, ada_cols, axis=1),
                        ((0, MOD_ROWS - N_DEV), (0, 0)))
    grads["w_ada"] = _matmul("g_w_ada", c_act, dmod_cols, ta=True)

    reduced = _shard_sums(*big["pending"])
    grads["w_in"] = reduced[0].T
    for n, g in zip(MATRICES[1:], reduced[1:]):
        grads[n] = g

    delta, new_m, new_v = {}, {}, {}
    for n in ("w_ada",) + MATRICES:
        delta[n], new_m[n], new_v[n] = _adamw("adamw_" + n, wgt[n], grads[n], mom[n], var[n])
    for group, width, tag in ((SMALL_PARAMS, d, "small"), (SHARDED_SMALL, LANES, "conv")):
        shapes = [wgt[n].shape for n in group]
        packs = [_pack([src[n] for n in group], width) for src in (wgt, grads, mom, var)]
        outs = _adamw("adamw_" + tag, *packs)
        for res, o in zip((delta, new_m, new_v), outs):
            res.update(zip(group, _unpack(o, shapes, width)))

    lead = lambda a: a[None]
    return (loss, grad_x[None], *[lead(grads[n]) for n in ALL_PARAMS], *[lead(delta[n]) for n in ALL_PARAMS],
            *[lead(new_m[n]) for n in ALL_PARAMS], *[lead(new_v[n]) for n in ALL_PARAMS])
```

```python
import functools

import jax
import jax.numpy as jnp
from jax import lax
from jax.experimental import pallas as pl
from jax.experimental.pallas import tpu as pltpu

F32 = jnp.float32
BF16 = jnp.bfloat16

N_GROUPS = 8
HEAD_DIM = 64
D_STATE = 128
CHUNK = 128
EPS = 1e-6
N_MOD = 6
ADAM_LR = 0.001
ADAM_B1 = 0.9
ADAM_B2 = 0.999
ADAM_EPS = 1e-08
ADAM_WD = 0.01
ADAM_STEP = 10

V7X_VMEM_BYTES = 64 * 1024 * 1024
VMEM_LIMIT = V7X_VMEM_BYTES - 8 * 1024 * 1024
ROW_TILE_BUDGET = 20 * 1024 * 1024
LANES = 128
NEG = -1e30
MESH = pl.DeviceIdType.MESH
N_CHIPS = 4
N_DEV = 8


def _params(sem):
    return pltpu.CompilerParams(dimension_semantics=sem, vmem_limit_bytes=VMEM_LIMIT)


def _sigmoid(x):
    return 1.0 / (1.0 + jnp.exp(-x))


def _silu(x):
    return x * _sigmoid(x)


def _dsilu(x):
    s = _sigmoid(x)
    return s * (1.0 + x * (1.0 - s))


def _softplus(x):
    return jnp.maximum(x, 0.0) + jnp.log(1.0 + jnp.exp(-jnp.abs(x)))


def _sum0(a):
    return jnp.sum(a, axis=0, keepdims=True)


def _mean1(a):
    return jnp.mean(a, axis=1, keepdims=True)


def _rowwise(name, body, rows, bcasts, out_rows, out_accs=(), tile=None, plan=None):
    rows = [r if isinstance(r, tuple) else (r, r.shape[1], 0) for r in rows]
    s = rows[0][0].shape[0]
    if tile is None:
        per_row = sum(w * a.dtype.itemsize for a, w, _ in rows) + sum(w * jnp.dtype(dt).itemsize for w, dt in out_rows)
        tile = 1024
        while tile > 16 and (tile * per_row * 2 > ROW_TILE_BUDGET or s % tile):
            tile //= 2
        if s * per_row * 2 <= ROW_TILE_BUDGET:
            tile = s
    assert s % tile == 0
    n_in = len(rows) + len(bcasts)
    n_o = len(out_rows)
    n_out = n_o + len(out_accs)
    p_ops, p_in_specs, p_out_shapes, p_out_specs, p_scratch, _ = _plan_call_parts(plan)
    n_pi, n_po = len(p_ops), len(p_out_shapes)
    n_steps = (s // tile,)

    def kern(*refs):
        plan_ins = refs[n_in:n_in + n_pi]
        outs = refs[n_in + n_pi:n_in + n_pi + n_out]
        plan_outs = refs[n_in + n_pi + n_out:n_in + n_pi + n_out + n_po]
        plan_sems = refs[n_in + n_pi + n_out + n_po:]
        _plan_phase(plan, "start", n_steps, plan_ins, plan_outs, plan_sems)
        _plan_phase(plan, "middle", n_steps, plan_ins, plan_outs, plan_sems)
        res = body(*[r[...].astype(F32) for r in refs[:n_in]])
        for o, v in zip(outs[:n_o], res[:n_o]):
            o[...] = v.astype(o.dtype)
        if out_accs:
            @pl.when(pl.program_id(0) == 0)
            def _():
                for o in outs[n_o:]:
                    o[...] = jnp.zeros_like(o)
            for o, v in zip(outs[n_o:], res[n_o:]):
                o[...] += v
        _plan_phase(plan, "end", n_steps, plan_ins, plan_outs, plan_sems)

    in_specs = [pl.BlockSpec((tile, w), functools.partial(lambda i, cb: (i, cb), cb=cb)) for _, w, cb in rows]
    in_specs += [pl.BlockSpec(b.shape, functools.partial(lambda i, nd: (0,) * nd, nd=b.ndim)) for b in bcasts]
    out_shape = [jax.ShapeDtypeStruct((s, w), dt) for w, dt in out_rows]
    out_shape += [jax.ShapeDtypeStruct((1, w), F32) for w in out_accs]
    out_specs = [pl.BlockSpec((tile, w), lambda i: (i, 0)) for w, _ in out_rows]
    out_specs += [pl.BlockSpec((1, w), lambda i: (0, 0)) for w in out_accs]
    return pl.pallas_call(
        kern, name=name, grid=n_steps, in_specs=in_specs + p_in_specs, out_specs=out_specs + p_out_specs,
        out_shape=out_shape + p_out_shapes, scratch_shapes=p_scratch,
        compiler_params=_params(("arbitrary",) if out_accs or plan is not None else ("parallel",)),
    )(*[a for a, _, _ in rows], *bcasts, *p_ops)


def _tile(n, pref):
    if n <= pref:
        return n
    t = (pref // LANES) * LANES
    while t >= LANES:
        if n % t == 0:
            return t
        t -= LANES
    return n


def _matmul(name, a, b, *, ta=False, tb=False, out_dtype=F32, bias=None, add=None, b_blocks=1, out_blocks=1,
            b_rows=None, out_rows=None, plan=None, tm=1024, tn=1408, tk=2816):
    m, k = (a.shape[1], a.shape[0]) if ta else a.shape
    if b_blocks > 1:
        rows_b, cols_b = b.shape[1], b.shape[2] * b_blocks
    else:
        rows_b, cols_b = b.shape
    if b_rows is not None:
        rows_b = b_rows[1]
    n, kb = (rows_b, cols_b) if tb else (cols_b, rows_b)
    assert k == kb, (name, a.shape, b.shape)
    tm, tn, tk = _tile(m, tm), _tile(n, tn), _tile(k, tk)
    if b_blocks > 1:
        per = cols_b // b_blocks
        if tb:
            tk = _tile(per, tk)
        else:
            tn = _tile(per, tn)
    if out_blocks > 1:
        tn = _tile(n // out_blocks, tn)
    nk = k // tk
    grid = (m // tm, n // tn, nk)

    a_spec = pl.BlockSpec((tk, tm), lambda i, j, kk: (kk, i)) if ta else pl.BlockSpec((tm, tk), lambda i, j, kk: (i, kk))
    if b_blocks > 1:
        if tb:
            nb = per // tk
            b_spec = pl.BlockSpec((1, tn, tk), lambda i, j, kk: (kk // nb, j, kk % nb))
        else:
            nb = per // tn
            b_spec = pl.BlockSpec((1, tk, tn), lambda i, j, kk: (j // nb, kk, j % nb))
    elif b_rows is not None:
        first = b_rows[0]
        if tb:
            b_spec = pl.BlockSpec((pl.Element(tn), pl.Element(tk)),
                                  lambda i, j, kk: (pl.multiple_of(first + j * tn, LANES), kk * tk))
        else:
            b_spec = pl.BlockSpec((pl.Element(tk), pl.Element(tn)),
                                  lambda i, j, kk: (pl.multiple_of(first + kk * tk, LANES), j * tn))
    else:
        b_spec = pl.BlockSpec((tn, tk), lambda i, j, kk: (j, kk)) if tb else pl.BlockSpec((tk, tn), lambda i, j, kk: (kk, j))
    in_specs = [a_spec, b_spec]
    operands = [a, b]
    if bias is not None:
        in_specs.append(pl.BlockSpec((1, tn), lambda i, j, kk: (0, j)))
        operands.append(bias)
    if add is not None:
        in_specs.append(pl.BlockSpec((tm, tn), lambda i, j, kk: (i, j)))
        operands.append(add)
    aliases = {}
    if out_blocks > 1:
        nbo = (n // out_blocks) // tn
        out_spec = pl.BlockSpec((1, tm, tn), lambda i, j, kk: (j // nbo, i, j % nbo))
        out_shape = jax.ShapeDtypeStruct((out_blocks, m, n // out_blocks), out_dtype)
    elif out_rows is not None:
        first_out, total, previous = out_rows
        out_spec = pl.BlockSpec((pl.Element(tm), pl.Element(tn)),
                                lambda i, j, kk: (pl.multiple_of(first_out + i * tm, LANES), j * tn))
        out_shape = jax.ShapeDtypeStruct((total, n), out_dtype)
        if previous is not None:
            aliases = {len(operands): 0}
            in_specs.append(pl.BlockSpec(memory_space=pl.ANY))
            operands.append(previous)
    else:
        out_spec = pl.BlockSpec((tm, tn), lambda i, j, kk: (i, j))
        out_shape = jax.ShapeDtypeStruct((m, n), out_dtype)
    dims = (((0 if ta else 1,), (1 if tb else 0,)), ((), ()))
    has_bias, has_add, has_previous = bias is not None, add is not None, bool(aliases)
    p_ops, p_in_specs, p_out_shapes, p_out_specs, p_scratch, _ = _plan_call_parts(plan)
    n_pi, n_po = len(p_ops), len(p_out_shapes)

    def kern(*refs):
        a_ref, b_ref = refs[0], refs[1]
        pos = 2
        bias_ref = add_ref = None
        if has_bias:
            bias_ref = refs[pos]
            pos += 1
        if has_add:
            add_ref = refs[pos]
            pos += 1
        if has_previous:
            pos += 1
        plan_ins = refs[pos:pos + n_pi]
        o_ref = refs[pos + n_pi]
        plan_outs = refs[pos + n_pi + 1:pos + n_pi + 1 + n_po]
        pos += n_pi + 1 + n_po
        acc_ref = refs[pos] if nk > 1 else None
        plan_sems = refs[pos + (1 if nk > 1 else 0):]
        _plan_phase(plan, "start", grid, plan_ins, plan_outs, plan_sems)
        av = a_ref[...].astype(BF16)
        bv = (b_ref[0] if b_blocks > 1 else b_ref[...]).astype(BF16)
        p = lax.dot_general(av, bv, dims, preferred_element_type=F32)

        def finish(acc):
            if has_bias:
                acc = acc + bias_ref[...]
            if has_add:
                acc = acc + add_ref[...]
            if out_blocks > 1:
                o_ref[0] = acc.astype(o_ref.dtype)
            else:
                o_ref[...] = acc.astype(o_ref.dtype)

        if nk == 1:
            finish(p)
        else:
            kk = pl.program_id(2)

            @pl.when(kk == 0)
            def _():
                acc_ref[...] = p

            @pl.when(kk > 0)
            def _():
                acc_ref[...] += p

            @pl.when(kk == nk - 1)
            def _():
                finish(acc_ref[...])
        _plan_phase(plan, "end", grid, plan_ins, plan_outs, plan_sems)

    res = pl.pallas_call(
        kern, name=name, grid=grid, in_specs=in_specs + p_in_specs, out_specs=[out_spec] + p_out_specs,
        out_shape=[out_shape] + p_out_shapes, input_output_aliases=aliases,
        scratch_shapes=([pltpu.VMEM((tm, tn), F32)] if nk > 1 else []) + p_scratch,
        compiler_params=_params(("parallel", "parallel", "arbitrary") if plan is None else ("arbitrary",) * 3),
    )(*operands, *p_ops)
    return res[0] if plan is None else (res[0], list(res[1:]))


CONV_HALO = 16
CONV_ROWS = 256


def _taps(win, shifts, rows):
    n = win.shape[0]
    for j, s in enumerate(shifts):
        yield j, (pltpu.roll(win, (n - s) % n, axis=0) if s % n else win)[:rows]


def _dwconv_fwd(name, x, w, b, *, silu, out_dtype=F32, plan=None):
    s, c = x.shape
    k = w.shape[0]
    pad = (k - 1) // 2
    assert pad <= CONV_HALO and c % LANES == 0
    t = min(CONV_ROWS, s)
    n_chunks = s // t
    fwd_shifts = [CONV_HALO - pad + j for j in range(k)]
    p_ops, p_in_specs, p_out_shapes, p_out_specs, p_scratch, _ = _plan_call_parts(plan)
    n_pi, n_po = len(p_ops), len(p_out_shapes)
    n_steps = (c // LANES,)

    def kern(*refs):
        x_ref, w_ref, b_ref = refs[:3]
        plan_ins = refs[3:3 + n_pi]
        o_ref = refs[3 + n_pi]
        plan_outs = refs[4 + n_pi:4 + n_pi + n_po]
        xp_ref = refs[4 + n_pi + n_po]
        plan_sems = refs[5 + n_pi + n_po:]
        _plan_phase(plan, "start", n_steps, plan_ins, plan_outs, plan_sems)
        _plan_phase(plan, "middle", n_steps, plan_ins, plan_outs, plan_sems)
        zeros = jnp.zeros((CONV_HALO, LANES), F32)
        xp_ref[0:CONV_HALO, :] = zeros
        xp_ref[CONV_HALO + s:CONV_HALO + s + CONV_HALO, :] = zeros
        xp_ref[CONV_HALO:CONV_HALO + s, :] = x_ref[...].astype(F32)
        bv = b_ref[...]

        def chunk(i, carry):
            base = pl.multiple_of(i * t, 16)
            win = xp_ref[pl.ds(base, t + 2 * CONV_HALO), :]
            acc = jnp.zeros((t, LANES), F32)
            for j, xs in _taps(win, fwd_shifts, t):
                acc = acc + xs * w_ref[pl.ds(j, 1), :]
            acc = acc + bv
            o_ref[pl.ds(base, t), :] = (_silu(acc) if silu else acc).astype(o_ref.dtype)
            return carry

        lax.fori_loop(0, n_chunks, chunk, 0)
        _plan_phase(plan, "end", n_steps, plan_ins, plan_outs, plan_sems)

    res = pl.pallas_call(
        kern, name=name, grid=n_steps,
        in_specs=[pl.BlockSpec((s, LANES), lambda i: (0, i)), pl.BlockSpec((k, LANES), lambda i: (0, i)),
                  pl.BlockSpec((1, LANES), lambda i: (0, i))] + p_in_specs,
        out_specs=[pl.BlockSpec((s, LANES), lambda i: (0, i))] + p_out_specs,
        out_shape=[jax.ShapeDtypeStruct((s, c), out_dtype)] + p_out_shapes,
        scratch_shapes=[pltpu.VMEM((s + 2 * CONV_HALO, LANES), F32)] + p_scratch,
        compiler_params=_params(("parallel",) if plan is None else ("arbitrary",)),
    )(x, w, b, *p_ops)
    return res[0] if plan is None else (res[0], list(res[1:]))


def _dwconv_bwd(name, x, w, b, dout, *, silu, dx_dtype=F32):
    s, c = x.shape
    k = w.shape[0]
    pad = (k - 1) // 2
    t = min(CONV_ROWS, s)
    n_chunks = s // t
    fwd_shifts = [CONV_HALO - pad + j for j in range(k)]
    bwd_shifts = [CONV_HALO + pad - j for j in range(k)]

    def kern(x_ref, w_ref, b_ref, do_ref, dx_ref, dw_ref, db_ref, xp_ref, dp_ref):
        zeros = jnp.zeros((CONV_HALO, LANES), F32)
        for ref in (xp_ref, dp_ref):
            ref[0:CONV_HALO, :] = zeros
            ref[CONV_HALO + s:CONV_HALO + s + CONV_HALO, :] = zeros
        xp_ref[CONV_HALO:CONV_HALO + s, :] = x_ref[...].astype(F32)
        bv = b_ref[...]
        dw_ref[...] = jnp.zeros_like(dw_ref)

        def pre_chunk(i, dbias):
            base = pl.multiple_of(i * t, 16)
            win = xp_ref[pl.ds(base, t + 2 * CONV_HALO), :]
            dpre = do_ref[pl.ds(base, t), :].astype(F32)
            if silu:
                acc = jnp.zeros((t, LANES), F32)
                for j, xs in _taps(win, fwd_shifts, t):
                    acc = acc + xs * w_ref[pl.ds(j, 1), :]
                dpre = dpre * _dsilu(acc + bv)
            dp_ref[pl.ds(base + CONV_HALO, t), :] = dpre
            for j, xs in _taps(win, fwd_shifts, t):
                dw_ref[pl.ds(j, 1), :] += _sum0(dpre * xs)
            return dbias + _sum0(dpre)

        db_ref[...] = lax.fori_loop(0, n_chunks, pre_chunk, jnp.zeros((1, LANES), F32))

        def dx_chunk(i, carry):
            base = pl.multiple_of(i * t, 16)
            win = dp_ref[pl.ds(base, t + 2 * CONV_HALO), :]
            acc = jnp.zeros((t, LANES), F32)
            for j, dps in _taps(win, bwd_shifts, t):
                acc = acc + dps * w_ref[pl.ds(j, 1), :]
            dx_ref[pl.ds(base, t), :] = acc.astype(dx_ref.dtype)
            return carry

        lax.fori_loop(0, n_chunks, dx_chunk, 0)

    col = lambda rows: pl.BlockSpec((rows, LANES), lambda i: (0, i))
    return pl.pallas_call(
        kern, name=name, grid=(c // LANES,),
        in_specs=[col(s), col(k), col(1), col(s)],
        out_specs=[col(s), col(k), col(1)],
        out_shape=[jax.ShapeDtypeStruct((s, c), dx_dtype), jax.ShapeDtypeStruct((k, c), F32),
                   jax.ShapeDtypeStruct((1, c), F32)],
        scratch_shapes=[pltpu.VMEM((s + 2 * CONV_HALO, LANES), F32), pltpu.VMEM((s + 2 * CONV_HALO, LANES), F32)],
        compiler_params=_params(("parallel",)),
    )(x, w, b, dout)


_NT =(((1,), (1,)), ((), ()))
_TN = (((0,), (0,)), ((), ()))


def _dot(a, b, dims=None):
    if dims is None:
        return jnp.dot(a, b, preferred_element_type=F32)
    return lax.dot_general(a, b, dims, preferred_element_type=F32)


HEAD_QUANTITIES = 4
GROUPS_PER_STEP = 8


def _scan_tables(n_heads):
    j_heads = n_heads // N_GROUPS
    used = 3 * HEAD_QUANTITIES * j_heads
    assert used <= LANES and j_heads % 2 == 0 and N_GROUPS % GROUPS_PER_STEP == 0
    gw = j_heads * HEAD_DIM
    r = jnp.arange(LANES)[:, None]

    def expand(quantity, width):
        head_of_lane = jnp.arange(j_heads * width)[None] // width
        return ((r // (3 * j_heads) == quantity) & (r % j_heads == head_of_lane) & (r < used)).astype(BF16)

    sel_cols = (jnp.arange(gw)[:, None] // HEAD_DIM == jnp.arange(LANES)[None]).astype(BF16)
    h2 = 2 * n_heads
    rows = jnp.arange(3 * HEAD_QUANTITIES * h2)
    head = rows % n_heads
    col = ((rows % h2) // n_heads * N_GROUPS + head // j_heads) * LANES + (rows // h2) * j_heads + head % j_heads
    route = (col[:, None] == jnp.arange(2 * N_GROUPS * LANES)[None]).astype(BF16)
    return {"ex_dt": expand(0, HEAD_DIM), "ex_gam": expand(1, CHUNK), "ex_din": expand(2, HEAD_DIM),
            "ex_dst": expand(3, HEAD_DIM), "sel_cols": sel_cols, "route": route}


def _chunk_decay(etot, n_heads, direction):
    j_heads = n_heads // N_GROUPS
    ed = etot[:, :, direction * n_heads:(direction + 1) * n_heads]
    per_group = jnp.pad(ed.reshape(ed.shape[0], 8, N_GROUPS, j_heads), ((0, 0), (0, 0), (0, 0), (0, LANES - j_heads)))
    return jnp.repeat(ed, HEAD_DIM, axis=2), per_group.reshape(ed.shape[0], 8, N_GROUPS * LANES)


def _scan_specs(reverse_order, direction, nc, j_heads, d_ssm):
    gps = GROUPS_PER_STEP
    gw = j_heads * HEAD_DIM
    b_off = d_ssm // (gps * D_STATE)
    c_off = b_off + N_GROUPS // gps
    d_off = direction * (N_GROUPS // gps)
    zz = (lambda z: nc - 1 - z) if reverse_order else (lambda z: z)
    const = lambda shape: pl.BlockSpec(shape, lambda g, z: (0,) * len(shape))
    return {
        "xs": pl.BlockSpec((CHUNK, gps * gw), lambda g, z: (zz(z), g)),
        "b": pl.BlockSpec((CHUNK, gps * D_STATE), lambda g, z: (zz(z), b_off + g)),
        "c": pl.BlockSpec((CHUNK, gps * D_STATE), lambda g, z: (zz(z), c_off + g)),
        "q": pl.BlockSpec((CHUNK, gps * LANES), lambda g, z: (zz(z), d_off + g)),
        "gam_t": pl.BlockSpec((gps * j_heads, CHUNK), lambda g, z: (d_off + g, zz(z))),
        "etot_x": pl.BlockSpec((1, 8, gps * gw), lambda g, z: (zz(z), 0, g)),
        "etot_g": pl.BlockSpec((1, 8, gps * LANES), lambda g, z: (zz(z), 0, g)),
        "state": pl.BlockSpec((gps, 1, D_STATE, gw), lambda g, z: (g, zz(z), 0, 0)),
        "grp": pl.BlockSpec((CHUNK, gps * D_STATE), lambda g, z: (zz(z), g)),
        "ex": const((LANES, gw)), "ex_gam": const((LANES, j_heads * CHUNK)), "sel": const((gw, LANES)),
    }


def _scan_masks(reverse):
    li = lax.broadcasted_iota(jnp.int32, (CHUNK, CHUNK), 0)
    si = lax.broadcasted_iota(jnp.int32, (CHUNK, CHUNK), 1)
    mask = (li <= si) if reverse else (li >= si)
    mask_t = (si <= li) if reverse else (si >= li)
    return li, si, mask, mask_t, si < HEAD_DIM


def _plan_phase(plan, phase, n_steps, ins, outs, sems):
    if plan is None or phase not in plan.phases:
        return
    ids = [pl.program_id(i) for i in range(len(n_steps))]
    first = {"start": 0, "middle": n_steps[0] // 2, "end": n_steps[0] - 1}[phase]
    when = ids[0] == first
    for i, n in zip(ids[1:], n_steps[1:]):
        when = when & (i == (n - 1 if phase == "end" else 0))

    @pl.when(when)
    def _():
        plan.phases[phase](ins, outs, *sems)


def _plan_call_parts(plan):
    if plan is None:
        return [], [], [], [], [], ("parallel", "arbitrary")
    n_in, n_out = len(plan.operands), len(plan.out_shapes)
    return plan.operands, [ANY] * n_in, plan.out_shapes, [ANY] * n_out, plan.sems(), ("arbitrary", "arbitrary")


def _scan_fwd(name, xbc_c, q_all, gam_t, etot_x, tb, *, direction, d_ssm, plan=None):
    s = xbc_c.shape[0]
    nc = s // CHUNK
    j_heads = tb["ex_dt"].shape[1] // HEAD_DIM
    gw = j_heads * HEAD_DIM
    gps = GROUPS_PER_STEP
    reverse = direction == 1
    sp = _scan_specs(reverse, direction, nc, j_heads, d_ssm)
    p_ops, p_in_specs, p_out_shapes, p_out_specs, p_scratch, semantics = _plan_call_parts(plan)
    n_pi, n_po = len(p_ops), len(p_out_shapes)
    n_steps = (N_GROUPS // gps, nc)

    def kern(*refs):
        xs_ref, b_ref, c_ref, q_ref, gamt_ref, etx_ref, exdt_ref, exgam_ref, exdin_ref, exdst_ref = refs[:10]
        plan_ins = refs[10:10 + n_pi]
        y_ref, hs_ref = refs[10 + n_pi:12 + n_pi]
        plan_outs = refs[12 + n_pi:12 + n_pi + n_po]
        h_ref = refs[12 + n_pi + n_po]
        plan_sems = refs[13 + n_pi + n_po:]
        _plan_phase(plan, "start", n_steps, plan_ins, plan_outs, plan_sems)
        _plan_phase(plan, "middle", n_steps, plan_ins, plan_outs, plan_sems)

        @pl.when(pl.program_id(1) == 0)
        def _():
            h_ref[...] = jnp.zeros_like(h_ref)

        _, _, mask, _, lo = _scan_masks(reverse)
        for gi in range(gps):
            bb = b_ref[:, gi * D_STATE:(gi + 1) * D_STATE].astype(BF16)
            cb = c_ref[:, gi * D_STATE:(gi + 1) * D_STATE].astype(BF16)
            cbt = _dot(cb, bb, _NT)
            q = q_ref[:, gi * LANES:(gi + 1) * LANES]
            dtx, dinx, dstx = _dot(q, exdt_ref[...]), _dot(q, exdin_ref[...]), _dot(q, exdst_ref[...])
            gcol = _dot(q, exgam_ref[...])
            xdt = xs_ref[:, gi * gw:(gi + 1) * gw].astype(F32) * dtx
            ht = h_ref[gi]
            y_off = _dot(cb, ht.astype(BF16)) * dinx
            hs_ref[gi, 0] = ht
            for p in range(j_heads // 2):
                lanes = slice(p * CHUNK, (p + 1) * CHUNK)
                x2 = xdt[:, lanes]
                acc = y_off[:, lanes]
                for idx, j in enumerate((2 * p, 2 * p + 1)):
                    g_row = gamt_ref[pl.ds(gi * j_heads + j, 1), :]
                    decay = jnp.exp(jnp.where(mask, gcol[:, j * CHUNK:(j + 1) * CHUNK] - g_row, NEG))
                    x_head = jnp.where(lo if idx == 0 else jnp.logical_not(lo), x2, 0.0).astype(BF16)
                    acc = acc + _dot((cbt * decay).astype(BF16), x_head)
                y_ref[:, gi * gw + p * CHUNK:gi * gw + (p + 1) * CHUNK] = acc.astype(y_ref.dtype)
            h_ref[gi] = ht * etx_ref[0, 0:1, gi * gw:(gi + 1) * gw] + _dot(bb, (xdt * dstx).astype(BF16), _TN)
        _plan_phase(plan, "end", n_steps, plan_ins, plan_outs, plan_sems)

    res = pl.pallas_call(
        kern, name=name, grid=n_steps,
        in_specs=[sp["xs"], sp["b"], sp["c"], sp["q"], sp["gam_t"], sp["etot_x"], sp["ex"], sp["ex_gam"], sp["ex"],
                  sp["ex"]] + p_in_specs,
        out_specs=[sp["xs"], sp["state"]] + p_out_specs,
        out_shape=[jax.ShapeDtypeStruct((s, d_ssm), BF16),
                   jax.ShapeDtypeStruct((N_GROUPS, nc, D_STATE, gw), F32)] + p_out_shapes,
        scratch_shapes=[pltpu.VMEM((gps, D_STATE, gw), F32)] + p_scratch,
        compiler_params=_params(semantics),
    )(xbc_c, xbc_c, xbc_c, q_all, gam_t, etot_x, tb["ex_dt"], tb["ex_gam"], tb["ex_din"], tb["ex_dst"], *p_ops)
    return res[0], res[1], list(res[2:])


def _scan_bwd(name, xbc_c, dy, hs, q_all, gam_t, etot_x, etot_g, tb, *, direction, d_ssm, plan=None):
    s = xbc_c.shape[0]
    nc = s // CHUNK
    j_heads = tb["ex_dt"].shape[1] // HEAD_DIM
    gw = j_heads * HEAD_DIM
    gps = GROUPS_PER_STEP
    reverse = direction == 1
    sp = _scan_specs(not reverse, direction, nc, j_heads, d_ssm)
    hp = lax.Precision.HIGHEST
    p_ops, p_in_specs, p_out_shapes, p_out_specs, p_scratch, semantics = _plan_call_parts(plan)
    n_pi, n_po = len(p_ops), len(p_out_shapes)
    n_steps = (N_GROUPS // gps, nc)

    def kern(*refs):
        (xs_ref, b_ref, c_ref, dy_ref, hs_ref, q_ref, gamt_ref, etg_ref, etx_ref, exdt_ref, exgam_ref, exdin_ref,
         exdst_ref, sel_ref) = refs[:14]
        plan_ins = refs[14:14 + n_pi]
        dxs_ref, db_ref, dc_ref, ddt_ref, da_ref = refs[14 + n_pi:19 + n_pi]
        plan_outs = refs[19 + n_pi:19 + n_pi + n_po]
        dh_ref, tmp_ref = refs[19 + n_pi + n_po:21 + n_pi + n_po]
        plan_sems = refs[21 + n_pi + n_po:]
        _plan_phase(plan, "start", n_steps, plan_ins, plan_outs, plan_sems)
        _plan_phase(plan, "middle", n_steps, plan_ins, plan_outs, plan_sems)

        @pl.when(pl.program_id(1) == 0)
        def _():
            dh_ref[...] = jnp.zeros_like(dh_ref)

        li, si, mask, mask_t, lo = _scan_masks(reverse)
        sel = sel_ref[...]
        incl = ((si <= li) if reverse else (si >= li)).astype(F32)
        excl = ((si > li) if reverse else (si < li)).astype(F32)
        for gi in range(gps):
            grp_lanes = slice(gi * D_STATE, (gi + 1) * D_STATE)
            bb = b_ref[:, grp_lanes].astype(BF16)
            cb = c_ref[:, grp_lanes].astype(BF16)
            cbt = _dot(cb, bb, _NT)
            cbt_t = _dot(bb, cb, _NT)
            q = q_ref[:, gi * LANES:(gi + 1) * LANES]
            dtx, dinx, dstx = _dot(q, exdt_ref[...]), _dot(q, exdin_ref[...]), _dot(q, exdst_ref[...])
            gcol = _dot(q, exgam_ref[...])
            x_all = xs_ref[:, gi * gw:(gi + 1) * gw].astype(F32)
            dy_all = dy_ref[:, gi * gw:(gi + 1) * gw].astype(F32)
            xdt = x_all * dtx
            xb = xdt.astype(BF16)
            ht = hs_ref[gi, 0]
            hb = ht.astype(BF16)
            dht = dh_ref[gi]
            dhb = dht.astype(BF16)
            y_off = _dot(cb, hb) * dinx
            dx_off = _dot(bb, dhb) * dstx
            dyd = (dy_all * dinx).astype(BF16)
            xd = (xdt * dstx).astype(BF16)
            dc_acc = _dot(dyd, hb, _NT)
            db_acc = _dot(xd, dhb, _NT)
            dh_ref[gi] = dht * etx_ref[0, 0:1, gi * gw:(gi + 1) * gw] + _dot(cb, dyd, _TN)
            q_cols = _dot((dy_all * y_off).astype(BF16), sel)
            c_cols = _dot((xdt * dx_off).astype(BF16), sel)
            through = _sum0(_dot((dht * ht).astype(BF16), sel))
            dcbt = jnp.zeros((CHUNK, CHUNK), F32)
            for p in range(j_heads // 2):
                lanes = slice(p * CHUNK, (p + 1) * CHUNK)
                out_lanes = slice(gi * gw + p * CHUNK, gi * gw + (p + 1) * CHUNK)
                x2b = xb[:, lanes]
                dy2 = dy_all[:, lanes]
                acc = dx_off[:, lanes]
                for idx, j in enumerate((2 * p, 2 * p + 1)):
                    gc = gcol[:, j * CHUNK:(j + 1) * CHUNK]
                    gr = gamt_ref[pl.ds(gi * j_heads + j, 1), :]
                    decay = jnp.exp(jnp.where(mask, gc - gr, NEG))
                    decay_t = jnp.exp(jnp.where(mask_t, gr - gc, NEG))
                    dy_head = jnp.where(lo if idx == 0 else jnp.logical_not(lo), dy2, 0.0).astype(BF16)
                    acc = acc + _dot((cbt_t * decay_t).astype(BF16), dy_head)
                    dm = decay * _dot(dy_head, x2b, _NT)
                    dcbt = dcbt + dm
                    e = (cbt * dm).astype(BF16)
                    in_lane_j = si == j
                    q_cols = (q_cols + jnp.where(in_lane_j, jnp.sum(e.astype(F32), axis=1, keepdims=True), 0.0)
                              - _dot(e, jnp.where(in_lane_j, 1.0, 0.0).astype(BF16), _TN))
                dxs_ref[:, out_lanes] = (acc * dtx[:, lanes]).astype(dxs_ref.dtype)
                tmp_ref[:, lanes] = acc * x_all[:, lanes]
            dcb = dcbt.astype(BF16)
            dc_ref[:, grp_lanes] = (dc_acc + _dot(dcb, bb)).astype(dc_ref.dtype)
            db_ref[:, grp_lanes] = (db_acc + _dot(dcb, cb, _TN)).astype(db_ref.dtype)
            ddt_ref[:, gi * LANES:(gi + 1) * LANES] = _dot(tmp_ref[...].astype(BF16), sel)
            da_ref[:, gi * LANES:(gi + 1) * LANES] = (
                jnp.dot(incl, q_cols, preferred_element_type=F32, precision=hp)
                + jnp.dot(excl, c_cols, preferred_element_type=F32, precision=hp)
                + through * etg_ref[0, 0:1, gi * LANES:(gi + 1) * LANES])
        _plan_phase(plan, "end", n_steps, plan_ins, plan_outs, plan_sems)

    gn = N_GROUPS * D_STATE
    res = pl.pallas_call(
        kern, name=name, grid=n_steps,
        in_specs=[sp["xs"], sp["b"], sp["c"], sp["xs"], sp["state"], sp["q"], sp["gam_t"], sp["etot_g"], sp["etot_x"],
                  sp["ex"], sp["ex_gam"], sp["ex"], sp["ex"], sp["sel"]] + p_in_specs,
        out_specs=[sp["xs"], sp["grp"], sp["grp"], sp["grp"], sp["grp"]] + p_out_specs,
        out_shape=[jax.ShapeDtypeStruct((s, d_ssm), BF16), jax.ShapeDtypeStruct((s, gn), BF16),
                   jax.ShapeDtypeStruct((s, gn), BF16), jax.ShapeDtypeStruct((s, gn), F32),
                   jax.ShapeDtypeStruct((s, gn), F32)] + p_out_shapes,
        scratch_shapes=[pltpu.VMEM((gps, D_STATE, gw), F32), pltpu.VMEM((CHUNK, gw), F32)] + p_scratch,
        compiler_params=_params(semantics),
    )(xbc_c, xbc_c, xbc_c, dy, hs, q_all, gam_t, etot_g, etot_x, tb["ex_dt"], tb["ex_gam"], tb["ex_din"], tb["ex_dst"],
      tb["sel_cols"], *p_ops)
    return res[:5], list(res[5:])


def _dt_prepare(dt_raw, dt_bias, a_neg, route):
    s, h2 = dt_raw.shape
    n_heads = h2 // 2
    qw = route.shape[1]

    def kern(raw_ref, bias_ref, a_ref, route_ref, dt_ref, q_ref, gamt_ref, etot_ref):
        dt = _softplus(raw_ref[...] + bias_ref[...])
        a = dt * a_ref[...]
        li = lax.broadcasted_iota(jnp.int32, (CHUNK, CHUNK), 0)
        si = lax.broadcasted_iota(jnp.int32, (CHUNK, CHUNK), 1)
        tri = (li >= si).astype(F32)
        cs = jnp.dot(tri, a, preferred_element_type=F32, precision=lax.Precision.HIGHEST)
        tot = _sum0(a)
        fwd = lax.broadcasted_iota(jnp.int32, (CHUNK, h2), 1) < n_heads
        gam = jnp.where(fwd, cs, a - cs)
        din = jnp.where(fwd, jnp.exp(cs), jnp.exp(tot + gam))
        dst = jnp.where(fwd, jnp.exp(tot - cs), jnp.exp(cs - a))
        pieces = []
        for v in (dt, gam, din, dst):
            hi = v.astype(BF16)
            rest = v - hi.astype(F32)
            mid = rest.astype(BF16)
            pieces += [hi, mid, (rest - mid.astype(F32)).astype(BF16)]
        q_ref[...] = _dot(jnp.concatenate(pieces, axis=1), route_ref[...]).astype(BF16)
        dt_ref[...] = dt
        gamt_ref[...] = gam.T
        etot_ref[...] = jnp.broadcast_to(jnp.exp(tot), (8, h2))

    nc = s // CHUNK
    rows = lambda w: pl.BlockSpec((CHUNK, w), lambda i: (i, 0))
    whole = lambda a: pl.BlockSpec(a.shape, lambda i: (0, 0))
    return pl.pallas_call(
        kern, name="dt_prepare", grid=(nc,),
        in_specs=[rows(h2), whole(dt_bias), whole(a_neg), whole(route)],
        out_specs=[rows(h2), rows(qw), pl.BlockSpec((h2, CHUNK), lambda i: (0, i)), pl.BlockSpec((8, h2), lambda i: (i, 0))],
        out_shape=[jax.ShapeDtypeStruct((s, h2), F32), jax.ShapeDtypeStruct((s, qw), BF16),
                   jax.ShapeDtypeStruct((h2, s), F32), jax.ShapeDtypeStruct((nc * 8, h2), F32)],
        compiler_params=_params(("parallel",)),
    )(dt_raw, dt_bias, a_neg, route)


def _dt_backward(da_dirs, ddt_dirs, dt, dt_raw, dt_bias, a_neg):
    h2 = dt.shape[1]
    n_heads = h2 // 2
    j_heads = n_heads // N_GROUPS
    lane = jnp.arange(N_GROUPS * LANES)[:, None]
    head = (lane // LANES) * j_heads + lane % LANES
    pick = [((lane % LANES < j_heads) & (head + direction * n_heads == jnp.arange(h2)[None])).astype(BF16)
            for direction in (0, 1)]

    def compact(wide_f, wide_r, pick_f, pick_r):
        total = 0.0
        for wide, sel in ((wide_f, pick_f), (wide_r, pick_r)):
            hi = wide.astype(BF16)
            total = total + _dot(hi, sel.astype(BF16)) + _dot((wide - hi.astype(F32)).astype(BF16), sel.astype(BF16))
        return total

    def body(da_f, da_r, ddt_f, ddt_r, dtv, raw, pick_f, pick_r, bias, a_head):
        dav = compact(da_f, da_r, pick_f, pick_r)
        ddtv = compact(ddt_f, ddt_r, pick_f, pick_r)
        draw = (ddtv + dav * a_head) * _sigmoid(raw + bias)
        return draw, _sum0(draw), _sum0(dav * dtv) * a_head
    return _rowwise("dt_backward", body, [*da_dirs, *ddt_dirs, dt, dt_raw], [*pick, dt_bias, a_neg], [(h2, BF16)], [h2, h2])


def _rms(x):
    r = lax.rsqrt(_mean1(x * x) + EPS)
    return x * r, r


def _rms_bwd(dy, y, r):
    return r * (dy - y * _mean1(dy * y))


def _norm_mod_fwd(name, x, g, sc, sh):
    def body(xv, gv, scv, shv):
        y, _ = _rms(xv)
        return ((y * gv) * (1.0 + scv) + shv,)
    return _rowwise(name, body, [x], [g, sc, sh], [(x.shape[1], BF16)])[0]


def _norm_mod_bwd(name, x, dh, dpass, g, sc, plan=None):
    d = x.shape[1]

    def body(xv, dhv, dpv, gv, scv):
        y, r = _rms(xv)
        dn = dhv * (1.0 + scv)
        dx = _rms_bwd(dn * gv, y, r) + dpv
        return dx, _sum0(dn * y), _sum0(dhv * (y * gv)), _sum0(dhv)
    return _rowwise(name, body, [x, dh, dpass], [g, sc], [(d, F32)], [d, d, d], plan=plan)


def _gated_residual_fwd(name, x, m, gate, gp):
    def body(xv, mv, gatev, gpv):
        y, _ = _rms(mv)
        return (xv + gatev * (y * gpv),)
    return _rowwise(name, body, [x, m], [gate, gp], [(x.shape[1], F32)])[0]


def _gated_residual_bwd(name, m, dx1, gate, gp):
    d = m.shape[1]

    def body(mv, dv, gatev, gpv):
        y, r = _rms(mv)
        dn = dv * gatev
        return _rms_bwd(dn * gpv, y, r), _sum0(dv * (y * gpv)), _sum0(dn * y)
    return _rowwise(name, body, [m, dx1], [gate, gp], [(d, BF16)], [d, d])


def _final_residual_loss(x1, f, tgt, gate, gp):
    d = x1.shape[1]

    def body(xv, fv, tv, gatev, gpv):
        y, r = _rms(fv)
        n = y * gpv
        err = xv + gatev * n - tv
        dx2 = err * (1.0 / d)
        dn = dx2 * gatev
        sq = jnp.sum(_sum0(err * err), axis=1, keepdims=True)
        return dx2, _rms_bwd(dn * gpv, y, r), jnp.broadcast_to(sq, (1, LANES)), _sum0(dx2 * n), _sum0(dn * y)
    return _rowwise("final_residual_loss", body, [x1, f, tgt], [gate, gp], [(d, F32), (d, BF16)], [LANES, d, d])


def _swiglu_fwd(gu):
    f = gu.shape[1] // 2

    def body(v):
        return (_silu(v[:, :f]) * v[:, f:],)
    return _rowwise("swiglu_fwd", body, [gu], [], [(f, BF16)])[0]


def _swiglu_bwd(gu, dact):
    f = gu.shape[1] // 2

    def body(v, dv):
        gt, up = v[:, :f], v[:, f:]
        return (jnp.concatenate([dv * up * _dsilu(gt), dv * _silu(gt)], axis=1),)
    return _rowwise("swiglu_bwd", body, [gu, dact], [], [(2 * f, BF16)])[0]


def _glu_fwd(glu_in, b_glu):
    c = glu_in.shape[1] // 2

    def body(v, bv):
        t = v + bv
        return (t[:, :c] * _sigmoid(t[:, c:]),)
    return _rowwise("glu_fwd", body, [glu_in], [b_glu], [(c, F32)])[0]


def _glu_bwd(glu_in, b_glu, du0):
    c = glu_in.shape[1] // 2

    def body(v, dv, bv):
        t = v + bv
        a, s = t[:, :c], _sigmoid(t[:, c:])
        dg = jnp.concatenate([dv * s, dv * a * s * (1.0 - s)], axis=1)
        return dg, _sum0(dg)
    return _rowwise("glu_bwd", body, [glu_in, du0], [b_glu], [(2 * c, BF16)], [2 * c])


def _ln_parts(u1):
    xc = u1 - _mean1(u1)
    r = lax.rsqrt(_mean1(xc * xc) + EPS)
    return xc * r, r


def _ln_silu_fwd(u1, ln_g, ln_b):
    def body(v, gv, bv):
        yh, _ = _ln_parts(v)
        return (_silu(yh * gv + bv),)
    return _rowwise("ln_silu_fwd", body, [u1], [ln_g, ln_b], [(u1.shape[1], BF16)])[0]


def _ln_silu_bwd(u1, du, ln_g, ln_b):
    d = u1.shape[1]

    def body(v, dv, gv, bv):
        yh, r = _ln_parts(v)
        dl = dv * _dsilu(yh * gv + bv)
        dyh = dl * gv
        du1 = r * (dyh - _mean1(dyh) - yh * _mean1(dyh * yh))
        return du1, _sum0(dl * yh), _sum0(dl)
    return _rowwise("ln_silu_bwd", body, [u1, du], [ln_g, ln_b], [(d, F32)], [d, d])


def _gate_merge_fwd(y_a, y_b, gl, b_gate):
    d = y_a.shape[1]

    def body(ya, yb, glv, bv):
        s = _sigmoid(glv + bv)
        return (s[:, :d] * ya + s[:, d:] * yb,)
    return _rowwise("gate_merge_fwd", body, [y_a, y_b, gl], [b_gate], [(d, BF16)])[0]


def _gate_merge_bwd(dmixin, y_a, y_b, gl, b_gate):
    d = y_a.shape[1]

    def body(dv, ya, yb, glv, bv):
        s = _sigmoid(glv + bv)
        sa, sb = s[:, :d], s[:, d:]
        dya, dyb = dv * sa, dv * sb
        dgl = jnp.concatenate([dv * ya * sa * (1.0 - sa), dv * yb * sb * (1.0 - sb)], axis=1)
        return dya, dyb, dgl, _sum0(dgl), _sum0(dyb)
    return _rowwise("gate_merge_bwd", body, [dmixin, y_a, y_b, gl], [b_gate],
                    [(d, BF16), (d, BF16), (2 * d, BF16)], [2 * d, d])


def _group_slices(d_ssm):
    gw = d_ssm // N_GROUPS
    return [slice(g * gw, (g + 1) * gw) for g in range(N_GROUPS)]


def _gated_norm_fwd(y_f, y_b, xbc_c, z, d_skip_x, g_ssm):
    d_ssm = y_f.shape[1]

    def body(yf, yb, xs, zv, dsk, gv):
        y = yf + yb + dsk * xs
        v = y * _silu(zv)
        outs = []
        for sl in _group_slices(d_ssm):
            w, _ = _rms(v[:, sl])
            outs.append(w)
        return y, jnp.concatenate(outs, axis=1) * gv
    return _rowwise("gated_norm_fwd", body, [y_f, y_b, (xbc_c, d_ssm, 0), z], [d_skip_x, g_ssm],
                    [(d_ssm, BF16), (d_ssm, BF16)])


def _gated_norm_bwd(y, z, dyn, xbc_c, d_skip_x, g_ssm, plan=None):
    d_ssm = y.shape[1]

    def body(yv, zv, dv, xs, dsk, gv):
        sz = _silu(zv)
        v = yv * sz
        dw = dv * gv
        dvs, ws = [], []
        for sl in _group_slices(d_ssm):
            w, r = _rms(v[:, sl])
            ws.append(w)
            dvs.append(_rms_bwd(dw[:, sl], w, r))
        dvv = jnp.concatenate(dvs, axis=1)
        dy = dvv * sz
        return dy, dvv * yv * _dsilu(zv), _sum0(dv * jnp.concatenate(ws, axis=1)), _sum0(dy * xs)
    return _rowwise("gated_norm_bwd", body, [y, z, dyn, (xbc_c, d_ssm, 0)], [d_skip_x, g_ssm],
                    [(d_ssm, BF16), (d_ssm, BF16)], [d_ssm, d_ssm], plan=plan)


def _ssd_grad_merge(dxs_f, dxs_b, dy, db_f, db_b, dc_f, dc_b, d_skip_x):
    d_ssm = dy.shape[1]
    width = d_ssm + 2 * N_GROUPS * D_STATE

    def body(xf, xb, dv, bf, bb, cf, cbv, dsk):
        return (jnp.concatenate([xf + xb + dsk * dv, bf + bb, cf + cbv], axis=1),)
    return _rowwise("ssd_grad_merge", body, [dxs_f, dxs_b, dy, db_f, db_b, dc_f, dc_b], [d_skip_x], [(width, BF16)])[0]


def _adamw(name, w, g, m, v):
    c = w.shape[1]
    c1 = 1.0 - ADAM_B1 ** ADAM_STEP
    c2 = 1.0 - ADAM_B2 ** ADAM_STEP

    def body(wv, gv, mv, vv):
        mn = ADAM_B1 * mv + (1.0 - ADAM_B1) * gv
        vn = ADAM_B2 * vv + (1.0 - ADAM_B2) * (gv * gv)
        delta = -ADAM_LR * ((mn / c1) / (jnp.sqrt(vn / c2) + ADAM_EPS) + ADAM_WD * wv)
        return delta, mn, vn
    return _rowwise(name, body, [w, g, m, v], [], [(c, F32)] * 3)


def _local_step(x, tgt, mod, wts, sm, late=None):
    s, d = x.shape
    d_ssm = 2 * d
    n_heads = d_ssm // HEAD_DIM
    d_xbc = d_ssm + 2 * N_GROUPS * D_STATE
    sec = [0, d_ssm, d_ssm + d_xbc, d_ssm + d_xbc + 2 * n_heads, d_ssm + d_xbc + 2 * n_heads + 2 * d]
    sec.append(sec[-1] + 2 * d)
    sh1, sc1, g1, sh2, sc2, g2 = [mod[:, i * d:(i + 1) * d] for i in range(N_MOD)]
    win_t = wts["w_in_t"]
    sections = [(nm, (sec[i], sec[i + 1] - sec[i])) for i, nm in enumerate(("z", "xbc", "dt", "glu", "gate"))]

    h1 = _norm_mod_fwd("pre_mix_norm", x, sm["g_pre_mix"], sc1, sh1)
    z, xbc, dt_raw, glu_in, gate_l = [
        _matmul(f"proj_{nm}", h1, win_t, tb=True, b_rows=rows, out_dtype=F32 if nm == "dt" else BF16)
        for nm, rows in sections]
    plan_c = plan_f = plan_r = None
    if late is not None:
        plan_c = _gather_plan([late[n] for n in MATRICES[1:4]])
        plan_f = _gather_plan([late["w_down"]])
        plan_r = _gather_plan([late["w_gate_up"]])
    xbc_c = _dwconv_fwd("ssm_conv_fwd", xbc, sm["w_conv_ssm"], sm["b_conv_ssm"], silu=True, out_dtype=BF16, plan=plan_c)
    if late is not None:
        xbc_c, got_c = xbc_c
    tables = _scan_tables(n_heads)
    dt, q_all, gam_t, etot = _dt_prepare(dt_raw, sm["dt_bias"], sm["a_neg"], tables["route"])
    etot = etot.reshape(s // CHUNK, 8, 2 * n_heads)
    (etx_f, etg_f), (etx_r, etg_r) = [_chunk_decay(etot, n_heads, direction) for direction in (0, 1)]
    y_f, hs_f, got_f = _scan_fwd("ssd_fwd_f", xbc_c, q_all, gam_t, etx_f, tables, direction=0, d_ssm=d_ssm, plan=plan_f)
    y_r, hs_r, got_r = _scan_fwd("ssd_fwd_r", xbc_c, q_all, gam_t, etx_r, tables, direction=1, d_ssm=d_ssm, plan=plan_r)
    if late is not None:
        wts = dict(wts, w_ssm_out=got_c[0].reshape(-1, d), w_conv_out=got_c[1].reshape(-1, d),
                   w_mix_out=got_c[2].reshape(-1, d), w_down=got_f[0].reshape(-1, d), w_gate_up=got_r[0])
    y_ssd, yn = _gated_norm_fwd(y_f, y_r, xbc_c, z, sm["d_skip_x"], sm["g_ssm_norm"])
    y_a = _matmul("ssm_out", yn, wts["w_ssm_out"])
    u0 = _glu_fwd(glu_in, sm["b_glu"])
    u1 = _dwconv_fwd("dw_conv_fwd", u0, sm["w_dw"], sm["b_dw"], silu=False)
    u = _ln_silu_fwd(u1, sm["ln_g"], sm["ln_b"])
    y_b = _matmul("conv_out", u, wts["w_conv_out"], bias=sm["b_conv_out"])
    mixin = _gate_merge_fwd(y_a, y_b, gate_l, sm["b_gate"])
    mix = _matmul("mix_out", mixin, wts["w_mix_out"])
    x1 = _gated_residual_fwd("post_mix_residual", x, mix, g1, sm["g_post_mix"])
    h2 = _norm_mod_fwd("pre_ffn_norm", x1, sm["g_pre_ffn"], sc2, sh2)
    gu = _matmul("ffn_gate_up", h2, wts["w_gate_up"], b_blocks=N_CHIPS, out_dtype=BF16)
    act = _swiglu_fwd(gu)
    f = _matmul("ffn_down", act, wts["w_down"])

    dx2, df, sq, d_g2, d_gpf = _final_residual_loss(x1, f, tgt, g2, sm["g_post_ffn"])
    dact = _matmul("d_act", df, wts["w_down"], tb=True, out_dtype=BF16)
    g_w_down = _matmul("g_w_down", act, df, ta=True, out_dtype=BF16)
    dgu = _swiglu_bwd(gu, dact)
    dh2 = _matmul("d_h2", dgu, wts["w_gate_up"], tb=True, b_blocks=N_CHIPS)
    g_w_gate_up = _matmul("g_w_gate_up", h2, dgu, ta=True, out_dtype=BF16, out_blocks=N_CHIPS)
    dx1, d_gpre_ffn, d_sc2, d_sh2 = _norm_mod_bwd("pre_ffn_norm_bwd", x1, dh2, dx2, sm["g_pre_ffn"], sc2)
    dmix, d_g1, d_gpm = _gated_residual_bwd("post_mix_residual_bwd", mix, dx1, g1, sm["g_post_mix"])
    dmixin = _matmul("d_mixin", dmix, wts["w_mix_out"], tb=True, out_dtype=BF16)
    g_w_mix = _matmul("g_w_mix_out", mixin, dmix, ta=True, out_dtype=BF16)
    dy_a, dy_b, dgate_l, d_bgate, d_bco = _gate_merge_bwd(dmixin, y_a, y_b, gate_l, sm["b_gate"])
    du = _matmul("d_u", dy_b, wts["w_conv_out"], tb=True, out_dtype=BF16)
    g_w_co = _matmul("g_w_conv_out", u, dy_b, ta=True, out_dtype=BF16)
    du1, d_lng, d_lnb = _ln_silu_bwd(u1, du, sm["ln_g"], sm["ln_b"])
    du0, d_wdw, d_bdw = _dwconv_bwd("dw_conv_bwd", u0, sm["w_dw"], sm["b_dw"], du1, silu=False)
    dglu, d_bglu = _glu_bwd(glu_in, sm["b_glu"], du0)
    dyn = _matmul("d_yn", dy_a, wts["w_ssm_out"], tb=True, out_dtype=BF16)
    g_w_ssm = _matmul("g_w_ssm_out", yn, dy_a, ta=True, out_dtype=BF16)
    early = _by_halves([g_w_ssm.reshape(N_CHIPS, -1, d), g_w_co.reshape(N_CHIPS, -1, d), g_w_mix.reshape(N_CHIPS, -1, d),
                        g_w_gate_up, g_w_down.reshape(N_CHIPS, -1, d)])
    dy_ssd, dz, d_gssm, d_dskip_x, *from_sibling = _gated_norm_bwd(
        y_ssd, z, dyn, xbc_c, sm["d_skip_x"], sm["g_ssm_norm"], plan=None if late is None else _sibling_plan(early))
    plan_b = sums = None
    if late is not None:
        sums = _chip_sums("early", early, from_sibling)
        plan_b = _send_chips_plan(sums)
    (dxs_f, db_f, dc_f, ddt_f, da_f), received = _scan_bwd(
        "ssd_bwd_f", xbc_c, dy_ssd, hs_f, q_all, gam_t, etx_f, etg_f, tables, direction=0, d_ssm=d_ssm, plan=plan_b)
    (dxs_r, db_r, dc_r, ddt_r, da_r), _ = _scan_bwd(
        "ssd_bwd_r", xbc_c, dy_ssd, hs_r, q_all, gam_t, etx_r, etg_r, tables, direction=1, d_ssm=d_ssm)
    ddt_raw, d_dtbias, d_alog = _dt_backward((da_f, da_r), (ddt_f, ddt_r), dt, dt_raw, sm["dt_bias"], sm["a_neg"])
    dxbc_c = _ssd_grad_merge(dxs_f, dxs_r, dy_ssd, db_f, db_r, dc_f, dc_r, sm["d_skip_x"])
    dxbc, d_wconv, d_bconv = _dwconv_bwd("ssm_conv_bwd", xbc, sm["w_conv_ssm"], sm["b_conv_ssm"], dxbc_c,
                                         silu=True, dx_dtype=BF16)
    dsecs = [dz, dxbc, ddt_raw, dglu, dgate_l]
    dh1 = g_win = None
    for (nm, rows), dsec in zip(sections, dsecs):
        g_win = _matmul(f"g_w_in_{nm}", dsec, h1, ta=True, out_dtype=BF16, out_rows=(rows[0], sec[-1], g_win))
    hosts = {} if late is None else {"z": "sibling", "xbc": 0, "glu": 1, "gate": 2}
    halves_in = _by_halves([g_win.reshape(N_CHIPS, -1, d)])
    sums_in, received_in = None, [None] * 3
    for (nm, rows), dsec in zip(sections, dsecs):
        host = hosts.get(nm)
        plan_in = None if host is None else _sibling_plan(halves_in) if host == "sibling" else _send_chips_plan(sums_in, (host,))
        dh1 = _matmul(f"d_h1_{nm}", dsec, win_t, b_rows=rows, add=dh1, plan=plan_in)
        if host == "sibling":
            dh1, from_sibling_in = dh1
            sums_in = _chip_sums("w_in", halves_in, from_sibling_in)
        elif host is not None:
            dh1, (received_in[host],) = dh1
    grad_x, d_gpre_mix, d_sc1, d_sh1 = _norm_mod_bwd("pre_mix_norm_bwd", x, dh1, dx1, sm["g_pre_mix"], sc1)

    dmod = jnp.concatenate([d_sh1, d_sc1, d_g1, d_sh2, d_sc2, d_g2], axis=1)
    if late is None:
        big = {"w_in_t": g_win, "w_ssm_out": g_w_ssm, "w_conv_out": g_w_co, "w_mix_out": g_w_mix,
               "w_gate_up": g_w_gate_up, "w_down": g_w_down}
    else:
        big = {"pending": (sums_in + sums, [received_in] + [[t, t, t] for t in received])}
    small = {"g_pre_mix": d_gpre_mix, "g_post_mix": d_gpm, "w_conv_ssm": d_wconv, "b_conv_ssm": d_bconv,
             "dt_bias": d_dtbias, "a_log": d_alog, "d_skip_x": d_dskip_x, "g_ssm_norm": d_gssm, "b_glu": d_bglu,
             "w_dw": d_wdw, "b_dw": d_bdw, "ln_g": d_lng, "ln_b": d_lnb, "b_conv_out": d_bco, "b_gate": d_bgate,
             "g_pre_ffn": d_gpre_ffn, "g_post_ffn": d_gpf}
    return sq, grad_x, big, small, dmod


ANY = pl.BlockSpec(memory_space=pl.ANY)
WHOLE_VMEM = pl.BlockSpec(memory_space=pltpu.VMEM)


def _mesh_place():
    x, y, c = lax.axis_index("x"), lax.axis_index("y"), lax.axis_index("c")
    other_chips = [(1 - x, y), (x, 1 - y), (1 - x, 1 - y)]
    return x, y, c, other_chips


def _remote(src, dst, send_sems, recv_sems, k, device):
    return pltpu.make_async_remote_copy(src_ref=src, dst_ref=dst, send_sem=send_sems.at[k], recv_sem=recv_sems.at[k],
                                        device_id=device, device_id_type=MESH)


def _gather_devices(name, block):
    m_per, n = block.shape

    def body(x_ref, out_ref, send_sems, recv_sems, local_sem):
        x, y, c, chips = _mesh_place()
        me, sibling = (x, y, c), (x, y, 1 - c)

        def rows(px, py, pc):
            return out_ref.at[pl.ds((4 * px + 2 * py + pc) * m_per, m_per), :]

        def copy(k, blk, to, src=None):
            return _remote(rows(*blk) if src is None else src, rows(*blk), send_sems, recv_sems, k, to)

        mine = pltpu.make_async_copy(x_ref, rows(*me), local_sem)
        mine.start()
        first = [copy(0, me, sibling, src=x_ref)]
        first += [copy(1 + j, me, (*chip, c), src=x_ref) for j, chip in enumerate(chips)]
        for cp in first:
            cp.start()
        passed = [copy(4 + j, (*chip, c), sibling) for j, chip in enumerate(chips)]
        for j, chip in enumerate(chips):
            copy(1 + j, (*chip, c), me).wait_recv()
            passed[j].start()
        copy(0, sibling, me).wait_recv()
        for j, chip in enumerate(chips):
            copy(4 + j, (*chip, 1 - c), me).wait_recv()
        for cp in first + passed:
            cp.wait_send()
        mine.wait()

    return pl.pallas_call(
        body, name=name, out_shape=jax.ShapeDtypeStruct((N_DEV * m_per, n), block.dtype),
        in_specs=[WHOLE_VMEM], out_specs=WHOLE_VMEM,
        scratch_shapes=[pltpu.SemaphoreType.DMA((7,)), pltpu.SemaphoreType.DMA((7,)), pltpu.SemaphoreType.DMA],
        compiler_params=pltpu.CompilerParams(vmem_limit_bytes=VMEM_LIMIT),
    )(block)


class _Plan:
    def __init__(self, operands, out_shapes, copies, phases):
        self.operands, self.out_shapes, self.copies, self.phases = list(operands), list(out_shapes), copies, phases

    def sems(self):
        return [pltpu.SemaphoreType.DMA((self.copies,)), pltpu.SemaphoreType.DMA((self.copies,))]


def _run_plan(name, plan):
    n_in, n_out = len(plan.operands), len(plan.out_shapes)

    def body(*refs):
        ins, outs = refs[:n_in], refs[n_in:n_in + n_out]
        send_sems, recv_sems = refs[n_in + n_out:]
        for phase in ("start", "middle", "end"):
            if phase in plan.phases:
                plan.phases[phase](ins, outs, send_sems, recv_sems)

    return list(pl.pallas_call(body, name=name, out_shape=plan.out_shapes, in_specs=[ANY] * n_in,
                               out_specs=[ANY] * n_out, scratch_shapes=plan.sems())(*plan.operands))


def _gather_plan(shards):
    n = len(shards)

    def copies(kinds, ins, outs, send_sems, recv_sems):
        x, y, c, chips = _mesh_place()
        me = 2 * x + y
        sibling = (x, y, 1 - c)

        def half(i, h):
            hr = ins[i].shape[0] // 2
            return pl.ds(h * hr, hr)

        def block(i, j, h):
            cx, cy = chips[j]
            return outs[i].at[2 * cx + cy, half(i, h)]

        make = {
            "over_ici": lambda i, j: _remote(ins[i].at[half(i, c)], outs[i].at[me, half(i, c)], send_sems, recv_sems,
                                             6 * i + j, (*chips[j], c)),
            "arrived": lambda i, j: _remote(block(i, j, c), block(i, j, c), send_sems, recv_sems, 6 * i + j, (*chips[j], c)),
            "passed_on": lambda i, j: _remote(block(i, j, c), block(i, j, c), send_sems, recv_sems, 6 * i + 3 + j, sibling),
            "from_sibling": lambda i, j: _remote(block(i, j, 1 - c), block(i, j, 1 - c), send_sems, recv_sems,
                                                 6 * i + 3 + j, sibling),
        }
        res = []
        for kind in kinds:
            if kind == "own":
                res.append([_remote(ins[i], outs[i].at[me], send_sems, recv_sems, 6 * n + i, sibling) for i in range(n)])
            else:
                res.append([make[kind](i, j) for i in range(n) for j in range(3)])
        return res

    def start(*refs):
        over_ici, own = copies(("over_ici", "own"), *refs)
        for cp in over_ici + own:
            cp.start()

    def end(*refs):
        arrived, passed_on = copies(("arrived", "passed_on"), *refs)
        for got, fwd in zip(arrived, passed_on):
            got.wait_recv()
            fwd.start()
        from_sibling, own_in = copies(("from_sibling", "own"), *refs)
        for cp in from_sibling + own_in:
            cp.wait_recv()
        over_ici, passed_on, own_out = copies(("over_ici", "passed_on", "own"), *refs)
        for cp in over_ici + passed_on + own_out:
            cp.wait_send()

    return _Plan(shards, [jax.ShapeDtypeStruct((N_CHIPS,) + s.shape, s.dtype) for s in shards], 7 * n,
                 {"start": start, "end": end})


def _sibling_plan(grads):
    n = len(grads)

    def copies(ins, outs, send_sems, recv_sems):
        x, y, c, _ = _mesh_place()
        return [_remote(ins[i].at[j, 1 - c], outs[i].at[j], send_sems, recv_sems, N_CHIPS * i + j, (x, y, 1 - c))
                for i in range(n) for j in range(N_CHIPS)]

    def start(*refs):
        for cp in copies(*refs):
            cp.start()

    def end(*refs):
        for cp in copies(*refs):
            cp.wait_recv()
        for cp in copies(*refs):
            cp.wait_send()

    return _Plan(grads, [jax.ShapeDtypeStruct((g.shape[0],) + g.shape[2:], g.dtype) for g in grads], N_CHIPS * n,
                 {"start": start, "end": end})


def _send_chips_plan(sums, neighbours=(0, 1, 2)):
    n = len(sums)

    def copies(ins, outs, send_sems, recv_sems):
        x, y, c, chips = _mesh_place()
        return [_remote(ins[i].at[2 * chips[j][0] + chips[j][1]], outs[i].at[j], send_sems, recv_sems, 3 * i + j,
                        (*chips[j], c))
                for i in range(n) for j in neighbours]

    def start(*refs):
        for cp in copies(*refs):
            cp.start()

    def end(*refs):
        for cp in copies(*refs):
            cp.wait_recv()
        for cp in copies(*refs):
            cp.wait_send()

    return _Plan(sums, [jax.ShapeDtypeStruct((3,) + g.shape[1:], g.dtype) for g in sums], 3 * n,
                 {"start": start, "end": end})


def _exchange_halves(name, shards):
    n = len(shards)

    def body(*refs):
        outs = refs[n:2 * n]
        send_sems, recv_sems = refs[2 * n:]
        x, y, c, _ = _mesh_place()
        sibling = (x, y, 1 - c)
        remote = [_remote(outs[i].at[c], outs[i].at[c], send_sems, recv_sems, i, sibling) for i in range(n)]
        for cp in remote:
            cp.start()
        for i in range(n):
            _remote(outs[i].at[1 - c], outs[i].at[1 - c], send_sems, recv_sems, i, sibling).wait_recv()
        for cp in remote:
            cp.wait_send()

    return pl.pallas_call(
        body, name=name,
        out_shape=[jax.ShapeDtypeStruct(h.shape, h.dtype) for h in shards],
        in_specs=[ANY] * n, out_specs=[ANY] * n, input_output_aliases={i: i for i in range(n)},
        scratch_shapes=[pltpu.SemaphoreType.DMA((n,)), pltpu.SemaphoreType.DMA((n,))],
    )(*shards)


def _divisor_tile(rows, row_bytes, quantum=16):
    best = rows
    for t in range(quantum, rows + 1, quantum):
        if rows % t == 0 and 2 * t * row_bytes <= ROW_TILE_BUDGET:
            best = t
    return best


def _add_sibling(name, g4, t1):
    nb, _, hr, cols = g4.shape
    t = _divisor_tile(hr, cols * 6)

    def kern(g_ref, t_ref, o_ref):
        o_ref[0] = (g_ref[0, 0].astype(F32) + t_ref[0].astype(F32)).astype(o_ref.dtype)

    return pl.pallas_call(
        kern, name=name, grid=(nb, hr // t),
        in_specs=[pl.BlockSpec((1, 1, t, cols), lambda j, i: (j, lax.axis_index("c"), i, 0)),
                  pl.BlockSpec((1, t, cols), lambda j, i: (j, i, 0))],
        out_specs=pl.BlockSpec((1, t, cols), lambda j, i: (j, i, 0)),
        out_shape=jax.ShapeDtypeStruct((nb, hr, cols), g4.dtype),
        compiler_params=_params(("parallel", "parallel")),
    )(g4, t1)


def _add_chips(name, s1, t3):
    _, hr, cols = s1.shape
    t = _divisor_tile(hr, cols * 12)

    def kern(s_ref, t0_ref, t1_ref, t2_ref, o_ref):
        acc = s_ref[0].astype(F32)
        for t_ref in (t0_ref, t1_ref, t2_ref):
            acc = acc + t_ref[0].astype(F32)
        o_ref[0] = acc

    slot = lambda j: pl.BlockSpec((1, t, cols), functools.partial(lambda i, j: (j, i, 0), j=j))
    return pl.pallas_call(
        kern, name=name, grid=(hr // t,),
        in_specs=[pl.BlockSpec((1, t, cols), lambda i: (2 * lax.axis_index("x") + lax.axis_index("y"), i, 0)),
                  slot(0), slot(1), slot(2)],
        out_specs=pl.BlockSpec((1, t, cols), lambda i: (lax.axis_index("c"), i, 0)),
        out_shape=jax.ShapeDtypeStruct((2, hr, cols), F32),
        compiler_params=_params(("parallel",)),
    )(s1, *t3)


def _by_halves(grads):
    return [g.reshape(N_CHIPS, 2, g.shape[1] // 2, g.shape[2]) for g in grads]


def _chip_sums(tag, halves, from_sibling):
    return [_add_sibling(f"chip_sum_{tag}_{i}", g, t) for i, (g, t) in enumerate(zip(halves, from_sibling))]


def _shard_sums(sums, received):
    halves = [_add_chips(f"shard_sum_{i}", s, t) for i, (s, t) in enumerate(zip(sums, received))]
    full = _exchange_halves("grad_halves_to_sibling", halves)
    return [f.reshape(f.shape[1] * 2, f.shape[2]) for f in full]


def _pack_rows(size, width):
    return -(-size // (8 * width)) * 8


def _pack(arrays, width):
    parts = []
    for a in arrays:
        flat = a.reshape(-1).astype(F32)
        rows = _pack_rows(flat.shape[0], width)
        parts.append(jnp.pad(flat, (0, rows * width - flat.shape[0])).reshape(rows, width))
    return jnp.concatenate(parts, axis=0)


def _unpack(block, shapes, width):
    out, r = [], 0
    for shp in shapes:
        size = 1
        for s_ in shp:
            size *= s_
        rows = _pack_rows(size, width)
        out.append(block[r:r + rows].reshape(-1)[:size].reshape(shp))
        r += rows
    return out


SMALL_PARAMS = ("b_ada", "g_pre_mix", "g_post_mix", "b_conv_ssm", "dt_bias_fwd", "dt_bias_bwd", "a_log_fwd", "a_log_bwd",
                "d_skip", "g_ssm_norm", "b_glu", "b_dw", "ln_g", "ln_b", "b_conv_out", "b_gate", "g_pre_ffn", "g_post_ffn")
SHARDED_SMALL = ("w_conv_ssm", "w_dw")
MATRICES = ("w_in", "w_ssm_out", "w_conv_out", "w_mix_out", "w_gate_up", "w_down")
ALL_PARAMS = ("w_ada", "b_ada", "g_pre_mix", "g_post_mix", "w_in", "w_conv_ssm", "b_conv_ssm", "dt_bias_fwd", "dt_bias_bwd",
              "a_log_fwd", "a_log_bwd", "d_skip", "g_ssm_norm", "w_ssm_out", "b_glu", "w_dw", "b_dw", "ln_g", "ln_b",
              "w_conv_out", "b_conv_out", "b_gate", "w_mix_out", "g_pre_ffn", "g_post_ffn", "w_gate_up", "w_down")
COND_ROWS = 48
COND_CONV_ROW = 8
COND_DW_ROW = 16
MOD_ROWS = 16


def kernel(x, c, w_ada, b_ada, g_pre_mix, g_post_mix, w_in, w_conv_ssm, b_conv_ssm, dt_bias_fwd, dt_bias_bwd, a_log_fwd, a_log_bwd, d_skip, g_ssm_norm, w_ssm_out, b_glu, w_dw, b_dw, ln_g, ln_b, w_conv_out, b_conv_out, b_gate, w_mix_out, g_pre_ffn, g_post_ffn, w_gate_up, w_down, loss_target, m_w_ada, m_b_ada, m_g_pre_mix, m_g_post_mix, m_w_in, m_w_conv_ssm, m_b_conv_ssm, m_dt_bias_fwd, m_dt_bias_bwd, m_a_log_fwd, m_a_log_bwd, m_d_skip, m_g_ssm_norm, m_w_ssm_out, m_b_glu, m_w_dw, m_b_dw, m_ln_g, m_ln_b, m_w_conv_out, m_b_conv_out, m_b_gate, m_w_mix_out, m_g_pre_ffn, m_g_post_ffn, m_w_gate_up, m_w_down, v_w_ada, v_b_ada, v_g_pre_mix, v_g_post_mix, v_w_in, v_w_conv_ssm, v_b_conv_ssm, v_dt_bias_fwd, v_dt_bias_bwd, v_a_log_fwd, v_a_log_bwd, v_d_skip, v_g_ssm_norm, v_w_ssm_out, v_b_glu, v_w_dw, v_b_dw, v_ln_g, v_ln_b, v_w_conv_out, v_b_conv_out, v_b_gate, v_w_mix_out, v_g_pre_ffn, v_g_post_ffn, v_w_gate_up, v_w_down):
    given = dict(locals())
    wgt = {n: given[n][0] for n in ALL_PARAMS}
    mom = {n: given["m_" + n][0] for n in ALL_PARAMS}
    var = {n: given["v_" + n][0] for n in ALL_PARAMS}
    xs, tgt = x[0], loss_target[0]
    s, d = xs.shape
    d_ssm = 2 * d
    n_heads = d_ssm // HEAD_DIM
    d_xbc = d_ssm + 2 * N_GROUPS * D_STATE
    xi, yi, ci = lax.axis_index("x"), lax.axis_index("y"), lax.axis_index("c")
    chip = 2 * xi + yi
    dev = 2 * chip + ci
    k_conv, k_dw = wgt["w_conv_ssm"].shape[0], wgt["w_dw"].shape[0]
    xbc_shard, dw_shard = d_xbc // N_CHIPS, d // N_CHIPS

    width1 = max(d, xbc_shard)
    blk = jnp.zeros((COND_ROWS, width1), F32)
    blk = blk.at[0, :d].set(c[0])
    blk = blk.at[COND_CONV_ROW:COND_CONV_ROW + k_conv, :xbc_shard].set(wgt["w_conv_ssm"])
    blk = blk.at[COND_DW_ROW:COND_DW_ROW + k_dw, :dw_shard].set(wgt["w_dw"])
    g1 = _gather_devices("gather_cond", blk).reshape(N_DEV, COND_ROWS, width1)
    c_all = g1[:, 0, :d]
    w_conv_full = jnp.concatenate([g1[2 * k, COND_CONV_ROW:COND_CONV_ROW + k_conv, :xbc_shard] for k in range(N_CHIPS)], axis=1)
    w_dw_full = jnp.concatenate([g1[2 * k, COND_DW_ROW:COND_DW_ROW + k_dw, :dw_shard] for k in range(N_CHIPS)], axis=1)
    c_act = jnp.pad(c_all * _sigmoid(c_all), ((0, MOD_ROWS - N_DEV), (0, 0)))

    mod_part = _matmul("ada_mod", c_act, wgt["w_ada"])
    g2 = _gather_devices("gather_mod", mod_part).reshape(N_DEV, MOD_ROWS, mod_part.shape[1])
    mod_all = jnp.concatenate([g2[2 * k, :N_DEV] for k in range(N_CHIPS)], axis=1) + wgt["b_ada"][None]
    mod = lax.dynamic_slice_in_dim(mod_all, dev, 1, axis=0)

    shards = [wgt["w_in"].T.astype(BF16)] + [wgt[n].astype(BF16) for n in MATRICES[1:]]
    wts = {"w_in_t": _run_plan("gather_w_in", _gather_plan(shards[:1]))[0].reshape(-1, d)}
    late = dict(zip(MATRICES[1:], shards[1:]))
    row = lambda v: v.reshape(1, -1)
    sm = {"g_pre_mix": row(wgt["g_pre_mix"]), "g_post_mix": row(wgt["g_post_mix"]), "w_conv_ssm": w_conv_full,
          "b_conv_ssm": row(wgt["b_conv_ssm"]),
          "dt_bias": row(jnp.concatenate([wgt["dt_bias_fwd"], wgt["dt_bias_bwd"]])),
          "a_neg": row(-jnp.exp(jnp.concatenate([wgt["a_log_fwd"], wgt["a_log_bwd"]]))),
          "d_skip_x": row(jnp.repeat(wgt["d_skip"], HEAD_DIM)), "g_ssm_norm": row(wgt["g_ssm_norm"]),
          "b_glu": row(wgt["b_glu"]), "w_dw": w_dw_full, "b_dw": row(wgt["b_dw"]), "ln_g": row(wgt["ln_g"]),
          "ln_b": row(wgt["ln_b"]), "b_conv_out": row(wgt["b_conv_out"]), "b_gate": row(wgt["b_gate"]),
          "g_pre_ffn": row(wgt["g_pre_ffn"]), "g_post_ffn": row(wgt["g_post_ffn"])}

    sq, grad_x, big, small, dmod = _local_step(xs, tgt, mod, wts, sm, late=late)
    loss = lax.psum((0.5 / d) * sq[0, 0], ("x", "y", "c"))

    local_small = {"b_ada": dmod, "g_pre_mix": small["g_pre_mix"], "g_post_mix": small["g_post_mix"],
                   "b_conv_ssm": small["b_conv_ssm"], "dt_bias_fwd": small["dt_bias"][:, :n_heads],
                   "dt_bias_bwd": small["dt_bias"][:, n_heads:], "a_log_fwd": small["a_log"][:, :n_heads],
                   "a_log_bwd": small["a_log"][:, n_heads:],
                   "d_skip": jnp.sum(small["d_skip_x"].reshape(n_heads, HEAD_DIM), axis=1),
                   "g_ssm_norm": small["g_ssm_norm"], "b_glu": small["b_glu"], "b_dw": small["b_dw"],
                   "ln_g": small["ln_g"], "ln_b": small["ln_b"], "b_conv_out": small["b_conv_out"],
                   "b_gate": small["b_gate"], "g_pre_ffn": small["g_pre_ffn"], "g_post_ffn": small["g_post_ffn"],
                   "w_conv_ssm": small["w_conv_ssm"], "w_dw": small["w_dw"]}
    names = SMALL_PARAMS + SHARDED_SMALL
    pack = _pack([local_small[n] for n in names], d)
    rows_p = pack.shape[0]
    g3 = _gather_devices("gather_small_grads", pack).reshape(N_DEV, rows_p, d)
    total = _rowwise("sum_small_grads", lambda *blocks: (functools.reduce(lambda a, b: a + b, blocks),),
                     [g3[i] for i in range(N_DEV)], [], [(d, F32)])[0]
    full_shapes = [wgt[n].shape for n in SMALL_PARAMS] + [(k_conv, d_xbc), (k_dw, d)]
    summed = dict(zip(names, _unpack(total, full_shapes, d)))
    grads = {n: summed[n] for n in SMALL_PARAMS}
    grads["w_conv_ssm"] = lax.dynamic_slice_in_dim(summed["w_conv_ssm"], chip * xbc_shard, xbc_shard, axis=1)
    grads["w_dw"] = lax.dynamic_slice_in_dim(summed["w_dw"], chip * dw_shard, dw_shard, axis=1)

    dmod_all = g3[:, :N_MOD, :].reshape(N_DEV, N_MOD * d)
    ada_cols = wgt["w_ada"].shape[1]
    dmod_cols = jnp.pad(lax.dynamic_slice_in_dim(dmod_all, chip * ada_cols, ada_cols, axis=1),
                        ((0, MOD_ROWS - N_DEV), (0, 0)))
    grads["w_ada"] = _matmul("g_w_ada", c_act, dmod_cols, ta=True)

    reduced = _shard_sums(*big["pending"])
    grads["w_in"] = reduced[0].T
    for n, g in zip(MATRICES[1:], reduced[1:]):
        grads[n] = g

    delta, new_m, new_v = {}, {}, {}
    for n in ("w_ada",) + MATRICES:
        delta[n], new_m[n], new_v[n] = _adamw("adamw_" + n, wgt[n], grads[n], mom[n], var[n])
    for group, width, tag in ((SMALL_PARAMS, d, "small"), (SHARDED_SMALL, LANES, "conv")):
        shapes = [wgt[n].shape for n in group]
        packs = [_pack([src[n] for n in group], width) for src in (wgt, grads, mom, var)]
        outs = _adamw("adamw_" + tag, *packs)
        for res, o in zip((delta, new_m, new_v), outs):
            res.update(zip(group, _unpack(o, shapes, width)))

    lead = lambda a: a[None]
    return (loss, grad_x[None], *[lead(grads[n]) for n in ALL_PARAMS], *[lead(delta[n]) for n in ALL_PARAMS],
            *[lead(new_m[n]) for n in ALL_PARAMS], *[lead(new_v[n]) for n in ALL_PARAMS])
```

```python
import functools

import jax
import jax.numpy as jnp
from jax import lax
from jax.experimental import pallas as pl
from jax.experimental.pallas import tpu as pltpu

F32 = jnp.float32
BF16 = jnp.bfloat16

N_GROUPS = 8
HEAD_DIM = 64
D_STATE = 128
CHUNK = 128
EPS = 1e-6
N_MOD = 6
ADAM_LR = 0.001
ADAM_B1 = 0.9
ADAM_B2 = 0.999
ADAM_EPS = 1e-08
ADAM_WD = 0.01
ADAM_STEP = 10

V7X_VMEM_BYTES = 64 * 1024 * 1024
VMEM_LIMIT = V7X_VMEM_BYTES - 8 * 1024 * 1024
ROW_TILE_BUDGET = 20 * 1024 * 1024
LANES = 128
NEG = -1e30
MESH = pl.DeviceIdType.MESH
N_CHIPS = 4
N_DEV = 8


def _params(sem):
    return pltpu.CompilerParams(dimension_semantics=sem, vmem_limit_bytes=VMEM_LIMIT)


def _sigmoid(x):
    return 1.0 / (1.0 + jnp.exp(-x))


def _silu(x):
    return x * _sigmoid(x)


def _dsilu(x):
    s = _sigmoid(x)
    return s * (1.0 + x * (1.0 - s))


def _softplus(x):
    return jnp.maximum(x, 0.0) + jnp.log(1.0 + jnp.exp(-jnp.abs(x)))


def _sum0(a):
    return jnp.sum(a, axis=0, keepdims=True)


def _mean1(a):
    return jnp.mean(a, axis=1, keepdims=True)


def _rowwise(name, body, rows, bcasts, out_rows, out_accs=(), tile=None, plan=None):
    rows = [r if isinstance(r, tuple) else (r, r.shape[1], 0) for r in rows]
    s = rows[0][0].shape[0]
    if tile is None:
        per_row = sum(w * a.dtype.itemsize for a, w, _ in rows) + sum(w * jnp.dtype(dt).itemsize for w, dt in out_rows)
        tile = 1024
        while tile > 16 and (tile * per_row * 2 > ROW_TILE_BUDGET or s % tile):
            tile //= 2
        if s * per_row * 2 <= ROW_TILE_BUDGET:
            tile = s
    assert s % tile == 0
    n_in = len(rows) + len(bcasts)
    n_o = len(out_rows)
    n_out = n_o + len(out_accs)
    p_ops, p_in_specs, p_out_shapes, p_out_specs, p_scratch, _ = _plan_call_parts(plan)
    n_pi, n_po = len(p_ops), len(p_out_shapes)
    n_steps = (s // tile,)

    def kern(*refs):
        plan_ins = refs[n_in:n_in + n_pi]
        outs = refs[n_in + n_pi:n_in + n_pi + n_out]
        plan_outs = refs[n_in + n_pi + n_out:n_in + n_pi + n_out + n_po]
        plan_sems = refs[n_in + n_pi + n_out + n_po:]
        _plan_phase(plan, "start", n_steps, plan_ins, plan_outs, plan_sems)
        _plan_phase(plan, "middle", n_steps, plan_ins, plan_outs, plan_sems)
        res = body(*[r[...].astype(F32) for r in refs[:n_in]])
        for o, v in zip(outs[:n_o], res[:n_o]):
            o[...] = v.astype(o.dtype)
        if out_accs:
            @pl.when(pl.program_id(0) == 0)
            def _():
                for o in outs[n_o:]:
                    o[...] = jnp.zeros_like(o)
            for o, v in zip(outs[n_o:], res[n_o:]):
                o[...] += v
        _plan_phase(plan, "end", n_steps, plan_ins, plan_outs, plan_sems)

    in_specs = [pl.BlockSpec((tile, w), functools.partial(lambda i, cb: (i, cb), cb=cb)) for _, w, cb in rows]
    in_specs += [pl.BlockSpec(b.shape, functools.partial(lambda i, nd: (0,) * nd, nd=b.ndim)) for b in bcasts]
    out_shape = [jax.ShapeDtypeStruct((s, w), dt) for w, dt in out_rows]
    out_shape += [jax.ShapeDtypeStruct((1, w), F32) for w in out_accs]
    out_specs = [pl.BlockSpec((tile, w), lambda i: (i, 0)) for w, _ in out_rows]
    out_specs += [pl.BlockSpec((1, w), lambda i: (0, 0)) for w in out_accs]
    return pl.pallas_call(
        kern, name=name, grid=n_steps, in_specs=in_specs + p_in_specs, out_specs=out_specs + p_out_specs,
        out_shape=out_shape + p_out_shapes, scratch_shapes=p_scratch,
        compiler_params=_params(("arbitrary",) if out_accs or plan is not None else ("parallel",)),
    )(*[a for a, _, _ in rows], *bcasts, *p_ops)


def _tile(n, pref):
    if n <= pref:
        return n
    t = (pref // LANES) * LANES
    while t >= LANES:
        if n % t == 0:
            return t
        t -= LANES
    return n


def _matmul(name, a, b, *, ta=False, tb=False, out_dtype=F32, bias=None, add=None, b_blocks=1, out_blocks=1,
            b_rows=None, out_rows=None, plan=None, tm=1024, tn=1408, tk=2816):
    m, k = (a.shape[1], a.shape[0]) if ta else a.shape
    if b_blocks > 1:
        rows_b, cols_b = b.shape[1], b.shape[2] * b_blocks
    else:
        rows_b, cols_b = b.shape
    if b_rows is not None:
        rows_b = b_rows[1]
    n, kb = (rows_b, cols_b) if tb else (cols_b, rows_b)
    assert k == kb, (name, a.shape, b.shape)
    if not ta and out_dtype == BF16 and k <= 2048:
        tm *= 2
    tm, tn, tk = _tile(m, tm), _tile(n, tn), _tile(k, tk)
    if b_blocks > 1:
        per = cols_b // b_blocks
        if tb:
            tk = _tile(per, tk)
        else:
            tn = _tile(per, tn)
    if out_blocks > 1:
        tn = _tile(n // out_blocks, tn)
    nk = k // tk
    grid = (m // tm, n // tn, nk)

    a_spec = pl.BlockSpec((tk, tm), lambda i, j, kk: (kk, i)) if ta else pl.BlockSpec((tm, tk), lambda i, j, kk: (i, kk))
    if b_blocks > 1:
        if tb:
            nb = per // tk
            b_spec = pl.BlockSpec((1, tn, tk), lambda i, j, kk: (kk // nb, j, kk % nb))
        else:
            nb = per // tn
            b_spec = pl.BlockSpec((1, tk, tn), lambda i, j, kk: (j // nb, kk, j % nb))
    elif b_rows is not None:
        first = b_rows[0]
        if tb:
            b_spec = pl.BlockSpec((pl.Element(tn), pl.Element(tk)),
                                  lambda i, j, kk: (pl.multiple_of(first + j * tn, LANES), kk * tk))
        else:
            b_spec = pl.BlockSpec((pl.Element(tk), pl.Element(tn)),
                                  lambda i, j, kk: (pl.multiple_of(first + kk * tk, LANES), j * tn))
    else:
        b_spec = pl.BlockSpec((tn, tk), lambda i, j, kk: (j, kk)) if tb else pl.BlockSpec((tk, tn), lambda i, j, kk: (kk, j))
    in_specs = [a_spec, b_spec]
    operands = [a, b]
    if bias is not None:
        in_specs.append(pl.BlockSpec((1, tn), lambda i, j, kk: (0, j)))
        operands.append(bias)
    if add is not None:
        in_specs.append(pl.BlockSpec((tm, tn), lambda i, j, kk: (i, j)))
        operands.append(add)
    aliases = {}
    if out_blocks > 1:
        nbo = (n // out_blocks) // tn
        out_spec = pl.BlockSpec((1, tm, tn), lambda i, j, kk: (j // nbo, i, j % nbo))
        out_shape = jax.ShapeDtypeStruct((out_blocks, m, n // out_blocks), out_dtype)
    elif out_rows is not None:
        first_out, total, previous = out_rows
        out_spec = pl.BlockSpec((pl.Element(tm), pl.Element(tn)),
                                lambda i, j, kk: (pl.multiple_of(first_out + i * tm, LANES), j * tn))
        out_shape = jax.ShapeDtypeStruct((total, n), out_dtype)
        if previous is not None:
            aliases = {len(operands): 0}
            in_specs.append(pl.BlockSpec(memory_space=pl.ANY))
            operands.append(previous)
    else:
        out_spec = pl.BlockSpec((tm, tn), lambda i, j, kk: (i, j))
        out_shape = jax.ShapeDtypeStruct((m, n), out_dtype)
    dims = (((0 if ta else 1,), (1 if tb else 0,)), ((), ()))
    has_bias, has_add, has_previous = bias is not None, add is not None, bool(aliases)
    p_ops, p_in_specs, p_out_shapes, p_out_specs, p_scratch, _ = _plan_call_parts(plan)
    n_pi, n_po = len(p_ops), len(p_out_shapes)

    def kern(*refs):
        a_ref, b_ref = refs[0], refs[1]
        pos = 2
        bias_ref = add_ref = None
        if has_bias:
            bias_ref = refs[pos]
            pos += 1
        if has_add:
            add_ref = refs[pos]
            pos += 1
        if has_previous:
            pos += 1
        plan_ins = refs[pos:pos + n_pi]
        o_ref = refs[pos + n_pi]
        plan_outs = refs[pos + n_pi + 1:pos + n_pi + 1 + n_po]
        pos += n_pi + 1 + n_po
        acc_ref = refs[pos] if nk > 1 else None
        plan_sems = refs[pos + (1 if nk > 1 else 0):]
        _plan_phase(plan, "start", grid, plan_ins, plan_outs, plan_sems)
        av = a_ref[...].astype(BF16)
        bv = (b_ref[0] if b_blocks > 1 else b_ref[...]).astype(BF16)
        p = lax.dot_general(av, bv, dims, preferred_element_type=F32)

        def finish(acc):
            if has_bias:
                acc = acc + bias_ref[...]
            if has_add:
                acc = acc + add_ref[...]
            if out_blocks > 1:
                o_ref[0] = acc.astype(o_ref.dtype)
            else:
                o_ref[...] = acc.astype(o_ref.dtype)

        if nk == 1:
            finish(p)
        else:
            kk = pl.program_id(2)

            @pl.when(kk == 0)
            def _():
                acc_ref[...] = p

            @pl.when(kk > 0)
            def _():
                acc_ref[...] += p

            @pl.when(kk == nk - 1)
            def _():
                finish(acc_ref[...])
        _plan_phase(plan, "end", grid, plan_ins, plan_outs, plan_sems)

    res = pl.pallas_call(
        kern, name=name, grid=grid, in_specs=in_specs + p_in_specs, out_specs=[out_spec] + p_out_specs,
        out_shape=[out_shape] + p_out_shapes, input_output_aliases=aliases,
        scratch_shapes=([pltpu.VMEM((tm, tn), F32)] if nk > 1 else []) + p_scratch,
        compiler_params=_params(("parallel", "parallel", "arbitrary") if plan is None else ("arbitrary",) * 3),
    )(*operands, *p_ops)
    return res[0] if plan is None else (res[0], list(res[1:]))


CONV_HALO = 16
CONV_ROWS = 256


def _taps(win, shifts, rows):
    n = win.shape[0]
    for j, s in enumerate(shifts):
        yield j, (pltpu.roll(win, (n - s) % n, axis=0) if s % n else win)[:rows]


def _dwconv_fwd(name, x, w, b, *, silu, out_dtype=F32, plan=None):
    s, c = x.shape
    k = w.shape[0]
    pad = (k - 1) // 2
    assert pad <= CONV_HALO and c % LANES == 0
    t = min(CONV_ROWS, s)
    n_chunks = s // t
    fwd_shifts = [CONV_HALO - pad + j for j in range(k)]
    p_ops, p_in_specs, p_out_shapes, p_out_specs, p_scratch, _ = _plan_call_parts(plan)
    n_pi, n_po = len(p_ops), len(p_out_shapes)
    n_steps = (c // LANES,)

    def kern(*refs):
        x_ref, w_ref, b_ref = refs[:3]
        plan_ins = refs[3:3 + n_pi]
        o_ref = refs[3 + n_pi]
        plan_outs = refs[4 + n_pi:4 + n_pi + n_po]
        xp_ref = refs[4 + n_pi + n_po]
        plan_sems = refs[5 + n_pi + n_po:]
        _plan_phase(plan, "start", n_steps, plan_ins, plan_outs, plan_sems)
        _plan_phase(plan, "middle", n_steps, plan_ins, plan_outs, plan_sems)
        zeros = jnp.zeros((CONV_HALO, LANES), F32)
        xp_ref[0:CONV_HALO, :] = zeros
        xp_ref[CONV_HALO + s:CONV_HALO + s + CONV_HALO, :] = zeros
        xp_ref[CONV_HALO:CONV_HALO + s, :] = x_ref[...].astype(F32)
        bv = b_ref[...]

        def chunk(i, carry):
            base = pl.multiple_of(i * t, 16)
            win = xp_ref[pl.ds(base, t + 2 * CONV_HALO), :]
            acc = jnp.zeros((t, LANES), F32)
            for j, xs in _taps(win, fwd_shifts, t):
                acc = acc + xs * w_ref[pl.ds(j, 1), :]
            acc = acc + bv
            o_ref[pl.ds(base, t), :] = (_silu(acc) if silu else acc).astype(o_ref.dtype)
            return carry

        lax.fori_loop(0, n_chunks, chunk, 0)
        _plan_phase(plan, "end", n_steps, plan_ins, plan_outs, plan_sems)

    res = pl.pallas_call(
        kern, name=name, grid=n_steps,
        in_specs=[pl.BlockSpec((s, LANES), lambda i: (0, i)), pl.BlockSpec((k, LANES), lambda i: (0, i)),
                  pl.BlockSpec((1, LANES), lambda i: (0, i))] + p_in_specs,
        out_specs=[pl.BlockSpec((s, LANES), lambda i: (0, i))] + p_out_specs,
        out_shape=[jax.ShapeDtypeStruct((s, c), out_dtype)] + p_out_shapes,
        scratch_shapes=[pltpu.VMEM((s + 2 * CONV_HALO, LANES), F32)] + p_scratch,
        compiler_params=_params(("parallel",) if plan is None else ("arbitrary",)),
    )(x, w, b, *p_ops)
    return res[0] if plan is None else (res[0], list(res[1:]))


def _dwconv_bwd(name, x, w, b, dout, *, silu, dx_dtype=F32):
    s, c = x.shape
    k = w.shape[0]
    pad = (k - 1) // 2
    t = min(CONV_ROWS, s)
    n_chunks = s // t
    fwd_shifts = [CONV_HALO - pad + j for j in range(k)]
    bwd_shifts = [CONV_HALO + pad - j for j in range(k)]

    def kern(x_ref, w_ref, b_ref, do_ref, dx_ref, dw_ref, db_ref, xp_ref, dp_ref):
        zeros = jnp.zeros((CONV_HALO, LANES), F32)
        for ref in (xp_ref, dp_ref):
            ref[0:CONV_HALO, :] = zeros
            ref[CONV_HALO + s:CONV_HALO + s + CONV_HALO, :] = zeros
        xp_ref[CONV_HALO:CONV_HALO + s, :] = x_ref[...].astype(F32)
        bv = b_ref[...]
        dw_ref[...] = jnp.zeros_like(dw_ref)

        def pre_chunk(i, dbias):
            base = pl.multiple_of(i * t, 16)
            win = xp_ref[pl.ds(base, t + 2 * CONV_HALO), :]
            dpre = do_ref[pl.ds(base, t), :].astype(F32)
            if silu:
                acc = jnp.zeros((t, LANES), F32)
                for j, xs in _taps(win, fwd_shifts, t):
                    acc = acc + xs * w_ref[pl.ds(j, 1), :]
                dpre = dpre * _dsilu(acc + bv)
            dp_ref[pl.ds(base + CONV_HALO, t), :] = dpre
            for j, xs in _taps(win, fwd_shifts, t):
                dw_ref[pl.ds(j, 1), :] += _sum0(dpre * xs)
            return dbias + _sum0(dpre)

        db_ref[...] = lax.fori_loop(0, n_chunks, pre_chunk, jnp.zeros((1, LANES), F32))

        def dx_chunk(i, carry):
            base = pl.multiple_of(i * t, 16)
            win = dp_ref[pl.ds(base, t + 2 * CONV_HALO), :]
            acc = jnp.zeros((t, LANES), F32)
            for j, dps in _taps(win, bwd_shifts, t):
                acc = acc + dps * w_ref[pl.ds(j, 1), :]
            dx_ref[pl.ds(base, t), :] = acc.astype(dx_ref.dtype)
            return carry

        lax.fori_loop(0, n_chunks, dx_chunk, 0)

    col = lambda rows: pl.BlockSpec((rows, LANES), lambda i: (0, i))
    return pl.pallas_call(
        kern, name=name, grid=(c // LANES,),
        in_specs=[col(s), col(k), col(1), col(s)],
        out_specs=[col(s), col(k), col(1)],
        out_shape=[jax.ShapeDtypeStruct((s, c), dx_dtype), jax.ShapeDtypeStruct((k, c), F32),
                   jax.ShapeDtypeStruct((1, c), F32)],
        scratch_shapes=[pltpu.VMEM((s + 2 * CONV_HALO, LANES), F32), pltpu.VMEM((s + 2 * CONV_HALO, LANES), F32)],
        compiler_params=_params(("parallel",)),
    )(x, w, b, dout)


_NT =(((1,), (1,)), ((), ()))
_TN = (((0,), (0,)), ((), ()))


def _dot(a, b, dims=None):
    if dims is None:
        return jnp.dot(a, b, preferred_element_type=F32)
    return lax.dot_general(a, b, dims, preferred_element_type=F32)


HEAD_QUANTITIES = 4
GROUPS_PER_STEP = 8


def _scan_tables(n_heads):
    j_heads = n_heads // N_GROUPS
    used = 3 * HEAD_QUANTITIES * j_heads
    assert used <= LANES and j_heads % 2 == 0 and N_GROUPS % GROUPS_PER_STEP == 0
    gw = j_heads * HEAD_DIM
    r = jnp.arange(LANES)[:, None]

    def expand(quantity, width):
        head_of_lane = jnp.arange(j_heads * width)[None] // width
        return ((r // (3 * j_heads) == quantity) & (r % j_heads == head_of_lane) & (r < used)).astype(BF16)

    sel_cols = (jnp.arange(gw)[:, None] // HEAD_DIM == jnp.arange(LANES)[None]).astype(BF16)
    h2 = 2 * n_heads
    rows = jnp.arange(3 * HEAD_QUANTITIES * h2)
    head = rows % n_heads
    col = ((rows % h2) // n_heads * N_GROUPS + head // j_heads) * LANES + (rows // h2) * j_heads + head % j_heads
    route = (col[:, None] == jnp.arange(2 * N_GROUPS * LANES)[None]).astype(BF16)
    return {"ex_dt": expand(0, HEAD_DIM), "ex_gam": expand(1, CHUNK), "ex_din": expand(2, HEAD_DIM),
            "ex_dst": expand(3, HEAD_DIM), "sel_cols": sel_cols, "route": route}


def _chunk_decay(etot, n_heads, direction):
    j_heads = n_heads // N_GROUPS
    ed = etot[:, :, direction * n_heads:(direction + 1) * n_heads]
    per_group = jnp.pad(ed.reshape(ed.shape[0], 8, N_GROUPS, j_heads), ((0, 0), (0, 0), (0, 0), (0, LANES - j_heads)))
    return jnp.repeat(ed, HEAD_DIM, axis=2), per_group.reshape(ed.shape[0], 8, N_GROUPS * LANES)


def _scan_specs(reverse_order, direction, nc, j_heads, d_ssm):
    gps = GROUPS_PER_STEP
    gw = j_heads * HEAD_DIM
    b_off = d_ssm // (gps * D_STATE)
    c_off = b_off + N_GROUPS // gps
    d_off = direction * (N_GROUPS // gps)
    zz = (lambda z: nc - 1 - z) if reverse_order else (lambda z: z)
    const = lambda shape: pl.BlockSpec(shape, lambda g, z: (0,) * len(shape))
    return {
        "xs": pl.BlockSpec((CHUNK, gps * gw), lambda g, z: (zz(z), g)),
        "b": pl.BlockSpec((CHUNK, gps * D_STATE), lambda g, z: (zz(z), b_off + g)),
        "c": pl.BlockSpec((CHUNK, gps * D_STATE), lambda g, z: (zz(z), c_off + g)),
        "q": pl.BlockSpec((CHUNK, gps * LANES), lambda g, z: (zz(z), d_off + g)),
        "gam_t": pl.BlockSpec((gps * j_heads, CHUNK), lambda g, z: (d_off + g, zz(z))),
        "etot_x": pl.BlockSpec((1, 8, gps * gw), lambda g, z: (zz(z), 0, g)),
        "etot_g": pl.BlockSpec((1, 8, gps * LANES), lambda g, z: (zz(z), 0, g)),
        "state": pl.BlockSpec((gps, 1, D_STATE, gw), lambda g, z: (g, zz(z), 0, 0)),
        "grp": pl.BlockSpec((CHUNK, gps * D_STATE), lambda g, z: (zz(z), g)),
        "ex": const((LANES, gw)), "ex_gam": const((LANES, j_heads * CHUNK)), "sel": const((gw, LANES)),
    }


def _scan_masks(reverse):
    li = lax.broadcasted_iota(jnp.int32, (CHUNK, CHUNK), 0)
    si = lax.broadcasted_iota(jnp.int32, (CHUNK, CHUNK), 1)
    mask = (li <= si) if reverse else (li >= si)
    mask_t = (si <= li) if reverse else (si >= li)
    return li, si, mask, mask_t, si < HEAD_DIM


def _plan_phase(plan, phase, n_steps, ins, outs, sems):
    if plan is None or phase not in plan.phases:
        return
    ids = [pl.program_id(i) for i in range(len(n_steps))]
    first = {"start": 0, "middle": n_steps[0] // 2, "end": n_steps[0] - 1}[phase]
    when = ids[0] == first
    for i, n in zip(ids[1:], n_steps[1:]):
        when = when & (i == (n - 1 if phase == "end" else 0))

    @pl.when(when)
    def _():
        plan.phases[phase](ins, outs, *sems)


def _plan_call_parts(plan):
    if plan is None:
        return [], [], [], [], [], ("parallel", "arbitrary")
    n_in, n_out = len(plan.operands), len(plan.out_shapes)
    return plan.operands, [ANY] * n_in, plan.out_shapes, [ANY] * n_out, plan.sems(), ("arbitrary", "arbitrary")


def _scan_fwd(name, xbc_c, q_all, gam_t, etot_x, tb, *, direction, d_ssm, plan=None):
    s = xbc_c.shape[0]
    nc = s // CHUNK
    j_heads = tb["ex_dt"].shape[1] // HEAD_DIM
    gw = j_heads * HEAD_DIM
    gps = GROUPS_PER_STEP
    reverse = direction == 1
    sp = _scan_specs(reverse, direction, nc, j_heads, d_ssm)
    p_ops, p_in_specs, p_out_shapes, p_out_specs, p_scratch, semantics = _plan_call_parts(plan)
    n_pi, n_po = len(p_ops), len(p_out_shapes)
    n_steps = (N_GROUPS // gps, nc)

    def kern(*refs):
        xs_ref, b_ref, c_ref, q_ref, gamt_ref, etx_ref, exdt_ref, exgam_ref, exdin_ref, exdst_ref = refs[:10]
        plan_ins = refs[10:10 + n_pi]
        y_ref, hs_ref = refs[10 + n_pi:12 + n_pi]
        plan_outs = refs[12 + n_pi:12 + n_pi + n_po]
        h_ref = refs[12 + n_pi + n_po]
        plan_sems = refs[13 + n_pi + n_po:]
        _plan_phase(plan, "start", n_steps, plan_ins, plan_outs, plan_sems)
        _plan_phase(plan, "middle", n_steps, plan_ins, plan_outs, plan_sems)

        @pl.when(pl.program_id(1) == 0)
        def _():
            h_ref[...] = jnp.zeros_like(h_ref)

        _, _, mask, _, lo = _scan_masks(reverse)
        for gi in range(gps):
            bb = b_ref[:, gi * D_STATE:(gi + 1) * D_STATE].astype(BF16)
            cb = c_ref[:, gi * D_STATE:(gi + 1) * D_STATE].astype(BF16)
            cbt = _dot(cb, bb, _NT)
            q = q_ref[:, gi * LANES:(gi + 1) * LANES]
            dtx, dinx, dstx = _dot(q, exdt_ref[...]), _dot(q, exdin_ref[...]), _dot(q, exdst_ref[...])
            gcol = _dot(q, exgam_ref[...])
            xdt = xs_ref[:, gi * gw:(gi + 1) * gw].astype(F32) * dtx
            ht = h_ref[gi]
            y_off = _dot(cb, ht.astype(BF16)) * dinx
            hs_ref[gi, 0] = ht
            for p in range(j_heads // 2):
                lanes = slice(p * CHUNK, (p + 1) * CHUNK)
                x2 = xdt[:, lanes]
                acc = y_off[:, lanes]
                for idx, j in enumerate((2 * p, 2 * p + 1)):
                    g_row = gamt_ref[pl.ds(gi * j_heads + j, 1), :]
                    decay = jnp.exp(jnp.where(mask, gcol[:, j * CHUNK:(j + 1) * CHUNK] - g_row, NEG))
                    x_head = jnp.where(lo if idx == 0 else jnp.logical_not(lo), x2, 0.0).astype(BF16)
                    acc = acc + _dot((cbt * decay).astype(BF16), x_head)
                y_ref[:, gi * gw + p * CHUNK:gi * gw + (p + 1) * CHUNK] = acc.astype(y_ref.dtype)
            h_ref[gi] = ht * etx_ref[0, 0:1, gi * gw:(gi + 1) * gw] + _dot(bb, (xdt * dstx).astype(BF16), _TN)
        _plan_phase(plan, "end", n_steps, plan_ins, plan_outs, plan_sems)

    res = pl.pallas_call(
        kern, name=name, grid=n_steps,
        in_specs=[sp["xs"], sp["b"], sp["c"], sp["q"], sp["gam_t"], sp["etot_x"], sp["ex"], sp["ex_gam"], sp["ex"],
                  sp["ex"]] + p_in_specs,
        out_specs=[sp["xs"], sp["state"]] + p_out_specs,
        out_shape=[jax.ShapeDtypeStruct((s, d_ssm), BF16),
                   jax.ShapeDtypeStruct((N_GROUPS, nc, D_STATE, gw), F32)] + p_out_shapes,
        scratch_shapes=[pltpu.VMEM((gps, D_STATE, gw), F32)] + p_scratch,
        compiler_params=_params(semantics),
    )(xbc_c, xbc_c, xbc_c, q_all, gam_t, etot_x, tb["ex_dt"], tb["ex_gam"], tb["ex_din"], tb["ex_dst"], *p_ops)
    return res[0], res[1], list(res[2:])


def _scan_bwd(name, xbc_c, dy, hs, q_all, gam_t, etot_x, etot_g, tb, *, direction, d_ssm, plan=None):
    s = xbc_c.shape[0]
    nc = s // CHUNK
    j_heads = tb["ex_dt"].shape[1] // HEAD_DIM
    gw = j_heads * HEAD_DIM
    gps = GROUPS_PER_STEP
    reverse = direction == 1
    sp = _scan_specs(not reverse, direction, nc, j_heads, d_ssm)
    hp = lax.Precision.HIGHEST
    p_ops, p_in_specs, p_out_shapes, p_out_specs, p_scratch, semantics = _plan_call_parts(plan)
    n_pi, n_po = len(p_ops), len(p_out_shapes)
    n_steps = (N_GROUPS // gps, nc)

    def kern(*refs):
        (xs_ref, b_ref, c_ref, dy_ref, hs_ref, q_ref, gamt_ref, etg_ref, etx_ref, exdt_ref, exgam_ref, exdin_ref,
         exdst_ref, sel_ref) = refs[:14]
        plan_ins = refs[14:14 + n_pi]
        dxs_ref, db_ref, dc_ref, ddt_ref, da_ref = refs[14 + n_pi:19 + n_pi]
        plan_outs = refs[19 + n_pi:19 + n_pi + n_po]
        dh_ref, tmp_ref = refs[19 + n_pi + n_po:21 + n_pi + n_po]
        plan_sems = refs[21 + n_pi + n_po:]
        _plan_phase(plan, "start", n_steps, plan_ins, plan_outs, plan_sems)
        _plan_phase(plan, "middle", n_steps, plan_ins, plan_outs, plan_sems)

        @pl.when(pl.program_id(1) == 0)
        def _():
            dh_ref[...] = jnp.zeros_like(dh_ref)

        li, si, mask, mask_t, lo = _scan_masks(reverse)
        sel = sel_ref[...]
        incl = ((si <= li) if reverse else (si >= li)).astype(F32)
        excl = ((si > li) if reverse else (si < li)).astype(F32)
        for gi in range(gps):
            grp_lanes = slice(gi * D_STATE, (gi + 1) * D_STATE)
            bb = b_ref[:, grp_lanes].astype(BF16)
            cb = c_ref[:, grp_lanes].astype(BF16)
            cbt = _dot(cb, bb, _NT)
            cbt_t = _dot(bb, cb, _NT)
            q = q_ref[:, gi * LANES:(gi + 1) * LANES]
            dtx, dinx, dstx = _dot(q, exdt_ref[...]), _dot(q, exdin_ref[...]), _dot(q, exdst_ref[...])
            gcol = _dot(q, exgam_ref[...])
            x_all = xs_ref[:, gi * gw:(gi + 1) * gw].astype(F32)
            dy_all = dy_ref[:, gi * gw:(gi + 1) * gw].astype(F32)
            xdt = x_all * dtx
            xb = xdt.astype(BF16)
            ht = hs_ref[gi, 0]
            hb = ht.astype(BF16)
            dht = dh_ref[gi]
            dhb = dht.astype(BF16)
            y_off = _dot(cb, hb) * dinx
            dx_off = _dot(bb, dhb) * dstx
            dyd = (dy_all * dinx).astype(BF16)
            xd = (xdt * dstx).astype(BF16)
            dc_acc = _dot(dyd, hb, _NT)
            db_acc = _dot(xd, dhb, _NT)
            dh_ref[gi] = dht * etx_ref[0, 0:1, gi * gw:(gi + 1) * gw] + _dot(cb, dyd, _TN)
            q_cols = _dot((dy_all * y_off).astype(BF16), sel)
            c_cols = _dot((xdt * dx_off).astype(BF16), sel)
            through = _sum0(_dot((dht * ht).astype(BF16), sel))
            dcbt = jnp.zeros((CHUNK, CHUNK), F32)
            for p in range(j_heads // 2):
                lanes = slice(p * CHUNK, (p + 1) * CHUNK)
                out_lanes = slice(gi * gw + p * CHUNK, gi * gw + (p + 1) * CHUNK)
                x2b = xb[:, lanes]
                dy2 = dy_all[:, lanes]
                acc = dx_off[:, lanes]
                for idx, j in enumerate((2 * p, 2 * p + 1)):
                    gc = gcol[:, j * CHUNK:(j + 1) * CHUNK]
                    gr = gamt_ref[pl.ds(gi * j_heads + j, 1), :]
                    decay = jnp.exp(jnp.where(mask, gc - gr, NEG))
                    decay_t = jnp.exp(jnp.where(mask_t, gr - gc, NEG))
                    dy_head = jnp.where(lo if idx == 0 else jnp.logical_not(lo), dy2, 0.0).astype(BF16)
                    acc = acc + _dot((cbt_t * decay_t).astype(BF16), dy_head)
                    dm = decay * _dot(dy_head, x2b, _NT)
                    dcbt = dcbt + dm
                    e = (cbt * dm).astype(BF16)
                    in_lane_j = si == j
                    q_cols = (q_cols + jnp.where(in_lane_j, jnp.sum(e.astype(F32), axis=1, keepdims=True), 0.0)
                              - _dot(e, jnp.where(in_lane_j, 1.0, 0.0).astype(BF16), _TN))
                dxs_ref[:, out_lanes] = (acc * dtx[:, lanes]).astype(dxs_ref.dtype)
                tmp_ref[:, lanes] = acc * x_all[:, lanes]
            dcb = dcbt.astype(BF16)
            dc_ref[:, grp_lanes] = (dc_acc + _dot(dcb, bb)).astype(dc_ref.dtype)
            db_ref[:, grp_lanes] = (db_acc + _dot(dcb, cb, _TN)).astype(db_ref.dtype)
            ddt_ref[:, gi * LANES:(gi + 1) * LANES] = _dot(tmp_ref[...].astype(BF16), sel)
            da_ref[:, gi * LANES:(gi + 1) * LANES] = (
                jnp.dot(incl, q_cols, preferred_element_type=F32, precision=hp)
                + jnp.dot(excl, c_cols, preferred_element_type=F32, precision=hp)
                + through * etg_ref[0, 0:1, gi * LANES:(gi + 1) * LANES])
        _plan_phase(plan, "end", n_steps, plan_ins, plan_outs, plan_sems)

    gn = N_GROUPS * D_STATE
    res = pl.pallas_call(
        kern, name=name, grid=n_steps,
        in_specs=[sp["xs"], sp["b"], sp["c"], sp["xs"], sp["state"], sp["q"], sp["gam_t"], sp["etot_g"], sp["etot_x"],
                  sp["ex"], sp["ex_gam"], sp["ex"], sp["ex"], sp["sel"]] + p_in_specs,
        out_specs=[sp["xs"], sp["grp"], sp["grp"], sp["grp"], sp["grp"]] + p_out_specs,
        out_shape=[jax.ShapeDtypeStruct((s, d_ssm), BF16), jax.ShapeDtypeStruct((s, gn), BF16),
                   jax.ShapeDtypeStruct((s, gn), BF16), jax.ShapeDtypeStruct((s, gn), F32),
                   jax.ShapeDtypeStruct((s, gn), F32)] + p_out_shapes,
        scratch_shapes=[pltpu.VMEM((gps, D_STATE, gw), F32), pltpu.VMEM((CHUNK, gw), F32)] + p_scratch,
        compiler_params=_params(semantics),
    )(xbc_c, xbc_c, xbc_c, dy, hs, q_all, gam_t, etot_g, etot_x, tb["ex_dt"], tb["ex_gam"], tb["ex_din"], tb["ex_dst"],
      tb["sel_cols"], *p_ops)
    return res[:5], list(res[5:])


def _dt_prepare(dt_raw, dt_bias, a_neg, route):
    s, h2 = dt_raw.shape
    n_heads = h2 // 2
    qw = route.shape[1]

    def kern(raw_ref, bias_ref, a_ref, route_ref, dt_ref, q_ref, gamt_ref, etot_ref):
        dt = _softplus(raw_ref[...] + bias_ref[...])
        a = dt * a_ref[...]
        li = lax.broadcasted_iota(jnp.int32, (CHUNK, CHUNK), 0)
        si = lax.broadcasted_iota(jnp.int32, (CHUNK, CHUNK), 1)
        tri = (li >= si).astype(F32)
        cs = jnp.dot(tri, a, preferred_element_type=F32, precision=lax.Precision.HIGHEST)
        tot = _sum0(a)
        fwd = lax.broadcasted_iota(jnp.int32, (CHUNK, h2), 1) < n_heads
        gam = jnp.where(fwd, cs, a - cs)
        din = jnp.where(fwd, jnp.exp(cs), jnp.exp(tot + gam))
        dst = jnp.where(fwd, jnp.exp(tot - cs), jnp.exp(cs - a))
        pieces = []
        for v in (dt, gam, din, dst):
            hi = v.astype(BF16)
            rest = v - hi.astype(F32)
            mid = rest.astype(BF16)
            pieces += [hi, mid, (rest - mid.astype(F32)).astype(BF16)]
        q_ref[...] = _dot(jnp.concatenate(pieces, axis=1), route_ref[...]).astype(BF16)
        dt_ref[...] = dt
        gamt_ref[...] = gam.T
        etot_ref[...] = jnp.broadcast_to(jnp.exp(tot), (8, h2))

    nc = s // CHUNK
    rows = lambda w: pl.BlockSpec((CHUNK, w), lambda i: (i, 0))
    whole = lambda a: pl.BlockSpec(a.shape, lambda i: (0, 0))
    return pl.pallas_call(
        kern, name="dt_prepare", grid=(nc,),
        in_specs=[rows(h2), whole(dt_bias), whole(a_neg), whole(route)],
        out_specs=[rows(h2), rows(qw), pl.BlockSpec((h2, CHUNK), lambda i: (0, i)), pl.BlockSpec((8, h2), lambda i: (i, 0))],
        out_shape=[jax.ShapeDtypeStruct((s, h2), F32), jax.ShapeDtypeStruct((s, qw), BF16),
                   jax.ShapeDtypeStruct((h2, s), F32), jax.ShapeDtypeStruct((nc * 8, h2), F32)],
        compiler_params=_params(("parallel",)),
    )(dt_raw, dt_bias, a_neg, route)


def _dt_backward(da_dirs, ddt_dirs, dt, dt_raw, dt_bias, a_neg):
    h2 = dt.shape[1]
    n_heads = h2 // 2
    j_heads = n_heads // N_GROUPS
    lane = jnp.arange(N_GROUPS * LANES)[:, None]
    head = (lane // LANES) * j_heads + lane % LANES
    pick = [((lane % LANES < j_heads) & (head + direction * n_heads == jnp.arange(h2)[None])).astype(BF16)
            for direction in (0, 1)]

    def compact(wide_f, wide_r, pick_f, pick_r):
        total = 0.0
        for wide, sel in ((wide_f, pick_f), (wide_r, pick_r)):
            hi = wide.astype(BF16)
            total = total + _dot(hi, sel.astype(BF16)) + _dot((wide - hi.astype(F32)).astype(BF16), sel.astype(BF16))
        return total

    def body(da_f, da_r, ddt_f, ddt_r, dtv, raw, pick_f, pick_r, bias, a_head):
        dav = compact(da_f, da_r, pick_f, pick_r)
        ddtv = compact(ddt_f, ddt_r, pick_f, pick_r)
        draw = (ddtv + dav * a_head) * _sigmoid(raw + bias)
        return draw, _sum0(draw), _sum0(dav * dtv) * a_head
    return _rowwise("dt_backward", body, [*da_dirs, *ddt_dirs, dt, dt_raw], [*pick, dt_bias, a_neg], [(h2, BF16)], [h2, h2])


def _rms(x):
    r = lax.rsqrt(_mean1(x * x) + EPS)
    return x * r, r


def _rms_bwd(dy, y, r):
    return r * (dy - y * _mean1(dy * y))


def _norm_mod_fwd(name, x, g, sc, sh):
    def body(xv, gv, scv, shv):
        y, _ = _rms(xv)
        return ((y * gv) * (1.0 + scv) + shv,)
    return _rowwise(name, body, [x], [g, sc, sh], [(x.shape[1], BF16)])[0]


def _norm_mod_bwd(name, x, dh, dpass, g, sc, plan=None):
    d = x.shape[1]

    def body(xv, dhv, dpv, gv, scv):
        y, r = _rms(xv)
        dn = dhv * (1.0 + scv)
        dx = _rms_bwd(dn * gv, y, r) + dpv
        return dx, _sum0(dn * y), _sum0(dhv * (y * gv)), _sum0(dhv)
    return _rowwise(name, body, [x, dh, dpass], [g, sc], [(d, F32)], [d, d, d], plan=plan)


def _gated_residual_fwd(name, x, m, gate, gp):
    def body(xv, mv, gatev, gpv):
        y, _ = _rms(mv)
        return (xv + gatev * (y * gpv),)
    return _rowwise(name, body, [x, m], [gate, gp], [(x.shape[1], F32)])[0]


def _gated_residual_bwd(name, m, dx1, gate, gp):
    d = m.shape[1]

    def body(mv, dv, gatev, gpv):
        y, r = _rms(mv)
        dn = dv * gatev
        return _rms_bwd(dn * gpv, y, r), _sum0(dv * (y * gpv)), _sum0(dn * y)
    return _rowwise(name, body, [m, dx1], [gate, gp], [(d, BF16)], [d, d])


def _final_residual_loss(x1, f, tgt, gate, gp):
    d = x1.shape[1]

    def body(xv, fv, tv, gatev, gpv):
        y, r = _rms(fv)
        n = y * gpv
        err = xv + gatev * n - tv
        dx2 = err * (1.0 / d)
        dn = dx2 * gatev
        sq = jnp.sum(_sum0(err * err), axis=1, keepdims=True)
        return dx2, _rms_bwd(dn * gpv, y, r), jnp.broadcast_to(sq, (1, LANES)), _sum0(dx2 * n), _sum0(dn * y)
    return _rowwise("final_residual_loss", body, [x1, f, tgt], [gate, gp], [(d, F32), (d, BF16)], [LANES, d, d])


def _swiglu_fwd(gu):
    f = gu.shape[1] // 2

    def body(v):
        return (_silu(v[:, :f]) * v[:, f:],)
    return _rowwise("swiglu_fwd", body, [gu], [], [(f, BF16)])[0]


def _swiglu_bwd(gu, dact):
    f = gu.shape[1] // 2

    def body(v, dv):
        gt, up = v[:, :f], v[:, f:]
        return (jnp.concatenate([dv * up * _dsilu(gt), dv * _silu(gt)], axis=1),)
    return _rowwise("swiglu_bwd", body, [gu, dact], [], [(2 * f, BF16)])[0]


def _glu_fwd(glu_in, b_glu):
    c = glu_in.shape[1] // 2

    def body(v, bv):
        t = v + bv
        return (t[:, :c] * _sigmoid(t[:, c:]),)
    return _rowwise("glu_fwd", body, [glu_in], [b_glu], [(c, F32)])[0]


def _glu_bwd(glu_in, b_glu, du0):
    c = glu_in.shape[1] // 2

    def body(v, dv, bv):
        t = v + bv
        a, s = t[:, :c], _sigmoid(t[:, c:])
        dg = jnp.concatenate([dv * s, dv * a * s * (1.0 - s)], axis=1)
        return dg, _sum0(dg)
    return _rowwise("glu_bwd", body, [glu_in, du0], [b_glu], [(2 * c, BF16)], [2 * c])


def _ln_parts(u1):
    xc = u1 - _mean1(u1)
    r = lax.rsqrt(_mean1(xc * xc) + EPS)
    return xc * r, r


def _ln_silu_fwd(u1, ln_g, ln_b):
    def body(v, gv, bv):
        yh, _ = _ln_parts(v)
        return (_silu(yh * gv + bv),)
    return _rowwise("ln_silu_fwd", body, [u1], [ln_g, ln_b], [(u1.shape[1], BF16)])[0]


def _ln_silu_bwd(u1, du, ln_g, ln_b):
    d = u1.shape[1]

    def body(v, dv, gv, bv):
        yh, r = _ln_parts(v)
        dl = dv * _dsilu(yh * gv + bv)
        dyh = dl * gv
        du1 = r * (dyh - _mean1(dyh) - yh * _mean1(dyh * yh))
        return du1, _sum0(dl * yh), _sum0(dl)
    return _rowwise("ln_silu_bwd", body, [u1, du], [ln_g, ln_b], [(d, F32)], [d, d])


def _gate_merge_fwd(y_a, y_b, gl, b_gate):
    d = y_a.shape[1]

    def body(ya, yb, glv, bv):
        s = _sigmoid(glv + bv)
        return (s[:, :d] * ya + s[:, d:] * yb,)
    return _rowwise("gate_merge_fwd", body, [y_a, y_b, gl], [b_gate], [(d, BF16)])[0]


def _gate_merge_bwd(dmixin, y_a, y_b, gl, b_gate):
    d = y_a.shape[1]

    def body(dv, ya, yb, glv, bv):
        s = _sigmoid(glv + bv)
        sa, sb = s[:, :d], s[:, d:]
        dya, dyb = dv * sa, dv * sb
        dgl = jnp.concatenate([dv * ya * sa * (1.0 - sa), dv * yb * sb * (1.0 - sb)], axis=1)
        return dya, dyb, dgl, _sum0(dgl), _sum0(dyb)
    return _rowwise("gate_merge_bwd", body, [dmixin, y_a, y_b, gl], [b_gate],
                    [(d, BF16), (d, BF16), (2 * d, BF16)], [2 * d, d])


def _group_slices(d_ssm):
    gw = d_ssm // N_GROUPS
    return [slice(g * gw, (g + 1) * gw) for g in range(N_GROUPS)]


def _gated_norm_fwd(y_f, y_b, xbc_c, z, d_skip_x, g_ssm):
    d_ssm = y_f.shape[1]

    def body(yf, yb, xs, zv, dsk, gv):
        y = yf + yb + dsk * xs
        v = y * _silu(zv)
        outs = []
        for sl in _group_slices(d_ssm):
            w, _ = _rms(v[:, sl])
            outs.append(w)
        return y, jnp.concatenate(outs, axis=1) * gv
    return _rowwise("gated_norm_fwd", body, [y_f, y_b, (xbc_c, d_ssm, 0), z], [d_skip_x, g_ssm],
                    [(d_ssm, BF16), (d_ssm, BF16)])


def _gated_norm_bwd(y, z, dyn, xbc_c, d_skip_x, g_ssm, plan=None):
    d_ssm = y.shape[1]

    def body(yv, zv, dv, xs, dsk, gv):
        sz = _silu(zv)
        v = yv * sz
        dw = dv * gv
        dvs, ws = [], []
        for sl in _group_slices(d_ssm):
            w, r = _rms(v[:, sl])
            ws.append(w)
            dvs.append(_rms_bwd(dw[:, sl], w, r))
        dvv = jnp.concatenate(dvs, axis=1)
        dy = dvv * sz
        return dy, dvv * yv * _dsilu(zv), _sum0(dv * jnp.concatenate(ws, axis=1)), _sum0(dy * xs)
    return _rowwise("gated_norm_bwd", body, [y, z, dyn, (xbc_c, d_ssm, 0)], [d_skip_x, g_ssm],
                    [(d_ssm, BF16), (d_ssm, BF16)], [d_ssm, d_ssm], plan=plan)


def _ssd_grad_merge(dxs_f, dxs_b, dy, db_f, db_b, dc_f, dc_b, d_skip_x):
    d_ssm = dy.shape[1]
    width = d_ssm + 2 * N_GROUPS * D_STATE

    def body(xf, xb, dv, bf, bb, cf, cbv, dsk):
        return (jnp.concatenate([xf + xb + dsk * dv, bf + bb, cf + cbv], axis=1),)
    return _rowwise("ssd_grad_merge", body, [dxs_f, dxs_b, dy, db_f, db_b, dc_f, dc_b], [d_skip_x], [(width, BF16)])[0]


def _adamw(name, w, g, m, v):
    c = w.shape[1]
    c1 = 1.0 - ADAM_B1 ** ADAM_STEP
    c2 = 1.0 - ADAM_B2 ** ADAM_STEP

    def body(wv, gv, mv, vv):
        mn = ADAM_B1 * mv + (1.0 - ADAM_B1) * gv
        vn = ADAM_B2 * vv + (1.0 - ADAM_B2) * (gv * gv)
        delta = -ADAM_LR * ((mn / c1) / (jnp.sqrt(vn / c2) + ADAM_EPS) + ADAM_WD * wv)
        return delta, mn, vn
    return _rowwise(name, body, [w, g, m, v], [], [(c, F32)] * 3)


def _local_step(x, tgt, mod, wts, sm, late=None):
    s, d = x.shape
    d_ssm = 2 * d
    n_heads = d_ssm // HEAD_DIM
    d_xbc = d_ssm + 2 * N_GROUPS * D_STATE
    sec = [0, d_ssm, d_ssm + d_xbc, d_ssm + d_xbc + 2 * n_heads, d_ssm + d_xbc + 2 * n_heads + 2 * d]
    sec.append(sec[-1] + 2 * d)
    sh1, sc1, g1, sh2, sc2, g2 = [mod[:, i * d:(i + 1) * d] for i in range(N_MOD)]
    win_t = wts["w_in_t"]
    sections = [(nm, (sec[i], sec[i + 1] - sec[i])) for i, nm in enumerate(("z", "xbc", "dt", "glu", "gate"))]

    h1 = _norm_mod_fwd("pre_mix_norm", x, sm["g_pre_mix"], sc1, sh1)
    z, xbc, dt_raw, glu_in, gate_l = [
        _matmul(f"proj_{nm}", h1, win_t, tb=True, b_rows=rows, out_dtype=F32 if nm == "dt" else BF16)
        for nm, rows in sections]
    plan_c = plan_f = plan_r = None
    if late is not None:
        plan_c = _gather_plan([late[n] for n in MATRICES[1:4]])
        plan_f = _gather_plan([late["w_down"]])
        plan_r = _gather_plan([late["w_gate_up"]])
    xbc_c = _dwconv_fwd("ssm_conv_fwd", xbc, sm["w_conv_ssm"], sm["b_conv_ssm"], silu=True, out_dtype=BF16, plan=plan_c)
    if late is not None:
        xbc_c, got_c = xbc_c
    tables = _scan_tables(n_heads)
    dt, q_all, gam_t, etot = _dt_prepare(dt_raw, sm["dt_bias"], sm["a_neg"], tables["route"])
    etot = etot.reshape(s // CHUNK, 8, 2 * n_heads)
    (etx_f, etg_f), (etx_r, etg_r) = [_chunk_decay(etot, n_heads, direction) for direction in (0, 1)]
    y_f, hs_f, got_f = _scan_fwd("ssd_fwd_f", xbc_c, q_all, gam_t, etx_f, tables, direction=0, d_ssm=d_ssm, plan=plan_f)
    y_r, hs_r, got_r = _scan_fwd("ssd_fwd_r", xbc_c, q_all, gam_t, etx_r, tables, direction=1, d_ssm=d_ssm, plan=plan_r)
    if late is not None:
        wts = dict(wts, w_ssm_out=got_c[0].reshape(-1, d), w_conv_out=got_c[1].reshape(-1, d),
                   w_mix_out=got_c[2].reshape(-1, d), w_down=got_f[0].reshape(-1, d), w_gate_up=got_r[0])
    y_ssd, yn = _gated_norm_fwd(y_f, y_r, xbc_c, z, sm["d_skip_x"], sm["g_ssm_norm"])
    y_a = _matmul("ssm_out", yn, wts["w_ssm_out"])
    u0 = _glu_fwd(glu_in, sm["b_glu"])
    u1 = _dwconv_fwd("dw_conv_fwd", u0, sm["w_dw"], sm["b_dw"], silu=False)
    u = _ln_silu_fwd(u1, sm["ln_g"], sm["ln_b"])
    y_b = _matmul("conv_out", u, wts["w_conv_out"], bias=sm["b_conv_out"])
    mixin = _gate_merge_fwd(y_a, y_b, gate_l, sm["b_gate"])
    mix = _matmul("mix_out", mixin, wts["w_mix_out"])
    x1 = _gated_residual_fwd("post_mix_residual", x, mix, g1, sm["g_post_mix"])
    h2 = _norm_mod_fwd("pre_ffn_norm", x1, sm["g_pre_ffn"], sc2, sh2)
    gu = _matmul("ffn_gate_up", h2, wts["w_gate_up"], b_blocks=N_CHIPS, out_dtype=BF16)
    act = _swiglu_fwd(gu)
    f = _matmul("ffn_down", act, wts["w_down"])

    dx2, df, sq, d_g2, d_gpf = _final_residual_loss(x1, f, tgt, g2, sm["g_post_ffn"])
    dact = _matmul("d_act", df, wts["w_down"], tb=True, out_dtype=BF16)
    g_w_down = _matmul("g_w_down", act, df, ta=True, out_dtype=BF16)
    dgu = _swiglu_bwd(gu, dact)
    dh2 = _matmul("d_h2", dgu, wts["w_gate_up"], tb=True, b_blocks=N_CHIPS)
    g_w_gate_up = _matmul("g_w_gate_up", h2, dgu, ta=True, out_dtype=BF16, out_blocks=N_CHIPS)
    dx1, d_gpre_ffn, d_sc2, d_sh2 = _norm_mod_bwd("pre_ffn_norm_bwd", x1, dh2, dx2, sm["g_pre_ffn"], sc2)
    dmix, d_g1, d_gpm = _gated_residual_bwd("post_mix_residual_bwd", mix, dx1, g1, sm["g_post_mix"])
    dmixin = _matmul("d_mixin", dmix, wts["w_mix_out"], tb=True, out_dtype=BF16)
    g_w_mix = _matmul("g_w_mix_out", mixin, dmix, ta=True, out_dtype=BF16)
    dy_a, dy_b, dgate_l, d_bgate, d_bco = _gate_merge_bwd(dmixin, y_a, y_b, gate_l, sm["b_gate"])
    du = _matmul("d_u", dy_b, wts["w_conv_out"], tb=True, out_dtype=BF16)
    g_w_co = _matmul("g_w_conv_out", u, dy_b, ta=True, out_dtype=BF16)
    du1, d_lng, d_lnb = _ln_silu_bwd(u1, du, sm["ln_g"], sm["ln_b"])
    du0, d_wdw, d_bdw = _dwconv_bwd("dw_conv_bwd", u0, sm["w_dw"], sm["b_dw"], du1, silu=False)
    dglu, d_bglu = _glu_bwd(glu_in, sm["b_glu"], du0)
    dyn = _matmul("d_yn", dy_a, wts["w_ssm_out"], tb=True, out_dtype=BF16)
    g_w_ssm = _matmul("g_w_ssm_out", yn, dy_a, ta=True, out_dtype=BF16)
    early = _by_halves([g_w_ssm.reshape(N_CHIPS, -1, d), g_w_co.reshape(N_CHIPS, -1, d), g_w_mix.reshape(N_CHIPS, -1, d),
                        g_w_gate_up, g_w_down.reshape(N_CHIPS, -1, d)])
    dy_ssd, dz, d_gssm, d_dskip_x, *from_sibling = _gated_norm_bwd(
        y_ssd, z, dyn, xbc_c, sm["d_skip_x"], sm["g_ssm_norm"], plan=None if late is None else _sibling_plan(early))
    plan_b = sums = None
    if late is not None:
        sums = _chip_sums("early", early, from_sibling)
        plan_b = _send_chips_plan(sums)
    (dxs_f, db_f, dc_f, ddt_f, da_f), received = _scan_bwd(
        "ssd_bwd_f", xbc_c, dy_ssd, hs_f, q_all, gam_t, etx_f, etg_f, tables, direction=0, d_ssm=d_ssm, plan=plan_b)
    (dxs_r, db_r, dc_r, ddt_r, da_r), _ = _scan_bwd(
        "ssd_bwd_r", xbc_c, dy_ssd, hs_r, q_all, gam_t, etx_r, etg_r, tables, direction=1, d_ssm=d_ssm)
    ddt_raw, d_dtbias, d_alog = _dt_backward((da_f, da_r), (ddt_f, ddt_r), dt, dt_raw, sm["dt_bias"], sm["a_neg"])
    dxbc_c = _ssd_grad_merge(dxs_f, dxs_r, dy_ssd, db_f, db_r, dc_f, dc_r, sm["d_skip_x"])
    dxbc, d_wconv, d_bconv = _dwconv_bwd("ssm_conv_bwd", xbc, sm["w_conv_ssm"], sm["b_conv_ssm"], dxbc_c,
                                         silu=True, dx_dtype=BF16)
    dsecs = [dz, dxbc, ddt_raw, dglu, dgate_l]
    dh1 = g_win = None
    for (nm, rows), dsec in zip(sections, dsecs):
        g_win = _matmul(f"g_w_in_{nm}", dsec, h1, ta=True, out_dtype=BF16, out_rows=(rows[0], sec[-1], g_win))
    hosts = {} if late is None else {"z": "sibling", "xbc": 0, "glu": 1, "gate": 2}
    halves_in = _by_halves([g_win.reshape(N_CHIPS, -1, d)])
    sums_in, received_in = None, [None] * 3
    for (nm, rows), dsec in zip(sections, dsecs):
        host = hosts.get(nm)
        plan_in = None if host is None else _sibling_plan(halves_in) if host == "sibling" else _send_chips_plan(sums_in, (host,))
        dh1 = _matmul(f"d_h1_{nm}", dsec, win_t, b_rows=rows, add=dh1, plan=plan_in)
        if host == "sibling":
            dh1, from_sibling_in = dh1
            sums_in = _chip_sums("w_in", halves_in, from_sibling_in)
        elif host is not None:
            dh1, (received_in[host],) = dh1
    grad_x, d_gpre_mix, d_sc1, d_sh1 = _norm_mod_bwd("pre_mix_norm_bwd", x, dh1, dx1, sm["g_pre_mix"], sc1)

    dmod = jnp.concatenate([d_sh1, d_sc1, d_g1, d_sh2, d_sc2, d_g2], axis=1)
    if late is None:
        big = {"w_in_t": g_win, "w_ssm_out": g_w_ssm, "w_conv_out": g_w_co, "w_mix_out": g_w_mix,
               "w_gate_up": g_w_gate_up, "w_down": g_w_down}
    else:
        big = {"pending": (sums_in + sums, [received_in] + [[t, t, t] for t in received])}
    small = {"g_pre_mix": d_gpre_mix, "g_post_mix": d_gpm, "w_conv_ssm": d_wconv, "b_conv_ssm": d_bconv,
             "dt_bias": d_dtbias, "a_log": d_alog, "d_skip_x": d_dskip_x, "g_ssm_norm": d_gssm, "b_glu": d_bglu,
             "w_dw": d_wdw, "b_dw": d_bdw, "ln_g": d_lng, "ln_b": d_lnb, "b_conv_out": d_bco, "b_gate": d_bgate,
             "g_pre_ffn": d_gpre_ffn, "g_post_ffn": d_gpf}
    return sq, grad_x, big, small, dmod


ANY = pl.BlockSpec(memory_space=pl.ANY)
WHOLE_VMEM = pl.BlockSpec(memory_space=pltpu.VMEM)


def _mesh_place():
    x, y, c = lax.axis_index("x"), lax.axis_index("y"), lax.axis_index("c")
    other_chips = [(1 - x, y), (x, 1 - y), (1 - x, 1 - y)]
    return x, y, c, other_chips


def _remote(src, dst, send_sems, recv_sems, k, device):
    return pltpu.make_async_remote_copy(src_ref=src, dst_ref=dst, send_sem=send_sems.at[k], recv_sem=recv_sems.at[k],
                                        device_id=device, device_id_type=MESH)


def _gather_devices(name, block):
    m_per, n = block.shape

    def body(x_ref, out_ref, send_sems, recv_sems, local_sem):
        x, y, c, chips = _mesh_place()
        me, sibling = (x, y, c), (x, y, 1 - c)

        def rows(px, py, pc):
            return out_ref.at[pl.ds((4 * px + 2 * py + pc) * m_per, m_per), :]

        def copy(k, blk, to, src=None):
            return _remote(rows(*blk) if src is None else src, rows(*blk), send_sems, recv_sems, k, to)

        mine = pltpu.make_async_copy(x_ref, rows(*me), local_sem)
        mine.start()
        first = [copy(0, me, sibling, src=x_ref)]
        first += [copy(1 + j, me, (*chip, c), src=x_ref) for j, chip in enumerate(chips)]
        for cp in first:
            cp.start()
        passed = [copy(4 + j, (*chip, c), sibling) for j, chip in enumerate(chips)]
        for j, chip in enumerate(chips):
            copy(1 + j, (*chip, c), me).wait_recv()
            passed[j].start()
        copy(0, sibling, me).wait_recv()
        for j, chip in enumerate(chips):
            copy(4 + j, (*chip, 1 - c), me).wait_recv()
        for cp in first + passed:
            cp.wait_send()
        mine.wait()

    return pl.pallas_call(
        body, name=name, out_shape=jax.ShapeDtypeStruct((N_DEV * m_per, n), block.dtype),
        in_specs=[WHOLE_VMEM], out_specs=WHOLE_VMEM,
        scratch_shapes=[pltpu.SemaphoreType.DMA((7,)), pltpu.SemaphoreType.DMA((7,)), pltpu.SemaphoreType.DMA],
        compiler_params=pltpu.CompilerParams(vmem_limit_bytes=VMEM_LIMIT),
    )(block)


class _Plan:
    def __init__(self, operands, out_shapes, copies, phases):
        self.operands, self.out_shapes, self.copies, self.phases = list(operands), list(out_shapes), copies, phases

    def sems(self):
        return [pltpu.SemaphoreType.DMA((self.copies,)), pltpu.SemaphoreType.DMA((self.copies,))]


def _run_plan(name, plan):
    n_in, n_out = len(plan.operands), len(plan.out_shapes)

    def body(*refs):
        ins, outs = refs[:n_in], refs[n_in:n_in + n_out]
        send_sems, recv_sems = refs[n_in + n_out:]
        for phase in ("start", "middle", "end"):
            if phase in plan.phases:
                plan.phases[phase](ins, outs, send_sems, recv_sems)

    return list(pl.pallas_call(body, name=name, out_shape=plan.out_shapes, in_specs=[ANY] * n_in,
                               out_specs=[ANY] * n_out, scratch_shapes=plan.sems())(*plan.operands))


def _gather_plan(shards):
    n = len(shards)

    def copies(kinds, ins, outs, send_sems, recv_sems):
        x, y, c, chips = _mesh_place()
        me = 2 * x + y
        sibling = (x, y, 1 - c)

        def half(i, h):
            hr = ins[i].shape[0] // 2
            return pl.ds(h * hr, hr)

        def block(i, j, h):
            cx, cy = chips[j]
            return outs[i].at[2 * cx + cy, half(i, h)]

        make = {
            "over_ici": lambda i, j: _remote(ins[i].at[half(i, c)], outs[i].at[me, half(i, c)], send_sems, recv_sems,
                                             6 * i + j, (*chips[j], c)),
            "arrived": lambda i, j: _remote(block(i, j, c), block(i, j, c), send_sems, recv_sems, 6 * i + j, (*chips[j], c)),
            "passed_on": lambda i, j: _remote(block(i, j, c), block(i, j, c), send_sems, recv_sems, 6 * i + 3 + j, sibling),
            "from_sibling": lambda i, j: _remote(block(i, j, 1 - c), block(i, j, 1 - c), send_sems, recv_sems,
                                                 6 * i + 3 + j, sibling),
        }
        res = []
        for kind in kinds:
            if kind == "own":
                res.append([_remote(ins[i], outs[i].at[me], send_sems, recv_sems, 6 * n + i, sibling) for i in range(n)])
            else:
                res.append([make[kind](i, j) for i in range(n) for j in range(3)])
        return res

    def start(*refs):
        over_ici, own = copies(("over_ici", "own"), *refs)
        for cp in over_ici + own:
            cp.start()

    def end(*refs):
        arrived, passed_on = copies(("arrived", "passed_on"), *refs)
        for got, fwd in zip(arrived, passed_on):
            got.wait_recv()
            fwd.start()
        from_sibling, own_in = copies(("from_sibling", "own"), *refs)
        for cp in from_sibling + own_in:
            cp.wait_recv()
        over_ici, passed_on, own_out = copies(("over_ici", "passed_on", "own"), *refs)
        for cp in over_ici + passed_on + own_out:
            cp.wait_send()

    return _Plan(shards, [jax.ShapeDtypeStruct((N_CHIPS,) + s.shape, s.dtype) for s in shards], 7 * n,
                 {"start": start, "end": end})


def _sibling_plan(grads):
    n = len(grads)

    def copies(ins, outs, send_sems, recv_sems):
        x, y, c, _ = _mesh_place()
        return [_remote(ins[i].at[j, 1 - c], outs[i].at[j], send_sems, recv_sems, N_CHIPS * i + j, (x, y, 1 - c))
                for i in range(n) for j in range(N_CHIPS)]

    def start(*refs):
        for cp in copies(*refs):
            cp.start()

    def end(*refs):
        for cp in copies(*refs):
            cp.wait_recv()
        for cp in copies(*refs):
            cp.wait_send()

    return _Plan(grads, [jax.ShapeDtypeStruct((g.shape[0],) + g.shape[2:], g.dtype) for g in grads], N_CHIPS * n,
                 {"start": start, "end": end})


def _send_chips_plan(sums, neighbours=(0, 1, 2)):
    n = len(sums)

    def copies(ins, outs, send_sems, recv_sems):
        x, y, c, chips = _mesh_place()
        return [_remote(ins[i].at[2 * chips[j][0] + chips[j][1]], outs[i].at[j], send_sems, recv_sems, 3 * i + j,
                        (*chips[j], c))
                for i in range(n) for j in neighbours]

    def start(*refs):
        for cp in copies(*refs):
            cp.start()

    def end(*refs):
        for cp in copies(*refs):
            cp.wait_recv()
        for cp in copies(*refs):
            cp.wait_send()

    return _Plan(sums, [jax.ShapeDtypeStruct((3,) + g.shape[1:], g.dtype) for g in sums], 3 * n,
                 {"start": start, "end": end})


def _exchange_halves(name, shards):
    n = len(shards)

    def body(*refs):
        outs = refs[n:2 * n]
        send_sems, recv_sems = refs[2 * n:]
        x, y, c, _ = _mesh_place()
        sibling = (x, y, 1 - c)
        remote = [_remote(outs[i].at[c], outs[i].at[c], send_sems, recv_sems, i, sibling) for i in range(n)]
        for cp in remote:
            cp.start()
        for i in range(n):
            _remote(outs[i].at[1 - c], outs[i].at[1 - c], send_sems, recv_sems, i, sibling).wait_recv()
        for cp in remote:
            cp.wait_send()

    return pl.pallas_call(
        body, name=name,
        out_shape=[jax.ShapeDtypeStruct(h.shape, h.dtype) for h in shards],
        in_specs=[ANY] * n, out_specs=[ANY] * n, input_output_aliases={i: i for i in range(n)},
        scratch_shapes=[pltpu.SemaphoreType.DMA((n,)), pltpu.SemaphoreType.DMA((n,))],
    )(*shards)


def _divisor_tile(rows, row_bytes, quantum=16):
    best = rows
    for t in range(quantum, rows + 1, quantum):
        if rows % t == 0 and 2 * t * row_bytes <= ROW_TILE_BUDGET:
            best = t
    return best


def _add_sibling(name, g4, t1):
    nb, _, hr, cols = g4.shape
    t = _divisor_tile(hr, cols * 6)

    def kern(g_ref, t_ref, o_ref):
        o_ref[0] = (g_ref[0, 0].astype(F32) + t_ref[0].astype(F32)).astype(o_ref.dtype)

    return pl.pallas_call(
        kern, name=name, grid=(nb, hr // t),
        in_specs=[pl.BlockSpec((1, 1, t, cols), lambda j, i: (j, lax.axis_index("c"), i, 0)),
                  pl.BlockSpec((1, t, cols), lambda j, i: (j, i, 0))],
        out_specs=pl.BlockSpec((1, t, cols), lambda j, i: (j, i, 0)),
        out_shape=jax.ShapeDtypeStruct((nb, hr, cols), g4.dtype),
        compiler_params=_params(("parallel", "parallel")),
    )(g4, t1)


def _add_chips(name, s1, t3):
    _, hr, cols = s1.shape
    t = _divisor_tile(hr, cols * 12)

    def kern(s_ref, t0_ref, t1_ref, t2_ref, o_ref):
        acc = s_ref[0].astype(F32)
        for t_ref in (t0_ref, t1_ref, t2_ref):
            acc = acc + t_ref[0].astype(F32)
        o_ref[0] = acc

    slot = lambda j: pl.BlockSpec((1, t, cols), functools.partial(lambda i, j: (j, i, 0), j=j))
    return pl.pallas_call(
        kern, name=name, grid=(hr // t,),
        in_specs=[pl.BlockSpec((1, t, cols), lambda i: (2 * lax.axis_index("x") + lax.axis_index("y"), i, 0)),
                  slot(0), slot(1), slot(2)],
        out_specs=pl.BlockSpec((1, t, cols), lambda i: (lax.axis_index("c"), i, 0)),
        out_shape=jax.ShapeDtypeStruct((2, hr, cols), F32),
        compiler_params=_params(("parallel",)),
    )(s1, *t3)


def _by_halves(grads):
    return [g.reshape(N_CHIPS, 2, g.shape[1] // 2, g.shape[2]) for g in grads]


def _chip_sums(tag, halves, from_sibling):
    return [_add_sibling(f"chip_sum_{tag}_{i}", g, t) for i, (g, t) in enumerate(zip(halves, from_sibling))]


def _shard_sums(sums, received):
    halves = [_add_chips(f"shard_sum_{i}", s, t) for i, (s, t) in enumerate(zip(sums, received))]
    full = _exchange_halves("grad_halves_to_sibling", halves)
    return [f.reshape(f.shape[1] * 2, f.shape[2]) for f in full]


def _pack_rows(size, width):
    return -(-size // (8 * width)) * 8


def _pack(arrays, width):
    parts = []
    for a in arrays:
        flat = a.reshape(-1).astype(F32)
        rows = _pack_rows(flat.shape[0], width)
        parts.append(jnp.pad(flat, (0, rows * width - flat.shape[0])).reshape(rows, width))
    return jnp.concatenate(parts, axis=0)


def _unpack(block, shapes, width):
    out, r = [], 0
    for shp in shapes:
        size = 1
        for s_ in shp:
            size *= s_
        rows = _pack_rows(size, width)
        out.append(block[r:r + rows].reshape(-1)[:size].reshape(shp))
        r += rows
    return out


SMALL_PARAMS = ("b_ada", "g_pre_mix", "g_post_mix", "b_conv_ssm", "dt_bias_fwd", "dt_bias_bwd", "a_log_fwd", "a_log_bwd",
                "d_skip", "g_ssm_norm", "b_glu", "b_dw", "ln_g", "ln_b", "b_conv_out", "b_gate", "g_pre_ffn", "g_post_ffn")
SHARDED_SMALL = ("w_conv_ssm", "w_dw")
MATRICES = ("w_in", "w_ssm_out", "w_conv_out", "w_mix_out", "w_gate_up", "w_down")
ALL_PARAMS = ("w_ada", "b_ada", "g_pre_mix", "g_post_mix", "w_in", "w_conv_ssm", "b_conv_ssm", "dt_bias_fwd", "dt_bias_bwd",
              "a_log_fwd", "a_log_bwd", "d_skip", "g_ssm_norm", "w_ssm_out", "b_glu", "w_dw", "b_dw", "ln_g", "ln_b",
              "w_conv_out", "b_conv_out", "b_gate", "w_mix_out", "g_pre_ffn", "g_post_ffn", "w_gate_up", "w_down")
COND_ROWS = 48
COND_CONV_ROW = 8
COND_DW_ROW = 16
MOD_ROWS = 16


def kernel(x, c, w_ada, b_ada, g_pre_mix, g_post_mix, w_in, w_conv_ssm, b_conv_ssm, dt_bias_fwd, dt_bias_bwd, a_log_fwd, a_log_bwd, d_skip, g_ssm_norm, w_ssm_out, b_glu, w_dw, b_dw, ln_g, ln_b, w_conv_out, b_conv_out, b_gate, w_mix_out, g_pre_ffn, g_post_ffn, w_gate_up, w_down, loss_target, m_w_ada, m_b_ada, m_g_pre_mix, m_g_post_mix, m_w_in, m_w_conv_ssm, m_b_conv_ssm, m_dt_bias_fwd, m_dt_bias_bwd, m_a_log_fwd, m_a_log_bwd, m_d_skip, m_g_ssm_norm, m_w_ssm_out, m_b_glu, m_w_dw, m_b_dw, m_ln_g, m_ln_b, m_w_conv_out, m_b_conv_out, m_b_gate, m_w_mix_out, m_g_pre_ffn, m_g_post_ffn, m_w_gate_up, m_w_down, v_w_ada, v_b_ada, v_g_pre_mix, v_g_post_mix, v_w_in, v_w_conv_ssm, v_b_conv_ssm, v_dt_bias_fwd, v_dt_bias_bwd, v_a_log_fwd, v_a_log_bwd, v_d_skip, v_g_ssm_norm, v_w_ssm_out, v_b_glu, v_w_dw, v_b_dw, v_ln_g, v_ln_b, v_w_conv_out, v_b_conv_out, v_b_gate, v_w_mix_out, v_g_pre_ffn, v_g_post_ffn, v_w_gate_up, v_w_down):
    given = dict(locals())
    wgt = {n: given[n][0] for n in ALL_PARAMS}
    mom = {n: given["m_" + n][0] for n in ALL_PARAMS}
    var = {n: given["v_" + n][0] for n in ALL_PARAMS}
    xs, tgt = x[0], loss_target[0]
    s, d = xs.shape
    d_ssm = 2 * d
    n_heads = d_ssm // HEAD_DIM
    d_xbc = d_ssm + 2 * N_GROUPS * D_STATE
    xi, yi, ci = lax.axis_index("x"), lax.axis_index("y"), lax.axis_index("c")
    chip = 2 * xi + yi
    dev = 2 * chip + ci
    k_conv, k_dw = wgt["w_conv_ssm"].shape[0], wgt["w_dw"].shape[0]
    xbc_shard, dw_shard = d_xbc // N_CHIPS, d // N_CHIPS

    width1 = max(d, xbc_shard)
    blk = jnp.zeros((COND_ROWS, width1), F32)
    blk = blk.at[0, :d].set(c[0])
    blk = blk.at[COND_CONV_ROW:COND_CONV_ROW + k_conv, :xbc_shard].set(wgt["w_conv_ssm"])
    blk = blk.at[COND_DW_ROW:COND_DW_ROW + k_dw, :dw_shard].set(wgt["w_dw"])
    g1 = _gather_devices("gather_cond", blk).reshape(N_DEV, COND_ROWS, width1)
    c_all = g1[:, 0, :d]
    w_conv_full = jnp.concatenate([g1[2 * k, COND_CONV_ROW:COND_CONV_ROW + k_conv, :xbc_shard] for k in range(N_CHIPS)], axis=1)
    w_dw_full = jnp.concatenate([g1[2 * k, COND_DW_ROW:COND_DW_ROW + k_dw, :dw_shard] for k in range(N_CHIPS)], axis=1)
    c_act = jnp.pad(c_all * _sigmoid(c_all), ((0, MOD_ROWS - N_DEV), (0, 0)))

    mod_part = _matmul("ada_mod", c_act, wgt["w_ada"])
    g2 = _gather_devices("gather_mod", mod_part).reshape(N_DEV, MOD_ROWS, mod_part.shape[1])
    mod_all = jnp.concatenate([g2[2 * k, :N_DEV] for k in range(N_CHIPS)], axis=1) + wgt["b_ada"][None]
    mod = lax.dynamic_slice_in_dim(mod_all, dev, 1, axis=0)

    shards = [wgt["w_in"].T.astype(BF16)] + [wgt[n].astype(BF16) for n in MATRICES[1:]]
    wts = {"w_in_t": _run_plan("gather_w_in", _gather_plan(shards[:1]))[0].reshape(-1, d)}
    late = dict(zip(MATRICES[1:], shards[1:]))
    row = lambda v: v.reshape(1, -1)
    sm = {"g_pre_mix": row(wgt["g_pre_mix"]), "g_post_mix": row(wgt["g_post_mix"]), "w_conv_ssm": w_conv_full,
          "b_conv_ssm": row(wgt["b_conv_ssm"]),
          "dt_bias": row(jnp.concatenate([wgt["dt_bias_fwd"], wgt["dt_bias_bwd"]])),
          "a_neg": row(-jnp.exp(jnp.concatenate([wgt["a_log_fwd"], wgt["a_log_bwd"]]))),
          "d_skip_x": row(jnp.repeat(wgt["d_skip"], HEAD_DIM)), "g_ssm_norm": row(wgt["g_ssm_norm"]),
          "b_glu": row(wgt["b_glu"]), "w_dw": w_dw_full, "b_dw": row(wgt["b_dw"]), "ln_g": row(wgt["ln_g"]),
          "ln_b": row(wgt["ln_b"]), "b_conv_out": row(wgt["b_conv_out"]), "b_gate": row(wgt["b_gate"]),
          "g_pre_ffn": row(wgt["g_pre_ffn"]), "g_post_ffn": row(wgt["g_post_ffn"])}

    sq, grad_x, big, small, dmod = _local_step(xs, tgt, mod, wts, sm, late=late)
    loss = lax.psum((0.5 / d) * sq[0, 0], ("x", "y", "c"))

    local_small = {"b_ada": dmod, "g_pre_mix": small["g_pre_mix"], "g_post_mix": small["g_post_mix"],
                   "b_conv_ssm": small["b_conv_ssm"], "dt_bias_fwd": small["dt_bias"][:, :n_heads],
                   "dt_bias_bwd": small["dt_bias"][:, n_heads:], "a_log_fwd": small["a_log"][:, :n_heads],
                   "a_log_bwd": small["a_log"][:, n_heads:],
                   "d_skip": jnp.sum(small["d_skip_x"].reshape(n_heads, HEAD_DIM), axis=1),
                   "g_ssm_norm": small["g_ssm_norm"], "b_glu": small["b_glu"], "b_dw": small["b_dw"],
                   "ln_g": small["ln_g"], "ln_b": small["ln_b"], "b_conv_out": small["b_conv_out"],
                   "b_gate": small["b_gate"], "g_pre_ffn": small["g_pre_ffn"], "g_post_ffn": small["g_post_ffn"],
                   "w_conv_ssm": small["w_conv_ssm"], "w_dw": small["w_dw"]}
    names = SMALL_PARAMS + SHARDED_SMALL
    pack = _pack([local_small[n] for n in names], d)
    rows_p = pack.shape[0]
    g3 = _gather_devices("gather_small_grads", pack).reshape(N_DEV, rows_p, d)
    total = _rowwise("sum_small_grads", lambda *blocks: (functools.reduce(lambda a, b: a + b, blocks),),
                     [g3[i] for i in range(N_DEV)], [], [(d, F32)])[0]
    full_shapes = [wgt[n].shape for n in SMALL_PARAMS] + [(k_conv, d_xbc), (k_dw, d)]
    summed = dict(zip(names, _unpack(total, full_shapes, d)))
    grads = {n: summed[n] for n in SMALL_PARAMS}
    grads["w_conv_ssm"] = lax.dynamic_slice_in_dim(summed["w_conv_ssm"], chip * xbc_shard, xbc_shard, axis=1)
    grads["w_dw"] = lax.dynamic_slice_in_dim(summed["w_dw"], chip * dw_shard, dw_shard, axis=1)

    dmod_all = g3[:, :N_MOD, :].reshape(N_DEV, N_MOD * d)
    ada_cols = wgt["w_ada"].shape[1]
    dmod_cols = jnp.pad(lax.dynamic_slice_in_dim(dmod_all, chip * ada_cols, ada_cols, axis=1),
                        ((0, MOD_ROWS - N_DEV), (0, 0)))
    grads["w_ada"] = _matmul("g_w_ada", c_act, dmod_cols, ta=True)

    reduced = _shard_sums(*big["pending"])
    grads["w_in"] = reduced[0].T
    for n, g in zip(MATRICES[1:], reduced[1:]):
        grads[n] = g

    delta, new_m, new_v = {}, {}, {}
    for n in ("w_ada",) + MATRICES:
        delta[n], new_m[n], new_v[n] = _adamw("adamw_" + n, wgt[n], grads[n], mom[n], var[n])
    for group, width, tag in ((SMALL_PARAMS, d, "small"), (SHARDED_SMALL, LANES, "conv")):
        shapes = [wgt[n].shape for n in group]
        packs = [_pack([src[n] for n in group], width) for src in (wgt, grads, mom, var)]
        outs = _adamw("adamw_" + tag, *packs)
        for res, o in zip((delta, new_m, new_v), outs):
            res.update(zip(group, _unpack(o, shapes, width)))

    lead = lambda a: a[None]
    return (loss, grad_x[None], *[lead(grads[n]) for n in ALL_PARAMS], *[lead(delta[n]) for n in ALL_PARAMS],
            *[lead(new_m[n]) for n in ALL_PARAMS], *[lead(new_v[n]) for n in ALL_PARAMS])
```

```python
import functools

import jax
import jax.numpy as jnp
from jax import lax
from jax.experimental import pallas as pl
from jax.experimental.pallas import tpu as pltpu

F32 = jnp.float32
BF16 = jnp.bfloat16

N_GROUPS = 8
HEAD_DIM = 64
D_STATE = 128
CHUNK = 128
EPS = 1e-6
N_MOD = 6
ADAM_LR = 0.001
ADAM_B1 = 0.9
ADAM_B2 = 0.999
ADAM_EPS = 1e-08
ADAM_WD = 0.01
ADAM_STEP = 10

V7X_VMEM_BYTES = 64 * 1024 * 1024
VMEM_LIMIT = V7X_VMEM_BYTES - 8 * 1024 * 1024
ROW_TILE_BUDGET = 20 * 1024 * 1024
LANES = 128
NEG = -1e30
MESH = pl.DeviceIdType.MESH
N_CHIPS = 4
N_DEV = 8


def _params(sem):
    return pltpu.CompilerParams(dimension_semantics=sem, vmem_limit_bytes=VMEM_LIMIT)


def _sigmoid(x):
    return 1.0 / (1.0 + jnp.exp(-x))


def _silu(x):
    return x * _sigmoid(x)


def _dsilu(x):
    s = _sigmoid(x)
    return s * (1.0 + x * (1.0 - s))


def _softplus(x):
    return jnp.maximum(x, 0.0) + jnp.log(1.0 + jnp.exp(-jnp.abs(x)))


def _sum0(a):
    return jnp.sum(a, axis=0, keepdims=True)


def _mean1(a):
    return jnp.mean(a, axis=1, keepdims=True)


def _rowwise(name, body, rows, bcasts, out_rows, out_accs=(), tile=None, plan=None):
    rows = [r if isinstance(r, tuple) else (r, r.shape[1], 0) for r in rows]
    s = rows[0][0].shape[0]
    if tile is None:
        per_row = sum(w * a.dtype.itemsize for a, w, _ in rows) + sum(w * jnp.dtype(dt).itemsize for w, dt in out_rows)
        tile = 1024
        while tile > 16 and (tile * per_row * 2 > ROW_TILE_BUDGET or s % tile):
            tile //= 2
        if s * per_row * 2 <= ROW_TILE_BUDGET:
            tile = s
    assert s % tile == 0
    n_in = len(rows) + len(bcasts)
    n_o = len(out_rows)
    n_out = n_o + len(out_accs)
    p_ops, p_in_specs, p_out_shapes, p_out_specs, p_scratch, _ = _plan_call_parts(plan)
    n_pi, n_po = len(p_ops), len(p_out_shapes)
    n_steps = (s // tile,)

    def kern(*refs):
        plan_ins = refs[n_in:n_in + n_pi]
        outs = refs[n_in + n_pi:n_in + n_pi + n_out]
        plan_outs = refs[n_in + n_pi + n_out:n_in + n_pi + n_out + n_po]
        plan_sems = refs[n_in + n_pi + n_out + n_po:]
        _plan_phase(plan, "start", n_steps, plan_ins, plan_outs, plan_sems)
        _plan_phase(plan, "middle", n_steps, plan_ins, plan_outs, plan_sems)
        res = body(*[r[...].astype(F32) for r in refs[:n_in]])
        for o, v in zip(outs[:n_o], res[:n_o]):
            o[...] = v.astype(o.dtype)
        if out_accs:
            @pl.when(pl.program_id(0) == 0)
            def _():
                for o in outs[n_o:]:
                    o[...] = jnp.zeros_like(o)
            for o, v in zip(outs[n_o:], res[n_o:]):
                o[...] += v
        _plan_phase(plan, "end", n_steps, plan_ins, plan_outs, plan_sems)

    in_specs = [pl.BlockSpec((tile, w), functools.partial(lambda i, cb: (i, cb), cb=cb)) for _, w, cb in rows]
    in_specs += [pl.BlockSpec(b.shape, functools.partial(lambda i, nd: (0,) * nd, nd=b.ndim)) for b in bcasts]
    out_shape = [jax.ShapeDtypeStruct((s, w), dt) for w, dt in out_rows]
    out_shape += [jax.ShapeDtypeStruct((1, w), F32) for w in out_accs]
    out_specs = [pl.BlockSpec((tile, w), lambda i: (i, 0)) for w, _ in out_rows]
    out_specs += [pl.BlockSpec((1, w), lambda i: (0, 0)) for w in out_accs]
    return pl.pallas_call(
        kern, name=name, grid=n_steps, in_specs=in_specs + p_in_specs, out_specs=out_specs + p_out_specs,
        out_shape=out_shape + p_out_shapes, scratch_shapes=p_scratch,
        compiler_params=_params(("arbitrary",) if out_accs or plan is not None else ("parallel",)),
    )(*[a for a, _, _ in rows], *bcasts, *p_ops)


def _tile(n, pref):
    if n <= pref:
        return n
    t = (pref // LANES) * LANES
    while t >= LANES:
        if n % t == 0:
            return t
        t -= LANES
    return n


def _matmul(name, a, b, *, ta=False, tb=False, out_dtype=F32, bias=None, add=None, b_blocks=1, out_blocks=1,
            b_rows=None, out_rows=None, plan=None, tm=1024, tn=1408, tk=2816):
    m, k = (a.shape[1], a.shape[0]) if ta else a.shape
    if b_blocks > 1:
        rows_b, cols_b = b.shape[1], b.shape[2] * b_blocks
    else:
        rows_b, cols_b = b.shape
    if b_rows is not None:
        rows_b = b_rows[1]
    n, kb = (rows_b, cols_b) if tb else (cols_b, rows_b)
    assert k == kb, (name, a.shape, b.shape)
    tm, tn, tk = _tile(m, tm), _tile(n, tn), _tile(k, tk)
    if b_blocks > 1:
        per = cols_b // b_blocks
        if tb:
            tk = _tile(per, tk)
        else:
            tn = _tile(per, tn)
    if out_blocks > 1:
        tn = _tile(n // out_blocks, tn)
    nk = k // tk
    grid = (m // tm, n // tn, nk)

    a_spec = pl.BlockSpec((tk, tm), lambda i, j, kk: (kk, i)) if ta else pl.BlockSpec((tm, tk), lambda i, j, kk: (i, kk))
    if b_blocks > 1:
        if tb:
            nb = per // tk
            b_spec = pl.BlockSpec((1, tn, tk), lambda i, j, kk: (kk // nb, j, kk % nb))
        else:
            nb = per // tn
            b_spec = pl.BlockSpec((1, tk, tn), lambda i, j, kk: (j // nb, kk, j % nb))
    elif b_rows is not None:
        first = b_rows[0]
        if tb:
            b_spec = pl.BlockSpec((pl.Element(tn), pl.Element(tk)),
                                  lambda i, j, kk: (pl.multiple_of(first + j * tn, LANES), kk * tk))
        else:
            b_spec = pl.BlockSpec((pl.Element(tk), pl.Element(tn)),
                                  lambda i, j, kk: (pl.multiple_of(first + kk * tk, LANES), j * tn))
    else:
        b_spec = pl.BlockSpec((tn, tk), lambda i, j, kk: (j, kk)) if tb else pl.BlockSpec((tk, tn), lambda i, j, kk: (kk, j))
    in_specs = [a_spec, b_spec]
    operands = [a, b]
    if bias is not None:
        in_specs.append(pl.BlockSpec((1, tn), lambda i, j, kk: (0, j)))
        operands.append(bias)
    if add is not None:
        in_specs.append(pl.BlockSpec((tm, tn), lambda i, j, kk: (i, j)))
        operands.append(add)
    aliases = {}
    if out_blocks > 1:
        nbo = (n // out_blocks) // tn
        out_spec = pl.BlockSpec((1, tm, tn), lambda i, j, kk: (j // nbo, i, j % nbo))
        out_shape = jax.ShapeDtypeStruct((out_blocks, m, n // out_blocks), out_dtype)
    elif out_rows is not None:
        first_out, total, previous = out_rows
        out_spec = pl.BlockSpec((pl.Element(tm), pl.Element(tn)),
                                lambda i, j, kk: (pl.multiple_of(first_out + i * tm, LANES), j * tn))
        out_shape = jax.ShapeDtypeStruct((total, n), out_dtype)
        if previous is not None:
            aliases = {len(operands): 0}
            in_specs.append(pl.BlockSpec(memory_space=pl.ANY))
            operands.append(previous)
    else:
        out_spec = pl.BlockSpec((tm, tn), lambda i, j, kk: (i, j))
        out_shape = jax.ShapeDtypeStruct((m, n), out_dtype)
    dims = (((0 if ta else 1,), (1 if tb else 0,)), ((), ()))
    has_bias, has_add, has_previous = bias is not None, add is not None, bool(aliases)
    p_ops, p_in_specs, p_out_shapes, p_out_specs, p_scratch, _ = _plan_call_parts(plan)
    n_pi, n_po = len(p_ops), len(p_out_shapes)

    def kern(*refs):
        a_ref, b_ref = refs[0], refs[1]
        pos = 2
        bias_ref = add_ref = None
        if has_bias:
            bias_ref = refs[pos]
            pos += 1
        if has_add:
            add_ref = refs[pos]
            pos += 1
        if has_previous:
            pos += 1
        plan_ins = refs[pos:pos + n_pi]
        o_ref = refs[pos + n_pi]
        plan_outs = refs[pos + n_pi + 1:pos + n_pi + 1 + n_po]
        pos += n_pi + 1 + n_po
        acc_ref = refs[pos] if nk > 1 else None
        plan_sems = refs[pos + (1 if nk > 1 else 0):]
        _plan_phase(plan, "start", grid, plan_ins, plan_outs, plan_sems)
        av = a_ref[...].astype(BF16)
        bv = (b_ref[0] if b_blocks > 1 else b_ref[...]).astype(BF16)
        p = lax.dot_general(av, bv, dims, preferred_element_type=F32)

        def finish(acc):
            if has_bias:
                acc = acc + bias_ref[...]
            if has_add:
                acc = acc + add_ref[...]
            if out_blocks > 1:
                o_ref[0] = acc.astype(o_ref.dtype)
            else:
                o_ref[...] = acc.astype(o_ref.dtype)

        if nk == 1:
            finish(p)
        else:
            kk = pl.program_id(2)

            @pl.when(kk == 0)
            def _():
                acc_ref[...] = p

            @pl.when(kk > 0)
            def _():
                acc_ref[...] += p

            @pl.when(kk == nk - 1)
            def _():
                finish(acc_ref[...])
        _plan_phase(plan, "end", grid, plan_ins, plan_outs, plan_sems)

    res = pl.pallas_call(
        kern, name=name, grid=grid, in_specs=in_specs + p_in_specs, out_specs=[out_spec] + p_out_specs,
        out_shape=[out_shape] + p_out_shapes, input_output_aliases=aliases,
        scratch_shapes=([pltpu.VMEM((tm, tn), F32)] if nk > 1 else []) + p_scratch,
        compiler_params=_params(("parallel", "parallel", "arbitrary") if plan is None else ("arbitrary",) * 3),
    )(*operands, *p_ops)
    return res[0] if plan is None else (res[0], list(res[1:]))


CONV_HALO = 16
CONV_ROWS = 256


def _taps(win, shifts, rows):
    n = win.shape[0]
    for j, s in enumerate(shifts):
        yield j, (pltpu.roll(win, (n - s) % n, axis=0) if s % n else win)[:rows]


def _dwconv_fwd(name, x, w, b, *, silu, out_dtype=F32, plan=None):
    s, c = x.shape
    k = w.shape[0]
    pad = (k - 1) // 2
    assert pad <= CONV_HALO and c % LANES == 0
    t = min(CONV_ROWS, s)
    n_chunks = s // t
    fwd_shifts = [CONV_HALO - pad + j for j in range(k)]
    p_ops, p_in_specs, p_out_shapes, p_out_specs, p_scratch, _ = _plan_call_parts(plan)
    n_pi, n_po = len(p_ops), len(p_out_shapes)
    n_steps = (c // LANES,)

    def kern(*refs):
        x_ref, w_ref, b_ref = refs[:3]
        plan_ins = refs[3:3 + n_pi]
        o_ref = refs[3 + n_pi]
        plan_outs = refs[4 + n_pi:4 + n_pi + n_po]
        xp_ref = refs[4 + n_pi + n_po]
        plan_sems = refs[5 + n_pi + n_po:]
        _plan_phase(plan, "start", n_steps, plan_ins, plan_outs, plan_sems)
        _plan_phase(plan, "middle", n_steps, plan_ins, plan_outs, plan_sems)
        zeros = jnp.zeros((CONV_HALO, LANES), F32)
        xp_ref[0:CONV_HALO, :] = zeros
        xp_ref[CONV_HALO + s:CONV_HALO + s + CONV_HALO, :] = zeros
        xp_ref[CONV_HALO:CONV_HALO + s, :] = x_ref[...].astype(F32)
        bv = b_ref[...]

        def chunk(i, carry):
            base = pl.multiple_of(i * t, 16)
            win = xp_ref[pl.ds(base, t + 2 * CONV_HALO), :]
            acc = jnp.zeros((t, LANES), F32)
            for j, xs in _taps(win, fwd_shifts, t):
                acc = acc + xs * w_ref[pl.ds(j, 1), :]
            acc = acc + bv
            o_ref[pl.ds(base, t), :] = (_silu(acc) if silu else acc).astype(o_ref.dtype)
            return carry

        lax.fori_loop(0, n_chunks, chunk, 0)
        _plan_phase(plan, "end", n_steps, plan_ins, plan_outs, plan_sems)

    res = pl.pallas_call(
        kern, name=name, grid=n_steps,
        in_specs=[pl.BlockSpec((s, LANES), lambda i: (0, i)), pl.BlockSpec((k, LANES), lambda i: (0, i)),
                  pl.BlockSpec((1, LANES), lambda i: (0, i))] + p_in_specs,
        out_specs=[pl.BlockSpec((s, LANES), lambda i: (0, i))] + p_out_specs,
        out_shape=[jax.ShapeDtypeStruct((s, c), out_dtype)] + p_out_shapes,
        scratch_shapes=[pltpu.VMEM((s + 2 * CONV_HALO, LANES), F32)] + p_scratch,
        compiler_params=_params(("parallel",) if plan is None else ("arbitrary",)),
    )(x, w, b, *p_ops)
    return res[0] if plan is None else (res[0], list(res[1:]))


def _dwconv_bwd(name, x, w, b, dout, *, silu, dx_dtype=F32):
    s, c = x.shape
    k = w.shape[0]
    pad = (k - 1) // 2
    t = min(CONV_ROWS, s)
    n_chunks = s // t
    fwd_shifts = [CONV_HALO - pad + j for j in range(k)]
    bwd_shifts = [CONV_HALO + pad - j for j in range(k)]

    def kern(x_ref, w_ref, b_ref, do_ref, dx_ref, dw_ref, db_ref, xp_ref, dp_ref):
        zeros = jnp.zeros((CONV_HALO, LANES), F32)
        for ref in (xp_ref, dp_ref):
            ref[0:CONV_HALO, :] = zeros
            ref[CONV_HALO + s:CONV_HALO + s + CONV_HALO, :] = zeros
        xp_ref[CONV_HALO:CONV_HALO + s, :] = x_ref[...].astype(F32)
        bv = b_ref[...]
        dw_ref[...] = jnp.zeros_like(dw_ref)

        def pre_chunk(i, dbias):
            base = pl.multiple_of(i * t, 16)
            win = xp_ref[pl.ds(base, t + 2 * CONV_HALO), :]
            dpre = do_ref[pl.ds(base, t), :].astype(F32)
            if silu:
                acc = jnp.zeros((t, LANES), F32)
                for j, xs in _taps(win, fwd_shifts, t):
                    acc = acc + xs * w_ref[pl.ds(j, 1), :]
                dpre = dpre * _dsilu(acc + bv)
            dp_ref[pl.ds(base + CONV_HALO, t), :] = dpre
            for j, xs in _taps(win, fwd_shifts, t):
                dw_ref[pl.ds(j, 1), :] += _sum0(dpre * xs)
            return dbias + _sum0(dpre)

        db_ref[...] = lax.fori_loop(0, n_chunks, pre_chunk, jnp.zeros((1, LANES), F32))

        def dx_chunk(i, carry):
            base = pl.multiple_of(i * t, 16)
            win = dp_ref[pl.ds(base, t + 2 * CONV_HALO), :]
            acc = jnp.zeros((t, LANES), F32)
            for j, dps in _taps(win, bwd_shifts, t):
                acc = acc + dps * w_ref[pl.ds(j, 1), :]
            dx_ref[pl.ds(base, t), :] = acc.astype(dx_ref.dtype)
            return carry

        lax.fori_loop(0, n_chunks, dx_chunk, 0)

    col = lambda rows: pl.BlockSpec((rows, LANES), lambda i: (0, i))
    return pl.pallas_call(
        kern, name=name, grid=(c // LANES,),
        in_specs=[col(s), col(k), col(1), col(s)],
        out_specs=[col(s), col(k), col(1)],
        out_shape=[jax.ShapeDtypeStruct((s, c), dx_dtype), jax.ShapeDtypeStruct((k, c), F32),
                   jax.ShapeDtypeStruct((1, c), F32)],
        scratch_shapes=[pltpu.VMEM((s + 2 * CONV_HALO, LANES), F32), pltpu.VMEM((s + 2 * CONV_HALO, LANES), F32)],
        compiler_params=_params(("parallel",)),
    )(x, w, b, dout)


_NT =(((1,), (1,)), ((), ()))
_TN = (((0,), (0,)), ((), ()))


def _dot(a, b, dims=None):
    if dims is None:
        return jnp.dot(a, b, preferred_element_type=F32)
    return lax.dot_general(a, b, dims, preferred_element_type=F32)


HEAD_QUANTITIES = 4
GROUPS_PER_STEP = 8


def _scan_tables(n_heads):
    j_heads = n_heads // N_GROUPS
    used = 3 * HEAD_QUANTITIES * j_heads
    assert used <= LANES and j_heads % 2 == 0 and N_GROUPS % GROUPS_PER_STEP == 0
    gw = j_heads * HEAD_DIM
    r = jnp.arange(LANES)[:, None]

    def expand(quantity, width):
        head_of_lane = jnp.arange(j_heads * width)[None] // width
        return ((r // (3 * j_heads) == quantity) & (r % j_heads == head_of_lane) & (r < used)).astype(BF16)

    sel_cols = (jnp.arange(gw)[:, None] // HEAD_DIM == jnp.arange(LANES)[None]).astype(BF16)
    h2 = 2 * n_heads
    rows = jnp.arange(3 * HEAD_QUANTITIES * h2)
    head = rows % n_heads
    col = ((rows % h2) // n_heads * N_GROUPS + head // j_heads) * LANES + (rows // h2) * j_heads + head % j_heads
    route = (col[:, None] == jnp.arange(2 * N_GROUPS * LANES)[None]).astype(BF16)
    return {"ex_dt": expand(0, HEAD_DIM), "ex_gam": expand(1, CHUNK), "ex_din": expand(2, HEAD_DIM),
            "ex_dst": expand(3, HEAD_DIM), "sel_cols": sel_cols, "route": route}


def _chunk_decay(etot, n_heads, direction):
    j_heads = n_heads // N_GROUPS
    ed = etot[:, :, direction * n_heads:(direction + 1) * n_heads]
    per_group = jnp.pad(ed.reshape(ed.shape[0], 8, N_GROUPS, j_heads), ((0, 0), (0, 0), (0, 0), (0, LANES - j_heads)))
    return jnp.repeat(ed, HEAD_DIM, axis=2), per_group.reshape(ed.shape[0], 8, N_GROUPS * LANES)


def _scan_specs(reverse_order, direction, nc, j_heads, d_ssm):
    gps = GROUPS_PER_STEP
    gw = j_heads * HEAD_DIM
    b_off = d_ssm // (gps * D_STATE)
    c_off = b_off + N_GROUPS // gps
    d_off = direction * (N_GROUPS // gps)
    zz = (lambda z: nc - 1 - z) if reverse_order else (lambda z: z)
    const = lambda shape: pl.BlockSpec(shape, lambda g, z: (0,) * len(shape))
    return {
        "xs": pl.BlockSpec((CHUNK, gps * gw), lambda g, z: (zz(z), g)),
        "b": pl.BlockSpec((CHUNK, gps * D_STATE), lambda g, z: (zz(z), b_off + g)),
        "c": pl.BlockSpec((CHUNK, gps * D_STATE), lambda g, z: (zz(z), c_off + g)),
        "q": pl.BlockSpec((CHUNK, gps * LANES), lambda g, z: (zz(z), d_off + g)),
        "gam_t": pl.BlockSpec((gps * j_heads, CHUNK), lambda g, z: (d_off + g, zz(z))),
        "etot_x": pl.BlockSpec((1, 8, gps * gw), lambda g, z: (zz(z), 0, g)),
        "etot_g": pl.BlockSpec((1, 8, gps * LANES), lambda g, z: (zz(z), 0, g)),
        "state": pl.BlockSpec((gps, 1, D_STATE, gw), lambda g, z: (g, zz(z), 0, 0)),
        "grp": pl.BlockSpec((CHUNK, gps * D_STATE), lambda g, z: (zz(z), g)),
        "ex": const((LANES, gw)), "ex_gam": const((LANES, j_heads * CHUNK)), "sel": const((gw, LANES)),
    }


def _scan_masks(reverse):
    li = lax.broadcasted_iota(jnp.int32, (CHUNK, CHUNK), 0)
    si = lax.broadcasted_iota(jnp.int32, (CHUNK, CHUNK), 1)
    mask = (li <= si) if reverse else (li >= si)
    mask_t = (si <= li) if reverse else (si >= li)
    return li, si, mask, mask_t, si < HEAD_DIM


def _plan_phase(plan, phase, n_steps, ins, outs, sems):
    if plan is None or phase not in plan.phases:
        return
    ids = [pl.program_id(i) for i in range(len(n_steps))]
    first = {"start": 0, "middle": n_steps[0] // 2, "end": n_steps[0] - 1}[phase]
    when = ids[0] == first
    for i, n in zip(ids[1:], n_steps[1:]):
        when = when & (i == (n - 1 if phase == "end" else 0))

    @pl.when(when)
    def _():
        plan.phases[phase](ins, outs, *sems)


def _plan_call_parts(plan):
    if plan is None:
        return [], [], [], [], [], ("parallel", "arbitrary")
    n_in, n_out = len(plan.operands), len(plan.out_shapes)
    return plan.operands, [ANY] * n_in, plan.out_shapes, [ANY] * n_out, plan.sems(), ("arbitrary", "arbitrary")


def _scan_fwd(name, xbc_c, q_all, gam_t, etot_x, tb, *, direction, d_ssm, plan=None):
    s = xbc_c.shape[0]
    nc = s // CHUNK
    j_heads = tb["ex_dt"].shape[1] // HEAD_DIM
    gw = j_heads * HEAD_DIM
    gps = GROUPS_PER_STEP
    reverse = direction == 1
    sp = _scan_specs(reverse, direction, nc, j_heads, d_ssm)
    p_ops, p_in_specs, p_out_shapes, p_out_specs, p_scratch, semantics = _plan_call_parts(plan)
    n_pi, n_po = len(p_ops), len(p_out_shapes)
    n_steps = (N_GROUPS // gps, nc)

    def kern(*refs):
        xs_ref, b_ref, c_ref, q_ref, gamt_ref, etx_ref, exdt_ref, exgam_ref, exdin_ref, exdst_ref = refs[:10]
        plan_ins = refs[10:10 + n_pi]
        y_ref, hs_ref = refs[10 + n_pi:12 + n_pi]
        plan_outs = refs[12 + n_pi:12 + n_pi + n_po]
        h_ref = refs[12 + n_pi + n_po]
        plan_sems = refs[13 + n_pi + n_po:]
        _plan_phase(plan, "start", n_steps, plan_ins, plan_outs, plan_sems)
        _plan_phase(plan, "middle", n_steps, plan_ins, plan_outs, plan_sems)

        @pl.when(pl.program_id(1) == 0)
        def _():
            h_ref[...] = jnp.zeros_like(h_ref)

        _, _, mask, _, lo = _scan_masks(reverse)
        for gi in range(gps):
            bb = b_ref[:, gi * D_STATE:(gi + 1) * D_STATE].astype(BF16)
            cb = c_ref[:, gi * D_STATE:(gi + 1) * D_STATE].astype(BF16)
            cbt = _dot(cb, bb, _NT)
            q = q_ref[:, gi * LANES:(gi + 1) * LANES]
            dtx, dinx, dstx = _dot(q, exdt_ref[...]), _dot(q, exdin_ref[...]), _dot(q, exdst_ref[...])
            gcol = _dot(q, exgam_ref[...])
            xdt = xs_ref[:, gi * gw:(gi + 1) * gw].astype(F32) * dtx
            ht = h_ref[gi]
            y_off = _dot(cb, ht.astype(BF16)) * dinx
            hs_ref[gi, 0] = ht
            for p in range(j_heads // 2):
                lanes = slice(p * CHUNK, (p + 1) * CHUNK)
                x2 = xdt[:, lanes]
                acc = y_off[:, lanes]
                for idx, j in enumerate((2 * p, 2 * p + 1)):
                    g_row = gamt_ref[pl.ds(gi * j_heads + j, 1), :]
                    decay = jnp.exp(jnp.where(mask, gcol[:, j * CHUNK:(j + 1) * CHUNK] - g_row, NEG))
                    x_head = jnp.where(lo if idx == 0 else jnp.logical_not(lo), x2, 0.0).astype(BF16)
                    acc = acc + _dot((cbt * decay).astype(BF16), x_head)
                y_ref[:, gi * gw + p * CHUNK:gi * gw + (p + 1) * CHUNK] = acc.astype(y_ref.dtype)
            h_ref[gi] = ht * etx_ref[0, 0:1, gi * gw:(gi + 1) * gw] + _dot(bb, (xdt * dstx).astype(BF16), _TN)
        _plan_phase(plan, "end", n_steps, plan_ins, plan_outs, plan_sems)

    res = pl.pallas_call(
        kern, name=name, grid=n_steps,
        in_specs=[sp["xs"], sp["b"], sp["c"], sp["q"], sp["gam_t"], sp["etot_x"], sp["ex"], sp["ex_gam"], sp["ex"],
                  sp["ex"]] + p_in_specs,
        out_specs=[sp["xs"], sp["state"]] + p_out_specs,
        out_shape=[jax.ShapeDtypeStruct((s, d_ssm), BF16),
                   jax.ShapeDtypeStruct((N_GROUPS, nc, D_STATE, gw), F32)] + p_out_shapes,
        scratch_shapes=[pltpu.VMEM((gps, D_STATE, gw), F32)] + p_scratch,
        compiler_params=_params(semantics),
    )(xbc_c, xbc_c, xbc_c, q_all, gam_t, etot_x, tb["ex_dt"], tb["ex_gam"], tb["ex_din"], tb["ex_dst"], *p_ops)
    return res[0], res[1], list(res[2:])


def _scan_bwd(name, xbc_c, dy, hs, q_all, gam_t, etot_x, etot_g, tb, *, direction, d_ssm, plan=None):
    s = xbc_c.shape[0]
    nc = s // CHUNK
    j_heads = tb["ex_dt"].shape[1] // HEAD_DIM
    gw = j_heads * HEAD_DIM
    gps = GROUPS_PER_STEP
    reverse = direction == 1
    sp = _scan_specs(not reverse, direction, nc, j_heads, d_ssm)
    hp = lax.Precision.HIGHEST
    p_ops, p_in_specs, p_out_shapes, p_out_specs, p_scratch, semantics = _plan_call_parts(plan)
    n_pi, n_po = len(p_ops), len(p_out_shapes)
    n_steps = (N_GROUPS // gps, nc)

    def kern(*refs):
        (xs_ref, b_ref, c_ref, dy_ref, hs_ref, q_ref, gamt_ref, etg_ref, etx_ref, exdt_ref, exgam_ref, exdin_ref,
         exdst_ref, sel_ref) = refs[:14]
        plan_ins = refs[14:14 + n_pi]
        dxs_ref, db_ref, dc_ref, ddt_ref, da_ref = refs[14 + n_pi:19 + n_pi]
        plan_outs = refs[19 + n_pi:19 + n_pi + n_po]
        dh_ref, tmp_ref = refs[19 + n_pi + n_po:21 + n_pi + n_po]
        plan_sems = refs[21 + n_pi + n_po:]
        _plan_phase(plan, "start", n_steps, plan_ins, plan_outs, plan_sems)
        _plan_phase(plan, "middle", n_steps, plan_ins, plan_outs, plan_sems)

        @pl.when(pl.program_id(1) == 0)
        def _():
            dh_ref[...] = jnp.zeros_like(dh_ref)

        li, si, mask, mask_t, lo = _scan_masks(reverse)
        sel = sel_ref[...]
        incl = ((si <= li) if reverse else (si >= li)).astype(F32)
        excl = ((si > li) if reverse else (si < li)).astype(F32)
        for gi in range(gps):
            grp_lanes = slice(gi * D_STATE, (gi + 1) * D_STATE)
            bb = b_ref[:, grp_lanes].astype(BF16)
            cb = c_ref[:, grp_lanes].astype(BF16)
            cbt = _dot(cb, bb, _NT)
            cbt_t = _dot(bb, cb, _NT)
            q = q_ref[:, gi * LANES:(gi + 1) * LANES]
            dtx, dinx, dstx = _dot(q, exdt_ref[...]), _dot(q, exdin_ref[...]), _dot(q, exdst_ref[...])
            gcol = _dot(q, exgam_ref[...])
            x_all = xs_ref[:, gi * gw:(gi + 1) * gw].astype(F32)
            dy_all = dy_ref[:, gi * gw:(gi + 1) * gw].astype(F32)
            xdt = x_all * dtx
            xb = xdt.astype(BF16)
            ht = hs_ref[gi, 0]
            hb = ht.astype(BF16)
            dht = dh_ref[gi]
            dhb = dht.astype(BF16)
            y_off = _dot(cb, hb) * dinx
            dx_off = _dot(bb, dhb) * dstx
            dyd = (dy_all * dinx).astype(BF16)
            xd = (xdt * dstx).astype(BF16)
            dc_acc = _dot(dyd, hb, _NT)
            db_acc = _dot(xd, dhb, _NT)
            dh_ref[gi] = dht * etx_ref[0, 0:1, gi * gw:(gi + 1) * gw] + _dot(cb, dyd, _TN)
            q_cols = _dot((dy_all * y_off).astype(BF16), sel)
            c_cols = _dot((xdt * dx_off).astype(BF16), sel)
            through = _sum0(_dot((dht * ht).astype(BF16), sel))
            dcbt = jnp.zeros((CHUNK, CHUNK), F32)
            for p in range(j_heads // 2):
                lanes = slice(p * CHUNK, (p + 1) * CHUNK)
                out_lanes = slice(gi * gw + p * CHUNK, gi * gw + (p + 1) * CHUNK)
                x2b = xb[:, lanes]
                dy2 = dy_all[:, lanes]
                acc = dx_off[:, lanes]
                for idx, j in enumerate((2 * p, 2 * p + 1)):
                    gc = gcol[:, j * CHUNK:(j + 1) * CHUNK]
                    gr = gamt_ref[pl.ds(gi * j_heads + j, 1), :]
                    decay = jnp.exp(jnp.where(mask, gc - gr, NEG))
                    decay_t = jnp.exp(jnp.where(mask_t, gr - gc, NEG))
                    dy_head = jnp.where(lo if idx == 0 else jnp.logical_not(lo), dy2, 0.0).astype(BF16)
                    acc = acc + _dot((cbt_t * decay_t).astype(BF16), dy_head)
                    dm = decay * _dot(dy_head, x2b, _NT)
                    dcbt = dcbt + dm
                    e = (cbt * dm).astype(BF16)
                    in_lane_j = si == j
                    q_cols = (q_cols + jnp.where(in_lane_j, jnp.sum(e.astype(F32), axis=1, keepdims=True), 0.0)
                              - _dot(e, jnp.where(in_lane_j, 1.0, 0.0).astype(BF16), _TN))
                dxs_ref[:, out_lanes] = (acc * dtx[:, lanes]).astype(dxs_ref.dtype)
                tmp_ref[:, lanes] = acc * x_all[:, lanes]
            dcb = dcbt.astype(BF16)
            dc_ref[:, grp_lanes] = (dc_acc + _dot(dcb, bb)).astype(dc_ref.dtype)
            db_ref[:, grp_lanes] = (db_acc + _dot(dcb, cb, _TN)).astype(db_ref.dtype)
            ddt_ref[:, gi * LANES:(gi + 1) * LANES] = _dot(tmp_ref[...].astype(BF16), sel)
            da_ref[:, gi * LANES:(gi + 1) * LANES] = (
                jnp.dot(incl, q_cols, preferred_element_type=F32, precision=hp)
                + jnp.dot(excl, c_cols, preferred_element_type=F32, precision=hp)
                + through * etg_ref[0, 0:1, gi * LANES:(gi + 1) * LANES])
        _plan_phase(plan, "end", n_steps, plan_ins, plan_outs, plan_sems)

    gn = N_GROUPS * D_STATE
    res = pl.pallas_call(
        kern, name=name, grid=n_steps,
        in_specs=[sp["xs"], sp["b"], sp["c"], sp["xs"], sp["state"], sp["q"], sp["gam_t"], sp["etot_g"], sp["etot_x"],
                  sp["ex"], sp["ex_gam"], sp["ex"], sp["ex"], sp["sel"]] + p_in_specs,
        out_specs=[sp["xs"], sp["grp"], sp["grp"], sp["grp"], sp["grp"]] + p_out_specs,
        out_shape=[jax.ShapeDtypeStruct((s, d_ssm), BF16), jax.ShapeDtypeStruct((s, gn), BF16),
                   jax.ShapeDtypeStruct((s, gn), BF16), jax.ShapeDtypeStruct((s, gn), F32),
                   jax.ShapeDtypeStruct((s, gn), F32)] + p_out_shapes,
        scratch_shapes=[pltpu.VMEM((gps, D_STATE, gw), F32), pltpu.VMEM((CHUNK, gw), F32)] + p_scratch,
        compiler_params=_params(semantics),
    )(xbc_c, xbc_c, xbc_c, dy, hs, q_all, gam_t, etot_g, etot_x, tb["ex_dt"], tb["ex_gam"], tb["ex_din"], tb["ex_dst"],
      tb["sel_cols"], *p_ops)
    return res[:5], list(res[5:])


def _dt_prepare(dt_raw, dt_bias, a_neg, route):
    s, h2 = dt_raw.shape
    n_heads = h2 // 2
    qw = route.shape[1]

    def kern(raw_ref, bias_ref, a_ref, route_ref, dt_ref, q_ref, gamt_ref, etot_ref):
        dt = _softplus(raw_ref[...] + bias_ref[...])
        a = dt * a_ref[...]
        li = lax.broadcasted_iota(jnp.int32, (CHUNK, CHUNK), 0)
        si = lax.broadcasted_iota(jnp.int32, (CHUNK, CHUNK), 1)
        tri = (li >= si).astype(F32)
        cs = jnp.dot(tri, a, preferred_element_type=F32, precision=lax.Precision.HIGHEST)
        tot = _sum0(a)
        fwd = lax.broadcasted_iota(jnp.int32, (CHUNK, h2), 1) < n_heads
        gam = jnp.where(fwd, cs, a - cs)
        din = jnp.where(fwd, jnp.exp(cs), jnp.exp(tot + gam))
        dst = jnp.where(fwd, jnp.exp(tot - cs), jnp.exp(cs - a))
        pieces = []
        for v in (dt, gam, din, dst):
            hi = v.astype(BF16)
            rest = v - hi.astype(F32)
            mid = rest.astype(BF16)
            pieces += [hi, mid, (rest - mid.astype(F32)).astype(BF16)]
        q_ref[...] = _dot(jnp.concatenate(pieces, axis=1), route_ref[...]).astype(BF16)
        dt_ref[...] = dt
        gamt_ref[...] = gam.T
        etot_ref[...] = jnp.broadcast_to(jnp.exp(tot), (8, h2))

    nc = s // CHUNK
    rows = lambda w: pl.BlockSpec((CHUNK, w), lambda i: (i, 0))
    whole = lambda a: pl.BlockSpec(a.shape, lambda i: (0, 0))
    return pl.pallas_call(
        kern, name="dt_prepare", grid=(nc,),
        in_specs=[rows(h2), whole(dt_bias), whole(a_neg), whole(route)],
        out_specs=[rows(h2), rows(qw), pl.BlockSpec((h2, CHUNK), lambda i: (0, i)), pl.BlockSpec((8, h2), lambda i: (i, 0))],
        out_shape=[jax.ShapeDtypeStruct((s, h2), F32), jax.ShapeDtypeStruct((s, qw), BF16),
                   jax.ShapeDtypeStruct((h2, s), F32), jax.ShapeDtypeStruct((nc * 8, h2), F32)],
        compiler_params=_params(("parallel",)),
    )(dt_raw, dt_bias, a_neg, route)


def _dt_backward(da_dirs, ddt_dirs, dt, dt_raw, dt_bias, a_neg):
    h2 = dt.shape[1]
    n_heads = h2 // 2
    j_heads = n_heads // N_GROUPS
    lane = jnp.arange(N_GROUPS * LANES)[:, None]
    head = (lane // LANES) * j_heads + lane % LANES
    pick = [((lane % LANES < j_heads) & (head + direction * n_heads == jnp.arange(h2)[None])).astype(BF16)
            for direction in (0, 1)]

    def compact(wide_f, wide_r, pick_f, pick_r):
        total = 0.0
        for wide, sel in ((wide_f, pick_f), (wide_r, pick_r)):
            hi = wide.astype(BF16)
            total = total + _dot(hi, sel.astype(BF16)) + _dot((wide - hi.astype(F32)).astype(BF16), sel.astype(BF16))
        return total

    def body(da_f, da_r, ddt_f, ddt_r, dtv, raw, pick_f, pick_r, bias, a_head):
        dav = compact(da_f, da_r, pick_f, pick_r)
        ddtv = compact(ddt_f, ddt_r, pick_f, pick_r)
        draw = (ddtv + dav * a_head) * _sigmoid(raw + bias)
        return draw, _sum0(draw), _sum0(dav * dtv) * a_head
    return _rowwise("dt_backward", body, [*da_dirs, *ddt_dirs, dt, dt_raw], [*pick, dt_bias, a_neg], [(h2, BF16)], [h2, h2])


def _rms(x):
    r = lax.rsqrt(_mean1(x * x) + EPS)
    return x * r, r


def _rms_bwd(dy, y, r):
    return r * (dy - y * _mean1(dy * y))


def _norm_mod_fwd(name, x, g, sc, sh, plan=None):
    def body(xv, gv, scv, shv):
        y, _ = _rms(xv)
        return ((y * gv) * (1.0 + scv) + shv,)
    res = _rowwise(name, body, [x], [g, sc, sh], [(x.shape[1], BF16)], plan=plan)
    return res[0] if plan is None else (res[0], list(res[1:]))


def _norm_mod_bwd(name, x, dh, dpass, g, sc, plan=None):
    d = x.shape[1]

    def body(xv, dhv, dpv, gv, scv):
        y, r = _rms(xv)
        dn = dhv * (1.0 + scv)
        dx = _rms_bwd(dn * gv, y, r) + dpv
        return dx, _sum0(dn * y), _sum0(dhv * (y * gv)), _sum0(dhv)
    return _rowwise(name, body, [x, dh, dpass], [g, sc], [(d, F32)], [d, d, d], plan=plan)


def _gated_residual_fwd(name, x, m, gate, gp):
    def body(xv, mv, gatev, gpv):
        y, _ = _rms(mv)
        return (xv + gatev * (y * gpv),)
    return _rowwise(name, body, [x, m], [gate, gp], [(x.shape[1], F32)])[0]


def _gated_residual_bwd(name, m, dx1, gate, gp):
    d = m.shape[1]

    def body(mv, dv, gatev, gpv):
        y, r = _rms(mv)
        dn = dv * gatev
        return _rms_bwd(dn * gpv, y, r), _sum0(dv * (y * gpv)), _sum0(dn * y)
    return _rowwise(name, body, [m, dx1], [gate, gp], [(d, BF16)], [d, d])


def _final_residual_loss(x1, f, tgt, gate, gp):
    d = x1.shape[1]

    def body(xv, fv, tv, gatev, gpv):
        y, r = _rms(fv)
        n = y * gpv
        err = xv + gatev * n - tv
        dx2 = err * (1.0 / d)
        dn = dx2 * gatev
        sq = jnp.sum(_sum0(err * err), axis=1, keepdims=True)
        return dx2, _rms_bwd(dn * gpv, y, r), jnp.broadcast_to(sq, (1, LANES)), _sum0(dx2 * n), _sum0(dn * y)
    return _rowwise("final_residual_loss", body, [x1, f, tgt], [gate, gp], [(d, F32), (d, BF16)], [LANES, d, d])


def _swiglu_fwd(gu):
    f = gu.shape[1] // 2

    def body(v):
        return (_silu(v[:, :f]) * v[:, f:],)
    return _rowwise("swiglu_fwd", body, [gu], [], [(f, BF16)])[0]


def _swiglu_bwd(gu, dact):
    f = gu.shape[1] // 2

    def body(v, dv):
        gt, up = v[:, :f], v[:, f:]
        return (jnp.concatenate([dv * up * _dsilu(gt), dv * _silu(gt)], axis=1),)
    return _rowwise("swiglu_bwd", body, [gu, dact], [], [(2 * f, BF16)])[0]


def _glu_fwd(glu_in, b_glu):
    c = glu_in.shape[1] // 2

    def body(v, bv):
        t = v + bv
        return (t[:, :c] * _sigmoid(t[:, c:]),)
    return _rowwise("glu_fwd", body, [glu_in], [b_glu], [(c, F32)])[0]


def _glu_bwd(glu_in, b_glu, du0):
    c = glu_in.shape[1] // 2

    def body(v, dv, bv):
        t = v + bv
        a, s = t[:, :c], _sigmoid(t[:, c:])
        dg = jnp.concatenate([dv * s, dv * a * s * (1.0 - s)], axis=1)
        return dg, _sum0(dg)
    return _rowwise("glu_bwd", body, [glu_in, du0], [b_glu], [(2 * c, BF16)], [2 * c])


def _ln_parts(u1):
    xc = u1 - _mean1(u1)
    r = lax.rsqrt(_mean1(xc * xc) + EPS)
    return xc * r, r


def _ln_silu_fwd(u1, ln_g, ln_b):
    def body(v, gv, bv):
        yh, _ = _ln_parts(v)
        return (_silu(yh * gv + bv),)
    return _rowwise("ln_silu_fwd", body, [u1], [ln_g, ln_b], [(u1.shape[1], BF16)])[0]


def _ln_silu_bwd(u1, du, ln_g, ln_b):
    d = u1.shape[1]

    def body(v, dv, gv, bv):
        yh, r = _ln_parts(v)
        dl = dv * _dsilu(yh * gv + bv)
        dyh = dl * gv
        du1 = r * (dyh - _mean1(dyh) - yh * _mean1(dyh * yh))
        return du1, _sum0(dl * yh), _sum0(dl)
    return _rowwise("ln_silu_bwd", body, [u1, du], [ln_g, ln_b], [(d, F32)], [d, d])


def _gate_merge_fwd(y_a, y_b, gl, b_gate):
    d = y_a.shape[1]

    def body(ya, yb, glv, bv):
        s = _sigmoid(glv + bv)
        return (s[:, :d] * ya + s[:, d:] * yb,)
    return _rowwise("gate_merge_fwd", body, [y_a, y_b, gl], [b_gate], [(d, BF16)])[0]


def _gate_merge_bwd(dmixin, y_a, y_b, gl, b_gate):
    d = y_a.shape[1]

    def body(dv, ya, yb, glv, bv):
        s = _sigmoid(glv + bv)
        sa, sb = s[:, :d], s[:, d:]
        dya, dyb = dv * sa, dv * sb
        dgl = jnp.concatenate([dv * ya * sa * (1.0 - sa), dv * yb * sb * (1.0 - sb)], axis=1)
        return dya, dyb, dgl, _sum0(dgl), _sum0(dyb)
    return _rowwise("gate_merge_bwd", body, [dmixin, y_a, y_b, gl], [b_gate],
                    [(d, BF16), (d, BF16), (2 * d, BF16)], [2 * d, d])


def _group_slices(d_ssm):
    gw = d_ssm // N_GROUPS
    return [slice(g * gw, (g + 1) * gw) for g in range(N_GROUPS)]


def _gated_norm_fwd(y_f, y_b, xbc_c, z, d_skip_x, g_ssm):
    d_ssm = y_f.shape[1]

    def body(yf, yb, xs, zv, dsk, gv):
        y = yf + yb + dsk * xs
        v = y * _silu(zv)
        outs = []
        for sl in _group_slices(d_ssm):
            w, _ = _rms(v[:, sl])
            outs.append(w)
        return y, jnp.concatenate(outs, axis=1) * gv
    return _rowwise("gated_norm_fwd", body, [y_f, y_b, (xbc_c, d_ssm, 0), z], [d_skip_x, g_ssm],
                    [(d_ssm, BF16), (d_ssm, BF16)])


def _gated_norm_bwd(y, z, dyn, xbc_c, d_skip_x, g_ssm, plan=None):
    d_ssm = y.shape[1]

    def body(yv, zv, dv, xs, dsk, gv):
        sz = _silu(zv)
        v = yv * sz
        dw = dv * gv
        dvs, ws = [], []
        for sl in _group_slices(d_ssm):
            w, r = _rms(v[:, sl])
            ws.append(w)
            dvs.append(_rms_bwd(dw[:, sl], w, r))
        dvv = jnp.concatenate(dvs, axis=1)
        dy = dvv * sz
        return dy, dvv * yv * _dsilu(zv), _sum0(dv * jnp.concatenate(ws, axis=1)), _sum0(dy * xs)
    return _rowwise("gated_norm_bwd", body, [y, z, dyn, (xbc_c, d_ssm, 0)], [d_skip_x, g_ssm],
                    [(d_ssm, BF16), (d_ssm, BF16)], [d_ssm, d_ssm], plan=plan)


def _ssd_grad_merge(dxs_f, dxs_b, dy, db_f, db_b, dc_f, dc_b, d_skip_x):
    d_ssm = dy.shape[1]
    width = d_ssm + 2 * N_GROUPS * D_STATE

    def body(xf, xb, dv, bf, bb, cf, cbv, dsk):
        return (jnp.concatenate([xf + xb + dsk * dv, bf + bb, cf + cbv], axis=1),)
    return _rowwise("ssd_grad_merge", body, [dxs_f, dxs_b, dy, db_f, db_b, dc_f, dc_b], [d_skip_x], [(width, BF16)])[0]


def _adamw(name, w, g, m, v):
    c = w.shape[1]
    c1 = 1.0 - ADAM_B1 ** ADAM_STEP
    c2 = 1.0 - ADAM_B2 ** ADAM_STEP

    def body(wv, gv, mv, vv):
        mn = ADAM_B1 * mv + (1.0 - ADAM_B1) * gv
        vn = ADAM_B2 * vv + (1.0 - ADAM_B2) * (gv * gv)
        delta = -ADAM_LR * ((mn / c1) / (jnp.sqrt(vn / c2) + ADAM_EPS) + ADAM_WD * wv)
        return delta, mn, vn
    return _rowwise(name, body, [w, g, m, v], [], [(c, F32)] * 3)


def _local_step(x, tgt, mod, wts, sm, late=None):
    s, d = x.shape
    d_ssm = 2 * d
    n_heads = d_ssm // HEAD_DIM
    d_xbc = d_ssm + 2 * N_GROUPS * D_STATE
    sec = [0, d_ssm, d_ssm + d_xbc, d_ssm + d_xbc + 2 * n_heads, d_ssm + d_xbc + 2 * n_heads + 2 * d]
    sec.append(sec[-1] + 2 * d)
    sh1, sc1, g1, sh2, sc2, g2 = [mod[:, i * d:(i + 1) * d] for i in range(N_MOD)]
    win_t = wts.get("w_in_t")
    sections = [(nm, (sec[i], sec[i + 1] - sec[i])) for i, nm in enumerate(("z", "xbc", "dt", "glu", "gate"))]

    if late is None:
        h1 = _norm_mod_fwd("pre_mix_norm", x, sm["g_pre_mix"], sc1, sh1)
    else:
        h1, (win_g,) = _norm_mod_fwd("pre_mix_norm", x, sm["g_pre_mix"], sc1, sh1, plan=_gather_plan([late["w_in_t"]]))
        win_t = win_g.reshape(-1, d)
    z, xbc, dt_raw, glu_in, gate_l = [
        _matmul(f"proj_{nm}", h1, win_t, tb=True, b_rows=rows, out_dtype=F32 if nm == "dt" else BF16)
        for nm, rows in sections]
    plan_c = plan_f = plan_r = None
    if late is not None:
        plan_c = _gather_plan([late[n] for n in MATRICES[1:4]])
        plan_f = _gather_plan([late["w_down"]])
        plan_r = _gather_plan([late["w_gate_up"]])
    xbc_c = _dwconv_fwd("ssm_conv_fwd", xbc, sm["w_conv_ssm"], sm["b_conv_ssm"], silu=True, out_dtype=BF16, plan=plan_c)
    if late is not None:
        xbc_c, got_c = xbc_c
    tables = _scan_tables(n_heads)
    dt, q_all, gam_t, etot = _dt_prepare(dt_raw, sm["dt_bias"], sm["a_neg"], tables["route"])
    etot = etot.reshape(s // CHUNK, 8, 2 * n_heads)
    (etx_f, etg_f), (etx_r, etg_r) = [_chunk_decay(etot, n_heads, direction) for direction in (0, 1)]
    y_f, hs_f, got_f = _scan_fwd("ssd_fwd_f", xbc_c, q_all, gam_t, etx_f, tables, direction=0, d_ssm=d_ssm, plan=plan_f)
    y_r, hs_r, got_r = _scan_fwd("ssd_fwd_r", xbc_c, q_all, gam_t, etx_r, tables, direction=1, d_ssm=d_ssm, plan=plan_r)
    if late is not None:
        wts = dict(wts, w_ssm_out=got_c[0].reshape(-1, d), w_conv_out=got_c[1].reshape(-1, d),
                   w_mix_out=got_c[2].reshape(-1, d), w_down=got_f[0].reshape(-1, d), w_gate_up=got_r[0])
    y_ssd, yn = _gated_norm_fwd(y_f, y_r, xbc_c, z, sm["d_skip_x"], sm["g_ssm_norm"])
    y_a = _matmul("ssm_out", yn, wts["w_ssm_out"])
    u0 = _glu_fwd(glu_in, sm["b_glu"])
    u1 = _dwconv_fwd("dw_conv_fwd", u0, sm["w_dw"], sm["b_dw"], silu=False)
    u = _ln_silu_fwd(u1, sm["ln_g"], sm["ln_b"])
    y_b = _matmul("conv_out", u, wts["w_conv_out"], bias=sm["b_conv_out"])
    mixin = _gate_merge_fwd(y_a, y_b, gate_l, sm["b_gate"])
    mix = _matmul("mix_out", mixin, wts["w_mix_out"])
    x1 = _gated_residual_fwd("post_mix_residual", x, mix, g1, sm["g_post_mix"])
    h2 = _norm_mod_fwd("pre_ffn_norm", x1, sm["g_pre_ffn"], sc2, sh2)
    gu = _matmul("ffn_gate_up", h2, wts["w_gate_up"], b_blocks=N_CHIPS, out_dtype=BF16)
    act = _swiglu_fwd(gu)
    f = _matmul("ffn_down", act, wts["w_down"])

    dx2, df, sq, d_g2, d_gpf = _final_residual_loss(x1, f, tgt, g2, sm["g_post_ffn"])
    dact = _matmul("d_act", df, wts["w_down"], tb=True, out_dtype=BF16)
    g_w_down = _matmul("g_w_down", act, df, ta=True, out_dtype=BF16)
    dgu = _swiglu_bwd(gu, dact)
    dh2 = _matmul("d_h2", dgu, wts["w_gate_up"], tb=True, b_blocks=N_CHIPS)
    g_w_gate_up = _matmul("g_w_gate_up", h2, dgu, ta=True, out_dtype=BF16, out_blocks=N_CHIPS)
    dx1, d_gpre_ffn, d_sc2, d_sh2 = _norm_mod_bwd("pre_ffn_norm_bwd", x1, dh2, dx2, sm["g_pre_ffn"], sc2)
    dmix, d_g1, d_gpm = _gated_residual_bwd("post_mix_residual_bwd", mix, dx1, g1, sm["g_post_mix"])
    dmixin = _matmul("d_mixin", dmix, wts["w_mix_out"], tb=True, out_dtype=BF16)
    g_w_mix = _matmul("g_w_mix_out", mixin, dmix, ta=True, out_dtype=BF16)
    dy_a, dy_b, dgate_l, d_bgate, d_bco = _gate_merge_bwd(dmixin, y_a, y_b, gate_l, sm["b_gate"])
    du = _matmul("d_u", dy_b, wts["w_conv_out"], tb=True, out_dtype=BF16)
    g_w_co = _matmul("g_w_conv_out", u, dy_b, ta=True, out_dtype=BF16)
    du1, d_lng, d_lnb = _ln_silu_bwd(u1, du, sm["ln_g"], sm["ln_b"])
    du0, d_wdw, d_bdw = _dwconv_bwd("dw_conv_bwd", u0, sm["w_dw"], sm["b_dw"], du1, silu=False)
    dglu, d_bglu = _glu_bwd(glu_in, sm["b_glu"], du0)
    dyn = _matmul("d_yn", dy_a, wts["w_ssm_out"], tb=True, out_dtype=BF16)
    g_w_ssm = _matmul("g_w_ssm_out", yn, dy_a, ta=True, out_dtype=BF16)
    early = _by_halves([g_w_ssm.reshape(N_CHIPS, -1, d), g_w_co.reshape(N_CHIPS, -1, d), g_w_mix.reshape(N_CHIPS, -1, d),
                        g_w_gate_up, g_w_down.reshape(N_CHIPS, -1, d)])
    dy_ssd, dz, d_gssm, d_dskip_x, *from_sibling = _gated_norm_bwd(
        y_ssd, z, dyn, xbc_c, sm["d_skip_x"], sm["g_ssm_norm"], plan=None if late is None else _sibling_plan(early))
    plan_b = sums = None
    if late is not None:
        sums = _chip_sums("early", early, from_sibling)
        plan_b = _send_chips_plan(sums)
    (dxs_f, db_f, dc_f, ddt_f, da_f), received = _scan_bwd(
        "ssd_bwd_f", xbc_c, dy_ssd, hs_f, q_all, gam_t, etx_f, etg_f, tables, direction=0, d_ssm=d_ssm, plan=plan_b)
    (dxs_r, db_r, dc_r, ddt_r, da_r), _ = _scan_bwd(
        "ssd_bwd_r", xbc_c, dy_ssd, hs_r, q_all, gam_t, etx_r, etg_r, tables, direction=1, d_ssm=d_ssm)
    ddt_raw, d_dtbias, d_alog = _dt_backward((da_f, da_r), (ddt_f, ddt_r), dt, dt_raw, sm["dt_bias"], sm["a_neg"])
    dxbc_c = _ssd_grad_merge(dxs_f, dxs_r, dy_ssd, db_f, db_r, dc_f, dc_r, sm["d_skip_x"])
    dxbc, d_wconv, d_bconv = _dwconv_bwd("ssm_conv_bwd", xbc, sm["w_conv_ssm"], sm["b_conv_ssm"], dxbc_c,
                                         silu=True, dx_dtype=BF16)
    dsecs = [dz, dxbc, ddt_raw, dglu, dgate_l]
    dh1 = g_win = None
    for (nm, rows), dsec in zip(sections, dsecs):
        g_win = _matmul(f"g_w_in_{nm}", dsec, h1, ta=True, out_dtype=BF16, out_rows=(rows[0], sec[-1], g_win))
    hosts = {} if late is None else {"z": "sibling", "xbc": 0, "glu": 1, "gate": 2}
    halves_in = _by_halves([g_win.reshape(N_CHIPS, -1, d)])
    sums_in, received_in = None, [None] * 3
    for (nm, rows), dsec in zip(sections, dsecs):
        host = hosts.get(nm)
        plan_in = None if host is None else _sibling_plan(halves_in) if host == "sibling" else _send_chips_plan(sums_in, (host,))
        dh1 = _matmul(f"d_h1_{nm}", dsec, win_t, b_rows=rows, add=dh1, plan=plan_in)
        if host == "sibling":
            dh1, from_sibling_in = dh1
            sums_in = _chip_sums("w_in", halves_in, from_sibling_in)
        elif host is not None:
            dh1, (received_in[host],) = dh1
    grad_x, d_gpre_mix, d_sc1, d_sh1 = _norm_mod_bwd("pre_mix_norm_bwd", x, dh1, dx1, sm["g_pre_mix"], sc1)

    dmod = jnp.concatenate([d_sh1, d_sc1, d_g1, d_sh2, d_sc2, d_g2], axis=1)
    if late is None:
        big = {"w_in_t": g_win, "w_ssm_out": g_w_ssm, "w_conv_out": g_w_co, "w_mix_out": g_w_mix,
               "w_gate_up": g_w_gate_up, "w_down": g_w_down}
    else:
        big = {"pending": (sums_in + sums, [received_in] + [[t, t, t] for t in received])}
    small = {"g_pre_mix": d_gpre_mix, "g_post_mix": d_gpm, "w_conv_ssm": d_wconv, "b_conv_ssm": d_bconv,
             "dt_bias": d_dtbias, "a_log": d_alog, "d_skip_x": d_dskip_x, "g_ssm_norm": d_gssm, "b_glu": d_bglu,
             "w_dw": d_wdw, "b_dw": d_bdw, "ln_g": d_lng, "ln_b": d_lnb, "b_conv_out": d_bco, "b_gate": d_bgate,
             "g_pre_ffn": d_gpre_ffn, "g_post_ffn": d_gpf}
    return sq, grad_x, big, small, dmod


ANY = pl.BlockSpec(memory_space=pl.ANY)
WHOLE_VMEM = pl.BlockSpec(memory_space=pltpu.VMEM)


def _mesh_place():
    x, y, c = lax.axis_index("x"), lax.axis_index("y"), lax.axis_index("c")
    other_chips = [(1 - x, y), (x, 1 - y), (1 - x, 1 - y)]
    return x, y, c, other_chips


def _remote(src, dst, send_sems, recv_sems, k, device):
    return pltpu.make_async_remote_copy(src_ref=src, dst_ref=dst, send_sem=send_sems.at[k], recv_sem=recv_sems.at[k],
                                        device_id=device, device_id_type=MESH)


def _gather_devices(name, block):
    m_per, n = block.shape

    def body(x_ref, out_ref, send_sems, recv_sems, local_sem):
        x, y, c, chips = _mesh_place()
        me, sibling = (x, y, c), (x, y, 1 - c)

        def rows(px, py, pc):
            return out_ref.at[pl.ds((4 * px + 2 * py + pc) * m_per, m_per), :]

        def copy(k, blk, to, src=None):
            return _remote(rows(*blk) if src is None else src, rows(*blk), send_sems, recv_sems, k, to)

        mine = pltpu.make_async_copy(x_ref, rows(*me), local_sem)
        mine.start()
        first = [copy(0, me, sibling, src=x_ref)]
        first += [copy(1 + j, me, (*chip, c), src=x_ref) for j, chip in enumerate(chips)]
        for cp in first:
            cp.start()
        passed = [copy(4 + j, (*chip, c), sibling) for j, chip in enumerate(chips)]
        for j, chip in enumerate(chips):
            copy(1 + j, (*chip, c), me).wait_recv()
            passed[j].start()
        copy(0, sibling, me).wait_recv()
        for j, chip in enumerate(chips):
            copy(4 + j, (*chip, 1 - c), me).wait_recv()
        for cp in first + passed:
            cp.wait_send()
        mine.wait()

    return pl.pallas_call(
        body, name=name, out_shape=jax.ShapeDtypeStruct((N_DEV * m_per, n), block.dtype),
        in_specs=[WHOLE_VMEM], out_specs=WHOLE_VMEM,
        scratch_shapes=[pltpu.SemaphoreType.DMA((7,)), pltpu.SemaphoreType.DMA((7,)), pltpu.SemaphoreType.DMA],
        compiler_params=pltpu.CompilerParams(vmem_limit_bytes=VMEM_LIMIT),
    )(block)


class _Plan:
    def __init__(self, operands, out_shapes, copies, phases):
        self.operands, self.out_shapes, self.copies, self.phases = list(operands), list(out_shapes), copies, phases

    def sems(self):
        return [pltpu.SemaphoreType.DMA((self.copies,)), pltpu.SemaphoreType.DMA((self.copies,))]


def _run_plan(name, plan):
    n_in, n_out = len(plan.operands), len(plan.out_shapes)

    def body(*refs):
        ins, outs = refs[:n_in], refs[n_in:n_in + n_out]
        send_sems, recv_sems = refs[n_in + n_out:]
        for phase in ("start", "middle", "end"):
            if phase in plan.phases:
                plan.phases[phase](ins, outs, send_sems, recv_sems)

    return list(pl.pallas_call(body, name=name, out_shape=plan.out_shapes, in_specs=[ANY] * n_in,
                               out_specs=[ANY] * n_out, scratch_shapes=plan.sems())(*plan.operands))


def _gather_plan(shards):
    n = len(shards)

    def copies(kinds, ins, outs, send_sems, recv_sems):
        x, y, c, chips = _mesh_place()
        me = 2 * x + y
        sibling = (x, y, 1 - c)

        def half(i, h):
            hr = ins[i].shape[0] // 2
            return pl.ds(h * hr, hr)

        def block(i, j, h):
            cx, cy = chips[j]
            return outs[i].at[2 * cx + cy, half(i, h)]

        make = {
            "over_ici": lambda i, j: _remote(ins[i].at[half(i, c)], outs[i].at[me, half(i, c)], send_sems, recv_sems,
                                             6 * i + j, (*chips[j], c)),
            "arrived": lambda i, j: _remote(block(i, j, c), block(i, j, c), send_sems, recv_sems, 6 * i + j, (*chips[j], c)),
            "passed_on": lambda i, j: _remote(block(i, j, c), block(i, j, c), send_sems, recv_sems, 6 * i + 3 + j, sibling),
            "from_sibling": lambda i, j: _remote(block(i, j, 1 - c), block(i, j, 1 - c), send_sems, recv_sems,
                                                 6 * i + 3 + j, sibling),
        }
        res = []
        for kind in kinds:
            if kind == "own":
                res.append([_remote(ins[i], outs[i].at[me], send_sems, recv_sems, 6 * n + i, sibling) for i in range(n)])
            else:
                res.append([make[kind](i, j) for i in range(n) for j in range(3)])
        return res

    def start(*refs):
        over_ici, own = copies(("over_ici", "own"), *refs)
        for cp in over_ici + own:
            cp.start()

    def end(*refs):
        arrived, passed_on = copies(("arrived", "passed_on"), *refs)
        for got, fwd in zip(arrived, passed_on):
            got.wait_recv()
            fwd.start()
        from_sibling, own_in = copies(("from_sibling", "own"), *refs)
        for cp in from_sibling + own_in:
            cp.wait_recv()
        over_ici, passed_on, own_out = copies(("over_ici", "passed_on", "own"), *refs)
        for cp in over_ici + passed_on + own_out:
            cp.wait_send()

    return _Plan(shards, [jax.ShapeDtypeStruct((N_CHIPS,) + s.shape, s.dtype) for s in shards], 7 * n,
                 {"start": start, "end": end})


def _sibling_plan(grads):
    n = len(grads)

    def copies(ins, outs, send_sems, recv_sems):
        x, y, c, _ = _mesh_place()
        return [_remote(ins[i].at[j, 1 - c], outs[i].at[j], send_sems, recv_sems, N_CHIPS * i + j, (x, y, 1 - c))
                for i in range(n) for j in range(N_CHIPS)]

    def start(*refs):
        for cp in copies(*refs):
            cp.start()

    def end(*refs):
        for cp in copies(*refs):
            cp.wait_recv()
        for cp in copies(*refs):
            cp.wait_send()

    return _Plan(grads, [jax.ShapeDtypeStruct((g.shape[0],) + g.shape[2:], g.dtype) for g in grads], N_CHIPS * n,
                 {"start": start, "end": end})


def _send_chips_plan(sums, neighbours=(0, 1, 2)):
    n = len(sums)

    def copies(ins, outs, send_sems, recv_sems):
        x, y, c, chips = _mesh_place()
        return [_remote(ins[i].at[2 * chips[j][0] + chips[j][1]], outs[i].at[j], send_sems, recv_sems, 3 * i + j,
                        (*chips[j], c))
                for i in range(n) for j in neighbours]

    def start(*refs):
        for cp in copies(*refs):
            cp.start()

    def end(*refs):
        for cp in copies(*refs):
            cp.wait_recv()
        for cp in copies(*refs):
            cp.wait_send()

    return _Plan(sums, [jax.ShapeDtypeStruct((3,) + g.shape[1:], g.dtype) for g in sums], 3 * n,
                 {"start": start, "end": end})


def _exchange_halves(name, shards):
    n = len(shards)

    def body(*refs):
        outs = refs[n:2 * n]
        send_sems, recv_sems = refs[2 * n:]
        x, y, c, _ = _mesh_place()
        sibling = (x, y, 1 - c)
        remote = [_remote(outs[i].at[c], outs[i].at[c], send_sems, recv_sems, i, sibling) for i in range(n)]
        for cp in remote:
            cp.start()
        for i in range(n):
            _remote(outs[i].at[1 - c], outs[i].at[1 - c], send_sems, recv_sems, i, sibling).wait_recv()
        for cp in remote:
            cp.wait_send()

    return pl.pallas_call(
        body, name=name,
        out_shape=[jax.ShapeDtypeStruct(h.shape, h.dtype) for h in shards],
        in_specs=[ANY] * n, out_specs=[ANY] * n, input_output_aliases={i: i for i in range(n)},
        scratch_shapes=[pltpu.SemaphoreType.DMA((n,)), pltpu.SemaphoreType.DMA((n,))],
    )(*shards)


def _divisor_tile(rows, row_bytes, quantum=16):
    best = rows
    for t in range(quantum, rows + 1, quantum):
        if rows % t == 0 and 2 * t * row_bytes <= ROW_TILE_BUDGET:
            best = t
    return best


def _add_sibling(name, g4, t1):
    nb, _, hr, cols = g4.shape
    t = _divisor_tile(hr, cols * 6)

    def kern(g_ref, t_ref, o_ref):
        o_ref[0] = (g_ref[0, 0].astype(F32) + t_ref[0].astype(F32)).astype(o_ref.dtype)

    return pl.pallas_call(
        kern, name=name, grid=(nb, hr // t),
        in_specs=[pl.BlockSpec((1, 1, t, cols), lambda j, i: (j, lax.axis_index("c"), i, 0)),
                  pl.BlockSpec((1, t, cols), lambda j, i: (j, i, 0))],
        out_specs=pl.BlockSpec((1, t, cols), lambda j, i: (j, i, 0)),
        out_shape=jax.ShapeDtypeStruct((nb, hr, cols), g4.dtype),
        compiler_params=_params(("parallel", "parallel")),
    )(g4, t1)


def _add_chips(name, s1, t3):
    _, hr, cols = s1.shape
    t = _divisor_tile(hr, cols * 12)

    def kern(s_ref, t0_ref, t1_ref, t2_ref, o_ref):
        acc = s_ref[0].astype(F32)
        for t_ref in (t0_ref, t1_ref, t2_ref):
            acc = acc + t_ref[0].astype(F32)
        o_ref[0] = acc

    slot = lambda j: pl.BlockSpec((1, t, cols), functools.partial(lambda i, j: (j, i, 0), j=j))
    return pl.pallas_call(
        kern, name=name, grid=(hr // t,),
        in_specs=[pl.BlockSpec((1, t, cols), lambda i: (2 * lax.axis_index("x") + lax.axis_index("y"), i, 0)),
                  slot(0), slot(1), slot(2)],
        out_specs=pl.BlockSpec((1, t, cols), lambda i: (lax.axis_index("c"), i, 0)),
        out_shape=jax.ShapeDtypeStruct((2, hr, cols), F32),
        compiler_params=_params(("parallel",)),
    )(s1, *t3)


def _by_halves(grads):
    return [g.reshape(N_CHIPS, 2, g.shape[1] // 2, g.shape[2]) for g in grads]


def _chip_sums(tag, halves, from_sibling):
    return [_add_sibling(f"chip_sum_{tag}_{i}", g, t) for i, (g, t) in enumerate(zip(halves, from_sibling))]


def _shard_sums(sums, received):
    halves = [_add_chips(f"shard_sum_{i}", s, t) for i, (s, t) in enumerate(zip(sums, received))]
    full = _exchange_halves("grad_halves_to_sibling", halves)
    return [f.reshape(f.shape[1] * 2, f.shape[2]) for f in full]


def _pack_rows(size, width):
    return -(-size // (8 * width)) * 8


def _pack(arrays, width):
    parts = []
    for a in arrays:
        flat = a.reshape(-1).astype(F32)
        rows = _pack_rows(flat.shape[0], width)
        parts.append(jnp.pad(flat, (0, rows * width - flat.shape[0])).reshape(rows, width))
    return jnp.concatenate(parts, axis=0)


def _unpack(block, shapes, width):
    out, r = [], 0
    for shp in shapes:
        size = 1
        for s_ in shp:
            size *= s_
        rows = _pack_rows(size, width)
        out.append(block[r:r + rows].reshape(-1)[:size].reshape(shp))
        r += rows
    return out


SMALL_PARAMS = ("b_ada", "g_pre_mix", "g_post_mix", "b_conv_ssm", "dt_bias_fwd", "dt_bias_bwd", "a_log_fwd", "a_log_bwd",
                "d_skip", "g_ssm_norm", "b_glu", "b_dw", "ln_g", "ln_b", "b_conv_out", "b_gate", "g_pre_ffn", "g_post_ffn")
SHARDED_SMALL = ("w_conv_ssm", "w_dw")
MATRICES = ("w_in", "w_ssm_out", "w_conv_out", "w_mix_out", "w_gate_up", "w_down")
ALL_PARAMS = ("w_ada", "b_ada", "g_pre_mix", "g_post_mix", "w_in", "w_conv_ssm", "b_conv_ssm", "dt_bias_fwd", "dt_bias_bwd",
              "a_log_fwd", "a_log_bwd", "d_skip", "g_ssm_norm", "w_ssm_out", "b_glu", "w_dw", "b_dw", "ln_g", "ln_b",
              "w_conv_out", "b_conv_out", "b_gate", "w_mix_out", "g_pre_ffn", "g_post_ffn", "w_gate_up", "w_down")
COND_ROWS = 48
COND_CONV_ROW = 8
COND_DW_ROW = 16
MOD_ROWS = 16


def kernel(x, c, w_ada, b_ada, g_pre_mix, g_post_mix, w_in, w_conv_ssm, b_conv_ssm, dt_bias_fwd, dt_bias_bwd, a_log_fwd, a_log_bwd, d_skip, g_ssm_norm, w_ssm_out, b_glu, w_dw, b_dw, ln_g, ln_b, w_conv_out, b_conv_out, b_gate, w_mix_out, g_pre_ffn, g_post_ffn, w_gate_up, w_down, loss_target, m_w_ada, m_b_ada, m_g_pre_mix, m_g_post_mix, m_w_in, m_w_conv_ssm, m_b_conv_ssm, m_dt_bias_fwd, m_dt_bias_bwd, m_a_log_fwd, m_a_log_bwd, m_d_skip, m_g_ssm_norm, m_w_ssm_out, m_b_glu, m_w_dw, m_b_dw, m_ln_g, m_ln_b, m_w_conv_out, m_b_conv_out, m_b_gate, m_w_mix_out, m_g_pre_ffn, m_g_post_ffn, m_w_gate_up, m_w_down, v_w_ada, v_b_ada, v_g_pre_mix, v_g_post_mix, v_w_in, v_w_conv_ssm, v_b_conv_ssm, v_dt_bias_fwd, v_dt_bias_bwd, v_a_log_fwd, v_a_log_bwd, v_d_skip, v_g_ssm_norm, v_w_ssm_out, v_b_glu, v_w_dw, v_b_dw, v_ln_g, v_ln_b, v_w_conv_out, v_b_conv_out, v_b_gate, v_w_mix_out, v_g_pre_ffn, v_g_post_ffn, v_w_gate_up, v_w_down):
    given = dict(locals())
    wgt = {n: given[n][0] for n in ALL_PARAMS}
    mom = {n: given["m_" + n][0] for n in ALL_PARAMS}
    var = {n: given["v_" + n][0] for n in ALL_PARAMS}
    xs, tgt = x[0], loss_target[0]
    s, d = xs.shape
    d_ssm = 2 * d
    n_heads = d_ssm // HEAD_DIM
    d_xbc = d_ssm + 2 * N_GROUPS * D_STATE
    xi, yi, ci = lax.axis_index("x"), lax.axis_index("y"), lax.axis_index("c")
    chip = 2 * xi + yi
    dev = 2 * chip + ci
    k_conv, k_dw = wgt["w_conv_ssm"].shape[0], wgt["w_dw"].shape[0]
    xbc_shard, dw_shard = d_xbc // N_CHIPS, d // N_CHIPS

    width1 = max(d, xbc_shard)
    blk = jnp.zeros((COND_ROWS, width1), F32)
    blk = blk.at[0, :d].set(c[0])
    blk = blk.at[COND_CONV_ROW:COND_CONV_ROW + k_conv, :xbc_shard].set(wgt["w_conv_ssm"])
    blk = blk.at[COND_DW_ROW:COND_DW_ROW + k_dw, :dw_shard].set(wgt["w_dw"])
    g1 = _gather_devices("gather_cond", blk).reshape(N_DEV, COND_ROWS, width1)
    c_all = g1[:, 0, :d]
    w_conv_full = jnp.concatenate([g1[2 * k, COND_CONV_ROW:COND_CONV_ROW + k_conv, :xbc_shard] for k in range(N_CHIPS)], axis=1)
    w_dw_full = jnp.concatenate([g1[2 * k, COND_DW_ROW:COND_DW_ROW + k_dw, :dw_shard] for k in range(N_CHIPS)], axis=1)
    c_act = jnp.pad(c_all * _sigmoid(c_all), ((0, MOD_ROWS - N_DEV), (0, 0)))

    mod_part = _matmul("ada_mod", c_act, wgt["w_ada"])
    g2 = _gather_devices("gather_mod", mod_part).reshape(N_DEV, MOD_ROWS, mod_part.shape[1])
    mod_all = jnp.concatenate([g2[2 * k, :N_DEV] for k in range(N_CHIPS)], axis=1) + wgt["b_ada"][None]
    mod = lax.dynamic_slice_in_dim(mod_all, dev, 1, axis=0)

    shards = [wgt["w_in"].T.astype(BF16)] + [wgt[n].astype(BF16) for n in MATRICES[1:]]
    wts = {}
    late = dict(zip(MATRICES[1:], shards[1:]), w_in_t=shards[0])
    row = lambda v: v.reshape(1, -1)
    sm = {"g_pre_mix": row(wgt["g_pre_mix"]), "g_post_mix": row(wgt["g_post_mix"]), "w_conv_ssm": w_conv_full,
          "b_conv_ssm": row(wgt["b_conv_ssm"]),
          "dt_bias": row(jnp.concatenate([wgt["dt_bias_fwd"], wgt["dt_bias_bwd"]])),
          "a_neg": row(-jnp.exp(jnp.concatenate([wgt["a_log_fwd"], wgt["a_log_bwd"]]))),
          "d_skip_x": row(jnp.repeat(wgt["d_skip"], HEAD_DIM)), "g_ssm_norm": row(wgt["g_ssm_norm"]),
          "b_glu": row(wgt["b_glu"]), "w_dw": w_dw_full, "b_dw": row(wgt["b_dw"]), "ln_g": row(wgt["ln_g"]),
          "ln_b": row(wgt["ln_b"]), "b_conv_out": row(wgt["b_conv_out"]), "b_gate": row(wgt["b_gate"]),
          "g_pre_ffn": row(wgt["g_pre_ffn"]), "g_post_ffn": row(wgt["g_post_ffn"])}

    sq, grad_x, big, small, dmod = _local_step(xs, tgt, mod, wts, sm, late=late)
    loss = lax.psum((0.5 / d) * sq[0, 0], ("x", "y", "c"))

    local_small = {"b_ada": dmod, "g_pre_mix": small["g_pre_mix"], "g_post_mix": small["g_post_mix"],
                   "b_conv_ssm": small["b_conv_ssm"], "dt_bias_fwd": small["dt_bias"][:, :n_heads],
                   "dt_bias_bwd": small["dt_bias"][:, n_heads:], "a_log_fwd": small["a_log"][:, :n_heads],
                   "a_log_bwd": small["a_log"][:, n_heads:],
                   "d_skip": jnp.sum(small["d_skip_x"].reshape(n_heads, HEAD_DIM), axis=1),
                   "g_ssm_norm": small["g_ssm_norm"], "b_glu": small["b_glu"], "b_dw": small["b_dw"],
                   "ln_g": small["ln_g"], "ln_b": small["ln_b"], "b_conv_out": small["b_conv_out"],
                   "b_gate": small["b_gate"], "g_pre_ffn": small["g_pre_ffn"], "g_post_ffn": small["g_post_ffn"],
                   "w_conv_ssm": small["w_conv_ssm"], "w_dw": small["w_dw"]}
    names = SMALL_PARAMS + SHARDED_SMALL
    pack = _pack([local_small[n] for n in names], d)
    rows_p = pack.shape[0]
    g3 = _gather_devices("gather_small_grads", pack).reshape(N_DEV, rows_p, d)
    total = _rowwise("sum_small_grads", lambda *blocks: (functools.reduce(lambda a, b: a + b, blocks),),
                     [g3[i] for i in range(N_DEV)], [], [(d, F32)])[0]
    full_shapes = [wgt[n].shape for n in SMALL_PARAMS] + [(k_conv, d_xbc), (k_dw, d)]
    summed = dict(zip(names, _unpack(total, full_shapes, d)))
    grads = {n: summed[n] for n in SMALL_PARAMS}
    grads["w_conv_ssm"] = lax.dynamic_slice_in_dim(summed["w_conv_ssm"], chip * xbc_shard, xbc_shard, axis=1)
    grads["w_dw"] = lax.dynamic_slice_in_dim(summed["w_dw"], chip * dw_shard, dw_shard, axis=1)

    dmod_all = g3[:, :N_MOD, :].reshape(N_DEV, N_MOD * d)
    ada_cols = wgt["w_ada"].shape[1]
    dmod_cols = jnp.pad(lax.dynamic_slice_in_dim(dmod_all, chip * ada_cols, ada_cols, axis=1),
                        ((0, MOD_ROWS - N_DEV), (0, 0)))
    grads["w_ada"] = _matmul("g_w_ada", c_act, dmod_cols, ta=True)

    reduced = _shard_sums(*big["pending"])
    grads["w_in"] = reduced[0].T
    for n, g in zip(MATRICES[1:], reduced[1:]):
        grads[n] = g

    delta, new_m, new_v = {}, {}, {}
    for n in ("w_ada",) + MATRICES:
        delta[n], new_m[n], new_v[n] = _adamw("adamw_" + n, wgt[n], grads[n], mom[n], var[n])
    for group, width, tag in ((SMALL_PARAMS, d, "small"), (SHARDED_SMALL, LANES, "conv")):
        shapes = [wgt[n].shape for n in group]
        packs = [_pack([src[n] for n in group], width) for src in (wgt, grads, mom, var)]
        outs = _adamw("adamw_" + tag, *packs)
        for res, o in zip((delta, new_m, new_v), outs):
            res.update(zip(group, _unpack(o, shapes, width)))

    lead = lambda a: a[None]
    return (loss, grad_x[None], *[lead(grads[n]) for n in ALL_PARAMS], *[lead(delta[n]) for n in ALL_PARAMS],
            *[lead(new_m[n]) for n in ALL_PARAMS], *[lead(new_v[n]) for n in ALL_PARAMS])
```
